```python
import jax, jax.numpy as jnp
from jax import lax
import numpy as np

D_MODEL = 1024
BATCH = 8
SEQ = 8192
DEPTH = 1

D_MIX = D_MODEL
ATTN_WIDTH = D_MIX // 2
N_HEADS = 8
HEAD_DIM = ATTN_WIDTH // N_HEADS
DILATED_PATTERNS = ((128, 1), (512, 4), (2048, 16))
BLOCK = 128
POOL_WIDTH = D_MIX - ATTN_WIDTH
POOL_WINDOWS = (2, 4, 8, 16)
N_POOL_GROUPS = len(POOL_WINDOWS)
POOL_GROUP_DIM = POOL_WIDTH // N_POOL_GROUPS
D_FF = 2816
CONV_WIDTH = 3
EPS = 1e-6
NEG_INF = -1e30

kernel_name = "hybrid_dilated_attn_pool_convffn_sandwich"


def rms_norm(x, g):
    x32 = x.astype(jnp.float32)
    y = x32 * lax.rsqrt(jnp.mean(x32 * x32, axis=-1, keepdims=True) + EPS)
    return (y * g.astype(jnp.float32)).astype(x.dtype)


def dilated_window_attention(q, k, v, window, dilation):
    B, H, S, hd = q.shape
    span = window // dilation
    L = S // dilation
    nb = -(-L // BLOCK)
    Lp = nb * BLOCK

    def to_res(a):
        return a.reshape(B, H, L, dilation, hd).transpose(0, 1, 3, 2, 4)

    lead = ((0, 0), (0, 0), (0, 0))
    qb = jnp.pad(to_res(q), lead + ((0, Lp - L), (0, 0))).reshape(B, H, dilation, nb, BLOCK, hd)

    def key_blocks(a):
        ap = jnp.pad(to_res(a), lead + ((BLOCK, Lp - L), (0, 0)))
        prev = ap[:, :, :, :Lp].reshape(B, H, dilation, nb, BLOCK, hd)
        cur = ap[:, :, :, BLOCK:].reshape(B, H, dilation, nb, BLOCK, hd)
        return jnp.concatenate([prev, cur], axis=4)

    kb = key_blocks(k)
    vb = key_blocks(v)
    scores = jnp.einsum('bhrnqd,bhrnkd->bhrnqk', qb.astype(jnp.float32),
                        kb.astype(jnp.float32)) * (hd ** -0.5)
    qi = jnp.arange(BLOCK)[:, None]
    ki = jnp.arange(2 * BLOCK)[None, :]
    blk = jnp.arange(nb)[:, None, None]
    dist = qi + BLOCK - ki
    key_pos = blk * BLOCK - BLOCK + ki
    mask = (dist >= 0) & (dist <= span) & (key_pos >= 0)
    scores = jnp.where(mask, scores, NEG_INF)
    m = jnp.max(scores, axis=-1, keepdims=True)
    p = jnp.exp(scores - m)
    denom = jnp.sum(p, axis=-1, keepdims=True)
    out = jnp.einsum('bhrnqk,bhrnkd->bhrnqd', p, vb.astype(jnp.float32)) / denom
    lse = (m + jnp.log(denom))[..., 0]
    out = out.reshape(B, H, dilation, Lp, hd)[:, :, :, :L]
    out = out.transpose(0, 1, 3, 2, 4).reshape(B, H, S, hd)
    lse = lse.reshape(B, H, dilation, Lp)[..., :L].transpose(0, 1, 3, 2).reshape(B, H, S)
    return out, lse


def dilated_mixture_attention(q, k, v):
    outs, lses = [], []
    for window, dilation in DILATED_PATTERNS:
        o, l = dilated_window_attention(q, k, v, window, dilation)
        outs.append(o)
        lses.append(l)
    w = jax.nn.softmax(jnp.stack(lses, axis=0), axis=0)
    return jnp.sum(w[..., None] * jnp.stack(outs, axis=0), axis=0)


def multiscale_pool_mixer(u, pool_w, pool_scale):
    B, S, _ = u.shape
    ug = u.astype(jnp.float32).reshape(B, S, N_POOL_GROUPS, POOL_GROUP_DIM)
    t = jnp.arange(S)
    outs = []
    for g, w in enumerate(POOL_WINDOWS):
        xg = ug[:, :, g]
        csum = jnp.cumsum(xg, axis=1)
        lagged = jnp.pad(csum, ((0, 0), (w, 0), (0, 0)))[:, :S]
        count = jnp.minimum(t + 1, w).astype(jnp.float32)[None, :, None]
        pooled = (csum - lagged) / count - xg
        outs.append(jnp.einsum('bsc,cd->bsd', pooled, pool_w[g].astype(jnp.float32)))
    y = jnp.concatenate(outs, axis=-1) * pool_scale.astype(jnp.float32)
    return y.astype(u.dtype)


def conv_gated_mlp(h, w_up, conv_w, conv_b, w_down):
    S = h.shape[1]
    u = jnp.einsum('bsd,df->bsf', h, w_up)
    up = jnp.pad(u, ((0, 0), (CONV_WIDTH - 1, 0), (0, 0)))
    c = conv_b + sum(up[:, j:j + S] * conv_w[j] for j in range(CONV_WIDTH))
    gate, val = jnp.split(c, 2, axis=-1)
    y = jax.nn.gelu(gate, approximate=True) * val
    return jnp.einsum('bsf,fd->bsd', y, w_down)


def _fwd_setup_inputs(seed: int = 0) -> dict:
    key = jax.random.key(seed)
    ks = jax.random.split(key, 16)
    f32 = jnp.float32

    def nrm(k, shape, scale):
        return jax.random.normal(k, shape, f32) * scale

    def gain(k, n):
        return 1.0 + 0.05 * jax.random.normal(k, (DEPTH, n), f32)

    n_in = 3 * ATTN_WIDTH + POOL_WIDTH
    return {
        "x": jax.random.normal(ks[0], (BATCH, SEQ, D_MODEL), f32),
        "g_mix_pre": gain(ks[1], D_MODEL),
        "w_in": nrm(ks[2], (DEPTH, D_MODEL, n_in), D_MODEL ** -0.5),
        "pool_w": nrm(ks[3], (DEPTH, N_POOL_GROUPS, POOL_GROUP_DIM, POOL_GROUP_DIM), POOL_GROUP_DIM ** -0.5),
        "pool_scale": 1.0 + 0.1 * jax.random.normal(ks[4], (DEPTH, POOL_WIDTH), f32),
        "w_out": nrm(ks[5], (DEPTH, D_MIX, D_MODEL), D_MIX ** -0.5),
        "g_mix_post": gain(ks[6], D_MODEL),
        "g_ffn_pre": gain(ks[7], D_MODEL),
        "w_up": nrm(ks[8], (DEPTH, D_MODEL, 2 * D_FF), D_MODEL ** -0.5),
        "conv_w": nrm(ks[9], (DEPTH, CONV_WIDTH, 2 * D_FF), CONV_WIDTH ** -0.5),
        "conv_b": nrm(ks[10], (DEPTH, 2 * D_FF), 0.01),
        "w_down": nrm(ks[11], (DEPTH, D_FF, D_MODEL), D_FF ** -0.5),
        "g_ffn_post": gain(ks[12], D_MODEL),
    }


def _fwd_reference(x, g_mix_pre, w_in, pool_w, pool_scale, w_out, g_mix_post,
              g_ffn_pre, w_up, conv_w, conv_b, w_down, g_ffn_post):
    B, S, _ = x.shape
    for layer in range(DEPTH):
        h = rms_norm(x, g_mix_pre[layer])
        proj = jnp.einsum('bsd,dn->bsn', h, w_in[layer])
        q, k, v, pool_in = jnp.split(
            proj, [ATTN_WIDTH, 2 * ATTN_WIDTH, 3 * ATTN_WIDTH], axis=-1)

        def heads(a):
            return a.reshape(B, S, N_HEADS, HEAD_DIM).transpose(0, 2, 1, 3)

        attn = dilated_mixture_attention(heads(q), heads(k), heads(v))
        attn = attn.transpose(0, 2, 1, 3).reshape(B, S, ATTN_WIDTH).astype(x.dtype)
        pool = multiscale_pool_mixer(pool_in, pool_w[layer], pool_scale[layer])
        mixed = jnp.einsum('bsm,md->bsd', jnp.concatenate([attn, pool], axis=-1), w_out[layer])
        x = x + rms_norm(mixed, g_mix_post[layer])
        h = rms_norm(x, g_ffn_pre[layer])
        f = conv_gated_mlp(h, w_up[layer], conv_w[layer], conv_b[layer], w_down[layer])
        x = x + rms_norm(f, g_ffn_post[layer])
    return x


import jax as _jax
import jax.numpy as _jnp

TWIN_FORMAT = 'train_step'
FWD_PARAMS = ['x', 'g_mix_pre', 'w_in', 'pool_w', 'pool_scale', 'w_out', 'g_mix_post', 'g_ffn_pre', 'w_up', 'conv_w', 'conv_b', 'w_down', 'g_ffn_post']
TWIN_WEIGHTS = ['g_mix_pre', 'w_in', 'pool_w', 'pool_scale', 'w_out', 'g_mix_post', 'g_ffn_pre', 'w_up', 'conv_w', 'conv_b', 'w_down', 'g_ffn_post']
TWIN_DIFF_INPUT = 'x'
TWIN_INPUTS = ['x', 'g_mix_pre', 'w_in', 'pool_w', 'pool_scale', 'w_out', 'g_mix_post', 'g_ffn_pre', 'w_up', 'conv_w', 'conv_b', 'w_down', 'g_ffn_post', 'loss_target', 'm_g_mix_pre', 'm_w_in', 'm_pool_w', 'm_pool_scale', 'm_w_out', 'm_g_mix_post', 'm_g_ffn_pre', 'm_w_up', 'm_conv_w', 'm_conv_b', 'm_w_down', 'm_g_ffn_post', 'v_g_mix_pre', 'v_w_in', 'v_pool_w', 'v_pool_scale', 'v_w_out', 'v_g_mix_post', 'v_g_ffn_pre', 'v_w_up', 'v_conv_w', 'v_conv_b', 'v_w_down', 'v_g_ffn_post']
TWIN_OUTPUTS = ['loss', 'grad_x', 'grad_g_mix_pre', 'grad_w_in', 'grad_pool_w', 'grad_pool_scale', 'grad_w_out', 'grad_g_mix_post', 'grad_g_ffn_pre', 'grad_w_up', 'grad_conv_w', 'grad_conv_b', 'grad_w_down', 'grad_g_ffn_post', 'delta_g_mix_pre', 'delta_w_in', 'delta_pool_w', 'delta_pool_scale', 'delta_w_out', 'delta_g_mix_post', 'delta_g_ffn_pre', 'delta_w_up', 'delta_conv_w', 'delta_conv_b', 'delta_w_down', 'delta_g_ffn_post', 'new_m_g_mix_pre', 'new_m_w_in', 'new_m_pool_w', 'new_m_pool_scale', 'new_m_w_out', 'new_m_g_mix_post', 'new_m_g_ffn_pre', 'new_m_w_up', 'new_m_conv_w', 'new_m_conv_b', 'new_m_w_down', 'new_m_g_ffn_post', 'new_v_g_mix_pre', 'new_v_w_in', 'new_v_pool_w', 'new_v_pool_scale', 'new_v_w_out', 'new_v_g_mix_post', 'new_v_g_ffn_pre', 'new_v_w_up', 'new_v_conv_w', 'new_v_conv_b', 'new_v_w_down', 'new_v_g_ffn_post']
TWIN_LEAF_KINDS = {'loss': 'loss', 'grad_x': 'grad_x', 'grad_g_mix_pre': 'grad_w', 'grad_w_in': 'grad_w', 'grad_pool_w': 'grad_w', 'grad_pool_scale': 'grad_w', 'grad_w_out': 'grad_w', 'grad_g_mix_post': 'grad_w', 'grad_g_ffn_pre': 'grad_w', 'grad_w_up': 'grad_w', 'grad_conv_w': 'grad_w', 'grad_conv_b': 'grad_w', 'grad_w_down': 'grad_w', 'grad_g_ffn_post': 'grad_w', 'delta_g_mix_pre': 'delta_w', 'delta_w_in': 'delta_w', 'delta_pool_w': 'delta_w', 'delta_pool_scale': 'delta_w', 'delta_w_out': 'delta_w', 'delta_g_mix_post': 'delta_w', 'delta_g_ffn_pre': 'delta_w', 'delta_w_up': 'delta_w', 'delta_conv_w': 'delta_w', 'delta_conv_b': 'delta_w', 'delta_w_down': 'delta_w', 'delta_g_ffn_post': 'delta_w', 'new_m_g_mix_pre': 'new_m', 'new_m_w_in': 'new_m', 'new_m_pool_w': 'new_m', 'new_m_pool_scale': 'new_m', 'new_m_w_out': 'new_m', 'new_m_g_mix_post': 'new_m', 'new_m_g_ffn_pre': 'new_m', 'new_m_w_up': 'new_m', 'new_m_conv_w': 'new_m', 'new_m_conv_b': 'new_m', 'new_m_w_down': 'new_m', 'new_m_g_ffn_post': 'new_m', 'new_v_g_mix_pre': 'new_v', 'new_v_w_in': 'new_v', 'new_v_pool_w': 'new_v', 'new_v_pool_scale': 'new_v', 'new_v_w_out': 'new_v', 'new_v_g_mix_post': 'new_v', 'new_v_g_ffn_pre': 'new_v', 'new_v_w_up': 'new_v', 'new_v_conv_w': 'new_v', 'new_v_conv_b': 'new_v', 'new_v_w_down': 'new_v', 'new_v_g_ffn_post': 'new_v'}


def _forward(args):
    return _fwd_reference(*[args[k] for k in FWD_PARAMS])


def _output_shape():
    def fwd():
        inp = _fwd_setup_inputs(0)
        return _fwd_reference(*[inp[k] for k in FWD_PARAMS])
    out = _jax.eval_shape(fwd)
    return out.shape, out.dtype

N_MICROBATCH = 1
ADAM_LR = 0.001
ADAM_B1 = 0.9
ADAM_B2 = 0.999
ADAM_EPS = 1e-08
ADAM_WD = 0.01
ADAM_STEP = 10
PER_EXAMPLE_BATCH_AXIS = {'x': 0, 'loss_target': 0}
SHARED_INPUTS = []
_WEIGHT_DTYPES = {'g_mix_pre': _jnp.float32, 'w_in': _jnp.float32, 'pool_w': _jnp.float32, 'pool_scale': _jnp.float32, 'w_out': _jnp.float32, 'g_mix_post': _jnp.float32, 'g_ffn_pre': _jnp.float32, 'w_up': _jnp.float32, 'conv_w': _jnp.float32, 'conv_b': _jnp.float32, 'w_down': _jnp.float32, 'g_ffn_post': _jnp.float32}
MOMENT_SCALE = {'g_mix_pre': 1.485939e+00, 'w_in': 9.483659e-01, 'pool_w': 2.178811e+00, 'pool_scale': 2.269643e+00, 'w_out': 1.617443e+00, 'g_mix_post': 6.427108e+01, 'g_ffn_pre': 9.296320e-01, 'w_up': 4.039906e-01, 'conv_w': 4.797460e-01, 'conv_b': 9.431179e-01, 'w_down': 8.939045e-01, 'g_ffn_post': 6.428721e+01}


def _to_microbatches(a, axis):
    t = _jnp.moveaxis(a, axis, 0)
    t = t.reshape((N_MICROBATCH, t.shape[0] // N_MICROBATCH) + t.shape[1:])
    return _jnp.moveaxis(t, 1, axis + 1)


def setup_inputs(seed: int = 0) -> dict:
    inp = _fwd_setup_inputs(seed)
    key = _jax.random.fold_in(_jax.random.key(seed), 7919)
    shape, _ = _output_shape()
    out = dict(inp)
    out["loss_target"] = _jax.random.normal(_jax.random.fold_in(key, 0), shape, _jnp.float32)
    for i, name in enumerate(TWIN_WEIGHTS):
        w = inp[name].astype(_jnp.float32)
        if MOMENT_SCALE is None:
            s = _jnp.sqrt(_jnp.mean(_jnp.square(w)) + 1e-30)
        else:
            s = MOMENT_SCALE[name]
        km, kv = _jax.random.split(_jax.random.fold_in(key, i + 1))
        out[name] = w
        out["m_" + name] = s * _jax.random.normal(km, w.shape, _jnp.float32)
        out["v_" + name] = (s * s) * _jax.random.uniform(kv, w.shape, _jnp.float32, 0.5, 1.5)
    if N_MICROBATCH > 1:
        for name, axis in PER_EXAMPLE_BATCH_AXIS.items():
            out[name] = _to_microbatches(out[name], axis)
    return {'x': out['x'], 'g_mix_pre': out['g_mix_pre'], 'w_in': out['w_in'], 'pool_w': out['pool_w'], 'pool_scale': out['pool_scale'], 'w_out': out['w_out'], 'g_mix_post': out['g_mix_post'], 'g_ffn_pre': out['g_ffn_pre'], 'w_up': out['w_up'], 'conv_w': out['conv_w'], 'conv_b': out['conv_b'], 'w_down': out['w_down'], 'g_ffn_post': out['g_ffn_post'], 'loss_target': out['loss_target'], 'm_g_mix_pre': out['m_g_mix_pre'], 'm_w_in': out['m_w_in'], 'm_pool_w': out['m_pool_w'], 'm_pool_scale': out['m_pool_scale'], 'm_w_out': out['m_w_out'], 'm_g_mix_post': out['m_g_mix_post'], 'm_g_ffn_pre': out['m_g_ffn_pre'], 'm_w_up': out['m_w_up'], 'm_conv_w': out['m_conv_w'], 'm_conv_b': out['m_conv_b'], 'm_w_down': out['m_w_down'], 'm_g_ffn_post': out['m_g_ffn_post'], 'v_g_mix_pre': out['v_g_mix_pre'], 'v_w_in': out['v_w_in'], 'v_pool_w': out['v_pool_w'], 'v_pool_scale': out['v_pool_scale'], 'v_w_out': out['v_w_out'], 'v_g_mix_post': out['v_g_mix_post'], 'v_g_ffn_pre': out['v_g_ffn_pre'], 'v_w_up': out['v_w_up'], 'v_conv_w': out['v_conv_w'], 'v_conv_b': out['v_conv_b'], 'v_w_down': out['v_w_down'], 'v_g_ffn_post': out['v_g_ffn_post']}


def _loss(weights, diff, rest, loss_target):
    with _jax.named_scope("forward"):
        args = {**rest, TWIN_DIFF_INPUT: diff, **{k: w.astype(_WEIGHT_DTYPES[k]) for k, w in weights.items()}}
        y = _forward(args)
    with _jax.named_scope("loss_head"):
        err = _jnp.square(y.astype(_jnp.float32) - loss_target)
        return 0.5 * _jnp.sum(_jnp.mean(err, axis=-1)) if err.ndim else 0.5 * err


def _adamw(w, g, m, v):
    m = ADAM_B1 * m + (1.0 - ADAM_B1) * g
    v = ADAM_B2 * v + (1.0 - ADAM_B2) * _jnp.square(g)
    m_hat = m / (1.0 - ADAM_B1 ** ADAM_STEP)
    v_hat = v / (1.0 - ADAM_B2 ** ADAM_STEP)
    delta = -ADAM_LR * (m_hat / (_jnp.sqrt(v_hat) + ADAM_EPS) + ADAM_WD * w)
    return delta, m, v


def reference(x, g_mix_pre, w_in, pool_w, pool_scale, w_out, g_mix_post, g_ffn_pre, w_up, conv_w, conv_b, w_down, g_ffn_post, loss_target, m_g_mix_pre, m_w_in, m_pool_w, m_pool_scale, m_w_out, m_g_mix_post, m_g_ffn_pre, m_w_up, m_conv_w, m_conv_b, m_w_down, m_g_ffn_post, v_g_mix_pre, v_w_in, v_pool_w, v_pool_scale, v_w_out, v_g_mix_post, v_g_ffn_pre, v_w_up, v_conv_w, v_conv_b, v_w_down, v_g_ffn_post):
    given = dict(x=x, g_mix_pre=g_mix_pre, w_in=w_in, pool_w=pool_w, pool_scale=pool_scale, w_out=w_out, g_mix_post=g_mix_post, g_ffn_pre=g_ffn_pre, w_up=w_up, conv_w=conv_w, conv_b=conv_b, w_down=w_down, g_ffn_post=g_ffn_post, loss_target=loss_target, m_g_mix_pre=m_g_mix_pre, m_w_in=m_w_in, m_pool_w=m_pool_w, m_pool_scale=m_pool_scale, m_w_out=m_w_out, m_g_mix_post=m_g_mix_post, m_g_ffn_pre=m_g_ffn_pre, m_w_up=m_w_up, m_conv_w=m_conv_w, m_conv_b=m_conv_b, m_w_down=m_w_down, m_g_ffn_post=m_g_ffn_post, v_g_mix_pre=v_g_mix_pre, v_w_in=v_w_in, v_pool_w=v_pool_w, v_pool_scale=v_pool_scale, v_w_out=v_w_out, v_g_mix_post=v_g_mix_post, v_g_ffn_pre=v_g_ffn_pre, v_w_up=v_w_up, v_conv_w=v_conv_w, v_conv_b=v_conv_b, v_w_down=v_w_down, v_g_ffn_post=v_g_ffn_post)
    weights = {n: given[n] for n in TWIN_WEIGHTS}
    shared = {n: given[n] for n in SHARED_INPUTS}
    per_example = {n: given[n] for n in ['x']}
    grad_fn = _jax.value_and_grad(_loss, argnums=(0, 1))

    def one_microbatch(ex, loss_target):
        ex = dict(ex)
        diff = ex.pop(TWIN_DIFF_INPUT)
        return grad_fn(weights, diff, {**shared, **ex}, loss_target)

    if N_MICROBATCH == 1:
        loss, (grad_w, grad_x) = one_microbatch(per_example, given["loss_target"])
    else:
        def body(carry, xs):
            loss_sum, grad_sum = carry
            l_k, (gw_k, gx_k) = one_microbatch(xs[0], xs[1])
            with _jax.named_scope("update"):
                return (loss_sum + l_k, _jax.tree.map(_jnp.add, grad_sum, gw_k)), gx_k

        init = (_jnp.zeros((), _jnp.float32), _jax.tree.map(_jnp.zeros_like, weights))
        (loss, grad_w), grad_x = _jax.lax.scan(body, init, (per_example, given["loss_target"]))
    with _jax.named_scope("update"):
        delta_w, new_m, new_v = {}, {}, {}
        for n in TWIN_WEIGHTS:
            delta_w[n], new_m[n], new_v[n] = _adamw(weights[n], grad_w[n], given["m_" + n], given["v_" + n])
    return (loss, grad_x, *[grad_w[n] for n in TWIN_WEIGHTS], *[delta_w[n] for n in TWIN_WEIGHTS],
            *[new_m[n] for n in TWIN_WEIGHTS], *[new_v[n] for n in TWIN_WEIGHTS])
```

```python
import functools

import jax
import jax.numpy as jnp
from jax import lax
from jax.experimental import pallas as pl
from jax.experimental.pallas import tpu as pltpu

F32 = jnp.float32
BF16 = jnp.bfloat16

RMS_EPS = 1e-6
NEG_INF = -1e30
N_HEADS = 8
HEAD_DIM = 64
ATTN_WIDTH = N_HEADS * HEAD_DIM
ATTN_SCALE = HEAD_DIM ** -0.5
ATTN_BLOCK = 128
DILATIONS = (1, 4, 16)
POOL_WINDOWS = (2, 4, 8, 16)
POOL_GROUP = 128
POOL_WIDTH = POOL_GROUP * len(POOL_WINDOWS)
POOL_HALO = 16
CONV_WIDTH = 3
CONV_HALO = 8
N_SHARD = 4

ADAM_LR = 0.001
ADAM_B1 = 0.9
ADAM_B2 = 0.999
ADAM_EPS = 1e-08
ADAM_WD = 0.01
ADAM_STEP = 10

VMEM_LIMIT = 60 * 1024 * 1024
MESH = pl.DeviceIdType.MESH
NT = (((1,), (1,)), ((), ()))
TN = (((0,), (0,)), ((), ()))


def _params(sem, vmem=None):
    return pltpu.CompilerParams(dimension_semantics=sem, vmem_limit_bytes=vmem)


def _const_spec(shape):
    zeros = (0,) * len(shape)
    return pl.BlockSpec(shape, lambda *_: zeros, pipeline_mode=pl.Buffered(1))


def _dot(a, b):
    return jnp.dot(a, b, preferred_element_type=F32)


def _dot_nt(a, b):
    return lax.dot_general(a, b, NT, preferred_element_type=F32)


def _dot_tn(a, b):
    return lax.dot_general(a, b, TN, preferred_element_type=F32)


def _rms_stats(x):
    r = lax.rsqrt(jnp.mean(x * x, axis=-1, keepdims=True) + RMS_EPS)
    return x * r, r


def _rms_bwd(dy, n, r, g):
    dg = jnp.sum(dy * n, axis=0, keepdims=True)
    dn = dy * g
    dx = r * (dn - n * jnp.mean(dn * n, axis=-1, keepdims=True))
    return dx, dg


def _gelu_tanh(g):
    k = 0.7978845608028654
    a = k * (g + 0.044715 * (g * g * g))
    t = jnp.tanh(a)
    gelu = 0.5 * g * (1.0 + t)
    dgelu = 0.5 * (1.0 + t) + 0.5 * g * (1.0 - t * t) * k * (1.0 + 3.0 * 0.044715 * (g * g))
    return gelu, dgelu


def _mix_in_fwd(x, g_pre, w_in):
    S, D = x.shape
    TM = 512

    def body(x_ref, g_ref, w_ref, q_ref, k_ref, v_ref, p_ref, h_ref):
        n, _ = _rms_stats(x_ref[...])
        hb = (n * g_ref[...]).astype(BF16)
        h_ref[...] = hb
        q_ref[...] = (_dot(hb, w_ref[0]) * ATTN_SCALE).astype(BF16)
        k_ref[...] = _dot(hb, w_ref[1]).astype(BF16)
        v_ref[...] = _dot(hb, w_ref[2]).astype(BF16)
        p_ref[...] = _dot(hb, w_ref[3])

    row = lambda w: pl.BlockSpec((TM, w), lambda i: (i, 0))
    return pl.pallas_call(
        body, name="mix_in_fwd", grid=(S // TM,),
        in_specs=[row(D), _const_spec((1, D)), _const_spec(w_in.shape)],
        out_specs=[row(ATTN_WIDTH)] * 3 + [row(POOL_WIDTH), row(D)],
        out_shape=[jax.ShapeDtypeStruct((S, ATTN_WIDTH), BF16)] * 3
        + [jax.ShapeDtypeStruct((S, POOL_WIDTH), F32), jax.ShapeDtypeStruct((S, D), BF16)],
        compiler_params=_params(("parallel",), VMEM_LIMIT),
    )(x, g_pre, w_in)


def _band_mask(n):
    qi = lax.broadcasted_iota(jnp.int32, (ATTN_BLOCK, 2 * ATTN_BLOCK), 0)
    ki = lax.broadcasted_iota(jnp.int32, (ATTN_BLOCK, 2 * ATTN_BLOCK), 1)
    dist = qi + ATTN_BLOCK - ki
    return (dist >= 0) & (dist <= ATTN_BLOCK) & ((ki >= ATTN_BLOCK) | (n > 0))


def _attn_fwd(q, k, v, d):
    S = q.shape[0]
    L = S // d
    nb = L // ATTN_BLOCK
    view = lambda a: a.reshape(L, d * ATTN_WIDTH)

    def body(q_ref, kp_ref, kc_ref, vp_ref, vc_ref, o_ref, lse_ref):
        valid = _band_mask(pl.program_id(1))
        for h in range(N_HEADS):
            sl = slice(h * HEAD_DIM, (h + 1) * HEAD_DIM)
            kh = jnp.concatenate([kp_ref[:, sl], kc_ref[:, sl]], axis=0)
            vh = jnp.concatenate([vp_ref[:, sl], vc_ref[:, sl]], axis=0)
            s = jnp.where(valid, _dot_nt(q_ref[:, sl], kh), NEG_INF)
            m = jnp.max(s, axis=-1, keepdims=True)
            p = jnp.exp(s - m)
            den = jnp.sum(p, axis=-1, keepdims=True)
            o_ref[:, sl] = _dot(p.astype(BF16), vh) / den
            lse_ref[:, sl] = jnp.broadcast_to(m + jnp.log(den), (ATTN_BLOCK, HEAD_DIM))

    cur = pl.BlockSpec((ATTN_BLOCK, ATTN_WIDTH), lambda r, n: (n, r))
    prev = pl.BlockSpec((ATTN_BLOCK, ATTN_WIDTH), lambda r, n: (jnp.maximum(n - 1, 0), r))
    o, lse = pl.pallas_call(
        body, name=f"attn_fwd_d{d}", grid=(d, nb),
        in_specs=[cur, prev, cur, prev, cur],
        out_specs=[cur, cur],
        out_shape=[jax.ShapeDtypeStruct((L, d * ATTN_WIDTH), F32)] * 2,
        compiler_params=_params(("parallel", "parallel")),
    )(view(q), view(k), view(k), view(v), view(v))
    return o.reshape(S, ATTN_WIDTH), lse.reshape(S, ATTN_WIDTH)


def _attn_mix(outs, lses):
    S = outs[0].shape[0]
    TM = 512

    def body(o0, o1, o2, l0, l1, l2, attn_ref, lse_ref):
        ls = [l0[...], l1[...], l2[...]]
        m = jnp.maximum(jnp.maximum(ls[0], ls[1]), ls[2])
        es = [jnp.exp(l - m) for l in ls]
        den = es[0] + es[1] + es[2]
        attn_ref[...] = (es[0] * o0[...] + es[1] * o1[...] + es[2] * o2[...]) / den
        lse_ref[...] = m + jnp.log(den)

    row = pl.BlockSpec((TM, ATTN_WIDTH), lambda i: (i, 0))
    return pl.pallas_call(
        body, name="attn_mix", grid=(S // TM,),
        in_specs=[row] * 6, out_specs=[row, row],
        out_shape=[jax.ShapeDtypeStruct((S, ATTN_WIDTH), F32)] * 2,
        compiler_params=_params(("parallel",)),
    )(*outs, *lses)


def _pool_counts(first_row, rows, w):
    t = first_row + lax.broadcasted_iota(jnp.int32, (rows, 1), 0)
    return jnp.minimum(t + 1, w).astype(F32)


def _trailing_sums(xe, w):
    s, k = xe, 1
    while k < w:
        s = s + pltpu.roll(s, k, 0)
        k *= 2
    return s


def _leading_sums(xe, w):
    rows = xe.shape[0]
    s, k = xe, 1
    while k < w:
        s = s + pltpu.roll(s, rows - k, 0)
        k *= 2
    return s


def _pooled_groups(halo, cur, first_row):
    TM = cur.shape[0]
    xe = jnp.concatenate([halo, cur], axis=0)
    out = []
    for g, w in enumerate(POOL_WINDOWS):
        a = xe[:, g * POOL_GROUP:(g + 1) * POOL_GROUP]
        s = _trailing_sums(a, w)[POOL_HALO:]
        out.append(s / _pool_counts(first_row, TM, w) - a[POOL_HALO:])
    return out


def _pool_fwd(pool_in, pool_w, pool_scale):
    S = pool_in.shape[0]
    TM = 512
    HB = TM // POOL_HALO

    def body(cur_ref, halo_ref, w_ref, sc_ref, y_ref):
        i = pl.program_id(0)
        halo = jnp.where(i > 0, halo_ref[...], 0.0)
        pooled = _pooled_groups(halo, cur_ref[...], i * TM)
        for g in range(len(POOL_WINDOWS)):
            sl = slice(g * POOL_GROUP, (g + 1) * POOL_GROUP)
            y = _dot(pooled[g].astype(BF16), w_ref[g].astype(BF16)) * sc_ref[:, sl]
            y_ref[:, sl] = y.astype(BF16)

    return pl.pallas_call(
        body, name="pool_fwd", grid=(S // TM,),
        in_specs=[pl.BlockSpec((TM, POOL_WIDTH), lambda i: (i, 0)),
                  pl.BlockSpec((POOL_HALO, POOL_WIDTH), lambda i: (jnp.maximum(i * HB - 1, 0), 0)),
                  _const_spec(pool_w.shape), _const_spec((1, POOL_WIDTH))],
        out_specs=pl.BlockSpec((TM, POOL_WIDTH), lambda i: (i, 0)),
        out_shape=jax.ShapeDtypeStruct((S, POOL_WIDTH), BF16),
        compiler_params=_params(("parallel",)),
    )(pool_in, pool_in, pool_w, pool_scale)


def _mix_out_fwd(attn, pool, w_out, x, g_post, g_ffn_pre):
    S, D = x.shape
    TM = 512

    def body(a_ref, p_ref, w_ref, x_ref, gp_ref, gf_ref, mixed_ref, x1_ref, h2_ref, cat_ref):
        ab = a_ref[...].astype(BF16)
        cat_ref[:, :ATTN_WIDTH] = ab
        cat_ref[:, ATTN_WIDTH:] = p_ref[...]
        mixed = _dot(ab, w_ref[:ATTN_WIDTH, :]) + _dot(p_ref[...], w_ref[ATTN_WIDTH:, :])
        mixed_ref[...] = mixed
        n, _ = _rms_stats(mixed)
        x1 = x_ref[...] + n * gp_ref[...]
        x1_ref[...] = x1
        n2, _ = _rms_stats(x1)
        h2_ref[...] = (n2 * gf_ref[...]).astype(BF16)

    row = lambda w: pl.BlockSpec((TM, w), lambda i: (i, 0))
    return pl.pallas_call(
        body, name="mix_out_fwd", grid=(S // TM,),
        in_specs=[row(ATTN_WIDTH), row(POOL_WIDTH), _const_spec(w_out.shape), row(D),
                  _const_spec((1, D)), _const_spec((1, D))],
        out_specs=[row(D), row(D), row(D), row(D)],
        out_shape=[jax.ShapeDtypeStruct((S, D), F32), jax.ShapeDtypeStruct((S, D), F32),
                   jax.ShapeDtypeStruct((S, D), BF16), jax.ShapeDtypeStruct((S, D), BF16)],
        compiler_params=_params(("parallel",), VMEM_LIMIT),
    )(attn, pool, w_out, x, g_post, g_ffn_pre)


def _ffn_fwd(h2, x1, target, w_up, w_down, conv_w, conv_b, g_post):
    S, D = x1.shape
    CW = w_up.shape[2]
    FF = 2 * CW
    TM = 256

    def body(h2_ref, x1_ref, t_ref, wu_ref, wd_ref, cw_ref, cb_ref, g_ref,
             u_ref, yv_ref, dy_ref, df_ref, dc_ref, loss_ref, dg_ref, dcb_ref, dcw_ref,
             ue_s, cg_s, cv_s):
        i = pl.program_id(0)

        @pl.when(i == 0)
        def _():
            loss_ref[...] = jnp.zeros_like(loss_ref)
            dg_ref[...] = jnp.zeros_like(dg_ref)
            dcb_ref[...] = jnp.zeros_like(dcb_ref)
            dcw_ref[...] = jnp.zeros_like(dcw_ref)
            ue_s[0:CONV_HALO, :] = jnp.zeros((CONV_HALO, 2 * FF), F32)

        @pl.when(i > 0)
        def _():
            ue_s[0:CONV_HALO, :] = ue_s[TM:TM + CONV_HALO, :]

        def shifted(cols, k):
            return pltpu.roll(ue_s[:, cols], k, 0)[CONV_HALO:]

        hb = h2_ref[...]
        f = jnp.zeros((TM, D), F32)
        for j in range(2):
            jc = slice(j * CW, (j + 1) * CW)
            for half, c_s in ((0, cg_s), (1, cv_s)):
                blk = 2 * half + j
                cols = slice(blk * CW, (blk + 1) * CW)
                uu = _dot(hb, wu_ref[blk])
                u_ref[:, cols] = uu.astype(BF16)
                ue_s[CONV_HALO:, cols] = uu
                c_s[:, jc] = (cb_ref[:, cols] + cw_ref[2, :, cols] * uu
                              + cw_ref[1, :, cols] * shifted(cols, 1)
                              + cw_ref[0, :, cols] * shifted(cols, 2))
            gelu, _ = _gelu_tanh(cg_s[:, jc])
            yvb = (gelu * cv_s[:, jc]).astype(BF16)
            yv_ref[:, jc] = yvb
            f = f + _dot(yvb, wd_ref[jc, :])

        n, r = _rms_stats(f)
        err = x1_ref[...] + n * g_ref[...] - t_ref[...]
        loss_ref[...] += 0.5 * jnp.sum(jnp.mean(err * err, axis=-1, keepdims=True), axis=0, keepdims=True)
        dy = err / D
        dy_ref[...] = dy
        df, dg = _rms_bwd(dy, n, r, g_ref[...])
        dg_ref[...] += dg
        dfb = df.astype(BF16)
        df_ref[...] = dfb

        for j in range(2):
            jc = slice(j * CW, (j + 1) * CW)
            dyv = _dot_nt(dfb, wd_ref[jc, :])
            cv = cv_s[:, jc]
            gelu, dgelu = _gelu_tanh(cg_s[:, jc])
            for half, dcv in ((0, dyv * cv * dgelu), (1, dyv * gelu)):
                blk = 2 * half + j
                cols = slice(blk * CW, (blk + 1) * CW)
                dc_ref[:, cols] = dcv.astype(BF16)
                dcb_ref[:, cols] += jnp.sum(dcv, axis=0, keepdims=True)
                dcw_ref[2, :, cols] += jnp.sum(dcv * ue_s[CONV_HALO:, cols], axis=0, keepdims=True)
                dcw_ref[1, :, cols] += jnp.sum(dcv * shifted(cols, 1), axis=0, keepdims=True)
                dcw_ref[0, :, cols] += jnp.sum(dcv * shifted(cols, 2), axis=0, keepdims=True)

    row = lambda w: pl.BlockSpec((TM, w), lambda i: (i, 0))
    acc = lambda shape: pl.BlockSpec(shape, lambda i: (0,) * len(shape))
    return pl.pallas_call(
        body, name="ffn_fwd", grid=(S // TM,),
        in_specs=[row(D), row(D), row(D), _const_spec(w_up.shape), _const_spec(w_down.shape),
                  _const_spec(conv_w.shape), _const_spec((1, 2 * FF)), _const_spec((1, D))],
        out_specs=[row(2 * FF), row(FF), row(D), row(D), row(2 * FF),
                   acc((1, 1)), acc((1, D)), acc((1, 2 * FF)), acc((CONV_WIDTH, 1, 2 * FF))],
        out_shape=[jax.ShapeDtypeStruct((S, 2 * FF), BF16), jax.ShapeDtypeStruct((S, FF), BF16),
                   jax.ShapeDtypeStruct((S, D), F32), jax.ShapeDtypeStruct((S, D), BF16),
                   jax.ShapeDtypeStruct((S, 2 * FF), BF16),
                   jax.ShapeDtypeStruct((1, 1), F32), jax.ShapeDtypeStruct((1, D), F32),
                   jax.ShapeDtypeStruct((1, 2 * FF), F32), jax.ShapeDtypeStruct((CONV_WIDTH, 1, 2 * FF), F32)],
        scratch_shapes=[pltpu.VMEM((TM + CONV_HALO, 2 * FF), F32), pltpu.VMEM((TM, FF), F32),
                        pltpu.VMEM((TM, FF), F32)],
        compiler_params=_params(("arbitrary",), VMEM_LIMIT),
    )(h2, x1, target, w_up, w_down, conv_w, conv_b, g_post)


def _ffn_bwd(dc, conv_w, w_up, x1, g_ffn_pre, dy):
    S, D = x1.shape
    CW = w_up.shape[2]
    F2 = 4 * CW
    TM = 256
    HB = TM // CONV_HALO
    last = S // CONV_HALO - 1
    n_tiles = S // TM

    def body(dc_ref, halo_ref, cw_ref, wu_ref, x1_ref, g_ref, dy_ref, du_ref, dx1_ref, dg_ref):
        i = pl.program_id(0)

        @pl.when(i == 0)
        def _():
            dg_ref[...] = jnp.zeros_like(dg_ref)

        keep = i < n_tiles - 1
        dh2 = jnp.zeros((TM, D), F32)
        for blk in range(N_SHARD):
            cols = slice(blk * CW, (blk + 1) * CW)
            halo = jnp.where(keep, halo_ref[:, cols].astype(F32), 0.0)
            dce = jnp.concatenate([dc_ref[:, cols].astype(F32), halo], axis=0)
            rows = TM + CONV_HALO
            du = (cw_ref[2, :, cols] * dce[:TM]
                  + cw_ref[1, :, cols] * pltpu.roll(dce, rows - 1, 0)[:TM]
                  + cw_ref[0, :, cols] * pltpu.roll(dce, rows - 2, 0)[:TM])
            dub = du.astype(BF16)
            du_ref[:, cols] = dub
            dh2 = dh2 + _dot_nt(dub, wu_ref[blk])
        n2, r2 = _rms_stats(x1_ref[...])
        dx, dg = _rms_bwd(dh2, n2, r2, g_ref[...])
        dg_ref[...] += dg
        dx1_ref[...] = dy_ref[...] + dx

    row = lambda w: pl.BlockSpec((TM, w), lambda i: (i, 0))
    return pl.pallas_call(
        body, name="ffn_bwd", grid=(S // TM,),
        in_specs=[row(F2), pl.BlockSpec((CONV_HALO, F2), lambda i: (jnp.minimum((i + 1) * HB, last), 0)),
                  _const_spec(conv_w.shape), _const_spec(w_up.shape), row(D), _const_spec((1, D)), row(D)],
        out_specs=[row(F2), row(D), pl.BlockSpec((1, D), lambda i: (0, 0))],
        out_shape=[jax.ShapeDtypeStruct((S, F2), BF16), jax.ShapeDtypeStruct((S, D), F32),
                   jax.ShapeDtypeStruct((1, D), F32)],
        compiler_params=_params(("arbitrary",), VMEM_LIMIT),
    )(dc, dc, conv_w, w_up, x1, g_ffn_pre, dy)


def _matmul_tn(a, b, n_blocks, name):
    S, M = a.shape
    N = b.shape[1]
    tn = N // n_blocks
    tm = M if M <= 1024 else M // 2
    tk = 512
    nk = S // tk

    def body(a_ref, b_ref, o_ref):
        @pl.when(pl.program_id(2) == 0)
        def _():
            o_ref[...] = jnp.zeros_like(o_ref)
        o_ref[0] += _dot_tn(a_ref[...], b_ref[...])

    return pl.pallas_call(
        body, name=name, grid=(M // tm, n_blocks, nk),
        in_specs=[pl.BlockSpec((tk, tm), lambda i, j, k: (k, i)), pl.BlockSpec((tk, tn), lambda i, j, k: (k, j))],
        out_specs=pl.BlockSpec((1, tm, tn), lambda i, j, k: (j, i, 0)),
        out_shape=jax.ShapeDtypeStruct((n_blocks, M, tn), F32),
        compiler_params=_params(("parallel", "parallel", "arbitrary"), VMEM_LIMIT),
    )(a, b)


def _mix_out_bwd(dx1, mixed, g_post, w_out, attn):
    S, D = dx1.shape
    TM = 512

    def body(dx_ref, m_ref, g_ref, w_ref, a_ref, dm_ref, da_ref, dp_ref, dl_ref, dg_ref):
        @pl.when(pl.program_id(0) == 0)
        def _():
            dg_ref[...] = jnp.zeros_like(dg_ref)

        n, r = _rms_stats(m_ref[...])
        dm, dg = _rms_bwd(dx_ref[...], n, r, g_ref[...])
        dg_ref[...] += dg
        dmb = dm.astype(BF16)
        dm_ref[...] = dmb
        da = _dot_nt(dmb, w_ref[:ATTN_WIDTH, :])
        da_ref[...] = da.astype(BF16)
        dp_ref[...] = _dot_nt(dmb, w_ref[ATTN_WIDTH:, :])
        prod = da * a_ref[...]
        hi = prod.astype(BF16)
        lo = (prod - hi.astype(F32)).astype(BF16)
        ri = lax.broadcasted_iota(jnp.int32, (ATTN_WIDTH, ATTN_WIDTH), 0) // HEAD_DIM
        ci = lax.broadcasted_iota(jnp.int32, (ATTN_WIDTH, ATTN_WIDTH), 1) // HEAD_DIM
        ones = (ri == ci).astype(BF16)
        dl_ref[...] = _dot(hi, ones) + _dot(lo, ones)

    row = lambda w: pl.BlockSpec((TM, w), lambda i: (i, 0))
    return pl.pallas_call(
        body, name="mix_out_bwd", grid=(S // TM,),
        in_specs=[row(D), row(D), _const_spec((1, D)), _const_spec(w_out.shape), row(ATTN_WIDTH)],
        out_specs=[row(D), row(ATTN_WIDTH), row(POOL_WIDTH), row(ATTN_WIDTH), pl.BlockSpec((1, D), lambda i: (0, 0))],
        out_shape=[jax.ShapeDtypeStruct((S, D), BF16), jax.ShapeDtypeStruct((S, ATTN_WIDTH), BF16),
                   jax.ShapeDtypeStruct((S, POOL_WIDTH), F32), jax.ShapeDtypeStruct((S, ATTN_WIDTH), F32),
                   jax.ShapeDtypeStruct((1, D), F32)],
        compiler_params=_params(("arbitrary",), VMEM_LIMIT),
    )(dx1, mixed, g_post, w_out, attn)


def _pool_bwd(pool_in, d_pool, pool_w, pool_scale):
    S = pool_in.shape[0]
    TM = 512
    HB = TM // POOL_HALO
    last = S // POOL_HALO - 1
    G = len(POOL_WINDOWS)

    def body(cur_ref, halo_ref, dcur_ref, dnext_ref, w_ref, sc_ref, dxin_ref, dw_ref, dsc_ref):
        i = pl.program_id(0)

        @pl.when(i == 0)
        def _():
            dw_ref[...] = jnp.zeros_like(dw_ref)
            dsc_ref[...] = jnp.zeros_like(dsc_ref)

        halo = jnp.where(i > 0, halo_ref[...], 0.0)
        pooled = _pooled_groups(halo, cur_ref[...], i * TM)
        dnext = jnp.where(i < S // TM - 1, dnext_ref[...], 0.0)
        dye = jnp.concatenate([dcur_ref[...], dnext], axis=0)
        for g, w in enumerate(POOL_WINDOWS):
            sl = slice(g * POOL_GROUP, (g + 1) * POOL_GROUP)
            wg = w_ref[g].astype(BF16)
            pb = pooled[g].astype(BF16)
            dsc_ref[:, sl] += jnp.sum(dye[:TM, sl] * _dot(pb, wg), axis=0, keepdims=True)
            dpre = (dye[:, sl] * sc_ref[:, sl]).astype(BF16)
            dw_ref[g] += _dot_tn(pb, dpre[:TM])
            dpooled = _dot_nt(dpre, wg)
            z = dpooled / _pool_counts(i * TM, TM + POOL_HALO, w)
            dxin_ref[:, sl] = (_leading_sums(z, w)[:TM] - dpooled[:TM]).astype(BF16)

    row = pl.BlockSpec((TM, POOL_WIDTH), lambda i: (i, 0))
    return pl.pallas_call(
        body, name="pool_bwd", grid=(S // TM,),
        in_specs=[row, pl.BlockSpec((POOL_HALO, POOL_WIDTH), lambda i: (jnp.maximum(i * HB - 1, 0), 0)),
                  row, pl.BlockSpec((POOL_HALO, POOL_WIDTH), lambda i: (jnp.minimum((i + 1) * HB, last), 0)),
                  _const_spec(pool_w.shape), _const_spec((1, POOL_WIDTH))],
        out_specs=[row, pl.BlockSpec((G, POOL_GROUP, POOL_GROUP), lambda i: (0, 0, 0)),
                   pl.BlockSpec((1, POOL_WIDTH), lambda i: (0, 0))],
        out_shape=[jax.ShapeDtypeStruct((S, POOL_WIDTH), BF16), jax.ShapeDtypeStruct((G, POOL_GROUP, POOL_GROUP), F32),
                   jax.ShapeDtypeStruct((1, POOL_WIDTH), F32)],
        compiler_params=_params(("arbitrary",)),
    )(pool_in, pool_in, d_pool, d_pool, pool_w, pool_scale)


def _attn_bwd(q, k, v, d_attn, lse, delta, d, acc):
    S = q.shape[0]
    L = S // d
    nb = L // ATTN_BLOCK
    view = lambda a: a.reshape(L, d * ATTN_WIDTH)
    has_acc = acc is not None

    def body(*refs):
        q_ref, kp_ref, kc_ref, vp_ref, vc_ref, do_ref, lse_ref, dl_ref = refs[:8]
        refs = refs[8:]
        if has_acc:
            aq_ref, ak_ref, av_ref = refs[:3]
            refs = refs[3:]
        dq_ref, dk_ref, dv_ref, ck_s, cv_s = refs
        n = pl.program_id(1)

        @pl.when(n == 0)
        def _():
            ck_s[...] = jnp.zeros_like(ck_s)
            cv_s[...] = jnp.zeros_like(cv_s)

        @pl.when(n < nb)
        def _():
            valid = _band_mask(n)
            for h in range(N_HEADS):
                sl = slice(h * HEAD_DIM, (h + 1) * HEAD_DIM)
                qh = q_ref[:, sl]
                doh = do_ref[:, sl]
                kh = jnp.concatenate([kp_ref[:, sl], kc_ref[:, sl]], axis=0)
                vh = jnp.concatenate([vp_ref[:, sl], vc_ref[:, sl]], axis=0)
                s = _dot_nt(qh, kh)
                p = jnp.where(valid, jnp.exp(s - lse_ref[:, h * HEAD_DIM:h * HEAD_DIM + 1]), 0.0)
                dp = _dot_nt(doh, vh)
                ds = (p * (dp - dl_ref[:, h * HEAD_DIM:h * HEAD_DIM + 1])).astype(BF16)
                dq = _dot(ds, kh) * ATTN_SCALE
                dq_ref[:, sl] = dq + aq_ref[:, sl] if has_acc else dq
                dk = _dot_tn(ds, qh)
                dv = _dot_tn(p.astype(BF16), doh)
                dk_prev = ck_s[:, sl] + dk[:ATTN_BLOCK]
                dv_prev = cv_s[:, sl] + dv[:ATTN_BLOCK]
                dk_ref[:, sl] = dk_prev + ak_ref[:, sl] if has_acc else dk_prev
                dv_ref[:, sl] = dv_prev + av_ref[:, sl] if has_acc else dv_prev
                ck_s[:, sl] = dk[ATTN_BLOCK:]
                cv_s[:, sl] = dv[ATTN_BLOCK:]

        @pl.when(n == nb)
        def _():
            dk_ref[...] = ck_s[...] + ak_ref[...] if has_acc else ck_s[...]
            dv_ref[...] = cv_s[...] + av_ref[...] if has_acc else cv_s[...]

    blk = (ATTN_BLOCK, ATTN_WIDTH)
    cur = pl.BlockSpec(blk, lambda r, n: (jnp.minimum(n, nb - 1), r))
    prev = pl.BlockSpec(blk, lambda r, n: (jnp.maximum(jnp.minimum(n, nb - 1) - 1, 0), r))
    done = pl.BlockSpec(blk, lambda r, n: (jnp.maximum(n - 1, 0), r))
    shape = jax.ShapeDtypeStruct((L, d * ATTN_WIDTH), F32)
    args = [view(q), view(k), view(k), view(v), view(v), view(d_attn), view(lse), view(delta)]
    in_specs = [cur, prev, cur, prev, cur, cur, cur, cur]
    if has_acc:
        args += [view(a) for a in acc]
        in_specs += [cur, done, done]
    dq, dk, dv = pl.pallas_call(
        body, name=f"attn_bwd_d{d}", grid=(d, nb + 1),
        in_specs=in_specs, out_specs=[cur, done, done], out_shape=[shape] * 3,
        scratch_shapes=[pltpu.VMEM(blk, F32), pltpu.VMEM(blk, F32)],
        compiler_params=_params(("parallel", "arbitrary")),
    )(*args)
    return tuple(a.reshape(S, ATTN_WIDTH) for a in (dq, dk, dv))


def _mix_in_bwd(dq, dk, dv, d_pool_in, w_in, x, g_pre, dx1):
    S, D = x.shape
    TM = 512

    def body(dq_ref, dk_ref, dv_ref, dpi_ref, w_ref, x_ref, g_ref, dx1_ref, dproj_ref, gx_ref, dg_ref):
        @pl.when(pl.program_id(0) == 0)
        def _():
            dg_ref[...] = jnp.zeros_like(dg_ref)

        dh = jnp.zeros((TM, D), F32)
        for blk, ref in enumerate((dq_ref, dk_ref, dv_ref, dpi_ref)):
            db = ref[...].astype(BF16)
            dproj_ref[:, blk * ATTN_WIDTH:(blk + 1) * ATTN_WIDTH] = db
            dh = dh + _dot_nt(db, w_ref[blk])
        n, r = _rms_stats(x_ref[...])
        dx, dg = _rms_bwd(dh, n, r, g_ref[...])
        dg_ref[...] += dg
        gx_ref[...] = dx1_ref[...] + dx

    row = lambda w: pl.BlockSpec((TM, w), lambda i: (i, 0))
    return pl.pallas_call(
        body, name="mix_in_bwd", grid=(S // TM,),
        in_specs=[row(ATTN_WIDTH)] * 3 + [row(POOL_WIDTH), _const_spec(w_in.shape), row(D), _const_spec((1, D)), row(D)],
        out_specs=[row(4 * ATTN_WIDTH), row(D), pl.BlockSpec((1, D), lambda i: (0, 0))],
        out_shape=[jax.ShapeDtypeStruct((S, 4 * ATTN_WIDTH), BF16), jax.ShapeDtypeStruct((S, D), F32),
                   jax.ShapeDtypeStruct((1, D), F32)],
        compiler_params=_params(("arbitrary",), VMEM_LIMIT),
    )(dq, dk, dv, d_pool_in, w_in, x, g_pre, dx1)


def _local_step(x, target, g_mix_pre, w_in, pool_w, pool_scale, w_out, g_mix_post, g_ffn_pre,
                w_up, conv_w, conv_b, w_down, g_ffn_post):
    q, k, v, pool_in, h1 = _mix_in_fwd(x, g_mix_pre, w_in)
    outs, lses = zip(*[_attn_fwd(q, k, v, d) for d in DILATIONS])
    attn, lse = _attn_mix(outs, lses)
    pool = _pool_fwd(pool_in, pool_w, pool_scale)
    mixed, x1, h2, cat = _mix_out_fwd(attn, pool, w_out, x, g_mix_post, g_ffn_pre)

    u, yv, dy, df, dc, loss, d_g_ffn_post, d_conv_b, d_conv_w = _ffn_fwd(
        h2, x1, target, w_up, w_down, conv_w, conv_b, g_ffn_post)
    del u
    du, dx1, d_g_ffn_pre = _ffn_bwd(dc, conv_w, w_up, x1, g_ffn_pre, dy)
    d_w_up = _matmul_tn(h2, du, N_SHARD, "grad_w_up")
    d_w_down = _matmul_tn(yv, df, 1, "grad_w_down")[0]

    d_mixed, d_attn, d_pool, delta, d_g_mix_post = _mix_out_bwd(dx1, mixed, g_mix_post, w_out, attn)
    d_w_out = _matmul_tn(cat, d_mixed, 1, "grad_w_out")[0]
    d_pool_in, d_pool_w, d_pool_scale = _pool_bwd(pool_in, d_pool, pool_w, pool_scale)
    acc = None
    for d in DILATIONS:
        acc = _attn_bwd(q, k, v, d_attn, lse, delta, d, acc)
    d_proj, grad_x, d_g_mix_pre = _mix_in_bwd(*acc, d_pool_in, w_in, x, g_mix_pre, dx1)
    d_w_in = _matmul_tn(h1, d_proj, N_SHARD, "grad_w_in")

    grads = dict(g_mix_pre=d_g_mix_pre, w_in=d_w_in, pool_w=d_pool_w, pool_scale=d_pool_scale, w_out=d_w_out,
                 g_mix_post=d_g_mix_post, g_ffn_pre=d_g_ffn_pre, w_up=d_w_up, conv_w=d_conv_w, conv_b=d_conv_b,
                 w_down=d_w_down, g_ffn_post=d_g_ffn_post)
    return loss, grad_x, grads


ANY = pl.BlockSpec(memory_space=pl.ANY)


def _position():
    x, y, c = lax.axis_index("x"), lax.axis_index("y"), lax.axis_index("c")
    chips = [(1 - x, y), (x, 1 - y), (1 - x, 1 - y)]
    return x, y, c, chips


def _remote(src, dst, send_sem, recv_sem, to):
    return pltpu.make_async_remote_copy(src_ref=src, dst_ref=dst, send_sem=send_sem, recv_sem=recv_sem,
                                        device_id=to, device_id_type=MESH)


def _cast_bf16(w, name):
    R, C = w.shape
    tr = R // 2

    def body(w_ref, o_ref):
        o_ref[...] = w_ref[...].astype(BF16)

    blk = pl.BlockSpec((tr, C), lambda i: (i, 0))
    return pl.pallas_call(body, name=name, grid=(2,), in_specs=[blk], out_specs=blk,
                          out_shape=jax.ShapeDtypeStruct((R, C), BF16),
                          compiler_params=_params(("parallel",)))(w)


def _gather_weights(shards, conv_w):
    n = len(shards)

    def body(*refs):
        ins, cw_in = refs[:n], refs[n]
        outs, cw_out = refs[n + 1:2 * n + 1], refs[2 * n + 1]
        ici_send, ici_recv, d2d_send, d2d_recv, local_sem = refs[2 * n + 2:]
        x, y, c, chips = _position()
        s = 2 * x + y
        sibling = (x, y, 1 - c)

        def half(a, shard, h):
            rows = ins[a].shape[0] // 2
            return outs[a].at[shard, pl.ds(h * rows, rows), :]

        local = [pltpu.make_async_copy(ins[a], outs[a].at[s], local_sem.at[a]) for a in range(n)]
        local.append(pltpu.make_async_copy(cw_in, cw_out.at[s], local_sem.at[n]))
        for cp in local:
            cp.start()
        sends = []
        for a in range(n):
            rows = ins[a].shape[0] // 2
            for j, (px, py) in enumerate(chips):
                sends.append(_remote(ins[a].at[pl.ds(c * rows, rows), :], half(a, s, c),
                                     ici_send.at[3 * a + j], ici_recv.at[3 * a + j], (px, py, c)))
        for j, (px, py) in enumerate(chips):
            sends.append(_remote(cw_in, cw_out.at[s], ici_send.at[3 * n + j], ici_recv.at[3 * n + j], (px, py, c)))
        for cp in sends:
            cp.start()
        passed = []
        for a in range(n):
            for j, (px, py) in enumerate(chips):
                sj = 2 * px + py
                got = half(a, sj, c)
                _remote(got, got, ici_send.at[3 * a + j], ici_recv.at[3 * a + j], (px, py, c)).wait_recv()
                fwd = _remote(got, got, d2d_send.at[3 * a + j], d2d_recv.at[3 * a + j], sibling)
                fwd.start()
                passed.append(fwd)
        for j, (px, py) in enumerate(chips):
            got = cw_out.at[2 * px + py]
            _remote(got, got, ici_send.at[3 * n + j], ici_recv.at[3 * n + j], (px, py, c)).wait_recv()
        for a in range(n):
            for j, (px, py) in enumerate(chips):
                got = half(a, 2 * px + py, 1 - c)
                _remote(got, got, d2d_send.at[3 * a + j], d2d_recv.at[3 * a + j], sibling).wait_recv()
        for cp in sends + passed:
            cp.wait_send()
        for cp in local:
            cp.wait()

    return pl.pallas_call(
        body, name="gather_weights",
        in_specs=[ANY] * (n + 1), out_specs=[ANY] * (n + 1),
        out_shape=[jax.ShapeDtypeStruct((N_SHARD,) + w.shape, w.dtype) for w in shards]
        + [jax.ShapeDtypeStruct((N_SHARD,) + conv_w.shape, conv_w.dtype)],
        scratch_shapes=[pltpu.SemaphoreType.DMA((3 * n + 3,)), pltpu.SemaphoreType.DMA((3 * n + 3,)),
                        pltpu.SemaphoreType.DMA((3 * n,)), pltpu.SemaphoreType.DMA((3 * n,)),
                        pltpu.SemaphoreType.DMA((n + 1,))],
        compiler_params=pltpu.CompilerParams(has_side_effects=True),
    )(*shards, conv_w)


def _swap_halves(grads):
    n = len(grads)

    def body(*refs):
        ins, outs, send_sem, recv_sem = refs[:n], refs[n:2 * n], refs[2 * n], refs[2 * n + 1]
        x, y, c, _ = _position()
        copies = []
        for a in range(n):
            rows = ins[a].shape[1] // 2
            copies.append(_remote(ins[a].at[:, pl.ds((1 - c) * rows, rows), :], outs[a],
                                  send_sem.at[a], recv_sem.at[a], (x, y, 1 - c)))
        for cp in copies:
            cp.start()
        for cp in copies:
            cp.wait()

    return pl.pallas_call(
        body, name="swap_grad_halves",
        in_specs=[ANY] * n, out_specs=[ANY] * n,
        out_shape=[jax.ShapeDtypeStruct((g.shape[0], g.shape[1] // 2, g.shape[2]), F32) for g in grads],
        scratch_shapes=[pltpu.SemaphoreType.DMA((n,)), pltpu.SemaphoreType.DMA((n,))],
        compiler_params=pltpu.CompilerParams(has_side_effects=True),
    )(*grads)


def _pair_sum(g, got, c_arr, name):
    n_sh, R, C = g.shape
    rows = R // 2

    def body(c_ref, g_ref, r_ref, f_ref, b_ref):
        t = g_ref[...] + r_ref[...]
        f_ref[...] = t
        b_ref[...] = t.astype(BF16)

    blk = pl.BlockSpec((1, rows, C), lambda i, c_ref: (i, 0, 0))
    return pl.pallas_call(
        body, name=name,
        grid_spec=pltpu.PrefetchScalarGridSpec(
            num_scalar_prefetch=1, grid=(n_sh,),
            in_specs=[pl.BlockSpec((1, rows, C), lambda i, c_ref: (i, c_ref[0], 0)), blk],
            out_specs=[blk, blk]),
        out_shape=[jax.ShapeDtypeStruct((n_sh, rows, C), F32), jax.ShapeDtypeStruct((n_sh, rows, C), BF16)],
        compiler_params=_params(("parallel",)),
    )(c_arr, g, got)


def _scatter_grads(sums_f32, sums_bf16, small):
    n = len(sums_f32)

    def body(*refs):
        f_ins, b_ins, sm_in = refs[:n], refs[n:2 * n], refs[2 * n]
        owns, recvs, sm_out = refs[2 * n + 1:3 * n + 1], refs[3 * n + 1:4 * n + 1], refs[4 * n + 1]
        ici_send, ici_recv, sm_send, sm_recv, local_sem = refs[4 * n + 2:]
        x, y, c, chips = _position()
        s = 2 * x + y
        me = 4 * x + 2 * y + c
        local = [pltpu.make_async_copy(f_ins[a].at[s], owns[a], local_sem.at[a]) for a in range(n)]
        local.append(pltpu.make_async_copy(sm_in, sm_out.at[me], local_sem.at[n]))
        for cp in local:
            cp.start()
        copies = []
        for a in range(n):
            for j, (px, py) in enumerate(chips):
                copies.append(_remote(b_ins[a].at[2 * px + py], recvs[a].at[j],
                                      ici_send.at[3 * a + j], ici_recv.at[3 * a + j], (px, py, c)))
        for k in range(1, 8):
            peer = (x ^ (k >> 2), y ^ ((k >> 1) & 1), c ^ (k & 1))
            copies.append(_remote(sm_in, sm_out.at[me], sm_send.at[k - 1], sm_recv.at[k - 1], peer))
        for cp in copies:
            cp.start()
        for cp in copies:
            cp.wait()
        for cp in local:
            cp.wait()

    return pl.pallas_call(
        body, name="scatter_grads",
        in_specs=[ANY] * (2 * n + 1), out_specs=[ANY] * (2 * n + 1),
        out_shape=[jax.ShapeDtypeStruct(f.shape[1:], F32) for f in sums_f32]
        + [jax.ShapeDtypeStruct((3,) + f.shape[1:], BF16) for f in sums_f32]
        + [jax.ShapeDtypeStruct((8,) + small.shape, F32)],
        scratch_shapes=[pltpu.SemaphoreType.DMA((3 * n,)), pltpu.SemaphoreType.DMA((3 * n,)),
                        pltpu.SemaphoreType.DMA((7,)), pltpu.SemaphoreType.DMA((7,)),
                        pltpu.SemaphoreType.DMA((n + 1,))],
        compiler_params=pltpu.CompilerParams(has_side_effects=True),
    )(*sums_f32, *sums_bf16, small)


def _shard_sum(own, recv, name):
    rows, C = own.shape

    def body(o_ref, r_ref, t_ref):
        t_ref[...] = ((o_ref[...] + r_ref[0].astype(F32)) + r_ref[1].astype(F32)) + r_ref[2].astype(F32)

    return pl.pallas_call(
        body, name=name, grid=(1,),
        in_specs=[pl.BlockSpec((rows, C), lambda i: (0, 0)), pl.BlockSpec((3, rows, C), lambda i: (0, 0, 0))],
        out_specs=pl.BlockSpec((rows, C), lambda i: (0, 0)),
        out_shape=jax.ShapeDtypeStruct((rows, C), F32),
        compiler_params=_params(("arbitrary",)),
    )(own, recv)


def _join_halves(halves):
    n = len(halves)

    def body(*refs):
        ins, outs, send_sem, recv_sem, local_sem = refs[:n], refs[n:2 * n], refs[2 * n], refs[2 * n + 1], refs[2 * n + 2]
        x, y, c, _ = _position()
        copies, local = [], []
        for a in range(n):
            rows = ins[a].shape[0]
            mine = outs[a].at[pl.ds(c * rows, rows), :]
            local.append(pltpu.make_async_copy(ins[a], mine, local_sem.at[a]))
            copies.append(_remote(ins[a], mine, send_sem.at[a], recv_sem.at[a], (x, y, 1 - c)))
        for cp in local + copies:
            cp.start()
        for cp in copies:
            cp.wait()
        for cp in local:
            cp.wait()

    return pl.pallas_call(
        body, name="join_grad_halves",
        in_specs=[ANY] * n, out_specs=[ANY] * n,
        out_shape=[jax.ShapeDtypeStruct((2 * h.shape[0], h.shape[1]), F32) for h in halves],
        scratch_shapes=[pltpu.SemaphoreType.DMA((n,)), pltpu.SemaphoreType.DMA((n,)), pltpu.SemaphoreType.DMA((n,))],
        compiler_params=pltpu.CompilerParams(has_side_effects=True),
    )(*halves)


def _small_sum(parts):
    _, R, C = parts.shape

    def body(p_ref, o_ref):
        t = p_ref[0]
        for k in range(1, 8):
            t = t + p_ref[k]
        o_ref[...] = t

    return pl.pallas_call(
        body, name="small_grad_sum", grid=(1,),
        in_specs=[pl.BlockSpec((8, R, C), lambda i: (0, 0, 0))], out_specs=pl.BlockSpec((R, C), lambda i: (0, 0)),
        out_shape=jax.ShapeDtypeStruct((R, C), F32), compiler_params=_params(("arbitrary",)),
    )(parts)


def _adamw_math(w, g, m, v):
    m = ADAM_B1 * m + (1.0 - ADAM_B1) * g
    v = ADAM_B2 * v + (1.0 - ADAM_B2) * (g * g)
    m_hat = m / (1.0 - ADAM_B1 ** ADAM_STEP)
    v_hat = v / (1.0 - ADAM_B2 ** ADAM_STEP)
    delta = -ADAM_LR * (m_hat / (jnp.sqrt(v_hat) + ADAM_EPS) + ADAM_WD * w)
    return delta, m, v


def _adamw_big(w, g, m, v, name):
    R, C = w.shape
    tr = R // 4

    def body(w_ref, g_ref, m_ref, v_ref, d_ref, nm_ref, nv_ref):
        d_ref[...], nm_ref[...], nv_ref[...] = _adamw_math(w_ref[...], g_ref[...], m_ref[...], v_ref[...])

    blk = pl.BlockSpec((tr, C), lambda i: (i, 0))
    return pl.pallas_call(
        body, name=name, grid=(4,), in_specs=[blk] * 4, out_specs=[blk] * 3,
        out_shape=[jax.ShapeDtypeStruct((R, C), F32)] * 3, compiler_params=_params(("parallel",)),
    )(w, g, m, v)


def _adamw_small(ws, gs, ms, vs):
    n = len(ws)

    def body(*refs):
        for a in range(n):
            w, g, m, v = (refs[k * n + a][...] for k in range(4))
            d, nm, nv = _adamw_math(w, g, m, v)
            refs[4 * n + a][...] = d
            refs[5 * n + a][...] = nm
            refs[6 * n + a][...] = nv

    shapes = [jax.ShapeDtypeStruct(w.shape, F32) for w in ws]
    out = pl.pallas_call(body, name="adamw_small", out_shape=shapes * 3)(*ws, *gs, *ms, *vs)
    return out[:n], out[n:2 * n], out[2 * n:]


BIG = ("w_in", "w_out", "w_up", "w_down")
SMALL = ("g_mix_pre", "pool_w", "pool_scale", "g_mix_post", "g_ffn_pre", "conv_b", "g_ffn_post", "conv_w")
ORDER = ("g_mix_pre", "w_in", "pool_w", "pool_scale", "w_out", "g_mix_post", "g_ffn_pre", "w_up", "conv_w", "conv_b",
         "w_down", "g_ffn_post")
LANES = 128


def kernel(x, g_mix_pre, w_in, pool_w, pool_scale, w_out, g_mix_post, g_ffn_pre, w_up, conv_w, conv_b, w_down, g_ffn_post, loss_target, m_g_mix_pre, m_w_in, m_pool_w, m_pool_scale, m_w_out, m_g_mix_post, m_g_ffn_pre, m_w_up, m_conv_w, m_conv_b, m_w_down, m_g_ffn_post, v_g_mix_pre, v_w_in, v_pool_w, v_pool_scale, v_w_out, v_g_mix_post, v_g_ffn_pre, v_w_up, v_conv_w, v_conv_b, v_w_down, v_g_ffn_post):
    args = dict(locals())
    W = {n: args[n][0] for n in ORDER}
    M = {n: args["m_" + n][0] for n in ORDER}
    V = {n: args["v_" + n][0] for n in ORDER}
    for d in (W, M, V):
        d["pool_w"] = d["pool_w"].reshape(-1, POOL_GROUP)
        for n in ("g_mix_pre", "pool_scale", "g_mix_post", "g_ffn_pre", "conv_b", "g_ffn_post"):
            d[n] = d[n].reshape(1, -1)
    CW = W["w_up"].shape[1]
    c_arr = lax.axis_index("c").astype(jnp.int32).reshape(1)
    shard = 2 * lax.axis_index("x") + lax.axis_index("y")

    gathered = _gather_weights([_cast_bf16(W[n], "cast_" + n) for n in BIG], W["conv_w"])
    w_in_g, w_out_g, w_up_g, w_down_g, conv_w_g = gathered
    D = w_in_g.shape[1]
    conv_w_full = conv_w_g.transpose(1, 0, 2).reshape(CONV_WIDTH, 1, N_SHARD * CW)

    loss, grad_x, G = _local_step(
        x[0], loss_target[0], W["g_mix_pre"], w_in_g, W["pool_w"].reshape(-1, POOL_GROUP, POOL_GROUP), W["pool_scale"],
        w_out_g.reshape(D, D), W["g_mix_post"], W["g_ffn_pre"], w_up_g, conv_w_full, W["conv_b"],
        w_down_g.reshape(2 * CW, D), W["g_ffn_post"])

    partial = [G["w_in"], G["w_out"].reshape(N_SHARD, D // N_SHARD, D), G["w_up"],
               G["w_down"].reshape(N_SHARD, 2 * CW // N_SHARD, D)]
    from_sibling = _swap_halves(partial)
    sums = [_pair_sum(g, r, c_arr, "pair_sum_" + n) for g, r, n in zip(partial, from_sibling, BIG)]
    G["conv_w"] = G["conv_w"].reshape(CONV_WIDTH, N_SHARD, CW).transpose(1, 0, 2)
    G["pool_w"] = G["pool_w"].reshape(-1, POOL_GROUP)
    small = jnp.concatenate([G[n].reshape(-1, LANES) for n in SMALL], axis=0)
    scattered = _scatter_grads([s[0] for s in sums], [s[1] for s in sums], small)
    owns, recvs, small_all = scattered[:4], scattered[4:8], scattered[8]
    halves = [_shard_sum(o, r, "shard_sum_" + n) for o, r, n in zip(owns, recvs, BIG)]
    full = dict(zip(BIG, _join_halves(halves)))

    small_total = _small_sum(small_all)
    row = 0
    for n in SMALL:
        rows = G[n].size // LANES
        g = small_total[row:row + rows]
        row += rows
        if n == "conv_w":
            g = lax.dynamic_slice_in_dim(g.reshape(N_SHARD, CONV_WIDTH, CW), shard, 1, axis=0)[0]
        full[n] = g.reshape(W[n].shape)

    delta, new_m, new_v = {}, {}, {}
    for n in BIG:
        delta[n], new_m[n], new_v[n] = _adamw_big(W[n], full[n], M[n], V[n], "adamw_" + n)
    ds, nms, nvs = _adamw_small([W[n] for n in SMALL], [full[n] for n in SMALL], [M[n] for n in SMALL],
                                [V[n] for n in SMALL])
    for n, d, nm, nv in zip(SMALL, ds, nms, nvs):
        delta[n], new_m[n], new_v[n] = d, nm, nv

    loss = lax.psum(loss[0, 0], ("x", "y", "c"))
    shaped = lambda d: [d[n].reshape(args[n].shape) for n in ORDER]
    return (loss, grad_x[None], *shaped(full), *shaped(delta), *shaped(new_m), *shaped(new_v))
```

```python
import functools

import jax
import jax.numpy as jnp
from jax import lax
from jax.experimental import pallas as pl
from jax.experimental.pallas import tpu as pltpu

F32 = jnp.float32
BF16 = jnp.bfloat16

RMS_EPS = 1e-6
NEG_INF = -1e30
N_HEADS = 8
HEAD_DIM = 64
ATTN_WIDTH = N_HEADS * HEAD_DIM
ATTN_SCALE = HEAD_DIM ** -0.5
ATTN_BLOCK = 128
DILATIONS = (1, 4, 16)
POOL_WINDOWS = (2, 4, 8, 16)
POOL_GROUP = 128
POOL_WIDTH = POOL_GROUP * len(POOL_WINDOWS)
POOL_HALO = 16
CONV_WIDTH = 3
CONV_HALO = 8
N_SHARD = 4
LANES = 128

ADAM_LR = 0.001
ADAM_B1 = 0.9
ADAM_B2 = 0.999
ADAM_EPS = 1e-08
ADAM_WD = 0.01
ADAM_STEP = 10

VMEM_LIMIT = 60 * 1024 * 1024
MESH = pl.DeviceIdType.MESH
NT = (((1,), (1,)), ((), ()))
TN = (((0,), (0,)), ((), ()))


def _params(sem, vmem=None):
    return pltpu.CompilerParams(dimension_semantics=sem, vmem_limit_bytes=vmem)


def _const_spec(shape):
    zeros = (0,) * len(shape)
    return pl.BlockSpec(shape, lambda *_: zeros, pipeline_mode=pl.Buffered(1))


def _dot(a, b):
    return jnp.dot(a, b, preferred_element_type=F32)


def _dot_nt(a, b):
    return lax.dot_general(a, b, NT, preferred_element_type=F32)


def _dot_tn(a, b):
    return lax.dot_general(a, b, TN, preferred_element_type=F32)


def _rms_stats(x):
    r = lax.rsqrt(jnp.mean(x * x, axis=-1, keepdims=True) + RMS_EPS)
    return x * r, r


def _rms_bwd(dy, n, r, g):
    dg = jnp.sum(dy * n, axis=0, keepdims=True)
    dn = dy * g
    dx = r * (dn - n * jnp.mean(dn * n, axis=-1, keepdims=True))
    return dx, dg


def _gelu_tanh(g):
    k = 0.7978845608028654
    a = k * (g + 0.044715 * (g * g * g))
    t = jnp.tanh(a)
    gelu = 0.5 * g * (1.0 + t)
    dgelu = 0.5 * (1.0 + t) + 0.5 * g * (1.0 - t * t) * k * (1.0 + 3.0 * 0.044715 * (g * g))
    return gelu, dgelu


def _residue_shape(S, d, dtype):
    return jax.ShapeDtypeStruct((S // d, d * ATTN_WIDTH), dtype)


def _residue_spec(TM, d):
    return pl.BlockSpec((TM // d, d * ATTN_WIDTH), lambda i: (i, 0))


def _token_scratch(TM):
    return [pltpu.VMEM((TM, LANES), F32)] * (ATTN_WIDTH // LANES)


def _put_tokens(dst_s, val):
    for cb, chunk in enumerate(dst_s):
        chunk[...] = val[:, cb * LANES:(cb + 1) * LANES]


def _get_tokens(src_s):
    return jnp.concatenate([chunk[...] for chunk in src_s], axis=1)


def _to_residue(val, src_s, out_ref, d, dtype):
    if d == 1:
        out_ref[...] = val.astype(dtype)
        return
    rows = src_s[0].shape[0]
    for r in range(d):
        for cb, chunk in enumerate(src_s):
            col = r * ATTN_WIDTH + cb * LANES
            out_ref[:, col:col + LANES] = chunk[pl.ds(r, rows // d, stride=d), :].astype(dtype)


def _from_residue(in_ref, dst_s, d):
    if d == 1:
        return in_ref[...].astype(F32)
    rows = dst_s[0].shape[0]
    for r in range(d):
        for cb, chunk in enumerate(dst_s):
            col = r * ATTN_WIDTH + cb * LANES
            chunk[pl.ds(r, rows // d, stride=d), :] = in_ref[:, col:col + LANES].astype(F32)
    return _get_tokens(dst_s)


def _mix_in_fwd(x, g_pre, w_in):
    S, D = x.shape
    TM = 512

    def body(x_ref, g_ref, w_ref, *refs):
        qkv_refs, p_ref, h_ref, t_s = refs[:9], refs[9], refs[10], refs[11:]
        n, _ = _rms_stats(x_ref[...])
        hb = (n * g_ref[...]).astype(BF16)
        h_ref[...] = hb
        for a in range(3):
            res = _dot(hb, w_ref[a])
            if a == 0:
                res = res * ATTN_SCALE
            _put_tokens(t_s, res)
            for i, d in enumerate(DILATIONS):
                _to_residue(res, t_s, qkv_refs[3 * i + a], d, BF16)
        p_ref[...] = _dot(hb, w_ref[3])

    row = lambda w: pl.BlockSpec((TM, w), lambda i: (i, 0))
    out = pl.pallas_call(
        body, name="mix_in_fwd", grid=(S // TM,),
        in_specs=[row(D), _const_spec((1, D)), _const_spec(w_in.shape)],
        out_specs=[_residue_spec(TM, d) for d in DILATIONS for _ in range(3)] + [row(POOL_WIDTH), row(D)],
        out_shape=[_residue_shape(S, d, BF16) for d in DILATIONS for _ in range(3)]
        + [jax.ShapeDtypeStruct((S, POOL_WIDTH), F32), jax.ShapeDtypeStruct((S, D), BF16)],
        scratch_shapes=_token_scratch(TM),
        compiler_params=_params(("parallel",), VMEM_LIMIT),
    )(x, g_pre, w_in)
    return [out[0:3], out[3:6], out[6:9]], out[9], out[10]


def _band_mask(n):
    qi = lax.broadcasted_iota(jnp.int32, (ATTN_BLOCK, 2 * ATTN_BLOCK), 0)
    ki = lax.broadcasted_iota(jnp.int32, (ATTN_BLOCK, 2 * ATTN_BLOCK), 1)
    dist = qi + ATTN_BLOCK - ki
    return (dist >= 0) & (dist <= ATTN_BLOCK) & ((ki >= ATTN_BLOCK) | (n > 0))


def _attn_fwd(q, k, v, d):
    L = q.shape[0]
    nb = L // ATTN_BLOCK

    def body(q_ref, kp_ref, kc_ref, vp_ref, vc_ref, o_ref, lse_ref):
        valid = _band_mask(pl.program_id(1))
        for h in range(N_HEADS):
            sl = slice(h * HEAD_DIM, (h + 1) * HEAD_DIM)
            kh = jnp.concatenate([kp_ref[:, sl], kc_ref[:, sl]], axis=0)
            vh = jnp.concatenate([vp_ref[:, sl], vc_ref[:, sl]], axis=0)
            s = jnp.where(valid, _dot_nt(q_ref[:, sl], kh), NEG_INF)
            m = jnp.max(s, axis=-1, keepdims=True)
            p = jnp.exp(s - m)
            den = jnp.sum(p, axis=-1, keepdims=True)
            o_ref[:, sl] = _dot(p.astype(BF16), vh) / den
            lse_ref[:, sl] = jnp.broadcast_to(m + jnp.log(den), (ATTN_BLOCK, HEAD_DIM))

    cur = pl.BlockSpec((ATTN_BLOCK, ATTN_WIDTH), lambda r, n: (n, r))
    prev = pl.BlockSpec((ATTN_BLOCK, ATTN_WIDTH), lambda r, n: (jnp.maximum(n - 1, 0), r))
    return pl.pallas_call(
        body, name=f"attn_fwd_d{d}", grid=(d, nb),
        in_specs=[cur, prev, cur, prev, cur],
        out_specs=[cur, cur],
        out_shape=[jax.ShapeDtypeStruct((L, d * ATTN_WIDTH), F32)] * 2,
        compiler_params=_params(("parallel", "parallel")),
    )(q, k, k, v, v)


def _attn_mix(outs, lses):
    S = outs[0].shape[0]
    TM = 512
    n = len(DILATIONS)

    def body(*refs):
        o_refs, l_refs, attn_ref, lse_refs, t_s = refs[:n], refs[n:2 * n], refs[2 * n], refs[2 * n + 1:3 * n + 1], refs[3 * n + 1:]
        os = [_from_residue(o_refs[i], t_s, d) for i, d in enumerate(DILATIONS)]
        ls = [_from_residue(l_refs[i], t_s, d) for i, d in enumerate(DILATIONS)]
        m = jnp.maximum(jnp.maximum(ls[0], ls[1]), ls[2])
        es = [jnp.exp(l - m) for l in ls]
        den = es[0] + es[1] + es[2]
        attn_ref[...] = (es[0] * os[0] + es[1] * os[1] + es[2] * os[2]) / den
        lse = m + jnp.log(den)
        _put_tokens(t_s, lse)
        for i, d in enumerate(DILATIONS):
            _to_residue(lse, t_s, lse_refs[i], d, F32)

    specs = [_residue_spec(TM, d) for d in DILATIONS]
    out = pl.pallas_call(
        body, name="attn_mix", grid=(S // TM,),
        in_specs=specs * 2, out_specs=[specs[0]] + specs,
        out_shape=[jax.ShapeDtypeStruct((S, ATTN_WIDTH), F32)] + [_residue_shape(S, d, F32) for d in DILATIONS],
        scratch_shapes=_token_scratch(TM),
        compiler_params=_params(("parallel",)),
    )(*outs, *lses)
    return out[0], out[1:]


def _pool_counts(first_row, rows, w):
    t = first_row + lax.broadcasted_iota(jnp.int32, (rows, 1), 0)
    return jnp.minimum(t + 1, w).astype(F32)


def _trailing_sums(xe, w):
    s, k = xe, 1
    while k < w:
        s = s + pltpu.roll(s, k, 0)
        k *= 2
    return s


def _leading_sums(xe, w):
    rows = xe.shape[0]
    s, k = xe, 1
    while k < w:
        s = s + pltpu.roll(s, rows - k, 0)
        k *= 2
    return s


def _pooled_groups(halo, cur, first_row):
    TM = cur.shape[0]
    xe = jnp.concatenate([halo, cur], axis=0)
    out = []
    for g, w in enumerate(POOL_WINDOWS):
        a = xe[:, g * POOL_GROUP:(g + 1) * POOL_GROUP]
        s = _trailing_sums(a, w)[POOL_HALO:]
        out.append(s / _pool_counts(first_row, TM, w) - a[POOL_HALO:])
    return out


def _pool_fwd(pool_in, pool_w, pool_scale):
    S = pool_in.shape[0]
    TM = 512
    HB = TM // POOL_HALO

    def body(cur_ref, halo_ref, w_ref, sc_ref, y_ref):
        i = pl.program_id(0)
        halo = jnp.where(i > 0, halo_ref[...], 0.0)
        pooled = _pooled_groups(halo, cur_ref[...], i * TM)
        for g in range(len(POOL_WINDOWS)):
            sl = slice(g * POOL_GROUP, (g + 1) * POOL_GROUP)
            y = _dot(pooled[g].astype(BF16), w_ref[g].astype(BF16)) * sc_ref[:, sl]
            y_ref[:, sl] = y.astype(BF16)

    return pl.pallas_call(
        body, name="pool_fwd", grid=(S // TM,),
        in_specs=[pl.BlockSpec((TM, POOL_WIDTH), lambda i: (i, 0)),
                  pl.BlockSpec((POOL_HALO, POOL_WIDTH), lambda i: (jnp.maximum(i * HB - 1, 0), 0)),
                  _const_spec(pool_w.shape), _const_spec((1, POOL_WIDTH))],
        out_specs=pl.BlockSpec((TM, POOL_WIDTH), lambda i: (i, 0)),
        out_shape=jax.ShapeDtypeStruct((S, POOL_WIDTH), BF16),
        compiler_params=_params(("parallel",)),
    )(pool_in, pool_in, pool_w, pool_scale)


def _mix_out_fwd(attn, pool, w_out, x, g_post, g_ffn_pre):
    S, D = x.shape
    TM = 512

    def body(a_ref, p_ref, w_ref, x_ref, gp_ref, gf_ref, mixed_ref, x1_ref, h2_ref, cat_ref):
        ab = a_ref[...].astype(BF16)
        cat_ref[:, :ATTN_WIDTH] = ab
        cat_ref[:, ATTN_WIDTH:] = p_ref[...]
        mixed = _dot(ab, w_ref[:ATTN_WIDTH, :]) + _dot(p_ref[...], w_ref[ATTN_WIDTH:, :])
        mixed_ref[...] = mixed
        n, _ = _rms_stats(mixed)
        x1 = x_ref[...] + n * gp_ref[...]
        x1_ref[...] = x1
        n2, _ = _rms_stats(x1)
        h2_ref[...] = (n2 * gf_ref[...]).astype(BF16)

    row = lambda w: pl.BlockSpec((TM, w), lambda i: (i, 0))
    return pl.pallas_call(
        body, name="mix_out_fwd", grid=(S // TM,),
        in_specs=[row(ATTN_WIDTH), row(POOL_WIDTH), _const_spec(w_out.shape), row(D),
                  _const_spec((1, D)), _const_spec((1, D))],
        out_specs=[row(D), row(D), row(D), row(D)],
        out_shape=[jax.ShapeDtypeStruct((S, D), F32), jax.ShapeDtypeStruct((S, D), F32),
                   jax.ShapeDtypeStruct((S, D), BF16), jax.ShapeDtypeStruct((S, D), BF16)],
        compiler_params=_params(("parallel",), VMEM_LIMIT),
    )(attn, pool, w_out, x, g_post, g_ffn_pre)


def _ffn_fwd(h2, x1, target, w_up, w_down, conv_w, conv_b, g_post):
    S, D = x1.shape
    CW = w_up.shape[2]
    FF = 2 * CW
    TM = 256

    def body(h2_ref, x1_ref, t_ref, wu_ref, wd_ref, cw_ref, cb_ref, g_ref,
             u_ref, yv_ref, dy_ref, df_ref, dc_ref, loss_ref, dg_ref, dcb_ref, dcw_ref,
             ue_s, cg_s, cv_s):
        i = pl.program_id(0)

        @pl.when(i == 0)
        def _():
            loss_ref[...] = jnp.zeros_like(loss_ref)
            dg_ref[...] = jnp.zeros_like(dg_ref)
            dcb_ref[...] = jnp.zeros_like(dcb_ref)
            dcw_ref[...] = jnp.zeros_like(dcw_ref)
            ue_s[0:CONV_HALO, :] = jnp.zeros((CONV_HALO, 2 * FF), F32)

        @pl.when(i > 0)
        def _():
            ue_s[0:CONV_HALO, :] = ue_s[TM:TM + CONV_HALO, :]

        def shifted(cols, k):
            return pltpu.roll(ue_s[:, cols], k, 0)[CONV_HALO:]

        hb = h2_ref[...]
        f = jnp.zeros((TM, D), F32)
        for j in range(2):
            jc = slice(j * CW, (j + 1) * CW)
            for half, c_s in ((0, cg_s), (1, cv_s)):
                blk = 2 * half + j
                cols = slice(blk * CW, (blk + 1) * CW)
                uu = _dot(hb, wu_ref[blk])
                u_ref[:, cols] = uu.astype(BF16)
                ue_s[CONV_HALO:, cols] = uu
                c_s[:, jc] = (cb_ref[:, cols] + cw_ref[2, :, cols] * uu
                              + cw_ref[1, :, cols] * shifted(cols, 1)
                              + cw_ref[0, :, cols] * shifted(cols, 2))
            gelu, _ = _gelu_tanh(cg_s[:, jc])
            yvb = (gelu * cv_s[:, jc]).astype(BF16)
            yv_ref[:, jc] = yvb
            f = f + _dot(yvb, wd_ref[jc, :])

        n, r = _rms_stats(f)
        err = x1_ref[...] + n * g_ref[...] - t_ref[...]
        loss_ref[...] += 0.5 * jnp.sum(jnp.mean(err * err, axis=-1, keepdims=True), axis=0, keepdims=True)
        dy = err / D
        dy_ref[...] = dy
        df, dg = _rms_bwd(dy, n, r, g_ref[...])
        dg_ref[...] += dg
        dfb = df.astype(BF16)
        df_ref[...] = dfb

        for j in range(2):
            jc = slice(j * CW, (j + 1) * CW)
            dyv = _dot_nt(dfb, wd_ref[jc, :])
            cv = cv_s[:, jc]
            gelu, dgelu = _gelu_tanh(cg_s[:, jc])
            for half, dcv in ((0, dyv * cv * dgelu), (1, dyv * gelu)):
                blk = 2 * half + j
                cols = slice(blk * CW, (blk + 1) * CW)
                dc_ref[:, cols] = dcv.astype(BF16)
                dcb_ref[:, cols] += jnp.sum(dcv, axis=0, keepdims=True)
                dcw_ref[2, :, cols] += jnp.sum(dcv * ue_s[CONV_HALO:, cols], axis=0, keepdims=True)
                dcw_ref[1, :, cols] += jnp.sum(dcv * shifted(cols, 1), axis=0, keepdims=True)
                dcw_ref[0, :, cols] += jnp.sum(dcv * shifted(cols, 2), axis=0, keepdims=True)

    row = lambda w: pl.BlockSpec((TM, w), lambda i: (i, 0))
    acc = lambda shape: pl.BlockSpec(shape, lambda i: (0,) * len(shape))
    return pl.pallas_call(
        body, name="ffn_fwd", grid=(S // TM,),
        in_specs=[row(D), row(D), row(D), _const_spec(w_up.shape), _const_spec(w_down.shape),
                  _const_spec(conv_w.shape), _const_spec((1, 2 * FF)), _const_spec((1, D))],
        out_specs=[row(2 * FF), row(FF), row(D), row(D), row(2 * FF),
                   acc((1, 1)), acc((1, D)), acc((1, 2 * FF)), acc((CONV_WIDTH, 1, 2 * FF))],
        out_shape=[jax.ShapeDtypeStruct((S, 2 * FF), BF16), jax.ShapeDtypeStruct((S, FF), BF16),
                   jax.ShapeDtypeStruct((S, D), F32), jax.ShapeDtypeStruct((S, D), BF16),
                   jax.ShapeDtypeStruct((S, 2 * FF), BF16),
                   jax.ShapeDtypeStruct((1, 1), F32), jax.ShapeDtypeStruct((1, D), F32),
                   jax.ShapeDtypeStruct((1, 2 * FF), F32), jax.ShapeDtypeStruct((CONV_WIDTH, 1, 2 * FF), F32)],
        scratch_shapes=[pltpu.VMEM((TM + CONV_HALO, 2 * FF), F32), pltpu.VMEM((TM, FF), F32),
                        pltpu.VMEM((TM, FF), F32)],
        compiler_params=_params(("arbitrary",), VMEM_LIMIT),
    )(h2, x1, target, w_up, w_down, conv_w, conv_b, g_post)


def _ffn_bwd(dc, conv_w, w_up, x1, g_ffn_pre, dy):
    S, D = x1.shape
    CW = w_up.shape[2]
    F2 = 4 * CW
    TM = 256
    HB = TM // CONV_HALO
    last = S // CONV_HALO - 1
    n_tiles = S // TM

    def body(dc_ref, halo_ref, cw_ref, wu_ref, x1_ref, g_ref, dy_ref, du_ref, dx1_ref, dg_ref):
        i = pl.program_id(0)

        @pl.when(i == 0)
        def _():
            dg_ref[...] = jnp.zeros_like(dg_ref)

        keep = i < n_tiles - 1
        dh2 = jnp.zeros((TM, D), F32)
        for blk in range(N_SHARD):
            cols = slice(blk * CW, (blk + 1) * CW)
            halo = jnp.where(keep, halo_ref[:, cols].astype(F32), 0.0)
            dce = jnp.concatenate([dc_ref[:, cols].astype(F32), halo], axis=0)
            rows = TM + CONV_HALO
            du = (cw_ref[2, :, cols] * dce[:TM]
                  + cw_ref[1, :, cols] * pltpu.roll(dce, rows - 1, 0)[:TM]
                  + cw_ref[0, :, cols] * pltpu.roll(dce, rows - 2, 0)[:TM])
            dub = du.astype(BF16)
            du_ref[:, cols] = dub
            dh2 = dh2 + _dot_nt(dub, wu_ref[blk])
        n2, r2 = _rms_stats(x1_ref[...])
        dx, dg = _rms_bwd(dh2, n2, r2, g_ref[...])
        dg_ref[...] += dg
        dx1_ref[...] = dy_ref[...] + dx

    row = lambda w: pl.BlockSpec((TM, w), lambda i: (i, 0))
    return pl.pallas_call(
        body, name="ffn_bwd", grid=(S // TM,),
        in_specs=[row(F2), pl.BlockSpec((CONV_HALO, F2), lambda i: (jnp.minimum((i + 1) * HB, last), 0)),
                  _const_spec(conv_w.shape), _const_spec(w_up.shape), row(D), _const_spec((1, D)), row(D)],
        out_specs=[row(F2), row(D), pl.BlockSpec((1, D), lambda i: (0, 0))],
        out_shape=[jax.ShapeDtypeStruct((S, F2), BF16), jax.ShapeDtypeStruct((S, D), F32),
                   jax.ShapeDtypeStruct((1, D), F32)],
        compiler_params=_params(("arbitrary",), VMEM_LIMIT),
    )(dc, dc, conv_w, w_up, x1, g_ffn_pre, dy)


def _matmul_tn(a, b, n_blocks, name):
    S, M = a.shape
    N = b.shape[1]
    tn = N // n_blocks
    tm = M if M <= 1024 else M // 2
    tk = 512
    nk = S // tk

    def body(a_ref, b_ref, o_ref):
        @pl.when(pl.program_id(2) == 0)
        def _():
            o_ref[...] = jnp.zeros_like(o_ref)
        o_ref[0] += _dot_tn(a_ref[...], b_ref[...])

    return pl.pallas_call(
        body, name=name, grid=(M // tm, n_blocks, nk),
        in_specs=[pl.BlockSpec((tk, tm), lambda i, j, k: (k, i)), pl.BlockSpec((tk, tn), lambda i, j, k: (k, j))],
        out_specs=pl.BlockSpec((1, tm, tn), lambda i, j, k: (j, i, 0)),
        out_shape=jax.ShapeDtypeStruct((n_blocks, M, tn), F32),
        compiler_params=_params(("parallel", "parallel", "arbitrary"), VMEM_LIMIT),
    )(a, b)


def _mix_out_bwd(dx1, mixed, g_post, w_out, attn):
    S, D = dx1.shape
    TM = 512

    nd = len(DILATIONS)

    def body(dx_ref, m_ref, g_ref, w_ref, a_ref, dm_ref, dp_ref, dg_ref, *refs):
        da_refs, dl_refs, t_s = refs[:nd], refs[nd:2 * nd], refs[2 * nd:]

        @pl.when(pl.program_id(0) == 0)
        def _():
            dg_ref[...] = jnp.zeros_like(dg_ref)

        n, r = _rms_stats(m_ref[...])
        dm, dg = _rms_bwd(dx_ref[...], n, r, g_ref[...])
        dg_ref[...] += dg
        dmb = dm.astype(BF16)
        dm_ref[...] = dmb
        da = _dot_nt(dmb, w_ref[:ATTN_WIDTH, :])
        _put_tokens(t_s, da)
        for i, d in enumerate(DILATIONS):
            _to_residue(da, t_s, da_refs[i], d, BF16)
        dp_ref[...] = _dot_nt(dmb, w_ref[ATTN_WIDTH:, :])
        prod = da * a_ref[...]
        hi = prod.astype(BF16)
        lo = (prod - hi.astype(F32)).astype(BF16)
        ri = lax.broadcasted_iota(jnp.int32, (ATTN_WIDTH, ATTN_WIDTH), 0) // HEAD_DIM
        ci = lax.broadcasted_iota(jnp.int32, (ATTN_WIDTH, ATTN_WIDTH), 1) // HEAD_DIM
        ones = (ri == ci).astype(BF16)
        delta = _dot(hi, ones) + _dot(lo, ones)
        _put_tokens(t_s, delta)
        for i, d in enumerate(DILATIONS):
            _to_residue(delta, t_s, dl_refs[i], d, F32)

    row = lambda w: pl.BlockSpec((TM, w), lambda i: (i, 0))
    specs = [_residue_spec(TM, d) for d in DILATIONS]
    out = pl.pallas_call(
        body, name="mix_out_bwd", grid=(S // TM,),
        in_specs=[row(D), row(D), _const_spec((1, D)), _const_spec(w_out.shape), row(ATTN_WIDTH)],
        out_specs=[row(D), row(POOL_WIDTH), pl.BlockSpec((1, D), lambda i: (0, 0))] + specs * 2,
        out_shape=[jax.ShapeDtypeStruct((S, D), BF16), jax.ShapeDtypeStruct((S, POOL_WIDTH), F32),
                   jax.ShapeDtypeStruct((1, D), F32)]
        + [_residue_shape(S, d, BF16) for d in DILATIONS] + [_residue_shape(S, d, F32) for d in DILATIONS],
        scratch_shapes=_token_scratch(TM),
        compiler_params=_params(("arbitrary",), VMEM_LIMIT),
    )(dx1, mixed, g_post, w_out, attn)
    return out[0], out[1], out[2], out[3:3 + nd], out[3 + nd:]


def _pool_bwd(pool_in, d_pool, pool_w, pool_scale):
    S = pool_in.shape[0]
    TM = 512
    HB = TM // POOL_HALO
    last = S // POOL_HALO - 1
    G = len(POOL_WINDOWS)

    def body(cur_ref, halo_ref, dcur_ref, dnext_ref, w_ref, sc_ref, dxin_ref, dw_ref, dsc_ref):
        i = pl.program_id(0)

        @pl.when(i == 0)
        def _():
            dw_ref[...] = jnp.zeros_like(dw_ref)
            dsc_ref[...] = jnp.zeros_like(dsc_ref)

        halo = jnp.where(i > 0, halo_ref[...], 0.0)
        pooled = _pooled_groups(halo, cur_ref[...], i * TM)
        dnext = jnp.where(i < S // TM - 1, dnext_ref[...], 0.0)
        dye = jnp.concatenate([dcur_ref[...], dnext], axis=0)
        for g, w in enumerate(POOL_WINDOWS):
            sl = slice(g * POOL_GROUP, (g + 1) * POOL_GROUP)
            wg = w_ref[g].astype(BF16)
            pb = pooled[g].astype(BF16)
            dsc_ref[:, sl] += jnp.sum(dye[:TM, sl] * _dot(pb, wg), axis=0, keepdims=True)
            dpre = (dye[:, sl] * sc_ref[:, sl]).astype(BF16)
            dw_ref[g] += _dot_tn(pb, dpre[:TM])
            dpooled = _dot_nt(dpre, wg)
            z = dpooled / _pool_counts(i * TM, TM + POOL_HALO, w)
            dxin_ref[:, sl] = (_leading_sums(z, w)[:TM] - dpooled[:TM]).astype(BF16)

    row = pl.BlockSpec((TM, POOL_WIDTH), lambda i: (i, 0))
    return pl.pallas_call(
        body, name="pool_bwd", grid=(S // TM,),
        in_specs=[row, pl.BlockSpec((POOL_HALO, POOL_WIDTH), lambda i: (jnp.maximum(i * HB - 1, 0), 0)),
                  row, pl.BlockSpec((POOL_HALO, POOL_WIDTH), lambda i: (jnp.minimum((i + 1) * HB, last), 0)),
                  _const_spec(pool_w.shape), _const_spec((1, POOL_WIDTH))],
        out_specs=[row, pl.BlockSpec((G, POOL_GROUP, POOL_GROUP), lambda i: (0, 0, 0)),
                   pl.BlockSpec((1, POOL_WIDTH), lambda i: (0, 0))],
        out_shape=[jax.ShapeDtypeStruct((S, POOL_WIDTH), BF16), jax.ShapeDtypeStruct((G, POOL_GROUP, POOL_GROUP), F32),
                   jax.ShapeDtypeStruct((1, POOL_WIDTH), F32)],
        compiler_params=_params(("arbitrary",)),
    )(pool_in, pool_in, d_pool, d_pool, pool_w, pool_scale)


def _attn_bwd(q, k, v, d_attn, lse, delta, d):
    L = q.shape[0]
    nb = L // ATTN_BLOCK

    def body(q_ref, kp_ref, kc_ref, vp_ref, vc_ref, do_ref, lse_ref, dl_ref, dq_ref, dk_ref, dv_ref, ck_s, cv_s):
        n = pl.program_id(1)

        @pl.when(n == 0)
        def _():
            ck_s[...] = jnp.zeros_like(ck_s)
            cv_s[...] = jnp.zeros_like(cv_s)

        @pl.when(n < nb)
        def _():
            valid = _band_mask(n)
            for h in range(N_HEADS):
                sl = slice(h * HEAD_DIM, (h + 1) * HEAD_DIM)
                qh = q_ref[:, sl]
                doh = do_ref[:, sl]
                kh = jnp.concatenate([kp_ref[:, sl], kc_ref[:, sl]], axis=0)
                vh = jnp.concatenate([vp_ref[:, sl], vc_ref[:, sl]], axis=0)
                s = _dot_nt(qh, kh)
                p = jnp.where(valid, jnp.exp(s - lse_ref[:, h * HEAD_DIM:h * HEAD_DIM + 1]), 0.0)
                dp = _dot_nt(doh, vh)
                ds = (p * (dp - dl_ref[:, h * HEAD_DIM:h * HEAD_DIM + 1])).astype(BF16)
                dq_ref[:, sl] = _dot(ds, kh) * ATTN_SCALE
                dk = _dot_tn(ds, qh)
                dv = _dot_tn(p.astype(BF16), doh)
                dk_ref[:, sl] = ck_s[:, sl] + dk[:ATTN_BLOCK]
                dv_ref[:, sl] = cv_s[:, sl] + dv[:ATTN_BLOCK]
                ck_s[:, sl] = dk[ATTN_BLOCK:]
                cv_s[:, sl] = dv[ATTN_BLOCK:]

        @pl.when(n == nb)
        def _():
            dk_ref[...] = ck_s[...]
            dv_ref[...] = cv_s[...]

    blk = (ATTN_BLOCK, ATTN_WIDTH)
    cur = pl.BlockSpec(blk, lambda r, n: (jnp.minimum(n, nb - 1), r))
    prev = pl.BlockSpec(blk, lambda r, n: (jnp.maximum(jnp.minimum(n, nb - 1) - 1, 0), r))
    done = pl.BlockSpec(blk, lambda r, n: (jnp.maximum(n - 1, 0), r))
    return pl.pallas_call(
        body, name=f"attn_bwd_d{d}", grid=(d, nb + 1),
        in_specs=[cur, prev, cur, prev, cur, cur, cur, cur], out_specs=[cur, done, done],
        out_shape=[jax.ShapeDtypeStruct((L, d * ATTN_WIDTH), F32)] * 3,
        scratch_shapes=[pltpu.VMEM(blk, F32), pltpu.VMEM(blk, F32)],
        compiler_params=_params(("parallel", "arbitrary")),
    )(q, k, k, v, v, d_attn, lse, delta)


def _mix_in_bwd(dqkv, d_pool_in, w_in, x, g_pre, dx1):
    S, D = x.shape
    TM = 512
    nd = len(DILATIONS)

    def body(*refs):
        g_refs = refs[:3 * nd]
        dpi_ref, w_ref, x_ref, g_ref, dx1_ref, dproj_ref, gx_ref, dg_ref = refs[3 * nd:3 * nd + 8]
        t_s = refs[3 * nd + 8:]

        @pl.when(pl.program_id(0) == 0)
        def _():
            dg_ref[...] = jnp.zeros_like(dg_ref)

        dh = jnp.zeros((TM, D), F32)
        for a in range(4):
            if a < 3:
                tot = g_refs[a][...]
                for i, d in enumerate(DILATIONS[1:]):
                    tot = tot + _from_residue(g_refs[3 * (i + 1) + a], t_s, d)
                db = tot.astype(BF16)
            else:
                db = dpi_ref[...]
            dproj_ref[:, a * ATTN_WIDTH:(a + 1) * ATTN_WIDTH] = db
            dh = dh + _dot_nt(db, w_ref[a])
        n, r = _rms_stats(x_ref[...])
        dx, dg = _rms_bwd(dh, n, r, g_ref[...])
        dg_ref[...] += dg
        gx_ref[...] = dx1_ref[...] + dx

    row = lambda w: pl.BlockSpec((TM, w), lambda i: (i, 0))
    return pl.pallas_call(
        body, name="mix_in_bwd", grid=(S // TM,),
        in_specs=[_residue_spec(TM, d) for d in DILATIONS for _ in range(3)]
        + [row(POOL_WIDTH), _const_spec(w_in.shape), row(D), _const_spec((1, D)), row(D)],
        out_specs=[row(4 * ATTN_WIDTH), row(D), pl.BlockSpec((1, D), lambda i: (0, 0))],
        out_shape=[jax.ShapeDtypeStruct((S, 4 * ATTN_WIDTH), BF16), jax.ShapeDtypeStruct((S, D), F32),
                   jax.ShapeDtypeStruct((1, D), F32)],
        scratch_shapes=_token_scratch(TM),
        compiler_params=_params(("arbitrary",), VMEM_LIMIT),
    )(*[g for gs in dqkv for g in gs], d_pool_in, w_in, x, g_pre, dx1)


def _local_step(x, target, g_mix_pre, w_in, pool_w, pool_scale, w_out, g_mix_post, g_ffn_pre,
                w_up, conv_w, conv_b, w_down, g_ffn_post):
    qkv, pool_in, h1 = _mix_in_fwd(x, g_mix_pre, w_in)
    outs, lses = zip(*[_attn_fwd(*qkv[i], d) for i, d in enumerate(DILATIONS)])
    attn, lse = _attn_mix(outs, lses)
    pool = _pool_fwd(pool_in, pool_w, pool_scale)
    mixed, x1, h2, cat = _mix_out_fwd(attn, pool, w_out, x, g_mix_post, g_ffn_pre)

    u, yv, dy, df, dc, loss, d_g_ffn_post, d_conv_b, d_conv_w = _ffn_fwd(
        h2, x1, target, w_up, w_down, conv_w, conv_b, g_ffn_post)
    del u
    du, dx1, d_g_ffn_pre = _ffn_bwd(dc, conv_w, w_up, x1, g_ffn_pre, dy)
    d_w_up = _matmul_tn(h2, du, N_SHARD, "grad_w_up")
    d_w_down = _matmul_tn(yv, df, 1, "grad_w_down")[0]

    d_mixed, d_pool, d_g_mix_post, d_attn, delta = _mix_out_bwd(dx1, mixed, g_mix_post, w_out, attn)
    d_w_out = _matmul_tn(cat, d_mixed, 1, "grad_w_out")[0]
    d_pool_in, d_pool_w, d_pool_scale = _pool_bwd(pool_in, d_pool, pool_w, pool_scale)
    dqkv = [_attn_bwd(*qkv[i], d_attn[i], lse[i], delta[i], d) for i, d in enumerate(DILATIONS)]
    d_proj, grad_x, d_g_mix_pre = _mix_in_bwd(dqkv, d_pool_in, w_in, x, g_mix_pre, dx1)
    d_w_in = _matmul_tn(h1, d_proj, N_SHARD, "grad_w_in")

    grads = dict(g_mix_pre=d_g_mix_pre, w_in=d_w_in, pool_w=d_pool_w, pool_scale=d_pool_scale, w_out=d_w_out,
                 g_mix_post=d_g_mix_post, g_ffn_pre=d_g_ffn_pre, w_up=d_w_up, conv_w=d_conv_w, conv_b=d_conv_b,
                 w_down=d_w_down, g_ffn_post=d_g_ffn_post)
    return loss, grad_x, grads


ANY = pl.BlockSpec(memory_space=pl.ANY)


def _position():
    x, y, c = lax.axis_index("x"), lax.axis_index("y"), lax.axis_index("c")
    chips = [(1 - x, y), (x, 1 - y), (1 - x, 1 - y)]
    return x, y, c, chips


def _remote(src, dst, send_sem, recv_sem, to):
    return pltpu.make_async_remote_copy(src_ref=src, dst_ref=dst, send_sem=send_sem, recv_sem=recv_sem,
                                        device_id=to, device_id_type=MESH)


def _cast_bf16(w, shard_arr, name):
    R, C = w.shape
    tr = R // 2

    def body(s_ref, w_ref, o_ref):
        o_ref[0] = w_ref[...].astype(BF16)

    return pl.pallas_call(
        body, name=name,
        grid_spec=pltpu.PrefetchScalarGridSpec(
            num_scalar_prefetch=1, grid=(2,),
            in_specs=[pl.BlockSpec((tr, C), lambda i, s_ref: (i, 0))],
            out_specs=pl.BlockSpec((1, tr, C), lambda i, s_ref: (s_ref[0], i, 0))),
        out_shape=jax.ShapeDtypeStruct((N_SHARD, R, C), BF16),
        compiler_params=_params(("parallel",)))(shard_arr, w)


def _gather_weights(bufs):
    n = len(bufs) - 1

    def body(*refs):
        outs, cw_out = refs[n + 1:2 * n + 1], refs[2 * n + 1]
        ici_send, ici_recv, d2d_send, d2d_recv = refs[2 * n + 2:]
        x, y, c, chips = _position()
        s = 2 * x + y
        sibling = (x, y, 1 - c)

        def half(a, shard, h):
            rows = outs[a].shape[1] // 2
            return outs[a].at[shard, pl.ds(h * rows, rows), :]

        sends = []
        for a in range(n):
            for j, (px, py) in enumerate(chips):
                sends.append(_remote(half(a, s, c), half(a, s, c),
                                     ici_send.at[3 * a + j], ici_recv.at[3 * a + j], (px, py, c)))
        for j, (px, py) in enumerate(chips):
            sends.append(_remote(cw_out.at[s], cw_out.at[s], ici_send.at[3 * n + j], ici_recv.at[3 * n + j], (px, py, c)))
        for cp in sends:
            cp.start()
        passed = []
        for a in range(n):
            for j, (px, py) in enumerate(chips):
                sj = 2 * px + py
                got = half(a, sj, c)
                _remote(got, got, ici_send.at[3 * a + j], ici_recv.at[3 * a + j], (px, py, c)).wait_recv()
                fwd = _remote(got, got, d2d_send.at[3 * a + j], d2d_recv.at[3 * a + j], sibling)
                fwd.start()
                passed.append(fwd)
        for j, (px, py) in enumerate(chips):
            got = cw_out.at[2 * px + py]
            _remote(got, got, ici_send.at[3 * n + j], ici_recv.at[3 * n + j], (px, py, c)).wait_recv()
        for a in range(n):
            for j, (px, py) in enumerate(chips):
                got = half(a, 2 * px + py, 1 - c)
                _remote(got, got, d2d_send.at[3 * a + j], d2d_recv.at[3 * a + j], sibling).wait_recv()
        for cp in sends + passed:
            cp.wait_send()

    return pl.pallas_call(
        body, name="gather_weights",
        in_specs=[ANY] * (n + 1), out_specs=[ANY] * (n + 1),
        out_shape=[jax.ShapeDtypeStruct(b.shape, b.dtype) for b in bufs],
        input_output_aliases={i: i for i in range(n + 1)},
        scratch_shapes=[pltpu.SemaphoreType.DMA((3 * n + 3,)), pltpu.SemaphoreType.DMA((3 * n + 3,)),
                        pltpu.SemaphoreType.DMA((3 * n,)), pltpu.SemaphoreType.DMA((3 * n,))],
        compiler_params=pltpu.CompilerParams(has_side_effects=True),
    )(*bufs)


def _swap_halves(grads):
    n = len(grads)

    def body(*refs):
        ins, outs, send_sem, recv_sem = refs[:n], refs[n:2 * n], refs[2 * n], refs[2 * n + 1]
        x, y, c, _ = _position()
        copies = []
        for a in range(n):
            rows = ins[a].shape[1] // 2
            copies.append(_remote(ins[a].at[:, pl.ds((1 - c) * rows, rows), :], outs[a],
                                  send_sem.at[a], recv_sem.at[a], (x, y, 1 - c)))
        for cp in copies:
            cp.start()
        for cp in copies:
            cp.wait()

    return pl.pallas_call(
        body, name="swap_grad_halves",
        in_specs=[ANY] * n, out_specs=[ANY] * n,
        out_shape=[jax.ShapeDtypeStruct((g.shape[0], g.shape[1] // 2, g.shape[2]), F32) for g in grads],
        scratch_shapes=[pltpu.SemaphoreType.DMA((n,)), pltpu.SemaphoreType.DMA((n,))],
        compiler_params=pltpu.CompilerParams(has_side_effects=True),
    )(*grads)


def _pair_sum(g, got, c_arr, name):
    n_sh, R, C = g.shape
    rows = R // 2

    def body(c_ref, g_ref, r_ref, f_ref, b_ref):
        t = g_ref[...] + r_ref[...]
        f_ref[...] = t
        b_ref[...] = t.astype(BF16)

    blk = pl.BlockSpec((1, rows, C), lambda i, c_ref: (i, 0, 0))
    return pl.pallas_call(
        body, name=name,
        grid_spec=pltpu.PrefetchScalarGridSpec(
            num_scalar_prefetch=1, grid=(n_sh,),
            in_specs=[pl.BlockSpec((1, rows, C), lambda i, c_ref: (i, c_ref[0], 0)), blk],
            out_specs=[blk, blk]),
        out_shape=[jax.ShapeDtypeStruct((n_sh, rows, C), F32), jax.ShapeDtypeStruct((n_sh, rows, C), BF16)],
        compiler_params=_params(("parallel",)),
    )(c_arr, g, got)


def _scatter_grads(sums_bf16, small_all):
    n = len(sums_bf16)

    def body(*refs):
        b_ins = refs[:n]
        recvs, sm = refs[n + 1:2 * n + 1], refs[2 * n + 1]
        ici_send, ici_recv, sm_send, sm_recv = refs[2 * n + 2:]
        x, y, c, chips = _position()
        me = 4 * x + 2 * y + c
        copies = []
        for a in range(n):
            for j, (px, py) in enumerate(chips):
                copies.append(_remote(b_ins[a].at[2 * px + py], recvs[a].at[j],
                                      ici_send.at[3 * a + j], ici_recv.at[3 * a + j], (px, py, c)))
        for k in range(1, 8):
            peer = (x ^ (k >> 2), y ^ ((k >> 1) & 1), c ^ (k & 1))
            copies.append(_remote(sm.at[me], sm.at[me], sm_send.at[k - 1], sm_recv.at[k - 1], peer))
        for cp in copies:
            cp.start()
        for cp in copies:
            cp.wait_send()
        for a in range(n):
            for j, (px, py) in enumerate(chips):
                _remote(recvs[a].at[j], recvs[a].at[j], ici_send.at[3 * a + j], ici_recv.at[3 * a + j],
                        (px, py, c)).wait_recv()
        for k in range(1, 8):
            peer = (x ^ (k >> 2), y ^ ((k >> 1) & 1), c ^ (k & 1))
            theirs = sm.at[4 * peer[0] + 2 * peer[1] + peer[2]]
            _remote(theirs, theirs, sm_send.at[k - 1], sm_recv.at[k - 1], peer).wait_recv()

    out = pl.pallas_call(
        body, name="scatter_grads",
        in_specs=[ANY] * (n + 1), out_specs=[ANY] * (n + 1),
        out_shape=[jax.ShapeDtypeStruct((3,) + b.shape[1:], BF16) for b in sums_bf16]
        + [jax.ShapeDtypeStruct(small_all.shape, F32)],
        input_output_aliases={n: n},
        scratch_shapes=[pltpu.SemaphoreType.DMA((3 * n,)), pltpu.SemaphoreType.DMA((3 * n,)),
                        pltpu.SemaphoreType.DMA((7,)), pltpu.SemaphoreType.DMA((7,))],
        compiler_params=pltpu.CompilerParams(has_side_effects=True),
    )(*sums_bf16, small_all)
    return out[:n], out[n]


def _shard_sum(sums_f32, recv, shard_arr, c_arr, name):
    _, rows, C = sums_f32.shape

    def body(s_ref, c_ref, o_ref, r_ref, t_ref):
        t_ref[...] = ((o_ref[0] + r_ref[0].astype(F32)) + r_ref[1].astype(F32)) + r_ref[2].astype(F32)

    return pl.pallas_call(
        body, name=name,
        grid_spec=pltpu.PrefetchScalarGridSpec(
            num_scalar_prefetch=2, grid=(1,),
            in_specs=[pl.BlockSpec((1, rows, C), lambda i, s_ref, c_ref: (s_ref[0], 0, 0)),
                      pl.BlockSpec((3, rows, C), lambda i, s_ref, c_ref: (0, 0, 0))],
            out_specs=pl.BlockSpec((rows, C), lambda i, s_ref, c_ref: (c_ref[0], 0))),
        out_shape=jax.ShapeDtypeStruct((2 * rows, C), F32),
        compiler_params=_params(("arbitrary",)),
    )(shard_arr, c_arr, sums_f32, recv)


def _join_halves(bufs):
    n = len(bufs)

    def body(*refs):
        outs, send_sem, recv_sem = refs[n:2 * n], refs[2 * n], refs[2 * n + 1]
        x, y, c, _ = _position()
        copies = []
        for a in range(n):
            rows = outs[a].shape[0] // 2
            mine = outs[a].at[pl.ds(c * rows, rows), :]
            copies.append(_remote(mine, mine, send_sem.at[a], recv_sem.at[a], (x, y, 1 - c)))
        for cp in copies:
            cp.start()
        for a, cp in enumerate(copies):
            cp.wait_send()
            rows = outs[a].shape[0] // 2
            theirs = outs[a].at[pl.ds((1 - c) * rows, rows), :]
            _remote(theirs, theirs, send_sem.at[a], recv_sem.at[a], (x, y, 1 - c)).wait_recv()

    return pl.pallas_call(
        body, name="join_grad_halves",
        in_specs=[ANY] * n, out_specs=[ANY] * n,
        out_shape=[jax.ShapeDtypeStruct(b.shape, F32) for b in bufs],
        input_output_aliases={i: i for i in range(n)},
        scratch_shapes=[pltpu.SemaphoreType.DMA((n,)), pltpu.SemaphoreType.DMA((n,))],
        compiler_params=pltpu.CompilerParams(has_side_effects=True),
    )(*bufs)


def _small_sum(parts):
    _, R, C = parts.shape

    def body(p_ref, o_ref):
        t = p_ref[0]
        for k in range(1, 8):
            t = t + p_ref[k]
        o_ref[...] = t

    return pl.pallas_call(
        body, name="small_grad_sum", grid=(1,),
        in_specs=[pl.BlockSpec((8, R, C), lambda i: (0, 0, 0))], out_specs=pl.BlockSpec((R, C), lambda i: (0, 0)),
        out_shape=jax.ShapeDtypeStruct((R, C), F32), compiler_params=_params(("arbitrary",)),
    )(parts)


def _adamw_math(w, g, m, v):
    m = ADAM_B1 * m + (1.0 - ADAM_B1) * g
    v = ADAM_B2 * v + (1.0 - ADAM_B2) * (g * g)
    m_hat = m / (1.0 - ADAM_B1 ** ADAM_STEP)
    v_hat = v / (1.0 - ADAM_B2 ** ADAM_STEP)
    delta = -ADAM_LR * (m_hat / (jnp.sqrt(v_hat) + ADAM_EPS) + ADAM_WD * w)
    return delta, m, v


def _adamw_big(w, g, m, v, name):
    R, C = w.shape
    tr = R // 4

    def body(w_ref, g_ref, m_ref, v_ref, d_ref, nm_ref, nv_ref):
        d_ref[...], nm_ref[...], nv_ref[...] = _adamw_math(w_ref[...], g_ref[...], m_ref[...], v_ref[...])

    blk = pl.BlockSpec((tr, C), lambda i: (i, 0))
    return pl.pallas_call(
        body, name=name, grid=(4,), in_specs=[blk] * 4, out_specs=[blk] * 3,
        out_shape=[jax.ShapeDtypeStruct((R, C), F32)] * 3, compiler_params=_params(("parallel",)),
    )(w, g, m, v)


def _adamw_small(ws, gs, ms, vs):
    n = len(ws)

    def body(*refs):
        for a in range(n):
            w, g, m, v = (refs[k * n + a][...] for k in range(4))
            d, nm, nv = _adamw_math(w, g, m, v)
            refs[4 * n + a][...] = d
            refs[5 * n + a][...] = nm
            refs[6 * n + a][...] = nv

    shapes = [jax.ShapeDtypeStruct(w.shape, F32) for w in ws]
    out = pl.pallas_call(body, name="adamw_small", out_shape=shapes * 3)(*ws, *gs, *ms, *vs)
    return out[:n], out[n:2 * n], out[2 * n:]


BIG = ("w_in", "w_out", "w_up", "w_down")
SMALL = ("g_mix_pre", "pool_w", "pool_scale", "g_mix_post", "g_ffn_pre", "conv_b", "g_ffn_post", "conv_w")
ORDER = ("g_mix_pre", "w_in", "pool_w", "pool_scale", "w_out", "g_mix_post", "g_ffn_pre", "w_up", "conv_w", "conv_b",
         "w_down", "g_ffn_post")


def kernel(x, g_mix_pre, w_in, pool_w, pool_scale, w_out, g_mix_post, g_ffn_pre, w_up, conv_w, conv_b, w_down, g_ffn_post, loss_target, m_g_mix_pre, m_w_in, m_pool_w, m_pool_scale, m_w_out, m_g_mix_post, m_g_ffn_pre, m_w_up, m_conv_w, m_conv_b, m_w_down, m_g_ffn_post, v_g_mix_pre, v_w_in, v_pool_w, v_pool_scale, v_w_out, v_g_mix_post, v_g_ffn_pre, v_w_up, v_conv_w, v_conv_b, v_w_down, v_g_ffn_post):
    args = dict(locals())
    W = {n: args[n][0] for n in ORDER}
    M = {n: args["m_" + n][0] for n in ORDER}
    V = {n: args["v_" + n][0] for n in ORDER}
    for d in (W, M, V):
        d["pool_w"] = d["pool_w"].reshape(-1, POOL_GROUP)
        for n in ("g_mix_pre", "pool_scale", "g_mix_post", "g_ffn_pre", "conv_b", "g_ffn_post"):
            d[n] = d[n].reshape(1, -1)
    CW = W["w_up"].shape[1]
    c_arr = lax.axis_index("c").astype(jnp.int32).reshape(1)
    shard = 2 * lax.axis_index("x") + lax.axis_index("y")
    shard_arr = shard.astype(jnp.int32).reshape(1)
    device = 2 * shard + lax.axis_index("c")

    conv_w_slots = lax.dynamic_update_index_in_dim(jnp.zeros((N_SHARD,) + W["conv_w"].shape, F32), W["conv_w"], shard, 0)
    gathered = _gather_weights([_cast_bf16(W[n], shard_arr, "cast_" + n) for n in BIG] + [conv_w_slots])
    w_in_g, w_out_g, w_up_g, w_down_g, conv_w_g = gathered
    D = w_in_g.shape[1]
    conv_w_full = conv_w_g.transpose(1, 0, 2).reshape(CONV_WIDTH, 1, N_SHARD * CW)

    loss, grad_x, G = _local_step(
        x[0], loss_target[0], W["g_mix_pre"], w_in_g, W["pool_w"].reshape(-1, POOL_GROUP, POOL_GROUP), W["pool_scale"],
        w_out_g.reshape(D, D), W["g_mix_post"], W["g_ffn_pre"], w_up_g, conv_w_full, W["conv_b"],
        w_down_g.reshape(2 * CW, D), W["g_ffn_post"])

    partial = [G["w_in"], G["w_out"].reshape(N_SHARD, D // N_SHARD, D), G["w_up"],
               G["w_down"].reshape(N_SHARD, 2 * CW // N_SHARD, D)]
    from_sibling = _swap_halves(partial)
    sums = [_pair_sum(g, r, c_arr, "pair_sum_" + n) for g, r, n in zip(partial, from_sibling, BIG)]
    G["conv_w"] = G["conv_w"].reshape(CONV_WIDTH, N_SHARD, CW).transpose(1, 0, 2)
    G["pool_w"] = G["pool_w"].reshape(-1, POOL_GROUP)
    small = jnp.concatenate([G[n].reshape(-1, LANES) for n in SMALL], axis=0)
    small_slots = lax.dynamic_update_index_in_dim(jnp.zeros((8,) + small.shape, F32), small, device, 0)
    recvs, small_all = _scatter_grads([s[1] for s in sums], small_slots)
    halves = [_shard_sum(s[0], r, shard_arr, c_arr, "shard_sum_" + n) for s, r, n in zip(sums, recvs, BIG)]
    full = dict(zip(BIG, _join_halves(halves)))

    small_total = _small_sum(small_all)
    row = 0
    for n in SMALL:
        rows = G[n].size // LANES
        g = small_total[row:row + rows]
        row += rows
        if n == "conv_w":
            g = lax.dynamic_slice_in_dim(g.reshape(N_SHARD, CONV_WIDTH, CW), shard, 1, axis=0)[0]
        full[n] = g.reshape(W[n].shape)

    delta, new_m, new_v = {}, {}, {}
    for n in BIG:
        delta[n], new_m[n], new_v[n] = _adamw_big(W[n], full[n], M[n], V[n], "adamw_" + n)
    ds, nms, nvs = _adamw_small([W[n] for n in SMALL], [full[n] for n in SMALL], [M[n] for n in SMALL],
                                [V[n] for n in SMALL])
    for n, d, nm, nv in zip(SMALL, ds, nms, nvs):
        delta[n], new_m[n], new_v[n] = d, nm, nv

    loss = lax.psum(loss[0, 0], ("x", "y", "c"))
    shaped = lambda d: [d[n].reshape(args[n].shape) for n in ORDER]
    return (loss, grad_x[None], *shaped(full), *shaped(delta), *shaped(new_m), *shaped(new_v))
```

```python
import functools

import jax
import jax.numpy as jnp
from jax import lax
from jax.experimental import pallas as pl
from jax.experimental.pallas import tpu as pltpu

F32 = jnp.float32
BF16 = jnp.bfloat16

RMS_EPS = 1e-6
NEG_INF = -1e30
N_HEADS = 8
HEAD_DIM = 64
ATTN_WIDTH = N_HEADS * HEAD_DIM
ATTN_SCALE = HEAD_DIM ** -0.5
ATTN_BLOCK = 128
DILATIONS = (1, 4, 16)
POOL_WINDOWS = (2, 4, 8, 16)
POOL_GROUP = 128
POOL_WIDTH = POOL_GROUP * len(POOL_WINDOWS)
POOL_HALO = 16
CONV_WIDTH = 3
CONV_HALO = 8
N_SHARD = 4
LANES = 128

ADAM_LR = 0.001
ADAM_B1 = 0.9
ADAM_B2 = 0.999
ADAM_EPS = 1e-08
ADAM_WD = 0.01
ADAM_STEP = 10

VMEM_LIMIT = 60 * 1024 * 1024
MESH = pl.DeviceIdType.MESH
NT = (((1,), (1,)), ((), ()))
TN = (((0,), (0,)), ((), ()))


def _params(sem, vmem=None):
    return pltpu.CompilerParams(dimension_semantics=sem, vmem_limit_bytes=vmem)


def _const_spec(shape):
    zeros = (0,) * len(shape)
    return pl.BlockSpec(shape, lambda *_: zeros, pipeline_mode=pl.Buffered(1))


def _dot(a, b):
    return jnp.dot(a, b, preferred_element_type=F32)


def _dot_nt(a, b):
    return lax.dot_general(a, b, NT, preferred_element_type=F32)


def _dot_tn(a, b):
    return lax.dot_general(a, b, TN, preferred_element_type=F32)


def _rms_stats(x):
    r = lax.rsqrt(jnp.mean(x * x, axis=-1, keepdims=True) + RMS_EPS)
    return x * r, r


def _rms_bwd(dy, n, r, g):
    dg = jnp.sum(dy * n, axis=0, keepdims=True)
    dn = dy * g
    dx = r * (dn - n * jnp.mean(dn * n, axis=-1, keepdims=True))
    return dx, dg


def _gelu_tanh(g):
    k = 0.7978845608028654
    a = k * (g + 0.044715 * (g * g * g))
    t = jnp.tanh(a)
    gelu = 0.5 * g * (1.0 + t)
    dgelu = 0.5 * (1.0 + t) + 0.5 * g * (1.0 - t * t) * k * (1.0 + 3.0 * 0.044715 * (g * g))
    return gelu, dgelu


def _residue_shape(S, d, dtype):
    return jax.ShapeDtypeStruct((S // d, d * ATTN_WIDTH), dtype)


def _residue_spec(TM, d):
    return pl.BlockSpec((TM // d, d * ATTN_WIDTH), lambda i: (i, 0))


def _token_scratch(TM):
    return [pltpu.VMEM((TM, LANES), F32)] * (ATTN_WIDTH // LANES)


def _put_tokens(dst_s, val):
    for cb, chunk in enumerate(dst_s):
        chunk[...] = val[:, cb * LANES:(cb + 1) * LANES]


def _get_tokens(src_s):
    return jnp.concatenate([chunk[...] for chunk in src_s], axis=1)


def _to_residue(val, src_s, out_ref, d, dtype):
    if d == 1:
        out_ref[...] = val.astype(dtype)
        return
    rows = src_s[0].shape[0]
    for r in range(d):
        for cb, chunk in enumerate(src_s):
            col = r * ATTN_WIDTH + cb * LANES
            out_ref[:, col:col + LANES] = chunk[pl.ds(r, rows // d, stride=d), :].astype(dtype)


def _from_residue(in_ref, dst_s, d):
    if d == 1:
        return in_ref[...].astype(F32)
    rows = dst_s[0].shape[0]
    for r in range(d):
        for cb, chunk in enumerate(dst_s):
            col = r * ATTN_WIDTH + cb * LANES
            chunk[pl.ds(r, rows // d, stride=d), :] = in_ref[:, col:col + LANES].astype(F32)
    return _get_tokens(dst_s)


def _mix_in_fwd(x, g_pre, w_in):
    S, D = x.shape
    TM = 512

    def body(x_ref, g_ref, w_ref, *refs):
        qkv_refs, p_ref, h_ref, t_s = refs[:9], refs[9], refs[10], refs[11:]
        n, _ = _rms_stats(x_ref[...])
        hb = (n * g_ref[...]).astype(BF16)
        h_ref[...] = hb
        for a in range(3):
            res = _dot(hb, w_ref[a])
            if a == 0:
                res = res * ATTN_SCALE
            _put_tokens(t_s, res)
            for i, d in enumerate(DILATIONS):
                _to_residue(res, t_s, qkv_refs[3 * i + a], d, BF16)
        p_ref[...] = _dot(hb, w_ref[3])

    row = lambda w: pl.BlockSpec((TM, w), lambda i: (i, 0))
    out = pl.pallas_call(
        body, name="mix_in_fwd", grid=(S // TM,),
        in_specs=[row(D), _const_spec((1, D)), _const_spec(w_in.shape)],
        out_specs=[_residue_spec(TM, d) for d in DILATIONS for _ in range(3)] + [row(POOL_WIDTH), row(D)],
        out_shape=[_residue_shape(S, d, BF16) for d in DILATIONS for _ in range(3)]
        + [jax.ShapeDtypeStruct((S, POOL_WIDTH), F32), jax.ShapeDtypeStruct((S, D), BF16)],
        scratch_shapes=_token_scratch(TM),
        compiler_params=_params(("parallel",), VMEM_LIMIT),
    )(x, g_pre, w_in)
    return [out[0:3], out[3:6], out[6:9]], out[9], out[10]


def _band_mask(n):
    qi = lax.broadcasted_iota(jnp.int32, (ATTN_BLOCK, 2 * ATTN_BLOCK), 0)
    ki = lax.broadcasted_iota(jnp.int32, (ATTN_BLOCK, 2 * ATTN_BLOCK), 1)
    dist = qi + ATTN_BLOCK - ki
    return (dist >= 0) & (dist <= ATTN_BLOCK) & ((ki >= ATTN_BLOCK) | (n > 0))


def _first_head_lanes():
    return lax.broadcasted_iota(jnp.int32, (1, LANES), 1) < HEAD_DIM


def _stack_heads(pair, first):
    zero = jnp.zeros_like(pair)
    return jnp.concatenate([jnp.where(first, pair, zero), jnp.where(first, zero, pair)], axis=0)


def _unstack_heads(stacked, first):
    return jnp.where(first, stacked[:ATTN_BLOCK], stacked[ATTN_BLOCK:])


def _attn_fwd(q, k, v, d):
    L = q.shape[0]
    nb = L // ATTN_BLOCK

    def body(q_ref, kp_ref, kc_ref, vp_ref, vc_ref, o_ref, lse_ref):
        valid = _band_mask(pl.program_id(1))
        valid2 = jnp.concatenate([valid, valid], axis=0)
        first = _first_head_lanes()
        for hp in range(N_HEADS // 2):
            sl = slice(hp * LANES, (hp + 1) * LANES)
            kk = jnp.concatenate([kp_ref[:, sl], kc_ref[:, sl]], axis=0)
            vv = jnp.concatenate([vp_ref[:, sl], vc_ref[:, sl]], axis=0)
            s = jnp.where(valid2, _dot_nt(_stack_heads(q_ref[:, sl], first), kk), NEG_INF)
            m = jnp.max(s, axis=-1, keepdims=True)
            p = jnp.exp(s - m)
            den = jnp.sum(p, axis=-1, keepdims=True)
            o_ref[:, sl] = _unstack_heads(_dot(p.astype(BF16), vv) / den, first)
            lse_ref[:, sl] = _unstack_heads(m + jnp.log(den), first)

    cur = pl.BlockSpec((ATTN_BLOCK, ATTN_WIDTH), lambda r, n: (n, r))
    prev = pl.BlockSpec((ATTN_BLOCK, ATTN_WIDTH), lambda r, n: (jnp.maximum(n - 1, 0), r))
    return pl.pallas_call(
        body, name=f"attn_fwd_d{d}", grid=(d, nb),
        in_specs=[cur, prev, cur, prev, cur],
        out_specs=[cur, cur],
        out_shape=[jax.ShapeDtypeStruct((L, d * ATTN_WIDTH), F32)] * 2,
        compiler_params=_params(("parallel", "parallel")),
    )(q, k, k, v, v)


def _attn_mix(outs, lses):
    S = outs[0].shape[0]
    TM = 512
    n = len(DILATIONS)

    def body(*refs):
        o_refs, l_refs, attn_ref, lse_refs, t_s = refs[:n], refs[n:2 * n], refs[2 * n], refs[2 * n + 1:3 * n + 1], refs[3 * n + 1:]
        os = [_from_residue(o_refs[i], t_s, d) for i, d in enumerate(DILATIONS)]
        ls = [_from_residue(l_refs[i], t_s, d) for i, d in enumerate(DILATIONS)]
        m = jnp.maximum(jnp.maximum(ls[0], ls[1]), ls[2])
        es = [jnp.exp(l - m) for l in ls]
        den = es[0] + es[1] + es[2]
        attn_ref[...] = (es[0] * os[0] + es[1] * os[1] + es[2] * os[2]) / den
        lse = m + jnp.log(den)
        _put_tokens(t_s, lse)
        for i, d in enumerate(DILATIONS):
            _to_residue(lse, t_s, lse_refs[i], d, F32)

    specs = [_residue_spec(TM, d) for d in DILATIONS]
    out = pl.pallas_call(
        body, name="attn_mix", grid=(S // TM,),
        in_specs=specs * 2, out_specs=[specs[0]] + specs,
        out_shape=[jax.ShapeDtypeStruct((S, ATTN_WIDTH), F32)] + [_residue_shape(S, d, F32) for d in DILATIONS],
        scratch_shapes=_token_scratch(TM),
        compiler_params=_params(("parallel",)),
    )(*outs, *lses)
    return out[0], out[1:]


def _pool_counts(first_row, rows, w):
    t = first_row + lax.broadcasted_iota(jnp.int32, (rows, 1), 0)
    return jnp.minimum(t + 1, w).astype(F32)


def _trailing_sums(xe, w):
    s, k = xe, 1
    while k < w:
        s = s + pltpu.roll(s, k, 0)
        k *= 2
    return s


def _leading_sums(xe, w):
    rows = xe.shape[0]
    s, k = xe, 1
    while k < w:
        s = s + pltpu.roll(s, rows - k, 0)
        k *= 2
    return s


def _pooled_groups(halo, cur, first_row):
    TM = cur.shape[0]
    xe = jnp.concatenate([halo, cur], axis=0)
    out = []
    for g, w in enumerate(POOL_WINDOWS):
        a = xe[:, g * POOL_GROUP:(g + 1) * POOL_GROUP]
        s = _trailing_sums(a, w)[POOL_HALO:]
        out.append(s / _pool_counts(first_row, TM, w) - a[POOL_HALO:])
    return out


def _pool_fwd(pool_in, pool_w, pool_scale):
    S = pool_in.shape[0]
    TM = 512
    HB = TM // POOL_HALO

    def body(cur_ref, halo_ref, w_ref, sc_ref, y_ref):
        i = pl.program_id(0)
        halo = jnp.where(i > 0, halo_ref[...], 0.0)
        pooled = _pooled_groups(halo, cur_ref[...], i * TM)
        for g in range(len(POOL_WINDOWS)):
            sl = slice(g * POOL_GROUP, (g + 1) * POOL_GROUP)
            y = _dot(pooled[g].astype(BF16), w_ref[g].astype(BF16)) * sc_ref[:, sl]
            y_ref[:, sl] = y.astype(BF16)

    return pl.pallas_call(
        body, name="pool_fwd", grid=(S // TM,),
        in_specs=[pl.BlockSpec((TM, POOL_WIDTH), lambda i: (i, 0)),
                  pl.BlockSpec((POOL_HALO, POOL_WIDTH), lambda i: (jnp.maximum(i * HB - 1, 0), 0)),
                  _const_spec(pool_w.shape), _const_spec((1, POOL_WIDTH))],
        out_specs=pl.BlockSpec((TM, POOL_WIDTH), lambda i: (i, 0)),
        out_shape=jax.ShapeDtypeStruct((S, POOL_WIDTH), BF16),
        compiler_params=_params(("parallel",)),
    )(pool_in, pool_in, pool_w, pool_scale)


def _mix_out_fwd(attn, pool, w_out, x, g_post, g_ffn_pre):
    S, D = x.shape
    TM = 512

    def body(a_ref, p_ref, w_ref, x_ref, gp_ref, gf_ref, mixed_ref, x1_ref, h2_ref, cat_ref):
        ab = a_ref[...].astype(BF16)
        cat_ref[:, :ATTN_WIDTH] = ab
        cat_ref[:, ATTN_WIDTH:] = p_ref[...]
        mixed = _dot(ab, w_ref[:ATTN_WIDTH, :]) + _dot(p_ref[...], w_ref[ATTN_WIDTH:, :])
        mixed_ref[...] = mixed
        n, _ = _rms_stats(mixed)
        x1 = x_ref[...] + n * gp_ref[...]
        x1_ref[...] = x1
        n2, _ = _rms_stats(x1)
        h2_ref[...] = (n2 * gf_ref[...]).astype(BF16)

    row = lambda w: pl.BlockSpec((TM, w), lambda i: (i, 0))
    return pl.pallas_call(
        body, name="mix_out_fwd", grid=(S // TM,),
        in_specs=[row(ATTN_WIDTH), row(POOL_WIDTH), _const_spec(w_out.shape), row(D),
                  _const_spec((1, D)), _const_spec((1, D))],
        out_specs=[row(D), row(D), row(D), row(D)],
        out_shape=[jax.ShapeDtypeStruct((S, D), F32), jax.ShapeDtypeStruct((S, D), F32),
                   jax.ShapeDtypeStruct((S, D), BF16), jax.ShapeDtypeStruct((S, D), BF16)],
        compiler_params=_params(("parallel",), VMEM_LIMIT),
    )(attn, pool, w_out, x, g_post, g_ffn_pre)


def _ffn_fwd(h2, x1, target, w_up, w_down, conv_w, conv_b, g_post):
    S, D = x1.shape
    CW = w_up.shape[2]
    FF = 2 * CW
    TM = 256

    def body(h2_ref, x1_ref, t_ref, wu_ref, wd_ref, cw_ref, cb_ref, g_ref,
             u_ref, yv_ref, dy_ref, df_ref, dc_ref, loss_ref, dg_ref, dcb_ref, dcw_ref,
             ue_s, cg_s, cv_s):
        i = pl.program_id(0)

        @pl.when(i == 0)
        def _():
            loss_ref[...] = jnp.zeros_like(loss_ref)
            dg_ref[...] = jnp.zeros_like(dg_ref)
            dcb_ref[...] = jnp.zeros_like(dcb_ref)
            dcw_ref[...] = jnp.zeros_like(dcw_ref)
            ue_s[0:CONV_HALO, :] = jnp.zeros((CONV_HALO, 2 * FF), F32)

        @pl.when(i > 0)
        def _():
            ue_s[0:CONV_HALO, :] = ue_s[TM:TM + CONV_HALO, :]

        def shifted(cols, k):
            return pltpu.roll(ue_s[:, cols], k, 0)[CONV_HALO:]

        hb = h2_ref[...]
        f = jnp.zeros((TM, D), F32)
        for j in range(2):
            jc = slice(j * CW, (j + 1) * CW)
            for half, c_s in ((0, cg_s), (1, cv_s)):
                blk = 2 * half + j
                cols = slice(blk * CW, (blk + 1) * CW)
                uu = _dot(hb, wu_ref[blk])
                u_ref[:, cols] = uu.astype(BF16)
                ue_s[CONV_HALO:, cols] = uu
                c_s[:, jc] = (cb_ref[:, cols] + cw_ref[2, :, cols] * uu
                              + cw_ref[1, :, cols] * shifted(cols, 1)
                              + cw_ref[0, :, cols] * shifted(cols, 2))
            gelu, _ = _gelu_tanh(cg_s[:, jc])
            yvb = (gelu * cv_s[:, jc]).astype(BF16)
            yv_ref[:, jc] = yvb
            f = f + _dot(yvb, wd_ref[jc, :])

        n, r = _rms_stats(f)
        err = x1_ref[...] + n * g_ref[...] - t_ref[...]
        loss_ref[...] += 0.5 * jnp.sum(jnp.mean(err * err, axis=-1, keepdims=True), axis=0, keepdims=True)
        dy = err / D
        dy_ref[...] = dy
        df, dg = _rms_bwd(dy, n, r, g_ref[...])
        dg_ref[...] += dg
        dfb = df.astype(BF16)
        df_ref[...] = dfb

        for j in range(2):
            jc = slice(j * CW, (j + 1) * CW)
            dyv = _dot_nt(dfb, wd_ref[jc, :])
            cv = cv_s[:, jc]
            gelu, dgelu = _gelu_tanh(cg_s[:, jc])
            for half, dcv in ((0, dyv * cv * dgelu), (1, dyv * gelu)):
                blk = 2 * half + j
                cols = slice(blk * CW, (blk + 1) * CW)
                dc_ref[:, cols] = dcv.astype(BF16)
                dcb_ref[:, cols] += jnp.sum(dcv, axis=0, keepdims=True)
                dcw_ref[2, :, cols] += jnp.sum(dcv * ue_s[CONV_HALO:, cols], axis=0, keepdims=True)
                dcw_ref[1, :, cols] += jnp.sum(dcv * shifted(cols, 1), axis=0, keepdims=True)
                dcw_ref[0, :, cols] += jnp.sum(dcv * shifted(cols, 2), axis=0, keepdims=True)

    row = lambda w: pl.BlockSpec((TM, w), lambda i: (i, 0))
    acc = lambda shape: pl.BlockSpec(shape, lambda i: (0,) * len(shape))
    return pl.pallas_call(
        body, name="ffn_fwd", grid=(S // TM,),
        in_specs=[row(D), row(D), row(D), _const_spec(w_up.shape), _const_spec(w_down.shape),
                  _const_spec(conv_w.shape), _const_spec((1, 2 * FF)), _const_spec((1, D))],
        out_specs=[row(2 * FF), row(FF), row(D), row(D), row(2 * FF),
                   acc((1, 1)), acc((1, D)), acc((1, 2 * FF)), acc((CONV_WIDTH, 1, 2 * FF))],
        out_shape=[jax.ShapeDtypeStruct((S, 2 * FF), BF16), jax.ShapeDtypeStruct((S, FF), BF16),
                   jax.ShapeDtypeStruct((S, D), F32), jax.ShapeDtypeStruct((S, D), BF16),
                   jax.ShapeDtypeStruct((S, 2 * FF), BF16),
                   jax.ShapeDtypeStruct((1, 1), F32), jax.ShapeDtypeStruct((1, D), F32),
                   jax.ShapeDtypeStruct((1, 2 * FF), F32), jax.ShapeDtypeStruct((CONV_WIDTH, 1, 2 * FF), F32)],
        scratch_shapes=[pltpu.VMEM((TM + CONV_HALO, 2 * FF), F32), pltpu.VMEM((TM, FF), F32),
                        pltpu.VMEM((TM, FF), F32)],
        compiler_params=_params(("arbitrary",), VMEM_LIMIT),
    )(h2, x1, target, w_up, w_down, conv_w, conv_b, g_post)


def _ffn_bwd(dc, conv_w, w_up, x1, g_ffn_pre, dy):
    S, D = x1.shape
    CW = w_up.shape[2]
    F2 = 4 * CW
    TM = 256
    HB = TM // CONV_HALO
    last = S // CONV_HALO - 1
    n_tiles = S // TM

    def body(dc_ref, halo_ref, cw_ref, wu_ref, x1_ref, g_ref, dy_ref, du_ref, dx1_ref, dg_ref):
        i = pl.program_id(0)

        @pl.when(i == 0)
        def _():
            dg_ref[...] = jnp.zeros_like(dg_ref)

        keep = i < n_tiles - 1
        dh2 = jnp.zeros((TM, D), F32)
        for blk in range(N_SHARD):
            cols = slice(blk * CW, (blk + 1) * CW)
            halo = jnp.where(keep, halo_ref[:, cols].astype(F32), 0.0)
            dce = jnp.concatenate([dc_ref[:, cols].astype(F32), halo], axis=0)
            rows = TM + CONV_HALO
            du = (cw_ref[2, :, cols] * dce[:TM]
                  + cw_ref[1, :, cols] * pltpu.roll(dce, rows - 1, 0)[:TM]
                  + cw_ref[0, :, cols] * pltpu.roll(dce, rows - 2, 0)[:TM])
            dub = du.astype(BF16)
            du_ref[:, cols] = dub
            dh2 = dh2 + _dot_nt(dub, wu_ref[blk])
        n2, r2 = _rms_stats(x1_ref[...])
        dx, dg = _rms_bwd(dh2, n2, r2, g_ref[...])
        dg_ref[...] += dg
        dx1_ref[...] = dy_ref[...] + dx

    row = lambda w: pl.BlockSpec((TM, w), lambda i: (i, 0))
    return pl.pallas_call(
        body, name="ffn_bwd", grid=(S // TM,),
        in_specs=[row(F2), pl.BlockSpec((CONV_HALO, F2), lambda i: (jnp.minimum((i + 1) * HB, last), 0)),
                  _const_spec(conv_w.shape), _const_spec(w_up.shape), row(D), _const_spec((1, D)), row(D)],
        out_specs=[row(F2), row(D), pl.BlockSpec((1, D), lambda i: (0, 0))],
        out_shape=[jax.ShapeDtypeStruct((S, F2), BF16), jax.ShapeDtypeStruct((S, D), F32),
                   jax.ShapeDtypeStruct((1, D), F32)],
        compiler_params=_params(("arbitrary",), VMEM_LIMIT),
    )(dc, dc, conv_w, w_up, x1, g_ffn_pre, dy)


def _matmul_tn(a, b, n_blocks, name):
    S, M = a.shape
    N = b.shape[1]
    tn = N // n_blocks
    tm = M if M <= 1024 else M // 2
    tk = 2048
    nk = S // tk

    def body(a_ref, b_ref, o_ref):
        @pl.when(pl.program_id(2) == 0)
        def _():
            o_ref[...] = jnp.zeros_like(o_ref)
        o_ref[0] += _dot_tn(a_ref[...], b_ref[...])

    return pl.pallas_call(
        body, name=name, grid=(M // tm, n_blocks, nk),
        in_specs=[pl.BlockSpec((tk, tm), lambda i, j, k: (k, i)), pl.BlockSpec((tk, tn), lambda i, j, k: (k, j))],
        out_specs=pl.BlockSpec((1, tm, tn), lambda i, j, k: (j, i, 0)),
        out_shape=jax.ShapeDtypeStruct((n_blocks, M, tn), F32),
        compiler_params=_params(("parallel", "parallel", "arbitrary"), VMEM_LIMIT),
    )(a, b)


def _mix_out_bwd(dx1, mixed, g_post, w_out, attn):
    S, D = dx1.shape
    TM = 512

    nd = len(DILATIONS)

    def body(dx_ref, m_ref, g_ref, w_ref, a_ref, dm_ref, dp_ref, dg_ref, *refs):
        da_refs, dl_refs, t_s = refs[:nd], refs[nd:2 * nd], refs[2 * nd:]

        @pl.when(pl.program_id(0) == 0)
        def _():
            dg_ref[...] = jnp.zeros_like(dg_ref)

        n, r = _rms_stats(m_ref[...])
        dm, dg = _rms_bwd(dx_ref[...], n, r, g_ref[...])
        dg_ref[...] += dg
        dmb = dm.astype(BF16)
        dm_ref[...] = dmb
        da = _dot_nt(dmb, w_ref[:ATTN_WIDTH, :])
        _put_tokens(t_s, da)
        for i, d in enumerate(DILATIONS):
            _to_residue(da, t_s, da_refs[i], d, BF16)
        dp_ref[...] = _dot_nt(dmb, w_ref[ATTN_WIDTH:, :])
        prod = da * a_ref[...]
        hi = prod.astype(BF16)
        lo = (prod - hi.astype(F32)).astype(BF16)
        ri = lax.broadcasted_iota(jnp.int32, (ATTN_WIDTH, ATTN_WIDTH), 0) // HEAD_DIM
        ci = lax.broadcasted_iota(jnp.int32, (ATTN_WIDTH, ATTN_WIDTH), 1) // HEAD_DIM
        ones = (ri == ci).astype(BF16)
        delta = _dot(hi, ones) + _dot(lo, ones)
        _put_tokens(t_s, delta)
        for i, d in enumerate(DILATIONS):
            _to_residue(delta, t_s, dl_refs[i], d, F32)

    row = lambda w: pl.BlockSpec((TM, w), lambda i: (i, 0))
    specs = [_residue_spec(TM, d) for d in DILATIONS]
    out = pl.pallas_call(
        body, name="mix_out_bwd", grid=(S // TM,),
        in_specs=[row(D), row(D), _const_spec((1, D)), _const_spec(w_out.shape), row(ATTN_WIDTH)],
        out_specs=[row(D), row(POOL_WIDTH), pl.BlockSpec((1, D), lambda i: (0, 0))] + specs * 2,
        out_shape=[jax.ShapeDtypeStruct((S, D), BF16), jax.ShapeDtypeStruct((S, POOL_WIDTH), F32),
                   jax.ShapeDtypeStruct((1, D), F32)]
        + [_residue_shape(S, d, BF16) for d in DILATIONS] + [_residue_shape(S, d, F32) for d in DILATIONS],
        scratch_shapes=_token_scratch(TM),
        compiler_params=_params(("arbitrary",), VMEM_LIMIT),
    )(dx1, mixed, g_post, w_out, attn)
    return out[0], out[1], out[2], out[3:3 + nd], out[3 + nd:]


def _pool_bwd(pool_in, d_pool, pool_w, pool_scale):
    S = pool_in.shape[0]
    TM = 512
    HB = TM // POOL_HALO
    last = S // POOL_HALO - 1
    G = len(POOL_WINDOWS)

    def body(cur_ref, halo_ref, dcur_ref, dnext_ref, w_ref, sc_ref, dxin_ref, dw_ref, dsc_ref):
        i = pl.program_id(0)

        @pl.when(i == 0)
        def _():
            dw_ref[...] = jnp.zeros_like(dw_ref)
            dsc_ref[...] = jnp.zeros_like(dsc_ref)

        halo = jnp.where(i > 0, halo_ref[...], 0.0)
        pooled = _pooled_groups(halo, cur_ref[...], i * TM)
        dnext = jnp.where(i < S // TM - 1, dnext_ref[...], 0.0)
        dye = jnp.concatenate([dcur_ref[...], dnext], axis=0)
        for g, w in enumerate(POOL_WINDOWS):
            sl = slice(g * POOL_GROUP, (g + 1) * POOL_GROUP)
            wg = w_ref[g].astype(BF16)
            pb = pooled[g].astype(BF16)
            dsc_ref[:, sl] += jnp.sum(dye[:TM, sl] * _dot(pb, wg), axis=0, keepdims=True)
            dpre = (dye[:, sl] * sc_ref[:, sl]).astype(BF16)
            dw_ref[g] += _dot_tn(pb, dpre[:TM])
            dpooled = _dot_nt(dpre, wg)
            z = dpooled / _pool_counts(i * TM, TM + POOL_HALO, w)
            dxin_ref[:, sl] = (_leading_sums(z, w)[:TM] - dpooled[:TM]).astype(BF16)

    row = pl.BlockSpec((TM, POOL_WIDTH), lambda i: (i, 0))
    return pl.pallas_call(
        body, name="pool_bwd", grid=(S // TM,),
        in_specs=[row, pl.BlockSpec((POOL_HALO, POOL_WIDTH), lambda i: (jnp.maximum(i * HB - 1, 0), 0)),
                  row, pl.BlockSpec((POOL_HALO, POOL_WIDTH), lambda i: (jnp.minimum((i + 1) * HB, last), 0)),
                  _const_spec(pool_w.shape), _const_spec((1, POOL_WIDTH))],
        out_specs=[row, pl.BlockSpec((G, POOL_GROUP, POOL_GROUP), lambda i: (0, 0, 0)),
                   pl.BlockSpec((1, POOL_WIDTH), lambda i: (0, 0))],
        out_shape=[jax.ShapeDtypeStruct((S, POOL_WIDTH), BF16), jax.ShapeDtypeStruct((G, POOL_GROUP, POOL_GROUP), F32),
                   jax.ShapeDtypeStruct((1, POOL_WIDTH), F32)],
        compiler_params=_params(("arbitrary",)),
    )(pool_in, pool_in, d_pool, d_pool, pool_w, pool_scale)


def _attn_bwd(q, k, v, d_attn, lse, delta, d):
    L = q.shape[0]
    nb = L // ATTN_BLOCK

    def body(q_ref, kp_ref, kc_ref, vp_ref, vc_ref, do_ref, lse_ref, dl_ref, dq_ref, dk_ref, dv_ref, ck_s, cv_s):
        n = pl.program_id(1)

        @pl.when(n == 0)
        def _():
            ck_s[...] = jnp.zeros_like(ck_s)
            cv_s[...] = jnp.zeros_like(cv_s)

        @pl.when(n < nb)
        def _():
            valid = _band_mask(n)
            valid2 = jnp.concatenate([valid, valid], axis=0)
            first = _first_head_lanes()

            def stacked_column(ref, lane):
                return jnp.concatenate([ref[:, lane:lane + 1], ref[:, lane + HEAD_DIM:lane + HEAD_DIM + 1]], axis=0)

            for hp in range(N_HEADS // 2):
                sl = slice(hp * LANES, (hp + 1) * LANES)
                qq = _stack_heads(q_ref[:, sl], first)
                dd = _stack_heads(do_ref[:, sl], first)
                kk = jnp.concatenate([kp_ref[:, sl], kc_ref[:, sl]], axis=0)
                vv = jnp.concatenate([vp_ref[:, sl], vc_ref[:, sl]], axis=0)
                s = _dot_nt(qq, kk)
                p = jnp.where(valid2, jnp.exp(s - stacked_column(lse_ref, hp * LANES)), 0.0)
                dp = _dot_nt(dd, vv)
                ds = (p * (dp - stacked_column(dl_ref, hp * LANES))).astype(BF16)
                dq_ref[:, sl] = (_unstack_heads(_dot(ds, kk), first) * ATTN_SCALE).astype(BF16)
                dk = _dot_tn(ds, qq)
                dv = _dot_tn(p.astype(BF16), dd)
                dk_ref[:, sl] = (ck_s[:, sl] + dk[:ATTN_BLOCK]).astype(BF16)
                dv_ref[:, sl] = (cv_s[:, sl] + dv[:ATTN_BLOCK]).astype(BF16)
                ck_s[:, sl] = dk[ATTN_BLOCK:]
                cv_s[:, sl] = dv[ATTN_BLOCK:]

        @pl.when(n == nb)
        def _():
            dk_ref[...] = ck_s[...].astype(BF16)
            dv_ref[...] = cv_s[...].astype(BF16)

    blk = (ATTN_BLOCK, ATTN_WIDTH)
    cur = pl.BlockSpec(blk, lambda r, n: (jnp.minimum(n, nb - 1), r))
    prev = pl.BlockSpec(blk, lambda r, n: (jnp.maximum(jnp.minimum(n, nb - 1) - 1, 0), r))
    done = pl.BlockSpec(blk, lambda r, n: (jnp.maximum(n - 1, 0), r))
    return pl.pallas_call(
        body, name=f"attn_bwd_d{d}", grid=(d, nb + 1),
        in_specs=[cur, prev, cur, prev, cur, cur, cur, cur], out_specs=[cur, done, done],
        out_shape=[jax.ShapeDtypeStruct((L, d * ATTN_WIDTH), BF16)] * 3,
        scratch_shapes=[pltpu.VMEM(blk, F32), pltpu.VMEM(blk, F32)],
        compiler_params=_params(("parallel", "arbitrary")),
    )(q, k, k, v, v, d_attn, lse, delta)


def _mix_in_bwd(dqkv, d_pool_in, w_in, x, g_pre, dx1):
    S, D = x.shape
    TM = 512
    nd = len(DILATIONS)

    def body(*refs):
        g_refs = refs[:3 * nd]
        dpi_ref, w_ref, x_ref, g_ref, dx1_ref, dproj_ref, gx_ref, dg_ref = refs[3 * nd:3 * nd + 8]
        t_s = refs[3 * nd + 8:]

        @pl.when(pl.program_id(0) == 0)
        def _():
            dg_ref[...] = jnp.zeros_like(dg_ref)

        dh = jnp.zeros((TM, D), F32)
        for a in range(4):
            if a < 3:
                tot = g_refs[a][...].astype(F32)
                for i, d in enumerate(DILATIONS[1:]):
                    tot = tot + _from_residue(g_refs[3 * (i + 1) + a], t_s, d)
                db = tot.astype(BF16)
            else:
                db = dpi_ref[...]
            dproj_ref[:, a * ATTN_WIDTH:(a + 1) * ATTN_WIDTH] = db
            dh = dh + _dot_nt(db, w_ref[a])
        n, r = _rms_stats(x_ref[...])
        dx, dg = _rms_bwd(dh, n, r, g_ref[...])
        dg_ref[...] += dg
        gx_ref[...] = dx1_ref[...] + dx

    row = lambda w: pl.BlockSpec((TM, w), lambda i: (i, 0))
    return pl.pallas_call(
        body, name="mix_in_bwd", grid=(S // TM,),
        in_specs=[_residue_spec(TM, d) for d in DILATIONS for _ in range(3)]
        + [row(POOL_WIDTH), _const_spec(w_in.shape), row(D), _const_spec((1, D)), row(D)],
        out_specs=[row(4 * ATTN_WIDTH), row(D), pl.BlockSpec((1, D), lambda i: (0, 0))],
        out_shape=[jax.ShapeDtypeStruct((S, 4 * ATTN_WIDTH), BF16), jax.ShapeDtypeStruct((S, D), F32),
                   jax.ShapeDtypeStruct((1, D), F32)],
        scratch_shapes=_token_scratch(TM),
        compiler_params=_params(("arbitrary",), VMEM_LIMIT),
    )(*[g for gs in dqkv for g in gs], d_pool_in, w_in, x, g_pre, dx1)


def _local_step(x, target, g_mix_pre, w_in, pool_w, pool_scale, w_out, g_mix_post, g_ffn_pre,
                w_up, conv_w, conv_b, w_down, g_ffn_post):
    qkv, pool_in, h1 = _mix_in_fwd(x, g_mix_pre, w_in)
    outs, lses = zip(*[_attn_fwd(*qkv[i], d) for i, d in enumerate(DILATIONS)])
    attn, lse = _attn_mix(outs, lses)
    pool = _pool_fwd(pool_in, pool_w, pool_scale)
    mixed, x1, h2, cat = _mix_out_fwd(attn, pool, w_out, x, g_mix_post, g_ffn_pre)

    u, yv, dy, df, dc, loss, d_g_ffn_post, d_conv_b, d_conv_w = _ffn_fwd(
        h2, x1, target, w_up, w_down, conv_w, conv_b, g_ffn_post)
    del u
    du, dx1, d_g_ffn_pre = _ffn_bwd(dc, conv_w, w_up, x1, g_ffn_pre, dy)
    d_w_up = _matmul_tn(h2, du, N_SHARD, "grad_w_up")
    d_w_down = _matmul_tn(yv, df, 1, "grad_w_down")[0]

    d_mixed, d_pool, d_g_mix_post, d_attn, delta = _mix_out_bwd(dx1, mixed, g_mix_post, w_out, attn)
    d_w_out = _matmul_tn(cat, d_mixed, 1, "grad_w_out")[0]
    d_pool_in, d_pool_w, d_pool_scale = _pool_bwd(pool_in, d_pool, pool_w, pool_scale)
    dqkv = [_attn_bwd(*qkv[i], d_attn[i], lse[i], delta[i], d) for i, d in enumerate(DILATIONS)]
    d_proj, grad_x, d_g_mix_pre = _mix_in_bwd(dqkv, d_pool_in, w_in, x, g_mix_pre, dx1)
    d_w_in = _matmul_tn(h1, d_proj, N_SHARD, "grad_w_in")

    grads = dict(g_mix_pre=d_g_mix_pre, w_in=d_w_in, pool_w=d_pool_w, pool_scale=d_pool_scale, w_out=d_w_out,
                 g_mix_post=d_g_mix_post, g_ffn_pre=d_g_ffn_pre, w_up=d_w_up, conv_w=d_conv_w, conv_b=d_conv_b,
                 w_down=d_w_down, g_ffn_post=d_g_ffn_post)
    return loss, grad_x, grads


ANY = pl.BlockSpec(memory_space=pl.ANY)


def _position():
    x, y, c = lax.axis_index("x"), lax.axis_index("y"), lax.axis_index("c")
    chips = [(1 - x, y), (x, 1 - y), (1 - x, 1 - y)]
    return x, y, c, chips


def _remote(src, dst, send_sem, recv_sem, to):
    return pltpu.make_async_remote_copy(src_ref=src, dst_ref=dst, send_sem=send_sem, recv_sem=recv_sem,
                                        device_id=to, device_id_type=MESH)


def _cast_bf16(w, shard_arr, name):
    R, C = w.shape
    tr = R // 2

    def body(s_ref, w_ref, o_ref):
        o_ref[0] = w_ref[...].astype(BF16)

    return pl.pallas_call(
        body, name=name,
        grid_spec=pltpu.PrefetchScalarGridSpec(
            num_scalar_prefetch=1, grid=(2,),
            in_specs=[pl.BlockSpec((tr, C), lambda i, s_ref: (i, 0))],
            out_specs=pl.BlockSpec((1, tr, C), lambda i, s_ref: (s_ref[0], i, 0))),
        out_shape=jax.ShapeDtypeStruct((N_SHARD, R, C), BF16),
        compiler_params=_params(("parallel",)))(shard_arr, w)


def _gather_weights(bufs):
    n = len(bufs) - 1

    def body(*refs):
        outs, cw_out = refs[n + 1:2 * n + 1], refs[2 * n + 1]
        ici_send, ici_recv, d2d_send, d2d_recv = refs[2 * n + 2:]
        x, y, c, chips = _position()
        s = 2 * x + y
        sibling = (x, y, 1 - c)

        def half(a, shard, h):
            rows = outs[a].shape[1] // 2
            return outs[a].at[shard, pl.ds(h * rows, rows), :]

        sends = []
        for a in range(n):
            for j, (px, py) in enumerate(chips):
                sends.append(_remote(half(a, s, c), half(a, s, c),
                                     ici_send.at[3 * a + j], ici_recv.at[3 * a + j], (px, py, c)))
        for j, (px, py) in enumerate(chips):
            sends.append(_remote(cw_out.at[s], cw_out.at[s], ici_send.at[3 * n + j], ici_recv.at[3 * n + j], (px, py, c)))
        for cp in sends:
            cp.start()
        passed = []
        for a in range(n):
            for j, (px, py) in enumerate(chips):
                sj = 2 * px + py
                got = half(a, sj, c)
                _remote(got, got, ici_send.at[3 * a + j], ici_recv.at[3 * a + j], (px, py, c)).wait_recv()
                fwd = _remote(got, got, d2d_send.at[3 * a + j], d2d_recv.at[3 * a + j], sibling)
                fwd.start()
                passed.append(fwd)
        for j, (px, py) in enumerate(chips):
            got = cw_out.at[2 * px + py]
            _remote(got, got, ici_send.at[3 * n + j], ici_recv.at[3 * n + j], (px, py, c)).wait_recv()
        for a in range(n):
            for j, (px, py) in enumerate(chips):
                got = half(a, 2 * px + py, 1 - c)
                _remote(got, got, d2d_send.at[3 * a + j], d2d_recv.at[3 * a + j], sibling).wait_recv()
        for cp in sends + passed:
            cp.wait_send()

    return pl.pallas_call(
        body, name="gather_weights",
        in_specs=[ANY] * (n + 1), out_specs=[ANY] * (n + 1),
        out_shape=[jax.ShapeDtypeStruct(b.shape, b.dtype) for b in bufs],
        input_output_aliases={i: i for i in range(n + 1)},
        scratch_shapes=[pltpu.SemaphoreType.DMA((3 * n + 3,)), pltpu.SemaphoreType.DMA((3 * n + 3,)),
                        pltpu.SemaphoreType.DMA((3 * n,)), pltpu.SemaphoreType.DMA((3 * n,))],
        compiler_params=pltpu.CompilerParams(has_side_effects=True),
    )(*bufs)


def _swap_halves(grads):
    n = len(grads)

    def body(*refs):
        ins, outs, send_sem, recv_sem = refs[:n], refs[n:2 * n], refs[2 * n], refs[2 * n + 1]
        x, y, c, _ = _position()
        copies = []
        for a in range(n):
            rows = ins[a].shape[1] // 2
            copies.append(_remote(ins[a].at[:, pl.ds((1 - c) * rows, rows), :], outs[a],
                                  send_sem.at[a], recv_sem.at[a], (x, y, 1 - c)))
        for cp in copies:
            cp.start()
        for cp in copies:
            cp.wait()

    return pl.pallas_call(
        body, name="swap_grad_halves",
        in_specs=[ANY] * n, out_specs=[ANY] * n,
        out_shape=[jax.ShapeDtypeStruct((g.shape[0], g.shape[1] // 2, g.shape[2]), F32) for g in grads],
        scratch_shapes=[pltpu.SemaphoreType.DMA((n,)), pltpu.SemaphoreType.DMA((n,))],
        compiler_params=pltpu.CompilerParams(has_side_effects=True),
    )(*grads)


def _pair_sum(g, got, c_arr, name):
    n_sh, R, C = g.shape
    rows = R // 2

    def body(c_ref, g_ref, r_ref, f_ref, b_ref):
        t = g_ref[...] + r_ref[...]
        f_ref[...] = t
        b_ref[...] = t.astype(BF16)

    blk = pl.BlockSpec((1, rows, C), lambda i, c_ref: (i, 0, 0))
    return pl.pallas_call(
        body, name=name,
        grid_spec=pltpu.PrefetchScalarGridSpec(
            num_scalar_prefetch=1, grid=(n_sh,),
            in_specs=[pl.BlockSpec((1, rows, C), lambda i, c_ref: (i, c_ref[0], 0)), blk],
            out_specs=[blk, blk]),
        out_shape=[jax.ShapeDtypeStruct((n_sh, rows, C), F32), jax.ShapeDtypeStruct((n_sh, rows, C), BF16)],
        compiler_params=_params(("parallel",)),
    )(c_arr, g, got)


def _scatter_grads(sums_bf16, small_all):
    n = len(sums_bf16)

    def body(*refs):
        b_ins = refs[:n]
        recvs, sm = refs[n + 1:2 * n + 1], refs[2 * n + 1]
        ici_send, ici_recv, sm_send, sm_recv = refs[2 * n + 2:]
        x, y, c, chips = _position()
        me = 4 * x + 2 * y + c
        copies = []
        for a in range(n):
            for j, (px, py) in enumerate(chips):
                copies.append(_remote(b_ins[a].at[2 * px + py], recvs[a].at[j],
                                      ici_send.at[3 * a + j], ici_recv.at[3 * a + j], (px, py, c)))
        for k in range(1, 8):
            peer = (x ^ (k >> 2), y ^ ((k >> 1) & 1), c ^ (k & 1))
            copies.append(_remote(sm.at[me], sm.at[me], sm_send.at[k - 1], sm_recv.at[k - 1], peer))
        for cp in copies:
            cp.start()
        for cp in copies:
            cp.wait_send()
        for a in range(n):
            for j, (px, py) in enumerate(chips):
                _remote(recvs[a].at[j], recvs[a].at[j], ici_send.at[3 * a + j], ici_recv.at[3 * a + j],
                        (px, py, c)).wait_recv()
        for k in range(1, 8):
            peer = (x ^ (k >> 2), y ^ ((k >> 1) & 1), c ^ (k & 1))
            theirs = sm.at[4 * peer[0] + 2 * peer[1] + peer[2]]
            _remote(theirs, theirs, sm_send.at[k - 1], sm_recv.at[k - 1], peer).wait_recv()

    out = pl.pallas_call(
        body, name="scatter_grads",
        in_specs=[ANY] * (n + 1), out_specs=[ANY] * (n + 1),
        out_shape=[jax.ShapeDtypeStruct((3,) + b.shape[1:], BF16) for b in sums_bf16]
        + [jax.ShapeDtypeStruct(small_all.shape, F32)],
        input_output_aliases={n: n},
        scratch_shapes=[pltpu.SemaphoreType.DMA((3 * n,)), pltpu.SemaphoreType.DMA((3 * n,)),
                        pltpu.SemaphoreType.DMA((7,)), pltpu.SemaphoreType.DMA((7,))],
        compiler_params=pltpu.CompilerParams(has_side_effects=True),
    )(*sums_bf16, small_all)
    return out[:n], out[n]


def _shard_sum(sums_f32, recv, shard_arr, c_arr, name):
    _, rows, C = sums_f32.shape

    def body(s_ref, c_ref, o_ref, r_ref, t_ref):
        t_ref[...] = ((o_ref[0] + r_ref[0].astype(F32)) + r_ref[1].astype(F32)) + r_ref[2].astype(F32)

    return pl.pallas_call(
        body, name=name,
        grid_spec=pltpu.PrefetchScalarGridSpec(
            num_scalar_prefetch=2, grid=(1,),
            in_specs=[pl.BlockSpec((1, rows, C), lambda i, s_ref, c_ref: (s_ref[0], 0, 0)),
                      pl.BlockSpec((3, rows, C), lambda i, s_ref, c_ref: (0, 0, 0))],
            out_specs=pl.BlockSpec((rows, C), lambda i, s_ref, c_ref: (c_ref[0], 0))),
        out_shape=jax.ShapeDtypeStruct((2 * rows, C), F32),
        compiler_params=_params(("arbitrary",)),
    )(shard_arr, c_arr, sums_f32, recv)


def _join_halves(bufs):
    n = len(bufs)

    def body(*refs):
        outs, send_sem, recv_sem = refs[n:2 * n], refs[2 * n], refs[2 * n + 1]
        x, y, c, _ = _position()
        copies = []
        for a in range(n):
            rows = outs[a].shape[0] // 2
            mine = outs[a].at[pl.ds(c * rows, rows), :]
            copies.append(_remote(mine, mine, send_sem.at[a], recv_sem.at[a], (x, y, 1 - c)))
        for cp in copies:
            cp.start()
        for a, cp in enumerate(copies):
            cp.wait_send()
            rows = outs[a].shape[0] // 2
            theirs = outs[a].at[pl.ds((1 - c) * rows, rows), :]
            _remote(theirs, theirs, send_sem.at[a], recv_sem.at[a], (x, y, 1 - c)).wait_recv()

    return pl.pallas_call(
        body, name="join_grad_halves",
        in_specs=[ANY] * n, out_specs=[ANY] * n,
        out_shape=[jax.ShapeDtypeStruct(b.shape, F32) for b in bufs],
        input_output_aliases={i: i for i in range(n)},
        scratch_shapes=[pltpu.SemaphoreType.DMA((n,)), pltpu.SemaphoreType.DMA((n,))],
        compiler_params=pltpu.CompilerParams(has_side_effects=True),
    )(*bufs)


def _small_sum(parts):
    _, R, C = parts.shape

    def body(p_ref, o_ref):
        t = p_ref[0]
        for k in range(1, 8):
            t = t + p_ref[k]
        o_ref[...] = t

    return pl.pallas_call(
        body, name="small_grad_sum", grid=(1,),
        in_specs=[pl.BlockSpec((8, R, C), lambda i: (0, 0, 0))], out_specs=pl.BlockSpec((R, C), lambda i: (0, 0)),
        out_shape=jax.ShapeDtypeStruct((R, C), F32), compiler_params=_params(("arbitrary",)),
    )(parts)


def _adamw_math(w, g, m, v):
    m = ADAM_B1 * m + (1.0 - ADAM_B1) * g
    v = ADAM_B2 * v + (1.0 - ADAM_B2) * (g * g)
    m_hat = m / (1.0 - ADAM_B1 ** ADAM_STEP)
    v_hat = v / (1.0 - ADAM_B2 ** ADAM_STEP)
    delta = -ADAM_LR * (m_hat / (jnp.sqrt(v_hat) + ADAM_EPS) + ADAM_WD * w)
    return delta, m, v


def _adamw_big(w, g, m, v, name):
    R, C = w.shape
    tr = R // 4

    def body(w_ref, g_ref, m_ref, v_ref, d_ref, nm_ref, nv_ref):
        d_ref[...], nm_ref[...], nv_ref[...] = _adamw_math(w_ref[...], g_ref[...], m_ref[...], v_ref[...])

    blk = pl.BlockSpec((tr, C), lambda i: (i, 0))
    return pl.pallas_call(
        body, name=name, grid=(4,), in_specs=[blk] * 4, out_specs=[blk] * 3,
        out_shape=[jax.ShapeDtypeStruct((R, C), F32)] * 3, compiler_params=_params(("parallel",)),
    )(w, g, m, v)


def _adamw_small(ws, gs, ms, vs):
    n = len(ws)

    def body(*refs):
        for a in range(n):
            w, g, m, v = (refs[k * n + a][...] for k in range(4))
            d, nm, nv = _adamw_math(w, g, m, v)
            refs[4 * n + a][...] = d
            refs[5 * n + a][...] = nm
            refs[6 * n + a][...] = nv

    shapes = [jax.ShapeDtypeStruct(w.shape, F32) for w in ws]
    out = pl.pallas_call(body, name="adamw_small", out_shape=shapes * 3)(*ws, *gs, *ms, *vs)
    return out[:n], out[n:2 * n], out[2 * n:]


BIG = ("w_in", "w_out", "w_up", "w_down")
SMALL = ("g_mix_pre", "pool_w", "pool_scale", "g_mix_post", "g_ffn_pre", "conv_b", "g_ffn_post", "conv_w")
ORDER = ("g_mix_pre", "w_in", "pool_w", "pool_scale", "w_out", "g_mix_post", "g_ffn_pre", "w_up", "conv_w", "conv_b",
         "w_down", "g_ffn_post")


def kernel(x, g_mix_pre, w_in, pool_w, pool_scale, w_out, g_mix_post, g_ffn_pre, w_up, conv_w, conv_b, w_down, g_ffn_post, loss_target, m_g_mix_pre, m_w_in, m_pool_w, m_pool_scale, m_w_out, m_g_mix_post, m_g_ffn_pre, m_w_up, m_conv_w, m_conv_b, m_w_down, m_g_ffn_post, v_g_mix_pre, v_w_in, v_pool_w, v_pool_scale, v_w_out, v_g_mix_post, v_g_ffn_pre, v_w_up, v_conv_w, v_conv_b, v_w_down, v_g_ffn_post):
    args = dict(locals())
    W = {n: args[n][0] for n in ORDER}
    M = {n: args["m_" + n][0] for n in ORDER}
    V = {n: args["v_" + n][0] for n in ORDER}
    for d in (W, M, V):
        d["pool_w"] = d["pool_w"].reshape(-1, POOL_GROUP)
        for n in ("g_mix_pre", "pool_scale", "g_mix_post", "g_ffn_pre", "conv_b", "g_ffn_post"):
            d[n] = d[n].reshape(1, -1)
    CW = W["w_up"].shape[1]
    c_arr = lax.axis_index("c").astype(jnp.int32).reshape(1)
    shard = 2 * lax.axis_index("x") + lax.axis_index("y")
    shard_arr = shard.astype(jnp.int32).reshape(1)
    device = 2 * shard + lax.axis_index("c")

    conv_w_slots = lax.dynamic_update_index_in_dim(jnp.zeros((N_SHARD,) + W["conv_w"].shape, F32), W["conv_w"], shard, 0)
    gathered = _gather_weights([_cast_bf16(W[n], shard_arr, "cast_" + n) for n in BIG] + [conv_w_slots])
    w_in_g, w_out_g, w_up_g, w_down_g, conv_w_g = gathered
    D = w_in_g.shape[1]
    conv_w_full = conv_w_g.transpose(1, 0, 2).reshape(CONV_WIDTH, 1, N_SHARD * CW)

    loss, grad_x, G = _local_step(
        x[0], loss_target[0], W["g_mix_pre"], w_in_g, W["pool_w"].reshape(-1, POOL_GROUP, POOL_GROUP), W["pool_scale"],
        w_out_g.reshape(D, D), W["g_mix_post"], W["g_ffn_pre"], w_up_g, conv_w_full, W["conv_b"],
        w_down_g.reshape(2 * CW, D), W["g_ffn_post"])

    partial = [G["w_in"], G["w_out"].reshape(N_SHARD, D // N_SHARD, D), G["w_up"],
               G["w_down"].reshape(N_SHARD, 2 * CW // N_SHARD, D)]
    from_sibling = _swap_halves(partial)
    sums = [_pair_sum(g, r, c_arr, "pair_sum_" + n) for g, r, n in zip(partial, from_sibling, BIG)]
    G["conv_w"] = G["conv_w"].reshape(CONV_WIDTH, N_SHARD, CW).transpose(1, 0, 2)
    G["pool_w"] = G["pool_w"].reshape(-1, POOL_GROUP)
    small = jnp.concatenate([G[n].reshape(-1, LANES) for n in SMALL], axis=0)
    small_slots = lax.dynamic_update_index_in_dim(jnp.zeros((8,) + small.shape, F32), small, device, 0)
    recvs, small_all = _scatter_grads([s[1] for s in sums], small_slots)
    halves = [_shard_sum(s[0], r, shard_arr, c_arr, "shard_sum_" + n) for s, r, n in zip(sums, recvs, BIG)]
    full = dict(zip(BIG, _join_halves(halves)))

    small_total = _small_sum(small_all)
    row = 0
    for n in SMALL:
        rows = G[n].size // LANES
        g = small_total[row:row + rows]
        row += rows
        if n == "conv_w":
            g = lax.dynamic_slice_in_dim(g.reshape(N_SHARD, CONV_WIDTH, CW), shard, 1, axis=0)[0]
        full[n] = g.reshape(W[n].shape)

    delta, new_m, new_v = {}, {}, {}
    for n in BIG:
        delta[n], new_m[n], new_v[n] = _adamw_big(W[n], full[n], M[n], V[n], "adamw_" + n)
    ds, nms, nvs = _adamw_small([W[n] for n in SMALL], [full[n] for n in SMALL], [M[n] for n in SMALL],
                                [V[n] for n in SMALL])
    for n, d, nm, nv in zip(SMALL, ds, nms, nvs):
        delta[n], new_m[n], new_v[n] = d, nm, nv

    loss = lax.psum(loss[0, 0], ("x", "y", "c"))
    shaped = lambda d: [d[n].reshape(args[n].shape) for n in ORDER]
    return (loss, grad_x[None], *shaped(full), *shaped(delta), *shaped(new_m), *shaped(new_v))
```

```python
import functools

import jax
import jax.numpy as jnp
from jax import lax
from jax.experimental import pallas as pl
from jax.experimental.pallas import tpu as pltpu

F32 = jnp.float32
BF16 = jnp.bfloat16

RMS_EPS = 1e-6
NEG_INF = -1e30
N_HEADS = 8
HEAD_DIM = 64
ATTN_WIDTH = N_HEADS * HEAD_DIM
ATTN_SCALE = HEAD_DIM ** -0.5
ATTN_BLOCK = 128
DILATIONS = (1, 4, 16)
POOL_WINDOWS = (2, 4, 8, 16)
POOL_GROUP = 128
POOL_WIDTH = POOL_GROUP * len(POOL_WINDOWS)
POOL_HALO = 16
CONV_WIDTH = 3
CONV_HALO = 8
N_SHARD = 4
LANES = 128

ADAM_LR = 0.001
ADAM_B1 = 0.9
ADAM_B2 = 0.999
ADAM_EPS = 1e-08
ADAM_WD = 0.01
ADAM_STEP = 10

VMEM_LIMIT = 60 * 1024 * 1024
MESH = pl.DeviceIdType.MESH
NT = (((1,), (1,)), ((), ()))
TN = (((0,), (0,)), ((), ()))


def _params(sem, vmem=None):
    return pltpu.CompilerParams(dimension_semantics=sem, vmem_limit_bytes=vmem)


def _const_spec(shape):
    zeros = (0,) * len(shape)
    return pl.BlockSpec(shape, lambda *_: zeros, pipeline_mode=pl.Buffered(1))


def _dot(a, b):
    return jnp.dot(a, b, preferred_element_type=F32)


def _dot_nt(a, b):
    return lax.dot_general(a, b, NT, preferred_element_type=F32)


def _dot_tn(a, b):
    return lax.dot_general(a, b, TN, preferred_element_type=F32)


def _rms_stats(x):
    r = lax.rsqrt(jnp.mean(x * x, axis=-1, keepdims=True) + RMS_EPS)
    return x * r, r


def _rms_bwd(dy, n, r, g):
    dg = jnp.sum(dy * n, axis=0, keepdims=True)
    dn = dy * g
    dx = r * (dn - n * jnp.mean(dn * n, axis=-1, keepdims=True))
    return dx, dg


def _gelu_tanh(g):
    k = 0.7978845608028654
    a = k * (g + 0.044715 * (g * g * g))
    t = jnp.tanh(a)
    gelu = 0.5 * g * (1.0 + t)
    dgelu = 0.5 * (1.0 + t) + 0.5 * g * (1.0 - t * t) * k * (1.0 + 3.0 * 0.044715 * (g * g))
    return gelu, dgelu


def _residue_shape(S, d, dtype):
    return jax.ShapeDtypeStruct((S // d, d * ATTN_WIDTH), dtype)


def _residue_spec(TM, d):
    return pl.BlockSpec((TM // d, d * ATTN_WIDTH), lambda i: (i, 0))


def _token_scratch(TM):
    return [pltpu.VMEM((TM, LANES), F32)] * (ATTN_WIDTH // LANES)


def _put_tokens(dst_s, val):
    for cb, chunk in enumerate(dst_s):
        chunk[...] = val[:, cb * LANES:(cb + 1) * LANES]


def _get_tokens(src_s):
    return jnp.concatenate([chunk[...] for chunk in src_s], axis=1)


def _to_residue(val, src_s, out_ref, d, dtype):
    if d == 1:
        out_ref[...] = val.astype(dtype)
        return
    rows = src_s[0].shape[0]
    for r in range(d):
        for cb, chunk in enumerate(src_s):
            col = r * ATTN_WIDTH + cb * LANES
            out_ref[:, col:col + LANES] = chunk[pl.ds(r, rows // d, stride=d), :].astype(dtype)


def _from_residue(in_ref, dst_s, d):
    if d == 1:
        return in_ref[...].astype(F32)
    rows = dst_s[0].shape[0]
    for r in range(d):
        for cb, chunk in enumerate(dst_s):
            col = r * ATTN_WIDTH + cb * LANES
            chunk[pl.ds(r, rows // d, stride=d), :] = in_ref[:, col:col + LANES].astype(F32)
    return _get_tokens(dst_s)


def _mix_in_fwd(x, g_pre, w_in):
    S, D = x.shape
    TM = 512

    def body(x_ref, g_ref, w_ref, *refs):
        qkv_refs, p_ref, h_ref, t_s = refs[:9], refs[9], refs[10], refs[11:]
        n, _ = _rms_stats(x_ref[...])
        hb = (n * g_ref[...]).astype(BF16)
        h_ref[...] = hb
        for a in range(3):
            res = _dot(hb, w_ref[a])
            if a == 0:
                res = res * ATTN_SCALE
            _put_tokens(t_s, res)
            for i, d in enumerate(DILATIONS):
                _to_residue(res, t_s, qkv_refs[3 * i + a], d, BF16)
        p_ref[...] = _dot(hb, w_ref[3])

    row = lambda w: pl.BlockSpec((TM, w), lambda i: (i, 0))
    out = pl.pallas_call(
        body, name="mix_in_fwd", grid=(S // TM,),
        in_specs=[row(D), _const_spec((1, D)), _const_spec(w_in.shape)],
        out_specs=[_residue_spec(TM, d) for d in DILATIONS for _ in range(3)] + [row(POOL_WIDTH), row(D)],
        out_shape=[_residue_shape(S, d, BF16) for d in DILATIONS for _ in range(3)]
        + [jax.ShapeDtypeStruct((S, POOL_WIDTH), F32), jax.ShapeDtypeStruct((S, D), BF16)],
        scratch_shapes=_token_scratch(TM),
        compiler_params=_params(("parallel",), VMEM_LIMIT),
    )(x, g_pre, w_in)
    return [out[0:3], out[3:6], out[6:9]], out[9], out[10]


def _band_mask(n):
    qi = lax.broadcasted_iota(jnp.int32, (ATTN_BLOCK, 2 * ATTN_BLOCK), 0)
    ki = lax.broadcasted_iota(jnp.int32, (ATTN_BLOCK, 2 * ATTN_BLOCK), 1)
    dist = qi + ATTN_BLOCK - ki
    return (dist >= 0) & (dist <= ATTN_BLOCK) & ((ki >= ATTN_BLOCK) | (n > 0))


def _first_head_lanes():
    return lax.broadcasted_iota(jnp.int32, (1, LANES), 1) < HEAD_DIM


def _stack_heads(pair, first):
    zero = jnp.zeros_like(pair)
    return jnp.concatenate([jnp.where(first, pair, zero), jnp.where(first, zero, pair)], axis=0)


def _unstack_heads(stacked, first):
    return jnp.where(first, stacked[:ATTN_BLOCK], stacked[ATTN_BLOCK:])


def _cargo_copies(kinds, ins, outs, send_sems, recv_sems):
    x, y, c, chips = _position()
    s = 2 * x + y
    sibling = (x, y, 1 - c)
    sends, recvs = [], []
    for a, kind in enumerate(kinds):
        for j, (px, py) in enumerate(chips):
            k = 3 * a + j
            sj = 2 * px + py
            sem = (send_sems.at[k], recv_sems.at[k])
            if kind == "scatter":
                landing = outs[a].at[j]
                sends.append(_remote(ins[a].at[sj], landing, *sem, (px, py, c)))
                recvs.append(_remote(landing, landing, *sem, (px, py, c)))
                continue
            buf = outs[a]
            rows = buf.shape[1] // 2
            half = lambda shard, h: buf.at[shard, pl.ds(h * rows, rows), :]
            if kind == "ici":
                sends.append(_remote(half(s, c), half(s, c), *sem, (px, py, c)))
                recvs.append(_remote(half(sj, c), half(sj, c), *sem, (px, py, c)))
            else:
                sends.append(_remote(half(sj, c), half(sj, c), *sem, sibling))
                recvs.append(_remote(half(sj, 1 - c), half(sj, 1 - c), *sem, sibling))
    return sends, recvs


def _cargo_start(kinds, ins, outs, sems, first_step):
    @pl.when(first_step)
    def _():
        for cp in _cargo_copies(kinds, ins, outs, *sems)[0]:
            cp.start()


def _cargo_finish(kinds, ins, outs, sems, last_step):
    @pl.when(last_step)
    def _():
        sends, recvs = _cargo_copies(kinds, ins, outs, *sems)
        for cp in sends:
            cp.wait_send()
        for cp in recvs:
            cp.wait_recv()


def _cargo_call(cargo, n_in, n_out):
    arrays = [a for _, a in cargo]
    shapes = [jax.ShapeDtypeStruct((3,) + a.shape[1:] if kind == "scatter" else a.shape, a.dtype) for kind, a in cargo]
    aliases = {n_in + i: n_out + i for i, (kind, _) in enumerate(cargo) if kind != "scatter"}
    sems = [pltpu.SemaphoreType.DMA((3 * len(cargo),))] * 2 if cargo else []
    return arrays, [ANY] * len(cargo), shapes, aliases, sems


def _attn_fwd(q, k, v, d, cargo=()):
    L = q.shape[0]
    nb = L // ATTN_BLOCK
    nc = len(cargo)
    kinds = [kind for kind, _ in cargo]

    def body(*refs):
        q_ref, kp_ref, kc_ref, vp_ref, vc_ref = refs[:5]
        o_ref, lse_ref = refs[5 + nc:7 + nc]
        cargo_refs = (kinds, refs[5:5 + nc], refs[7 + nc:7 + 2 * nc], refs[7 + 2 * nc:])
        r, n = pl.program_id(0), pl.program_id(1)
        if nc:
            _cargo_start(*cargo_refs, (r == 0) & (n == 0))
        valid = _band_mask(pl.program_id(1))
        valid2 = jnp.concatenate([valid, valid], axis=0)
        first = _first_head_lanes()
        for hp in range(N_HEADS // 2):
            sl = slice(hp * LANES, (hp + 1) * LANES)
            kk = jnp.concatenate([kp_ref[:, sl], kc_ref[:, sl]], axis=0)
            vv = jnp.concatenate([vp_ref[:, sl], vc_ref[:, sl]], axis=0)
            s = jnp.where(valid2, _dot_nt(_stack_heads(q_ref[:, sl], first), kk), NEG_INF)
            m = jnp.max(s, axis=-1, keepdims=True)
            p = jnp.exp(s - m)
            den = jnp.sum(p, axis=-1, keepdims=True)
            o_ref[:, sl] = _unstack_heads(_dot(p.astype(BF16), vv) / den, first)
            lse_ref[:, sl] = _unstack_heads(m + jnp.log(den), first)
        if nc:
            _cargo_finish(*cargo_refs, (r == d - 1) & (n == nb - 1))

    cur = pl.BlockSpec((ATTN_BLOCK, ATTN_WIDTH), lambda r, n: (n, r))
    prev = pl.BlockSpec((ATTN_BLOCK, ATTN_WIDTH), lambda r, n: (jnp.maximum(n - 1, 0), r))
    arrays, specs, shapes, aliases, sems = _cargo_call(cargo, 5, 2)
    out = pl.pallas_call(
        body, name=f"attn_fwd_d{d}", grid=(d, nb),
        in_specs=[cur, prev, cur, prev, cur] + specs,
        out_specs=[cur, cur] + specs,
        out_shape=[jax.ShapeDtypeStruct((L, d * ATTN_WIDTH), F32)] * 2 + shapes,
        input_output_aliases=aliases, scratch_shapes=sems,
        compiler_params=_params(("arbitrary", "arbitrary")),
    )(q, k, k, v, v, *arrays)
    return out[0], out[1], out[2:]


def _attn_mix(outs, lses):
    S = outs[0].shape[0]
    TM = 512
    n = len(DILATIONS)

    def body(*refs):
        o_refs, l_refs, attn_ref, lse_refs, t_s = refs[:n], refs[n:2 * n], refs[2 * n], refs[2 * n + 1:3 * n + 1], refs[3 * n + 1:]
        os = [_from_residue(o_refs[i], t_s, d) for i, d in enumerate(DILATIONS)]
        ls = [_from_residue(l_refs[i], t_s, d) for i, d in enumerate(DILATIONS)]
        m = jnp.maximum(jnp.maximum(ls[0], ls[1]), ls[2])
        es = [jnp.exp(l - m) for l in ls]
        den = es[0] + es[1] + es[2]
        attn_ref[...] = (es[0] * os[0] + es[1] * os[1] + es[2] * os[2]) / den
        lse = m + jnp.log(den)
        _put_tokens(t_s, lse)
        for i, d in enumerate(DILATIONS):
            _to_residue(lse, t_s, lse_refs[i], d, F32)

    specs = [_residue_spec(TM, d) for d in DILATIONS]
    out = pl.pallas_call(
        body, name="attn_mix", grid=(S // TM,),
        in_specs=specs * 2, out_specs=[specs[0]] + specs,
        out_shape=[jax.ShapeDtypeStruct((S, ATTN_WIDTH), F32)] + [_residue_shape(S, d, F32) for d in DILATIONS],
        scratch_shapes=_token_scratch(TM),
        compiler_params=_params(("parallel",)),
    )(*outs, *lses)
    return out[0], out[1:]


def _pool_counts(first_row, rows, w):
    t = first_row + lax.broadcasted_iota(jnp.int32, (rows, 1), 0)
    return jnp.minimum(t + 1, w).astype(F32)


def _trailing_sums(xe, w):
    s, k = xe, 1
    while k < w:
        s = s + pltpu.roll(s, k, 0)
        k *= 2
    return s


def _leading_sums(xe, w):
    rows = xe.shape[0]
    s, k = xe, 1
    while k < w:
        s = s + pltpu.roll(s, rows - k, 0)
        k *= 2
    return s


def _pooled_groups(halo, cur, first_row):
    TM = cur.shape[0]
    xe = jnp.concatenate([halo, cur], axis=0)
    out = []
    for g, w in enumerate(POOL_WINDOWS):
        a = xe[:, g * POOL_GROUP:(g + 1) * POOL_GROUP]
        s = _trailing_sums(a, w)[POOL_HALO:]
        out.append(s / _pool_counts(first_row, TM, w) - a[POOL_HALO:])
    return out


def _pool_fwd(pool_in, pool_w, pool_scale):
    S = pool_in.shape[0]
    TM = 512
    HB = TM // POOL_HALO

    def body(cur_ref, halo_ref, w_ref, sc_ref, y_ref):
        i = pl.program_id(0)
        halo = jnp.where(i > 0, halo_ref[...], 0.0)
        pooled = _pooled_groups(halo, cur_ref[...], i * TM)
        for g in range(len(POOL_WINDOWS)):
            sl = slice(g * POOL_GROUP, (g + 1) * POOL_GROUP)
            y = _dot(pooled[g].astype(BF16), w_ref[g].astype(BF16)) * sc_ref[:, sl]
            y_ref[:, sl] = y.astype(BF16)

    return pl.pallas_call(
        body, name="pool_fwd", grid=(S // TM,),
        in_specs=[pl.BlockSpec((TM, POOL_WIDTH), lambda i: (i, 0)),
                  pl.BlockSpec((POOL_HALO, POOL_WIDTH), lambda i: (jnp.maximum(i * HB - 1, 0), 0)),
                  _const_spec(pool_w.shape), _const_spec((1, POOL_WIDTH))],
        out_specs=pl.BlockSpec((TM, POOL_WIDTH), lambda i: (i, 0)),
        out_shape=jax.ShapeDtypeStruct((S, POOL_WIDTH), BF16),
        compiler_params=_params(("parallel",)),
    )(pool_in, pool_in, pool_w, pool_scale)


def _mix_out_fwd(attn, pool, w_out, x, g_post, g_ffn_pre):
    S, D = x.shape
    TM = 512

    def body(a_ref, p_ref, w_ref, x_ref, gp_ref, gf_ref, mixed_ref, x1_ref, h2_ref, cat_ref):
        ab = a_ref[...].astype(BF16)
        cat_ref[:, :ATTN_WIDTH] = ab
        cat_ref[:, ATTN_WIDTH:] = p_ref[...]
        mixed = _dot(ab, w_ref[:ATTN_WIDTH, :]) + _dot(p_ref[...], w_ref[ATTN_WIDTH:, :])
        mixed_ref[...] = mixed
        n, _ = _rms_stats(mixed)
        x1 = x_ref[...] + n * gp_ref[...]
        x1_ref[...] = x1
        n2, _ = _rms_stats(x1)
        h2_ref[...] = (n2 * gf_ref[...]).astype(BF16)

    row = lambda w: pl.BlockSpec((TM, w), lambda i: (i, 0))
    return pl.pallas_call(
        body, name="mix_out_fwd", grid=(S // TM,),
        in_specs=[row(ATTN_WIDTH), row(POOL_WIDTH), _const_spec(w_out.shape), row(D),
                  _const_spec((1, D)), _const_spec((1, D))],
        out_specs=[row(D), row(D), row(D), row(D)],
        out_shape=[jax.ShapeDtypeStruct((S, D), F32), jax.ShapeDtypeStruct((S, D), F32),
                   jax.ShapeDtypeStruct((S, D), BF16), jax.ShapeDtypeStruct((S, D), BF16)],
        compiler_params=_params(("parallel",), VMEM_LIMIT),
    )(attn, pool, w_out, x, g_post, g_ffn_pre)


def _ffn_fwd(h2, x1, target, w_up, w_down, conv_w, conv_b, g_post):
    S, D = x1.shape
    CW = w_up.shape[2]
    FF = 2 * CW
    TM = 256

    def body(h2_ref, x1_ref, t_ref, wu_ref, wd_ref, cw_ref, cb_ref, g_ref,
             u_ref, yv_ref, dy_ref, df_ref, dc_ref, loss_ref, dg_ref, dcb_ref, dcw_ref,
             ue_s, cg_s, cv_s):
        i = pl.program_id(0)

        @pl.when(i == 0)
        def _():
            loss_ref[...] = jnp.zeros_like(loss_ref)
            dg_ref[...] = jnp.zeros_like(dg_ref)
            dcb_ref[...] = jnp.zeros_like(dcb_ref)
            dcw_ref[...] = jnp.zeros_like(dcw_ref)
            ue_s[0:CONV_HALO, :] = jnp.zeros((CONV_HALO, 2 * FF), F32)

        @pl.when(i > 0)
        def _():
            ue_s[0:CONV_HALO, :] = ue_s[TM:TM + CONV_HALO, :]

        def shifted(cols, k):
            return pltpu.roll(ue_s[:, cols], k, 0)[CONV_HALO:]

        hb = h2_ref[...]
        f = jnp.zeros((TM, D), F32)
        for j in range(2):
            jc = slice(j * CW, (j + 1) * CW)
            for half, c_s in ((0, cg_s), (1, cv_s)):
                blk = 2 * half + j
                cols = slice(blk * CW, (blk + 1) * CW)
                uu = _dot(hb, wu_ref[blk])
                u_ref[:, cols] = uu.astype(BF16)
                ue_s[CONV_HALO:, cols] = uu
                c_s[:, jc] = (cb_ref[:, cols] + cw_ref[2, :, cols] * uu
                              + cw_ref[1, :, cols] * shifted(cols, 1)
                              + cw_ref[0, :, cols] * shifted(cols, 2))
            gelu, _ = _gelu_tanh(cg_s[:, jc])
            yvb = (gelu * cv_s[:, jc]).astype(BF16)
            yv_ref[:, jc] = yvb
            f = f + _dot(yvb, wd_ref[jc, :])

        n, r = _rms_stats(f)
        err = x1_ref[...] + n * g_ref[...] - t_ref[...]
        loss_ref[...] += 0.5 * jnp.sum(jnp.mean(err * err, axis=-1, keepdims=True), axis=0, keepdims=True)
        dy = err / D
        dy_ref[...] = dy
        df, dg = _rms_bwd(dy, n, r, g_ref[...])
        dg_ref[...] += dg
        dfb = df.astype(BF16)
        df_ref[...] = dfb

        for j in range(2):
            jc = slice(j * CW, (j + 1) * CW)
            dyv = _dot_nt(dfb, wd_ref[jc, :])
            cv = cv_s[:, jc]
            gelu, dgelu = _gelu_tanh(cg_s[:, jc])
            for half, dcv in ((0, dyv * cv * dgelu), (1, dyv * gelu)):
                blk = 2 * half + j
                cols = slice(blk * CW, (blk + 1) * CW)
                dc_ref[:, cols] = dcv.astype(BF16)
                dcb_ref[:, cols] += jnp.sum(dcv, axis=0, keepdims=True)
                dcw_ref[2, :, cols] += jnp.sum(dcv * ue_s[CONV_HALO:, cols], axis=0, keepdims=True)
                dcw_ref[1, :, cols] += jnp.sum(dcv * shifted(cols, 1), axis=0, keepdims=True)
                dcw_ref[0, :, cols] += jnp.sum(dcv * shifted(cols, 2), axis=0, keepdims=True)

    row = lambda w: pl.BlockSpec((TM, w), lambda i: (i, 0))
    acc = lambda shape: pl.BlockSpec(shape, lambda i: (0,) * len(shape))
    return pl.pallas_call(
        body, name="ffn_fwd", grid=(S // TM,),
        in_specs=[row(D), row(D), row(D), _const_spec(w_up.shape), _const_spec(w_down.shape),
                  _const_spec(conv_w.shape), _const_spec((1, 2 * FF)), _const_spec((1, D))],
        out_specs=[row(2 * FF), row(FF), row(D), row(D), row(2 * FF),
                   acc((1, 1)), acc((1, D)), acc((1, 2 * FF)), acc((CONV_WIDTH, 1, 2 * FF))],
        out_shape=[jax.ShapeDtypeStruct((S, 2 * FF), BF16), jax.ShapeDtypeStruct((S, FF), BF16),
                   jax.ShapeDtypeStruct((S, D), F32), jax.ShapeDtypeStruct((S, D), BF16),
                   jax.ShapeDtypeStruct((S, 2 * FF), BF16),
                   jax.ShapeDtypeStruct((1, 1), F32), jax.ShapeDtypeStruct((1, D), F32),
                   jax.ShapeDtypeStruct((1, 2 * FF), F32), jax.ShapeDtypeStruct((CONV_WIDTH, 1, 2 * FF), F32)],
        scratch_shapes=[pltpu.VMEM((TM + CONV_HALO, 2 * FF), F32), pltpu.VMEM((TM, FF), F32),
                        pltpu.VMEM((TM, FF), F32)],
        compiler_params=_params(("arbitrary",), VMEM_LIMIT),
    )(h2, x1, target, w_up, w_down, conv_w, conv_b, g_post)


def _ffn_bwd(dc, conv_w, w_up, x1, g_ffn_pre, dy):
    S, D = x1.shape
    CW = w_up.shape[2]
    F2 = 4 * CW
    TM = 256
    HB = TM // CONV_HALO
    last = S // CONV_HALO - 1
    n_tiles = S // TM

    def body(dc_ref, halo_ref, cw_ref, wu_ref, x1_ref, g_ref, dy_ref, du_ref, dx1_ref, dg_ref):
        i = pl.program_id(0)

        @pl.when(i == 0)
        def _():
            dg_ref[...] = jnp.zeros_like(dg_ref)

        keep = i < n_tiles - 1
        dh2 = jnp.zeros((TM, D), F32)
        for blk in range(N_SHARD):
            cols = slice(blk * CW, (blk + 1) * CW)
            halo = jnp.where(keep, halo_ref[:, cols].astype(F32), 0.0)
            dce = jnp.concatenate([dc_ref[:, cols].astype(F32), halo], axis=0)
            rows = TM + CONV_HALO
            du = (cw_ref[2, :, cols] * dce[:TM]
                  + cw_ref[1, :, cols] * pltpu.roll(dce, rows - 1, 0)[:TM]
                  + cw_ref[0, :, cols] * pltpu.roll(dce, rows - 2, 0)[:TM])
            dub = du.astype(BF16)
            du_ref[:, cols] = dub
            dh2 = dh2 + _dot_nt(dub, wu_ref[blk])
        n2, r2 = _rms_stats(x1_ref[...])
        dx, dg = _rms_bwd(dh2, n2, r2, g_ref[...])
        dg_ref[...] += dg
        dx1_ref[...] = dy_ref[...] + dx

    row = lambda w: pl.BlockSpec((TM, w), lambda i: (i, 0))
    return pl.pallas_call(
        body, name="ffn_bwd", grid=(S // TM,),
        in_specs=[row(F2), pl.BlockSpec((CONV_HALO, F2), lambda i: (jnp.minimum((i + 1) * HB, last), 0)),
                  _const_spec(conv_w.shape), _const_spec(w_up.shape), row(D), _const_spec((1, D)), row(D)],
        out_specs=[row(F2), row(D), pl.BlockSpec((1, D), lambda i: (0, 0))],
        out_shape=[jax.ShapeDtypeStruct((S, F2), BF16), jax.ShapeDtypeStruct((S, D), F32),
                   jax.ShapeDtypeStruct((1, D), F32)],
        compiler_params=_params(("arbitrary",), VMEM_LIMIT),
    )(dc, dc, conv_w, w_up, x1, g_ffn_pre, dy)


def _matmul_tn(a, b, n_blocks, name):
    S, M = a.shape
    N = b.shape[1]
    tn = N // n_blocks
    tm = M if M <= 1024 else M // 2
    tk = 2048
    nk = S // tk

    def body(a_ref, b_ref, o_ref):
        @pl.when(pl.program_id(2) == 0)
        def _():
            o_ref[...] = jnp.zeros_like(o_ref)
        o_ref[0] += _dot_tn(a_ref[...], b_ref[...])

    return pl.pallas_call(
        body, name=name, grid=(M // tm, n_blocks, nk),
        in_specs=[pl.BlockSpec((tk, tm), lambda i, j, k: (k, i)), pl.BlockSpec((tk, tn), lambda i, j, k: (k, j))],
        out_specs=pl.BlockSpec((1, tm, tn), lambda i, j, k: (j, i, 0)),
        out_shape=jax.ShapeDtypeStruct((n_blocks, M, tn), F32),
        compiler_params=_params(("parallel", "parallel", "arbitrary"), VMEM_LIMIT),
    )(a, b)


def _mix_out_bwd(dx1, mixed, g_post, w_out, attn):
    S, D = dx1.shape
    TM = 512

    nd = len(DILATIONS)

    def body(dx_ref, m_ref, g_ref, w_ref, a_ref, dm_ref, dp_ref, dg_ref, *refs):
        da_refs, dl_refs, t_s = refs[:nd], refs[nd:2 * nd], refs[2 * nd:]

        @pl.when(pl.program_id(0) == 0)
        def _():
            dg_ref[...] = jnp.zeros_like(dg_ref)

        n, r = _rms_stats(m_ref[...])
        dm, dg = _rms_bwd(dx_ref[...], n, r, g_ref[...])
        dg_ref[...] += dg
        dmb = dm.astype(BF16)
        dm_ref[...] = dmb
        da = _dot_nt(dmb, w_ref[:ATTN_WIDTH, :])
        _put_tokens(t_s, da)
        for i, d in enumerate(DILATIONS):
            _to_residue(da, t_s, da_refs[i], d, BF16)
        dp_ref[...] = _dot_nt(dmb, w_ref[ATTN_WIDTH:, :])
        prod = da * a_ref[...]
        hi = prod.astype(BF16)
        lo = (prod - hi.astype(F32)).astype(BF16)
        ri = lax.broadcasted_iota(jnp.int32, (ATTN_WIDTH, ATTN_WIDTH), 0) // HEAD_DIM
        ci = lax.broadcasted_iota(jnp.int32, (ATTN_WIDTH, ATTN_WIDTH), 1) // HEAD_DIM
        ones = (ri == ci).astype(BF16)
        delta = _dot(hi, ones) + _dot(lo, ones)
        _put_tokens(t_s, delta)
        for i, d in enumerate(DILATIONS):
            _to_residue(delta, t_s, dl_refs[i], d, F32)

    row = lambda w: pl.BlockSpec((TM, w), lambda i: (i, 0))
    specs = [_residue_spec(TM, d) for d in DILATIONS]
    out = pl.pallas_call(
        body, name="mix_out_bwd", grid=(S // TM,),
        in_specs=[row(D), row(D), _const_spec((1, D)), _const_spec(w_out.shape), row(ATTN_WIDTH)],
        out_specs=[row(D), row(POOL_WIDTH), pl.BlockSpec((1, D), lambda i: (0, 0))] + specs * 2,
        out_shape=[jax.ShapeDtypeStruct((S, D), BF16), jax.ShapeDtypeStruct((S, POOL_WIDTH), F32),
                   jax.ShapeDtypeStruct((1, D), F32)]
        + [_residue_shape(S, d, BF16) for d in DILATIONS] + [_residue_shape(S, d, F32) for d in DILATIONS],
        scratch_shapes=_token_scratch(TM),
        compiler_params=_params(("arbitrary",), VMEM_LIMIT),
    )(dx1, mixed, g_post, w_out, attn)
    return out[0], out[1], out[2], out[3:3 + nd], out[3 + nd:]


def _pool_bwd(pool_in, d_pool, pool_w, pool_scale):
    S = pool_in.shape[0]
    TM = 512
    HB = TM // POOL_HALO
    last = S // POOL_HALO - 1
    G = len(POOL_WINDOWS)

    def body(cur_ref, halo_ref, dcur_ref, dnext_ref, w_ref, sc_ref, dxin_ref, dw_ref, dsc_ref):
        i = pl.program_id(0)

        @pl.when(i == 0)
        def _():
            dw_ref[...] = jnp.zeros_like(dw_ref)
            dsc_ref[...] = jnp.zeros_like(dsc_ref)

        halo = jnp.where(i > 0, halo_ref[...], 0.0)
        pooled = _pooled_groups(halo, cur_ref[...], i * TM)
        dnext = jnp.where(i < S // TM - 1, dnext_ref[...], 0.0)
        dye = jnp.concatenate([dcur_ref[...], dnext], axis=0)
        for g, w in enumerate(POOL_WINDOWS):
            sl = slice(g * POOL_GROUP, (g + 1) * POOL_GROUP)
            wg = w_ref[g].astype(BF16)
            pb = pooled[g].astype(BF16)
            dsc_ref[:, sl] += jnp.sum(dye[:TM, sl] * _dot(pb, wg), axis=0, keepdims=True)
            dpre = (dye[:, sl] * sc_ref[:, sl]).astype(BF16)
            dw_ref[g] += _dot_tn(pb, dpre[:TM])
            dpooled = _dot_nt(dpre, wg)
            z = dpooled / _pool_counts(i * TM, TM + POOL_HALO, w)
            dxin_ref[:, sl] = (_leading_sums(z, w)[:TM] - dpooled[:TM]).astype(BF16)

    row = pl.BlockSpec((TM, POOL_WIDTH), lambda i: (i, 0))
    return pl.pallas_call(
        body, name="pool_bwd", grid=(S // TM,),
        in_specs=[row, pl.BlockSpec((POOL_HALO, POOL_WIDTH), lambda i: (jnp.maximum(i * HB - 1, 0), 0)),
                  row, pl.BlockSpec((POOL_HALO, POOL_WIDTH), lambda i: (jnp.minimum((i + 1) * HB, last), 0)),
                  _const_spec(pool_w.shape), _const_spec((1, POOL_WIDTH))],
        out_specs=[row, pl.BlockSpec((G, POOL_GROUP, POOL_GROUP), lambda i: (0, 0, 0)),
                   pl.BlockSpec((1, POOL_WIDTH), lambda i: (0, 0))],
        out_shape=[jax.ShapeDtypeStruct((S, POOL_WIDTH), BF16), jax.ShapeDtypeStruct((G, POOL_GROUP, POOL_GROUP), F32),
                   jax.ShapeDtypeStruct((1, POOL_WIDTH), F32)],
        compiler_params=_params(("arbitrary",)),
    )(pool_in, pool_in, d_pool, d_pool, pool_w, pool_scale)


def _attn_bwd(q, k, v, d_attn, lse, delta, d, cargo=()):
    L = q.shape[0]
    nb = L // ATTN_BLOCK
    nc = len(cargo)
    kinds = [kind for kind, _ in cargo]

    def body(*refs):
        q_ref, kp_ref, kc_ref, vp_ref, vc_ref, do_ref, lse_ref, dl_ref = refs[:8]
        dq_ref, dk_ref, dv_ref = refs[8 + nc:11 + nc]
        ck_s, cv_s = refs[11 + 2 * nc:13 + 2 * nc]
        cargo_refs = (kinds, refs[8:8 + nc], refs[11 + nc:11 + 2 * nc], refs[13 + 2 * nc:])
        r, n = pl.program_id(0), pl.program_id(1)
        if nc:
            _cargo_start(*cargo_refs, (r == 0) & (n == 0))

        @pl.when(n == 0)
        def _():
            ck_s[...] = jnp.zeros_like(ck_s)
            cv_s[...] = jnp.zeros_like(cv_s)

        @pl.when(n < nb)
        def _():
            valid = _band_mask(n)
            valid2 = jnp.concatenate([valid, valid], axis=0)
            first = _first_head_lanes()

            def stacked_column(ref, lane):
                return jnp.concatenate([ref[:, lane:lane + 1], ref[:, lane + HEAD_DIM:lane + HEAD_DIM + 1]], axis=0)

            for hp in range(N_HEADS // 2):
                sl = slice(hp * LANES, (hp + 1) * LANES)
                qq = _stack_heads(q_ref[:, sl], first)
                dd = _stack_heads(do_ref[:, sl], first)
                kk = jnp.concatenate([kp_ref[:, sl], kc_ref[:, sl]], axis=0)
                vv = jnp.concatenate([vp_ref[:, sl], vc_ref[:, sl]], axis=0)
                s = _dot_nt(qq, kk)
                p = jnp.where(valid2, jnp.exp(s - stacked_column(lse_ref, hp * LANES)), 0.0)
                dp = _dot_nt(dd, vv)
                ds = (p * (dp - stacked_column(dl_ref, hp * LANES))).astype(BF16)
                dq_ref[:, sl] = (_unstack_heads(_dot(ds, kk), first) * ATTN_SCALE).astype(BF16)
                dk = _dot_tn(ds, qq)
                dv = _dot_tn(p.astype(BF16), dd)
                dk_ref[:, sl] = (ck_s[:, sl] + dk[:ATTN_BLOCK]).astype(BF16)
                dv_ref[:, sl] = (cv_s[:, sl] + dv[:ATTN_BLOCK]).astype(BF16)
                ck_s[:, sl] = dk[ATTN_BLOCK:]
                cv_s[:, sl] = dv[ATTN_BLOCK:]

        @pl.when(n == nb)
        def _():
            dk_ref[...] = ck_s[...].astype(BF16)
            dv_ref[...] = cv_s[...].astype(BF16)

        if nc:
            _cargo_finish(*cargo_refs, (r == d - 1) & (n == nb))

    blk = (ATTN_BLOCK, ATTN_WIDTH)
    cur = pl.BlockSpec(blk, lambda r, n: (jnp.minimum(n, nb - 1), r))
    prev = pl.BlockSpec(blk, lambda r, n: (jnp.maximum(jnp.minimum(n, nb - 1) - 1, 0), r))
    done = pl.BlockSpec(blk, lambda r, n: (jnp.maximum(n - 1, 0), r))
    arrays, specs, shapes, aliases, sems = _cargo_call(cargo, 8, 3)
    out = pl.pallas_call(
        body, name=f"attn_bwd_d{d}", grid=(d, nb + 1),
        in_specs=[cur, prev, cur, prev, cur, cur, cur, cur] + specs, out_specs=[cur, done, done] + specs,
        out_shape=[jax.ShapeDtypeStruct((L, d * ATTN_WIDTH), BF16)] * 3 + shapes,
        input_output_aliases=aliases,
        scratch_shapes=[pltpu.VMEM(blk, F32), pltpu.VMEM(blk, F32)] + sems,
        compiler_params=_params(("arbitrary", "arbitrary")),
    )(q, k, k, v, v, d_attn, lse, delta, *arrays)
    return out[:3], out[3:]


def _mix_in_bwd(dqkv, d_pool_in, w_in, x, g_pre, dx1):
    S, D = x.shape
    TM = 512
    nd = len(DILATIONS)

    def body(*refs):
        g_refs = refs[:3 * nd]
        dpi_ref, w_ref, x_ref, g_ref, dx1_ref, dproj_ref, gx_ref, dg_ref = refs[3 * nd:3 * nd + 8]
        t_s = refs[3 * nd + 8:]

        @pl.when(pl.program_id(0) == 0)
        def _():
            dg_ref[...] = jnp.zeros_like(dg_ref)

        dh = jnp.zeros((TM, D), F32)
        for a in range(4):
            if a < 3:
                tot = g_refs[a][...].astype(F32)
                for i, d in enumerate(DILATIONS[1:]):
                    tot = tot + _from_residue(g_refs[3 * (i + 1) + a], t_s, d)
                db = tot.astype(BF16)
            else:
                db = dpi_ref[...]
            dproj_ref[:, a * ATTN_WIDTH:(a + 1) * ATTN_WIDTH] = db
            dh = dh + _dot_nt(db, w_ref[a])
        n, r = _rms_stats(x_ref[...])
        dx, dg = _rms_bwd(dh, n, r, g_ref[...])
        dg_ref[...] += dg
        gx_ref[...] = dx1_ref[...] + dx

    row = lambda w: pl.BlockSpec((TM, w), lambda i: (i, 0))
    return pl.pallas_call(
        body, name="mix_in_bwd", grid=(S // TM,),
        in_specs=[_residue_spec(TM, d) for d in DILATIONS for _ in range(3)]
        + [row(POOL_WIDTH), _const_spec(w_in.shape), row(D), _const_spec((1, D)), row(D)],
        out_specs=[row(4 * ATTN_WIDTH), row(D), pl.BlockSpec((1, D), lambda i: (0, 0))],
        out_shape=[jax.ShapeDtypeStruct((S, 4 * ATTN_WIDTH), BF16), jax.ShapeDtypeStruct((S, D), F32),
                   jax.ShapeDtypeStruct((1, D), F32)],
        scratch_shapes=_token_scratch(TM),
        compiler_params=_params(("arbitrary",), VMEM_LIMIT),
    )(*[g for gs in dqkv for g in gs], d_pool_in, w_in, x, g_pre, dx1)


def _local_step(x, target, g_mix_pre, w_in, pool_w, pool_scale, w_out, g_mix_post, g_ffn_pre,
                w_up, conv_w, conv_b, w_down, g_ffn_post, c_arr=None):
    on_mesh = c_arr is not None
    D = x.shape[1]
    CW = w_up.shape[2]
    qkv, pool_in, h1 = _mix_in_fwd(x, g_mix_pre, w_in)
    o1, l1, got = _attn_fwd(*qkv[0], 1, [("ici", w_out), ("ici", w_down)] if on_mesh else ())
    w_out, w_down = got if on_mesh else (w_out, w_down)
    o4, l4, got = _attn_fwd(*qkv[1], 4, [("d2d", w_out), ("d2d", w_down), ("ici", w_up)] if on_mesh else ())
    w_out, w_down, w_up = got if on_mesh else (w_out, w_down, w_up)
    o16, l16, got = _attn_fwd(*qkv[2], 16, [("d2d", w_up)] if on_mesh else ())
    w_up = got[0] if on_mesh else w_up
    w_out = w_out.reshape(D, D)
    w_down = w_down.reshape(2 * CW, D)
    attn, lse = _attn_mix((o1, o4, o16), (l1, l4, l16))
    pool = _pool_fwd(pool_in, pool_w, pool_scale)
    mixed, x1, h2, cat = _mix_out_fwd(attn, pool, w_out, x, g_mix_post, g_ffn_pre)

    u, yv, dy, df, dc, loss, d_g_ffn_post, d_conv_b, d_conv_w = _ffn_fwd(
        h2, x1, target, w_up, w_down, conv_w, conv_b, g_ffn_post)
    del u
    du, dx1, d_g_ffn_pre = _ffn_bwd(dc, conv_w, w_up, x1, g_ffn_pre, dy)
    d_w_up = _matmul_tn(h2, du, N_SHARD, "grad_w_up")
    d_w_down = _matmul_tn(yv, df, 1, "grad_w_down")[0].reshape(N_SHARD, CW // 2, D)
    cargo = [(), (), ()]
    if on_mesh:
        from_sibling = _swap_halves([d_w_up, d_w_down], "ffn")
        up_f32, up_bf16 = _pair_sum(d_w_up, from_sibling[0], c_arr, "pair_sum_w_up")
        down_f32, down_bf16 = _pair_sum(d_w_down, from_sibling[1], c_arr, "pair_sum_w_down")
        cargo = [[("scatter", down_bf16)], [("scatter", up_bf16)], ()]

    d_mixed, d_pool, d_g_mix_post, d_attn, delta = _mix_out_bwd(dx1, mixed, g_mix_post, w_out, attn)
    d_w_out = _matmul_tn(cat, d_mixed, 1, "grad_w_out")[0].reshape(N_SHARD, D // N_SHARD, D)
    d_pool_in, d_pool_w, d_pool_scale = _pool_bwd(pool_in, d_pool, pool_w, pool_scale)
    dqkv, landed = zip(*[_attn_bwd(*qkv[i], d_attn[i], lse[i], delta[i], d, cargo[i]) for i, d in enumerate(DILATIONS)])
    if on_mesh:
        d_w_down, d_w_up = (down_f32, landed[0][0]), (up_f32, landed[1][0])
    d_proj, grad_x, d_g_mix_pre = _mix_in_bwd(dqkv, d_pool_in, w_in, x, g_mix_pre, dx1)
    d_w_in = _matmul_tn(h1, d_proj, N_SHARD, "grad_w_in")

    grads = dict(g_mix_pre=d_g_mix_pre, w_in=d_w_in, pool_w=d_pool_w, pool_scale=d_pool_scale, w_out=d_w_out,
                 g_mix_post=d_g_mix_post, g_ffn_pre=d_g_ffn_pre, w_up=d_w_up, conv_w=d_conv_w, conv_b=d_conv_b,
                 w_down=d_w_down, g_ffn_post=d_g_ffn_post)
    return loss, grad_x, grads


ANY = pl.BlockSpec(memory_space=pl.ANY)


def _position():
    x, y, c = lax.axis_index("x"), lax.axis_index("y"), lax.axis_index("c")
    chips = [(1 - x, y), (x, 1 - y), (1 - x, 1 - y)]
    return x, y, c, chips


def _remote(src, dst, send_sem, recv_sem, to):
    return pltpu.make_async_remote_copy(src_ref=src, dst_ref=dst, send_sem=send_sem, recv_sem=recv_sem,
                                        device_id=to, device_id_type=MESH)


def _cast_bf16(w, shard_arr, name):
    R, C = w.shape
    tr = R // 2

    def body(s_ref, w_ref, o_ref):
        o_ref[0] = w_ref[...].astype(BF16)

    return pl.pallas_call(
        body, name=name,
        grid_spec=pltpu.PrefetchScalarGridSpec(
            num_scalar_prefetch=1, grid=(2,),
            in_specs=[pl.BlockSpec((tr, C), lambda i, s_ref: (i, 0))],
            out_specs=pl.BlockSpec((1, tr, C), lambda i, s_ref: (s_ref[0], i, 0))),
        out_shape=jax.ShapeDtypeStruct((N_SHARD, R, C), BF16),
        compiler_params=_params(("parallel",)))(shard_arr, w)


def _gather_weights(bufs):
    n = len(bufs) - 1

    def body(*refs):
        outs, cw_out = refs[n + 1:2 * n + 1], refs[2 * n + 1]
        ici_send, ici_recv, d2d_send, d2d_recv = refs[2 * n + 2:]
        x, y, c, chips = _position()
        s = 2 * x + y
        sibling = (x, y, 1 - c)

        def half(a, shard, h):
            rows = outs[a].shape[1] // 2
            return outs[a].at[shard, pl.ds(h * rows, rows), :]

        sends = []
        for a in range(n):
            for j, (px, py) in enumerate(chips):
                sends.append(_remote(half(a, s, c), half(a, s, c),
                                     ici_send.at[3 * a + j], ici_recv.at[3 * a + j], (px, py, c)))
        for j, (px, py) in enumerate(chips):
            sends.append(_remote(cw_out.at[s], cw_out.at[s], ici_send.at[3 * n + j], ici_recv.at[3 * n + j], (px, py, c)))
        for cp in sends:
            cp.start()
        passed = []
        for a in range(n):
            for j, (px, py) in enumerate(chips):
                sj = 2 * px + py
                got = half(a, sj, c)
                _remote(got, got, ici_send.at[3 * a + j], ici_recv.at[3 * a + j], (px, py, c)).wait_recv()
                fwd = _remote(got, got, d2d_send.at[3 * a + j], d2d_recv.at[3 * a + j], sibling)
                fwd.start()
                passed.append(fwd)
        for j, (px, py) in enumerate(chips):
            got = cw_out.at[2 * px + py]
            _remote(got, got, ici_send.at[3 * n + j], ici_recv.at[3 * n + j], (px, py, c)).wait_recv()
        for a in range(n):
            for j, (px, py) in enumerate(chips):
                got = half(a, 2 * px + py, 1 - c)
                _remote(got, got, d2d_send.at[3 * a + j], d2d_recv.at[3 * a + j], sibling).wait_recv()
        for cp in sends + passed:
            cp.wait_send()

    return pl.pallas_call(
        body, name="gather_weights",
        in_specs=[ANY] * (n + 1), out_specs=[ANY] * (n + 1),
        out_shape=[jax.ShapeDtypeStruct(b.shape, b.dtype) for b in bufs],
        input_output_aliases={i: i for i in range(n + 1)},
        scratch_shapes=[pltpu.SemaphoreType.DMA((3 * n + 3,)), pltpu.SemaphoreType.DMA((3 * n + 3,)),
                        pltpu.SemaphoreType.DMA((3 * n,)), pltpu.SemaphoreType.DMA((3 * n,))],
        compiler_params=pltpu.CompilerParams(has_side_effects=True),
    )(*bufs)


def _swap_halves(grads, tag):
    n = len(grads)

    def body(*refs):
        ins, outs, send_sem, recv_sem = refs[:n], refs[n:2 * n], refs[2 * n], refs[2 * n + 1]
        x, y, c, _ = _position()
        copies = []
        for a in range(n):
            rows = ins[a].shape[1] // 2
            copies.append(_remote(ins[a].at[:, pl.ds((1 - c) * rows, rows), :], outs[a],
                                  send_sem.at[a], recv_sem.at[a], (x, y, 1 - c)))
        for cp in copies:
            cp.start()
        for cp in copies:
            cp.wait()

    return pl.pallas_call(
        body, name="swap_grad_halves_" + tag,
        in_specs=[ANY] * n, out_specs=[ANY] * n,
        out_shape=[jax.ShapeDtypeStruct((g.shape[0], g.shape[1] // 2, g.shape[2]), F32) for g in grads],
        scratch_shapes=[pltpu.SemaphoreType.DMA((n,)), pltpu.SemaphoreType.DMA((n,))],
        compiler_params=pltpu.CompilerParams(has_side_effects=True),
    )(*grads)


def _pair_sum(g, got, c_arr, name):
    n_sh, R, C = g.shape
    rows = R // 2

    def body(c_ref, g_ref, r_ref, f_ref, b_ref):
        t = g_ref[...] + r_ref[...]
        f_ref[...] = t
        b_ref[...] = t.astype(BF16)

    blk = pl.BlockSpec((1, rows, C), lambda i, c_ref: (i, 0, 0))
    return pl.pallas_call(
        body, name=name,
        grid_spec=pltpu.PrefetchScalarGridSpec(
            num_scalar_prefetch=1, grid=(n_sh,),
            in_specs=[pl.BlockSpec((1, rows, C), lambda i, c_ref: (i, c_ref[0], 0)), blk],
            out_specs=[blk, blk]),
        out_shape=[jax.ShapeDtypeStruct((n_sh, rows, C), F32), jax.ShapeDtypeStruct((n_sh, rows, C), BF16)],
        compiler_params=_params(("parallel",)),
    )(c_arr, g, got)


def _scatter_grads(sums_bf16, small_all):
    n = len(sums_bf16)

    def body(*refs):
        b_ins = refs[:n]
        recvs, sm = refs[n + 1:2 * n + 1], refs[2 * n + 1]
        ici_send, ici_recv, sm_send, sm_recv = refs[2 * n + 2:]
        x, y, c, chips = _position()
        me = 4 * x + 2 * y + c
        copies = []
        for a in range(n):
            for j, (px, py) in enumerate(chips):
                copies.append(_remote(b_ins[a].at[2 * px + py], recvs[a].at[j],
                                      ici_send.at[3 * a + j], ici_recv.at[3 * a + j], (px, py, c)))
        for k in range(1, 8):
            peer = (x ^ (k >> 2), y ^ ((k >> 1) & 1), c ^ (k & 1))
            copies.append(_remote(sm.at[me], sm.at[me], sm_send.at[k - 1], sm_recv.at[k - 1], peer))
        for cp in copies:
            cp.start()
        for cp in copies:
            cp.wait_send()
        for a in range(n):
            for j, (px, py) in enumerate(chips):
                _remote(recvs[a].at[j], recvs[a].at[j], ici_send.at[3 * a + j], ici_recv.at[3 * a + j],
                        (px, py, c)).wait_recv()
        for k in range(1, 8):
            peer = (x ^ (k >> 2), y ^ ((k >> 1) & 1), c ^ (k & 1))
            theirs = sm.at[4 * peer[0] + 2 * peer[1] + peer[2]]
            _remote(theirs, theirs, sm_send.at[k - 1], sm_recv.at[k - 1], peer).wait_recv()

    out = pl.pallas_call(
        body, name="scatter_grads",
        in_specs=[ANY] * (n + 1), out_specs=[ANY] * (n + 1),
        out_shape=[jax.ShapeDtypeStruct((3,) + b.shape[1:], BF16) for b in sums_bf16]
        + [jax.ShapeDtypeStruct(small_all.shape, F32)],
        input_output_aliases={n: n},
        scratch_shapes=[pltpu.SemaphoreType.DMA((3 * n,)), pltpu.SemaphoreType.DMA((3 * n,)),
                        pltpu.SemaphoreType.DMA((7,)), pltpu.SemaphoreType.DMA((7,))],
        compiler_params=pltpu.CompilerParams(has_side_effects=True),
    )(*sums_bf16, small_all)
    return out[:n], out[n]


def _shard_sum(sums_f32, recv, shard_arr, c_arr, name):
    _, rows, C = sums_f32.shape

    def body(s_ref, c_ref, o_ref, r_ref, t_ref):
        t_ref[...] = ((o_ref[0] + r_ref[0].astype(F32)) + r_ref[1].astype(F32)) + r_ref[2].astype(F32)

    return pl.pallas_call(
        body, name=name,
        grid_spec=pltpu.PrefetchScalarGridSpec(
            num_scalar_prefetch=2, grid=(1,),
            in_specs=[pl.BlockSpec((1, rows, C), lambda i, s_ref, c_ref: (s_ref[0], 0, 0)),
                      pl.BlockSpec((3, rows, C), lambda i, s_ref, c_ref: (0, 0, 0))],
            out_specs=pl.BlockSpec((rows, C), lambda i, s_ref, c_ref: (c_ref[0], 0))),
        out_shape=jax.ShapeDtypeStruct((2 * rows, C), F32),
        compiler_params=_params(("arbitrary",)),
    )(shard_arr, c_arr, sums_f32, recv)


def _join_halves(bufs):
    n = len(bufs)

    def body(*refs):
        outs, send_sem, recv_sem = refs[n:2 * n], refs[2 * n], refs[2 * n + 1]
        x, y, c, _ = _position()
        copies = []
        for a in range(n):
            rows = outs[a].shape[0] // 2
            mine = outs[a].at[pl.ds(c * rows, rows), :]
            copies.append(_remote(mine, mine, send_sem.at[a], recv_sem.at[a], (x, y, 1 - c)))
        for cp in copies:
            cp.start()
        for a, cp in enumerate(copies):
            cp.wait_send()
            rows = outs[a].shape[0] // 2
            theirs = outs[a].at[pl.ds((1 - c) * rows, rows), :]
            _remote(theirs, theirs, send_sem.at[a], recv_sem.at[a], (x, y, 1 - c)).wait_recv()

    return pl.pallas_call(
        body, name="join_grad_halves",
        in_specs=[ANY] * n, out_specs=[ANY] * n,
        out_shape=[jax.ShapeDtypeStruct(b.shape, F32) for b in bufs],
        input_output_aliases={i: i for i in range(n)},
        scratch_shapes=[pltpu.SemaphoreType.DMA((n,)), pltpu.SemaphoreType.DMA((n,))],
        compiler_params=pltpu.CompilerParams(has_side_effects=True),
    )(*bufs)


def _small_sum(parts):
    _, R, C = parts.shape

    def body(p_ref, o_ref):
        t = p_ref[0]
        for k in range(1, 8):
            t = t + p_ref[k]
        o_ref[...] = t

    return pl.pallas_call(
        body, name="small_grad_sum", grid=(1,),
        in_specs=[pl.BlockSpec((8, R, C), lambda i: (0, 0, 0))], out_specs=pl.BlockSpec((R, C), lambda i: (0, 0)),
        out_shape=jax.ShapeDtypeStruct((R, C), F32), compiler_params=_params(("arbitrary",)),
    )(parts)


def _adamw_math(w, g, m, v):
    m = ADAM_B1 * m + (1.0 - ADAM_B1) * g
    v = ADAM_B2 * v + (1.0 - ADAM_B2) * (g * g)
    m_hat = m / (1.0 - ADAM_B1 ** ADAM_STEP)
    v_hat = v / (1.0 - ADAM_B2 ** ADAM_STEP)
    delta = -ADAM_LR * (m_hat / (jnp.sqrt(v_hat) + ADAM_EPS) + ADAM_WD * w)
    return delta, m, v


def _adamw_big(w, g, m, v, name):
    R, C = w.shape
    tr = R // 4

    def body(w_ref, g_ref, m_ref, v_ref, d_ref, nm_ref, nv_ref):
        d_ref[...], nm_ref[...], nv_ref[...] = _adamw_math(w_ref[...], g_ref[...], m_ref[...], v_ref[...])

    blk = pl.BlockSpec((tr, C), lambda i: (i, 0))
    return pl.pallas_call(
        body, name=name, grid=(4,), in_specs=[blk] * 4, out_specs=[blk] * 3,
        out_shape=[jax.ShapeDtypeStruct((R, C), F32)] * 3, compiler_params=_params(("parallel",)),
    )(w, g, m, v)


def _adamw_small(ws, gs, ms, vs):
    n = len(ws)

    def body(*refs):
        for a in range(n):
            w, g, m, v = (refs[k * n + a][...] for k in range(4))
            d, nm, nv = _adamw_math(w, g, m, v)
            refs[4 * n + a][...] = d
            refs[5 * n + a][...] = nm
            refs[6 * n + a][...] = nv

    shapes = [jax.ShapeDtypeStruct(w.shape, F32) for w in ws]
    out = pl.pallas_call(body, name="adamw_small", out_shape=shapes * 3)(*ws, *gs, *ms, *vs)
    return out[:n], out[n:2 * n], out[2 * n:]


BIG = ("w_in", "w_out", "w_up", "w_down")
SMALL = ("g_mix_pre", "pool_w", "pool_scale", "g_mix_post", "g_ffn_pre", "conv_b", "g_ffn_post", "conv_w")
ORDER = ("g_mix_pre", "w_in", "pool_w", "pool_scale", "w_out", "g_mix_post", "g_ffn_pre", "w_up", "conv_w", "conv_b",
         "w_down", "g_ffn_post")


def kernel(x, g_mix_pre, w_in, pool_w, pool_scale, w_out, g_mix_post, g_ffn_pre, w_up, conv_w, conv_b, w_down, g_ffn_post, loss_target, m_g_mix_pre, m_w_in, m_pool_w, m_pool_scale, m_w_out, m_g_mix_post, m_g_ffn_pre, m_w_up, m_conv_w, m_conv_b, m_w_down, m_g_ffn_post, v_g_mix_pre, v_w_in, v_pool_w, v_pool_scale, v_w_out, v_g_mix_post, v_g_ffn_pre, v_w_up, v_conv_w, v_conv_b, v_w_down, v_g_ffn_post):
    args = dict(locals())
    W = {n: args[n][0] for n in ORDER}
    M = {n: args["m_" + n][0] for n in ORDER}
    V = {n: args["v_" + n][0] for n in ORDER}
    for d in (W, M, V):
        d["pool_w"] = d["pool_w"].reshape(-1, POOL_GROUP)
        for n in ("g_mix_pre", "pool_scale", "g_mix_post", "g_ffn_pre", "conv_b", "g_ffn_post"):
            d[n] = d[n].reshape(1, -1)
    CW = W["w_up"].shape[1]
    c_arr = lax.axis_index("c").astype(jnp.int32).reshape(1)
    shard = 2 * lax.axis_index("x") + lax.axis_index("y")
    shard_arr = shard.astype(jnp.int32).reshape(1)
    device = 2 * shard + lax.axis_index("c")

    conv_w_slots = lax.dynamic_update_index_in_dim(jnp.zeros((N_SHARD,) + W["conv_w"].shape, F32), W["conv_w"], shard, 0)
    slots = {n: _cast_bf16(W[n], shard_arr, "cast_" + n) for n in BIG}
    w_in_g, conv_w_g = _gather_weights([slots["w_in"], conv_w_slots])
    conv_w_full = conv_w_g.transpose(1, 0, 2).reshape(CONV_WIDTH, 1, N_SHARD * CW)

    loss, grad_x, G = _local_step(
        x[0], loss_target[0], W["g_mix_pre"], w_in_g, W["pool_w"].reshape(-1, POOL_GROUP, POOL_GROUP), W["pool_scale"],
        slots["w_out"], W["g_mix_post"], W["g_ffn_pre"], slots["w_up"], conv_w_full, W["conv_b"],
        slots["w_down"], W["g_ffn_post"], c_arr)

    late = ("w_in", "w_out")
    from_sibling = _swap_halves([G[n] for n in late], "mix")
    sums = {n: _pair_sum(G[n], r, c_arr, "pair_sum_" + n) for n, r in zip(late, from_sibling)}
    G["conv_w"] = G["conv_w"].reshape(CONV_WIDTH, N_SHARD, CW).transpose(1, 0, 2)
    G["pool_w"] = G["pool_w"].reshape(-1, POOL_GROUP)
    small = jnp.concatenate([G[n].reshape(-1, LANES) for n in SMALL], axis=0)
    small_slots = lax.dynamic_update_index_in_dim(jnp.zeros((8,) + small.shape, F32), small, device, 0)
    recvs, small_all = _scatter_grads([sums[n][1] for n in late], small_slots)
    reduced = {n: (sums[n][0], r) for n, r in zip(late, recvs)}
    reduced.update({n: G[n] for n in ("w_up", "w_down")})
    halves = [_shard_sum(*reduced[n], shard_arr, c_arr, "shard_sum_" + n) for n in BIG]
    full = dict(zip(BIG, _join_halves(halves)))

    small_total = _small_sum(small_all)
    row = 0
    for n in SMALL:
        rows = G[n].size // LANES
        g = small_total[row:row + rows]
        row += rows
        if n == "conv_w":
            g = lax.dynamic_slice_in_dim(g.reshape(N_SHARD, CONV_WIDTH, CW), shard, 1, axis=0)[0]
        full[n] = g.reshape(W[n].shape)

    delta, new_m, new_v = {}, {}, {}
    for n in BIG:
        delta[n], new_m[n], new_v[n] = _adamw_big(W[n], full[n], M[n], V[n], "adamw_" + n)
    ds, nms, nvs = _adamw_small([W[n] for n in SMALL], [full[n] for n in SMALL], [M[n] for n in SMALL],
                                [V[n] for n in SMALL])
    for n, d, nm, nv in zip(SMALL, ds, nms, nvs):
        delta[n], new_m[n], new_v[n] = d, nm, nv

    loss = lax.psum(loss[0, 0], ("x", "y", "c"))
    shaped = lambda d: [d[n].reshape(args[n].shape) for n in ORDER]
    return (loss, grad_x[None], *shaped(full), *shaped(delta), *shaped(new_m), *shaped(new_v))
```

```python
import functools

import jax
import jax.numpy as jnp
from jax import lax
from jax.experimental import pallas as pl
from jax.experimental.pallas import tpu as pltpu

F32 = jnp.float32
BF16 = jnp.bfloat16

RMS_EPS = 1e-6
NEG_INF = -1e30
N_HEADS = 8
HEAD_DIM = 64
ATTN_WIDTH = N_HEADS * HEAD_DIM
ATTN_SCALE = HEAD_DIM ** -0.5
ATTN_BLOCK = 128
DILATIONS = (1, 4, 16)
POOL_WINDOWS = (2, 4, 8, 16)
POOL_GROUP = 128
POOL_WIDTH = POOL_GROUP * len(POOL_WINDOWS)
POOL_HALO = 16
CONV_WIDTH = 3
CONV_HALO = 8
N_SHARD = 4
LANES = 128

ADAM_LR = 0.001
ADAM_B1 = 0.9
ADAM_B2 = 0.999
ADAM_EPS = 1e-08
ADAM_WD = 0.01
ADAM_STEP = 10

VMEM_LIMIT = 60 * 1024 * 1024
MESH = pl.DeviceIdType.MESH
NT = (((1,), (1,)), ((), ()))
TN = (((0,), (0,)), ((), ()))


def _params(sem, vmem=None):
    return pltpu.CompilerParams(dimension_semantics=sem, vmem_limit_bytes=vmem)


def _const_spec(shape):
    zeros = (0,) * len(shape)
    return pl.BlockSpec(shape, lambda *_: zeros, pipeline_mode=pl.Buffered(1))


def _dot(a, b):
    return jnp.dot(a, b, preferred_element_type=F32)


def _dot_nt(a, b):
    return lax.dot_general(a, b, NT, preferred_element_type=F32)


def _dot_tn(a, b):
    return lax.dot_general(a, b, TN, preferred_element_type=F32)


def _rms_stats(x):
    r = lax.rsqrt(jnp.mean(x * x, axis=-1, keepdims=True) + RMS_EPS)
    return x * r, r


def _rms_bwd(dy, n, r, g):
    dg = jnp.sum(dy * n, axis=0, keepdims=True)
    dn = dy * g
    dx = r * (dn - n * jnp.mean(dn * n, axis=-1, keepdims=True))
    return dx, dg


def _gelu_tanh(g):
    k = 0.7978845608028654
    kc = k * 0.044715
    g2 = g * g
    t = jnp.tanh(g * (k + kc * g2))
    h = 0.5 * t + 0.5
    dh = (0.5 - 0.5 * (t * t)) * (k + (3.0 * kc) * g2)
    return g * h, h + g * dh


def _residue_shape(S, d, dtype):
    return jax.ShapeDtypeStruct((S // d, d * ATTN_WIDTH), dtype)


def _residue_spec(TM, d):
    return pl.BlockSpec((TM // d, d * ATTN_WIDTH), lambda i: (i, 0))


def _token_scratch(TM):
    return [pltpu.VMEM((TM, LANES), F32)] * (ATTN_WIDTH // LANES)


def _put_tokens(dst_s, val):
    for cb, chunk in enumerate(dst_s):
        chunk[...] = val[:, cb * LANES:(cb + 1) * LANES]


def _get_tokens(src_s):
    return jnp.concatenate([chunk[...] for chunk in src_s], axis=1)


def _to_residue(val, src_s, out_ref, d, dtype):
    if d == 1:
        out_ref[...] = val.astype(dtype)
        return
    rows = src_s[0].shape[0]
    for r in range(d):
        for cb, chunk in enumerate(src_s):
            col = r * ATTN_WIDTH + cb * LANES
            out_ref[:, col:col + LANES] = chunk[pl.ds(r, rows // d, stride=d), :].astype(dtype)


def _from_residue(in_ref, dst_s, d):
    if d == 1:
        return in_ref[...].astype(F32)
    rows = dst_s[0].shape[0]
    for r in range(d):
        for cb, chunk in enumerate(dst_s):
            col = r * ATTN_WIDTH + cb * LANES
            chunk[pl.ds(r, rows // d, stride=d), :] = in_ref[:, col:col + LANES].astype(F32)
    return _get_tokens(dst_s)


def _mix_in_fwd(x, g_pre, w_in):
    S, D = x.shape
    TM = 512

    def body(x_ref, g_ref, w_ref, *refs):
        qkv_refs, p_ref, h_ref, t_s = refs[:9], refs[9], refs[10], refs[11:]
        n, _ = _rms_stats(x_ref[...])
        hb = (n * g_ref[...]).astype(BF16)
        h_ref[...] = hb
        for a in range(3):
            res = _dot(hb, w_ref[a])
            if a == 0:
                res = res * ATTN_SCALE
            _put_tokens(t_s, res)
            for i, d in enumerate(DILATIONS):
                _to_residue(res, t_s, qkv_refs[3 * i + a], d, BF16)
        p_ref[...] = _dot(hb, w_ref[3])

    row = lambda w: pl.BlockSpec((TM, w), lambda i: (i, 0))
    out = pl.pallas_call(
        body, name="mix_in_fwd", grid=(S // TM,),
        in_specs=[row(D), _const_spec((1, D)), _const_spec(w_in.shape)],
        out_specs=[_residue_spec(TM, d) for d in DILATIONS for _ in range(3)] + [row(POOL_WIDTH), row(D)],
        out_shape=[_residue_shape(S, d, BF16) for d in DILATIONS for _ in range(3)]
        + [jax.ShapeDtypeStruct((S, POOL_WIDTH), F32), jax.ShapeDtypeStruct((S, D), BF16)],
        scratch_shapes=_token_scratch(TM),
        compiler_params=_params(("parallel",), VMEM_LIMIT),
    )(x, g_pre, w_in)
    return [out[0:3], out[3:6], out[6:9]], out[9], out[10]


def _band_mask(n):
    qi = lax.broadcasted_iota(jnp.int32, (ATTN_BLOCK, 2 * ATTN_BLOCK), 0)
    ki = lax.broadcasted_iota(jnp.int32, (ATTN_BLOCK, 2 * ATTN_BLOCK), 1)
    dist = qi + ATTN_BLOCK - ki
    return (dist >= 0) & (dist <= ATTN_BLOCK) & ((ki >= ATTN_BLOCK) | (n > 0))


def _first_head_lanes():
    return lax.broadcasted_iota(jnp.int32, (1, LANES), 1) < HEAD_DIM


def _stack_heads(pair, first):
    zero = jnp.zeros_like(pair)
    return jnp.concatenate([jnp.where(first, pair, zero), jnp.where(first, zero, pair)], axis=0)


def _unstack_heads(stacked, first):
    return jnp.where(first, stacked[:ATTN_BLOCK], stacked[ATTN_BLOCK:])


CARGO_COPIES = {"ici": 3, "d2d": 3, "scatter": 3, "swap": 1, "everyone": 7}
CARGO_IN_PLACE = ("ici", "d2d", "everyone")


def _cargo_copies(kinds, ins, outs, send_sems, recv_sems, want_recvs=True):
    x, y, c, chips = _position()
    s = 2 * x + y
    me = 2 * s + c
    sibling = (x, y, 1 - c)
    sends, recvs = [], []

    def add(k, src, dst, landing, to):
        sends.append(_remote(src, dst, send_sems.at[k], recv_sems.at[k], to))
        if want_recvs:
            recvs.append(_remote(landing, landing, send_sems.at[k], recv_sems.at[k], to))

    k0 = 0
    for a, kind in enumerate(kinds):
        if kind == "swap":
            rows = ins[a].shape[1] // 2
            add(k0, ins[a].at[:, pl.ds((1 - c) * rows, rows), :], outs[a], outs[a], sibling)
        elif kind == "everyone":
            for m in range(1, 8):
                peer = (x ^ (m >> 2), y ^ ((m >> 1) & 1), c ^ (m & 1))
                add(k0 + m - 1, outs[a].at[me], outs[a].at[me], outs[a].at[4 * peer[0] + 2 * peer[1] + peer[2]], peer)
        else:
            for j, (px, py) in enumerate(chips):
                sj = 2 * px + py
                if kind == "scatter":
                    add(k0 + j, ins[a].at[sj], outs[a].at[j], outs[a].at[j], (px, py, c))
                    continue
                buf = outs[a]
                rows = buf.shape[1] // 2
                half = lambda shard, h: buf.at[shard, pl.ds(h * rows, rows), :]
                if kind == "ici":
                    add(k0 + j, half(s, c), half(s, c), half(sj, c), (px, py, c))
                else:
                    add(k0 + j, half(sj, c), half(sj, c), half(sj, 1 - c), sibling)
        k0 += CARGO_COPIES[kind]
    return sends, recvs


def _cargo_start(kinds, ins, outs, sems, first_step):
    @pl.when(first_step)
    def _():
        for cp in _cargo_copies(kinds, ins, outs, *sems, want_recvs=False)[0]:
            cp.start()


def _cargo_finish(kinds, ins, outs, sems, last_step):
    @pl.when(last_step)
    def _():
        sends, recvs = _cargo_copies(kinds, ins, outs, *sems)
        for cp in sends:
            cp.wait_send()
        for cp in recvs:
            cp.wait_recv()


def _cargo_call(cargo, n_in, n_out):
    arrays = [a for _, a in cargo]
    shapes = []
    for kind, a in cargo:
        shape = {"scatter": (3,) + a.shape[1:], "swap": (a.shape[0], a.shape[1] // 2, a.shape[2])}.get(kind, a.shape)
        shapes.append(jax.ShapeDtypeStruct(shape, a.dtype))
    aliases = {n_in + i: n_out + i for i, (kind, _) in enumerate(cargo) if kind in CARGO_IN_PLACE}
    n_sems = sum(CARGO_COPIES[kind] for kind, _ in cargo)
    sems = [pltpu.SemaphoreType.DMA((n_sems,))] * 2 if cargo else []
    return arrays, [ANY] * len(cargo), shapes, aliases, sems


def _attn_fwd(q, k, v, d, cargo=()):
    L = q.shape[0]
    nb = L // ATTN_BLOCK
    nc = len(cargo)
    kinds = [kind for kind, _ in cargo]

    def body(*refs):
        q_ref, kp_ref, kc_ref, vp_ref, vc_ref = refs[:5]
        o_ref, lse_ref = refs[5 + nc:7 + nc]
        cargo_refs = (kinds, refs[5:5 + nc], refs[7 + nc:7 + 2 * nc], refs[7 + 2 * nc:])
        r, n = pl.program_id(0), pl.program_id(1)
        if nc:
            _cargo_start(*cargo_refs, (r == 0) & (n == 0))
        valid = _band_mask(pl.program_id(1))
        valid2 = jnp.concatenate([valid, valid], axis=0)
        first = _first_head_lanes()
        for hp in range(N_HEADS // 2):
            sl = slice(hp * LANES, (hp + 1) * LANES)
            kk = jnp.concatenate([kp_ref[:, sl], kc_ref[:, sl]], axis=0)
            vv = jnp.concatenate([vp_ref[:, sl], vc_ref[:, sl]], axis=0)
            s = jnp.where(valid2, _dot_nt(_stack_heads(q_ref[:, sl], first), kk), NEG_INF)
            m = jnp.max(s, axis=-1, keepdims=True)
            p = jnp.exp(s - m)
            den = jnp.sum(p, axis=-1, keepdims=True)
            o_ref[:, sl] = _unstack_heads(_dot(p.astype(BF16), vv) / den, first)
            lse_ref[:, sl] = _unstack_heads(m + jnp.log(den), first)
        if nc:
            _cargo_finish(*cargo_refs, (r == d - 1) & (n == nb - 1))

    cur = pl.BlockSpec((ATTN_BLOCK, ATTN_WIDTH), lambda r, n: (n, r))
    prev = pl.BlockSpec((ATTN_BLOCK, ATTN_WIDTH), lambda r, n: (jnp.maximum(n - 1, 0), r))
    arrays, specs, shapes, aliases, sems = _cargo_call(cargo, 5, 2)
    out = pl.pallas_call(
        body, name=f"attn_fwd_d{d}", grid=(d, nb),
        in_specs=[cur, prev, cur, prev, cur] + specs,
        out_specs=[cur, cur] + specs,
        out_shape=[jax.ShapeDtypeStruct((L, d * ATTN_WIDTH), F32)] * 2 + shapes,
        input_output_aliases=aliases, scratch_shapes=sems,
        compiler_params=_params(("arbitrary", "arbitrary")),
    )(q, k, k, v, v, *arrays)
    return out[0], out[1], out[2:]


def _attn_mix(outs, lses):
    S = outs[0].shape[0]
    TM = 512
    n = len(DILATIONS)

    def body(*refs):
        o_refs, l_refs, attn_ref, lse_refs, t_s = refs[:n], refs[n:2 * n], refs[2 * n], refs[2 * n + 1:3 * n + 1], refs[3 * n + 1:]
        os = [_from_residue(o_refs[i], t_s, d) for i, d in enumerate(DILATIONS)]
        ls = [_from_residue(l_refs[i], t_s, d) for i, d in enumerate(DILATIONS)]
        m = jnp.maximum(jnp.maximum(ls[0], ls[1]), ls[2])
        es = [jnp.exp(l - m) for l in ls]
        den = es[0] + es[1] + es[2]
        attn_ref[...] = (es[0] * os[0] + es[1] * os[1] + es[2] * os[2]) / den
        lse = m + jnp.log(den)
        _put_tokens(t_s, lse)
        for i, d in enumerate(DILATIONS):
            _to_residue(lse, t_s, lse_refs[i], d, F32)

    specs = [_residue_spec(TM, d) for d in DILATIONS]
    out = pl.pallas_call(
        body, name="attn_mix", grid=(S // TM,),
        in_specs=specs * 2, out_specs=[specs[0]] + specs,
        out_shape=[jax.ShapeDtypeStruct((S, ATTN_WIDTH), F32)] + [_residue_shape(S, d, F32) for d in DILATIONS],
        scratch_shapes=_token_scratch(TM),
        compiler_params=_params(("parallel",)),
    )(*outs, *lses)
    return out[0], out[1:]


def _pool_counts(first_row, rows, w):
    t = first_row + lax.broadcasted_iota(jnp.int32, (rows, 1), 0)
    return jnp.minimum(t + 1, w).astype(F32)


def _trailing_sums(xe, w):
    s, k = xe, 1
    while k < w:
        s = s + pltpu.roll(s, k, 0)
        k *= 2
    return s


def _leading_sums(xe, w):
    rows = xe.shape[0]
    s, k = xe, 1
    while k < w:
        s = s + pltpu.roll(s, rows - k, 0)
        k *= 2
    return s


def _pooled_groups(halo, cur, first_row):
    TM = cur.shape[0]
    xe = jnp.concatenate([halo, cur], axis=0)
    out = []
    for g, w in enumerate(POOL_WINDOWS):
        a = xe[:, g * POOL_GROUP:(g + 1) * POOL_GROUP]
        s = _trailing_sums(a, w)[POOL_HALO:]
        out.append(s / _pool_counts(first_row, TM, w) - a[POOL_HALO:])
    return out


def _pool_fwd(pool_in, pool_w, pool_scale):
    S = pool_in.shape[0]
    TM = 512
    HB = TM // POOL_HALO

    def body(cur_ref, halo_ref, w_ref, sc_ref, y_ref):
        i = pl.program_id(0)
        halo = jnp.where(i > 0, halo_ref[...], 0.0)
        pooled = _pooled_groups(halo, cur_ref[...], i * TM)
        for g in range(len(POOL_WINDOWS)):
            sl = slice(g * POOL_GROUP, (g + 1) * POOL_GROUP)
            y = _dot(pooled[g].astype(BF16), w_ref[g].astype(BF16)) * sc_ref[:, sl]
            y_ref[:, sl] = y.astype(BF16)

    return pl.pallas_call(
        body, name="pool_fwd", grid=(S // TM,),
        in_specs=[pl.BlockSpec((TM, POOL_WIDTH), lambda i: (i, 0)),
                  pl.BlockSpec((POOL_HALO, POOL_WIDTH), lambda i: (jnp.maximum(i * HB - 1, 0), 0)),
                  _const_spec(pool_w.shape), _const_spec((1, POOL_WIDTH))],
        out_specs=pl.BlockSpec((TM, POOL_WIDTH), lambda i: (i, 0)),
        out_shape=jax.ShapeDtypeStruct((S, POOL_WIDTH), BF16),
        compiler_params=_params(("parallel",)),
    )(pool_in, pool_in, pool_w, pool_scale)


def _mix_out_fwd(attn, pool, w_out, x, g_post, g_ffn_pre):
    S, D = x.shape
    TM = 512

    def body(a_ref, p_ref, w_ref, x_ref, gp_ref, gf_ref, mixed_ref, x1_ref, h2_ref, cat_ref):
        ab = a_ref[...].astype(BF16)
        cat_ref[:, :ATTN_WIDTH] = ab
        cat_ref[:, ATTN_WIDTH:] = p_ref[...]
        mixed = _dot(ab, w_ref[:ATTN_WIDTH, :]) + _dot(p_ref[...], w_ref[ATTN_WIDTH:, :])
        mixed_ref[...] = mixed
        n, _ = _rms_stats(mixed)
        x1 = x_ref[...] + n * gp_ref[...]
        x1_ref[...] = x1
        n2, _ = _rms_stats(x1)
        h2_ref[...] = (n2 * gf_ref[...]).astype(BF16)

    row = lambda w: pl.BlockSpec((TM, w), lambda i: (i, 0))
    return pl.pallas_call(
        body, name="mix_out_fwd", grid=(S // TM,),
        in_specs=[row(ATTN_WIDTH), row(POOL_WIDTH), _const_spec(w_out.shape), row(D),
                  _const_spec((1, D)), _const_spec((1, D))],
        out_specs=[row(D), row(D), row(D), row(D)],
        out_shape=[jax.ShapeDtypeStruct((S, D), F32), jax.ShapeDtypeStruct((S, D), F32),
                   jax.ShapeDtypeStruct((S, D), BF16), jax.ShapeDtypeStruct((S, D), BF16)],
        compiler_params=_params(("parallel",), VMEM_LIMIT),
    )(attn, pool, w_out, x, g_post, g_ffn_pre)


def _ffn_fwd(h2, x1, target, w_up, w_down, conv_w, conv_b, g_post):
    S, D = x1.shape
    CW = w_up.shape[2]
    FF = 2 * CW
    TM = 256
    piece = 4 * LANES
    pieces = [(lo, min(lo + piece, CW)) for lo in range(0, CW, piece)]

    def body(h2_ref, x1_ref, t_ref, wu_ref, wd_ref, cw_ref, cb_ref, g_ref,
             yv_ref, dy_ref, df_ref, dc_ref, loss_ref, dg_ref, dcb_ref, dcw_ref,
             ue_s, dgate_s, dval_s):
        i = pl.program_id(0)

        @pl.when(i == 0)
        def _():
            loss_ref[...] = jnp.zeros_like(loss_ref)
            dg_ref[...] = jnp.zeros_like(dg_ref)
            dcb_ref[...] = jnp.zeros_like(dcb_ref)
            dcw_ref[...] = jnp.zeros_like(dcw_ref)
            ue_s[0:CONV_HALO, :] = jnp.zeros((CONV_HALO, 2 * FF), F32)

        @pl.when(i > 0)
        def _():
            ue_s[0:CONV_HALO, :] = ue_s[TM:TM + CONV_HALO, :]

        def shifted(cols, k):
            return pltpu.roll(ue_s[:, cols], k, 0)[CONV_HALO:]

        def conv(cols):
            return (cb_ref[:, cols] + cw_ref[2, :, cols] * ue_s[CONV_HALO:, cols]
                    + cw_ref[1, :, cols] * shifted(cols, 1) + cw_ref[0, :, cols] * shifted(cols, 2))

        hb = h2_ref[...]
        f = jnp.zeros((TM, D), F32)
        for j in range(2):
            jc = slice(j * CW, (j + 1) * CW)
            for half in range(2):
                blk = 2 * half + j
                cols = slice(blk * CW, (blk + 1) * CW)
                ue_s[CONV_HALO:, cols] = _dot(hb, wu_ref[blk])
            for lo, hi in pieces:
                pc = slice(j * CW + lo, j * CW + hi)
                gelu, dgelu = _gelu_tanh(conv(pc))
                val = conv(slice(FF + j * CW + lo, FF + j * CW + hi))
                dgate_s[:, pc] = val * dgelu
                dval_s[:, pc] = gelu
                yv_ref[:, pc] = (gelu * val).astype(BF16)
            f = f + _dot(yv_ref[:, jc], wd_ref[jc, :])

        n, r = _rms_stats(f)
        err = x1_ref[...] + n * g_ref[...] - t_ref[...]
        loss_ref[...] += 0.5 * jnp.sum(jnp.mean(err * err, axis=-1, keepdims=True), axis=0, keepdims=True)
        dy = err / D
        dy_ref[...] = dy
        df, dg = _rms_bwd(dy, n, r, g_ref[...])
        dg_ref[...] += dg
        dfb = df.astype(BF16)
        df_ref[...] = dfb

        for j in range(2):
            jc = slice(j * CW, (j + 1) * CW)
            dyv = _dot_nt(dfb, wd_ref[jc, :])
            for lo, hi in pieces:
                pc = slice(j * CW + lo, j * CW + hi)
                for half, scale_s in ((0, dgate_s), (1, dval_s)):
                    cols = slice(half * FF + j * CW + lo, half * FF + j * CW + hi)
                    dcv = dyv[:, lo:hi] * scale_s[:, pc]
                    dc_ref[:, cols] = dcv.astype(BF16)
                    dcb_ref[:, cols] += jnp.sum(dcv, axis=0, keepdims=True)
                    dcw_ref[2, :, cols] += jnp.sum(dcv * ue_s[CONV_HALO:, cols], axis=0, keepdims=True)
                    dcw_ref[1, :, cols] += jnp.sum(dcv * shifted(cols, 1), axis=0, keepdims=True)
                    dcw_ref[0, :, cols] += jnp.sum(dcv * shifted(cols, 2), axis=0, keepdims=True)

    row = lambda w: pl.BlockSpec((TM, w), lambda i: (i, 0))
    acc = lambda shape: pl.BlockSpec(shape, lambda i: (0,) * len(shape))
    return pl.pallas_call(
        body, name="ffn_fwd", grid=(S // TM,),
        in_specs=[row(D), row(D), row(D), _const_spec(w_up.shape), _const_spec(w_down.shape),
                  _const_spec(conv_w.shape), _const_spec((1, 2 * FF)), _const_spec((1, D))],
        out_specs=[row(FF), row(D), row(D), row(2 * FF),
                   acc((1, 1)), acc((1, D)), acc((1, 2 * FF)), acc((CONV_WIDTH, 1, 2 * FF))],
        out_shape=[jax.ShapeDtypeStruct((S, FF), BF16),
                   jax.ShapeDtypeStruct((S, D), F32), jax.ShapeDtypeStruct((S, D), BF16),
                   jax.ShapeDtypeStruct((S, 2 * FF), BF16),
                   jax.ShapeDtypeStruct((1, 1), F32), jax.ShapeDtypeStruct((1, D), F32),
                   jax.ShapeDtypeStruct((1, 2 * FF), F32), jax.ShapeDtypeStruct((CONV_WIDTH, 1, 2 * FF), F32)],
        scratch_shapes=[pltpu.VMEM((TM + CONV_HALO, 2 * FF), F32), pltpu.VMEM((TM, FF), F32),
                        pltpu.VMEM((TM, FF), F32)],
        compiler_params=_params(("arbitrary",), VMEM_LIMIT),
    )(h2, x1, target, w_up, w_down, conv_w, conv_b, g_post)


def _ffn_bwd(dc, conv_w, w_up, x1, g_ffn_pre, dy):
    S, D = x1.shape
    CW = w_up.shape[2]
    F2 = 4 * CW
    TM = 256
    HB = TM // CONV_HALO
    last = S // CONV_HALO - 1
    n_tiles = S // TM

    def body(dc_ref, halo_ref, cw_ref, wu_ref, x1_ref, g_ref, dy_ref, du_ref, dx1_ref, dg_ref):
        i = pl.program_id(0)

        @pl.when(i == 0)
        def _():
            dg_ref[...] = jnp.zeros_like(dg_ref)

        keep = i < n_tiles - 1
        dh2 = jnp.zeros((TM, D), F32)
        for blk in range(N_SHARD):
            cols = slice(blk * CW, (blk + 1) * CW)
            halo = jnp.where(keep, halo_ref[:, cols].astype(F32), 0.0)
            dce = jnp.concatenate([dc_ref[:, cols].astype(F32), halo], axis=0)
            rows = TM + CONV_HALO
            du = (cw_ref[2, :, cols] * dce[:TM]
                  + cw_ref[1, :, cols] * pltpu.roll(dce, rows - 1, 0)[:TM]
                  + cw_ref[0, :, cols] * pltpu.roll(dce, rows - 2, 0)[:TM])
            dub = du.astype(BF16)
            du_ref[:, cols] = dub
            dh2 = dh2 + _dot_nt(dub, wu_ref[blk])
        n2, r2 = _rms_stats(x1_ref[...])
        dx, dg = _rms_bwd(dh2, n2, r2, g_ref[...])
        dg_ref[...] += dg
        dx1_ref[...] = dy_ref[...] + dx

    row = lambda w: pl.BlockSpec((TM, w), lambda i: (i, 0))
    return pl.pallas_call(
        body, name="ffn_bwd", grid=(S // TM,),
        in_specs=[row(F2), pl.BlockSpec((CONV_HALO, F2), lambda i: (jnp.minimum((i + 1) * HB, last), 0)),
                  _const_spec(conv_w.shape), _const_spec(w_up.shape), row(D), _const_spec((1, D)), row(D)],
        out_specs=[row(F2), row(D), pl.BlockSpec((1, D), lambda i: (0, 0))],
        out_shape=[jax.ShapeDtypeStruct((S, F2), BF16), jax.ShapeDtypeStruct((S, D), F32),
                   jax.ShapeDtypeStruct((1, D), F32)],
        compiler_params=_params(("arbitrary",), VMEM_LIMIT),
    )(dc, dc, conv_w, w_up, x1, g_ffn_pre, dy)


def _matmul_tn(a, b, n_blocks, name):
    S, M = a.shape
    N = b.shape[1]
    tn = N // n_blocks
    tm = M if M <= 1024 else M // 2
    tk = 2048
    nk = S // tk

    def body(a_ref, b_ref, o_ref):
        @pl.when(pl.program_id(2) == 0)
        def _():
            o_ref[...] = jnp.zeros_like(o_ref)
        o_ref[0] += _dot_tn(a_ref[...], b_ref[...])

    return pl.pallas_call(
        body, name=name, grid=(M // tm, n_blocks, nk),
        in_specs=[pl.BlockSpec((tk, tm), lambda i, j, k: (k, i)), pl.BlockSpec((tk, tn), lambda i, j, k: (k, j))],
        out_specs=pl.BlockSpec((1, tm, tn), lambda i, j, k: (j, i, 0)),
        out_shape=jax.ShapeDtypeStruct((n_blocks, M, tn), F32),
        compiler_params=_params(("parallel", "parallel", "arbitrary"), VMEM_LIMIT),
    )(a, b)


def _mix_out_bwd(dx1, mixed, g_post, w_out, attn, cargo=()):
    S, D = dx1.shape
    TM = 512
    nd = len(DILATIONS)
    nc = len(cargo)
    kinds = [kind for kind, _ in cargo]
    n_chunks = ATTN_WIDTH // LANES

    def body(*refs):
        dx_ref, m_ref, g_ref, w_ref, a_ref = refs[:5]
        dm_ref, dp_ref, dg_ref = refs[5 + nc:8 + nc]
        da_refs, dl_refs = refs[8 + nc:8 + nc + nd], refs[8 + nc + nd:8 + nc + 2 * nd]
        n_out = 8 + nc + 2 * nd
        t_s = refs[n_out + nc:n_out + nc + n_chunks]
        cargo_refs = (kinds, refs[5:5 + nc], refs[n_out:n_out + nc], refs[n_out + nc + n_chunks:])
        if nc:
            _cargo_start(*cargo_refs, pl.program_id(0) == 0)

        @pl.when(pl.program_id(0) == 0)
        def _():
            dg_ref[...] = jnp.zeros_like(dg_ref)

        n, r = _rms_stats(m_ref[...])
        dm, dg = _rms_bwd(dx_ref[...], n, r, g_ref[...])
        dg_ref[...] += dg
        dmb = dm.astype(BF16)
        dm_ref[...] = dmb
        da = _dot_nt(dmb, w_ref[:ATTN_WIDTH, :])
        _put_tokens(t_s, da)
        for i, d in enumerate(DILATIONS):
            _to_residue(da, t_s, da_refs[i], d, BF16)
        dp_ref[...] = _dot_nt(dmb, w_ref[ATTN_WIDTH:, :])
        prod = da * a_ref[...]
        hi = prod.astype(BF16)
        lo = (prod - hi.astype(F32)).astype(BF16)
        ri = lax.broadcasted_iota(jnp.int32, (ATTN_WIDTH, ATTN_WIDTH), 0) // HEAD_DIM
        ci = lax.broadcasted_iota(jnp.int32, (ATTN_WIDTH, ATTN_WIDTH), 1) // HEAD_DIM
        ones = (ri == ci).astype(BF16)
        delta = _dot(hi, ones) + _dot(lo, ones)
        _put_tokens(t_s, delta)
        for i, d in enumerate(DILATIONS):
            _to_residue(delta, t_s, dl_refs[i], d, F32)
        if nc:
            _cargo_finish(*cargo_refs, pl.program_id(0) == S // TM - 1)

    row = lambda w: pl.BlockSpec((TM, w), lambda i: (i, 0))
    specs = [_residue_spec(TM, d) for d in DILATIONS]
    arrays, cargo_specs, shapes, aliases, sems = _cargo_call(cargo, 5, 3 + 2 * nd)
    out = pl.pallas_call(
        body, name="mix_out_bwd", grid=(S // TM,),
        in_specs=[row(D), row(D), _const_spec((1, D)), _const_spec(w_out.shape), row(ATTN_WIDTH)] + cargo_specs,
        out_specs=[row(D), row(POOL_WIDTH), pl.BlockSpec((1, D), lambda i: (0, 0))] + specs * 2 + cargo_specs,
        out_shape=[jax.ShapeDtypeStruct((S, D), BF16), jax.ShapeDtypeStruct((S, POOL_WIDTH), F32),
                   jax.ShapeDtypeStruct((1, D), F32)]
        + [_residue_shape(S, d, BF16) for d in DILATIONS] + [_residue_shape(S, d, F32) for d in DILATIONS] + shapes,
        input_output_aliases=aliases,
        scratch_shapes=_token_scratch(TM) + sems,
        compiler_params=_params(("arbitrary",), VMEM_LIMIT),
    )(dx1, mixed, g_post, w_out, attn, *arrays)
    return out[0], out[1], out[2], out[3:3 + nd], out[3 + nd:3 + 2 * nd], out[3 + 2 * nd:]


def _pool_bwd(pool_in, d_pool, pool_w, pool_scale):
    S = pool_in.shape[0]
    TM = 512
    HB = TM // POOL_HALO
    last = S // POOL_HALO - 1
    G = len(POOL_WINDOWS)

    def body(cur_ref, halo_ref, dcur_ref, dnext_ref, w_ref, sc_ref, dxin_ref, dw_ref, dsc_ref):
        i = pl.program_id(0)

        @pl.when(i == 0)
        def _():
            dw_ref[...] = jnp.zeros_like(dw_ref)
            dsc_ref[...] = jnp.zeros_like(dsc_ref)

        halo = jnp.where(i > 0, halo_ref[...], 0.0)
        pooled = _pooled_groups(halo, cur_ref[...], i * TM)
        dnext = jnp.where(i < S // TM - 1, dnext_ref[...], 0.0)
        dye = jnp.concatenate([dcur_ref[...], dnext], axis=0)
        for g, w in enumerate(POOL_WINDOWS):
            sl = slice(g * POOL_GROUP, (g + 1) * POOL_GROUP)
            wg = w_ref[g].astype(BF16)
            pb = pooled[g].astype(BF16)
            dsc_ref[:, sl] += jnp.sum(dye[:TM, sl] * _dot(pb, wg), axis=0, keepdims=True)
            dpre = (dye[:, sl] * sc_ref[:, sl]).astype(BF16)
            dw_ref[g] += _dot_tn(pb, dpre[:TM])
            dpooled = _dot_nt(dpre, wg)
            z = dpooled / _pool_counts(i * TM, TM + POOL_HALO, w)
            dxin_ref[:, sl] = (_leading_sums(z, w)[:TM] - dpooled[:TM]).astype(BF16)

    row = pl.BlockSpec((TM, POOL_WIDTH), lambda i: (i, 0))
    return pl.pallas_call(
        body, name="pool_bwd", grid=(S // TM,),
        in_specs=[row, pl.BlockSpec((POOL_HALO, POOL_WIDTH), lambda i: (jnp.maximum(i * HB - 1, 0), 0)),
                  row, pl.BlockSpec((POOL_HALO, POOL_WIDTH), lambda i: (jnp.minimum((i + 1) * HB, last), 0)),
                  _const_spec(pool_w.shape), _const_spec((1, POOL_WIDTH))],
        out_specs=[row, pl.BlockSpec((G, POOL_GROUP, POOL_GROUP), lambda i: (0, 0, 0)),
                   pl.BlockSpec((1, POOL_WIDTH), lambda i: (0, 0))],
        out_shape=[jax.ShapeDtypeStruct((S, POOL_WIDTH), BF16), jax.ShapeDtypeStruct((G, POOL_GROUP, POOL_GROUP), F32),
                   jax.ShapeDtypeStruct((1, POOL_WIDTH), F32)],
        compiler_params=_params(("arbitrary",)),
    )(pool_in, pool_in, d_pool, d_pool, pool_w, pool_scale)


def _attn_bwd(q, k, v, d_attn, lse, delta, d, cargo=()):
    L = q.shape[0]
    nb = L // ATTN_BLOCK
    nc = len(cargo)
    kinds = [kind for kind, _ in cargo]

    def body(*refs):
        q_ref, kp_ref, kc_ref, vp_ref, vc_ref, do_ref, lse_ref, dl_ref = refs[:8]
        dq_ref, dk_ref, dv_ref = refs[8 + nc:11 + nc]
        ck_s, cv_s = refs[11 + 2 * nc:13 + 2 * nc]
        cargo_refs = (kinds, refs[8:8 + nc], refs[11 + nc:11 + 2 * nc], refs[13 + 2 * nc:])
        r, n = pl.program_id(0), pl.program_id(1)
        if nc:
            _cargo_start(*cargo_refs, (r == 0) & (n == 0))

        @pl.when(n == 0)
        def _():
            ck_s[...] = jnp.zeros_like(ck_s)
            cv_s[...] = jnp.zeros_like(cv_s)

        @pl.when(n < nb)
        def _():
            valid = _band_mask(n)
            valid2 = jnp.concatenate([valid, valid], axis=0)
            first = _first_head_lanes()

            def stacked_column(ref, lane):
                return jnp.concatenate([ref[:, lane:lane + 1], ref[:, lane + HEAD_DIM:lane + HEAD_DIM + 1]], axis=0)

            for hp in range(N_HEADS // 2):
                sl = slice(hp * LANES, (hp + 1) * LANES)
                qq = _stack_heads(q_ref[:, sl], first)
                dd = _stack_heads(do_ref[:, sl], first)
                kk = jnp.concatenate([kp_ref[:, sl], kc_ref[:, sl]], axis=0)
                vv = jnp.concatenate([vp_ref[:, sl], vc_ref[:, sl]], axis=0)
                s = _dot_nt(qq, kk)
                p = jnp.where(valid2, jnp.exp(s - stacked_column(lse_ref, hp * LANES)), 0.0)
                dp = _dot_nt(dd, vv)
                ds = (p * (dp - stacked_column(dl_ref, hp * LANES))).astype(BF16)
                dq_ref[:, sl] = (_unstack_heads(_dot(ds, kk), first) * ATTN_SCALE).astype(BF16)
                dk = _dot_tn(ds, qq)
                dv = _dot_tn(p.astype(BF16), dd)
                dk_ref[:, sl] = (ck_s[:, sl] + dk[:ATTN_BLOCK]).astype(BF16)
                dv_ref[:, sl] = (cv_s[:, sl] + dv[:ATTN_BLOCK]).astype(BF16)
                ck_s[:, sl] = dk[ATTN_BLOCK:]
                cv_s[:, sl] = dv[ATTN_BLOCK:]

        @pl.when(n == nb)
        def _():
            dk_ref[...] = ck_s[...].astype(BF16)
            dv_ref[...] = cv_s[...].astype(BF16)

        if nc:
            _cargo_finish(*cargo_refs, (r == d - 1) & (n == nb))

    blk = (ATTN_BLOCK, ATTN_WIDTH)
    cur = pl.BlockSpec(blk, lambda r, n: (jnp.minimum(n, nb - 1), r))
    prev = pl.BlockSpec(blk, lambda r, n: (jnp.maximum(jnp.minimum(n, nb - 1) - 1, 0), r))
    done = pl.BlockSpec(blk, lambda r, n: (jnp.maximum(n - 1, 0), r))
    arrays, specs, shapes, aliases, sems = _cargo_call(cargo, 8, 3)
    out = pl.pallas_call(
        body, name=f"attn_bwd_d{d}", grid=(d, nb + 1),
        in_specs=[cur, prev, cur, prev, cur, cur, cur, cur] + specs, out_specs=[cur, done, done] + specs,
        out_shape=[jax.ShapeDtypeStruct((L, d * ATTN_WIDTH), BF16)] * 3 + shapes,
        input_output_aliases=aliases,
        scratch_shapes=[pltpu.VMEM(blk, F32), pltpu.VMEM(blk, F32)] + sems,
        compiler_params=_params(("arbitrary", "arbitrary")),
    )(q, k, k, v, v, d_attn, lse, delta, *arrays)
    return out[:3], out[3:]


def _mix_in_bwd(dqkv, d_pool_in, w_in, x, g_pre, dx1):
    S, D = x.shape
    TM = 512
    nd = len(DILATIONS)

    def body(*refs):
        g_refs = refs[:3 * nd]
        dpi_ref, w_ref, x_ref, g_ref, dx1_ref, dproj_ref, gx_ref, dg_ref = refs[3 * nd:3 * nd + 8]
        t_s = refs[3 * nd + 8:]

        @pl.when(pl.program_id(0) == 0)
        def _():
            dg_ref[...] = jnp.zeros_like(dg_ref)

        dh = jnp.zeros((TM, D), F32)
        for a in range(4):
            if a < 3:
                tot = g_refs[a][...].astype(F32)
                for i, d in enumerate(DILATIONS[1:]):
                    tot = tot + _from_residue(g_refs[3 * (i + 1) + a], t_s, d)
                db = tot.astype(BF16)
            else:
                db = dpi_ref[...]
            dproj_ref[:, a * ATTN_WIDTH:(a + 1) * ATTN_WIDTH] = db
            dh = dh + _dot_nt(db, w_ref[a])
        n, r = _rms_stats(x_ref[...])
        dx, dg = _rms_bwd(dh, n, r, g_ref[...])
        dg_ref[...] += dg
        gx_ref[...] = dx1_ref[...] + dx

    row = lambda w: pl.BlockSpec((TM, w), lambda i: (i, 0))
    return pl.pallas_call(
        body, name="mix_in_bwd", grid=(S // TM,),
        in_specs=[_residue_spec(TM, d) for d in DILATIONS for _ in range(3)]
        + [row(POOL_WIDTH), _const_spec(w_in.shape), row(D), _const_spec((1, D)), row(D)],
        out_specs=[row(4 * ATTN_WIDTH), row(D), pl.BlockSpec((1, D), lambda i: (0, 0))],
        out_shape=[jax.ShapeDtypeStruct((S, 4 * ATTN_WIDTH), BF16), jax.ShapeDtypeStruct((S, D), F32),
                   jax.ShapeDtypeStruct((1, D), F32)],
        scratch_shapes=_token_scratch(TM),
        compiler_params=_params(("arbitrary",), VMEM_LIMIT),
    )(*[g for gs in dqkv for g in gs], d_pool_in, w_in, x, g_pre, dx1)


SMALL_EARLY = ("pool_w", "pool_scale", "g_mix_post", "g_ffn_pre", "conv_b", "g_ffn_post", "conv_w")
SMALL_LATE = ("g_mix_pre",)


def _pack_small(grads, names):
    parts = []
    for n in names:
        g = grads[n]
        if n == "conv_w":
            g = g.reshape(CONV_WIDTH, N_SHARD, -1).transpose(1, 0, 2)
        parts.append(g.reshape(-1, LANES))
    return jnp.concatenate(parts, axis=0) if len(parts) > 1 else parts[0]


def _unpack_small(packed, names, like, shard):
    out, row = {}, 0
    for n in names:
        size = like[n].size * (N_SHARD if n == "conv_w" else 1)
        g = packed[row:row + size // LANES]
        row += size // LANES
        if n == "conv_w":
            g = lax.dynamic_slice_in_dim(g.reshape((N_SHARD,) + like[n].shape), shard, 1, axis=0)[0]
        out[n] = g.reshape(like[n].shape)
    return out


def _local_step(x, target, g_mix_pre, w_in, pool_w, pool_scale, w_out, g_mix_post, g_ffn_pre,
                w_up, conv_w, conv_b, w_down, g_ffn_post, mesh_pos=None):
    on_mesh = mesh_pos is not None
    D = x.shape[1]
    CW = w_up.shape[2]
    qkv, pool_in, h1 = _mix_in_fwd(x, g_mix_pre, w_in)
    o1, l1, got = _attn_fwd(*qkv[0], 1, [("ici", w_out), ("ici", w_down)] if on_mesh else ())
    w_out, w_down = got if on_mesh else (w_out, w_down)
    o4, l4, got = _attn_fwd(*qkv[1], 4, [("d2d", w_out), ("d2d", w_down), ("ici", w_up)] if on_mesh else ())
    w_out, w_down, w_up = got if on_mesh else (w_out, w_down, w_up)
    o16, l16, got = _attn_fwd(*qkv[2], 16, [("d2d", w_up)] if on_mesh else ())
    w_up = got[0] if on_mesh else w_up
    w_out = w_out.reshape(D, D)
    w_down = w_down.reshape(2 * CW, D)
    attn, lse = _attn_mix((o1, o4, o16), (l1, l4, l16))
    pool = _pool_fwd(pool_in, pool_w, pool_scale)
    mixed, x1, h2, cat = _mix_out_fwd(attn, pool, w_out, x, g_mix_post, g_ffn_pre)

    yv, dy, df, dc, loss, d_g_ffn_post, d_conv_b, d_conv_w = _ffn_fwd(
        h2, x1, target, w_up, w_down, conv_w, conv_b, g_ffn_post)
    du, dx1, d_g_ffn_pre = _ffn_bwd(dc, conv_w, w_up, x1, g_ffn_pre, dy)
    d_w_up = _matmul_tn(h2, du, N_SHARD, "grad_w_up")
    d_w_down = _matmul_tn(yv, df, 1, "grad_w_down")[0].reshape(N_SHARD, CW // 2, D)
    swap = [("swap", d_w_up), ("swap", d_w_down)] if on_mesh else ()
    d_mixed, d_pool, d_g_mix_post, d_attn, delta, from_sibling = _mix_out_bwd(dx1, mixed, g_mix_post, w_out, attn, swap)
    d_w_out = _matmul_tn(cat, d_mixed, 1, "grad_w_out")[0].reshape(N_SHARD, D // N_SHARD, D)
    d_pool_in, d_pool_w, d_pool_scale = _pool_bwd(pool_in, d_pool, pool_w, pool_scale)
    grads = dict(pool_w=d_pool_w, pool_scale=d_pool_scale, w_out=d_w_out, g_mix_post=d_g_mix_post,
                 g_ffn_pre=d_g_ffn_pre, w_up=d_w_up, conv_w=d_conv_w, conv_b=d_conv_b, w_down=d_w_down,
                 g_ffn_post=d_g_ffn_post)
    cargo = [(), (), ()]
    if on_mesh:
        c_arr, device = mesh_pos
        up_f32, up_bf16 = _pair_sum(d_w_up, from_sibling[0], c_arr, "pair_sum_w_up")
        down_f32, down_bf16 = _pair_sum(d_w_down, from_sibling[1], c_arr, "pair_sum_w_down")
        early = _pack_small(grads, SMALL_EARLY)
        early_slots = lax.dynamic_update_index_in_dim(jnp.zeros((8,) + early.shape, F32), early, device, 0)
        cargo = [[("scatter", down_bf16)], [("scatter", up_bf16)], [("everyone", early_slots)]]

    dqkv, landed = zip(*[_attn_bwd(*qkv[i], d_attn[i], lse[i], delta[i], d, cargo[i]) for i, d in enumerate(DILATIONS)])
    if on_mesh:
        grads.update(w_down=(down_f32, landed[0][0]), w_up=(up_f32, landed[1][0]), small_early=landed[2][0])
    d_proj, grad_x, grads["g_mix_pre"] = _mix_in_bwd(dqkv, d_pool_in, w_in, x, g_mix_pre, dx1)
    grads["w_in"] = _matmul_tn(h1, d_proj, N_SHARD, "grad_w_in")
    return loss, grad_x, grads


ANY = pl.BlockSpec(memory_space=pl.ANY)


def _position():
    x, y, c = lax.axis_index("x"), lax.axis_index("y"), lax.axis_index("c")
    chips = [(1 - x, y), (x, 1 - y), (1 - x, 1 - y)]
    return x, y, c, chips


def _remote(src, dst, send_sem, recv_sem, to):
    return pltpu.make_async_remote_copy(src_ref=src, dst_ref=dst, send_sem=send_sem, recv_sem=recv_sem,
                                        device_id=to, device_id_type=MESH)


def _cast_bf16(w, shard_arr, name):
    R, C = w.shape
    tr = R // 2

    def body(s_ref, w_ref, o_ref):
        o_ref[0] = w_ref[...].astype(BF16)

    return pl.pallas_call(
        body, name=name,
        grid_spec=pltpu.PrefetchScalarGridSpec(
            num_scalar_prefetch=1, grid=(2,),
            in_specs=[pl.BlockSpec((tr, C), lambda i, s_ref: (i, 0))],
            out_specs=pl.BlockSpec((1, tr, C), lambda i, s_ref: (s_ref[0], i, 0))),
        out_shape=jax.ShapeDtypeStruct((N_SHARD, R, C), BF16),
        compiler_params=_params(("parallel",)))(shard_arr, w)


def _gather_weights(bufs):
    n = len(bufs) - 1

    def body(*refs):
        outs, cw_out = refs[n + 1:2 * n + 1], refs[2 * n + 1]
        ici_send, ici_recv, d2d_send, d2d_recv = refs[2 * n + 2:]
        x, y, c, chips = _position()
        s = 2 * x + y
        sibling = (x, y, 1 - c)

        def half(a, shard, h):
            rows = outs[a].shape[1] // 2
            return outs[a].at[shard, pl.ds(h * rows, rows), :]

        sends = []
        for a in range(n):
            for j, (px, py) in enumerate(chips):
                sends.append(_remote(half(a, s, c), half(a, s, c),
                                     ici_send.at[3 * a + j], ici_recv.at[3 * a + j], (px, py, c)))
        for j, (px, py) in enumerate(chips):
            sends.append(_remote(cw_out.at[s], cw_out.at[s], ici_send.at[3 * n + j], ici_recv.at[3 * n + j], (px, py, c)))
        for cp in sends:
            cp.start()
        passed = []
        for a in range(n):
            for j, (px, py) in enumerate(chips):
                sj = 2 * px + py
                got = half(a, sj, c)
                _remote(got, got, ici_send.at[3 * a + j], ici_recv.at[3 * a + j], (px, py, c)).wait_recv()
                fwd = _remote(got, got, d2d_send.at[3 * a + j], d2d_recv.at[3 * a + j], sibling)
                fwd.start()
                passed.append(fwd)
        for j, (px, py) in enumerate(chips):
            got = cw_out.at[2 * px + py]
            _remote(got, got, ici_send.at[3 * n + j], ici_recv.at[3 * n + j], (px, py, c)).wait_recv()
        for a in range(n):
            for j, (px, py) in enumerate(chips):
                got = half(a, 2 * px + py, 1 - c)
                _remote(got, got, d2d_send.at[3 * a + j], d2d_recv.at[3 * a + j], sibling).wait_recv()
        for cp in sends + passed:
            cp.wait_send()

    return pl.pallas_call(
        body, name="gather_weights",
        in_specs=[ANY] * (n + 1), out_specs=[ANY] * (n + 1),
        out_shape=[jax.ShapeDtypeStruct(b.shape, b.dtype) for b in bufs],
        input_output_aliases={i: i for i in range(n + 1)},
        scratch_shapes=[pltpu.SemaphoreType.DMA((3 * n + 3,)), pltpu.SemaphoreType.DMA((3 * n + 3,)),
                        pltpu.SemaphoreType.DMA((3 * n,)), pltpu.SemaphoreType.DMA((3 * n,))],
        compiler_params=pltpu.CompilerParams(has_side_effects=True),
    )(*bufs)


def _swap_halves(grads, tag):
    n = len(grads)

    def body(*refs):
        ins, outs, send_sem, recv_sem = refs[:n], refs[n:2 * n], refs[2 * n], refs[2 * n + 1]
        x, y, c, _ = _position()
        copies = []
        for a in range(n):
            rows = ins[a].shape[1] // 2
            copies.append(_remote(ins[a].at[:, pl.ds((1 - c) * rows, rows), :], outs[a],
                                  send_sem.at[a], recv_sem.at[a], (x, y, 1 - c)))
        for cp in copies:
            cp.start()
        for cp in copies:
            cp.wait()

    return pl.pallas_call(
        body, name="swap_grad_halves_" + tag,
        in_specs=[ANY] * n, out_specs=[ANY] * n,
        out_shape=[jax.ShapeDtypeStruct((g.shape[0], g.shape[1] // 2, g.shape[2]), F32) for g in grads],
        scratch_shapes=[pltpu.SemaphoreType.DMA((n,)), pltpu.SemaphoreType.DMA((n,))],
        compiler_params=pltpu.CompilerParams(has_side_effects=True),
    )(*grads)


def _pair_sum(g, got, c_arr, name):
    n_sh, R, C = g.shape
    rows = R // 2

    def body(c_ref, g_ref, r_ref, f_ref, b_ref):
        t = g_ref[...] + r_ref[...]
        f_ref[...] = t
        b_ref[...] = t.astype(BF16)

    blk = pl.BlockSpec((1, rows, C), lambda i, c_ref: (i, 0, 0))
    return pl.pallas_call(
        body, name=name,
        grid_spec=pltpu.PrefetchScalarGridSpec(
            num_scalar_prefetch=1, grid=(n_sh,),
            in_specs=[pl.BlockSpec((1, rows, C), lambda i, c_ref: (i, c_ref[0], 0)), blk],
            out_specs=[blk, blk]),
        out_shape=[jax.ShapeDtypeStruct((n_sh, rows, C), F32), jax.ShapeDtypeStruct((n_sh, rows, C), BF16)],
        compiler_params=_params(("parallel",)),
    )(c_arr, g, got)


def _scatter_grads(sums_bf16, small_all):
    n = len(sums_bf16)

    def body(*refs):
        b_ins = refs[:n]
        recvs, sm = refs[n + 1:2 * n + 1], refs[2 * n + 1]
        ici_send, ici_recv, sm_send, sm_recv = refs[2 * n + 2:]
        x, y, c, chips = _position()
        me = 4 * x + 2 * y + c
        copies = []
        for a in range(n):
            for j, (px, py) in enumerate(chips):
                copies.append(_remote(b_ins[a].at[2 * px + py], recvs[a].at[j],
                                      ici_send.at[3 * a + j], ici_recv.at[3 * a + j], (px, py, c)))
        for k in range(1, 8):
            peer = (x ^ (k >> 2), y ^ ((k >> 1) & 1), c ^ (k & 1))
            copies.append(_remote(sm.at[me], sm.at[me], sm_send.at[k - 1], sm_recv.at[k - 1], peer))
        for cp in copies:
            cp.start()
        for cp in copies:
            cp.wait_send()
        for a in range(n):
            for j, (px, py) in enumerate(chips):
                _remote(recvs[a].at[j], recvs[a].at[j], ici_send.at[3 * a + j], ici_recv.at[3 * a + j],
                        (px, py, c)).wait_recv()
        for k in range(1, 8):
            peer = (x ^ (k >> 2), y ^ ((k >> 1) & 1), c ^ (k & 1))
            theirs = sm.at[4 * peer[0] + 2 * peer[1] + peer[2]]
            _remote(theirs, theirs, sm_send.at[k - 1], sm_recv.at[k - 1], peer).wait_recv()

    out = pl.pallas_call(
        body, name="scatter_grads",
        in_specs=[ANY] * (n + 1), out_specs=[ANY] * (n + 1),
        out_shape=[jax.ShapeDtypeStruct((3,) + b.shape[1:], BF16) for b in sums_bf16]
        + [jax.ShapeDtypeStruct(small_all.shape, F32)],
        input_output_aliases={n: n},
        scratch_shapes=[pltpu.SemaphoreType.DMA((3 * n,)), pltpu.SemaphoreType.DMA((3 * n,)),
                        pltpu.SemaphoreType.DMA((7,)), pltpu.SemaphoreType.DMA((7,))],
        compiler_params=pltpu.CompilerParams(has_side_effects=True),
    )(*sums_bf16, small_all)
    return out[:n], out[n]


def _shard_sum(sums_f32, recv, shard_arr, c_arr, name):
    _, rows, C = sums_f32.shape

    def body(s_ref, c_ref, o_ref, r_ref, t_ref):
        t_ref[...] = ((o_ref[0] + r_ref[0].astype(F32)) + r_ref[1].astype(F32)) + r_ref[2].astype(F32)

    return pl.pallas_call(
        body, name=name,
        grid_spec=pltpu.PrefetchScalarGridSpec(
            num_scalar_prefetch=2, grid=(1,),
            in_specs=[pl.BlockSpec((1, rows, C), lambda i, s_ref, c_ref: (s_ref[0], 0, 0)),
                      pl.BlockSpec((3, rows, C), lambda i, s_ref, c_ref: (0, 0, 0))],
            out_specs=pl.BlockSpec((rows, C), lambda i, s_ref, c_ref: (c_ref[0], 0))),
        out_shape=jax.ShapeDtypeStruct((2 * rows, C), F32),
        compiler_params=_params(("arbitrary",)),
    )(shard_arr, c_arr, sums_f32, recv)


def _join_halves(bufs):
    n = len(bufs)

    def body(*refs):
        outs, send_sem, recv_sem = refs[n:2 * n], refs[2 * n], refs[2 * n + 1]
        x, y, c, _ = _position()
        copies = []
        for a in range(n):
            rows = outs[a].shape[0] // 2
            mine = outs[a].at[pl.ds(c * rows, rows), :]
            copies.append(_remote(mine, mine, send_sem.at[a], recv_sem.at[a], (x, y, 1 - c)))
        for cp in copies:
            cp.start()
        for a, cp in enumerate(copies):
            cp.wait_send()
            rows = outs[a].shape[0] // 2
            theirs = outs[a].at[pl.ds((1 - c) * rows, rows), :]
            _remote(theirs, theirs, send_sem.at[a], recv_sem.at[a], (x, y, 1 - c)).wait_recv()

    return pl.pallas_call(
        body, name="join_grad_halves",
        in_specs=[ANY] * n, out_specs=[ANY] * n,
        out_shape=[jax.ShapeDtypeStruct(b.shape, F32) for b in bufs],
        input_output_aliases={i: i for i in range(n)},
        scratch_shapes=[pltpu.SemaphoreType.DMA((n,)), pltpu.SemaphoreType.DMA((n,))],
        compiler_params=pltpu.CompilerParams(has_side_effects=True),
    )(*bufs)


def _small_sum(parts, tag):
    _, R, C = parts.shape

    def body(p_ref, o_ref):
        t = p_ref[0]
        for k in range(1, 8):
            t = t + p_ref[k]
        o_ref[...] = t

    return pl.pallas_call(
        body, name="small_grad_sum_" + tag, grid=(1,),
        in_specs=[pl.BlockSpec((8, R, C), lambda i: (0, 0, 0))], out_specs=pl.BlockSpec((R, C), lambda i: (0, 0)),
        out_shape=jax.ShapeDtypeStruct((R, C), F32), compiler_params=_params(("arbitrary",)),
    )(parts)


def _adamw_math(w, g, m, v):
    m = ADAM_B1 * m + (1.0 - ADAM_B1) * g
    v = ADAM_B2 * v + (1.0 - ADAM_B2) * (g * g)
    m_hat = m / (1.0 - ADAM_B1 ** ADAM_STEP)
    v_hat = v / (1.0 - ADAM_B2 ** ADAM_STEP)
    delta = -ADAM_LR * (m_hat / (jnp.sqrt(v_hat) + ADAM_EPS) + ADAM_WD * w)
    return delta, m, v


def _adamw_big(w, g, m, v, name):
    R, C = w.shape
    tr = R // 4

    def body(w_ref, g_ref, m_ref, v_ref, d_ref, nm_ref, nv_ref):
        d_ref[...], nm_ref[...], nv_ref[...] = _adamw_math(w_ref[...], g_ref[...], m_ref[...], v_ref[...])

    blk = pl.BlockSpec((tr, C), lambda i: (i, 0))
    return pl.pallas_call(
        body, name=name, grid=(4,), in_specs=[blk] * 4, out_specs=[blk] * 3,
        out_shape=[jax.ShapeDtypeStruct((R, C), F32)] * 3, compiler_params=_params(("parallel",)),
    )(w, g, m, v)


def _adamw_small(ws, gs, ms, vs):
    n = len(ws)

    def body(*refs):
        for a in range(n):
            w, g, m, v = (refs[k * n + a][...] for k in range(4))
            d, nm, nv = _adamw_math(w, g, m, v)
            refs[4 * n + a][...] = d
            refs[5 * n + a][...] = nm
            refs[6 * n + a][...] = nv

    shapes = [jax.ShapeDtypeStruct(w.shape, F32) for w in ws]
    out = pl.pallas_call(body, name="adamw_small", out_shape=shapes * 3)(*ws, *gs, *ms, *vs)
    return out[:n], out[n:2 * n], out[2 * n:]


BIG = ("w_in", "w_out", "w_up", "w_down")
SMALL = ("g_mix_pre", "pool_w", "pool_scale", "g_mix_post", "g_ffn_pre", "conv_b", "g_ffn_post", "conv_w")
ORDER = ("g_mix_pre", "w_in", "pool_w", "pool_scale", "w_out", "g_mix_post", "g_ffn_pre", "w_up", "conv_w", "conv_b",
         "w_down", "g_ffn_post")


def kernel(x, g_mix_pre, w_in, pool_w, pool_scale, w_out, g_mix_post, g_ffn_pre, w_up, conv_w, conv_b, w_down, g_ffn_post, loss_target, m_g_mix_pre, m_w_in, m_pool_w, m_pool_scale, m_w_out, m_g_mix_post, m_g_ffn_pre, m_w_up, m_conv_w, m_conv_b, m_w_down, m_g_ffn_post, v_g_mix_pre, v_w_in, v_pool_w, v_pool_scale, v_w_out, v_g_mix_post, v_g_ffn_pre, v_w_up, v_conv_w, v_conv_b, v_w_down, v_g_ffn_post):
    args = dict(locals())
    W = {n: args[n][0] for n in ORDER}
    M = {n: args["m_" + n][0] for n in ORDER}
    V = {n: args["v_" + n][0] for n in ORDER}
    for d in (W, M, V):
        d["pool_w"] = d["pool_w"].reshape(-1, POOL_GROUP)
        for n in ("g_mix_pre", "pool_scale", "g_mix_post", "g_ffn_pre", "conv_b", "g_ffn_post"):
            d[n] = d[n].reshape(1, -1)
    CW = W["w_up"].shape[1]
    c_arr = lax.axis_index("c").astype(jnp.int32).reshape(1)
    shard = 2 * lax.axis_index("x") + lax.axis_index("y")
    shard_arr = shard.astype(jnp.int32).reshape(1)
    device = 2 * shard + lax.axis_index("c")

    conv_w_slots = lax.dynamic_update_index_in_dim(jnp.zeros((N_SHARD,) + W["conv_w"].shape, F32), W["conv_w"], shard, 0)
    slots = {n: _cast_bf16(W[n], shard_arr, "cast_" + n) for n in BIG}
    w_in_g, conv_w_g = _gather_weights([slots["w_in"], conv_w_slots])
    conv_w_full = conv_w_g.transpose(1, 0, 2).reshape(CONV_WIDTH, 1, N_SHARD * CW)

    loss, grad_x, G = _local_step(
        x[0], loss_target[0], W["g_mix_pre"], w_in_g, W["pool_w"].reshape(-1, POOL_GROUP, POOL_GROUP), W["pool_scale"],
        slots["w_out"], W["g_mix_post"], W["g_ffn_pre"], slots["w_up"], conv_w_full, W["conv_b"],
        slots["w_down"], W["g_ffn_post"], (c_arr, device))

    late = ("w_in", "w_out")
    from_sibling = _swap_halves([G[n] for n in late], "mix")
    sums = {n: _pair_sum(G[n], r, c_arr, "pair_sum_" + n) for n, r in zip(late, from_sibling)}
    small = _pack_small(G, SMALL_LATE)
    small_slots = lax.dynamic_update_index_in_dim(jnp.zeros((8,) + small.shape, F32), small, device, 0)
    recvs, small_all = _scatter_grads([sums[n][1] for n in late], small_slots)
    reduced = {n: (sums[n][0], r) for n, r in zip(late, recvs)}
    reduced.update({n: G[n] for n in ("w_up", "w_down")})
    halves = [_shard_sum(*reduced[n], shard_arr, c_arr, "shard_sum_" + n) for n in BIG]
    full = dict(zip(BIG, _join_halves(halves)))
    full.update(_unpack_small(_small_sum(G["small_early"], "early"), SMALL_EARLY, W, shard))
    full.update(_unpack_small(_small_sum(small_all, "late"), SMALL_LATE, W, shard))

    delta, new_m, new_v = {}, {}, {}
    for n in BIG:
        delta[n], new_m[n], new_v[n] = _adamw_big(W[n], full[n], M[n], V[n], "adamw_" + n)
    ds, nms, nvs = _adamw_small([W[n] for n in SMALL], [full[n] for n in SMALL], [M[n] for n in SMALL],
                                [V[n] for n in SMALL])
    for n, d, nm, nv in zip(SMALL, ds, nms, nvs):
        delta[n], new_m[n], new_v[n] = d, nm, nv

    loss = lax.psum(loss[0, 0], ("x", "y", "c"))
    shaped = lambda d: [d[n].reshape(args[n].shape) for n in ORDER]
    return (loss, grad_x[None], *shaped(full), *shaped(delta), *shaped(new_m), *shaped(new_v))
```

```python
import functools

import jax
import jax.numpy as jnp
from jax import lax
from jax.experimental import pallas as pl
from jax.experimental.pallas import tpu as pltpu

F32 = jnp.float32
BF16 = jnp.bfloat16

RMS_EPS = 1e-6
NEG_INF = -1e30
N_HEADS = 8
HEAD_DIM = 64
ATTN_WIDTH = N_HEADS * HEAD_DIM
ATTN_SCALE = HEAD_DIM ** -0.5
ATTN_BLOCK = 128
DILATIONS = (1, 4, 16)
RESIDUES_PER_STEP = 4
POOL_WINDOWS = (2, 4, 8, 16)
POOL_GROUP = 128
POOL_WIDTH = POOL_GROUP * len(POOL_WINDOWS)
POOL_HALO = 16
CONV_WIDTH = 3
CONV_HALO = 8
N_SHARD = 4
LANES = 128

ADAM_LR = 0.001
ADAM_B1 = 0.9
ADAM_B2 = 0.999
ADAM_EPS = 1e-08
ADAM_WD = 0.01
ADAM_STEP = 10

VMEM_LIMIT = 60 * 1024 * 1024
MESH = pl.DeviceIdType.MESH
NT = (((1,), (1,)), ((), ()))
TN = (((0,), (0,)), ((), ()))


def _params(sem, vmem=None):
    return pltpu.CompilerParams(dimension_semantics=sem, vmem_limit_bytes=vmem)


def _const_spec(shape):
    zeros = (0,) * len(shape)
    return pl.BlockSpec(shape, lambda *_: zeros, pipeline_mode=pl.Buffered(1))


def _dot(a, b):
    return jnp.dot(a, b, preferred_element_type=F32)


def _dot_nt(a, b):
    return lax.dot_general(a, b, NT, preferred_element_type=F32)


def _dot_tn(a, b):
    return lax.dot_general(a, b, TN, preferred_element_type=F32)


def _rms_stats(x):
    r = lax.rsqrt(jnp.mean(x * x, axis=-1, keepdims=True) + RMS_EPS)
    return x * r, r


def _rms_bwd(dy, n, r, g):
    dg = jnp.sum(dy * n, axis=0, keepdims=True)
    dn = dy * g
    dx = r * (dn - n * jnp.mean(dn * n, axis=-1, keepdims=True))
    return dx, dg


def _gelu_tanh(g):
    k = 0.7978845608028654
    kc = k * 0.044715
    g2 = g * g
    t = jnp.tanh(g * (k + kc * g2))
    h = 0.5 * t + 0.5
    dh = (0.5 - 0.5 * (t * t)) * (k + (3.0 * kc) * g2)
    return g * h, h + g * dh


def _residue_shape(S, d, dtype):
    return jax.ShapeDtypeStruct((S // d, d * ATTN_WIDTH), dtype)


def _residue_spec(TM, d):
    return pl.BlockSpec((TM // d, d * ATTN_WIDTH), lambda i: (i, 0))


def _token_scratch(TM):
    return [pltpu.VMEM((TM, LANES), F32)] * (ATTN_WIDTH // LANES)


def _put_tokens(dst_s, val):
    for cb, chunk in enumerate(dst_s):
        chunk[...] = val[:, cb * LANES:(cb + 1) * LANES]


def _get_tokens(src_s):
    return jnp.concatenate([chunk[...] for chunk in src_s], axis=1)


def _to_residue(val, src_s, out_ref, d, dtype):
    if d == 1:
        out_ref[...] = val.astype(dtype)
        return
    rows = src_s[0].shape[0]
    for r in range(d):
        for cb, chunk in enumerate(src_s):
            col = r * ATTN_WIDTH + cb * LANES
            out_ref[:, col:col + LANES] = chunk[pl.ds(r, rows // d, stride=d), :].astype(dtype)


def _from_residue(in_ref, dst_s, d):
    if d == 1:
        return in_ref[...].astype(F32)
    rows = dst_s[0].shape[0]
    for r in range(d):
        for cb, chunk in enumerate(dst_s):
            col = r * ATTN_WIDTH + cb * LANES
            chunk[pl.ds(r, rows // d, stride=d), :] = in_ref[:, col:col + LANES].astype(F32)
    return _get_tokens(dst_s)


def _mix_in_fwd(x, g_pre, w_in):
    S, D = x.shape
    TM = 512

    def body(x_ref, g_ref, w_ref, *refs):
        qkv_refs, p_ref, h_ref, t_s = refs[:9], refs[9], refs[10], refs[11:]
        n, _ = _rms_stats(x_ref[...])
        hb = (n * g_ref[...]).astype(BF16)
        h_ref[...] = hb
        for a in range(3):
            res = _dot(hb, w_ref[a])
            if a == 0:
                res = res * ATTN_SCALE
            _put_tokens(t_s, res)
            for i, d in enumerate(DILATIONS):
                _to_residue(res, t_s, qkv_refs[3 * i + a], d, BF16)
        p_ref[...] = _dot(hb, w_ref[3])

    row = lambda w: pl.BlockSpec((TM, w), lambda i: (i, 0))
    out = pl.pallas_call(
        body, name="mix_in_fwd", grid=(S // TM,),
        in_specs=[row(D), _const_spec((1, D)), _const_spec(w_in.shape)],
        out_specs=[_residue_spec(TM, d) for d in DILATIONS for _ in range(3)] + [row(POOL_WIDTH), row(D)],
        out_shape=[_residue_shape(S, d, BF16) for d in DILATIONS for _ in range(3)]
        + [jax.ShapeDtypeStruct((S, POOL_WIDTH), F32), jax.ShapeDtypeStruct((S, D), BF16)],
        scratch_shapes=_token_scratch(TM),
        compiler_params=_params(("parallel",), VMEM_LIMIT),
    )(x, g_pre, w_in)
    return [out[0:3], out[3:6], out[6:9]], out[9], out[10]


def _band_mask(n):
    qi = lax.broadcasted_iota(jnp.int32, (ATTN_BLOCK, 2 * ATTN_BLOCK), 0)
    ki = lax.broadcasted_iota(jnp.int32, (ATTN_BLOCK, 2 * ATTN_BLOCK), 1)
    dist = qi + ATTN_BLOCK - ki
    return (dist >= 0) & (dist <= ATTN_BLOCK) & ((ki >= ATTN_BLOCK) | (n > 0))


def _first_head_lanes():
    return lax.broadcasted_iota(jnp.int32, (1, LANES), 1) < HEAD_DIM


def _stack_heads(pair, first):
    zero = jnp.zeros_like(pair)
    return jnp.concatenate([jnp.where(first, pair, zero), jnp.where(first, zero, pair)], axis=0)


def _unstack_heads(stacked, first):
    return jnp.where(first, stacked[:ATTN_BLOCK], stacked[ATTN_BLOCK:])


CARGO_COPIES = {"ici": 3, "d2d": 3, "scatter": 3, "swap": 1, "everyone": 7}
CARGO_IN_PLACE = ("ici", "d2d", "everyone")


def _cargo_copies(kinds, ins, outs, send_sems, recv_sems, want_recvs=True):
    x, y, c, chips = _position()
    s = 2 * x + y
    me = 2 * s + c
    sibling = (x, y, 1 - c)
    sends, recvs = [], []

    def add(k, src, dst, landing, to):
        sends.append(_remote(src, dst, send_sems.at[k], recv_sems.at[k], to))
        if want_recvs:
            recvs.append(_remote(landing, landing, send_sems.at[k], recv_sems.at[k], to))

    k0 = 0
    for a, kind in enumerate(kinds):
        if kind == "swap":
            rows = ins[a].shape[1] // 2
            add(k0, ins[a].at[:, pl.ds((1 - c) * rows, rows), :], outs[a], outs[a], sibling)
        elif kind == "everyone":
            for m in range(1, 8):
                peer = (x ^ (m >> 2), y ^ ((m >> 1) & 1), c ^ (m & 1))
                add(k0 + m - 1, outs[a].at[me], outs[a].at[me], outs[a].at[4 * peer[0] + 2 * peer[1] + peer[2]], peer)
        else:
            for j, (px, py) in enumerate(chips):
                sj = 2 * px + py
                if kind == "scatter":
                    add(k0 + j, ins[a].at[sj], outs[a].at[j], outs[a].at[j], (px, py, c))
                    continue
                buf = outs[a]
                rows = buf.shape[1] // 2
                half = lambda shard, h: buf.at[shard, pl.ds(h * rows, rows), :]
                if kind == "ici":
                    add(k0 + j, half(s, c), half(s, c), half(sj, c), (px, py, c))
                else:
                    add(k0 + j, half(sj, c), half(sj, c), half(sj, 1 - c), sibling)
        k0 += CARGO_COPIES[kind]
    return sends, recvs


def _cargo_start(kinds, ins, outs, sems, first_step):
    @pl.when(first_step)
    def _():
        for cp in _cargo_copies(kinds, ins, outs, *sems, want_recvs=False)[0]:
            cp.start()


def _cargo_finish(kinds, ins, outs, sems, last_step):
    @pl.when(last_step)
    def _():
        sends, recvs = _cargo_copies(kinds, ins, outs, *sems)
        for cp in sends:
            cp.wait_send()
        for cp in recvs:
            cp.wait_recv()


def _cargo_call(cargo, n_in, n_out):
    arrays = [a for _, a in cargo]
    shapes = []
    for kind, a in cargo:
        shape = {"scatter": (3,) + a.shape[1:], "swap": (a.shape[0], a.shape[1] // 2, a.shape[2])}.get(kind, a.shape)
        shapes.append(jax.ShapeDtypeStruct(shape, a.dtype))
    aliases = {n_in + i: n_out + i for i, (kind, _) in enumerate(cargo) if kind in CARGO_IN_PLACE}
    n_sems = sum(CARGO_COPIES[kind] for kind, _ in cargo)
    sems = [pltpu.SemaphoreType.DMA((n_sems,))] * 2 if cargo else []
    return arrays, [ANY] * len(cargo), shapes, aliases, sems


def _attn_fwd(q, k, v, d, cargo=()):
    L = q.shape[0]
    nb = L // ATTN_BLOCK
    group = min(d, RESIDUES_PER_STEP)
    width = group * ATTN_WIDTH
    nc = len(cargo)
    kinds = [kind for kind, _ in cargo]

    def body(*refs):
        q_ref, kp_ref, kc_ref, vp_ref, vc_ref = refs[:5]
        o_ref, lse_ref = refs[5 + nc:7 + nc]
        cargo_refs = (kinds, refs[5:5 + nc], refs[7 + nc:7 + 2 * nc], refs[7 + 2 * nc:])
        r, n = pl.program_id(0), pl.program_id(1)
        if nc:
            _cargo_start(*cargo_refs, (r == 0) & (n == 0))
        valid = _band_mask(pl.program_id(1))
        valid2 = jnp.concatenate([valid, valid], axis=0)
        first = _first_head_lanes()
        for hp in range(width // LANES):
            sl = slice(hp * LANES, (hp + 1) * LANES)
            kk = jnp.concatenate([kp_ref[:, sl], kc_ref[:, sl]], axis=0)
            vv = jnp.concatenate([vp_ref[:, sl], vc_ref[:, sl]], axis=0)
            s = jnp.where(valid2, _dot_nt(_stack_heads(q_ref[:, sl], first), kk), NEG_INF)
            m = jnp.max(s, axis=-1, keepdims=True)
            p = jnp.exp(s - m)
            den = jnp.sum(p, axis=-1, keepdims=True)
            o_ref[:, sl] = _unstack_heads(_dot(p.astype(BF16), vv) / den, first)
            lse_ref[:, sl] = _unstack_heads(m + jnp.log(den), first)
        if nc:
            _cargo_finish(*cargo_refs, (r == d // group - 1) & (n == nb - 1))

    cur = pl.BlockSpec((ATTN_BLOCK, width), lambda r, n: (n, r))
    prev = pl.BlockSpec((ATTN_BLOCK, width), lambda r, n: (jnp.maximum(n - 1, 0), r))
    arrays, specs, shapes, aliases, sems = _cargo_call(cargo, 5, 2)
    out = pl.pallas_call(
        body, name=f"attn_fwd_d{d}", grid=(d // group, nb),
        in_specs=[cur, prev, cur, prev, cur] + specs,
        out_specs=[cur, cur] + specs,
        out_shape=[jax.ShapeDtypeStruct((L, d * ATTN_WIDTH), F32)] * 2 + shapes,
        input_output_aliases=aliases, scratch_shapes=sems,
        compiler_params=_params(("arbitrary", "arbitrary")),
    )(q, k, k, v, v, *arrays)
    return out[0], out[1], out[2:]


def _attn_mix(outs, lses):
    S = outs[0].shape[0]
    TM = 512
    n = len(DILATIONS)

    def body(*refs):
        o_refs, l_refs, attn_ref, lse_refs, t_s = refs[:n], refs[n:2 * n], refs[2 * n], refs[2 * n + 1:3 * n + 1], refs[3 * n + 1:]
        os = [_from_residue(o_refs[i], t_s, d) for i, d in enumerate(DILATIONS)]
        ls = [_from_residue(l_refs[i], t_s, d) for i, d in enumerate(DILATIONS)]
        m = jnp.maximum(jnp.maximum(ls[0], ls[1]), ls[2])
        es = [jnp.exp(l - m) for l in ls]
        den = es[0] + es[1] + es[2]
        attn_ref[...] = (es[0] * os[0] + es[1] * os[1] + es[2] * os[2]) / den
        lse = m + jnp.log(den)
        _put_tokens(t_s, lse)
        for i, d in enumerate(DILATIONS):
            _to_residue(lse, t_s, lse_refs[i], d, F32)

    specs = [_residue_spec(TM, d) for d in DILATIONS]
    out = pl.pallas_call(
        body, name="attn_mix", grid=(S // TM,),
        in_specs=specs * 2, out_specs=[specs[0]] + specs,
        out_shape=[jax.ShapeDtypeStruct((S, ATTN_WIDTH), F32)] + [_residue_shape(S, d, F32) for d in DILATIONS],
        scratch_shapes=_token_scratch(TM),
        compiler_params=_params(("parallel",)),
    )(*outs, *lses)
    return out[0], out[1:]


def _pool_counts(first_row, rows, w):
    t = first_row + lax.broadcasted_iota(jnp.int32, (rows, 1), 0)
    return jnp.minimum(t + 1, w).astype(F32)


def _trailing_sums(xe, w):
    s, k = xe, 1
    while k < w:
        s = s + pltpu.roll(s, k, 0)
        k *= 2
    return s


def _leading_sums(xe, w):
    rows = xe.shape[0]
    s, k = xe, 1
    while k < w:
        s = s + pltpu.roll(s, rows - k, 0)
        k *= 2
    return s


def _pooled_groups(halo, cur, first_row):
    TM = cur.shape[0]
    xe = jnp.concatenate([halo, cur], axis=0)
    out = []
    for g, w in enumerate(POOL_WINDOWS):
        a = xe[:, g * POOL_GROUP:(g + 1) * POOL_GROUP]
        s = _trailing_sums(a, w)[POOL_HALO:]
        out.append(s / _pool_counts(first_row, TM, w) - a[POOL_HALO:])
    return out


def _pool_fwd(pool_in, pool_w, pool_scale):
    S = pool_in.shape[0]
    TM = 512
    HB = TM // POOL_HALO

    def body(cur_ref, halo_ref, w_ref, sc_ref, y_ref):
        i = pl.program_id(0)
        halo = jnp.where(i > 0, halo_ref[...], 0.0)
        pooled = _pooled_groups(halo, cur_ref[...], i * TM)
        for g in range(len(POOL_WINDOWS)):
            sl = slice(g * POOL_GROUP, (g + 1) * POOL_GROUP)
            y = _dot(pooled[g].astype(BF16), w_ref[g].astype(BF16)) * sc_ref[:, sl]
            y_ref[:, sl] = y.astype(BF16)

    return pl.pallas_call(
        body, name="pool_fwd", grid=(S // TM,),
        in_specs=[pl.BlockSpec((TM, POOL_WIDTH), lambda i: (i, 0)),
                  pl.BlockSpec((POOL_HALO, POOL_WIDTH), lambda i: (jnp.maximum(i * HB - 1, 0), 0)),
                  _const_spec(pool_w.shape), _const_spec((1, POOL_WIDTH))],
        out_specs=pl.BlockSpec((TM, POOL_WIDTH), lambda i: (i, 0)),
        out_shape=jax.ShapeDtypeStruct((S, POOL_WIDTH), BF16),
        compiler_params=_params(("parallel",)),
    )(pool_in, pool_in, pool_w, pool_scale)


def _mix_out_fwd(attn, pool, w_out, x, g_post, g_ffn_pre):
    S, D = x.shape
    TM = 512

    def body(a_ref, p_ref, w_ref, x_ref, gp_ref, gf_ref, mixed_ref, x1_ref, h2_ref, cat_ref):
        ab = a_ref[...].astype(BF16)
        cat_ref[:, :ATTN_WIDTH] = ab
        cat_ref[:, ATTN_WIDTH:] = p_ref[...]
        mixed = _dot(ab, w_ref[:ATTN_WIDTH, :]) + _dot(p_ref[...], w_ref[ATTN_WIDTH:, :])
        mixed_ref[...] = mixed
        n, _ = _rms_stats(mixed)
        x1 = x_ref[...] + n * gp_ref[...]
        x1_ref[...] = x1
        n2, _ = _rms_stats(x1)
        h2_ref[...] = (n2 * gf_ref[...]).astype(BF16)

    row = lambda w: pl.BlockSpec((TM, w), lambda i: (i, 0))
    return pl.pallas_call(
        body, name="mix_out_fwd", grid=(S // TM,),
        in_specs=[row(ATTN_WIDTH), row(POOL_WIDTH), _const_spec(w_out.shape), row(D),
                  _const_spec((1, D)), _const_spec((1, D))],
        out_specs=[row(D), row(D), row(D), row(D)],
        out_shape=[jax.ShapeDtypeStruct((S, D), F32), jax.ShapeDtypeStruct((S, D), F32),
                   jax.ShapeDtypeStruct((S, D), BF16), jax.ShapeDtypeStruct((S, D), BF16)],
        compiler_params=_params(("parallel",), VMEM_LIMIT),
    )(attn, pool, w_out, x, g_post, g_ffn_pre)


def _ffn_fwd(h2, x1, target, w_up, w_down, conv_w, conv_b, g_post):
    S, D = x1.shape
    CW = w_up.shape[2]
    FF = 2 * CW
    TM = 256
    piece = 4 * LANES
    pieces = [(lo, min(lo + piece, CW)) for lo in range(0, CW, piece)]

    def body(h2_ref, x1_ref, t_ref, wu_ref, wd_ref, cw_ref, cb_ref, g_ref,
             yv_ref, dy_ref, df_ref, dc_ref, loss_ref, dg_ref, dcb_ref, dcw_ref,
             ue_s, dgate_s, dval_s):
        i = pl.program_id(0)

        @pl.when(i == 0)
        def _():
            loss_ref[...] = jnp.zeros_like(loss_ref)
            dg_ref[...] = jnp.zeros_like(dg_ref)
            dcb_ref[...] = jnp.zeros_like(dcb_ref)
            dcw_ref[...] = jnp.zeros_like(dcw_ref)
            ue_s[0:CONV_HALO, :] = jnp.zeros((CONV_HALO, 2 * FF), F32)

        @pl.when(i > 0)
        def _():
            ue_s[0:CONV_HALO, :] = ue_s[TM:TM + CONV_HALO, :]

        def shifted(cols, k):
            return pltpu.roll(ue_s[:, cols], k, 0)[CONV_HALO:]

        def conv(cols):
            return (cb_ref[:, cols] + cw_ref[2, :, cols] * ue_s[CONV_HALO:, cols]
                    + cw_ref[1, :, cols] * shifted(cols, 1) + cw_ref[0, :, cols] * shifted(cols, 2))

        hb = h2_ref[...]
        f = jnp.zeros((TM, D), F32)
        for j in range(2):
            jc = slice(j * CW, (j + 1) * CW)
            for half in range(2):
                blk = 2 * half + j
                cols = slice(blk * CW, (blk + 1) * CW)
                ue_s[CONV_HALO:, cols] = _dot(hb, wu_ref[blk])
            for lo, hi in pieces:
                pc = slice(j * CW + lo, j * CW + hi)
                gelu, dgelu = _gelu_tanh(conv(pc))
                val = conv(slice(FF + j * CW + lo, FF + j * CW + hi))
                dgate_s[:, pc] = val * dgelu
                dval_s[:, pc] = gelu
                yv_ref[:, pc] = (gelu * val).astype(BF16)
            f = f + _dot(yv_ref[:, jc], wd_ref[jc, :])

        n, r = _rms_stats(f)
        err = x1_ref[...] + n * g_ref[...] - t_ref[...]
        loss_ref[...] += 0.5 * jnp.sum(jnp.mean(err * err, axis=-1, keepdims=True), axis=0, keepdims=True)
        dy = err / D
        dy_ref[...] = dy
        df, dg = _rms_bwd(dy, n, r, g_ref[...])
        dg_ref[...] += dg
        dfb = df.astype(BF16)
        df_ref[...] = dfb

        for j in range(2):
            jc = slice(j * CW, (j + 1) * CW)
            dyv = _dot_nt(dfb, wd_ref[jc, :])
            for lo, hi in pieces:
                pc = slice(j * CW + lo, j * CW + hi)
                for half, scale_s in ((0, dgate_s), (1, dval_s)):
                    cols = slice(half * FF + j * CW + lo, half * FF + j * CW + hi)
                    dcv = dyv[:, lo:hi] * scale_s[:, pc]
                    dc_ref[:, cols] = dcv.astype(BF16)
                    dcb_ref[:, cols] += jnp.sum(dcv, axis=0, keepdims=True)
                    dcw_ref[2, :, cols] += jnp.sum(dcv * ue_s[CONV_HALO:, cols], axis=0, keepdims=True)
                    dcw_ref[1, :, cols] += jnp.sum(dcv * shifted(cols, 1), axis=0, keepdims=True)
                    dcw_ref[0, :, cols] += jnp.sum(dcv * shifted(cols, 2), axis=0, keepdims=True)

    row = lambda w: pl.BlockSpec((TM, w), lambda i: (i, 0))
    acc = lambda shape: pl.BlockSpec(shape, lambda i: (0,) * len(shape))
    return pl.pallas_call(
        body, name="ffn_fwd", grid=(S // TM,),
        in_specs=[row(D), row(D), row(D), _const_spec(w_up.shape), _const_spec(w_down.shape),
                  _const_spec(conv_w.shape), _const_spec((1, 2 * FF)), _const_spec((1, D))],
        out_specs=[row(FF), row(D), row(D), row(2 * FF),
                   acc((1, 1)), acc((1, D)), acc((1, 2 * FF)), acc((CONV_WIDTH, 1, 2 * FF))],
        out_shape=[jax.ShapeDtypeStruct((S, FF), BF16),
                   jax.ShapeDtypeStruct((S, D), F32), jax.ShapeDtypeStruct((S, D), BF16),
                   jax.ShapeDtypeStruct((S, 2 * FF), BF16),
                   jax.ShapeDtypeStruct((1, 1), F32), jax.ShapeDtypeStruct((1, D), F32),
                   jax.ShapeDtypeStruct((1, 2 * FF), F32), jax.ShapeDtypeStruct((CONV_WIDTH, 1, 2 * FF), F32)],
        scratch_shapes=[pltpu.VMEM((TM + CONV_HALO, 2 * FF), F32), pltpu.VMEM((TM, FF), F32),
                        pltpu.VMEM((TM, FF), F32)],
        compiler_params=_params(("arbitrary",), VMEM_LIMIT),
    )(h2, x1, target, w_up, w_down, conv_w, conv_b, g_post)


def _ffn_bwd(dc, conv_w, w_up, x1, g_ffn_pre, dy):
    S, D = x1.shape
    CW = w_up.shape[2]
    F2 = 4 * CW
    TM = 256
    HB = TM // CONV_HALO
    last = S // CONV_HALO - 1
    n_tiles = S // TM

    def body(dc_ref, halo_ref, cw_ref, wu_ref, x1_ref, g_ref, dy_ref, du_ref, dx1_ref, dg_ref):
        i = pl.program_id(0)

        @pl.when(i == 0)
        def _():
            dg_ref[...] = jnp.zeros_like(dg_ref)

        keep = i < n_tiles - 1
        dh2 = jnp.zeros((TM, D), F32)
        for blk in range(N_SHARD):
            cols = slice(blk * CW, (blk + 1) * CW)
            halo = jnp.where(keep, halo_ref[:, cols].astype(F32), 0.0)
            dce = jnp.concatenate([dc_ref[:, cols].astype(F32), halo], axis=0)
            rows = TM + CONV_HALO
            du = (cw_ref[2, :, cols] * dce[:TM]
                  + cw_ref[1, :, cols] * pltpu.roll(dce, rows - 1, 0)[:TM]
                  + cw_ref[0, :, cols] * pltpu.roll(dce, rows - 2, 0)[:TM])
            dub = du.astype(BF16)
            du_ref[:, cols] = dub
            dh2 = dh2 + _dot_nt(dub, wu_ref[blk])
        n2, r2 = _rms_stats(x1_ref[...])
        dx, dg = _rms_bwd(dh2, n2, r2, g_ref[...])
        dg_ref[...] += dg
        dx1_ref[...] = dy_ref[...] + dx

    row = lambda w: pl.BlockSpec((TM, w), lambda i: (i, 0))
    return pl.pallas_call(
        body, name="ffn_bwd", grid=(S // TM,),
        in_specs=[row(F2), pl.BlockSpec((CONV_HALO, F2), lambda i: (jnp.minimum((i + 1) * HB, last), 0)),
                  _const_spec(conv_w.shape), _const_spec(w_up.shape), row(D), _const_spec((1, D)), row(D)],
        out_specs=[row(F2), row(D), pl.BlockSpec((1, D), lambda i: (0, 0))],
        out_shape=[jax.ShapeDtypeStruct((S, F2), BF16), jax.ShapeDtypeStruct((S, D), F32),
                   jax.ShapeDtypeStruct((1, D), F32)],
        compiler_params=_params(("arbitrary",), VMEM_LIMIT),
    )(dc, dc, conv_w, w_up, x1, g_ffn_pre, dy)


def _matmul_tn(a, b, n_blocks, name):
    S, M = a.shape
    N = b.shape[1]
    tn = N // n_blocks
    tm = M if M <= 1024 else M // 2
    tk = 2048
    nk = S // tk

    def body(a_ref, b_ref, o_ref):
        @pl.when(pl.program_id(2) == 0)
        def _():
            o_ref[...] = jnp.zeros_like(o_ref)
        o_ref[0] += _dot_tn(a_ref[...], b_ref[...])

    return pl.pallas_call(
        body, name=name, grid=(M // tm, n_blocks, nk),
        in_specs=[pl.BlockSpec((tk, tm), lambda i, j, k: (k, i)), pl.BlockSpec((tk, tn), lambda i, j, k: (k, j))],
        out_specs=pl.BlockSpec((1, tm, tn), lambda i, j, k: (j, i, 0)),
        out_shape=jax.ShapeDtypeStruct((n_blocks, M, tn), F32),
        compiler_params=_params(("parallel", "parallel", "arbitrary"), VMEM_LIMIT),
    )(a, b)


def _mix_out_bwd(dx1, mixed, g_post, w_out, attn, cargo=()):
    S, D = dx1.shape
    TM = 512
    nd = len(DILATIONS)
    nc = len(cargo)
    kinds = [kind for kind, _ in cargo]
    n_chunks = ATTN_WIDTH // LANES

    def body(*refs):
        dx_ref, m_ref, g_ref, w_ref, a_ref = refs[:5]
        dm_ref, dp_ref, dg_ref = refs[5 + nc:8 + nc]
        da_refs, dl_refs = refs[8 + nc:8 + nc + nd], refs[8 + nc + nd:8 + nc + 2 * nd]
        n_out = 8 + nc + 2 * nd
        t_s = refs[n_out + nc:n_out + nc + n_chunks]
        cargo_refs = (kinds, refs[5:5 + nc], refs[n_out:n_out + nc], refs[n_out + nc + n_chunks:])
        if nc:
            _cargo_start(*cargo_refs, pl.program_id(0) == 0)

        @pl.when(pl.program_id(0) == 0)
        def _():
            dg_ref[...] = jnp.zeros_like(dg_ref)

        n, r = _rms_stats(m_ref[...])
        dm, dg = _rms_bwd(dx_ref[...], n, r, g_ref[...])
        dg_ref[...] += dg
        dmb = dm.astype(BF16)
        dm_ref[...] = dmb
        da = _dot_nt(dmb, w_ref[:ATTN_WIDTH, :])
        _put_tokens(t_s, da)
        for i, d in enumerate(DILATIONS):
            _to_residue(da, t_s, da_refs[i], d, BF16)
        dp_ref[...] = _dot_nt(dmb, w_ref[ATTN_WIDTH:, :])
        prod = da * a_ref[...]
        hi = prod.astype(BF16)
        lo = (prod - hi.astype(F32)).astype(BF16)
        ri = lax.broadcasted_iota(jnp.int32, (ATTN_WIDTH, ATTN_WIDTH), 0) // HEAD_DIM
        ci = lax.broadcasted_iota(jnp.int32, (ATTN_WIDTH, ATTN_WIDTH), 1) // HEAD_DIM
        ones = (ri == ci).astype(BF16)
        delta = _dot(hi, ones) + _dot(lo, ones)
        _put_tokens(t_s, delta)
        for i, d in enumerate(DILATIONS):
            _to_residue(delta, t_s, dl_refs[i], d, F32)
        if nc:
            _cargo_finish(*cargo_refs, pl.program_id(0) == S // TM - 1)

    row = lambda w: pl.BlockSpec((TM, w), lambda i: (i, 0))
    specs = [_residue_spec(TM, d) for d in DILATIONS]
    arrays, cargo_specs, shapes, aliases, sems = _cargo_call(cargo, 5, 3 + 2 * nd)
    out = pl.pallas_call(
        body, name="mix_out_bwd", grid=(S // TM,),
        in_specs=[row(D), row(D), _const_spec((1, D)), _const_spec(w_out.shape), row(ATTN_WIDTH)] + cargo_specs,
        out_specs=[row(D), row(POOL_WIDTH), pl.BlockSpec((1, D), lambda i: (0, 0))] + specs * 2 + cargo_specs,
        out_shape=[jax.ShapeDtypeStruct((S, D), BF16), jax.ShapeDtypeStruct((S, POOL_WIDTH), F32),
                   jax.ShapeDtypeStruct((1, D), F32)]
        + [_residue_shape(S, d, BF16) for d in DILATIONS] + [_residue_shape(S, d, F32) for d in DILATIONS] + shapes,
        input_output_aliases=aliases,
        scratch_shapes=_token_scratch(TM) + sems,
        compiler_params=_params(("arbitrary",), VMEM_LIMIT),
    )(dx1, mixed, g_post, w_out, attn, *arrays)
    return out[0], out[1], out[2], out[3:3 + nd], out[3 + nd:3 + 2 * nd], out[3 + 2 * nd:]


def _pool_bwd(pool_in, d_pool, pool_w, pool_scale):
    S = pool_in.shape[0]
    TM = 512
    HB = TM // POOL_HALO
    last = S // POOL_HALO - 1
    G = len(POOL_WINDOWS)

    def body(cur_ref, halo_ref, dcur_ref, dnext_ref, w_ref, sc_ref, dxin_ref, dw_ref, dsc_ref):
        i = pl.program_id(0)

        @pl.when(i == 0)
        def _():
            dw_ref[...] = jnp.zeros_like(dw_ref)
            dsc_ref[...] = jnp.zeros_like(dsc_ref)

        halo = jnp.where(i > 0, halo_ref[...], 0.0)
        pooled = _pooled_groups(halo, cur_ref[...], i * TM)
        dnext = jnp.where(i < S // TM - 1, dnext_ref[...], 0.0)
        dye = jnp.concatenate([dcur_ref[...], dnext], axis=0)
        for g, w in enumerate(POOL_WINDOWS):
            sl = slice(g * POOL_GROUP, (g + 1) * POOL_GROUP)
            wg = w_ref[g].astype(BF16)
            pb = pooled[g].astype(BF16)
            dsc_ref[:, sl] += jnp.sum(dye[:TM, sl] * _dot(pb, wg), axis=0, keepdims=True)
            dpre = (dye[:, sl] * sc_ref[:, sl]).astype(BF16)
            dw_ref[g] += _dot_tn(pb, dpre[:TM])
            dpooled = _dot_nt(dpre, wg)
            z = dpooled / _pool_counts(i * TM, TM + POOL_HALO, w)
            dxin_ref[:, sl] = (_leading_sums(z, w)[:TM] - dpooled[:TM]).astype(BF16)

    row = pl.BlockSpec((TM, POOL_WIDTH), lambda i: (i, 0))
    return pl.pallas_call(
        body, name="pool_bwd", grid=(S // TM,),
        in_specs=[row, pl.BlockSpec((POOL_HALO, POOL_WIDTH), lambda i: (jnp.maximum(i * HB - 1, 0), 0)),
                  row, pl.BlockSpec((POOL_HALO, POOL_WIDTH), lambda i: (jnp.minimum((i + 1) * HB, last), 0)),
                  _const_spec(pool_w.shape), _const_spec((1, POOL_WIDTH))],
        out_specs=[row, pl.BlockSpec((G, POOL_GROUP, POOL_GROUP), lambda i: (0, 0, 0)),
                   pl.BlockSpec((1, POOL_WIDTH), lambda i: (0, 0))],
        out_shape=[jax.ShapeDtypeStruct((S, POOL_WIDTH), BF16), jax.ShapeDtypeStruct((G, POOL_GROUP, POOL_GROUP), F32),
                   jax.ShapeDtypeStruct((1, POOL_WIDTH), F32)],
        compiler_params=_params(("arbitrary",)),
    )(pool_in, pool_in, d_pool, d_pool, pool_w, pool_scale)


def _attn_bwd(q, k, v, d_attn, lse, delta, d, cargo=()):
    L = q.shape[0]
    nb = L // ATTN_BLOCK
    group = min(d, RESIDUES_PER_STEP)
    width = group * ATTN_WIDTH
    nc = len(cargo)
    kinds = [kind for kind, _ in cargo]

    def body(*refs):
        q_ref, kp_ref, kc_ref, vp_ref, vc_ref, do_ref, lse_ref, dl_ref = refs[:8]
        dq_ref, dk_ref, dv_ref = refs[8 + nc:11 + nc]
        ck_s, cv_s = refs[11 + 2 * nc:13 + 2 * nc]
        cargo_refs = (kinds, refs[8:8 + nc], refs[11 + nc:11 + 2 * nc], refs[13 + 2 * nc:])
        r, n = pl.program_id(0), pl.program_id(1)
        if nc:
            _cargo_start(*cargo_refs, (r == 0) & (n == 0))

        @pl.when(n == 0)
        def _():
            ck_s[...] = jnp.zeros_like(ck_s)
            cv_s[...] = jnp.zeros_like(cv_s)

        @pl.when(n < nb)
        def _():
            valid = _band_mask(n)
            valid2 = jnp.concatenate([valid, valid], axis=0)
            first = _first_head_lanes()

            def stacked_column(ref, lane):
                return jnp.concatenate([ref[:, lane:lane + 1], ref[:, lane + HEAD_DIM:lane + HEAD_DIM + 1]], axis=0)

            for hp in range(width // LANES):
                sl = slice(hp * LANES, (hp + 1) * LANES)
                qq = _stack_heads(q_ref[:, sl], first)
                dd = _stack_heads(do_ref[:, sl], first)
                kk = jnp.concatenate([kp_ref[:, sl], kc_ref[:, sl]], axis=0)
                vv = jnp.concatenate([vp_ref[:, sl], vc_ref[:, sl]], axis=0)
                s = _dot_nt(qq, kk)
                p = jnp.where(valid2, jnp.exp(s - stacked_column(lse_ref, hp * LANES)), 0.0)
                dp = _dot_nt(dd, vv)
                ds = (p * (dp - stacked_column(dl_ref, hp * LANES))).astype(BF16)
                dq_ref[:, sl] = (_unstack_heads(_dot(ds, kk), first) * ATTN_SCALE).astype(BF16)
                dk = _dot_tn(ds, qq)
                dv = _dot_tn(p.astype(BF16), dd)
                dk_ref[:, sl] = (ck_s[:, sl] + dk[:ATTN_BLOCK]).astype(BF16)
                dv_ref[:, sl] = (cv_s[:, sl] + dv[:ATTN_BLOCK]).astype(BF16)
                ck_s[:, sl] = dk[ATTN_BLOCK:]
                cv_s[:, sl] = dv[ATTN_BLOCK:]

        @pl.when(n == nb)
        def _():
            dk_ref[...] = ck_s[...].astype(BF16)
            dv_ref[...] = cv_s[...].astype(BF16)

        if nc:
            _cargo_finish(*cargo_refs, (r == d // group - 1) & (n == nb))

    blk = (ATTN_BLOCK, width)
    cur = pl.BlockSpec(blk, lambda r, n: (jnp.minimum(n, nb - 1), r))
    prev = pl.BlockSpec(blk, lambda r, n: (jnp.maximum(jnp.minimum(n, nb - 1) - 1, 0), r))
    done = pl.BlockSpec(blk, lambda r, n: (jnp.maximum(n - 1, 0), r))
    arrays, specs, shapes, aliases, sems = _cargo_call(cargo, 8, 3)
    out = pl.pallas_call(
        body, name=f"attn_bwd_d{d}", grid=(d // group, nb + 1),
        in_specs=[cur, prev, cur, prev, cur, cur, cur, cur] + specs, out_specs=[cur, done, done] + specs,
        out_shape=[jax.ShapeDtypeStruct((L, d * ATTN_WIDTH), BF16)] * 3 + shapes,
        input_output_aliases=aliases,
        scratch_shapes=[pltpu.VMEM(blk, F32), pltpu.VMEM(blk, F32)] + sems,
        compiler_params=_params(("arbitrary", "arbitrary")),
    )(q, k, k, v, v, d_attn, lse, delta, *arrays)
    return out[:3], out[3:]


def _mix_in_bwd(dqkv, d_pool_in, w_in, x, g_pre, dx1):
    S, D = x.shape
    TM = 512
    nd = len(DILATIONS)

    def body(*refs):
        g_refs = refs[:3 * nd]
        dpi_ref, w_ref, x_ref, g_ref, dx1_ref, dproj_ref, gx_ref, dg_ref = refs[3 * nd:3 * nd + 8]
        t_s = refs[3 * nd + 8:]

        @pl.when(pl.program_id(0) == 0)
        def _():
            dg_ref[...] = jnp.zeros_like(dg_ref)

        dh = jnp.zeros((TM, D), F32)
        for a in range(4):
            if a < 3:
                tot = g_refs[a][...].astype(F32)
                for i, d in enumerate(DILATIONS[1:]):
                    tot = tot + _from_residue(g_refs[3 * (i + 1) + a], t_s, d)
                db = tot.astype(BF16)
            else:
                db = dpi_ref[...]
            dproj_ref[:, a * ATTN_WIDTH:(a + 1) * ATTN_WIDTH] = db
            dh = dh + _dot_nt(db, w_ref[a])
        n, r = _rms_stats(x_ref[...])
        dx, dg = _rms_bwd(dh, n, r, g_ref[...])
        dg_ref[...] += dg
        gx_ref[...] = dx1_ref[...] + dx

    row = lambda w: pl.BlockSpec((TM, w), lambda i: (i, 0))
    return pl.pallas_call(
        body, name="mix_in_bwd", grid=(S // TM,),
        in_specs=[_residue_spec(TM, d) for d in DILATIONS for _ in range(3)]
        + [row(POOL_WIDTH), _const_spec(w_in.shape), row(D), _const_spec((1, D)), row(D)],
        out_specs=[row(4 * ATTN_WIDTH), row(D), pl.BlockSpec((1, D), lambda i: (0, 0))],
        out_shape=[jax.ShapeDtypeStruct((S, 4 * ATTN_WIDTH), BF16), jax.ShapeDtypeStruct((S, D), F32),
                   jax.ShapeDtypeStruct((1, D), F32)],
        scratch_shapes=_token_scratch(TM),
        compiler_params=_params(("arbitrary",), VMEM_LIMIT),
    )(*[g for gs in dqkv for g in gs], d_pool_in, w_in, x, g_pre, dx1)


SMALL_EARLY = ("pool_w", "pool_scale", "g_mix_post", "g_ffn_pre", "conv_b", "g_ffn_post", "conv_w")
SMALL_LATE = ("g_mix_pre",)


def _pack_small(grads, names):
    parts = []
    for n in names:
        g = grads[n]
        if n == "conv_w":
            g = g.reshape(CONV_WIDTH, N_SHARD, -1).transpose(1, 0, 2)
        parts.append(g.reshape(-1, LANES))
    return jnp.concatenate(parts, axis=0) if len(parts) > 1 else parts[0]


def _unpack_small(packed, names, like, shard):
    out, row = {}, 0
    for n in names:
        size = like[n].size * (N_SHARD if n == "conv_w" else 1)
        g = packed[row:row + size // LANES]
        row += size // LANES
        if n == "conv_w":
            g = lax.dynamic_slice_in_dim(g.reshape((N_SHARD,) + like[n].shape), shard, 1, axis=0)[0]
        out[n] = g.reshape(like[n].shape)
    return out


def _local_step(x, target, g_mix_pre, w_in, pool_w, pool_scale, w_out, g_mix_post, g_ffn_pre,
                w_up, conv_w, conv_b, w_down, g_ffn_post, mesh_pos=None):
    on_mesh = mesh_pos is not None
    D = x.shape[1]
    CW = w_up.shape[2]
    qkv, pool_in, h1 = _mix_in_fwd(x, g_mix_pre, w_in)
    o1, l1, got = _attn_fwd(*qkv[0], 1, [("ici", w_out), ("ici", w_down)] if on_mesh else ())
    w_out, w_down = got if on_mesh else (w_out, w_down)
    o4, l4, got = _attn_fwd(*qkv[1], 4, [("d2d", w_out), ("d2d", w_down), ("ici", w_up)] if on_mesh else ())
    w_out, w_down, w_up = got if on_mesh else (w_out, w_down, w_up)
    o16, l16, got = _attn_fwd(*qkv[2], 16, [("d2d", w_up)] if on_mesh else ())
    w_up = got[0] if on_mesh else w_up
    w_out = w_out.reshape(D, D)
    w_down = w_down.reshape(2 * CW, D)
    attn, lse = _attn_mix((o1, o4, o16), (l1, l4, l16))
    pool = _pool_fwd(pool_in, pool_w, pool_scale)
    mixed, x1, h2, cat = _mix_out_fwd(attn, pool, w_out, x, g_mix_post, g_ffn_pre)

    yv, dy, df, dc, loss, d_g_ffn_post, d_conv_b, d_conv_w = _ffn_fwd(
        h2, x1, target, w_up, w_down, conv_w, conv_b, g_ffn_post)
    du, dx1, d_g_ffn_pre = _ffn_bwd(dc, conv_w, w_up, x1, g_ffn_pre, dy)
    d_w_up = _matmul_tn(h2, du, N_SHARD, "grad_w_up")
    d_w_down = _matmul_tn(yv, df, 1, "grad_w_down")[0].reshape(N_SHARD, CW // 2, D)
    swap = [("swap", d_w_up), ("swap", d_w_down)] if on_mesh else ()
    d_mixed, d_pool, d_g_mix_post, d_attn, delta, from_sibling = _mix_out_bwd(dx1, mixed, g_mix_post, w_out, attn, swap)
    d_w_out = _matmul_tn(cat, d_mixed, 1, "grad_w_out")[0].reshape(N_SHARD, D // N_SHARD, D)
    d_pool_in, d_pool_w, d_pool_scale = _pool_bwd(pool_in, d_pool, pool_w, pool_scale)
    grads = dict(pool_w=d_pool_w, pool_scale=d_pool_scale, w_out=d_w_out, g_mix_post=d_g_mix_post,
                 g_ffn_pre=d_g_ffn_pre, w_up=d_w_up, conv_w=d_conv_w, conv_b=d_conv_b, w_down=d_w_down,
                 g_ffn_post=d_g_ffn_post)
    cargo = [(), (), ()]
    if on_mesh:
        c_arr, device = mesh_pos
        up_f32, up_bf16 = _pair_sum(d_w_up, from_sibling[0], c_arr, "pair_sum_w_up")
        down_f32, down_bf16 = _pair_sum(d_w_down, from_sibling[1], c_arr, "pair_sum_w_down")
        early = _pack_small(grads, SMALL_EARLY)
        early_slots = lax.dynamic_update_index_in_dim(jnp.zeros((8,) + early.shape, F32), early, device, 0)
        cargo = [[("scatter", down_bf16)], [("scatter", up_bf16)], [("everyone", early_slots)]]

    dqkv, landed = zip(*[_attn_bwd(*qkv[i], d_attn[i], lse[i], delta[i], d, cargo[i]) for i, d in enumerate(DILATIONS)])
    if on_mesh:
        grads.update(w_down=(down_f32, landed[0][0]), w_up=(up_f32, landed[1][0]), small_early=landed[2][0])
    d_proj, grad_x, grads["g_mix_pre"] = _mix_in_bwd(dqkv, d_pool_in, w_in, x, g_mix_pre, dx1)
    grads["w_in"] = _matmul_tn(h1, d_proj, N_SHARD, "grad_w_in")
    return loss, grad_x, grads


ANY = pl.BlockSpec(memory_space=pl.ANY)


def _position():
    x, y, c = lax.axis_index("x"), lax.axis_index("y"), lax.axis_index("c")
    chips = [(1 - x, y), (x, 1 - y), (1 - x, 1 - y)]
    return x, y, c, chips


def _remote(src, dst, send_sem, recv_sem, to):
    return pltpu.make_async_remote_copy(src_ref=src, dst_ref=dst, send_sem=send_sem, recv_sem=recv_sem,
                                        device_id=to, device_id_type=MESH)


def _cast_bf16(w, shard_arr, name):
    R, C = w.shape
    tr = R // 2

    def body(s_ref, w_ref, o_ref):
        o_ref[0] = w_ref[...].astype(BF16)

    return pl.pallas_call(
        body, name=name,
        grid_spec=pltpu.PrefetchScalarGridSpec(
            num_scalar_prefetch=1, grid=(2,),
            in_specs=[pl.BlockSpec((tr, C), lambda i, s_ref: (i, 0))],
            out_specs=pl.BlockSpec((1, tr, C), lambda i, s_ref: (s_ref[0], i, 0))),
        out_shape=jax.ShapeDtypeStruct((N_SHARD, R, C), BF16),
        compiler_params=_params(("parallel",)))(shard_arr, w)


def _gather_weights(bufs):
    n = len(bufs) - 1

    def body(*refs):
        outs, cw_out = refs[n + 1:2 * n + 1], refs[2 * n + 1]
        ici_send, ici_recv, d2d_send, d2d_recv = refs[2 * n + 2:]
        x, y, c, chips = _position()
        s = 2 * x + y
        sibling = (x, y, 1 - c)

        def half(a, shard, h):
            rows = outs[a].shape[1] // 2
            return outs[a].at[shard, pl.ds(h * rows, rows), :]

        sends = []
        for a in range(n):
            for j, (px, py) in enumerate(chips):
                sends.append(_remote(half(a, s, c), half(a, s, c),
                                     ici_send.at[3 * a + j], ici_recv.at[3 * a + j], (px, py, c)))
        for j, (px, py) in enumerate(chips):
            sends.append(_remote(cw_out.at[s], cw_out.at[s], ici_send.at[3 * n + j], ici_recv.at[3 * n + j], (px, py, c)))
        for cp in sends:
            cp.start()
        passed = []
        for a in range(n):
            for j, (px, py) in enumerate(chips):
                sj = 2 * px + py
                got = half(a, sj, c)
                _remote(got, got, ici_send.at[3 * a + j], ici_recv.at[3 * a + j], (px, py, c)).wait_recv()
                fwd = _remote(got, got, d2d_send.at[3 * a + j], d2d_recv.at[3 * a + j], sibling)
                fwd.start()
                passed.append(fwd)
        for j, (px, py) in enumerate(chips):
            got = cw_out.at[2 * px + py]
            _remote(got, got, ici_send.at[3 * n + j], ici_recv.at[3 * n + j], (px, py, c)).wait_recv()
        for a in range(n):
            for j, (px, py) in enumerate(chips):
                got = half(a, 2 * px + py, 1 - c)
                _remote(got, got, d2d_send.at[3 * a + j], d2d_recv.at[3 * a + j], sibling).wait_recv()
        for cp in sends + passed:
            cp.wait_send()

    return pl.pallas_call(
        body, name="gather_weights",
        in_specs=[ANY] * (n + 1), out_specs=[ANY] * (n + 1),
        out_shape=[jax.ShapeDtypeStruct(b.shape, b.dtype) for b in bufs],
        input_output_aliases={i: i for i in range(n + 1)},
        scratch_shapes=[pltpu.SemaphoreType.DMA((3 * n + 3,)), pltpu.SemaphoreType.DMA((3 * n + 3,)),
                        pltpu.SemaphoreType.DMA((3 * n,)), pltpu.SemaphoreType.DMA((3 * n,))],
        compiler_params=pltpu.CompilerParams(has_side_effects=True),
    )(*bufs)


def _swap_halves(grads, tag):
    n = len(grads)

    def body(*refs):
        ins, outs, send_sem, recv_sem = refs[:n], refs[n:2 * n], refs[2 * n], refs[2 * n + 1]
        x, y, c, _ = _position()
        copies = []
        for a in range(n):
            rows = ins[a].shape[1] // 2
            copies.append(_remote(ins[a].at[:, pl.ds((1 - c) * rows, rows), :], outs[a],
                                  send_sem.at[a], recv_sem.at[a], (x, y, 1 - c)))
        for cp in copies:
            cp.start()
        for cp in copies:
            cp.wait()

    return pl.pallas_call(
        body, name="swap_grad_halves_" + tag,
        in_specs=[ANY] * n, out_specs=[ANY] * n,
        out_shape=[jax.ShapeDtypeStruct((g.shape[0], g.shape[1] // 2, g.shape[2]), F32) for g in grads],
        scratch_shapes=[pltpu.SemaphoreType.DMA((n,)), pltpu.SemaphoreType.DMA((n,))],
        compiler_params=pltpu.CompilerParams(has_side_effects=True),
    )(*grads)


def _pair_sum(g, got, c_arr, name):
    n_sh, R, C = g.shape
    rows = R // 2

    def body(c_ref, g_ref, r_ref, f_ref, b_ref):
        t = g_ref[...] + r_ref[...]
        f_ref[...] = t
        b_ref[...] = t.astype(BF16)

    blk = pl.BlockSpec((1, rows, C), lambda i, c_ref: (i, 0, 0))
    return pl.pallas_call(
        body, name=name,
        grid_spec=pltpu.PrefetchScalarGridSpec(
            num_scalar_prefetch=1, grid=(n_sh,),
            in_specs=[pl.BlockSpec((1, rows, C), lambda i, c_ref: (i, c_ref[0], 0)), blk],
            out_specs=[blk, blk]),
        out_shape=[jax.ShapeDtypeStruct((n_sh, rows, C), F32), jax.ShapeDtypeStruct((n_sh, rows, C), BF16)],
        compiler_params=_params(("parallel",)),
    )(c_arr, g, got)


def _scatter_grads(sums_bf16, small_all):
    n = len(sums_bf16)

    def body(*refs):
        b_ins = refs[:n]
        recvs, sm = refs[n + 1:2 * n + 1], refs[2 * n + 1]
        ici_send, ici_recv, sm_send, sm_recv = refs[2 * n + 2:]
        x, y, c, chips = _position()
        me = 4 * x + 2 * y + c
        copies = []
        for a in range(n):
            for j, (px, py) in enumerate(chips):
                copies.append(_remote(b_ins[a].at[2 * px + py], recvs[a].at[j],
                                      ici_send.at[3 * a + j], ici_recv.at[3 * a + j], (px, py, c)))
        for k in range(1, 8):
            peer = (x ^ (k >> 2), y ^ ((k >> 1) & 1), c ^ (k & 1))
            copies.append(_remote(sm.at[me], sm.at[me], sm_send.at[k - 1], sm_recv.at[k - 1], peer))
        for cp in copies:
            cp.start()
        for cp in copies:
            cp.wait_send()
        for a in range(n):
            for j, (px, py) in enumerate(chips):
                _remote(recvs[a].at[j], recvs[a].at[j], ici_send.at[3 * a + j], ici_recv.at[3 * a + j],
                        (px, py, c)).wait_recv()
        for k in range(1, 8):
            peer = (x ^ (k >> 2), y ^ ((k >> 1) & 1), c ^ (k & 1))
            theirs = sm.at[4 * peer[0] + 2 * peer[1] + peer[2]]
            _remote(theirs, theirs, sm_send.at[k - 1], sm_recv.at[k - 1], peer).wait_recv()

    out = pl.pallas_call(
        body, name="scatter_grads",
        in_specs=[ANY] * (n + 1), out_specs=[ANY] * (n + 1),
        out_shape=[jax.ShapeDtypeStruct((3,) + b.shape[1:], BF16) for b in sums_bf16]
        + [jax.ShapeDtypeStruct(small_all.shape, F32)],
        input_output_aliases={n: n},
        scratch_shapes=[pltpu.SemaphoreType.DMA((3 * n,)), pltpu.SemaphoreType.DMA((3 * n,)),
                        pltpu.SemaphoreType.DMA((7,)), pltpu.SemaphoreType.DMA((7,))],
        compiler_params=pltpu.CompilerParams(has_side_effects=True),
    )(*sums_bf16, small_all)
    return out[:n], out[n]


def _shard_sum(sums_f32, recv, shard_arr, c_arr, name):
    _, rows, C = sums_f32.shape

    def body(s_ref, c_ref, o_ref, r_ref, t_ref):
        t_ref[...] = ((o_ref[0] + r_ref[0].astype(F32)) + r_ref[1].astype(F32)) + r_ref[2].astype(F32)

    return pl.pallas_call(
        body, name=name,
        grid_spec=pltpu.PrefetchScalarGridSpec(
            num_scalar_prefetch=2, grid=(1,),
            in_specs=[pl.BlockSpec((1, rows, C), lambda i, s_ref, c_ref: (s_ref[0], 0, 0)),
                      pl.BlockSpec((3, rows, C), lambda i, s_ref, c_ref: (0, 0, 0))],
            out_specs=pl.BlockSpec((rows, C), lambda i, s_ref, c_ref: (c_ref[0], 0))),
        out_shape=jax.ShapeDtypeStruct((2 * rows, C), F32),
        compiler_params=_params(("arbitrary",)),
    )(shard_arr, c_arr, sums_f32, recv)


def _join_halves(bufs):
    n = len(bufs)

    def body(*refs):
        outs, send_sem, recv_sem = refs[n:2 * n], refs[2 * n], refs[2 * n + 1]
        x, y, c, _ = _position()
        copies = []
        for a in range(n):
            rows = outs[a].shape[0] // 2
            mine = outs[a].at[pl.ds(c * rows, rows), :]
            copies.append(_remote(mine, mine, send_sem.at[a], recv_sem.at[a], (x, y, 1 - c)))
        for cp in copies:
            cp.start()
        for a, cp in enumerate(copies):
            cp.wait_send()
            rows = outs[a].shape[0] // 2
            theirs = outs[a].at[pl.ds((1 - c) * rows, rows), :]
            _remote(theirs, theirs, send_sem.at[a], recv_sem.at[a], (x, y, 1 - c)).wait_recv()

    return pl.pallas_call(
        body, name="join_grad_halves",
        in_specs=[ANY] * n, out_specs=[ANY] * n,
        out_shape=[jax.ShapeDtypeStruct(b.shape, F32) for b in bufs],
        input_output_aliases={i: i for i in range(n)},
        scratch_shapes=[pltpu.SemaphoreType.DMA((n,)), pltpu.SemaphoreType.DMA((n,))],
        compiler_params=pltpu.CompilerParams(has_side_effects=True),
    )(*bufs)


def _small_sum(parts, tag):
    _, R, C = parts.shape

    def body(p_ref, o_ref):
        t = p_ref[0]
        for k in range(1, 8):
            t = t + p_ref[k]
        o_ref[...] = t

    return pl.pallas_call(
        body, name="small_grad_sum_" + tag, grid=(1,),
        in_specs=[pl.BlockSpec((8, R, C), lambda i: (0, 0, 0))], out_specs=pl.BlockSpec((R, C), lambda i: (0, 0)),
        out_shape=jax.ShapeDtypeStruct((R, C), F32), compiler_params=_params(("arbitrary",)),
    )(parts)


def _adamw_math(w, g, m, v):
    m = ADAM_B1 * m + (1.0 - ADAM_B1) * g
    v = ADAM_B2 * v + (1.0 - ADAM_B2) * (g * g)
    m_hat = m / (1.0 - ADAM_B1 ** ADAM_STEP)
    v_hat = v / (1.0 - ADAM_B2 ** ADAM_STEP)
    delta = -ADAM_LR * (m_hat / (jnp.sqrt(v_hat) + ADAM_EPS) + ADAM_WD * w)
    return delta, m, v


def _adamw_big(w, g, m, v, name):
    R, C = w.shape
    tr = R // 4

    def body(w_ref, g_ref, m_ref, v_ref, d_ref, nm_ref, nv_ref):
        d_ref[...], nm_ref[...], nv_ref[...] = _adamw_math(w_ref[...], g_ref[...], m_ref[...], v_ref[...])

    blk = pl.BlockSpec((tr, C), lambda i: (i, 0))
    return pl.pallas_call(
        body, name=name, grid=(4,), in_specs=[blk] * 4, out_specs=[blk] * 3,
        out_shape=[jax.ShapeDtypeStruct((R, C), F32)] * 3, compiler_params=_params(("parallel",)),
    )(w, g, m, v)


def _adamw_small(ws, gs, ms, vs):
    n = len(ws)

    def body(*refs):
        for a in range(n):
            w, g, m, v = (refs[k * n + a][...] for k in range(4))
            d, nm, nv = _adamw_math(w, g, m, v)
            refs[4 * n + a][...] = d
            refs[5 * n + a][...] = nm
            refs[6 * n + a][...] = nv

    shapes = [jax.ShapeDtypeStruct(w.shape, F32) for w in ws]
    out = pl.pallas_call(body, name="adamw_small", out_shape=shapes * 3)(*ws, *gs, *ms, *vs)
    return out[:n], out[n:2 * n], out[2 * n:]


BIG = ("w_in", "w_out", "w_up", "w_down")
SMALL = ("g_mix_pre", "pool_w", "pool_scale", "g_mix_post", "g_ffn_pre", "conv_b", "g_ffn_post", "conv_w")
ORDER = ("g_mix_pre", "w_in", "pool_w", "pool_scale", "w_out", "g_mix_post", "g_ffn_pre", "w_up", "conv_w", "conv_b",
         "w_down", "g_ffn_post")


def kernel(x, g_mix_pre, w_in, pool_w, pool_scale, w_out, g_mix_post, g_ffn_pre, w_up, conv_w, conv_b, w_down, g_ffn_post, loss_target, m_g_mix_pre, m_w_in, m_pool_w, m_pool_scale, m_w_out, m_g_mix_post, m_g_ffn_pre, m_w_up, m_conv_w, m_conv_b, m_w_down, m_g_ffn_post, v_g_mix_pre, v_w_in, v_pool_w, v_pool_scale, v_w_out, v_g_mix_post, v_g_ffn_pre, v_w_up, v_conv_w, v_conv_b, v_w_down, v_g_ffn_post):
    args = dict(locals())
    W = {n: args[n][0] for n in ORDER}
    M = {n: args["m_" + n][0] for n in ORDER}
    V = {n: args["v_" + n][0] for n in ORDER}
    for d in (W, M, V):
        d["pool_w"] = d["pool_w"].reshape(-1, POOL_GROUP)
        for n in ("g_mix_pre", "pool_scale", "g_mix_post", "g_ffn_pre", "conv_b", "g_ffn_post"):
            d[n] = d[n].reshape(1, -1)
    CW = W["w_up"].shape[1]
    c_arr = lax.axis_index("c").astype(jnp.int32).reshape(1)
    shard = 2 * lax.axis_index("x") + lax.axis_index("y")
    shard_arr = shard.astype(jnp.int32).reshape(1)
    device = 2 * shard + lax.axis_index("c")

    conv_w_slots = lax.dynamic_update_index_in_dim(jnp.zeros((N_SHARD,) + W["conv_w"].shape, F32), W["conv_w"], shard, 0)
    slots = {n: _cast_bf16(W[n], shard_arr, "cast_" + n) for n in BIG}
    w_in_g, conv_w_g = _gather_weights([slots["w_in"], conv_w_slots])
    conv_w_full = conv_w_g.transpose(1, 0, 2).reshape(CONV_WIDTH, 1, N_SHARD * CW)

    loss, grad_x, G = _local_step(
        x[0], loss_target[0], W["g_mix_pre"], w_in_g, W["pool_w"].reshape(-1, POOL_GROUP, POOL_GROUP), W["pool_scale"],
        slots["w_out"], W["g_mix_post"], W["g_ffn_pre"], slots["w_up"], conv_w_full, W["conv_b"],
        slots["w_down"], W["g_ffn_post"], (c_arr, device))

    late = ("w_in", "w_out")
    from_sibling = _swap_halves([G[n] for n in late], "mix")
    sums = {n: _pair_sum(G[n], r, c_arr, "pair_sum_" + n) for n, r in zip(late, from_sibling)}
    small = _pack_small(G, SMALL_LATE)
    small_slots = lax.dynamic_update_index_in_dim(jnp.zeros((8,) + small.shape, F32), small, device, 0)
    recvs, small_all = _scatter_grads([sums[n][1] for n in late], small_slots)
    reduced = {n: (sums[n][0], r) for n, r in zip(late, recvs)}
    reduced.update({n: G[n] for n in ("w_up", "w_down")})
    halves = [_shard_sum(*reduced[n], shard_arr, c_arr, "shard_sum_" + n) for n in BIG]
    full = dict(zip(BIG, _join_halves(halves)))
    full.update(_unpack_small(_small_sum(G["small_early"], "early"), SMALL_EARLY, W, shard))
    full.update(_unpack_small(_small_sum(small_all, "late"), SMALL_LATE, W, shard))

    delta, new_m, new_v = {}, {}, {}
    for n in BIG:
        delta[n], new_m[n], new_v[n] = _adamw_big(W[n], full[n], M[n], V[n], "adamw_" + n)
    ds, nms, nvs = _adamw_small([W[n] for n in SMALL], [full[n] for n in SMALL], [M[n] for n in SMALL],
                                [V[n] for n in SMALL])
    for n, d, nm, nv in zip(SMALL, ds, nms, nvs):
        delta[n], new_m[n], new_v[n] = d, nm, nv

    loss = lax.psum(loss[0, 0], ("x", "y", "c"))
    shaped = lambda d: [d[n].reshape(args[n].shape) for n in ORDER]
    return (loss, grad_x[None], *shaped(full), *shaped(delta), *shaped(new_m), *shaped(new_v))
```

```python
import functools

import jax
import jax.numpy as jnp
from jax import lax
from jax.experimental import pallas as pl
from jax.experimental.pallas import tpu as pltpu

F32 = jnp.float32
BF16 = jnp.bfloat16

RMS_EPS = 1e-6
NEG_INF = -1e30
N_HEADS = 8
HEAD_DIM = 64
ATTN_WIDTH = N_HEADS * HEAD_DIM
ATTN_SCALE = HEAD_DIM ** -0.5
ATTN_BLOCK = 128
DILATIONS = (1, 4, 16)
RESIDUES_PER_STEP = 4
POOL_WINDOWS = (2, 4, 8, 16)
POOL_GROUP = 128
POOL_WIDTH = POOL_GROUP * len(POOL_WINDOWS)
POOL_HALO = 16
CONV_WIDTH = 3
CONV_HALO = 8
N_SHARD = 4
LANES = 128

ADAM_LR = 0.001
ADAM_B1 = 0.9
ADAM_B2 = 0.999
ADAM_EPS = 1e-08
ADAM_WD = 0.01
ADAM_STEP = 10

VMEM_LIMIT = 60 * 1024 * 1024
MESH = pl.DeviceIdType.MESH
NT = (((1,), (1,)), ((), ()))
TN = (((0,), (0,)), ((), ()))


def _params(sem, vmem=None):
    return pltpu.CompilerParams(dimension_semantics=sem, vmem_limit_bytes=vmem)


def _const_spec(shape):
    zeros = (0,) * len(shape)
    return pl.BlockSpec(shape, lambda *_: zeros, pipeline_mode=pl.Buffered(1))


def _dot(a, b):
    return jnp.dot(a, b, preferred_element_type=F32)


def _dot_nt(a, b):
    return lax.dot_general(a, b, NT, preferred_element_type=F32)


def _dot_tn(a, b):
    return lax.dot_general(a, b, TN, preferred_element_type=F32)


def _rms_stats(x):
    r = lax.rsqrt(jnp.mean(x * x, axis=-1, keepdims=True) + RMS_EPS)
    return x * r, r


def _rms_bwd(dy, n, r, g):
    dg = jnp.sum(dy * n, axis=0, keepdims=True)
    dn = dy * g
    dx = r * (dn - n * jnp.mean(dn * n, axis=-1, keepdims=True))
    return dx, dg


def _gelu_tanh(g):
    k = 0.7978845608028654
    kc = k * 0.044715
    g2 = g * g
    t = jnp.tanh(g * (k + kc * g2))
    h = 0.5 * t + 0.5
    dh = (0.5 - 0.5 * (t * t)) * (k + (3.0 * kc) * g2)
    return g * h, h + g * dh


def _residue_shape(S, d, dtype):
    return jax.ShapeDtypeStruct((S // d, d * ATTN_WIDTH), dtype)


def _residue_spec(TM, d):
    return pl.BlockSpec((TM // d, d * ATTN_WIDTH), lambda i: (i, 0))


def _token_scratch(TM):
    return [pltpu.VMEM((TM, LANES), F32)] * (ATTN_WIDTH // LANES)


def _put_tokens(dst_s, val):
    for cb, chunk in enumerate(dst_s):
        chunk[...] = val[:, cb * LANES:(cb + 1) * LANES]


def _get_tokens(src_s):
    return jnp.concatenate([chunk[...] for chunk in src_s], axis=1)


def _to_residue(val, src_s, out_ref, d, dtype):
    if d == 1:
        out_ref[...] = val.astype(dtype)
        return
    rows = src_s[0].shape[0]
    for r in range(d):
        for cb, chunk in enumerate(src_s):
            col = r * ATTN_WIDTH + cb * LANES
            out_ref[:, col:col + LANES] = chunk[pl.ds(r, rows // d, stride=d), :].astype(dtype)


def _from_residue(in_ref, dst_s, d):
    if d == 1:
        return in_ref[...].astype(F32)
    rows = dst_s[0].shape[0]
    for r in range(d):
        for cb, chunk in enumerate(dst_s):
            col = r * ATTN_WIDTH + cb * LANES
            chunk[pl.ds(r, rows // d, stride=d), :] = in_ref[:, col:col + LANES].astype(F32)
    return _get_tokens(dst_s)


def _mix_in_fwd(x, g_pre, w_in, cargo=()):
    S, D = x.shape
    TM = 512
    nc = len(cargo)
    kinds = [kind for kind, _ in cargo]
    n_chunks = ATTN_WIDTH // LANES

    def body(x_ref, g_ref, w_ref, *refs):
        cargo_in, refs = refs[:nc], refs[nc:]
        qkv_refs, p_ref, h_ref = refs[:9], refs[9], refs[10]
        t_s = refs[11 + nc:11 + nc + n_chunks]
        cargo_refs = (kinds, cargo_in, refs[11:11 + nc], refs[11 + nc + n_chunks:])
        if nc:
            _cargo_start(*cargo_refs, pl.program_id(0) == 0)
        n, _ = _rms_stats(x_ref[...])
        hb = (n * g_ref[...]).astype(BF16)
        h_ref[...] = hb
        for a in range(3):
            res = _dot(hb, w_ref[a])
            if a == 0:
                res = res * ATTN_SCALE
            _put_tokens(t_s, res)
            for i, d in enumerate(DILATIONS):
                _to_residue(res, t_s, qkv_refs[3 * i + a], d, BF16)
        p_ref[...] = _dot(hb, w_ref[3])
        if nc:
            _cargo_finish(*cargo_refs, pl.program_id(0) == S // TM - 1)

    row = lambda w: pl.BlockSpec((TM, w), lambda i: (i, 0))
    arrays, cargo_specs, shapes, aliases, sems = _cargo_call(cargo, 3, 11)
    out = pl.pallas_call(
        body, name="mix_in_fwd", grid=(S // TM,),
        in_specs=[row(D), _const_spec((1, D)), _const_spec(w_in.shape)] + cargo_specs,
        out_specs=[_residue_spec(TM, d) for d in DILATIONS for _ in range(3)] + [row(POOL_WIDTH), row(D)] + cargo_specs,
        out_shape=[_residue_shape(S, d, BF16) for d in DILATIONS for _ in range(3)]
        + [jax.ShapeDtypeStruct((S, POOL_WIDTH), F32), jax.ShapeDtypeStruct((S, D), BF16)] + shapes,
        input_output_aliases=aliases,
        scratch_shapes=_token_scratch(TM) + sems,
        compiler_params=_params(("arbitrary",), VMEM_LIMIT),
    )(x, g_pre, w_in, *arrays)
    return [out[0:3], out[3:6], out[6:9]], out[9], out[10], out[11:]


def _band_mask(n):
    qi = lax.broadcasted_iota(jnp.int32, (ATTN_BLOCK, 2 * ATTN_BLOCK), 0)
    ki = lax.broadcasted_iota(jnp.int32, (ATTN_BLOCK, 2 * ATTN_BLOCK), 1)
    dist = qi + ATTN_BLOCK - ki
    return (dist >= 0) & (dist <= ATTN_BLOCK) & ((ki >= ATTN_BLOCK) | (n > 0))


def _first_head_lanes():
    return lax.broadcasted_iota(jnp.int32, (1, LANES), 1) < HEAD_DIM


def _stack_heads(pair, first):
    zero = jnp.zeros_like(pair)
    return jnp.concatenate([jnp.where(first, pair, zero), jnp.where(first, zero, pair)], axis=0)


def _unstack_heads(stacked, first):
    return jnp.where(first, stacked[:ATTN_BLOCK], stacked[ATTN_BLOCK:])


CARGO_COPIES = {"ici": 3, "d2d": 3, "scatter": 3, "swap": 1, "everyone": 7}
CARGO_IN_PLACE = ("ici", "d2d", "everyone")


def _cargo_copies(kinds, ins, outs, send_sems, recv_sems, want_recvs=True):
    x, y, c, chips = _position()
    s = 2 * x + y
    me = 2 * s + c
    sibling = (x, y, 1 - c)
    sends, recvs = [], []

    def add(k, src, dst, landing, to):
        sends.append(_remote(src, dst, send_sems.at[k], recv_sems.at[k], to))
        if want_recvs:
            recvs.append(_remote(landing, landing, send_sems.at[k], recv_sems.at[k], to))

    k0 = 0
    for a, kind in enumerate(kinds):
        if kind == "swap":
            rows = ins[a].shape[1] // 2
            add(k0, ins[a].at[:, pl.ds((1 - c) * rows, rows), :], outs[a], outs[a], sibling)
        elif kind == "everyone":
            for m in range(1, 8):
                peer = (x ^ (m >> 2), y ^ ((m >> 1) & 1), c ^ (m & 1))
                add(k0 + m - 1, outs[a].at[me], outs[a].at[me], outs[a].at[4 * peer[0] + 2 * peer[1] + peer[2]], peer)
        else:
            for j, (px, py) in enumerate(chips):
                sj = 2 * px + py
                if kind == "scatter":
                    add(k0 + j, ins[a].at[sj], outs[a].at[j], outs[a].at[j], (px, py, c))
                    continue
                buf = outs[a]
                rows = buf.shape[1] // 2
                half = lambda shard, h: buf.at[shard, pl.ds(h * rows, rows), :]
                if kind == "ici":
                    add(k0 + j, half(s, c), half(s, c), half(sj, c), (px, py, c))
                else:
                    add(k0 + j, half(sj, c), half(sj, c), half(sj, 1 - c), sibling)
        k0 += CARGO_COPIES[kind]
    return sends, recvs


def _cargo_start(kinds, ins, outs, sems, first_step):
    @pl.when(first_step)
    def _():
        for cp in _cargo_copies(kinds, ins, outs, *sems, want_recvs=False)[0]:
            cp.start()


def _cargo_finish(kinds, ins, outs, sems, last_step):
    @pl.when(last_step)
    def _():
        sends, recvs = _cargo_copies(kinds, ins, outs, *sems)
        for cp in sends:
            cp.wait_send()
        for cp in recvs:
            cp.wait_recv()


def _cargo_call(cargo, n_in, n_out):
    arrays = [a for _, a in cargo]
    shapes = []
    for kind, a in cargo:
        shape = {"scatter": (3,) + a.shape[1:], "swap": (a.shape[0], a.shape[1] // 2, a.shape[2])}.get(kind, a.shape)
        shapes.append(jax.ShapeDtypeStruct(shape, a.dtype))
    aliases = {n_in + i: n_out + i for i, (kind, _) in enumerate(cargo) if kind in CARGO_IN_PLACE}
    n_sems = sum(CARGO_COPIES[kind] for kind, _ in cargo)
    sems = [pltpu.SemaphoreType.DMA((n_sems,))] * 2 if cargo else []
    return arrays, [ANY] * len(cargo), shapes, aliases, sems


def _attn_fwd(q, k, v, d, cargo=()):
    L = q.shape[0]
    group = min(d, RESIDUES_PER_STEP)
    width = group * ATTN_WIDTH
    qb = RESIDUES_PER_STEP // group
    steps = L // (qb * ATTN_BLOCK)
    nc = len(cargo)
    kinds = [kind for kind, _ in cargo]

    def body(*refs):
        q_ref, kp_ref, kc_ref, vp_ref, vc_ref = refs[:5]
        o_ref, lse_ref = refs[5 + nc:7 + nc]
        cargo_refs = (kinds, refs[5:5 + nc], refs[7 + nc:7 + 2 * nc], refs[7 + 2 * nc:])
        r, n = pl.program_id(0), pl.program_id(1)
        if nc:
            _cargo_start(*cargo_refs, (r == 0) & (n == 0))
        first = _first_head_lanes()
        for sub in range(qb):
            rows = slice(sub * ATTN_BLOCK, (sub + 1) * ATTN_BLOCK)
            valid = _band_mask(n if sub == 0 else 1)
            valid2 = jnp.concatenate([valid, valid], axis=0)
            for hp in range(width // LANES):
                sl = slice(hp * LANES, (hp + 1) * LANES)
                if sub == 0:
                    kk = jnp.concatenate([kp_ref[:, sl], kc_ref[rows, sl]], axis=0)
                    vv = jnp.concatenate([vp_ref[:, sl], vc_ref[rows, sl]], axis=0)
                else:
                    keys = slice((sub - 1) * ATTN_BLOCK, (sub + 1) * ATTN_BLOCK)
                    kk, vv = kc_ref[keys, sl], vc_ref[keys, sl]
                s = jnp.where(valid2, _dot_nt(_stack_heads(q_ref[rows, sl], first), kk), NEG_INF)
                m = jnp.max(s, axis=-1, keepdims=True)
                p = jnp.exp(s - m)
                den = jnp.sum(p, axis=-1, keepdims=True)
                o_ref[rows, sl] = _unstack_heads(_dot(p.astype(BF16), vv) / den, first)
                lse_ref[rows, sl] = _unstack_heads(m + jnp.log(den), first)
        if nc:
            _cargo_finish(*cargo_refs, (r == d // group - 1) & (n == steps - 1))

    cur = pl.BlockSpec((qb * ATTN_BLOCK, width), lambda r, n: (n, r))
    prev = pl.BlockSpec((ATTN_BLOCK, width), lambda r, n: (jnp.maximum(n * qb - 1, 0), r))
    arrays, specs, shapes, aliases, sems = _cargo_call(cargo, 5, 2)
    out = pl.pallas_call(
        body, name=f"attn_fwd_d{d}", grid=(d // group, steps),
        in_specs=[cur, prev, cur, prev, cur] + specs,
        out_specs=[cur, cur] + specs,
        out_shape=[jax.ShapeDtypeStruct((L, d * ATTN_WIDTH), F32)] * 2 + shapes,
        input_output_aliases=aliases, scratch_shapes=sems,
        compiler_params=_params(("arbitrary", "arbitrary")),
    )(q, k, k, v, v, *arrays)
    return out[0], out[1], out[2:]


def _attn_mix(outs, lses):
    S = outs[0].shape[0]
    TM = 512
    n = len(DILATIONS)

    def body(*refs):
        o_refs, l_refs, attn_ref, lse_refs, t_s = refs[:n], refs[n:2 * n], refs[2 * n], refs[2 * n + 1:3 * n + 1], refs[3 * n + 1:]
        os = [_from_residue(o_refs[i], t_s, d) for i, d in enumerate(DILATIONS)]
        ls = [_from_residue(l_refs[i], t_s, d) for i, d in enumerate(DILATIONS)]
        m = jnp.maximum(jnp.maximum(ls[0], ls[1]), ls[2])
        es = [jnp.exp(l - m) for l in ls]
        den = es[0] + es[1] + es[2]
        attn_ref[...] = (es[0] * os[0] + es[1] * os[1] + es[2] * os[2]) / den
        lse = m + jnp.log(den)
        _put_tokens(t_s, lse)
        for i, d in enumerate(DILATIONS):
            _to_residue(lse, t_s, lse_refs[i], d, F32)

    specs = [_residue_spec(TM, d) for d in DILATIONS]
    out = pl.pallas_call(
        body, name="attn_mix", grid=(S // TM,),
        in_specs=specs * 2, out_specs=[specs[0]] + specs,
        out_shape=[jax.ShapeDtypeStruct((S, ATTN_WIDTH), F32)] + [_residue_shape(S, d, F32) for d in DILATIONS],
        scratch_shapes=_token_scratch(TM),
        compiler_params=_params(("parallel",)),
    )(*outs, *lses)
    return out[0], out[1:]


def _pool_counts(first_row, rows, w):
    t = first_row + lax.broadcasted_iota(jnp.int32, (rows, 1), 0)
    return jnp.minimum(t + 1, w).astype(F32)


def _trailing_sums(xe, w):
    s, k = xe, 1
    while k < w:
        s = s + pltpu.roll(s, k, 0)
        k *= 2
    return s


def _leading_sums(xe, w):
    rows = xe.shape[0]
    s, k = xe, 1
    while k < w:
        s = s + pltpu.roll(s, rows - k, 0)
        k *= 2
    return s


def _pooled_groups(halo, cur, first_row):
    TM = cur.shape[0]
    xe = jnp.concatenate([halo, cur], axis=0)
    out = []
    for g, w in enumerate(POOL_WINDOWS):
        a = xe[:, g * POOL_GROUP:(g + 1) * POOL_GROUP]
        s = _trailing_sums(a, w)[POOL_HALO:]
        out.append(s / _pool_counts(first_row, TM, w) - a[POOL_HALO:])
    return out


def _pool_fwd(pool_in, pool_w, pool_scale):
    S = pool_in.shape[0]
    TM = 512
    HB = TM // POOL_HALO

    def body(cur_ref, halo_ref, w_ref, sc_ref, y_ref):
        i = pl.program_id(0)
        halo = jnp.where(i > 0, halo_ref[...], 0.0)
        pooled = _pooled_groups(halo, cur_ref[...], i * TM)
        for g in range(len(POOL_WINDOWS)):
            sl = slice(g * POOL_GROUP, (g + 1) * POOL_GROUP)
            y = _dot(pooled[g].astype(BF16), w_ref[g].astype(BF16)) * sc_ref[:, sl]
            y_ref[:, sl] = y.astype(BF16)

    return pl.pallas_call(
        body, name="pool_fwd", grid=(S // TM,),
        in_specs=[pl.BlockSpec((TM, POOL_WIDTH), lambda i: (i, 0)),
                  pl.BlockSpec((POOL_HALO, POOL_WIDTH), lambda i: (jnp.maximum(i * HB - 1, 0), 0)),
                  _const_spec(pool_w.shape), _const_spec((1, POOL_WIDTH))],
        out_specs=pl.BlockSpec((TM, POOL_WIDTH), lambda i: (i, 0)),
        out_shape=jax.ShapeDtypeStruct((S, POOL_WIDTH), BF16),
        compiler_params=_params(("parallel",)),
    )(pool_in, pool_in, pool_w, pool_scale)


def _mix_out_fwd(attn, pool, w_out, x, g_post, g_ffn_pre):
    S, D = x.shape
    TM = 512

    def body(a_ref, p_ref, w_ref, x_ref, gp_ref, gf_ref, mixed_ref, x1_ref, h2_ref, cat_ref):
        ab = a_ref[...].astype(BF16)
        cat_ref[:, :ATTN_WIDTH] = ab
        cat_ref[:, ATTN_WIDTH:] = p_ref[...]
        mixed = _dot(ab, w_ref[:ATTN_WIDTH, :]) + _dot(p_ref[...], w_ref[ATTN_WIDTH:, :])
        mixed_ref[...] = mixed
        n, _ = _rms_stats(mixed)
        x1 = x_ref[...] + n * gp_ref[...]
        x1_ref[...] = x1
        n2, _ = _rms_stats(x1)
        h2_ref[...] = (n2 * gf_ref[...]).astype(BF16)

    row = lambda w: pl.BlockSpec((TM, w), lambda i: (i, 0))
    return pl.pallas_call(
        body, name="mix_out_fwd", grid=(S // TM,),
        in_specs=[row(ATTN_WIDTH), row(POOL_WIDTH), _const_spec(w_out.shape), row(D),
                  _const_spec((1, D)), _const_spec((1, D))],
        out_specs=[row(D), row(D), row(D), row(D)],
        out_shape=[jax.ShapeDtypeStruct((S, D), F32), jax.ShapeDtypeStruct((S, D), F32),
                   jax.ShapeDtypeStruct((S, D), BF16), jax.ShapeDtypeStruct((S, D), BF16)],
        compiler_params=_params(("parallel",), VMEM_LIMIT),
    )(attn, pool, w_out, x, g_post, g_ffn_pre)


def _ffn_fwd(h2, x1, target, w_up, w_down, conv_w, conv_b, g_post):
    S, D = x1.shape
    CW = w_up.shape[2]
    FF = 2 * CW
    TM = 256
    piece = 4 * LANES
    pieces = [(lo, min(lo + piece, CW)) for lo in range(0, CW, piece)]

    def body(h2_ref, x1_ref, t_ref, wu_ref, wd_ref, cw_ref, cb_ref, g_ref,
             yv_ref, dy_ref, df_ref, dc_ref, loss_ref, dg_ref, dcb_ref, dcw_ref,
             ue_s, dgate_s, dval_s):
        i = pl.program_id(0)

        @pl.when(i == 0)
        def _():
            loss_ref[...] = jnp.zeros_like(loss_ref)
            dg_ref[...] = jnp.zeros_like(dg_ref)
            dcb_ref[...] = jnp.zeros_like(dcb_ref)
            dcw_ref[...] = jnp.zeros_like(dcw_ref)
            ue_s[0:CONV_HALO, :] = jnp.zeros((CONV_HALO, 2 * FF), F32)

        @pl.when(i > 0)
        def _():
            ue_s[0:CONV_HALO, :] = ue_s[TM:TM + CONV_HALO, :]

        def shifted(cols, k):
            return pltpu.roll(ue_s[:, cols], k, 0)[CONV_HALO:]

        def conv(cols):
            return (cb_ref[:, cols] + cw_ref[2, :, cols] * ue_s[CONV_HALO:, cols]
                    + cw_ref[1, :, cols] * shifted(cols, 1) + cw_ref[0, :, cols] * shifted(cols, 2))

        hb = h2_ref[...]
        f = jnp.zeros((TM, D), F32)
        for j in range(2):
            jc = slice(j * CW, (j + 1) * CW)
            for half in range(2):
                blk = 2 * half + j
                cols = slice(blk * CW, (blk + 1) * CW)
                ue_s[CONV_HALO:, cols] = _dot(hb, wu_ref[blk])
            for lo, hi in pieces:
                pc = slice(j * CW + lo, j * CW + hi)
                gelu, dgelu = _gelu_tanh(conv(pc))
                val = conv(slice(FF + j * CW + lo, FF + j * CW + hi))
                dgate_s[:, pc] = val * dgelu
                dval_s[:, pc] = gelu
                yv_ref[:, pc] = (gelu * val).astype(BF16)
            f = f + _dot(yv_ref[:, jc], wd_ref[jc, :])

        n, r = _rms_stats(f)
        err = x1_ref[...] + n * g_ref[...] - t_ref[...]
        loss_ref[...] += 0.5 * jnp.sum(jnp.mean(err * err, axis=-1, keepdims=True), axis=0, keepdims=True)
        dy = err / D
        dy_ref[...] = dy
        df, dg = _rms_bwd(dy, n, r, g_ref[...])
        dg_ref[...] += dg
        dfb = df.astype(BF16)
        df_ref[...] = dfb

        for j in range(2):
            jc = slice(j * CW, (j + 1) * CW)
            dyv = _dot_nt(dfb, wd_ref[jc, :])
            for lo, hi in pieces:
                pc = slice(j * CW + lo, j * CW + hi)
                for half, scale_s in ((0, dgate_s), (1, dval_s)):
                    cols = slice(half * FF + j * CW + lo, half * FF + j * CW + hi)
                    dcv = dyv[:, lo:hi] * scale_s[:, pc]
                    dc_ref[:, cols] = dcv.astype(BF16)
                    dcb_ref[:, cols] += jnp.sum(dcv, axis=0, keepdims=True)
                    dcw_ref[2, :, cols] += jnp.sum(dcv * ue_s[CONV_HALO:, cols], axis=0, keepdims=True)
                    dcw_ref[1, :, cols] += jnp.sum(dcv * shifted(cols, 1), axis=0, keepdims=True)
                    dcw_ref[0, :, cols] += jnp.sum(dcv * shifted(cols, 2), axis=0, keepdims=True)

    row = lambda w: pl.BlockSpec((TM, w), lambda i: (i, 0))
    acc = lambda shape: pl.BlockSpec(shape, lambda i: (0,) * len(shape))
    return pl.pallas_call(
        body, name="ffn_fwd", grid=(S // TM,),
        in_specs=[row(D), row(D), row(D), _const_spec(w_up.shape), _const_spec(w_down.shape),
                  _const_spec(conv_w.shape), _const_spec((1, 2 * FF)), _const_spec((1, D))],
        out_specs=[row(FF), row(D), row(D), row(2 * FF),
                   acc((1, 1)), acc((1, D)), acc((1, 2 * FF)), acc((CONV_WIDTH, 1, 2 * FF))],
        out_shape=[jax.ShapeDtypeStruct((S, FF), BF16),
                   jax.ShapeDtypeStruct((S, D), F32), jax.ShapeDtypeStruct((S, D), BF16),
                   jax.ShapeDtypeStruct((S, 2 * FF), BF16),
                   jax.ShapeDtypeStruct((1, 1), F32), jax.ShapeDtypeStruct((1, D), F32),
                   jax.ShapeDtypeStruct((1, 2 * FF), F32), jax.ShapeDtypeStruct((CONV_WIDTH, 1, 2 * FF), F32)],
        scratch_shapes=[pltpu.VMEM((TM + CONV_HALO, 2 * FF), F32), pltpu.VMEM((TM, FF), F32),
                        pltpu.VMEM((TM, FF), F32)],
        compiler_params=_params(("arbitrary",), VMEM_LIMIT),
    )(h2, x1, target, w_up, w_down, conv_w, conv_b, g_post)


def _ffn_bwd(dc, conv_w, w_up, x1, g_ffn_pre, dy):
    S, D = x1.shape
    CW = w_up.shape[2]
    F2 = 4 * CW
    TM = 256
    HB = TM // CONV_HALO
    last = S // CONV_HALO - 1
    n_tiles = S // TM

    def body(dc_ref, halo_ref, cw_ref, wu_ref, x1_ref, g_ref, dy_ref, du_ref, dx1_ref, dg_ref):
        i = pl.program_id(0)

        @pl.when(i == 0)
        def _():
            dg_ref[...] = jnp.zeros_like(dg_ref)

        keep = i < n_tiles - 1
        dh2 = jnp.zeros((TM, D), F32)
        for blk in range(N_SHARD):
            cols = slice(blk * CW, (blk + 1) * CW)
            halo = jnp.where(keep, halo_ref[:, cols].astype(F32), 0.0)
            dce = jnp.concatenate([dc_ref[:, cols].astype(F32), halo], axis=0)
            rows = TM + CONV_HALO
            du = (cw_ref[2, :, cols] * dce[:TM]
                  + cw_ref[1, :, cols] * pltpu.roll(dce, rows - 1, 0)[:TM]
                  + cw_ref[0, :, cols] * pltpu.roll(dce, rows - 2, 0)[:TM])
            dub = du.astype(BF16)
            du_ref[:, cols] = dub
            dh2 = dh2 + _dot_nt(dub, wu_ref[blk])
        n2, r2 = _rms_stats(x1_ref[...])
        dx, dg = _rms_bwd(dh2, n2, r2, g_ref[...])
        dg_ref[...] += dg
        dx1_ref[...] = dy_ref[...] + dx

    row = lambda w: pl.BlockSpec((TM, w), lambda i: (i, 0))
    return pl.pallas_call(
        body, name="ffn_bwd", grid=(S // TM,),
        in_specs=[row(F2), pl.BlockSpec((CONV_HALO, F2), lambda i: (jnp.minimum((i + 1) * HB, last), 0)),
                  _const_spec(conv_w.shape), _const_spec(w_up.shape), row(D), _const_spec((1, D)), row(D)],
        out_specs=[row(F2), row(D), pl.BlockSpec((1, D), lambda i: (0, 0))],
        out_shape=[jax.ShapeDtypeStruct((S, F2), BF16), jax.ShapeDtypeStruct((S, D), F32),
                   jax.ShapeDtypeStruct((1, D), F32)],
        compiler_params=_params(("arbitrary",), VMEM_LIMIT),
    )(dc, dc, conv_w, w_up, x1, g_ffn_pre, dy)


def _matmul_tn(a, b, n_blocks, name):
    S, M = a.shape
    N = b.shape[1]
    tn = N // n_blocks
    tm = M if M <= 1024 else M // 2
    tk = 2048
    nk = S // tk

    def body(a_ref, b_ref, o_ref):
        @pl.when(pl.program_id(2) == 0)
        def _():
            o_ref[...] = jnp.zeros_like(o_ref)
        o_ref[0] += _dot_tn(a_ref[...], b_ref[...])

    return pl.pallas_call(
        body, name=name, grid=(M // tm, n_blocks, nk),
        in_specs=[pl.BlockSpec((tk, tm), lambda i, j, k: (k, i)), pl.BlockSpec((tk, tn), lambda i, j, k: (k, j))],
        out_specs=pl.BlockSpec((1, tm, tn), lambda i, j, k: (j, i, 0)),
        out_shape=jax.ShapeDtypeStruct((n_blocks, M, tn), F32),
        compiler_params=_params(("parallel", "parallel", "arbitrary"), VMEM_LIMIT),
    )(a, b)


def _mix_out_bwd(dx1, mixed, g_post, w_out, attn, cargo=()):
    S, D = dx1.shape
    TM = 512
    nd = len(DILATIONS)
    nc = len(cargo)
    kinds = [kind for kind, _ in cargo]
    n_chunks = ATTN_WIDTH // LANES

    def body(*refs):
        dx_ref, m_ref, g_ref, w_ref, a_ref = refs[:5]
        dm_ref, dp_ref, dg_ref = refs[5 + nc:8 + nc]
        da_refs, dl_refs = refs[8 + nc:8 + nc + nd], refs[8 + nc + nd:8 + nc + 2 * nd]
        n_out = 8 + nc + 2 * nd
        t_s = refs[n_out + nc:n_out + nc + n_chunks]
        cargo_refs = (kinds, refs[5:5 + nc], refs[n_out:n_out + nc], refs[n_out + nc + n_chunks:])
        if nc:
            _cargo_start(*cargo_refs, pl.program_id(0) == 0)

        @pl.when(pl.program_id(0) == 0)
        def _():
            dg_ref[...] = jnp.zeros_like(dg_ref)

        n, r = _rms_stats(m_ref[...])
        dm, dg = _rms_bwd(dx_ref[...], n, r, g_ref[...])
        dg_ref[...] += dg
        dmb = dm.astype(BF16)
        dm_ref[...] = dmb
        da = _dot_nt(dmb, w_ref[:ATTN_WIDTH, :])
        _put_tokens(t_s, da)
        for i, d in enumerate(DILATIONS):
            _to_residue(da, t_s, da_refs[i], d, BF16)
        dp_ref[...] = _dot_nt(dmb, w_ref[ATTN_WIDTH:, :])
        prod = da * a_ref[...]
        hi = prod.astype(BF16)
        lo = (prod - hi.astype(F32)).astype(BF16)
        ri = lax.broadcasted_iota(jnp.int32, (ATTN_WIDTH, ATTN_WIDTH), 0) // HEAD_DIM
        ci = lax.broadcasted_iota(jnp.int32, (ATTN_WIDTH, ATTN_WIDTH), 1) // HEAD_DIM
        ones = (ri == ci).astype(BF16)
        delta = _dot(hi, ones) + _dot(lo, ones)
        _put_tokens(t_s, delta)
        for i, d in enumerate(DILATIONS):
            _to_residue(delta, t_s, dl_refs[i], d, F32)
        if nc:
            _cargo_finish(*cargo_refs, pl.program_id(0) == S // TM - 1)

    row = lambda w: pl.BlockSpec((TM, w), lambda i: (i, 0))
    specs = [_residue_spec(TM, d) for d in DILATIONS]
    arrays, cargo_specs, shapes, aliases, sems = _cargo_call(cargo, 5, 3 + 2 * nd)
    out = pl.pallas_call(
        body, name="mix_out_bwd", grid=(S // TM,),
        in_specs=[row(D), row(D), _const_spec((1, D)), _const_spec(w_out.shape), row(ATTN_WIDTH)] + cargo_specs,
        out_specs=[row(D), row(POOL_WIDTH), pl.BlockSpec((1, D), lambda i: (0, 0))] + specs * 2 + cargo_specs,
        out_shape=[jax.ShapeDtypeStruct((S, D), BF16), jax.ShapeDtypeStruct((S, POOL_WIDTH), F32),
                   jax.ShapeDtypeStruct((1, D), F32)]
        + [_residue_shape(S, d, BF16) for d in DILATIONS] + [_residue_shape(S, d, F32) for d in DILATIONS] + shapes,
        input_output_aliases=aliases,
        scratch_shapes=_token_scratch(TM) + sems,
        compiler_params=_params(("arbitrary",), VMEM_LIMIT),
    )(dx1, mixed, g_post, w_out, attn, *arrays)
    return out[0], out[1], out[2], out[3:3 + nd], out[3 + nd:3 + 2 * nd], out[3 + 2 * nd:]


def _pool_bwd(pool_in, d_pool, pool_w, pool_scale):
    S = pool_in.shape[0]
    TM = 512
    HB = TM // POOL_HALO
    last = S // POOL_HALO - 1
    G = len(POOL_WINDOWS)

    def body(cur_ref, halo_ref, dcur_ref, dnext_ref, w_ref, sc_ref, dxin_ref, dw_ref, dsc_ref):
        i = pl.program_id(0)

        @pl.when(i == 0)
        def _():
            dw_ref[...] = jnp.zeros_like(dw_ref)
            dsc_ref[...] = jnp.zeros_like(dsc_ref)

        halo = jnp.where(i > 0, halo_ref[...], 0.0)
        pooled = _pooled_groups(halo, cur_ref[...], i * TM)
        dnext = jnp.where(i < S // TM - 1, dnext_ref[...], 0.0)
        dye = jnp.concatenate([dcur_ref[...], dnext], axis=0)
        for g, w in enumerate(POOL_WINDOWS):
            sl = slice(g * POOL_GROUP, (g + 1) * POOL_GROUP)
            wg = w_ref[g].astype(BF16)
            pb = pooled[g].astype(BF16)
            dsc_ref[:, sl] += jnp.sum(dye[:TM, sl] * _dot(pb, wg), axis=0, keepdims=True)
            dpre = (dye[:, sl] * sc_ref[:, sl]).astype(BF16)
            dw_ref[g] += _dot_tn(pb, dpre[:TM])
            dpooled = _dot_nt(dpre, wg)
            z = dpooled / _pool_counts(i * TM, TM + POOL_HALO, w)
            dxin_ref[:, sl] = (_leading_sums(z, w)[:TM] - dpooled[:TM]).astype(BF16)

    row = pl.BlockSpec((TM, POOL_WIDTH), lambda i: (i, 0))
    return pl.pallas_call(
        body, name="pool_bwd", grid=(S // TM,),
        in_specs=[row, pl.BlockSpec((POOL_HALO, POOL_WIDTH), lambda i: (jnp.maximum(i * HB - 1, 0), 0)),
                  row, pl.BlockSpec((POOL_HALO, POOL_WIDTH), lambda i: (jnp.minimum((i + 1) * HB, last), 0)),
                  _const_spec(pool_w.shape), _const_spec((1, POOL_WIDTH))],
        out_specs=[row, pl.BlockSpec((G, POOL_GROUP, POOL_GROUP), lambda i: (0, 0, 0)),
                   pl.BlockSpec((1, POOL_WIDTH), lambda i: (0, 0))],
        out_shape=[jax.ShapeDtypeStruct((S, POOL_WIDTH), BF16), jax.ShapeDtypeStruct((G, POOL_GROUP, POOL_GROUP), F32),
                   jax.ShapeDtypeStruct((1, POOL_WIDTH), F32)],
        compiler_params=_params(("arbitrary",)),
    )(pool_in, pool_in, d_pool, d_pool, pool_w, pool_scale)


def _attn_bwd(q, k, v, d_attn, lse, delta, d, cargo=()):
    L = q.shape[0]
    nb = L // ATTN_BLOCK
    group = min(d, RESIDUES_PER_STEP)
    width = group * ATTN_WIDTH
    nc = len(cargo)
    kinds = [kind for kind, _ in cargo]

    def body(*refs):
        q_ref, kp_ref, kc_ref, vp_ref, vc_ref, do_ref, lse_ref, dl_ref = refs[:8]
        dq_ref, dk_ref, dv_ref = refs[8 + nc:11 + nc]
        ck_s, cv_s = refs[11 + 2 * nc:13 + 2 * nc]
        cargo_refs = (kinds, refs[8:8 + nc], refs[11 + nc:11 + 2 * nc], refs[13 + 2 * nc:])
        r, n = pl.program_id(0), pl.program_id(1)
        if nc:
            _cargo_start(*cargo_refs, (r == 0) & (n == 0))

        @pl.when(n == 0)
        def _():
            ck_s[...] = jnp.zeros_like(ck_s)
            cv_s[...] = jnp.zeros_like(cv_s)

        @pl.when(n < nb)
        def _():
            valid = _band_mask(n)
            valid2 = jnp.concatenate([valid, valid], axis=0)
            first = _first_head_lanes()

            def stacked_column(ref, lane):
                return jnp.concatenate([ref[:, lane:lane + 1], ref[:, lane + HEAD_DIM:lane + HEAD_DIM + 1]], axis=0)

            for hp in range(width // LANES):
                sl = slice(hp * LANES, (hp + 1) * LANES)
                qq = _stack_heads(q_ref[:, sl], first)
                dd = _stack_heads(do_ref[:, sl], first)
                kk = jnp.concatenate([kp_ref[:, sl], kc_ref[:, sl]], axis=0)
                vv = jnp.concatenate([vp_ref[:, sl], vc_ref[:, sl]], axis=0)
                s = _dot_nt(qq, kk)
                p = jnp.where(valid2, jnp.exp(s - stacked_column(lse_ref, hp * LANES)), 0.0)
                dp = _dot_nt(dd, vv)
                ds = (p * (dp - stacked_column(dl_ref, hp * LANES))).astype(BF16)
                dq_ref[:, sl] = (_unstack_heads(_dot(ds, kk), first) * ATTN_SCALE).astype(BF16)
                dk = _dot_tn(ds, qq)
                dv = _dot_tn(p.astype(BF16), dd)
                dk_ref[:, sl] = (ck_s[:, sl] + dk[:ATTN_BLOCK]).astype(BF16)
                dv_ref[:, sl] = (cv_s[:, sl] + dv[:ATTN_BLOCK]).astype(BF16)
                ck_s[:, sl] = dk[ATTN_BLOCK:]
                cv_s[:, sl] = dv[ATTN_BLOCK:]

        @pl.when(n == nb)
        def _():
            dk_ref[...] = ck_s[...].astype(BF16)
            dv_ref[...] = cv_s[...].astype(BF16)

        if nc:
            _cargo_finish(*cargo_refs, (r == d // group - 1) & (n == nb))

    blk = (ATTN_BLOCK, width)
    cur = pl.BlockSpec(blk, lambda r, n: (jnp.minimum(n, nb - 1), r))
    prev = pl.BlockSpec(blk, lambda r, n: (jnp.maximum(jnp.minimum(n, nb - 1) - 1, 0), r))
    done = pl.BlockSpec(blk, lambda r, n: (jnp.maximum(n - 1, 0), r))
    arrays, specs, shapes, aliases, sems = _cargo_call(cargo, 8, 3)
    out = pl.pallas_call(
        body, name=f"attn_bwd_d{d}", grid=(d // group, nb + 1),
        in_specs=[cur, prev, cur, prev, cur, cur, cur, cur] + specs, out_specs=[cur, done, done] + specs,
        out_shape=[jax.ShapeDtypeStruct((L, d * ATTN_WIDTH), BF16)] * 3 + shapes,
        input_output_aliases=aliases,
        scratch_shapes=[pltpu.VMEM(blk, F32), pltpu.VMEM(blk, F32)] + sems,
        compiler_params=_params(("arbitrary", "arbitrary")),
    )(q, k, k, v, v, d_attn, lse, delta, *arrays)
    return out[:3], out[3:]


def _mix_in_bwd(dqkv, d_pool_in, w_in, x, g_pre, dx1):
    S, D = x.shape
    TM = 512
    nd = len(DILATIONS)

    def body(*refs):
        g_refs = refs[:3 * nd]
        dpi_ref, w_ref, x_ref, g_ref, dx1_ref, dproj_ref, gx_ref, dg_ref = refs[3 * nd:3 * nd + 8]
        t_s = refs[3 * nd + 8:]

        @pl.when(pl.program_id(0) == 0)
        def _():
            dg_ref[...] = jnp.zeros_like(dg_ref)

        dh = jnp.zeros((TM, D), F32)
        for a in range(4):
            if a < 3:
                tot = g_refs[a][...].astype(F32)
                for i, d in enumerate(DILATIONS[1:]):
                    tot = tot + _from_residue(g_refs[3 * (i + 1) + a], t_s, d)
                db = tot.astype(BF16)
            else:
                db = dpi_ref[...]
            dproj_ref[:, a * ATTN_WIDTH:(a + 1) * ATTN_WIDTH] = db
            dh = dh + _dot_nt(db, w_ref[a])
        n, r = _rms_stats(x_ref[...])
        dx, dg = _rms_bwd(dh, n, r, g_ref[...])
        dg_ref[...] += dg
        gx_ref[...] = dx1_ref[...] + dx

    row = lambda w: pl.BlockSpec((TM, w), lambda i: (i, 0))
    return pl.pallas_call(
        body, name="mix_in_bwd", grid=(S // TM,),
        in_specs=[_residue_spec(TM, d) for d in DILATIONS for _ in range(3)]
        + [row(POOL_WIDTH), _const_spec(w_in.shape), row(D), _const_spec((1, D)), row(D)],
        out_specs=[row(4 * ATTN_WIDTH), row(D), pl.BlockSpec((1, D), lambda i: (0, 0))],
        out_shape=[jax.ShapeDtypeStruct((S, 4 * ATTN_WIDTH), BF16), jax.ShapeDtypeStruct((S, D), F32),
                   jax.ShapeDtypeStruct((1, D), F32)],
        scratch_shapes=_token_scratch(TM),
        compiler_params=_params(("arbitrary",), VMEM_LIMIT),
    )(*[g for gs in dqkv for g in gs], d_pool_in, w_in, x, g_pre, dx1)


SMALL_EARLY = ("pool_w", "pool_scale", "g_mix_post", "g_ffn_pre", "conv_b", "g_ffn_post", "conv_w")
SMALL_LATE = ("g_mix_pre",)


def _pack_small(grads, names):
    parts = []
    for n in names:
        g = grads[n]
        if n == "conv_w":
            g = g.reshape(CONV_WIDTH, N_SHARD, -1).transpose(1, 0, 2)
        parts.append(g.reshape(-1, LANES))
    return jnp.concatenate(parts, axis=0) if len(parts) > 1 else parts[0]


def _unpack_small(packed, names, like, shard):
    out, row = {}, 0
    for n in names:
        size = like[n].size * (N_SHARD if n == "conv_w" else 1)
        g = packed[row:row + size // LANES]
        row += size // LANES
        if n == "conv_w":
            g = lax.dynamic_slice_in_dim(g.reshape((N_SHARD,) + like[n].shape), shard, 1, axis=0)[0]
        out[n] = g.reshape(like[n].shape)
    return out


def _local_step(x, target, g_mix_pre, w_in, pool_w, pool_scale, w_out, g_mix_post, g_ffn_pre,
                w_up, conv_w, conv_b, w_down, g_ffn_post, mesh_pos=None):
    on_mesh = mesh_pos is not None
    D = x.shape[1]
    CW = w_up.shape[2]
    qkv, pool_in, h1, got = _mix_in_fwd(x, g_mix_pre, w_in, [("ici", w_up)] if on_mesh else ())
    w_up = got[0] if on_mesh else w_up
    o1, l1, got = _attn_fwd(*qkv[0], 1, [("d2d", w_up), ("ici", w_out), ("ici", w_down)] if on_mesh else ())
    w_up, w_out, w_down = got if on_mesh else (w_up, w_out, w_down)
    o4, l4, got = _attn_fwd(*qkv[1], 4, [("d2d", w_out), ("d2d", w_down)] if on_mesh else ())
    w_out, w_down = got if on_mesh else (w_out, w_down)
    o16, l16, _ = _attn_fwd(*qkv[2], 16)
    w_out = w_out.reshape(D, D)
    w_down = w_down.reshape(2 * CW, D)
    attn, lse = _attn_mix((o1, o4, o16), (l1, l4, l16))
    pool = _pool_fwd(pool_in, pool_w, pool_scale)
    mixed, x1, h2, cat = _mix_out_fwd(attn, pool, w_out, x, g_mix_post, g_ffn_pre)

    yv, dy, df, dc, loss, d_g_ffn_post, d_conv_b, d_conv_w = _ffn_fwd(
        h2, x1, target, w_up, w_down, conv_w, conv_b, g_ffn_post)
    du, dx1, d_g_ffn_pre = _ffn_bwd(dc, conv_w, w_up, x1, g_ffn_pre, dy)
    d_w_up = _matmul_tn(h2, du, N_SHARD, "grad_w_up")
    d_w_down = _matmul_tn(yv, df, 1, "grad_w_down")[0].reshape(N_SHARD, CW // 2, D)
    swap = [("swap", d_w_up), ("swap", d_w_down)] if on_mesh else ()
    d_mixed, d_pool, d_g_mix_post, d_attn, delta, from_sibling = _mix_out_bwd(dx1, mixed, g_mix_post, w_out, attn, swap)
    d_w_out = _matmul_tn(cat, d_mixed, 1, "grad_w_out")[0].reshape(N_SHARD, D // N_SHARD, D)
    d_pool_in, d_pool_w, d_pool_scale = _pool_bwd(pool_in, d_pool, pool_w, pool_scale)
    grads = dict(pool_w=d_pool_w, pool_scale=d_pool_scale, w_out=d_w_out, g_mix_post=d_g_mix_post,
                 g_ffn_pre=d_g_ffn_pre, w_up=d_w_up, conv_w=d_conv_w, conv_b=d_conv_b, w_down=d_w_down,
                 g_ffn_post=d_g_ffn_post)
    cargo = [(), (), ()]
    if on_mesh:
        c_arr, device = mesh_pos
        up_f32, up_bf16 = _pair_sum(d_w_up, from_sibling[0], c_arr, "pair_sum_w_up")
        down_f32, down_bf16 = _pair_sum(d_w_down, from_sibling[1], c_arr, "pair_sum_w_down")
        early = _pack_small(grads, SMALL_EARLY)
        early_slots = lax.dynamic_update_index_in_dim(jnp.zeros((8,) + early.shape, F32), early, device, 0)
        cargo = [[("scatter", down_bf16)], [("scatter", up_bf16)], [("everyone", early_slots)]]

    dqkv, landed = zip(*[_attn_bwd(*qkv[i], d_attn[i], lse[i], delta[i], d, cargo[i]) for i, d in enumerate(DILATIONS)])
    if on_mesh:
        grads.update(w_down=(down_f32, landed[0][0]), w_up=(up_f32, landed[1][0]), small_early=landed[2][0])
    d_proj, grad_x, grads["g_mix_pre"] = _mix_in_bwd(dqkv, d_pool_in, w_in, x, g_mix_pre, dx1)
    grads["w_in"] = _matmul_tn(h1, d_proj, N_SHARD, "grad_w_in")
    return loss, grad_x, grads


ANY = pl.BlockSpec(memory_space=pl.ANY)


def _position():
    x, y, c = lax.axis_index("x"), lax.axis_index("y"), lax.axis_index("c")
    chips = [(1 - x, y), (x, 1 - y), (1 - x, 1 - y)]
    return x, y, c, chips


def _remote(src, dst, send_sem, recv_sem, to):
    return pltpu.make_async_remote_copy(src_ref=src, dst_ref=dst, send_sem=send_sem, recv_sem=recv_sem,
                                        device_id=to, device_id_type=MESH)


def _cast_bf16(w, shard_arr, name):
    R, C = w.shape
    tr = R // 2

    def body(s_ref, w_ref, o_ref):
        o_ref[0] = w_ref[...].astype(BF16)

    return pl.pallas_call(
        body, name=name,
        grid_spec=pltpu.PrefetchScalarGridSpec(
            num_scalar_prefetch=1, grid=(2,),
            in_specs=[pl.BlockSpec((tr, C), lambda i, s_ref: (i, 0))],
            out_specs=pl.BlockSpec((1, tr, C), lambda i, s_ref: (s_ref[0], i, 0))),
        out_shape=jax.ShapeDtypeStruct((N_SHARD, R, C), BF16),
        compiler_params=_params(("parallel",)))(shard_arr, w)


def _gather_weights(bufs):
    n = len(bufs) - 1

    def body(*refs):
        outs, cw_out = refs[n + 1:2 * n + 1], refs[2 * n + 1]
        ici_send, ici_recv, d2d_send, d2d_recv = refs[2 * n + 2:]
        x, y, c, chips = _position()
        s = 2 * x + y
        sibling = (x, y, 1 - c)

        def half(a, shard, h):
            rows = outs[a].shape[1] // 2
            return outs[a].at[shard, pl.ds(h * rows, rows), :]

        sends = []
        for a in range(n):
            for j, (px, py) in enumerate(chips):
                sends.append(_remote(half(a, s, c), half(a, s, c),
                                     ici_send.at[3 * a + j], ici_recv.at[3 * a + j], (px, py, c)))
        for j, (px, py) in enumerate(chips):
            sends.append(_remote(cw_out.at[s], cw_out.at[s], ici_send.at[3 * n + j], ici_recv.at[3 * n + j], (px, py, c)))
        for cp in sends:
            cp.start()
        passed = []
        for a in range(n):
            for j, (px, py) in enumerate(chips):
                sj = 2 * px + py
                got = half(a, sj, c)
                _remote(got, got, ici_send.at[3 * a + j], ici_recv.at[3 * a + j], (px, py, c)).wait_recv()
                fwd = _remote(got, got, d2d_send.at[3 * a + j], d2d_recv.at[3 * a + j], sibling)
                fwd.start()
                passed.append(fwd)
        for j, (px, py) in enumerate(chips):
            got = cw_out.at[2 * px + py]
            _remote(got, got, ici_send.at[3 * n + j], ici_recv.at[3 * n + j], (px, py, c)).wait_recv()
        for a in range(n):
            for j, (px, py) in enumerate(chips):
                got = half(a, 2 * px + py, 1 - c)
                _remote(got, got, d2d_send.at[3 * a + j], d2d_recv.at[3 * a + j], sibling).wait_recv()
        for cp in sends + passed:
            cp.wait_send()

    return pl.pallas_call(
        body, name="gather_weights",
        in_specs=[ANY] * (n + 1), out_specs=[ANY] * (n + 1),
        out_shape=[jax.ShapeDtypeStruct(b.shape, b.dtype) for b in bufs],
        input_output_aliases={i: i for i in range(n + 1)},
        scratch_shapes=[pltpu.SemaphoreType.DMA((3 * n + 3,)), pltpu.SemaphoreType.DMA((3 * n + 3,)),
                        pltpu.SemaphoreType.DMA((3 * n,)), pltpu.SemaphoreType.DMA((3 * n,))],
        compiler_params=pltpu.CompilerParams(has_side_effects=True),
    )(*bufs)


def _swap_halves(grads, tag):
    n = len(grads)

    def body(*refs):
        ins, outs, send_sem, recv_sem = refs[:n], refs[n:2 * n], refs[2 * n], refs[2 * n + 1]
        x, y, c, _ = _position()
        copies = []
        for a in range(n):
            rows = ins[a].shape[1] // 2
            copies.append(_remote(ins[a].at[:, pl.ds((1 - c) * rows, rows), :], outs[a],
                                  send_sem.at[a], recv_sem.at[a], (x, y, 1 - c)))
        for cp in copies:
            cp.start()
        for cp in copies:
            cp.wait()

    return pl.pallas_call(
        body, name="swap_grad_halves_" + tag,
        in_specs=[ANY] * n, out_specs=[ANY] * n,
        out_shape=[jax.ShapeDtypeStruct((g.shape[0], g.shape[1] // 2, g.shape[2]), F32) for g in grads],
        scratch_shapes=[pltpu.SemaphoreType.DMA((n,)), pltpu.SemaphoreType.DMA((n,))],
        compiler_params=pltpu.CompilerParams(has_side_effects=True),
    )(*grads)


def _pair_sum(g, got, c_arr, name):
    n_sh, R, C = g.shape
    rows = R // 2

    def body(c_ref, g_ref, r_ref, f_ref, b_ref):
        t = g_ref[...] + r_ref[...]
        f_ref[...] = t
        b_ref[...] = t.astype(BF16)

    blk = pl.BlockSpec((1, rows, C), lambda i, c_ref: (i, 0, 0))
    return pl.pallas_call(
        body, name=name,
        grid_spec=pltpu.PrefetchScalarGridSpec(
            num_scalar_prefetch=1, grid=(n_sh,),
            in_specs=[pl.BlockSpec((1, rows, C), lambda i, c_ref: (i, c_ref[0], 0)), blk],
            out_specs=[blk, blk]),
        out_shape=[jax.ShapeDtypeStruct((n_sh, rows, C), F32), jax.ShapeDtypeStruct((n_sh, rows, C), BF16)],
        compiler_params=_params(("parallel",)),
    )(c_arr, g, got)


def _scatter_grads(sums_bf16, small_all):
    n = len(sums_bf16)

    def body(*refs):
        b_ins = refs[:n]
        recvs, sm = refs[n + 1:2 * n + 1], refs[2 * n + 1]
        ici_send, ici_recv, sm_send, sm_recv = refs[2 * n + 2:]
        x, y, c, chips = _position()
        me = 4 * x + 2 * y + c
        copies = []
        for a in range(n):
            for j, (px, py) in enumerate(chips):
                copies.append(_remote(b_ins[a].at[2 * px + py], recvs[a].at[j],
                                      ici_send.at[3 * a + j], ici_recv.at[3 * a + j], (px, py, c)))
        for k in range(1, 8):
            peer = (x ^ (k >> 2), y ^ ((k >> 1) & 1), c ^ (k & 1))
            copies.append(_remote(sm.at[me], sm.at[me], sm_send.at[k - 1], sm_recv.at[k - 1], peer))
        for cp in copies:
            cp.start()
        for cp in copies:
            cp.wait_send()
        for a in range(n):
            for j, (px, py) in enumerate(chips):
                _remote(recvs[a].at[j], recvs[a].at[j], ici_send.at[3 * a + j], ici_recv.at[3 * a + j],
                        (px, py, c)).wait_recv()
        for k in range(1, 8):
            peer = (x ^ (k >> 2), y ^ ((k >> 1) & 1), c ^ (k & 1))
            theirs = sm.at[4 * peer[0] + 2 * peer[1] + peer[2]]
            _remote(theirs, theirs, sm_send.at[k - 1], sm_recv.at[k - 1], peer).wait_recv()

    out = pl.pallas_call(
        body, name="scatter_grads",
        in_specs=[ANY] * (n + 1), out_specs=[ANY] * (n + 1),
        out_shape=[jax.ShapeDtypeStruct((3,) + b.shape[1:], BF16) for b in sums_bf16]
        + [jax.ShapeDtypeStruct(small_all.shape, F32)],
        input_output_aliases={n: n},
        scratch_shapes=[pltpu.SemaphoreType.DMA((3 * n,)), pltpu.SemaphoreType.DMA((3 * n,)),
                        pltpu.SemaphoreType.DMA((7,)), pltpu.SemaphoreType.DMA((7,))],
        compiler_params=pltpu.CompilerParams(has_side_effects=True),
    )(*sums_bf16, small_all)
    return out[:n], out[n]


def _shard_sum(sums_f32, recv, shard_arr, c_arr, name):
    _, rows, C = sums_f32.shape

    def body(s_ref, c_ref, o_ref, r_ref, t_ref):
        t_ref[...] = ((o_ref[0] + r_ref[0].astype(F32)) + r_ref[1].astype(F32)) + r_ref[2].astype(F32)

    return pl.pallas_call(
        body, name=name,
        grid_spec=pltpu.PrefetchScalarGridSpec(
            num_scalar_prefetch=2, grid=(1,),
            in_specs=[pl.BlockSpec((1, rows, C), lambda i, s_ref, c_ref: (s_ref[0], 0, 0)),
                      pl.BlockSpec((3, rows, C), lambda i, s_ref, c_ref: (0, 0, 0))],
            out_specs=pl.BlockSpec((rows, C), lambda i, s_ref, c_ref: (c_ref[0], 0))),
        out_shape=jax.ShapeDtypeStruct((2 * rows, C), F32),
        compiler_params=_params(("arbitrary",)),
    )(shard_arr, c_arr, sums_f32, recv)


def _join_halves(bufs):
    n = len(bufs)

    def body(*refs):
        outs, send_sem, recv_sem = refs[n:2 * n], refs[2 * n], refs[2 * n + 1]
        x, y, c, _ = _position()
        copies = []
        for a in range(n):
            rows = outs[a].shape[0] // 2
            mine = outs[a].at[pl.ds(c * rows, rows), :]
            copies.append(_remote(mine, mine, send_sem.at[a], recv_sem.at[a], (x, y, 1 - c)))
        for cp in copies:
            cp.start()
        for a, cp in enumerate(copies):
            cp.wait_send()
            rows = outs[a].shape[0] // 2
            theirs = outs[a].at[pl.ds((1 - c) * rows, rows), :]
            _remote(theirs, theirs, send_sem.at[a], recv_sem.at[a], (x, y, 1 - c)).wait_recv()

    return pl.pallas_call(
        body, name="join_grad_halves",
        in_specs=[ANY] * n, out_specs=[ANY] * n,
        out_shape=[jax.ShapeDtypeStruct(b.shape, F32) for b in bufs],
        input_output_aliases={i: i for i in range(n)},
        scratch_shapes=[pltpu.SemaphoreType.DMA((n,)), pltpu.SemaphoreType.DMA((n,))],
        compiler_params=pltpu.CompilerParams(has_side_effects=True),
    )(*bufs)


def _small_sum(parts, tag):
    _, R, C = parts.shape

    def body(p_ref, o_ref):
        t = p_ref[0]
        for k in range(1, 8):
            t = t + p_ref[k]
        o_ref[...] = t

    return pl.pallas_call(
        body, name="small_grad_sum_" + tag, grid=(1,),
        in_specs=[pl.BlockSpec((8, R, C), lambda i: (0, 0, 0))], out_specs=pl.BlockSpec((R, C), lambda i: (0, 0)),
        out_shape=jax.ShapeDtypeStruct((R, C), F32), compiler_params=_params(("arbitrary",)),
    )(parts)


def _adamw_math(w, g, m, v):
    m = ADAM_B1 * m + (1.0 - ADAM_B1) * g
    v = ADAM_B2 * v + (1.0 - ADAM_B2) * (g * g)
    m_hat = m / (1.0 - ADAM_B1 ** ADAM_STEP)
    v_hat = v / (1.0 - ADAM_B2 ** ADAM_STEP)
    delta = -ADAM_LR * (m_hat / (jnp.sqrt(v_hat) + ADAM_EPS) + ADAM_WD * w)
    return delta, m, v


def _adamw_big(w, g, m, v, name):
    R, C = w.shape
    tr = R // 4

    def body(w_ref, g_ref, m_ref, v_ref, d_ref, nm_ref, nv_ref):
        d_ref[...], nm_ref[...], nv_ref[...] = _adamw_math(w_ref[...], g_ref[...], m_ref[...], v_ref[...])

    blk = pl.BlockSpec((tr, C), lambda i: (i, 0))
    return pl.pallas_call(
        body, name=name, grid=(4,), in_specs=[blk] * 4, out_specs=[blk] * 3,
        out_shape=[jax.ShapeDtypeStruct((R, C), F32)] * 3, compiler_params=_params(("parallel",)),
    )(w, g, m, v)


def _adamw_small(ws, gs, ms, vs):
    n = len(ws)

    def body(*refs):
        for a in range(n):
            w, g, m, v = (refs[k * n + a][...] for k in range(4))
            d, nm, nv = _adamw_math(w, g, m, v)
            refs[4 * n + a][...] = d
            refs[5 * n + a][...] = nm
            refs[6 * n + a][...] = nv

    shapes = [jax.ShapeDtypeStruct(w.shape, F32) for w in ws]
    out = pl.pallas_call(body, name="adamw_small", out_shape=shapes * 3)(*ws, *gs, *ms, *vs)
    return out[:n], out[n:2 * n], out[2 * n:]


BIG = ("w_in", "w_out", "w_up", "w_down")
SMALL = ("g_mix_pre", "pool_w", "pool_scale", "g_mix_post", "g_ffn_pre", "conv_b", "g_ffn_post", "conv_w")
ORDER = ("g_mix_pre", "w_in", "pool_w", "pool_scale", "w_out", "g_mix_post", "g_ffn_pre", "w_up", "conv_w", "conv_b",
         "w_down", "g_ffn_post")


def kernel(x, g_mix_pre, w_in, pool_w, pool_scale, w_out, g_mix_post, g_ffn_pre, w_up, conv_w, conv_b, w_down, g_ffn_post, loss_target, m_g_mix_pre, m_w_in, m_pool_w, m_pool_scale, m_w_out, m_g_mix_post, m_g_ffn_pre, m_w_up, m_conv_w, m_conv_b, m_w_down, m_g_ffn_post, v_g_mix_pre, v_w_in, v_pool_w, v_pool_scale, v_w_out, v_g_mix_post, v_g_ffn_pre, v_w_up, v_conv_w, v_conv_b, v_w_down, v_g_ffn_post):
    args = dict(locals())
    W = {n: args[n][0] for n in ORDER}
    M = {n: args["m_" + n][0] for n in ORDER}
    V = {n: args["v_" + n][0] for n in ORDER}
    for d in (W, M, V):
        d["pool_w"] = d["pool_w"].reshape(-1, POOL_GROUP)
        for n in ("g_mix_pre", "pool_scale", "g_mix_post", "g_ffn_pre", "conv_b", "g_ffn_post"):
            d[n] = d[n].reshape(1, -1)
    CW = W["w_up"].shape[1]
    c_arr = lax.axis_index("c").astype(jnp.int32).reshape(1)
    shard = 2 * lax.axis_index("x") + lax.axis_index("y")
    shard_arr = shard.astype(jnp.int32).reshape(1)
    device = 2 * shard + lax.axis_index("c")

    conv_w_slots = lax.dynamic_update_index_in_dim(jnp.zeros((N_SHARD,) + W["conv_w"].shape, F32), W["conv_w"], shard, 0)
    slots = {n: _cast_bf16(W[n], shard_arr, "cast_" + n) for n in BIG}
    w_in_g, conv_w_g = _gather_weights([slots["w_in"], conv_w_slots])
    conv_w_full = conv_w_g.transpose(1, 0, 2).reshape(CONV_WIDTH, 1, N_SHARD * CW)

    loss, grad_x, G = _local_step(
        x[0], loss_target[0], W["g_mix_pre"], w_in_g, W["pool_w"].reshape(-1, POOL_GROUP, POOL_GROUP), W["pool_scale"],
        slots["w_out"], W["g_mix_post"], W["g_ffn_pre"], slots["w_up"], conv_w_full, W["conv_b"],
        slots["w_down"], W["g_ffn_post"], (c_arr, device))

    late = ("w_in", "w_out")
    from_sibling = _swap_halves([G[n] for n in late], "mix")
    sums = {n: _pair_sum(G[n], r, c_arr, "pair_sum_" + n) for n, r in zip(late, from_sibling)}
    small = _pack_small(G, SMALL_LATE)
    small_slots = lax.dynamic_update_index_in_dim(jnp.zeros((8,) + small.shape, F32), small, device, 0)
    recvs, small_all = _scatter_grads([sums[n][1] for n in late], small_slots)
    reduced = {n: (sums[n][0], r) for n, r in zip(late, recvs)}
    reduced.update({n: G[n] for n in ("w_up", "w_down")})
    halves = [_shard_sum(*reduced[n], shard_arr, c_arr, "shard_sum_" + n) for n in BIG]
    full = dict(zip(BIG, _join_halves(halves)))
    full.update(_unpack_small(_small_sum(G["small_early"], "early"), SMALL_EARLY, W, shard))
    full.update(_unpack_small(_small_sum(small_all, "late"), SMALL_LATE, W, shard))

    delta, new_m, new_v = {}, {}, {}
    for n in BIG:
        delta[n], new_m[n], new_v[n] = _adamw_big(W[n], full[n], M[n], V[n], "adamw_" + n)
    ds, nms, nvs = _adamw_small([W[n] for n in SMALL], [full[n] for n in SMALL], [M[n] for n in SMALL],
                                [V[n] for n in SMALL])
    for n, d, nm, nv in zip(SMALL, ds, nms, nvs):
        delta[n], new_m[n], new_v[n] = d, nm, nv

    loss = lax.psum(loss[0, 0], ("x", "y", "c"))
    shaped = lambda d: [d[n].reshape(args[n].shape) for n in ORDER]
    return (loss, grad_x[None], *shaped(full), *shaped(delta), *shaped(new_m), *shaped(new_v))
```

```python
import functools

import jax
import jax.numpy as jnp
from jax import lax
from jax.experimental import pallas as pl
from jax.experimental.pallas import tpu as pltpu

F32 = jnp.float32
BF16 = jnp.bfloat16

RMS_EPS = 1e-6
NEG_INF = -1e30
N_HEADS = 8
HEAD_DIM = 64
ATTN_WIDTH = N_HEADS * HEAD_DIM
ATTN_SCALE = HEAD_DIM ** -0.5
ATTN_BLOCK = 128
DILATIONS = (1, 4, 16)
RESIDUES_PER_STEP = 4
CONSECUTIVE_BLOCKS = 4
POOL_WINDOWS = (2, 4, 8, 16)
POOL_GROUP = 128
POOL_WIDTH = POOL_GROUP * len(POOL_WINDOWS)
POOL_HALO = 16
CONV_WIDTH = 3
CONV_HALO = 8
N_SHARD = 4
LANES = 128

ADAM_LR = 0.001
ADAM_B1 = 0.9
ADAM_B2 = 0.999
ADAM_EPS = 1e-08
ADAM_WD = 0.01
ADAM_STEP = 10

VMEM_LIMIT = 60 * 1024 * 1024
MESH = pl.DeviceIdType.MESH
NT = (((1,), (1,)), ((), ()))
TN = (((0,), (0,)), ((), ()))


def _params(sem, vmem=None):
    return pltpu.CompilerParams(dimension_semantics=sem, vmem_limit_bytes=vmem)


def _const_spec(shape):
    zeros = (0,) * len(shape)
    return pl.BlockSpec(shape, lambda *_: zeros, pipeline_mode=pl.Buffered(1))


def _dot(a, b):
    return jnp.dot(a, b, preferred_element_type=F32)


def _dot_nt(a, b):
    return lax.dot_general(a, b, NT, preferred_element_type=F32)


def _dot_tn(a, b):
    return lax.dot_general(a, b, TN, preferred_element_type=F32)


def _rms_stats(x):
    r = lax.rsqrt(jnp.mean(x * x, axis=-1, keepdims=True) + RMS_EPS)
    return x * r, r


def _rms_bwd(dy, n, r, g):
    dg = jnp.sum(dy * n, axis=0, keepdims=True)
    dn = dy * g
    dx = r * (dn - n * jnp.mean(dn * n, axis=-1, keepdims=True))
    return dx, dg


def _gelu_tanh(g):
    k = 0.7978845608028654
    kc = k * 0.044715
    g2 = g * g
    t = jnp.tanh(g * (k + kc * g2))
    h = 0.5 * t + 0.5
    dh = (0.5 - 0.5 * (t * t)) * (k + (3.0 * kc) * g2)
    return g * h, h + g * dh


def _residue_shape(S, d, dtype):
    return jax.ShapeDtypeStruct((S // d, d * ATTN_WIDTH), dtype)


def _residue_spec(TM, d):
    return pl.BlockSpec((TM // d, d * ATTN_WIDTH), lambda i: (i, 0))


def _token_scratch(TM):
    return [pltpu.VMEM((TM, LANES), F32)] * (ATTN_WIDTH // LANES)


def _put_tokens(dst_s, val):
    for cb, chunk in enumerate(dst_s):
        chunk[...] = val[:, cb * LANES:(cb + 1) * LANES]


def _get_tokens(src_s):
    return jnp.concatenate([chunk[...] for chunk in src_s], axis=1)


def _to_residue(val, src_s, out_ref, d, dtype):
    if d == 1:
        out_ref[...] = val.astype(dtype)
        return
    rows = src_s[0].shape[0]
    for r in range(d):
        for cb, chunk in enumerate(src_s):
            col = r * ATTN_WIDTH + cb * LANES
            out_ref[:, col:col + LANES] = chunk[pl.ds(r, rows // d, stride=d), :].astype(dtype)


def _from_residue(in_ref, dst_s, d):
    if d == 1:
        return in_ref[...].astype(F32)
    rows = dst_s[0].shape[0]
    for r in range(d):
        for cb, chunk in enumerate(dst_s):
            col = r * ATTN_WIDTH + cb * LANES
            chunk[pl.ds(r, rows // d, stride=d), :] = in_ref[:, col:col + LANES].astype(F32)
    return _get_tokens(dst_s)


def _mix_in_fwd(x, g_pre, w_in, cargo=()):
    S, D = x.shape
    TM = 512
    nc = len(cargo)
    kinds = [kind for kind, _ in cargo]
    n_chunks = ATTN_WIDTH // LANES

    def body(x_ref, g_ref, w_ref, *refs):
        cargo_in, refs = refs[:nc], refs[nc:]
        qkv_refs, p_ref, h_ref = refs[:9], refs[9], refs[10]
        t_s = refs[11 + nc:11 + nc + n_chunks]
        cargo_refs = (kinds, cargo_in, refs[11:11 + nc], refs[11 + nc + n_chunks:])
        if nc:
            _cargo_start(*cargo_refs, pl.program_id(0) == 0)
        n, _ = _rms_stats(x_ref[...])
        hb = (n * g_ref[...]).astype(BF16)
        h_ref[...] = hb
        for a in range(3):
            res = _dot(hb, w_ref[a])
            if a == 0:
                res = res * ATTN_SCALE
            _put_tokens(t_s, res)
            for i, d in enumerate(DILATIONS):
                _to_residue(res, t_s, qkv_refs[3 * i + a], d, BF16)
        p_ref[...] = _dot(hb, w_ref[3])
        if nc:
            _cargo_finish(*cargo_refs, pl.program_id(0) == S // TM - 1)

    row = lambda w: pl.BlockSpec((TM, w), lambda i: (i, 0))
    arrays, cargo_specs, shapes, aliases, sems = _cargo_call(cargo, 3, 11)
    out = pl.pallas_call(
        body, name="mix_in_fwd", grid=(S // TM,),
        in_specs=[row(D), _const_spec((1, D)), _const_spec(w_in.shape)] + cargo_specs,
        out_specs=[_residue_spec(TM, d) for d in DILATIONS for _ in range(3)] + [row(POOL_WIDTH), row(D)] + cargo_specs,
        out_shape=[_residue_shape(S, d, BF16) for d in DILATIONS for _ in range(3)]
        + [jax.ShapeDtypeStruct((S, POOL_WIDTH), F32), jax.ShapeDtypeStruct((S, D), BF16)] + shapes,
        input_output_aliases=aliases,
        scratch_shapes=_token_scratch(TM) + sems,
        compiler_params=_params(("arbitrary",), VMEM_LIMIT),
    )(x, g_pre, w_in, *arrays)
    return [out[0:3], out[3:6], out[6:9]], out[9], out[10], out[11:]


def _band_mask(n):
    qi = lax.broadcasted_iota(jnp.int32, (ATTN_BLOCK, 2 * ATTN_BLOCK), 0)
    ki = lax.broadcasted_iota(jnp.int32, (ATTN_BLOCK, 2 * ATTN_BLOCK), 1)
    dist = qi + ATTN_BLOCK - ki
    return (dist >= 0) & (dist <= ATTN_BLOCK) & ((ki >= ATTN_BLOCK) | (n > 0))


def _first_head_lanes():
    return lax.broadcasted_iota(jnp.int32, (1, LANES), 1) < HEAD_DIM


def _stack_heads(pair, first):
    zero = jnp.zeros_like(pair)
    return jnp.concatenate([jnp.where(first, pair, zero), jnp.where(first, zero, pair)], axis=0)


def _unstack_heads(stacked, first):
    return jnp.where(first, stacked[:ATTN_BLOCK], stacked[ATTN_BLOCK:])


CARGO_COPIES = {"ici": 3, "d2d": 3, "scatter": 3, "swap": 1, "everyone": 7}
CARGO_IN_PLACE = ("ici", "d2d", "everyone")


def _cargo_copies(kinds, ins, outs, send_sems, recv_sems, want_recvs=True):
    x, y, c, chips = _position()
    s = 2 * x + y
    me = 2 * s + c
    sibling = (x, y, 1 - c)
    sends, recvs = [], []

    def add(k, src, dst, landing, to):
        sends.append(_remote(src, dst, send_sems.at[k], recv_sems.at[k], to))
        if want_recvs:
            recvs.append(_remote(landing, landing, send_sems.at[k], recv_sems.at[k], to))

    k0 = 0
    for a, kind in enumerate(kinds):
        if kind == "swap":
            rows = ins[a].shape[1] // 2
            add(k0, ins[a].at[:, pl.ds((1 - c) * rows, rows), :], outs[a], outs[a], sibling)
        elif kind == "everyone":
            for m in range(1, 8):
                peer = (x ^ (m >> 2), y ^ ((m >> 1) & 1), c ^ (m & 1))
                add(k0 + m - 1, outs[a].at[me], outs[a].at[me], outs[a].at[4 * peer[0] + 2 * peer[1] + peer[2]], peer)
        else:
            for j, (px, py) in enumerate(chips):
                sj = 2 * px + py
                if kind == "scatter":
                    add(k0 + j, ins[a].at[sj], outs[a].at[j], outs[a].at[j], (px, py, c))
                    continue
                buf = outs[a]
                rows = buf.shape[1] // 2
                half = lambda shard, h: buf.at[shard, pl.ds(h * rows, rows), :]
                if kind == "ici":
                    add(k0 + j, half(s, c), half(s, c), half(sj, c), (px, py, c))
                else:
                    add(k0 + j, half(sj, c), half(sj, c), half(sj, 1 - c), sibling)
        k0 += CARGO_COPIES[kind]
    return sends, recvs


def _cargo_start(kinds, ins, outs, sems, first_step):
    @pl.when(first_step)
    def _():
        for cp in _cargo_copies(kinds, ins, outs, *sems, want_recvs=False)[0]:
            cp.start()


def _cargo_finish(kinds, ins, outs, sems, last_step):
    @pl.when(last_step)
    def _():
        sends, recvs = _cargo_copies(kinds, ins, outs, *sems)
        for cp in sends:
            cp.wait_send()
        for cp in recvs:
            cp.wait_recv()


def _cargo_call(cargo, n_in, n_out):
    arrays = [a for _, a in cargo]
    shapes = []
    for kind, a in cargo:
        shape = {"scatter": (3,) + a.shape[1:], "swap": (a.shape[0], a.shape[1] // 2, a.shape[2])}.get(kind, a.shape)
        shapes.append(jax.ShapeDtypeStruct(shape, a.dtype))
    aliases = {n_in + i: n_out + i for i, (kind, _) in enumerate(cargo) if kind in CARGO_IN_PLACE}
    n_sems = sum(CARGO_COPIES[kind] for kind, _ in cargo)
    sems = [pltpu.SemaphoreType.DMA((n_sems,))] * 2 if cargo else []
    return arrays, [ANY] * len(cargo), shapes, aliases, sems


def _attn_fwd(q, k, v, d, cargo=()):
    L = q.shape[0]
    group = min(d, RESIDUES_PER_STEP)
    width = group * ATTN_WIDTH
    qb = RESIDUES_PER_STEP // group
    steps = L // (qb * ATTN_BLOCK)
    nc = len(cargo)
    kinds = [kind for kind, _ in cargo]

    def body(*refs):
        q_ref, kp_ref, kc_ref, vp_ref, vc_ref = refs[:5]
        o_ref, lse_ref = refs[5 + nc:7 + nc]
        cargo_refs = (kinds, refs[5:5 + nc], refs[7 + nc:7 + 2 * nc], refs[7 + 2 * nc:])
        r, n = pl.program_id(0), pl.program_id(1)
        if nc:
            _cargo_start(*cargo_refs, (r == 0) & (n == 0))
        first = _first_head_lanes()
        for sub in range(qb):
            rows = slice(sub * ATTN_BLOCK, (sub + 1) * ATTN_BLOCK)
            valid = _band_mask(n if sub == 0 else 1)
            valid2 = jnp.concatenate([valid, valid], axis=0)
            for hp in range(width // LANES):
                sl = slice(hp * LANES, (hp + 1) * LANES)
                if sub == 0:
                    kk = jnp.concatenate([kp_ref[:, sl], kc_ref[rows, sl]], axis=0)
                    vv = jnp.concatenate([vp_ref[:, sl], vc_ref[rows, sl]], axis=0)
                else:
                    keys = slice((sub - 1) * ATTN_BLOCK, (sub + 1) * ATTN_BLOCK)
                    kk, vv = kc_ref[keys, sl], vc_ref[keys, sl]
                s = jnp.where(valid2, _dot_nt(_stack_heads(q_ref[rows, sl], first), kk), NEG_INF)
                m = jnp.max(s, axis=-1, keepdims=True)
                p = jnp.exp(s - m)
                den = jnp.sum(p, axis=-1, keepdims=True)
                o_ref[rows, sl] = _unstack_heads(_dot(p.astype(BF16), vv) / den, first)
                lse_ref[rows, sl] = _unstack_heads(m + jnp.log(den), first)
        if nc:
            _cargo_finish(*cargo_refs, (r == d // group - 1) & (n == steps - 1))

    cur = pl.BlockSpec((qb * ATTN_BLOCK, width), lambda r, n: (n, r))
    prev = pl.BlockSpec((ATTN_BLOCK, width), lambda r, n: (jnp.maximum(n * qb - 1, 0), r))
    arrays, specs, shapes, aliases, sems = _cargo_call(cargo, 5, 2)
    out = pl.pallas_call(
        body, name=f"attn_fwd_d{d}", grid=(d // group, steps),
        in_specs=[cur, prev, cur, prev, cur] + specs,
        out_specs=[cur, cur] + specs,
        out_shape=[jax.ShapeDtypeStruct((L, d * ATTN_WIDTH), F32)] * 2 + shapes,
        input_output_aliases=aliases, scratch_shapes=sems,
        compiler_params=_params(("arbitrary", "arbitrary")),
    )(q, k, k, v, v, *arrays)
    return out[0], out[1], out[2:]


def _attn_mix(outs, lses):
    S = outs[0].shape[0]
    TM = 512
    n = len(DILATIONS)

    def body(*refs):
        o_refs, l_refs, attn_ref, lse_refs, t_s = refs[:n], refs[n:2 * n], refs[2 * n], refs[2 * n + 1:3 * n + 1], refs[3 * n + 1:]
        os = [_from_residue(o_refs[i], t_s, d) for i, d in enumerate(DILATIONS)]
        ls = [_from_residue(l_refs[i], t_s, d) for i, d in enumerate(DILATIONS)]
        m = jnp.maximum(jnp.maximum(ls[0], ls[1]), ls[2])
        es = [jnp.exp(l - m) for l in ls]
        den = es[0] + es[1] + es[2]
        attn_ref[...] = (es[0] * os[0] + es[1] * os[1] + es[2] * os[2]) / den
        lse = m + jnp.log(den)
        _put_tokens(t_s, lse)
        for i, d in enumerate(DILATIONS):
            _to_residue(lse, t_s, lse_refs[i], d, F32)

    specs = [_residue_spec(TM, d) for d in DILATIONS]
    out = pl.pallas_call(
        body, name="attn_mix", grid=(S // TM,),
        in_specs=specs * 2, out_specs=[specs[0]] + specs,
        out_shape=[jax.ShapeDtypeStruct((S, ATTN_WIDTH), F32)] + [_residue_shape(S, d, F32) for d in DILATIONS],
        scratch_shapes=_token_scratch(TM),
        compiler_params=_params(("parallel",)),
    )(*outs, *lses)
    return out[0], out[1:]


def _pool_counts(first_row, rows, w):
    t = first_row + lax.broadcasted_iota(jnp.int32, (rows, 1), 0)
    return jnp.minimum(t + 1, w).astype(F32)


def _trailing_sums(xe, w):
    s, k = xe, 1
    while k < w:
        s = s + pltpu.roll(s, k, 0)
        k *= 2
    return s


def _leading_sums(xe, w):
    rows = xe.shape[0]
    s, k = xe, 1
    while k < w:
        s = s + pltpu.roll(s, rows - k, 0)
        k *= 2
    return s


def _pooled_groups(halo, cur, first_row):
    TM = cur.shape[0]
    xe = jnp.concatenate([halo, cur], axis=0)
    out = []
    for g, w in enumerate(POOL_WINDOWS):
        a = xe[:, g * POOL_GROUP:(g + 1) * POOL_GROUP]
        s = _trailing_sums(a, w)[POOL_HALO:]
        out.append(s / _pool_counts(first_row, TM, w) - a[POOL_HALO:])
    return out


def _pool_fwd(pool_in, pool_w, pool_scale):
    S = pool_in.shape[0]
    TM = 512
    HB = TM // POOL_HALO

    def body(cur_ref, halo_ref, w_ref, sc_ref, y_ref):
        i = pl.program_id(0)
        halo = jnp.where(i > 0, halo_ref[...], 0.0)
        pooled = _pooled_groups(halo, cur_ref[...], i * TM)
        for g in range(len(POOL_WINDOWS)):
            sl = slice(g * POOL_GROUP, (g + 1) * POOL_GROUP)
            y = _dot(pooled[g].astype(BF16), w_ref[g].astype(BF16)) * sc_ref[:, sl]
            y_ref[:, sl] = y.astype(BF16)

    return pl.pallas_call(
        body, name="pool_fwd", grid=(S // TM,),
        in_specs=[pl.BlockSpec((TM, POOL_WIDTH), lambda i: (i, 0)),
                  pl.BlockSpec((POOL_HALO, POOL_WIDTH), lambda i: (jnp.maximum(i * HB - 1, 0), 0)),
                  _const_spec(pool_w.shape), _const_spec((1, POOL_WIDTH))],
        out_specs=pl.BlockSpec((TM, POOL_WIDTH), lambda i: (i, 0)),
        out_shape=jax.ShapeDtypeStruct((S, POOL_WIDTH), BF16),
        compiler_params=_params(("parallel",)),
    )(pool_in, pool_in, pool_w, pool_scale)


def _mix_out_fwd(attn, pool, w_out, x, g_post, g_ffn_pre):
    S, D = x.shape
    TM = 512

    def body(a_ref, p_ref, w_ref, x_ref, gp_ref, gf_ref, mixed_ref, x1_ref, h2_ref, cat_ref):
        ab = a_ref[...].astype(BF16)
        cat_ref[:, :ATTN_WIDTH] = ab
        cat_ref[:, ATTN_WIDTH:] = p_ref[...]
        mixed = _dot(ab, w_ref[:ATTN_WIDTH, :]) + _dot(p_ref[...], w_ref[ATTN_WIDTH:, :])
        mixed_ref[...] = mixed
        n, _ = _rms_stats(mixed)
        x1 = x_ref[...] + n * gp_ref[...]
        x1_ref[...] = x1
        n2, _ = _rms_stats(x1)
        h2_ref[...] = (n2 * gf_ref[...]).astype(BF16)

    row = lambda w: pl.BlockSpec((TM, w), lambda i: (i, 0))
    return pl.pallas_call(
        body, name="mix_out_fwd", grid=(S // TM,),
        in_specs=[row(ATTN_WIDTH), row(POOL_WIDTH), _const_spec(w_out.shape), row(D),
                  _const_spec((1, D)), _const_spec((1, D))],
        out_specs=[row(D), row(D), row(D), row(D)],
        out_shape=[jax.ShapeDtypeStruct((S, D), F32), jax.ShapeDtypeStruct((S, D), F32),
                   jax.ShapeDtypeStruct((S, D), BF16), jax.ShapeDtypeStruct((S, D), BF16)],
        compiler_params=_params(("parallel",), VMEM_LIMIT),
    )(attn, pool, w_out, x, g_post, g_ffn_pre)


def _ffn_fwd(h2, x1, target, w_up, w_down, conv_w, conv_b, g_post):
    S, D = x1.shape
    CW = w_up.shape[2]
    FF = 2 * CW
    TM = 256
    piece = 4 * LANES
    pieces = [(lo, min(lo + piece, CW)) for lo in range(0, CW, piece)]

    def body(h2_ref, x1_ref, t_ref, wu_ref, wd_ref, cw_ref, cb_ref, g_ref,
             yv_ref, dy_ref, df_ref, dc_ref, loss_ref, dg_ref, dcb_ref, dcw_ref,
             ue_s, dgate_s, dval_s):
        i = pl.program_id(0)

        @pl.when(i == 0)
        def _():
            loss_ref[...] = jnp.zeros_like(loss_ref)
            dg_ref[...] = jnp.zeros_like(dg_ref)
            dcb_ref[...] = jnp.zeros_like(dcb_ref)
            dcw_ref[...] = jnp.zeros_like(dcw_ref)
            ue_s[0:CONV_HALO, :] = jnp.zeros((CONV_HALO, 2 * FF), F32)

        @pl.when(i > 0)
        def _():
            ue_s[0:CONV_HALO, :] = ue_s[TM:TM + CONV_HALO, :]

        def shifted(cols, k):
            return pltpu.roll(ue_s[:, cols], k, 0)[CONV_HALO:]

        def conv(cols):
            return (cb_ref[:, cols] + cw_ref[2, :, cols] * ue_s[CONV_HALO:, cols]
                    + cw_ref[1, :, cols] * shifted(cols, 1) + cw_ref[0, :, cols] * shifted(cols, 2))

        hb = h2_ref[...]
        f = jnp.zeros((TM, D), F32)
        for j in range(2):
            jc = slice(j * CW, (j + 1) * CW)
            for half in range(2):
                blk = 2 * half + j
                cols = slice(blk * CW, (blk + 1) * CW)
                ue_s[CONV_HALO:, cols] = _dot(hb, wu_ref[blk])
            for lo, hi in pieces:
                pc = slice(j * CW + lo, j * CW + hi)
                gelu, dgelu = _gelu_tanh(conv(pc))
                val = conv(slice(FF + j * CW + lo, FF + j * CW + hi))
                dgate_s[:, pc] = val * dgelu
                dval_s[:, pc] = gelu
                yv_ref[:, pc] = (gelu * val).astype(BF16)
            f = f + _dot(yv_ref[:, jc], wd_ref[jc, :])

        n, r = _rms_stats(f)
        err = x1_ref[...] + n * g_ref[...] - t_ref[...]
        loss_ref[...] += 0.5 * jnp.sum(jnp.mean(err * err, axis=-1, keepdims=True), axis=0, keepdims=True)
        dy = err / D
        dy_ref[...] = dy
        df, dg = _rms_bwd(dy, n, r, g_ref[...])
        dg_ref[...] += dg
        dfb = df.astype(BF16)
        df_ref[...] = dfb

        for j in range(2):
            jc = slice(j * CW, (j + 1) * CW)
            dyv = _dot_nt(dfb, wd_ref[jc, :])
            for lo, hi in pieces:
                pc = slice(j * CW + lo, j * CW + hi)
                for half, scale_s in ((0, dgate_s), (1, dval_s)):
                    cols = slice(half * FF + j * CW + lo, half * FF + j * CW + hi)
                    dcv = dyv[:, lo:hi] * scale_s[:, pc]
                    dc_ref[:, cols] = dcv.astype(BF16)
                    dcb_ref[:, cols] += jnp.sum(dcv, axis=0, keepdims=True)
                    dcw_ref[2, :, cols] += jnp.sum(dcv * ue_s[CONV_HALO:, cols], axis=0, keepdims=True)
                    dcw_ref[1, :, cols] += jnp.sum(dcv * shifted(cols, 1), axis=0, keepdims=True)
                    dcw_ref[0, :, cols] += jnp.sum(dcv * shifted(cols, 2), axis=0, keepdims=True)

    row = lambda w: pl.BlockSpec((TM, w), lambda i: (i, 0))
    acc = lambda shape: pl.BlockSpec(shape, lambda i: (0,) * len(shape))
    return pl.pallas_call(
        body, name="ffn_fwd", grid=(S // TM,),
        in_specs=[row(D), row(D), row(D), _const_spec(w_up.shape), _const_spec(w_down.shape),
                  _const_spec(conv_w.shape), _const_spec((1, 2 * FF)), _const_spec((1, D))],
        out_specs=[row(FF), row(D), row(D), row(2 * FF),
                   acc((1, 1)), acc((1, D)), acc((1, 2 * FF)), acc((CONV_WIDTH, 1, 2 * FF))],
        out_shape=[jax.ShapeDtypeStruct((S, FF), BF16),
                   jax.ShapeDtypeStruct((S, D), F32), jax.ShapeDtypeStruct((S, D), BF16),
                   jax.ShapeDtypeStruct((S, 2 * FF), BF16),
                   jax.ShapeDtypeStruct((1, 1), F32), jax.ShapeDtypeStruct((1, D), F32),
                   jax.ShapeDtypeStruct((1, 2 * FF), F32), jax.ShapeDtypeStruct((CONV_WIDTH, 1, 2 * FF), F32)],
        scratch_shapes=[pltpu.VMEM((TM + CONV_HALO, 2 * FF), F32), pltpu.VMEM((TM, FF), F32),
                        pltpu.VMEM((TM, FF), F32)],
        compiler_params=_params(("arbitrary",), VMEM_LIMIT),
    )(h2, x1, target, w_up, w_down, conv_w, conv_b, g_post)


def _ffn_bwd(dc, conv_w, w_up, x1, g_ffn_pre, dy):
    S, D = x1.shape
    CW = w_up.shape[2]
    F2 = 4 * CW
    TM = 256
    HB = TM // CONV_HALO
    last = S // CONV_HALO - 1
    n_tiles = S // TM

    def body(dc_ref, halo_ref, cw_ref, wu_ref, x1_ref, g_ref, dy_ref, du_ref, dx1_ref, dg_ref):
        i = pl.program_id(0)

        @pl.when(i == 0)
        def _():
            dg_ref[...] = jnp.zeros_like(dg_ref)

        keep = i < n_tiles - 1
        dh2 = jnp.zeros((TM, D), F32)
        for blk in range(N_SHARD):
            cols = slice(blk * CW, (blk + 1) * CW)
            halo = jnp.where(keep, halo_ref[:, cols].astype(F32), 0.0)
            dce = jnp.concatenate([dc_ref[:, cols].astype(F32), halo], axis=0)
            rows = TM + CONV_HALO
            du = (cw_ref[2, :, cols] * dce[:TM]
                  + cw_ref[1, :, cols] * pltpu.roll(dce, rows - 1, 0)[:TM]
                  + cw_ref[0, :, cols] * pltpu.roll(dce, rows - 2, 0)[:TM])
            dub = du.astype(BF16)
            du_ref[:, cols] = dub
            dh2 = dh2 + _dot_nt(dub, wu_ref[blk])
        n2, r2 = _rms_stats(x1_ref[...])
        dx, dg = _rms_bwd(dh2, n2, r2, g_ref[...])
        dg_ref[...] += dg
        dx1_ref[...] = dy_ref[...] + dx

    row = lambda w: pl.BlockSpec((TM, w), lambda i: (i, 0))
    return pl.pallas_call(
        body, name="ffn_bwd", grid=(S // TM,),
        in_specs=[row(F2), pl.BlockSpec((CONV_HALO, F2), lambda i: (jnp.minimum((i + 1) * HB, last), 0)),
                  _const_spec(conv_w.shape), _const_spec(w_up.shape), row(D), _const_spec((1, D)), row(D)],
        out_specs=[row(F2), row(D), pl.BlockSpec((1, D), lambda i: (0, 0))],
        out_shape=[jax.ShapeDtypeStruct((S, F2), BF16), jax.ShapeDtypeStruct((S, D), F32),
                   jax.ShapeDtypeStruct((1, D), F32)],
        compiler_params=_params(("arbitrary",), VMEM_LIMIT),
    )(dc, dc, conv_w, w_up, x1, g_ffn_pre, dy)


def _matmul_tn(a, b, n_blocks, name):
    S, M = a.shape
    N = b.shape[1]
    tn = N // n_blocks
    tm = M if M <= 1024 else M // 2
    tk = 2048
    nk = S // tk

    def body(a_ref, b_ref, o_ref):
        @pl.when(pl.program_id(2) == 0)
        def _():
            o_ref[...] = jnp.zeros_like(o_ref)
        o_ref[0] += _dot_tn(a_ref[...], b_ref[...])

    return pl.pallas_call(
        body, name=name, grid=(M // tm, n_blocks, nk),
        in_specs=[pl.BlockSpec((tk, tm), lambda i, j, k: (k, i)), pl.BlockSpec((tk, tn), lambda i, j, k: (k, j))],
        out_specs=pl.BlockSpec((1, tm, tn), lambda i, j, k: (j, i, 0)),
        out_shape=jax.ShapeDtypeStruct((n_blocks, M, tn), F32),
        compiler_params=_params(("parallel", "parallel", "arbitrary"), VMEM_LIMIT),
    )(a, b)


def _mix_out_bwd(dx1, mixed, g_post, w_out, attn, cargo=()):
    S, D = dx1.shape
    TM = 512
    nd = len(DILATIONS)
    nc = len(cargo)
    kinds = [kind for kind, _ in cargo]
    n_chunks = ATTN_WIDTH // LANES

    def body(*refs):
        dx_ref, m_ref, g_ref, w_ref, a_ref = refs[:5]
        dm_ref, dp_ref, dg_ref = refs[5 + nc:8 + nc]
        da_refs, dl_refs = refs[8 + nc:8 + nc + nd], refs[8 + nc + nd:8 + nc + 2 * nd]
        n_out = 8 + nc + 2 * nd
        t_s = refs[n_out + nc:n_out + nc + n_chunks]
        cargo_refs = (kinds, refs[5:5 + nc], refs[n_out:n_out + nc], refs[n_out + nc + n_chunks:])
        if nc:
            _cargo_start(*cargo_refs, pl.program_id(0) == 0)

        @pl.when(pl.program_id(0) == 0)
        def _():
            dg_ref[...] = jnp.zeros_like(dg_ref)

        n, r = _rms_stats(m_ref[...])
        dm, dg = _rms_bwd(dx_ref[...], n, r, g_ref[...])
        dg_ref[...] += dg
        dmb = dm.astype(BF16)
        dm_ref[...] = dmb
        da = _dot_nt(dmb, w_ref[:ATTN_WIDTH, :])
        _put_tokens(t_s, da)
        for i, d in enumerate(DILATIONS):
            _to_residue(da, t_s, da_refs[i], d, BF16)
        dp_ref[...] = _dot_nt(dmb, w_ref[ATTN_WIDTH:, :])
        prod = da * a_ref[...]
        hi = prod.astype(BF16)
        lo = (prod - hi.astype(F32)).astype(BF16)
        ri = lax.broadcasted_iota(jnp.int32, (ATTN_WIDTH, ATTN_WIDTH), 0) // HEAD_DIM
        ci = lax.broadcasted_iota(jnp.int32, (ATTN_WIDTH, ATTN_WIDTH), 1) // HEAD_DIM
        ones = (ri == ci).astype(BF16)
        delta = _dot(hi, ones) + _dot(lo, ones)
        _put_tokens(t_s, delta)
        for i, d in enumerate(DILATIONS):
            _to_residue(delta, t_s, dl_refs[i], d, F32)
        if nc:
            _cargo_finish(*cargo_refs, pl.program_id(0) == S // TM - 1)

    row = lambda w: pl.BlockSpec((TM, w), lambda i: (i, 0))
    specs = [_residue_spec(TM, d) for d in DILATIONS]
    arrays, cargo_specs, shapes, aliases, sems = _cargo_call(cargo, 5, 3 + 2 * nd)
    out = pl.pallas_call(
        body, name="mix_out_bwd", grid=(S // TM,),
        in_specs=[row(D), row(D), _const_spec((1, D)), _const_spec(w_out.shape), row(ATTN_WIDTH)] + cargo_specs,
        out_specs=[row(D), row(POOL_WIDTH), pl.BlockSpec((1, D), lambda i: (0, 0))] + specs * 2 + cargo_specs,
        out_shape=[jax.ShapeDtypeStruct((S, D), BF16), jax.ShapeDtypeStruct((S, POOL_WIDTH), F32),
                   jax.ShapeDtypeStruct((1, D), F32)]
        + [_residue_shape(S, d, BF16) for d in DILATIONS] + [_residue_shape(S, d, F32) for d in DILATIONS] + shapes,
        input_output_aliases=aliases,
        scratch_shapes=_token_scratch(TM) + sems,
        compiler_params=_params(("arbitrary",), VMEM_LIMIT),
    )(dx1, mixed, g_post, w_out, attn, *arrays)
    return out[0], out[1], out[2], out[3:3 + nd], out[3 + nd:3 + 2 * nd], out[3 + 2 * nd:]


def _pool_bwd(pool_in, d_pool, pool_w, pool_scale):
    S = pool_in.shape[0]
    TM = 512
    HB = TM // POOL_HALO
    last = S // POOL_HALO - 1
    G = len(POOL_WINDOWS)

    def body(cur_ref, halo_ref, dcur_ref, dnext_ref, w_ref, sc_ref, dxin_ref, dw_ref, dsc_ref):
        i = pl.program_id(0)

        @pl.when(i == 0)
        def _():
            dw_ref[...] = jnp.zeros_like(dw_ref)
            dsc_ref[...] = jnp.zeros_like(dsc_ref)

        halo = jnp.where(i > 0, halo_ref[...], 0.0)
        pooled = _pooled_groups(halo, cur_ref[...], i * TM)
        dnext = jnp.where(i < S // TM - 1, dnext_ref[...], 0.0)
        dye = jnp.concatenate([dcur_ref[...], dnext], axis=0)
        for g, w in enumerate(POOL_WINDOWS):
            sl = slice(g * POOL_GROUP, (g + 1) * POOL_GROUP)
            wg = w_ref[g].astype(BF16)
            pb = pooled[g].astype(BF16)
            dsc_ref[:, sl] += jnp.sum(dye[:TM, sl] * _dot(pb, wg), axis=0, keepdims=True)
            dpre = (dye[:, sl] * sc_ref[:, sl]).astype(BF16)
            dw_ref[g] += _dot_tn(pb, dpre[:TM])
            dpooled = _dot_nt(dpre, wg)
            z = dpooled / _pool_counts(i * TM, TM + POOL_HALO, w)
            dxin_ref[:, sl] = (_leading_sums(z, w)[:TM] - dpooled[:TM]).astype(BF16)

    row = pl.BlockSpec((TM, POOL_WIDTH), lambda i: (i, 0))
    return pl.pallas_call(
        body, name="pool_bwd", grid=(S // TM,),
        in_specs=[row, pl.BlockSpec((POOL_HALO, POOL_WIDTH), lambda i: (jnp.maximum(i * HB - 1, 0), 0)),
                  row, pl.BlockSpec((POOL_HALO, POOL_WIDTH), lambda i: (jnp.minimum((i + 1) * HB, last), 0)),
                  _const_spec(pool_w.shape), _const_spec((1, POOL_WIDTH))],
        out_specs=[row, pl.BlockSpec((G, POOL_GROUP, POOL_GROUP), lambda i: (0, 0, 0)),
                   pl.BlockSpec((1, POOL_WIDTH), lambda i: (0, 0))],
        out_shape=[jax.ShapeDtypeStruct((S, POOL_WIDTH), BF16), jax.ShapeDtypeStruct((G, POOL_GROUP, POOL_GROUP), F32),
                   jax.ShapeDtypeStruct((1, POOL_WIDTH), F32)],
        compiler_params=_params(("arbitrary",)),
    )(pool_in, pool_in, d_pool, d_pool, pool_w, pool_scale)


def _attn_bwd(q, k, v, d_attn, lse, delta, d, cargo=()):
    L = q.shape[0]
    nb = L // ATTN_BLOCK
    group = min(d, RESIDUES_PER_STEP)
    width = group * ATTN_WIDTH
    nc = len(cargo)
    kinds = [kind for kind, _ in cargo]

    def body(*refs):
        q_ref, kp_ref, kc_ref, vp_ref, vc_ref, do_ref, lse_ref, dl_ref = refs[:8]
        dq_ref, dk_ref, dv_ref = refs[8 + nc:11 + nc]
        ck_s, cv_s = refs[11 + 2 * nc:13 + 2 * nc]
        cargo_refs = (kinds, refs[8:8 + nc], refs[11 + nc:11 + 2 * nc], refs[13 + 2 * nc:])
        r, n = pl.program_id(0), pl.program_id(1)
        if nc:
            _cargo_start(*cargo_refs, (r == 0) & (n == 0))

        @pl.when(n == 0)
        def _():
            ck_s[...] = jnp.zeros_like(ck_s)
            cv_s[...] = jnp.zeros_like(cv_s)

        @pl.when(n < nb)
        def _():
            valid = _band_mask(n)
            valid2 = jnp.concatenate([valid, valid], axis=0)
            first = _first_head_lanes()

            def stacked_column(ref, lane):
                return jnp.concatenate([ref[:, lane:lane + 1], ref[:, lane + HEAD_DIM:lane + HEAD_DIM + 1]], axis=0)

            for hp in range(width // LANES):
                sl = slice(hp * LANES, (hp + 1) * LANES)
                qq = _stack_heads(q_ref[:, sl], first)
                dd = _stack_heads(do_ref[:, sl], first)
                kk = jnp.concatenate([kp_ref[:, sl], kc_ref[:, sl]], axis=0)
                vv = jnp.concatenate([vp_ref[:, sl], vc_ref[:, sl]], axis=0)
                s = _dot_nt(qq, kk)
                p = jnp.where(valid2, jnp.exp(s - stacked_column(lse_ref, hp * LANES)), 0.0)
                dp = _dot_nt(dd, vv)
                ds = (p * (dp - stacked_column(dl_ref, hp * LANES))).astype(BF16)
                dq_ref[:, sl] = (_unstack_heads(_dot(ds, kk), first) * ATTN_SCALE).astype(BF16)
                dk = _dot_tn(ds, qq)
                dv = _dot_tn(p.astype(BF16), dd)
                dk_ref[:, sl] = (ck_s[:, sl] + dk[:ATTN_BLOCK]).astype(BF16)
                dv_ref[:, sl] = (cv_s[:, sl] + dv[:ATTN_BLOCK]).astype(BF16)
                ck_s[:, sl] = dk[ATTN_BLOCK:]
                cv_s[:, sl] = dv[ATTN_BLOCK:]

        @pl.when(n == nb)
        def _():
            dk_ref[...] = ck_s[...].astype(BF16)
            dv_ref[...] = cv_s[...].astype(BF16)

        if nc:
            _cargo_finish(*cargo_refs, (r == d // group - 1) & (n == nb))

    blk = (ATTN_BLOCK, width)
    cur = pl.BlockSpec(blk, lambda r, n: (jnp.minimum(n, nb - 1), r))
    prev = pl.BlockSpec(blk, lambda r, n: (jnp.maximum(jnp.minimum(n, nb - 1) - 1, 0), r))
    done = pl.BlockSpec(blk, lambda r, n: (jnp.maximum(n - 1, 0), r))
    arrays, specs, shapes, aliases, sems = _cargo_call(cargo, 8, 3)
    out = pl.pallas_call(
        body, name=f"attn_bwd_d{d}", grid=(d // group, nb + 1),
        in_specs=[cur, prev, cur, prev, cur, cur, cur, cur] + specs, out_specs=[cur, done, done] + specs,
        out_shape=[jax.ShapeDtypeStruct((L, d * ATTN_WIDTH), BF16)] * 3 + shapes,
        input_output_aliases=aliases,
        scratch_shapes=[pltpu.VMEM(blk, F32), pltpu.VMEM(blk, F32)] + sems,
        compiler_params=_params(("arbitrary", "arbitrary")),
    )(q, k, k, v, v, d_attn, lse, delta, *arrays)
    return out[:3], out[3:]


def _attn_bwd_consecutive(q, k, v, d_attn, lse, delta, cargo=()):
    L = q.shape[0]
    qb = CONSECUTIVE_BLOCKS
    steps = L // (qb * ATTN_BLOCK)
    nc = len(cargo)
    kinds = [kind for kind, _ in cargo]

    def body(*refs):
        q_ref, kp_ref, kc_ref, vp_ref, vc_ref, do_ref, lse_ref, dl_ref = refs[:8]
        dq_ref, dk_ref, dv_ref, ek_ref, ev_ref = refs[8 + nc:13 + nc]
        cargo_refs = (kinds, refs[8:8 + nc], refs[13 + nc:13 + 2 * nc], refs[13 + 2 * nc:])
        n = pl.program_id(0)
        if nc:
            _cargo_start(*cargo_refs, n == 0)
        first = _first_head_lanes()
        for hp in range(ATTN_WIDTH // LANES):
            sl = slice(hp * LANES, (hp + 1) * LANES)
            for sub in range(qb):
                rows = slice(sub * ATTN_BLOCK, (sub + 1) * ATTN_BLOCK)
                valid = _band_mask(n if sub == 0 else 1)
                valid2 = jnp.concatenate([valid, valid], axis=0)
                if sub == 0:
                    kk = jnp.concatenate([kp_ref[:, sl], kc_ref[rows, sl]], axis=0)
                    vv = jnp.concatenate([vp_ref[:, sl], vc_ref[rows, sl]], axis=0)
                else:
                    keys = slice((sub - 1) * ATTN_BLOCK, (sub + 1) * ATTN_BLOCK)
                    kk, vv = kc_ref[keys, sl], vc_ref[keys, sl]
                qq = _stack_heads(q_ref[rows, sl], first)
                dd = _stack_heads(do_ref[rows, sl], first)
                column = lambda ref: jnp.concatenate(
                    [ref[rows, hp * LANES:hp * LANES + 1], ref[rows, hp * LANES + HEAD_DIM:hp * LANES + HEAD_DIM + 1]], axis=0)
                p = jnp.where(valid2, jnp.exp(_dot_nt(qq, kk) - column(lse_ref)), 0.0)
                ds = (p * (_dot_nt(dd, vv) - column(dl_ref))).astype(BF16)
                dq_ref[rows, sl] = (_unstack_heads(_dot(ds, kk), first) * ATTN_SCALE).astype(BF16)
                dk = _dot_tn(ds, qq)
                dv = _dot_tn(p.astype(BF16), dd)
                if sub == 0:
                    ek_ref[:, sl] = dk[:ATTN_BLOCK].astype(BF16)
                    ev_ref[:, sl] = dv[:ATTN_BLOCK].astype(BF16)
                else:
                    before = slice((sub - 1) * ATTN_BLOCK, sub * ATTN_BLOCK)
                    dk_ref[before, sl] = (carry_k + dk[:ATTN_BLOCK]).astype(BF16)
                    dv_ref[before, sl] = (carry_v + dv[:ATTN_BLOCK]).astype(BF16)
                carry_k, carry_v = dk[ATTN_BLOCK:], dv[ATTN_BLOCK:]
            dk_ref[rows, sl] = carry_k.astype(BF16)
            dv_ref[rows, sl] = carry_v.astype(BF16)
        if nc:
            _cargo_finish(*cargo_refs, n == steps - 1)

    cur = pl.BlockSpec((qb * ATTN_BLOCK, ATTN_WIDTH), lambda n: (n, 0))
    prev = pl.BlockSpec((ATTN_BLOCK, ATTN_WIDTH), lambda n: (jnp.maximum(n * qb - 1, 0), 0))
    edge = pl.BlockSpec((ATTN_BLOCK, ATTN_WIDTH), lambda n: (n, 0))
    arrays, specs, shapes, aliases, sems = _cargo_call(cargo, 8, 5)
    out = pl.pallas_call(
        body, name="attn_bwd_d1", grid=(steps,),
        in_specs=[cur, prev, cur, prev, cur, cur, cur, cur] + specs, out_specs=[cur, cur, cur, edge, edge] + specs,
        out_shape=[jax.ShapeDtypeStruct((L, ATTN_WIDTH), BF16)] * 3
        + [jax.ShapeDtypeStruct((steps * ATTN_BLOCK, ATTN_WIDTH), BF16)] * 2 + shapes,
        input_output_aliases=aliases, scratch_shapes=sems,
        compiler_params=_params(("arbitrary",)),
    )(q, k, k, v, v, d_attn, lse, delta, *arrays)
    return out[:3], out[3:5], out[5:]


def _mix_in_bwd(dqkv, edges, d_pool_in, w_in, x, g_pre, dx1):
    S, D = x.shape
    TM = CONSECUTIVE_BLOCKS * ATTN_BLOCK
    nd = len(DILATIONS)
    n_tiles = S // TM

    def body(*refs):
        g_refs = refs[:3 * nd]
        e_refs = (None,) + refs[3 * nd:3 * nd + 2]
        dpi_ref, w_ref, x_ref, g_ref, dx1_ref, dproj_ref, gx_ref, dg_ref = refs[3 * nd + 2:3 * nd + 10]
        t_s = refs[3 * nd + 10:]

        @pl.when(pl.program_id(0) == 0)
        def _():
            dg_ref[...] = jnp.zeros_like(dg_ref)

        dh = jnp.zeros((TM, D), F32)
        for a in range(4):
            if a < 3:
                tot = g_refs[a][...].astype(F32)
                if a > 0:
                    late = jnp.where(pl.program_id(0) < n_tiles - 1, e_refs[a][...].astype(F32), 0.0)
                    tot = jnp.concatenate([tot[:TM - ATTN_BLOCK], tot[TM - ATTN_BLOCK:] + late], axis=0)
                for i, d in enumerate(DILATIONS[1:]):
                    tot = tot + _from_residue(g_refs[3 * (i + 1) + a], t_s, d)
                db = tot.astype(BF16)
            else:
                db = dpi_ref[...]
            dproj_ref[:, a * ATTN_WIDTH:(a + 1) * ATTN_WIDTH] = db
            dh = dh + _dot_nt(db, w_ref[a])
        n, r = _rms_stats(x_ref[...])
        dx, dg = _rms_bwd(dh, n, r, g_ref[...])
        dg_ref[...] += dg
        gx_ref[...] = dx1_ref[...] + dx

    row = lambda w: pl.BlockSpec((TM, w), lambda i: (i, 0))
    edge = pl.BlockSpec((ATTN_BLOCK, ATTN_WIDTH), lambda i: (jnp.minimum(i + 1, n_tiles - 1), 0))
    return pl.pallas_call(
        body, name="mix_in_bwd", grid=(S // TM,),
        in_specs=[_residue_spec(TM, d) for d in DILATIONS for _ in range(3)] + [edge, edge]
        + [row(POOL_WIDTH), _const_spec(w_in.shape), row(D), _const_spec((1, D)), row(D)],
        out_specs=[row(4 * ATTN_WIDTH), row(D), pl.BlockSpec((1, D), lambda i: (0, 0))],
        out_shape=[jax.ShapeDtypeStruct((S, 4 * ATTN_WIDTH), BF16), jax.ShapeDtypeStruct((S, D), F32),
                   jax.ShapeDtypeStruct((1, D), F32)],
        scratch_shapes=_token_scratch(TM),
        compiler_params=_params(("arbitrary",), VMEM_LIMIT),
    )(*[g for gs in dqkv for g in gs], *edges, d_pool_in, w_in, x, g_pre, dx1)


SMALL_EARLY = ("pool_w", "pool_scale", "g_mix_post", "g_ffn_pre", "conv_b", "g_ffn_post", "conv_w")
SMALL_LATE = ("g_mix_pre",)


def _pack_small(grads, names):
    parts = []
    for n in names:
        g = grads[n]
        if n == "conv_w":
            g = g.reshape(CONV_WIDTH, N_SHARD, -1).transpose(1, 0, 2)
        parts.append(g.reshape(-1, LANES))
    return jnp.concatenate(parts, axis=0) if len(parts) > 1 else parts[0]


def _unpack_small(packed, names, like, shard):
    out, row = {}, 0
    for n in names:
        size = like[n].size * (N_SHARD if n == "conv_w" else 1)
        g = packed[row:row + size // LANES]
        row += size // LANES
        if n == "conv_w":
            g = lax.dynamic_slice_in_dim(g.reshape((N_SHARD,) + like[n].shape), shard, 1, axis=0)[0]
        out[n] = g.reshape(like[n].shape)
    return out


def _local_step(x, target, g_mix_pre, w_in, pool_w, pool_scale, w_out, g_mix_post, g_ffn_pre,
                w_up, conv_w, conv_b, w_down, g_ffn_post, mesh_pos=None):
    on_mesh = mesh_pos is not None
    D = x.shape[1]
    CW = w_up.shape[2]
    qkv, pool_in, h1, got = _mix_in_fwd(x, g_mix_pre, w_in, [("ici", w_up)] if on_mesh else ())
    w_up = got[0] if on_mesh else w_up
    o1, l1, got = _attn_fwd(*qkv[0], 1, [("d2d", w_up), ("ici", w_out), ("ici", w_down)] if on_mesh else ())
    w_up, w_out, w_down = got if on_mesh else (w_up, w_out, w_down)
    o4, l4, got = _attn_fwd(*qkv[1], 4, [("d2d", w_out), ("d2d", w_down)] if on_mesh else ())
    w_out, w_down = got if on_mesh else (w_out, w_down)
    o16, l16, _ = _attn_fwd(*qkv[2], 16)
    w_out = w_out.reshape(D, D)
    w_down = w_down.reshape(2 * CW, D)
    attn, lse = _attn_mix((o1, o4, o16), (l1, l4, l16))
    pool = _pool_fwd(pool_in, pool_w, pool_scale)
    mixed, x1, h2, cat = _mix_out_fwd(attn, pool, w_out, x, g_mix_post, g_ffn_pre)

    yv, dy, df, dc, loss, d_g_ffn_post, d_conv_b, d_conv_w = _ffn_fwd(
        h2, x1, target, w_up, w_down, conv_w, conv_b, g_ffn_post)
    du, dx1, d_g_ffn_pre = _ffn_bwd(dc, conv_w, w_up, x1, g_ffn_pre, dy)
    d_w_up = _matmul_tn(h2, du, N_SHARD, "grad_w_up")
    d_w_down = _matmul_tn(yv, df, 1, "grad_w_down")[0].reshape(N_SHARD, CW // 2, D)
    swap = [("swap", d_w_up), ("swap", d_w_down)] if on_mesh else ()
    d_mixed, d_pool, d_g_mix_post, d_attn, delta, from_sibling = _mix_out_bwd(dx1, mixed, g_mix_post, w_out, attn, swap)
    d_w_out = _matmul_tn(cat, d_mixed, 1, "grad_w_out")[0].reshape(N_SHARD, D // N_SHARD, D)
    d_pool_in, d_pool_w, d_pool_scale = _pool_bwd(pool_in, d_pool, pool_w, pool_scale)
    grads = dict(pool_w=d_pool_w, pool_scale=d_pool_scale, w_out=d_w_out, g_mix_post=d_g_mix_post,
                 g_ffn_pre=d_g_ffn_pre, w_up=d_w_up, conv_w=d_conv_w, conv_b=d_conv_b, w_down=d_w_down,
                 g_ffn_post=d_g_ffn_post)
    cargo = [(), (), ()]
    if on_mesh:
        c_arr, device = mesh_pos
        up_f32, up_bf16 = _pair_sum(d_w_up, from_sibling[0], c_arr, "pair_sum_w_up")
        down_f32, down_bf16 = _pair_sum(d_w_down, from_sibling[1], c_arr, "pair_sum_w_down")
        early = _pack_small(grads, SMALL_EARLY)
        early_slots = lax.dynamic_update_index_in_dim(jnp.zeros((8,) + early.shape, F32), early, device, 0)
        cargo = [[("scatter", down_bf16)], [("scatter", up_bf16)], [("everyone", early_slots)]]

    dqkv1, edges, landed1 = _attn_bwd_consecutive(*qkv[0], d_attn[0], lse[0], delta[0], cargo[0])
    dqkv, landed = zip(*[_attn_bwd(*qkv[i], d_attn[i], lse[i], delta[i], DILATIONS[i], cargo[i]) for i in (1, 2)])
    if on_mesh:
        grads.update(w_down=(down_f32, landed1[0]), w_up=(up_f32, landed[0][0]), small_early=landed[1][0])
    d_proj, grad_x, grads["g_mix_pre"] = _mix_in_bwd((dqkv1,) + dqkv, edges, d_pool_in, w_in, x, g_mix_pre, dx1)
    grads["w_in"] = _matmul_tn(h1, d_proj, N_SHARD, "grad_w_in")
    return loss, grad_x, grads


ANY = pl.BlockSpec(memory_space=pl.ANY)


def _position():
    x, y, c = lax.axis_index("x"), lax.axis_index("y"), lax.axis_index("c")
    chips = [(1 - x, y), (x, 1 - y), (1 - x, 1 - y)]
    return x, y, c, chips


def _remote(src, dst, send_sem, recv_sem, to):
    return pltpu.make_async_remote_copy(src_ref=src, dst_ref=dst, send_sem=send_sem, recv_sem=recv_sem,
                                        device_id=to, device_id_type=MESH)


def _cast_bf16(w, shard_arr, name):
    R, C = w.shape
    tr = R // 2

    def body(s_ref, w_ref, o_ref):
        o_ref[0] = w_ref[...].astype(BF16)

    return pl.pallas_call(
        body, name=name,
        grid_spec=pltpu.PrefetchScalarGridSpec(
            num_scalar_prefetch=1, grid=(2,),
            in_specs=[pl.BlockSpec((tr, C), lambda i, s_ref: (i, 0))],
            out_specs=pl.BlockSpec((1, tr, C), lambda i, s_ref: (s_ref[0], i, 0))),
        out_shape=jax.ShapeDtypeStruct((N_SHARD, R, C), BF16),
        compiler_params=_params(("parallel",)))(shard_arr, w)


def _gather_weights(bufs):
    n = len(bufs) - 1

    def body(*refs):
        outs, cw_out = refs[n + 1:2 * n + 1], refs[2 * n + 1]
        ici_send, ici_recv, d2d_send, d2d_recv = refs[2 * n + 2:]
        x, y, c, chips = _position()
        s = 2 * x + y
        sibling = (x, y, 1 - c)

        def half(a, shard, h):
            rows = outs[a].shape[1] // 2
            return outs[a].at[shard, pl.ds(h * rows, rows), :]

        sends = []
        for a in range(n):
            for j, (px, py) in enumerate(chips):
                sends.append(_remote(half(a, s, c), half(a, s, c),
                                     ici_send.at[3 * a + j], ici_recv.at[3 * a + j], (px, py, c)))
        for j, (px, py) in enumerate(chips):
            sends.append(_remote(cw_out.at[s], cw_out.at[s], ici_send.at[3 * n + j], ici_recv.at[3 * n + j], (px, py, c)))
        for cp in sends:
            cp.start()
        passed = []
        for a in range(n):
            for j, (px, py) in enumerate(chips):
                sj = 2 * px + py
                got = half(a, sj, c)
                _remote(got, got, ici_send.at[3 * a + j], ici_recv.at[3 * a + j], (px, py, c)).wait_recv()
                fwd = _remote(got, got, d2d_send.at[3 * a + j], d2d_recv.at[3 * a + j], sibling)
                fwd.start()
                passed.append(fwd)
        for j, (px, py) in enumerate(chips):
            got = cw_out.at[2 * px + py]
            _remote(got, got, ici_send.at[3 * n + j], ici_recv.at[3 * n + j], (px, py, c)).wait_recv()
        for a in range(n):
            for j, (px, py) in enumerate(chips):
                got = half(a, 2 * px + py, 1 - c)
                _remote(got, got, d2d_send.at[3 * a + j], d2d_recv.at[3 * a + j], sibling).wait_recv()
        for cp in sends + passed:
            cp.wait_send()

    return pl.pallas_call(
        body, name="gather_weights",
        in_specs=[ANY] * (n + 1), out_specs=[ANY] * (n + 1),
        out_shape=[jax.ShapeDtypeStruct(b.shape, b.dtype) for b in bufs],
        input_output_aliases={i: i for i in range(n + 1)},
        scratch_shapes=[pltpu.SemaphoreType.DMA((3 * n + 3,)), pltpu.SemaphoreType.DMA((3 * n + 3,)),
                        pltpu.SemaphoreType.DMA((3 * n,)), pltpu.SemaphoreType.DMA((3 * n,))],
        compiler_params=pltpu.CompilerParams(has_side_effects=True),
    )(*bufs)


def _swap_halves(grads, tag):
    n = len(grads)

    def body(*refs):
        ins, outs, send_sem, recv_sem = refs[:n], refs[n:2 * n], refs[2 * n], refs[2 * n + 1]
        x, y, c, _ = _position()
        copies = []
        for a in range(n):
            rows = ins[a].shape[1] // 2
            copies.append(_remote(ins[a].at[:, pl.ds((1 - c) * rows, rows), :], outs[a],
                                  send_sem.at[a], recv_sem.at[a], (x, y, 1 - c)))
        for cp in copies:
            cp.start()
        for cp in copies:
            cp.wait()

    return pl.pallas_call(
        body, name="swap_grad_halves_" + tag,
        in_specs=[ANY] * n, out_specs=[ANY] * n,
        out_shape=[jax.ShapeDtypeStruct((g.shape[0], g.shape[1] // 2, g.shape[2]), F32) for g in grads],
        scratch_shapes=[pltpu.SemaphoreType.DMA((n,)), pltpu.SemaphoreType.DMA((n,))],
        compiler_params=pltpu.CompilerParams(has_side_effects=True),
    )(*grads)


def _pair_sum(g, got, c_arr, name):
    n_sh, R, C = g.shape
    rows = R // 2

    def body(c_ref, g_ref, r_ref, f_ref, b_ref):
        t = g_ref[...] + r_ref[...]
        f_ref[...] = t
        b_ref[...] = t.astype(BF16)

    blk = pl.BlockSpec((1, rows, C), lambda i, c_ref: (i, 0, 0))
    return pl.pallas_call(
        body, name=name,
        grid_spec=pltpu.PrefetchScalarGridSpec(
            num_scalar_prefetch=1, grid=(n_sh,),
            in_specs=[pl.BlockSpec((1, rows, C), lambda i, c_ref: (i, c_ref[0], 0)), blk],
            out_specs=[blk, blk]),
        out_shape=[jax.ShapeDtypeStruct((n_sh, rows, C), F32), jax.ShapeDtypeStruct((n_sh, rows, C), BF16)],
        compiler_params=_params(("parallel",)),
    )(c_arr, g, got)


def _scatter_grads(sums_bf16, small_all):
    n = len(sums_bf16)

    def body(*refs):
        b_ins = refs[:n]
        recvs, sm = refs[n + 1:2 * n + 1], refs[2 * n + 1]
        ici_send, ici_recv, sm_send, sm_recv = refs[2 * n + 2:]
        x, y, c, chips = _position()
        me = 4 * x + 2 * y + c
        copies = []
        for a in range(n):
            for j, (px, py) in enumerate(chips):
                copies.append(_remote(b_ins[a].at[2 * px + py], recvs[a].at[j],
                                      ici_send.at[3 * a + j], ici_recv.at[3 * a + j], (px, py, c)))
        for k in range(1, 8):
            peer = (x ^ (k >> 2), y ^ ((k >> 1) & 1), c ^ (k & 1))
            copies.append(_remote(sm.at[me], sm.at[me], sm_send.at[k - 1], sm_recv.at[k - 1], peer))
        for cp in copies:
            cp.start()
        for cp in copies:
            cp.wait_send()
        for a in range(n):
            for j, (px, py) in enumerate(chips):
                _remote(recvs[a].at[j], recvs[a].at[j], ici_send.at[3 * a + j], ici_recv.at[3 * a + j],
                        (px, py, c)).wait_recv()
        for k in range(1, 8):
            peer = (x ^ (k >> 2), y ^ ((k >> 1) & 1), c ^ (k & 1))
            theirs = sm.at[4 * peer[0] + 2 * peer[1] + peer[2]]
            _remote(theirs, theirs, sm_send.at[k - 1], sm_recv.at[k - 1], peer).wait_recv()

    out = pl.pallas_call(
        body, name="scatter_grads",
        in_specs=[ANY] * (n + 1), out_specs=[ANY] * (n + 1),
        out_shape=[jax.ShapeDtypeStruct((3,) + b.shape[1:], BF16) for b in sums_bf16]
        + [jax.ShapeDtypeStruct(small_all.shape, F32)],
        input_output_aliases={n: n},
        scratch_shapes=[pltpu.SemaphoreType.DMA((3 * n,)), pltpu.SemaphoreType.DMA((3 * n,)),
                        pltpu.SemaphoreType.DMA((7,)), pltpu.SemaphoreType.DMA((7,))],
        compiler_params=pltpu.CompilerParams(has_side_effects=True),
    )(*sums_bf16, small_all)
    return out[:n], out[n]


def _shard_sum(sums_f32, recv, shard_arr, c_arr, name):
    _, rows, C = sums_f32.shape

    def body(s_ref, c_ref, o_ref, r_ref, t_ref):
        t_ref[...] = ((o_ref[0] + r_ref[0].astype(F32)) + r_ref[1].astype(F32)) + r_ref[2].astype(F32)

    return pl.pallas_call(
        body, name=name,
        grid_spec=pltpu.PrefetchScalarGridSpec(
            num_scalar_prefetch=2, grid=(1,),
            in_specs=[pl.BlockSpec((1, rows, C), lambda i, s_ref, c_ref: (s_ref[0], 0, 0)),
                      pl.BlockSpec((3, rows, C), lambda i, s_ref, c_ref: (0, 0, 0))],
            out_specs=pl.BlockSpec((rows, C), lambda i, s_ref, c_ref: (c_ref[0], 0))),
        out_shape=jax.ShapeDtypeStruct((2 * rows, C), F32),
        compiler_params=_params(("arbitrary",)),
    )(shard_arr, c_arr, sums_f32, recv)


def _join_halves(bufs):
    n = len(bufs)

    def body(*refs):
        outs, send_sem, recv_sem = refs[n:2 * n], refs[2 * n], refs[2 * n + 1]
        x, y, c, _ = _position()
        copies = []
        for a in range(n):
            rows = outs[a].shape[0] // 2
            mine = outs[a].at[pl.ds(c * rows, rows), :]
            copies.append(_remote(mine, mine, send_sem.at[a], recv_sem.at[a], (x, y, 1 - c)))
        for cp in copies:
            cp.start()
        for a, cp in enumerate(copies):
            cp.wait_send()
            rows = outs[a].shape[0] // 2
            theirs = outs[a].at[pl.ds((1 - c) * rows, rows), :]
            _remote(theirs, theirs, send_sem.at[a], recv_sem.at[a], (x, y, 1 - c)).wait_recv()

    return pl.pallas_call(
        body, name="join_grad_halves",
        in_specs=[ANY] * n, out_specs=[ANY] * n,
        out_shape=[jax.ShapeDtypeStruct(b.shape, F32) for b in bufs],
        input_output_aliases={i: i for i in range(n)},
        scratch_shapes=[pltpu.SemaphoreType.DMA((n,)), pltpu.SemaphoreType.DMA((n,))],
        compiler_params=pltpu.CompilerParams(has_side_effects=True),
    )(*bufs)


def _small_sum(parts, tag):
    _, R, C = parts.shape

    def body(p_ref, o_ref):
        t = p_ref[0]
        for k in range(1, 8):
            t = t + p_ref[k]
        o_ref[...] = t

    return pl.pallas_call(
        body, name="small_grad_sum_" + tag, grid=(1,),
        in_specs=[pl.BlockSpec((8, R, C), lambda i: (0, 0, 0))], out_specs=pl.BlockSpec((R, C), lambda i: (0, 0)),
        out_shape=jax.ShapeDtypeStruct((R, C), F32), compiler_params=_params(("arbitrary",)),
    )(parts)


def _adamw_math(w, g, m, v):
    m = ADAM_B1 * m + (1.0 - ADAM_B1) * g
    v = ADAM_B2 * v + (1.0 - ADAM_B2) * (g * g)
    m_hat = m / (1.0 - ADAM_B1 ** ADAM_STEP)
    v_hat = v / (1.0 - ADAM_B2 ** ADAM_STEP)
    delta = -ADAM_LR * (m_hat / (jnp.sqrt(v_hat) + ADAM_EPS) + ADAM_WD * w)
    return delta, m, v


def _adamw_big(w, g, m, v, name):
    R, C = w.shape
    tr = R // 4

    def body(w_ref, g_ref, m_ref, v_ref, d_ref, nm_ref, nv_ref):
        d_ref[...], nm_ref[...], nv_ref[...] = _adamw_math(w_ref[...], g_ref[...], m_ref[...], v_ref[...])

    blk = pl.BlockSpec((tr, C), lambda i: (i, 0))
    return pl.pallas_call(
        body, name=name, grid=(4,), in_specs=[blk] * 4, out_specs=[blk] * 3,
        out_shape=[jax.ShapeDtypeStruct((R, C), F32)] * 3, compiler_params=_params(("parallel",)),
    )(w, g, m, v)


def _adamw_small(ws, gs, ms, vs):
    n = len(ws)

    def body(*refs):
        for a in range(n):
            w, g, m, v = (refs[k * n + a][...] for k in range(4))
            d, nm, nv = _adamw_math(w, g, m, v)
            refs[4 * n + a][...] = d
            refs[5 * n + a][...] = nm
            refs[6 * n + a][...] = nv

    shapes = [jax.ShapeDtypeStruct(w.shape, F32) for w in ws]
    out = pl.pallas_call(body, name="adamw_small", out_shape=shapes * 3)(*ws, *gs, *ms, *vs)
    return out[:n], out[n:2 * n], out[2 * n:]


BIG = ("w_in", "w_out", "w_up", "w_down")
SMALL = ("g_mix_pre", "pool_w", "pool_scale", "g_mix_post", "g_ffn_pre", "conv_b", "g_ffn_post", "conv_w")
ORDER = ("g_mix_pre", "w_in", "pool_w", "pool_scale", "w_out", "g_mix_post", "g_ffn_pre", "w_up", "conv_w", "conv_b",
         "w_down", "g_ffn_post")


def kernel(x, g_mix_pre, w_in, pool_w, pool_scale, w_out, g_mix_post, g_ffn_pre, w_up, conv_w, conv_b, w_down, g_ffn_post, loss_target, m_g_mix_pre, m_w_in, m_pool_w, m_pool_scale, m_w_out, m_g_mix_post, m_g_ffn_pre, m_w_up, m_conv_w, m_conv_b, m_w_down, m_g_ffn_post, v_g_mix_pre, v_w_in, v_pool_w, v_pool_scale, v_w_out, v_g_mix_post, v_g_ffn_pre, v_w_up, v_conv_w, v_conv_b, v_w_down, v_g_ffn_post):
    args = dict(locals())
    W = {n: args[n][0] for n in ORDER}
    M = {n: args["m_" + n][0] for n in ORDER}
    V = {n: args["v_" + n][0] for n in ORDER}
    for d in (W, M, V):
        d["pool_w"] = d["pool_w"].reshape(-1, POOL_GROUP)
        for n in ("g_mix_pre", "pool_scale", "g_mix_post", "g_ffn_pre", "conv_b", "g_ffn_post"):
            d[n] = d[n].reshape(1, -1)
    CW = W["w_up"].shape[1]
    c_arr = lax.axis_index("c").astype(jnp.int32).reshape(1)
    shard = 2 * lax.axis_index("x") + lax.axis_index("y")
    shard_arr = shard.astype(jnp.int32).reshape(1)
    device = 2 * shard + lax.axis_index("c")

    conv_w_slots = lax.dynamic_update_index_in_dim(jnp.zeros((N_SHARD,) + W["conv_w"].shape, F32), W["conv_w"], shard, 0)
    slots = {n: _cast_bf16(W[n], shard_arr, "cast_" + n) for n in BIG}
    w_in_g, conv_w_g = _gather_weights([slots["w_in"], conv_w_slots])
    conv_w_full = conv_w_g.transpose(1, 0, 2).reshape(CONV_WIDTH, 1, N_SHARD * CW)

    loss, grad_x, G = _local_step(
        x[0], loss_target[0], W["g_mix_pre"], w_in_g, W["pool_w"].reshape(-1, POOL_GROUP, POOL_GROUP), W["pool_scale"],
        slots["w_out"], W["g_mix_post"], W["g_ffn_pre"], slots["w_up"], conv_w_full, W["conv_b"],
        slots["w_down"], W["g_ffn_post"], (c_arr, device))

    late = ("w_in", "w_out")
    from_sibling = _swap_halves([G[n] for n in late], "mix")
    sums = {n: _pair_sum(G[n], r, c_arr, "pair_sum_" + n) for n, r in zip(late, from_sibling)}
    small = _pack_small(G, SMALL_LATE)
    small_slots = lax.dynamic_update_index_in_dim(jnp.zeros((8,) + small.shape, F32), small, device, 0)
    recvs, small_all = _scatter_grads([sums[n][1] for n in late], small_slots)
    reduced = {n: (sums[n][0], r) for n, r in zip(late, recvs)}
    reduced.update({n: G[n] for n in ("w_up", "w_down")})
    halves = [_shard_sum(*reduced[n], shard_arr, c_arr, "shard_sum_" + n) for n in BIG]
    full = dict(zip(BIG, _join_halves(halves)))
    full.update(_unpack_small(_small_sum(G["small_early"], "early"), SMALL_EARLY, W, shard))
    full.update(_unpack_small(_small_sum(small_all, "late"), SMALL_LATE, W, shard))

    delta, new_m, new_v = {}, {}, {}
    for n in BIG:
        delta[n], new_m[n], new_v[n] = _adamw_big(W[n], full[n], M[n], V[n], "adamw_" + n)
    ds, nms, nvs = _adamw_small([W[n] for n in SMALL], [full[n] for n in SMALL], [M[n] for n in SMALL],
                                [V[n] for n in SMALL])
    for n, d, nm, nv in zip(SMALL, ds, nms, nvs):
        delta[n], new_m[n], new_v[n] = d, nm, nv

    loss = lax.psum(loss[0, 0], ("x", "y", "c"))
    shaped = lambda d: [d[n].reshape(args[n].shape) for n in ORDER]
    return (loss, grad_x[None], *shaped(full), *shaped(delta), *shaped(new_m), *shaped(new_v))
```

```python
import functools

import jax
import jax.numpy as jnp
from jax import lax
from jax.experimental import pallas as pl
from jax.experimental.pallas import tpu as pltpu

F32 = jnp.float32
BF16 = jnp.bfloat16

RMS_EPS = 1e-6
NEG_INF = -1e30
N_HEADS = 8
HEAD_DIM = 64
ATTN_WIDTH = N_HEADS * HEAD_DIM
ATTN_SCALE = HEAD_DIM ** -0.5
ATTN_BLOCK = 128
DILATIONS = (1, 4, 16)
RESIDUES_PER_STEP = 4
CONSECUTIVE_BLOCKS = 4
POOL_WINDOWS = (2, 4, 8, 16)
POOL_GROUP = 128
POOL_WIDTH = POOL_GROUP * len(POOL_WINDOWS)
POOL_HALO = 16
CONV_WIDTH = 3
CONV_HALO = 8
N_SHARD = 4
LANES = 128
STAT_LANES = 16
STAT_WIDTH = N_HEADS * STAT_LANES

ADAM_LR = 0.001
ADAM_B1 = 0.9
ADAM_B2 = 0.999
ADAM_EPS = 1e-08
ADAM_WD = 0.01
ADAM_STEP = 10

VMEM_LIMIT = 60 * 1024 * 1024
MESH = pl.DeviceIdType.MESH
NT = (((1,), (1,)), ((), ()))
TN = (((0,), (0,)), ((), ()))


def _params(sem, vmem=None):
    return pltpu.CompilerParams(dimension_semantics=sem, vmem_limit_bytes=vmem)


def _const_spec(shape):
    zeros = (0,) * len(shape)
    return pl.BlockSpec(shape, lambda *_: zeros, pipeline_mode=pl.Buffered(1))


def _dot(a, b):
    return jnp.dot(a, b, preferred_element_type=F32)


def _dot_nt(a, b):
    return lax.dot_general(a, b, NT, preferred_element_type=F32)


def _dot_tn(a, b):
    return lax.dot_general(a, b, TN, preferred_element_type=F32)


def _rms_stats(x):
    r = lax.rsqrt(jnp.mean(x * x, axis=-1, keepdims=True) + RMS_EPS)
    return x * r, r


def _rms_bwd(dy, n, r, g):
    dg = jnp.sum(dy * n, axis=0, keepdims=True)
    dn = dy * g
    dx = r * (dn - n * jnp.mean(dn * n, axis=-1, keepdims=True))
    return dx, dg


def _gelu_tanh(g):
    k = 0.7978845608028654
    kc = k * 0.044715
    g2 = g * g
    t = jnp.tanh(g * (k + kc * g2))
    h = 0.5 * t + 0.5
    dh = (0.5 - 0.5 * (t * t)) * (k + (3.0 * kc) * g2)
    return g * h, h + g * dh


def _residue_shape(S, d, dtype, width=ATTN_WIDTH):
    return jax.ShapeDtypeStruct((S // d, d * width), dtype)


def _residue_spec(TM, d, width=ATTN_WIDTH):
    return pl.BlockSpec((TM // d, d * width), lambda i: (i, 0))


def _token_scratch(TM, width=ATTN_WIDTH):
    return [pltpu.VMEM((TM, LANES), F32)] * (width // LANES)


def _head_stat_matrix(pick_first_lane):
    r = lax.broadcasted_iota(jnp.int32, (ATTN_WIDTH, STAT_WIDTH), 0)
    c = lax.broadcasted_iota(jnp.int32, (ATTN_WIDTH, STAT_WIDTH), 1) // STAT_LANES
    return ((r == c * HEAD_DIM) if pick_first_lane else (r // HEAD_DIM == c)).astype(BF16)


def _bf16_pieces(x, n):
    pieces = []
    for _ in range(n):
        p = x.astype(BF16)
        pieces.append(p)
        x = x - p.astype(F32)
    return pieces


def _put_tokens(dst_s, val):
    for cb, chunk in enumerate(dst_s):
        chunk[...] = val[:, cb * LANES:(cb + 1) * LANES]


def _get_tokens(src_s):
    return jnp.concatenate([chunk[...] for chunk in src_s], axis=1)


def _to_residue(val, src_s, out_ref, d, dtype):
    if d == 1:
        out_ref[...] = val.astype(dtype)
        return
    rows = src_s[0].shape[0]
    for r in range(d):
        for cb, chunk in enumerate(src_s):
            col = (r * len(src_s) + cb) * LANES
            out_ref[:, col:col + LANES] = chunk[pl.ds(r, rows // d, stride=d), :].astype(dtype)


def _from_residue(in_ref, dst_s, d):
    if d == 1:
        return in_ref[...].astype(F32)
    rows = dst_s[0].shape[0]
    for r in range(d):
        for cb, chunk in enumerate(dst_s):
            col = (r * len(dst_s) + cb) * LANES
            chunk[pl.ds(r, rows // d, stride=d), :] = in_ref[:, col:col + LANES].astype(F32)
    return _get_tokens(dst_s)


def _mix_in_fwd(x, g_pre, w_in, cargo=()):
    S, D = x.shape
    TM = 512
    nc = len(cargo)
    kinds = [kind for kind, _ in cargo]
    n_chunks = ATTN_WIDTH // LANES

    def body(x_ref, g_ref, w_ref, *refs):
        cargo_in, refs = refs[:nc], refs[nc:]
        qkv_refs, p_ref, h_ref = refs[:9], refs[9], refs[10]
        t_s = refs[11 + nc:11 + nc + n_chunks]
        cargo_refs = (kinds, cargo_in, refs[11:11 + nc], refs[11 + nc + n_chunks:])
        if nc:
            _cargo_start(*cargo_refs, pl.program_id(0) == 0)
        n, _ = _rms_stats(x_ref[...])
        hb = (n * g_ref[...]).astype(BF16)
        h_ref[...] = hb
        for a in range(3):
            res = _dot(hb, w_ref[a])
            if a == 0:
                res = res * ATTN_SCALE
            _put_tokens(t_s, res)
            for i, d in enumerate(DILATIONS):
                _to_residue(res, t_s, qkv_refs[3 * i + a], d, BF16)
        p_ref[...] = _dot(hb, w_ref[3])
        if nc:
            _cargo_finish(*cargo_refs, pl.program_id(0) == S // TM - 1)

    row = lambda w: pl.BlockSpec((TM, w), lambda i: (i, 0))
    arrays, cargo_specs, shapes, aliases, sems = _cargo_call(cargo, 3, 11)
    out = pl.pallas_call(
        body, name="mix_in_fwd", grid=(S // TM,),
        in_specs=[row(D), _const_spec((1, D)), _const_spec(w_in.shape)] + cargo_specs,
        out_specs=[_residue_spec(TM, d) for d in DILATIONS for _ in range(3)] + [row(POOL_WIDTH), row(D)] + cargo_specs,
        out_shape=[_residue_shape(S, d, BF16) for d in DILATIONS for _ in range(3)]
        + [jax.ShapeDtypeStruct((S, POOL_WIDTH), F32), jax.ShapeDtypeStruct((S, D), BF16)] + shapes,
        input_output_aliases=aliases,
        scratch_shapes=_token_scratch(TM) + sems,
        compiler_params=_params(("arbitrary",), VMEM_LIMIT),
    )(x, g_pre, w_in, *arrays)
    return [out[0:3], out[3:6], out[6:9]], out[9], out[10], out[11:]


def _band_mask(n):
    qi = lax.broadcasted_iota(jnp.int32, (ATTN_BLOCK, 2 * ATTN_BLOCK), 0)
    ki = lax.broadcasted_iota(jnp.int32, (ATTN_BLOCK, 2 * ATTN_BLOCK), 1)
    dist = qi + ATTN_BLOCK - ki
    return (dist >= 0) & (dist <= ATTN_BLOCK) & ((ki >= ATTN_BLOCK) | (n > 0))


def _first_head_lanes():
    return lax.broadcasted_iota(jnp.int32, (1, LANES), 1) < HEAD_DIM


def _stack_heads(pair, first):
    zero = jnp.zeros_like(pair)
    return jnp.concatenate([jnp.where(first, pair, zero), jnp.where(first, zero, pair)], axis=0)


def _unstack_heads(stacked, first):
    return jnp.where(first, stacked[:ATTN_BLOCK], stacked[ATTN_BLOCK:])


CARGO_COPIES = {"ici": 3, "d2d": 3, "scatter": 3, "swap": 1, "everyone": 7}
CARGO_IN_PLACE = ("ici", "d2d", "everyone")


def _cargo_copies(kinds, ins, outs, send_sems, recv_sems, want_recvs=True):
    x, y, c, chips = _position()
    s = 2 * x + y
    me = 2 * s + c
    sibling = (x, y, 1 - c)
    sends, recvs = [], []

    def add(k, src, dst, landing, to):
        sends.append(_remote(src, dst, send_sems.at[k], recv_sems.at[k], to))
        if want_recvs:
            recvs.append(_remote(landing, landing, send_sems.at[k], recv_sems.at[k], to))

    k0 = 0
    for a, kind in enumerate(kinds):
        if kind == "swap":
            rows = ins[a].shape[1] // 2
            add(k0, ins[a].at[:, pl.ds((1 - c) * rows, rows), :], outs[a], outs[a], sibling)
        elif kind == "everyone":
            for m in range(1, 8):
                peer = (x ^ (m >> 2), y ^ ((m >> 1) & 1), c ^ (m & 1))
                add(k0 + m - 1, outs[a].at[me], outs[a].at[me], outs[a].at[4 * peer[0] + 2 * peer[1] + peer[2]], peer)
        else:
            for j, (px, py) in enumerate(chips):
                sj = 2 * px + py
                if kind == "scatter":
                    add(k0 + j, ins[a].at[sj], outs[a].at[j], outs[a].at[j], (px, py, c))
                    continue
                buf = outs[a]
                rows = buf.shape[1] // 2
                half = lambda shard, h: buf.at[shard, pl.ds(h * rows, rows), :]
                if kind == "ici":
                    add(k0 + j, half(s, c), half(s, c), half(sj, c), (px, py, c))
                else:
                    add(k0 + j, half(sj, c), half(sj, c), half(sj, 1 - c), sibling)
        k0 += CARGO_COPIES[kind]
    return sends, recvs


def _cargo_start(kinds, ins, outs, sems, first_step):
    @pl.when(first_step)
    def _():
        for cp in _cargo_copies(kinds, ins, outs, *sems, want_recvs=False)[0]:
            cp.start()


def _cargo_finish(kinds, ins, outs, sems, last_step):
    @pl.when(last_step)
    def _():
        sends, recvs = _cargo_copies(kinds, ins, outs, *sems)
        for cp in sends:
            cp.wait_send()
        for cp in recvs:
            cp.wait_recv()


def _cargo_call(cargo, n_in, n_out):
    arrays = [a for _, a in cargo]
    shapes = []
    for kind, a in cargo:
        shape = {"scatter": (3,) + a.shape[1:], "swap": (a.shape[0], a.shape[1] // 2, a.shape[2])}.get(kind, a.shape)
        shapes.append(jax.ShapeDtypeStruct(shape, a.dtype))
    aliases = {n_in + i: n_out + i for i, (kind, _) in enumerate(cargo) if kind in CARGO_IN_PLACE}
    n_sems = sum(CARGO_COPIES[kind] for kind, _ in cargo)
    sems = [pltpu.SemaphoreType.DMA((n_sems,))] * 2 if cargo else []
    return arrays, [ANY] * len(cargo), shapes, aliases, sems


def _attn_fwd(q, k, v, d, cargo=()):
    L = q.shape[0]
    group = min(d, RESIDUES_PER_STEP)
    width = group * ATTN_WIDTH
    qb = RESIDUES_PER_STEP // group
    steps = L // (qb * ATTN_BLOCK)
    nc = len(cargo)
    kinds = [kind for kind, _ in cargo]

    def body(*refs):
        q_ref, kp_ref, kc_ref, vp_ref, vc_ref = refs[:5]
        o_ref, lse_ref = refs[5 + nc:7 + nc]
        cargo_refs = (kinds, refs[5:5 + nc], refs[7 + nc:7 + 2 * nc], refs[7 + 2 * nc:])
        r, n = pl.program_id(0), pl.program_id(1)
        if nc:
            _cargo_start(*cargo_refs, (r == 0) & (n == 0))
        first = _first_head_lanes()
        for sub in range(qb):
            rows = slice(sub * ATTN_BLOCK, (sub + 1) * ATTN_BLOCK)
            valid = _band_mask(n if sub == 0 else 1)
            valid2 = jnp.concatenate([valid, valid], axis=0)
            for hp in range(width // LANES):
                sl = slice(hp * LANES, (hp + 1) * LANES)
                if sub == 0:
                    kk = jnp.concatenate([kp_ref[:, sl], kc_ref[rows, sl]], axis=0)
                    vv = jnp.concatenate([vp_ref[:, sl], vc_ref[rows, sl]], axis=0)
                else:
                    keys = slice((sub - 1) * ATTN_BLOCK, (sub + 1) * ATTN_BLOCK)
                    kk, vv = kc_ref[keys, sl], vc_ref[keys, sl]
                s = jnp.where(valid2, _dot_nt(_stack_heads(q_ref[rows, sl], first), kk), NEG_INF)
                m = jnp.max(s, axis=-1, keepdims=True)
                p = jnp.exp(s - m)
                den = jnp.sum(p, axis=-1, keepdims=True)
                o_ref[rows, sl] = _unstack_heads(_dot(p.astype(BF16), vv) / den, first)
                lse_ref[rows, sl] = _unstack_heads(m + jnp.log(den), first)
        if nc:
            _cargo_finish(*cargo_refs, (r == d // group - 1) & (n == steps - 1))

    cur = pl.BlockSpec((qb * ATTN_BLOCK, width), lambda r, n: (n, r))
    prev = pl.BlockSpec((ATTN_BLOCK, width), lambda r, n: (jnp.maximum(n * qb - 1, 0), r))
    arrays, specs, shapes, aliases, sems = _cargo_call(cargo, 5, 2)
    out = pl.pallas_call(
        body, name=f"attn_fwd_d{d}", grid=(d // group, steps),
        in_specs=[cur, prev, cur, prev, cur] + specs,
        out_specs=[cur, cur] + specs,
        out_shape=[jax.ShapeDtypeStruct((L, d * ATTN_WIDTH), F32)] * 2 + shapes,
        input_output_aliases=aliases, scratch_shapes=sems,
        compiler_params=_params(("arbitrary", "arbitrary")),
    )(q, k, k, v, v, *arrays)
    return out[0], out[1], out[2:]


def _attn_mix(outs, lses):
    S = outs[0].shape[0]
    TM = 512
    n = len(DILATIONS)

    def body(*refs):
        o_refs, l_refs, attn_ref, lse_refs = refs[:n], refs[n:2 * n], refs[2 * n], refs[2 * n + 1:3 * n + 1]
        t_s, c_s = refs[3 * n + 1:-1], refs[-1:]
        os = [_from_residue(o_refs[i], t_s, d) for i, d in enumerate(DILATIONS)]
        ls = [_from_residue(l_refs[i], t_s, d) for i, d in enumerate(DILATIONS)]
        m = jnp.maximum(jnp.maximum(ls[0], ls[1]), ls[2])
        es = [jnp.exp(l - m) for l in ls]
        den = es[0] + es[1] + es[2]
        attn_ref[...] = (es[0] * os[0] + es[1] * os[1] + es[2] * os[2]) / den
        pick = _head_stat_matrix(pick_first_lane=True)
        lse = sum(_dot(p, pick) for p in _bf16_pieces(m + jnp.log(den), 3))
        _put_tokens(c_s, lse)
        for i, d in enumerate(DILATIONS):
            _to_residue(lse, c_s, lse_refs[i], d, F32)

    specs = [_residue_spec(TM, d) for d in DILATIONS]
    out = pl.pallas_call(
        body, name="attn_mix", grid=(S // TM,),
        in_specs=specs * 2, out_specs=[specs[0]] + [_residue_spec(TM, d, STAT_WIDTH) for d in DILATIONS],
        out_shape=[jax.ShapeDtypeStruct((S, ATTN_WIDTH), F32)]
        + [_residue_shape(S, d, F32, STAT_WIDTH) for d in DILATIONS],
        scratch_shapes=_token_scratch(TM) + _token_scratch(TM, STAT_WIDTH),
        compiler_params=_params(("parallel",)),
    )(*outs, *lses)
    return out[0], out[1:]


def _pool_counts(first_row, rows, w):
    t = first_row + lax.broadcasted_iota(jnp.int32, (rows, 1), 0)
    return jnp.minimum(t + 1, w).astype(F32)


def _trailing_sums(xe, w):
    s, k = xe, 1
    while k < w:
        s = s + pltpu.roll(s, k, 0)
        k *= 2
    return s


def _leading_sums(xe, w):
    rows = xe.shape[0]
    s, k = xe, 1
    while k < w:
        s = s + pltpu.roll(s, rows - k, 0)
        k *= 2
    return s


def _pooled_groups(halo, cur, first_row):
    TM = cur.shape[0]
    xe = jnp.concatenate([halo, cur], axis=0)
    out = []
    for g, w in enumerate(POOL_WINDOWS):
        a = xe[:, g * POOL_GROUP:(g + 1) * POOL_GROUP]
        s = _trailing_sums(a, w)[POOL_HALO:]
        out.append(s / _pool_counts(first_row, TM, w) - a[POOL_HALO:])
    return out


def _pool_fwd(pool_in, pool_w, pool_scale):
    S = pool_in.shape[0]
    TM = 512
    HB = TM // POOL_HALO

    def body(cur_ref, halo_ref, w_ref, sc_ref, y_ref):
        i = pl.program_id(0)
        halo = jnp.where(i > 0, halo_ref[...], 0.0)
        pooled = _pooled_groups(halo, cur_ref[...], i * TM)
        for g in range(len(POOL_WINDOWS)):
            sl = slice(g * POOL_GROUP, (g + 1) * POOL_GROUP)
            y = _dot(pooled[g].astype(BF16), w_ref[g].astype(BF16)) * sc_ref[:, sl]
            y_ref[:, sl] = y.astype(BF16)

    return pl.pallas_call(
        body, name="pool_fwd", grid=(S // TM,),
        in_specs=[pl.BlockSpec((TM, POOL_WIDTH), lambda i: (i, 0)),
                  pl.BlockSpec((POOL_HALO, POOL_WIDTH), lambda i: (jnp.maximum(i * HB - 1, 0), 0)),
                  _const_spec(pool_w.shape), _const_spec((1, POOL_WIDTH))],
        out_specs=pl.BlockSpec((TM, POOL_WIDTH), lambda i: (i, 0)),
        out_shape=jax.ShapeDtypeStruct((S, POOL_WIDTH), BF16),
        compiler_params=_params(("parallel",)),
    )(pool_in, pool_in, pool_w, pool_scale)


def _mix_out_fwd(attn, pool, w_out, x, g_post, g_ffn_pre):
    S, D = x.shape
    TM = 512

    def body(a_ref, p_ref, w_ref, x_ref, gp_ref, gf_ref, mixed_ref, x1_ref, h2_ref, cat_ref):
        ab = a_ref[...].astype(BF16)
        cat_ref[:, :ATTN_WIDTH] = ab
        cat_ref[:, ATTN_WIDTH:] = p_ref[...]
        mixed = _dot(ab, w_ref[:ATTN_WIDTH, :]) + _dot(p_ref[...], w_ref[ATTN_WIDTH:, :])
        mixed_ref[...] = mixed
        n, _ = _rms_stats(mixed)
        x1 = x_ref[...] + n * gp_ref[...]
        x1_ref[...] = x1
        n2, _ = _rms_stats(x1)
        h2_ref[...] = (n2 * gf_ref[...]).astype(BF16)

    row = lambda w: pl.BlockSpec((TM, w), lambda i: (i, 0))
    return pl.pallas_call(
        body, name="mix_out_fwd", grid=(S // TM,),
        in_specs=[row(ATTN_WIDTH), row(POOL_WIDTH), _const_spec(w_out.shape), row(D),
                  _const_spec((1, D)), _const_spec((1, D))],
        out_specs=[row(D), row(D), row(D), row(D)],
        out_shape=[jax.ShapeDtypeStruct((S, D), F32), jax.ShapeDtypeStruct((S, D), F32),
                   jax.ShapeDtypeStruct((S, D), BF16), jax.ShapeDtypeStruct((S, D), BF16)],
        compiler_params=_params(("parallel",), VMEM_LIMIT),
    )(attn, pool, w_out, x, g_post, g_ffn_pre)


def _ffn_fwd(h2, x1, target, w_up, w_down, conv_w, conv_b, g_post):
    S, D = x1.shape
    CW = w_up.shape[2]
    FF = 2 * CW
    TM = 256
    piece = 4 * LANES
    pieces = [(lo, min(lo + piece, CW)) for lo in range(0, CW, piece)]

    def body(h2_ref, x1_ref, t_ref, wu_ref, wd_ref, cw_ref, cb_ref, g_ref,
             yv_ref, dy_ref, df_ref, dc_ref, loss_ref, dg_ref, dcb_ref, dcw_ref,
             ue_s, dgate_s, dval_s):
        i = pl.program_id(0)

        @pl.when(i == 0)
        def _():
            loss_ref[...] = jnp.zeros_like(loss_ref)
            dg_ref[...] = jnp.zeros_like(dg_ref)
            dcb_ref[...] = jnp.zeros_like(dcb_ref)
            dcw_ref[...] = jnp.zeros_like(dcw_ref)
            ue_s[0:CONV_HALO, :] = jnp.zeros((CONV_HALO, 2 * FF), F32)

        @pl.when(i > 0)
        def _():
            ue_s[0:CONV_HALO, :] = ue_s[TM:TM + CONV_HALO, :]

        def shifted(cols, k):
            return pltpu.roll(ue_s[:, cols], k, 0)[CONV_HALO:]

        def conv(cols):
            return (cb_ref[:, cols] + cw_ref[2, :, cols] * ue_s[CONV_HALO:, cols]
                    + cw_ref[1, :, cols] * shifted(cols, 1) + cw_ref[0, :, cols] * shifted(cols, 2))

        hb = h2_ref[...]
        f = jnp.zeros((TM, D), F32)
        for j in range(2):
            jc = slice(j * CW, (j + 1) * CW)
            for half in range(2):
                blk = 2 * half + j
                cols = slice(blk * CW, (blk + 1) * CW)
                ue_s[CONV_HALO:, cols] = _dot(hb, wu_ref[blk])
            for lo, hi in pieces:
                pc = slice(j * CW + lo, j * CW + hi)
                gelu, dgelu = _gelu_tanh(conv(pc))
                val = conv(slice(FF + j * CW + lo, FF + j * CW + hi))
                dgate_s[:, pc] = val * dgelu
                dval_s[:, pc] = gelu
                yv_ref[:, pc] = (gelu * val).astype(BF16)
            f = f + _dot(yv_ref[:, jc], wd_ref[jc, :])

        n, r = _rms_stats(f)
        err = x1_ref[...] + n * g_ref[...] - t_ref[...]
        loss_ref[...] += 0.5 * jnp.sum(jnp.mean(err * err, axis=-1, keepdims=True), axis=0, keepdims=True)
        dy = err / D
        dy_ref[...] = dy
        df, dg = _rms_bwd(dy, n, r, g_ref[...])
        dg_ref[...] += dg
        dfb = df.astype(BF16)
        df_ref[...] = dfb

        for j in range(2):
            jc = slice(j * CW, (j + 1) * CW)
            dyv = _dot_nt(dfb, wd_ref[jc, :])
            for lo, hi in pieces:
                pc = slice(j * CW + lo, j * CW + hi)
                for half, scale_s in ((0, dgate_s), (1, dval_s)):
                    cols = slice(half * FF + j * CW + lo, half * FF + j * CW + hi)
                    dcv = dyv[:, lo:hi] * scale_s[:, pc]
                    dc_ref[:, cols] = dcv.astype(BF16)
                    dcb_ref[:, cols] += jnp.sum(dcv, axis=0, keepdims=True)
                    dcw_ref[2, :, cols] += jnp.sum(dcv * ue_s[CONV_HALO:, cols], axis=0, keepdims=True)
                    dcw_ref[1, :, cols] += jnp.sum(dcv * shifted(cols, 1), axis=0, keepdims=True)
                    dcw_ref[0, :, cols] += jnp.sum(dcv * shifted(cols, 2), axis=0, keepdims=True)

    row = lambda w: pl.BlockSpec((TM, w), lambda i: (i, 0))
    acc = lambda shape: pl.BlockSpec(shape, lambda i: (0,) * len(shape))
    return pl.pallas_call(
        body, name="ffn_fwd", grid=(S // TM,),
        in_specs=[row(D), row(D), row(D), _const_spec(w_up.shape), _const_spec(w_down.shape),
                  _const_spec(conv_w.shape), _const_spec((1, 2 * FF)), _const_spec((1, D))],
        out_specs=[row(FF), row(D), row(D), row(2 * FF),
                   acc((1, 1)), acc((1, D)), acc((1, 2 * FF)), acc((CONV_WIDTH, 1, 2 * FF))],
        out_shape=[jax.ShapeDtypeStruct((S, FF), BF16),
                   jax.ShapeDtypeStruct((S, D), F32), jax.ShapeDtypeStruct((S, D), BF16),
                   jax.ShapeDtypeStruct((S, 2 * FF), BF16),
                   jax.ShapeDtypeStruct((1, 1), F32), jax.ShapeDtypeStruct((1, D), F32),
                   jax.ShapeDtypeStruct((1, 2 * FF), F32), jax.ShapeDtypeStruct((CONV_WIDTH, 1, 2 * FF), F32)],
        scratch_shapes=[pltpu.VMEM((TM + CONV_HALO, 2 * FF), F32), pltpu.VMEM((TM, FF), F32),
                        pltpu.VMEM((TM, FF), F32)],
        compiler_params=_params(("arbitrary",), VMEM_LIMIT),
    )(h2, x1, target, w_up, w_down, conv_w, conv_b, g_post)


def _ffn_bwd(dc, conv_w, w_up, x1, g_ffn_pre, dy):
    S, D = x1.shape
    CW = w_up.shape[2]
    F2 = 4 * CW
    TM = 256
    HB = TM // CONV_HALO
    last = S // CONV_HALO - 1
    n_tiles = S // TM

    def body(dc_ref, halo_ref, cw_ref, wu_ref, x1_ref, g_ref, dy_ref, du_ref, dx1_ref, dg_ref):
        i = pl.program_id(0)

        @pl.when(i == 0)
        def _():
            dg_ref[...] = jnp.zeros_like(dg_ref)

        keep = i < n_tiles - 1
        dh2 = jnp.zeros((TM, D), F32)
        for blk in range(N_SHARD):
            cols = slice(blk * CW, (blk + 1) * CW)
            halo = jnp.where(keep, halo_ref[:, cols].astype(F32), 0.0)
            dce = jnp.concatenate([dc_ref[:, cols].astype(F32), halo], axis=0)
            rows = TM + CONV_HALO
            du = (cw_ref[2, :, cols] * dce[:TM]
                  + cw_ref[1, :, cols] * pltpu.roll(dce, rows - 1, 0)[:TM]
                  + cw_ref[0, :, cols] * pltpu.roll(dce, rows - 2, 0)[:TM])
            dub = du.astype(BF16)
            du_ref[:, cols] = dub
            dh2 = dh2 + _dot_nt(dub, wu_ref[blk])
        n2, r2 = _rms_stats(x1_ref[...])
        dx, dg = _rms_bwd(dh2, n2, r2, g_ref[...])
        dg_ref[...] += dg
        dx1_ref[...] = dy_ref[...] + dx

    row = lambda w: pl.BlockSpec((TM, w), lambda i: (i, 0))
    return pl.pallas_call(
        body, name="ffn_bwd", grid=(S // TM,),
        in_specs=[row(F2), pl.BlockSpec((CONV_HALO, F2), lambda i: (jnp.minimum((i + 1) * HB, last), 0)),
                  _const_spec(conv_w.shape), _const_spec(w_up.shape), row(D), _const_spec((1, D)), row(D)],
        out_specs=[row(F2), row(D), pl.BlockSpec((1, D), lambda i: (0, 0))],
        out_shape=[jax.ShapeDtypeStruct((S, F2), BF16), jax.ShapeDtypeStruct((S, D), F32),
                   jax.ShapeDtypeStruct((1, D), F32)],
        compiler_params=_params(("arbitrary",), VMEM_LIMIT),
    )(dc, dc, conv_w, w_up, x1, g_ffn_pre, dy)


def _matmul_tn(a, b, n_blocks, name):
    S, M = a.shape
    N = b.shape[1]
    tn = N // n_blocks
    tm = M if M <= 1024 else M // 2
    tk = 2048
    nk = S // tk

    def body(a_ref, b_ref, o_ref):
        @pl.when(pl.program_id(2) == 0)
        def _():
            o_ref[...] = jnp.zeros_like(o_ref)
        o_ref[0] += _dot_tn(a_ref[...], b_ref[...])

    return pl.pallas_call(
        body, name=name, grid=(M // tm, n_blocks, nk),
        in_specs=[pl.BlockSpec((tk, tm), lambda i, j, k: (k, i)), pl.BlockSpec((tk, tn), lambda i, j, k: (k, j))],
        out_specs=pl.BlockSpec((1, tm, tn), lambda i, j, k: (j, i, 0)),
        out_shape=jax.ShapeDtypeStruct((n_blocks, M, tn), F32),
        compiler_params=_params(("parallel", "parallel", "arbitrary"), VMEM_LIMIT),
    )(a, b)


def _mix_out_bwd(dx1, mixed, g_post, w_out, attn, cargo=()):
    S, D = dx1.shape
    TM = 512
    nd = len(DILATIONS)
    nc = len(cargo)
    kinds = [kind for kind, _ in cargo]
    n_chunks = ATTN_WIDTH // LANES

    def body(*refs):
        dx_ref, m_ref, g_ref, w_ref, a_ref = refs[:5]
        dm_ref, dp_ref, dg_ref = refs[5 + nc:8 + nc]
        da_refs, dl_refs = refs[8 + nc:8 + nc + nd], refs[8 + nc + nd:8 + nc + 2 * nd]
        n_out = 8 + nc + 2 * nd
        t_s = refs[n_out + nc:n_out + nc + n_chunks]
        c_s = refs[n_out + nc + n_chunks:n_out + nc + n_chunks + 1]
        cargo_refs = (kinds, refs[5:5 + nc], refs[n_out:n_out + nc], refs[n_out + nc + n_chunks + 1:])
        if nc:
            _cargo_start(*cargo_refs, pl.program_id(0) == 0)

        @pl.when(pl.program_id(0) == 0)
        def _():
            dg_ref[...] = jnp.zeros_like(dg_ref)

        n, r = _rms_stats(m_ref[...])
        dm, dg = _rms_bwd(dx_ref[...], n, r, g_ref[...])
        dg_ref[...] += dg
        dmb = dm.astype(BF16)
        dm_ref[...] = dmb
        da = _dot_nt(dmb, w_ref[:ATTN_WIDTH, :])
        _put_tokens(t_s, da)
        for i, d in enumerate(DILATIONS):
            _to_residue(da, t_s, da_refs[i], d, BF16)
        dp_ref[...] = _dot_nt(dmb, w_ref[ATTN_WIDTH:, :])
        gather = _head_stat_matrix(pick_first_lane=False)
        delta = sum(_dot(p, gather) for p in _bf16_pieces(da * a_ref[...], 2))
        _put_tokens(c_s, delta)
        for i, d in enumerate(DILATIONS):
            _to_residue(delta, c_s, dl_refs[i], d, F32)
        if nc:
            _cargo_finish(*cargo_refs, pl.program_id(0) == S // TM - 1)

    row = lambda w: pl.BlockSpec((TM, w), lambda i: (i, 0))
    specs = [_residue_spec(TM, d) for d in DILATIONS]
    arrays, cargo_specs, shapes, aliases, sems = _cargo_call(cargo, 5, 3 + 2 * nd)
    out = pl.pallas_call(
        body, name="mix_out_bwd", grid=(S // TM,),
        in_specs=[row(D), row(D), _const_spec((1, D)), _const_spec(w_out.shape), row(ATTN_WIDTH)] + cargo_specs,
        out_specs=[row(D), row(POOL_WIDTH), pl.BlockSpec((1, D), lambda i: (0, 0))] + specs
        + [_residue_spec(TM, d, STAT_WIDTH) for d in DILATIONS] + cargo_specs,
        out_shape=[jax.ShapeDtypeStruct((S, D), BF16), jax.ShapeDtypeStruct((S, POOL_WIDTH), F32),
                   jax.ShapeDtypeStruct((1, D), F32)]
        + [_residue_shape(S, d, BF16) for d in DILATIONS]
        + [_residue_shape(S, d, F32, STAT_WIDTH) for d in DILATIONS] + shapes,
        input_output_aliases=aliases,
        scratch_shapes=_token_scratch(TM) + _token_scratch(TM, STAT_WIDTH) + sems,
        compiler_params=_params(("arbitrary",), VMEM_LIMIT),
    )(dx1, mixed, g_post, w_out, attn, *arrays)
    return out[0], out[1], out[2], out[3:3 + nd], out[3 + nd:3 + 2 * nd], out[3 + 2 * nd:]


def _pool_bwd(pool_in, d_pool, pool_w, pool_scale):
    S = pool_in.shape[0]
    TM = 512
    HB = TM // POOL_HALO
    last = S // POOL_HALO - 1
    G = len(POOL_WINDOWS)

    def body(cur_ref, halo_ref, dcur_ref, dnext_ref, w_ref, sc_ref, dxin_ref, dw_ref, dsc_ref):
        i = pl.program_id(0)

        @pl.when(i == 0)
        def _():
            dw_ref[...] = jnp.zeros_like(dw_ref)
            dsc_ref[...] = jnp.zeros_like(dsc_ref)

        halo = jnp.where(i > 0, halo_ref[...], 0.0)
        pooled = _pooled_groups(halo, cur_ref[...], i * TM)
        dnext = jnp.where(i < S // TM - 1, dnext_ref[...], 0.0)
        dye = jnp.concatenate([dcur_ref[...], dnext], axis=0)
        for g, w in enumerate(POOL_WINDOWS):
            sl = slice(g * POOL_GROUP, (g + 1) * POOL_GROUP)
            wg = w_ref[g].astype(BF16)
            pb = pooled[g].astype(BF16)
            dsc_ref[:, sl] += jnp.sum(dye[:TM, sl] * _dot(pb, wg), axis=0, keepdims=True)
            dpre = (dye[:, sl] * sc_ref[:, sl]).astype(BF16)
            dw_ref[g] += _dot_tn(pb, dpre[:TM])
            dpooled = _dot_nt(dpre, wg)
            z = dpooled / _pool_counts(i * TM, TM + POOL_HALO, w)
            dxin_ref[:, sl] = (_leading_sums(z, w)[:TM] - dpooled[:TM]).astype(BF16)

    row = pl.BlockSpec((TM, POOL_WIDTH), lambda i: (i, 0))
    return pl.pallas_call(
        body, name="pool_bwd", grid=(S // TM,),
        in_specs=[row, pl.BlockSpec((POOL_HALO, POOL_WIDTH), lambda i: (jnp.maximum(i * HB - 1, 0), 0)),
                  row, pl.BlockSpec((POOL_HALO, POOL_WIDTH), lambda i: (jnp.minimum((i + 1) * HB, last), 0)),
                  _const_spec(pool_w.shape), _const_spec((1, POOL_WIDTH))],
        out_specs=[row, pl.BlockSpec((G, POOL_GROUP, POOL_GROUP), lambda i: (0, 0, 0)),
                   pl.BlockSpec((1, POOL_WIDTH), lambda i: (0, 0))],
        out_shape=[jax.ShapeDtypeStruct((S, POOL_WIDTH), BF16), jax.ShapeDtypeStruct((G, POOL_GROUP, POOL_GROUP), F32),
                   jax.ShapeDtypeStruct((1, POOL_WIDTH), F32)],
        compiler_params=_params(("arbitrary",)),
    )(pool_in, pool_in, d_pool, d_pool, pool_w, pool_scale)


def _attn_bwd(q, k, v, d_attn, lse, delta, d, cargo=()):
    L = q.shape[0]
    nb = L // ATTN_BLOCK
    group = min(d, RESIDUES_PER_STEP)
    width = group * ATTN_WIDTH
    nc = len(cargo)
    kinds = [kind for kind, _ in cargo]

    def body(*refs):
        q_ref, kp_ref, kc_ref, vp_ref, vc_ref, do_ref, lse_ref, dl_ref = refs[:8]
        dq_ref, dk_ref, dv_ref = refs[8 + nc:11 + nc]
        ck_s, cv_s = refs[11 + 2 * nc:13 + 2 * nc]
        cargo_refs = (kinds, refs[8:8 + nc], refs[11 + nc:11 + 2 * nc], refs[13 + 2 * nc:])
        r, n = pl.program_id(0), pl.program_id(1)
        if nc:
            _cargo_start(*cargo_refs, (r == 0) & (n == 0))

        @pl.when(n == 0)
        def _():
            ck_s[...] = jnp.zeros_like(ck_s)
            cv_s[...] = jnp.zeros_like(cv_s)

        @pl.when(n < nb)
        def _():
            valid = _band_mask(n)
            valid2 = jnp.concatenate([valid, valid], axis=0)
            first = _first_head_lanes()

            def stacked_column(ref, hp):
                lane = 2 * hp * STAT_LANES
                return jnp.concatenate([ref[:, lane:lane + 1], ref[:, lane + STAT_LANES:lane + STAT_LANES + 1]], axis=0)

            for hp in range(width // LANES):
                sl = slice(hp * LANES, (hp + 1) * LANES)
                qq = _stack_heads(q_ref[:, sl], first)
                dd = _stack_heads(do_ref[:, sl], first)
                kk = jnp.concatenate([kp_ref[:, sl], kc_ref[:, sl]], axis=0)
                vv = jnp.concatenate([vp_ref[:, sl], vc_ref[:, sl]], axis=0)
                s = _dot_nt(qq, kk)
                p = jnp.where(valid2, jnp.exp(s - stacked_column(lse_ref, hp)), 0.0)
                dp = _dot_nt(dd, vv)
                ds = (p * (dp - stacked_column(dl_ref, hp))).astype(BF16)
                dq_ref[:, sl] = (_unstack_heads(_dot(ds, kk), first) * ATTN_SCALE).astype(BF16)
                dk = _dot_tn(ds, qq)
                dv = _dot_tn(p.astype(BF16), dd)
                dk_ref[:, sl] = (ck_s[:, sl] + dk[:ATTN_BLOCK]).astype(BF16)
                dv_ref[:, sl] = (cv_s[:, sl] + dv[:ATTN_BLOCK]).astype(BF16)
                ck_s[:, sl] = dk[ATTN_BLOCK:]
                cv_s[:, sl] = dv[ATTN_BLOCK:]

        @pl.when(n == nb)
        def _():
            dk_ref[...] = ck_s[...].astype(BF16)
            dv_ref[...] = cv_s[...].astype(BF16)

        if nc:
            _cargo_finish(*cargo_refs, (r == d // group - 1) & (n == nb))

    blk = (ATTN_BLOCK, width)
    cur = pl.BlockSpec(blk, lambda r, n: (jnp.minimum(n, nb - 1), r))
    stat = pl.BlockSpec((ATTN_BLOCK, group * STAT_WIDTH), lambda r, n: (jnp.minimum(n, nb - 1), r))
    prev = pl.BlockSpec(blk, lambda r, n: (jnp.maximum(jnp.minimum(n, nb - 1) - 1, 0), r))
    done = pl.BlockSpec(blk, lambda r, n: (jnp.maximum(n - 1, 0), r))
    arrays, specs, shapes, aliases, sems = _cargo_call(cargo, 8, 3)
    out = pl.pallas_call(
        body, name=f"attn_bwd_d{d}", grid=(d // group, nb + 1),
        in_specs=[cur, prev, cur, prev, cur, cur, stat, stat] + specs, out_specs=[cur, done, done] + specs,
        out_shape=[jax.ShapeDtypeStruct((L, d * ATTN_WIDTH), BF16)] * 3 + shapes,
        input_output_aliases=aliases,
        scratch_shapes=[pltpu.VMEM(blk, F32), pltpu.VMEM(blk, F32)] + sems,
        compiler_params=_params(("arbitrary", "arbitrary")),
    )(q, k, k, v, v, d_attn, lse, delta, *arrays)
    return out[:3], out[3:]


def _attn_bwd_consecutive(q, k, v, d_attn, lse, delta, cargo=()):
    L = q.shape[0]
    qb = CONSECUTIVE_BLOCKS
    steps = L // (qb * ATTN_BLOCK)
    nc = len(cargo)
    kinds = [kind for kind, _ in cargo]

    def body(*refs):
        q_ref, kp_ref, kc_ref, vp_ref, vc_ref, do_ref, lse_ref, dl_ref = refs[:8]
        dq_ref, dk_ref, dv_ref, ek_ref, ev_ref = refs[8 + nc:13 + nc]
        cargo_refs = (kinds, refs[8:8 + nc], refs[13 + nc:13 + 2 * nc], refs[13 + 2 * nc:])
        n = pl.program_id(0)
        if nc:
            _cargo_start(*cargo_refs, n == 0)
        first = _first_head_lanes()
        for hp in range(ATTN_WIDTH // LANES):
            sl = slice(hp * LANES, (hp + 1) * LANES)
            for sub in range(qb):
                rows = slice(sub * ATTN_BLOCK, (sub + 1) * ATTN_BLOCK)
                valid = _band_mask(n if sub == 0 else 1)
                valid2 = jnp.concatenate([valid, valid], axis=0)
                if sub == 0:
                    kk = jnp.concatenate([kp_ref[:, sl], kc_ref[rows, sl]], axis=0)
                    vv = jnp.concatenate([vp_ref[:, sl], vc_ref[rows, sl]], axis=0)
                else:
                    keys = slice((sub - 1) * ATTN_BLOCK, (sub + 1) * ATTN_BLOCK)
                    kk, vv = kc_ref[keys, sl], vc_ref[keys, sl]
                qq = _stack_heads(q_ref[rows, sl], first)
                dd = _stack_heads(do_ref[rows, sl], first)
                lane = 2 * hp * STAT_LANES
                column = lambda ref: jnp.concatenate(
                    [ref[rows, lane:lane + 1], ref[rows, lane + STAT_LANES:lane + STAT_LANES + 1]], axis=0)
                p = jnp.where(valid2, jnp.exp(_dot_nt(qq, kk) - column(lse_ref)), 0.0)
                ds = (p * (_dot_nt(dd, vv) - column(dl_ref))).astype(BF16)
                dq_ref[rows, sl] = (_unstack_heads(_dot(ds, kk), first) * ATTN_SCALE).astype(BF16)
                dk = _dot_tn(ds, qq)
                dv = _dot_tn(p.astype(BF16), dd)
                if sub == 0:
                    ek_ref[:, sl] = dk[:ATTN_BLOCK].astype(BF16)
                    ev_ref[:, sl] = dv[:ATTN_BLOCK].astype(BF16)
                else:
                    before = slice((sub - 1) * ATTN_BLOCK, sub * ATTN_BLOCK)
                    dk_ref[before, sl] = (carry_k + dk[:ATTN_BLOCK]).astype(BF16)
                    dv_ref[before, sl] = (carry_v + dv[:ATTN_BLOCK]).astype(BF16)
                carry_k, carry_v = dk[ATTN_BLOCK:], dv[ATTN_BLOCK:]
            dk_ref[rows, sl] = carry_k.astype(BF16)
            dv_ref[rows, sl] = carry_v.astype(BF16)
        if nc:
            _cargo_finish(*cargo_refs, n == steps - 1)

    cur = pl.BlockSpec((qb * ATTN_BLOCK, ATTN_WIDTH), lambda n: (n, 0))
    prev = pl.BlockSpec((ATTN_BLOCK, ATTN_WIDTH), lambda n: (jnp.maximum(n * qb - 1, 0), 0))
    edge = pl.BlockSpec((ATTN_BLOCK, ATTN_WIDTH), lambda n: (n, 0))
    stat = pl.BlockSpec((qb * ATTN_BLOCK, STAT_WIDTH), lambda n: (n, 0))
    arrays, specs, shapes, aliases, sems = _cargo_call(cargo, 8, 5)
    out = pl.pallas_call(
        body, name="attn_bwd_d1", grid=(steps,),
        in_specs=[cur, prev, cur, prev, cur, cur, stat, stat] + specs, out_specs=[cur, cur, cur, edge, edge] + specs,
        out_shape=[jax.ShapeDtypeStruct((L, ATTN_WIDTH), BF16)] * 3
        + [jax.ShapeDtypeStruct((steps * ATTN_BLOCK, ATTN_WIDTH), BF16)] * 2 + shapes,
        input_output_aliases=aliases, scratch_shapes=sems,
        compiler_params=_params(("arbitrary",)),
    )(q, k, k, v, v, d_attn, lse, delta, *arrays)
    return out[:3], out[3:5], out[5:]


def _mix_in_bwd(dqkv, edges, d_pool_in, w_in, x, g_pre, dx1):
    S, D = x.shape
    TM = CONSECUTIVE_BLOCKS * ATTN_BLOCK
    nd = len(DILATIONS)
    n_tiles = S // TM

    def body(*refs):
        g_refs = refs[:3 * nd]
        e_refs = (None,) + refs[3 * nd:3 * nd + 2]
        dpi_ref, w_ref, x_ref, g_ref, dx1_ref, dproj_ref, gx_ref, dg_ref = refs[3 * nd + 2:3 * nd + 10]
        t_s = refs[3 * nd + 10:]

        @pl.when(pl.program_id(0) == 0)
        def _():
            dg_ref[...] = jnp.zeros_like(dg_ref)

        dh = jnp.zeros((TM, D), F32)
        for a in range(4):
            if a < 3:
                tot = g_refs[a][...].astype(F32)
                if a > 0:
                    late = jnp.where(pl.program_id(0) < n_tiles - 1, e_refs[a][...].astype(F32), 0.0)
                    tot = jnp.concatenate([tot[:TM - ATTN_BLOCK], tot[TM - ATTN_BLOCK:] + late], axis=0)
                for i, d in enumerate(DILATIONS[1:]):
                    tot = tot + _from_residue(g_refs[3 * (i + 1) + a], t_s, d)
                db = tot.astype(BF16)
            else:
                db = dpi_ref[...]
            dproj_ref[:, a * ATTN_WIDTH:(a + 1) * ATTN_WIDTH] = db
            dh = dh + _dot_nt(db, w_ref[a])
        n, r = _rms_stats(x_ref[...])
        dx, dg = _rms_bwd(dh, n, r, g_ref[...])
        dg_ref[...] += dg
        gx_ref[...] = dx1_ref[...] + dx

    row = lambda w: pl.BlockSpec((TM, w), lambda i: (i, 0))
    edge = pl.BlockSpec((ATTN_BLOCK, ATTN_WIDTH), lambda i: (jnp.minimum(i + 1, n_tiles - 1), 0))
    return pl.pallas_call(
        body, name="mix_in_bwd", grid=(S // TM,),
        in_specs=[_residue_spec(TM, d) for d in DILATIONS for _ in range(3)] + [edge, edge]
        + [row(POOL_WIDTH), _const_spec(w_in.shape), row(D), _const_spec((1, D)), row(D)],
        out_specs=[row(4 * ATTN_WIDTH), row(D), pl.BlockSpec((1, D), lambda i: (0, 0))],
        out_shape=[jax.ShapeDtypeStruct((S, 4 * ATTN_WIDTH), BF16), jax.ShapeDtypeStruct((S, D), F32),
                   jax.ShapeDtypeStruct((1, D), F32)],
        scratch_shapes=_token_scratch(TM),
        compiler_params=_params(("arbitrary",), VMEM_LIMIT),
    )(*[g for gs in dqkv for g in gs], *edges, d_pool_in, w_in, x, g_pre, dx1)


SMALL_EARLY = ("pool_w", "pool_scale", "g_mix_post", "g_ffn_pre", "conv_b", "g_ffn_post", "conv_w")
SMALL_LATE = ("g_mix_pre",)


def _pack_small(grads, names):
    parts = []
    for n in names:
        g = grads[n]
        if n == "conv_w":
            g = g.reshape(CONV_WIDTH, N_SHARD, -1).transpose(1, 0, 2)
        parts.append(g.reshape(-1, LANES))
    return jnp.concatenate(parts, axis=0) if len(parts) > 1 else parts[0]


def _unpack_small(packed, names, like, shard):
    out, row = {}, 0
    for n in names:
        size = like[n].size * (N_SHARD if n == "conv_w" else 1)
        g = packed[row:row + size // LANES]
        row += size // LANES
        if n == "conv_w":
            g = lax.dynamic_slice_in_dim(g.reshape((N_SHARD,) + like[n].shape), shard, 1, axis=0)[0]
        out[n] = g.reshape(like[n].shape)
    return out


def _local_step(x, target, g_mix_pre, w_in, pool_w, pool_scale, w_out, g_mix_post, g_ffn_pre,
                w_up, conv_w, conv_b, w_down, g_ffn_post, mesh_pos=None):
    on_mesh = mesh_pos is not None
    D = x.shape[1]
    CW = w_up.shape[2]
    qkv, pool_in, h1, got = _mix_in_fwd(x, g_mix_pre, w_in, [("ici", w_up)] if on_mesh else ())
    w_up = got[0] if on_mesh else w_up
    o1, l1, got = _attn_fwd(*qkv[0], 1, [("d2d", w_up), ("ici", w_out), ("ici", w_down)] if on_mesh else ())
    w_up, w_out, w_down = got if on_mesh else (w_up, w_out, w_down)
    o4, l4, got = _attn_fwd(*qkv[1], 4, [("d2d", w_out), ("d2d", w_down)] if on_mesh else ())
    w_out, w_down = got if on_mesh else (w_out, w_down)
    o16, l16, _ = _attn_fwd(*qkv[2], 16)
    w_out = w_out.reshape(D, D)
    w_down = w_down.reshape(2 * CW, D)
    attn, lse = _attn_mix((o1, o4, o16), (l1, l4, l16))
    pool = _pool_fwd(pool_in, pool_w, pool_scale)
    mixed, x1, h2, cat = _mix_out_fwd(attn, pool, w_out, x, g_mix_post, g_ffn_pre)

    yv, dy, df, dc, loss, d_g_ffn_post, d_conv_b, d_conv_w = _ffn_fwd(
        h2, x1, target, w_up, w_down, conv_w, conv_b, g_ffn_post)
    du, dx1, d_g_ffn_pre = _ffn_bwd(dc, conv_w, w_up, x1, g_ffn_pre, dy)
    d_w_up = _matmul_tn(h2, du, N_SHARD, "grad_w_up")
    d_w_down = _matmul_tn(yv, df, 1, "grad_w_down")[0].reshape(N_SHARD, CW // 2, D)
    swap = [("swap", d_w_up), ("swap", d_w_down)] if on_mesh else ()
    d_mixed, d_pool, d_g_mix_post, d_attn, delta, from_sibling = _mix_out_bwd(dx1, mixed, g_mix_post, w_out, attn, swap)
    d_w_out = _matmul_tn(cat, d_mixed, 1, "grad_w_out")[0].reshape(N_SHARD, D // N_SHARD, D)
    d_pool_in, d_pool_w, d_pool_scale = _pool_bwd(pool_in, d_pool, pool_w, pool_scale)
    grads = dict(pool_w=d_pool_w, pool_scale=d_pool_scale, w_out=d_w_out, g_mix_post=d_g_mix_post,
                 g_ffn_pre=d_g_ffn_pre, w_up=d_w_up, conv_w=d_conv_w, conv_b=d_conv_b, w_down=d_w_down,
                 g_ffn_post=d_g_ffn_post)
    cargo = [(), (), ()]
    if on_mesh:
        c_arr, device = mesh_pos
        up_f32, up_bf16 = _pair_sum(d_w_up, from_sibling[0], c_arr, "pair_sum_w_up")
        down_f32, down_bf16 = _pair_sum(d_w_down, from_sibling[1], c_arr, "pair_sum_w_down")
        early = _pack_small(grads, SMALL_EARLY)
        early_slots = lax.dynamic_update_index_in_dim(jnp.zeros((8,) + early.shape, F32), early, device, 0)
        cargo = [[("scatter", down_bf16)], [("scatter", up_bf16)], [("everyone", early_slots)]]

    dqkv1, edges, landed1 = _attn_bwd_consecutive(*qkv[0], d_attn[0], lse[0], delta[0], cargo[0])
    dqkv, landed = zip(*[_attn_bwd(*qkv[i], d_attn[i], lse[i], delta[i], DILATIONS[i], cargo[i]) for i in (1, 2)])
    if on_mesh:
        grads.update(w_down=(down_f32, landed1[0]), w_up=(up_f32, landed[0][0]), small_early=landed[1][0])
    d_proj, grad_x, grads["g_mix_pre"] = _mix_in_bwd((dqkv1,) + dqkv, edges, d_pool_in, w_in, x, g_mix_pre, dx1)
    grads["w_in"] = _matmul_tn(h1, d_proj, N_SHARD, "grad_w_in")
    return loss, grad_x, grads


ANY = pl.BlockSpec(memory_space=pl.ANY)


def _position():
    x, y, c = lax.axis_index("x"), lax.axis_index("y"), lax.axis_index("c")
    chips = [(1 - x, y), (x, 1 - y), (1 - x, 1 - y)]
    return x, y, c, chips


def _remote(src, dst, send_sem, recv_sem, to):
    return pltpu.make_async_remote_copy(src_ref=src, dst_ref=dst, send_sem=send_sem, recv_sem=recv_sem,
                                        device_id=to, device_id_type=MESH)


def _cast_bf16(w, shard_arr, name):
    R, C = w.shape
    tr = R // 2

    def body(s_ref, w_ref, o_ref):
        o_ref[0] = w_ref[...].astype(BF16)

    return pl.pallas_call(
        body, name=name,
        grid_spec=pltpu.PrefetchScalarGridSpec(
            num_scalar_prefetch=1, grid=(2,),
            in_specs=[pl.BlockSpec((tr, C), lambda i, s_ref: (i, 0))],
            out_specs=pl.BlockSpec((1, tr, C), lambda i, s_ref: (s_ref[0], i, 0))),
        out_shape=jax.ShapeDtypeStruct((N_SHARD, R, C), BF16),
        compiler_params=_params(("parallel",)))(shard_arr, w)


def _gather_weights(bufs):
    n = len(bufs) - 1

    def body(*refs):
        outs, cw_out = refs[n + 1:2 * n + 1], refs[2 * n + 1]
        ici_send, ici_recv, d2d_send, d2d_recv = refs[2 * n + 2:]
        x, y, c, chips = _position()
        s = 2 * x + y
        sibling = (x, y, 1 - c)

        def half(a, shard, h):
            rows = outs[a].shape[1] // 2
            return outs[a].at[shard, pl.ds(h * rows, rows), :]

        sends = []
        for a in range(n):
            for j, (px, py) in enumerate(chips):
                sends.append(_remote(half(a, s, c), half(a, s, c),
                                     ici_send.at[3 * a + j], ici_recv.at[3 * a + j], (px, py, c)))
        for j, (px, py) in enumerate(chips):
            sends.append(_remote(cw_out.at[s], cw_out.at[s], ici_send.at[3 * n + j], ici_recv.at[3 * n + j], (px, py, c)))
        for cp in sends:
            cp.start()
        passed = []
        for a in range(n):
            for j, (px, py) in enumerate(chips):
                sj = 2 * px + py
                got = half(a, sj, c)
                _remote(got, got, ici_send.at[3 * a + j], ici_recv.at[3 * a + j], (px, py, c)).wait_recv()
                fwd = _remote(got, got, d2d_send.at[3 * a + j], d2d_recv.at[3 * a + j], sibling)
                fwd.start()
                passed.append(fwd)
        for j, (px, py) in enumerate(chips):
            got = cw_out.at[2 * px + py]
            _remote(got, got, ici_send.at[3 * n + j], ici_recv.at[3 * n + j], (px, py, c)).wait_recv()
        for a in range(n):
            for j, (px, py) in enumerate(chips):
                got = half(a, 2 * px + py, 1 - c)
                _remote(got, got, d2d_send.at[3 * a + j], d2d_recv.at[3 * a + j], sibling).wait_recv()
        for cp in sends + passed:
            cp.wait_send()

    return pl.pallas_call(
        body, name="gather_weights",
        in_specs=[ANY] * (n + 1), out_specs=[ANY] * (n + 1),
        out_shape=[jax.ShapeDtypeStruct(b.shape, b.dtype) for b in bufs],
        input_output_aliases={i: i for i in range(n + 1)},
        scratch_shapes=[pltpu.SemaphoreType.DMA((3 * n + 3,)), pltpu.SemaphoreType.DMA((3 * n + 3,)),
                        pltpu.SemaphoreType.DMA((3 * n,)), pltpu.SemaphoreType.DMA((3 * n,))],
        compiler_params=pltpu.CompilerParams(has_side_effects=True),
    )(*bufs)


def _swap_halves(grads, tag):
    n = len(grads)

    def body(*refs):
        ins, outs, send_sem, recv_sem = refs[:n], refs[n:2 * n], refs[2 * n], refs[2 * n + 1]
        x, y, c, _ = _position()
        copies = []
        for a in range(n):
            rows = ins[a].shape[1] // 2
            copies.append(_remote(ins[a].at[:, pl.ds((1 - c) * rows, rows), :], outs[a],
                                  send_sem.at[a], recv_sem.at[a], (x, y, 1 - c)))
        for cp in copies:
            cp.start()
        for cp in copies:
            cp.wait()

    return pl.pallas_call(
        body, name="swap_grad_halves_" + tag,
        in_specs=[ANY] * n, out_specs=[ANY] * n,
        out_shape=[jax.ShapeDtypeStruct((g.shape[0], g.shape[1] // 2, g.shape[2]), F32) for g in grads],
        scratch_shapes=[pltpu.SemaphoreType.DMA((n,)), pltpu.SemaphoreType.DMA((n,))],
        compiler_params=pltpu.CompilerParams(has_side_effects=True),
    )(*grads)


def _pair_sum(g, got, c_arr, name):
    n_sh, R, C = g.shape
    rows = R // 2

    def body(c_ref, g_ref, r_ref, f_ref, b_ref):
        t = g_ref[...] + r_ref[...]
        f_ref[...] = t
        b_ref[...] = t.astype(BF16)

    blk = pl.BlockSpec((1, rows, C), lambda i, c_ref: (i, 0, 0))
    return pl.pallas_call(
        body, name=name,
        grid_spec=pltpu.PrefetchScalarGridSpec(
            num_scalar_prefetch=1, grid=(n_sh,),
            in_specs=[pl.BlockSpec((1, rows, C), lambda i, c_ref: (i, c_ref[0], 0)), blk],
            out_specs=[blk, blk]),
        out_shape=[jax.ShapeDtypeStruct((n_sh, rows, C), F32), jax.ShapeDtypeStruct((n_sh, rows, C), BF16)],
        compiler_params=_params(("parallel",)),
    )(c_arr, g, got)


def _scatter_grads(sums_bf16, small_all):
    n = len(sums_bf16)

    def body(*refs):
        b_ins = refs[:n]
        recvs, sm = refs[n + 1:2 * n + 1], refs[2 * n + 1]
        ici_send, ici_recv, sm_send, sm_recv = refs[2 * n + 2:]
        x, y, c, chips = _position()
        me = 4 * x + 2 * y + c
        copies = []
        for a in range(n):
            for j, (px, py) in enumerate(chips):
                copies.append(_remote(b_ins[a].at[2 * px + py], recvs[a].at[j],
                                      ici_send.at[3 * a + j], ici_recv.at[3 * a + j], (px, py, c)))
        for k in range(1, 8):
            peer = (x ^ (k >> 2), y ^ ((k >> 1) & 1), c ^ (k & 1))
            copies.append(_remote(sm.at[me], sm.at[me], sm_send.at[k - 1], sm_recv.at[k - 1], peer))
        for cp in copies:
            cp.start()
        for cp in copies:
            cp.wait_send()
        for a in range(n):
            for j, (px, py) in enumerate(chips):
                _remote(recvs[a].at[j], recvs[a].at[j], ici_send.at[3 * a + j], ici_recv.at[3 * a + j],
                        (px, py, c)).wait_recv()
        for k in range(1, 8):
            peer = (x ^ (k >> 2), y ^ ((k >> 1) & 1), c ^ (k & 1))
            theirs = sm.at[4 * peer[0] + 2 * peer[1] + peer[2]]
            _remote(theirs, theirs, sm_send.at[k - 1], sm_recv.at[k - 1], peer).wait_recv()

    out = pl.pallas_call(
        body, name="scatter_grads",
        in_specs=[ANY] * (n + 1), out_specs=[ANY] * (n + 1),
        out_shape=[jax.ShapeDtypeStruct((3,) + b.shape[1:], BF16) for b in sums_bf16]
        + [jax.ShapeDtypeStruct(small_all.shape, F32)],
        input_output_aliases={n: n},
        scratch_shapes=[pltpu.SemaphoreType.DMA((3 * n,)), pltpu.SemaphoreType.DMA((3 * n,)),
                        pltpu.SemaphoreType.DMA((7,)), pltpu.SemaphoreType.DMA((7,))],
        compiler_params=pltpu.CompilerParams(has_side_effects=True),
    )(*sums_bf16, small_all)
    return out[:n], out[n]


def _shard_sum(sums_f32, recv, shard_arr, c_arr, name):
    _, rows, C = sums_f32.shape

    def body(s_ref, c_ref, o_ref, r_ref, t_ref):
        t_ref[...] = ((o_ref[0] + r_ref[0].astype(F32)) + r_ref[1].astype(F32)) + r_ref[2].astype(F32)

    return pl.pallas_call(
        body, name=name,
        grid_spec=pltpu.PrefetchScalarGridSpec(
            num_scalar_prefetch=2, grid=(1,),
            in_specs=[pl.BlockSpec((1, rows, C), lambda i, s_ref, c_ref: (s_ref[0], 0, 0)),
                      pl.BlockSpec((3, rows, C), lambda i, s_ref, c_ref: (0, 0, 0))],
            out_specs=pl.BlockSpec((rows, C), lambda i, s_ref, c_ref: (c_ref[0], 0))),
        out_shape=jax.ShapeDtypeStruct((2 * rows, C), F32),
        compiler_params=_params(("arbitrary",)),
    )(shard_arr, c_arr, sums_f32, recv)


def _join_halves(bufs):
    n = len(bufs)

    def body(*refs):
        outs, send_sem, recv_sem = refs[n:2 * n], refs[2 * n], refs[2 * n + 1]
        x, y, c, _ = _position()
        copies = []
        for a in range(n):
            rows = outs[a].shape[0] // 2
            mine = outs[a].at[pl.ds(c * rows, rows), :]
            copies.append(_remote(mine, mine, send_sem.at[a], recv_sem.at[a], (x, y, 1 - c)))
        for cp in copies:
            cp.start()
        for a, cp in enumerate(copies):
            cp.wait_send()
            rows = outs[a].shape[0] // 2
            theirs = outs[a].at[pl.ds((1 - c) * rows, rows), :]
            _remote(theirs, theirs, send_sem.at[a], recv_sem.at[a], (x, y, 1 - c)).wait_recv()

    return pl.pallas_call(
        body, name="join_grad_halves",
        in_specs=[ANY] * n, out_specs=[ANY] * n,
        out_shape=[jax.ShapeDtypeStruct(b.shape, F32) for b in bufs],
        input_output_aliases={i: i for i in range(n)},
        scratch_shapes=[pltpu.SemaphoreType.DMA((n,)), pltpu.SemaphoreType.DMA((n,))],
        compiler_params=pltpu.CompilerParams(has_side_effects=True),
    )(*bufs)


def _small_sum(parts, tag):
    _, R, C = parts.shape

    def body(p_ref, o_ref):
        t = p_ref[0]
        for k in range(1, 8):
            t = t + p_ref[k]
        o_ref[...] = t

    return pl.pallas_call(
        body, name="small_grad_sum_" + tag, grid=(1,),
        in_specs=[pl.BlockSpec((8, R, C), lambda i: (0, 0, 0))], out_specs=pl.BlockSpec((R, C), lambda i: (0, 0)),
        out_shape=jax.ShapeDtypeStruct((R, C), F32), compiler_params=_params(("arbitrary",)),
    )(parts)


def _adamw_math(w, g, m, v):
    m = ADAM_B1 * m + (1.0 - ADAM_B1) * g
    v = ADAM_B2 * v + (1.0 - ADAM_B2) * (g * g)
    m_hat = m / (1.0 - ADAM_B1 ** ADAM_STEP)
    v_hat = v / (1.0 - ADAM_B2 ** ADAM_STEP)
    delta = -ADAM_LR * (m_hat / (jnp.sqrt(v_hat) + ADAM_EPS) + ADAM_WD * w)
    return delta, m, v


def _adamw_big(w, g, m, v, name):
    R, C = w.shape
    tr = R // 4

    def body(w_ref, g_ref, m_ref, v_ref, d_ref, nm_ref, nv_ref):
        d_ref[...], nm_ref[...], nv_ref[...] = _adamw_math(w_ref[...], g_ref[...], m_ref[...], v_ref[...])

    blk = pl.BlockSpec((tr, C), lambda i: (i, 0))
    return pl.pallas_call(
        body, name=name, grid=(4,), in_specs=[blk] * 4, out_specs=[blk] * 3,
        out_shape=[jax.ShapeDtypeStruct((R, C), F32)] * 3, compiler_params=_params(("parallel",)),
    )(w, g, m, v)


def _adamw_small(ws, gs, ms, vs):
    n = len(ws)

    def body(*refs):
        for a in range(n):
            w, g, m, v = (refs[k * n + a][...] for k in range(4))
            d, nm, nv = _adamw_math(w, g, m, v)
            refs[4 * n + a][...] = d
            refs[5 * n + a][...] = nm
            refs[6 * n + a][...] = nv

    shapes = [jax.ShapeDtypeStruct(w.shape, F32) for w in ws]
    out = pl.pallas_call(body, name="adamw_small", out_shape=shapes * 3)(*ws, *gs, *ms, *vs)
    return out[:n], out[n:2 * n], out[2 * n:]


BIG = ("w_in", "w_out", "w_up", "w_down")
SMALL = ("g_mix_pre", "pool_w", "pool_scale", "g_mix_post", "g_ffn_pre", "conv_b", "g_ffn_post", "conv_w")
ORDER = ("g_mix_pre", "w_in", "pool_w", "pool_scale", "w_out", "g_mix_post", "g_ffn_pre", "w_up", "conv_w", "conv_b",
         "w_down", "g_ffn_post")


def kernel(x, g_mix_pre, w_in, pool_w, pool_scale, w_out, g_mix_post, g_ffn_pre, w_up, conv_w, conv_b, w_down, g_ffn_post, loss_target, m_g_mix_pre, m_w_in, m_pool_w, m_pool_scale, m_w_out, m_g_mix_post, m_g_ffn_pre, m_w_up, m_conv_w, m_conv_b, m_w_down, m_g_ffn_post, v_g_mix_pre, v_w_in, v_pool_w, v_pool_scale, v_w_out, v_g_mix_post, v_g_ffn_pre, v_w_up, v_conv_w, v_conv_b, v_w_down, v_g_ffn_post):
    args = dict(locals())
    W = {n: args[n][0] for n in ORDER}
    M = {n: args["m_" + n][0] for n in ORDER}
    V = {n: args["v_" + n][0] for n in ORDER}
    for d in (W, M, V):
        d["pool_w"] = d["pool_w"].reshape(-1, POOL_GROUP)
        for n in ("g_mix_pre", "pool_scale", "g_mix_post", "g_ffn_pre", "conv_b", "g_ffn_post"):
            d[n] = d[n].reshape(1, -1)
    CW = W["w_up"].shape[1]
    c_arr = lax.axis_index("c").astype(jnp.int32).reshape(1)
    shard = 2 * lax.axis_index("x") + lax.axis_index("y")
    shard_arr = shard.astype(jnp.int32).reshape(1)
    device = 2 * shard + lax.axis_index("c")

    conv_w_slots = lax.dynamic_update_index_in_dim(jnp.zeros((N_SHARD,) + W["conv_w"].shape, F32), W["conv_w"], shard, 0)
    slots = {n: _cast_bf16(W[n], shard_arr, "cast_" + n) for n in BIG}
    w_in_g, conv_w_g = _gather_weights([slots["w_in"], conv_w_slots])
    conv_w_full = conv_w_g.transpose(1, 0, 2).reshape(CONV_WIDTH, 1, N_SHARD * CW)

    loss, grad_x, G = _local_step(
        x[0], loss_target[0], W["g_mix_pre"], w_in_g, W["pool_w"].reshape(-1, POOL_GROUP, POOL_GROUP), W["pool_scale"],
        slots["w_out"], W["g_mix_post"], W["g_ffn_pre"], slots["w_up"], conv_w_full, W["conv_b"],
        slots["w_down"], W["g_ffn_post"], (c_arr, device))

    late = ("w_in", "w_out")
    from_sibling = _swap_halves([G[n] for n in late], "mix")
    sums = {n: _pair_sum(G[n], r, c_arr, "pair_sum_" + n) for n, r in zip(late, from_sibling)}
    loss_rows = jnp.pad(loss, ((0, 7), (0, LANES - 1)))
    small = jnp.concatenate([_pack_small(G, SMALL_LATE), loss_rows], axis=0)
    small_slots = lax.dynamic_update_index_in_dim(jnp.zeros((8,) + small.shape, F32), small, device, 0)
    recvs, small_all = _scatter_grads([sums[n][1] for n in late], small_slots)
    reduced = {n: (sums[n][0], r) for n, r in zip(late, recvs)}
    reduced.update({n: G[n] for n in ("w_up", "w_down")})
    halves = [_shard_sum(*reduced[n], shard_arr, c_arr, "shard_sum_" + n) for n in BIG]
    full = dict(zip(BIG, _join_halves(halves)))
    full.update(_unpack_small(_small_sum(G["small_early"], "early"), SMALL_EARLY, W, shard))
    late_total = _small_sum(small_all, "late")
    full.update(_unpack_small(late_total, SMALL_LATE, W, shard))
    loss = late_total[-8, 0]

    delta, new_m, new_v = {}, {}, {}
    for n in BIG:
        delta[n], new_m[n], new_v[n] = _adamw_big(W[n], full[n], M[n], V[n], "adamw_" + n)
    ds, nms, nvs = _adamw_small([W[n] for n in SMALL], [full[n] for n in SMALL], [M[n] for n in SMALL],
                                [V[n] for n in SMALL])
    for n, d, nm, nv in zip(SMALL, ds, nms, nvs):
        delta[n], new_m[n], new_v[n] = d, nm, nv

    shaped = lambda d: [d[n].reshape(args[n].shape) for n in ORDER]
    return (loss, grad_x[None], *shaped(full), *shaped(delta), *shaped(new_m), *shaped(new_v))
```

```python
import functools

import jax
import jax.numpy as jnp
from jax import lax
from jax.experimental import pallas as pl
from jax.experimental.pallas import tpu as pltpu

F32 = jnp.float32
BF16 = jnp.bfloat16

RMS_EPS = 1e-6
NEG_INF = -1e30
N_HEADS = 8
HEAD_DIM = 64
ATTN_WIDTH = N_HEADS * HEAD_DIM
ATTN_SCALE = HEAD_DIM ** -0.5
ATTN_BLOCK = 128
DILATIONS = (1, 4, 16)
RESIDUES_PER_STEP = 4
CONSECUTIVE_BLOCKS = 4
POOL_WINDOWS = (2, 4, 8, 16)
POOL_GROUP = 128
POOL_WIDTH = POOL_GROUP * len(POOL_WINDOWS)
POOL_HALO = 16
CONV_WIDTH = 3
CONV_HALO = 8
N_SHARD = 4
LANES = 128
STAT_LANES = 16
STAT_WIDTH = N_HEADS * STAT_LANES

ADAM_LR = 0.001
ADAM_B1 = 0.9
ADAM_B2 = 0.999
ADAM_EPS = 1e-08
ADAM_WD = 0.01
ADAM_STEP = 10

VMEM_LIMIT = 60 * 1024 * 1024
MESH = pl.DeviceIdType.MESH
NT = (((1,), (1,)), ((), ()))
TN = (((0,), (0,)), ((), ()))


def _params(sem, vmem=None):
    return pltpu.CompilerParams(dimension_semantics=sem, vmem_limit_bytes=vmem)


def _const_spec(shape):
    zeros = (0,) * len(shape)
    return pl.BlockSpec(shape, lambda *_: zeros, pipeline_mode=pl.Buffered(1))


def _dot(a, b):
    return jnp.dot(a, b, preferred_element_type=F32)


def _dot_nt(a, b):
    return lax.dot_general(a, b, NT, preferred_element_type=F32)


def _dot_tn(a, b):
    return lax.dot_general(a, b, TN, preferred_element_type=F32)


def _rms_stats(x):
    r = lax.rsqrt(jnp.mean(x * x, axis=-1, keepdims=True) + RMS_EPS)
    return x * r, r


def _rms_bwd(dy, n, r, g):
    dg = jnp.sum(dy * n, axis=0, keepdims=True)
    dn = dy * g
    dx = r * (dn - n * jnp.mean(dn * n, axis=-1, keepdims=True))
    return dx, dg


def _gelu_tanh(g):
    k = 0.7978845608028654
    kc = k * 0.044715
    g2 = g * g
    t = jnp.tanh(g * (k + kc * g2))
    h = 0.5 * t + 0.5
    dh = (0.5 - 0.5 * (t * t)) * (k + (3.0 * kc) * g2)
    return g * h, h + g * dh


def _residue_shape(S, d, dtype, width=ATTN_WIDTH):
    return jax.ShapeDtypeStruct((S // d, d * width), dtype)


def _residue_spec(TM, d, width=ATTN_WIDTH):
    return pl.BlockSpec((TM // d, d * width), lambda i: (i, 0))


def _token_scratch(TM, width=ATTN_WIDTH):
    return [pltpu.VMEM((TM, LANES), F32)] * (width // LANES)


def _head_stat_matrix(pick_first_lane):
    r = lax.broadcasted_iota(jnp.int32, (ATTN_WIDTH, STAT_WIDTH), 0)
    c = lax.broadcasted_iota(jnp.int32, (ATTN_WIDTH, STAT_WIDTH), 1) // STAT_LANES
    return ((r == c * HEAD_DIM) if pick_first_lane else (r // HEAD_DIM == c)).astype(BF16)


def _bf16_pieces(x, n):
    pieces = []
    for _ in range(n):
        p = x.astype(BF16)
        pieces.append(p)
        x = x - p.astype(F32)
    return pieces


def _put_tokens(dst_s, val):
    for cb, chunk in enumerate(dst_s):
        chunk[...] = val[:, cb * LANES:(cb + 1) * LANES]


def _get_tokens(src_s):
    return jnp.concatenate([chunk[...] for chunk in src_s], axis=1)


def _to_residue(val, src_s, out_ref, d, dtype):
    if d == 1:
        out_ref[...] = val.astype(dtype)
        return
    rows = src_s[0].shape[0]
    for r in range(d):
        for cb, chunk in enumerate(src_s):
            col = (r * len(src_s) + cb) * LANES
            out_ref[:, col:col + LANES] = chunk[pl.ds(r, rows // d, stride=d), :].astype(dtype)


def _from_residue(in_ref, dst_s, d):
    if d == 1:
        return in_ref[...].astype(F32)
    rows = dst_s[0].shape[0]
    for r in range(d):
        for cb, chunk in enumerate(dst_s):
            col = (r * len(dst_s) + cb) * LANES
            chunk[pl.ds(r, rows // d, stride=d), :] = in_ref[:, col:col + LANES].astype(F32)
    return _get_tokens(dst_s)


def _mix_in_fwd(x, g_pre, w_in, cargo=()):
    S, D = x.shape
    TM = 512
    nc = len(cargo)
    kinds = [kind for kind, _ in cargo]
    n_chunks = ATTN_WIDTH // LANES

    def body(x_ref, g_ref, w_ref, *refs):
        cargo_in, refs = refs[:nc], refs[nc:]
        qkv_refs, p_ref, h_ref = refs[:9], refs[9], refs[10]
        t_s = refs[11 + nc:11 + nc + n_chunks]
        cargo_refs = (kinds, cargo_in, refs[11:11 + nc], refs[11 + nc + n_chunks:])
        if nc:
            _cargo_start(*cargo_refs, pl.program_id(0) == 0)
        n, _ = _rms_stats(x_ref[...])
        hb = (n * g_ref[...]).astype(BF16)
        h_ref[...] = hb
        for a in range(3):
            res = _dot(hb, w_ref[a])
            if a == 0:
                res = res * ATTN_SCALE
            _put_tokens(t_s, res)
            for i, d in enumerate(DILATIONS):
                _to_residue(res, t_s, qkv_refs[3 * i + a], d, BF16)
        p_ref[...] = _dot(hb, w_ref[3])
        if nc:
            _cargo_finish(*cargo_refs, pl.program_id(0) == S // TM - 1)

    row = lambda w: pl.BlockSpec((TM, w), lambda i: (i, 0))
    arrays, cargo_specs, shapes, aliases, sems = _cargo_call(cargo, 3, 11)
    out = pl.pallas_call(
        body, name="mix_in_fwd", grid=(S // TM,),
        in_specs=[row(D), _const_spec((1, D)), _const_spec(w_in.shape)] + cargo_specs,
        out_specs=[_residue_spec(TM, d) for d in DILATIONS for _ in range(3)] + [row(POOL_WIDTH), row(D)] + cargo_specs,
        out_shape=[_residue_shape(S, d, BF16) for d in DILATIONS for _ in range(3)]
        + [jax.ShapeDtypeStruct((S, POOL_WIDTH), F32), jax.ShapeDtypeStruct((S, D), BF16)] + shapes,
        input_output_aliases=aliases,
        scratch_shapes=_token_scratch(TM) + sems,
        compiler_params=_params(("arbitrary",), VMEM_LIMIT),
    )(x, g_pre, w_in, *arrays)
    return [out[0:3], out[3:6], out[6:9]], out[9], out[10], out[11:]


def _band_mask(n):
    qi = lax.broadcasted_iota(jnp.int32, (ATTN_BLOCK, 2 * ATTN_BLOCK), 0)
    ki = lax.broadcasted_iota(jnp.int32, (ATTN_BLOCK, 2 * ATTN_BLOCK), 1)
    dist = qi + ATTN_BLOCK - ki
    return (dist >= 0) & (dist <= ATTN_BLOCK) & ((ki >= ATTN_BLOCK) | (n > 0))


def _first_head_lanes():
    return lax.broadcasted_iota(jnp.int32, (1, LANES), 1) < HEAD_DIM


def _stack_heads(pair, first):
    zero = jnp.zeros_like(pair)
    return jnp.concatenate([jnp.where(first, pair, zero), jnp.where(first, zero, pair)], axis=0)


def _unstack_heads(stacked, first):
    return jnp.where(first, stacked[:ATTN_BLOCK], stacked[ATTN_BLOCK:])


CARGO_COPIES = {"ici": 3, "d2d": 3, "scatter": 3, "swap": 1, "everyone": 7}
CARGO_IN_PLACE = ("ici", "d2d", "everyone")


def _cargo_copies(kinds, ins, outs, send_sems, recv_sems, want_recvs=True):
    x, y, c, chips = _position()
    s = 2 * x + y
    me = 2 * s + c
    sibling = (x, y, 1 - c)
    sends, recvs = [], []

    def add(k, src, dst, landing, to):
        sends.append(_remote(src, dst, send_sems.at[k], recv_sems.at[k], to))
        if want_recvs:
            recvs.append(_remote(landing, landing, send_sems.at[k], recv_sems.at[k], to))

    k0 = 0
    for a, kind in enumerate(kinds):
        if kind == "swap":
            rows = ins[a].shape[1] // 2
            add(k0, ins[a].at[:, pl.ds((1 - c) * rows, rows), :], outs[a], outs[a], sibling)
        elif kind == "everyone":
            for m in range(1, 8):
                peer = (x ^ (m >> 2), y ^ ((m >> 1) & 1), c ^ (m & 1))
                add(k0 + m - 1, outs[a].at[me], outs[a].at[me], outs[a].at[4 * peer[0] + 2 * peer[1] + peer[2]], peer)
        else:
            for j, (px, py) in enumerate(chips):
                sj = 2 * px + py
                if kind == "scatter":
                    add(k0 + j, ins[a].at[sj], outs[a].at[j], outs[a].at[j], (px, py, c))
                    continue
                buf = outs[a]
                rows = buf.shape[1] // 2
                half = lambda shard, h: buf.at[shard, pl.ds(h * rows, rows), :]
                if kind == "ici":
                    add(k0 + j, half(s, c), half(s, c), half(sj, c), (px, py, c))
                else:
                    add(k0 + j, half(sj, c), half(sj, c), half(sj, 1 - c), sibling)
        k0 += CARGO_COPIES[kind]
    return sends, recvs


def _cargo_start(kinds, ins, outs, sems, first_step):
    @pl.when(first_step)
    def _():
        for cp in _cargo_copies(kinds, ins, outs, *sems, want_recvs=False)[0]:
            cp.start()


def _cargo_finish(kinds, ins, outs, sems, last_step):
    @pl.when(last_step)
    def _():
        sends, recvs = _cargo_copies(kinds, ins, outs, *sems)
        for cp in sends:
            cp.wait_send()
        for cp in recvs:
            cp.wait_recv()


def _cargo_call(cargo, n_in, n_out):
    arrays = [a for _, a in cargo]
    shapes = []
    for kind, a in cargo:
        shape = {"scatter": (3,) + a.shape[1:], "swap": (a.shape[0], a.shape[1] // 2, a.shape[2])}.get(kind, a.shape)
        shapes.append(jax.ShapeDtypeStruct(shape, a.dtype))
    aliases = {n_in + i: n_out + i for i, (kind, _) in enumerate(cargo) if kind in CARGO_IN_PLACE}
    n_sems = sum(CARGO_COPIES[kind] for kind, _ in cargo)
    sems = [pltpu.SemaphoreType.DMA((n_sems,))] * 2 if cargo else []
    return arrays, [ANY] * len(cargo), shapes, aliases, sems


def _attn_fwd(q, k, v, d, cargo=()):
    L = q.shape[0]
    group = min(d, RESIDUES_PER_STEP)
    width = group * ATTN_WIDTH
    qb = RESIDUES_PER_STEP // group
    steps = L // (qb * ATTN_BLOCK)
    nc = len(cargo)
    kinds = [kind for kind, _ in cargo]

    def body(*refs):
        q_ref, kp_ref, kc_ref, vp_ref, vc_ref = refs[:5]
        o_ref, lse_ref = refs[5 + nc:7 + nc]
        cargo_refs = (kinds, refs[5:5 + nc], refs[7 + nc:7 + 2 * nc], refs[7 + 2 * nc:])
        r, n = pl.program_id(0), pl.program_id(1)
        if nc:
            _cargo_start(*cargo_refs, (r == 0) & (n == 0))
        first = _first_head_lanes()
        for sub in range(qb):
            rows = slice(sub * ATTN_BLOCK, (sub + 1) * ATTN_BLOCK)
            valid = _band_mask(n if sub == 0 else 1)
            valid2 = jnp.concatenate([valid, valid], axis=0)
            for hp in range(width // LANES):
                sl = slice(hp * LANES, (hp + 1) * LANES)
                if sub == 0:
                    kk = jnp.concatenate([kp_ref[:, sl], kc_ref[rows, sl]], axis=0)
                    vv = jnp.concatenate([vp_ref[:, sl], vc_ref[rows, sl]], axis=0)
                else:
                    keys = slice((sub - 1) * ATTN_BLOCK, (sub + 1) * ATTN_BLOCK)
                    kk, vv = kc_ref[keys, sl], vc_ref[keys, sl]
                s = jnp.where(valid2, _dot_nt(_stack_heads(q_ref[rows, sl], first), kk), NEG_INF)
                m = jnp.max(s, axis=-1, keepdims=True)
                p = jnp.exp(s - m)
                den = jnp.sum(p, axis=-1, keepdims=True)
                o_ref[rows, sl] = _unstack_heads(_dot(p.astype(BF16), vv) / den, first).astype(BF16)
                lse_ref[rows, sl] = _unstack_heads(m + jnp.log(den), first)
        if nc:
            _cargo_finish(*cargo_refs, (r == d // group - 1) & (n == steps - 1))

    cur = pl.BlockSpec((qb * ATTN_BLOCK, width), lambda r, n: (n, r))
    prev = pl.BlockSpec((ATTN_BLOCK, width), lambda r, n: (jnp.maximum(n * qb - 1, 0), r))
    arrays, specs, shapes, aliases, sems = _cargo_call(cargo, 5, 2)
    out = pl.pallas_call(
        body, name=f"attn_fwd_d{d}", grid=(d // group, steps),
        in_specs=[cur, prev, cur, prev, cur] + specs,
        out_specs=[cur, cur] + specs,
        out_shape=[jax.ShapeDtypeStruct((L, d * ATTN_WIDTH), BF16), jax.ShapeDtypeStruct((L, d * ATTN_WIDTH), F32)] + shapes,
        input_output_aliases=aliases, scratch_shapes=sems,
        compiler_params=_params(("arbitrary", "arbitrary")),
    )(q, k, k, v, v, *arrays)
    return out[0], out[1], out[2:]


def _attn_mix(outs, lses):
    S = outs[0].shape[0]
    TM = 512
    n = len(DILATIONS)

    def body(*refs):
        o_refs, l_refs, attn_ref, lse_refs = refs[:n], refs[n:2 * n], refs[2 * n], refs[2 * n + 1:3 * n + 1]
        t_s, c_s = refs[3 * n + 1:-1], refs[-1:]
        os = [_from_residue(o_refs[i], t_s, d) for i, d in enumerate(DILATIONS)]
        ls = [_from_residue(l_refs[i], t_s, d) for i, d in enumerate(DILATIONS)]
        m = jnp.maximum(jnp.maximum(ls[0], ls[1]), ls[2])
        es = [jnp.exp(l - m) for l in ls]
        den = es[0] + es[1] + es[2]
        attn_ref[...] = (es[0] * os[0] + es[1] * os[1] + es[2] * os[2]) / den
        pick = _head_stat_matrix(pick_first_lane=True)
        lse = sum(_dot(p, pick) for p in _bf16_pieces(m + jnp.log(den), 3))
        _put_tokens(c_s, lse)
        for i, d in enumerate(DILATIONS):
            _to_residue(lse, c_s, lse_refs[i], d, F32)

    specs = [_residue_spec(TM, d) for d in DILATIONS]
    out = pl.pallas_call(
        body, name="attn_mix", grid=(S // TM,),
        in_specs=specs * 2, out_specs=[specs[0]] + [_residue_spec(TM, d, STAT_WIDTH) for d in DILATIONS],
        out_shape=[jax.ShapeDtypeStruct((S, ATTN_WIDTH), F32)]
        + [_residue_shape(S, d, F32, STAT_WIDTH) for d in DILATIONS],
        scratch_shapes=_token_scratch(TM) + _token_scratch(TM, STAT_WIDTH),
        compiler_params=_params(("parallel",)),
    )(*outs, *lses)
    return out[0], out[1:]


def _pool_counts(first_row, rows, w):
    t = first_row + lax.broadcasted_iota(jnp.int32, (rows, 1), 0)
    return jnp.minimum(t + 1, w).astype(F32)


def _trailing_sums(xe, w):
    s, k = xe, 1
    while k < w:
        s = s + pltpu.roll(s, k, 0)
        k *= 2
    return s


def _leading_sums(xe, w):
    rows = xe.shape[0]
    s, k = xe, 1
    while k < w:
        s = s + pltpu.roll(s, rows - k, 0)
        k *= 2
    return s


def _pooled_groups(halo, cur, first_row):
    TM = cur.shape[0]
    xe = jnp.concatenate([halo, cur], axis=0)
    out = []
    for g, w in enumerate(POOL_WINDOWS):
        a = xe[:, g * POOL_GROUP:(g + 1) * POOL_GROUP]
        s = _trailing_sums(a, w)[POOL_HALO:]
        out.append(s / _pool_counts(first_row, TM, w) - a[POOL_HALO:])
    return out


def _pool_fwd(pool_in, pool_w, pool_scale):
    S = pool_in.shape[0]
    TM = 512
    HB = TM // POOL_HALO

    def body(cur_ref, halo_ref, w_ref, sc_ref, y_ref):
        i = pl.program_id(0)
        halo = jnp.where(i > 0, halo_ref[...], 0.0)
        pooled = _pooled_groups(halo, cur_ref[...], i * TM)
        for g in range(len(POOL_WINDOWS)):
            sl = slice(g * POOL_GROUP, (g + 1) * POOL_GROUP)
            y = _dot(pooled[g].astype(BF16), w_ref[g].astype(BF16)) * sc_ref[:, sl]
            y_ref[:, sl] = y.astype(BF16)

    return pl.pallas_call(
        body, name="pool_fwd", grid=(S // TM,),
        in_specs=[pl.BlockSpec((TM, POOL_WIDTH), lambda i: (i, 0)),
                  pl.BlockSpec((POOL_HALO, POOL_WIDTH), lambda i: (jnp.maximum(i * HB - 1, 0), 0)),
                  _const_spec(pool_w.shape), _const_spec((1, POOL_WIDTH))],
        out_specs=pl.BlockSpec((TM, POOL_WIDTH), lambda i: (i, 0)),
        out_shape=jax.ShapeDtypeStruct((S, POOL_WIDTH), BF16),
        compiler_params=_params(("parallel",)),
    )(pool_in, pool_in, pool_w, pool_scale)


def _mix_out_fwd(attn, pool, w_out, x, g_post, g_ffn_pre):
    S, D = x.shape
    TM = 512

    def body(a_ref, p_ref, w_ref, x_ref, gp_ref, gf_ref, mixed_ref, x1_ref, h2_ref, cat_ref):
        ab = a_ref[...].astype(BF16)
        cat_ref[:, :ATTN_WIDTH] = ab
        cat_ref[:, ATTN_WIDTH:] = p_ref[...]
        mixed = _dot(ab, w_ref[:ATTN_WIDTH, :]) + _dot(p_ref[...], w_ref[ATTN_WIDTH:, :])
        mixed_ref[...] = mixed
        n, _ = _rms_stats(mixed)
        x1 = x_ref[...] + n * gp_ref[...]
        x1_ref[...] = x1
        n2, _ = _rms_stats(x1)
        h2_ref[...] = (n2 * gf_ref[...]).astype(BF16)

    row = lambda w: pl.BlockSpec((TM, w), lambda i: (i, 0))
    return pl.pallas_call(
        body, name="mix_out_fwd", grid=(S // TM,),
        in_specs=[row(ATTN_WIDTH), row(POOL_WIDTH), _const_spec(w_out.shape), row(D),
                  _const_spec((1, D)), _const_spec((1, D))],
        out_specs=[row(D), row(D), row(D), row(D)],
        out_shape=[jax.ShapeDtypeStruct((S, D), F32), jax.ShapeDtypeStruct((S, D), F32),
                   jax.ShapeDtypeStruct((S, D), BF16), jax.ShapeDtypeStruct((S, D), BF16)],
        compiler_params=_params(("parallel",), VMEM_LIMIT),
    )(attn, pool, w_out, x, g_post, g_ffn_pre)


def _ffn_fwd(h2, x1, target, w_up, w_down, conv_w, conv_b, g_post):
    S, D = x1.shape
    CW = w_up.shape[2]
    FF = 2 * CW
    TM = 256
    piece = 4 * LANES
    pieces = [(lo, min(lo + piece, CW)) for lo in range(0, CW, piece)]

    def body(h2_ref, x1_ref, t_ref, wu_ref, wd_ref, cw_ref, cb_ref, g_ref,
             yv_ref, dy_ref, df_ref, dc_ref, loss_ref, dg_ref, dcb_ref, dcw_ref,
             ue_s, dgate_s, dval_s):
        i = pl.program_id(0)

        @pl.when(i == 0)
        def _():
            loss_ref[...] = jnp.zeros_like(loss_ref)
            dg_ref[...] = jnp.zeros_like(dg_ref)
            dcb_ref[...] = jnp.zeros_like(dcb_ref)
            dcw_ref[...] = jnp.zeros_like(dcw_ref)
            ue_s[0:CONV_HALO, :] = jnp.zeros((CONV_HALO, 2 * FF), F32)

        @pl.when(i > 0)
        def _():
            ue_s[0:CONV_HALO, :] = ue_s[TM:TM + CONV_HALO, :]

        def shifted(cols, k):
            return pltpu.roll(ue_s[:, cols], k, 0)[CONV_HALO:]

        def conv(cols):
            return (cb_ref[:, cols] + cw_ref[2, :, cols] * ue_s[CONV_HALO:, cols]
                    + cw_ref[1, :, cols] * shifted(cols, 1) + cw_ref[0, :, cols] * shifted(cols, 2))

        hb = h2_ref[...]
        f = jnp.zeros((TM, D), F32)
        for j in range(2):
            jc = slice(j * CW, (j + 1) * CW)
            for half in range(2):
                blk = 2 * half + j
                cols = slice(blk * CW, (blk + 1) * CW)
                ue_s[CONV_HALO:, cols] = _dot(hb, wu_ref[blk])
            for lo, hi in pieces:
                pc = slice(j * CW + lo, j * CW + hi)
                gelu, dgelu = _gelu_tanh(conv(pc))
                val = conv(slice(FF + j * CW + lo, FF + j * CW + hi))
                dgate_s[:, pc] = val * dgelu
                dval_s[:, pc] = gelu
                yv_ref[:, pc] = (gelu * val).astype(BF16)
            f = f + _dot(yv_ref[:, jc], wd_ref[jc, :])

        n, r = _rms_stats(f)
        err = x1_ref[...] + n * g_ref[...] - t_ref[...]
        loss_ref[...] += 0.5 * jnp.sum(jnp.mean(err * err, axis=-1, keepdims=True), axis=0, keepdims=True)
        dy = err / D
        dy_ref[...] = dy
        df, dg = _rms_bwd(dy, n, r, g_ref[...])
        dg_ref[...] += dg
        dfb = df.astype(BF16)
        df_ref[...] = dfb

        for j in range(2):
            jc = slice(j * CW, (j + 1) * CW)
            dyv = _dot_nt(dfb, wd_ref[jc, :])
            for lo, hi in pieces:
                pc = slice(j * CW + lo, j * CW + hi)
                for half, scale_s in ((0, dgate_s), (1, dval_s)):
                    cols = slice(half * FF + j * CW + lo, half * FF + j * CW + hi)
                    dcv = dyv[:, lo:hi] * scale_s[:, pc]
                    dc_ref[:, cols] = dcv.astype(BF16)
                    dcb_ref[:, cols] += jnp.sum(dcv, axis=0, keepdims=True)
                    dcw_ref[2, :, cols] += jnp.sum(dcv * ue_s[CONV_HALO:, cols], axis=0, keepdims=True)
                    dcw_ref[1, :, cols] += jnp.sum(dcv * shifted(cols, 1), axis=0, keepdims=True)
                    dcw_ref[0, :, cols] += jnp.sum(dcv * shifted(cols, 2), axis=0, keepdims=True)

    row = lambda w: pl.BlockSpec((TM, w), lambda i: (i, 0))
    acc = lambda shape: pl.BlockSpec(shape, lambda i: (0,) * len(shape))
    return pl.pallas_call(
        body, name="ffn_fwd", grid=(S // TM,),
        in_specs=[row(D), row(D), row(D), _const_spec(w_up.shape), _const_spec(w_down.shape),
                  _const_spec(conv_w.shape), _const_spec((1, 2 * FF)), _const_spec((1, D))],
        out_specs=[row(FF), row(D), row(D), row(2 * FF),
                   acc((1, 1)), acc((1, D)), acc((1, 2 * FF)), acc((CONV_WIDTH, 1, 2 * FF))],
        out_shape=[jax.ShapeDtypeStruct((S, FF), BF16),
                   jax.ShapeDtypeStruct((S, D), F32), jax.ShapeDtypeStruct((S, D), BF16),
                   jax.ShapeDtypeStruct((S, 2 * FF), BF16),
                   jax.ShapeDtypeStruct((1, 1), F32), jax.ShapeDtypeStruct((1, D), F32),
                   jax.ShapeDtypeStruct((1, 2 * FF), F32), jax.ShapeDtypeStruct((CONV_WIDTH, 1, 2 * FF), F32)],
        scratch_shapes=[pltpu.VMEM((TM + CONV_HALO, 2 * FF), F32), pltpu.VMEM((TM, FF), F32),
                        pltpu.VMEM((TM, FF), F32)],
        compiler_params=_params(("arbitrary",), VMEM_LIMIT),
    )(h2, x1, target, w_up, w_down, conv_w, conv_b, g_post)


def _ffn_bwd(dc, conv_w, w_up, x1, g_ffn_pre, dy):
    S, D = x1.shape
    CW = w_up.shape[2]
    F2 = 4 * CW
    TM = 256
    HB = TM // CONV_HALO
    last = S // CONV_HALO - 1
    n_tiles = S // TM

    def body(dc_ref, halo_ref, cw_ref, wu_ref, x1_ref, g_ref, dy_ref, du_ref, dx1_ref, dg_ref):
        i = pl.program_id(0)

        @pl.when(i == 0)
        def _():
            dg_ref[...] = jnp.zeros_like(dg_ref)

        keep = i < n_tiles - 1
        dh2 = jnp.zeros((TM, D), F32)
        for blk in range(N_SHARD):
            cols = slice(blk * CW, (blk + 1) * CW)
            halo = jnp.where(keep, halo_ref[:, cols].astype(F32), 0.0)
            dce = jnp.concatenate([dc_ref[:, cols].astype(F32), halo], axis=0)
            rows = TM + CONV_HALO
            du = (cw_ref[2, :, cols] * dce[:TM]
                  + cw_ref[1, :, cols] * pltpu.roll(dce, rows - 1, 0)[:TM]
                  + cw_ref[0, :, cols] * pltpu.roll(dce, rows - 2, 0)[:TM])
            dub = du.astype(BF16)
            du_ref[:, cols] = dub
            dh2 = dh2 + _dot_nt(dub, wu_ref[blk])
        n2, r2 = _rms_stats(x1_ref[...])
        dx, dg = _rms_bwd(dh2, n2, r2, g_ref[...])
        dg_ref[...] += dg
        dx1_ref[...] = dy_ref[...] + dx

    row = lambda w: pl.BlockSpec((TM, w), lambda i: (i, 0))
    return pl.pallas_call(
        body, name="ffn_bwd", grid=(S // TM,),
        in_specs=[row(F2), pl.BlockSpec((CONV_HALO, F2), lambda i: (jnp.minimum((i + 1) * HB, last), 0)),
                  _const_spec(conv_w.shape), _const_spec(w_up.shape), row(D), _const_spec((1, D)), row(D)],
        out_specs=[row(F2), row(D), pl.BlockSpec((1, D), lambda i: (0, 0))],
        out_shape=[jax.ShapeDtypeStruct((S, F2), BF16), jax.ShapeDtypeStruct((S, D), F32),
                   jax.ShapeDtypeStruct((1, D), F32)],
        compiler_params=_params(("arbitrary",), VMEM_LIMIT),
    )(dc, dc, conv_w, w_up, x1, g_ffn_pre, dy)


def _matmul_tn(a, b, n_blocks, name):
    S, M = a.shape
    N = b.shape[1]
    tn = N // n_blocks
    tm = M if M <= 1024 else M // 2
    tk = 2048
    nk = S // tk

    def body(a_ref, b_ref, o_ref):
        @pl.when(pl.program_id(2) == 0)
        def _():
            o_ref[...] = jnp.zeros_like(o_ref)
        o_ref[0] += _dot_tn(a_ref[...], b_ref[...])

    return pl.pallas_call(
        body, name=name, grid=(M // tm, n_blocks, nk),
        in_specs=[pl.BlockSpec((tk, tm), lambda i, j, k: (k, i)), pl.BlockSpec((tk, tn), lambda i, j, k: (k, j))],
        out_specs=pl.BlockSpec((1, tm, tn), lambda i, j, k: (j, i, 0)),
        out_shape=jax.ShapeDtypeStruct((n_blocks, M, tn), F32),
        compiler_params=_params(("parallel", "parallel", "arbitrary"), VMEM_LIMIT),
    )(a, b)


def _mix_out_bwd(dx1, mixed, g_post, w_out, attn, cargo=()):
    S, D = dx1.shape
    TM = 512
    nd = len(DILATIONS)
    nc = len(cargo)
    kinds = [kind for kind, _ in cargo]
    n_chunks = ATTN_WIDTH // LANES

    def body(*refs):
        dx_ref, m_ref, g_ref, w_ref, a_ref = refs[:5]
        dm_ref, dp_ref, dg_ref = refs[5 + nc:8 + nc]
        da_refs, dl_refs = refs[8 + nc:8 + nc + nd], refs[8 + nc + nd:8 + nc + 2 * nd]
        n_out = 8 + nc + 2 * nd
        t_s = refs[n_out + nc:n_out + nc + n_chunks]
        c_s = refs[n_out + nc + n_chunks:n_out + nc + n_chunks + 1]
        cargo_refs = (kinds, refs[5:5 + nc], refs[n_out:n_out + nc], refs[n_out + nc + n_chunks + 1:])
        if nc:
            _cargo_start(*cargo_refs, pl.program_id(0) == 0)

        @pl.when(pl.program_id(0) == 0)
        def _():
            dg_ref[...] = jnp.zeros_like(dg_ref)

        n, r = _rms_stats(m_ref[...])
        dm, dg = _rms_bwd(dx_ref[...], n, r, g_ref[...])
        dg_ref[...] += dg
        dmb = dm.astype(BF16)
        dm_ref[...] = dmb
        da = _dot_nt(dmb, w_ref[:ATTN_WIDTH, :])
        _put_tokens(t_s, da)
        for i, d in enumerate(DILATIONS):
            _to_residue(da, t_s, da_refs[i], d, BF16)
        dp_ref[...] = _dot_nt(dmb, w_ref[ATTN_WIDTH:, :])
        gather = _head_stat_matrix(pick_first_lane=False)
        delta = sum(_dot(p, gather) for p in _bf16_pieces(da * a_ref[...], 2))
        _put_tokens(c_s, delta)
        for i, d in enumerate(DILATIONS):
            _to_residue(delta, c_s, dl_refs[i], d, F32)
        if nc:
            _cargo_finish(*cargo_refs, pl.program_id(0) == S // TM - 1)

    row = lambda w: pl.BlockSpec((TM, w), lambda i: (i, 0))
    specs = [_residue_spec(TM, d) for d in DILATIONS]
    arrays, cargo_specs, shapes, aliases, sems = _cargo_call(cargo, 5, 3 + 2 * nd)
    out = pl.pallas_call(
        body, name="mix_out_bwd", grid=(S // TM,),
        in_specs=[row(D), row(D), _const_spec((1, D)), _const_spec(w_out.shape), row(ATTN_WIDTH)] + cargo_specs,
        out_specs=[row(D), row(POOL_WIDTH), pl.BlockSpec((1, D), lambda i: (0, 0))] + specs
        + [_residue_spec(TM, d, STAT_WIDTH) for d in DILATIONS] + cargo_specs,
        out_shape=[jax.ShapeDtypeStruct((S, D), BF16), jax.ShapeDtypeStruct((S, POOL_WIDTH), F32),
                   jax.ShapeDtypeStruct((1, D), F32)]
        + [_residue_shape(S, d, BF16) for d in DILATIONS]
        + [_residue_shape(S, d, F32, STAT_WIDTH) for d in DILATIONS] + shapes,
        input_output_aliases=aliases,
        scratch_shapes=_token_scratch(TM) + _token_scratch(TM, STAT_WIDTH) + sems,
        compiler_params=_params(("arbitrary",), VMEM_LIMIT),
    )(dx1, mixed, g_post, w_out, attn, *arrays)
    return out[0], out[1], out[2], out[3:3 + nd], out[3 + nd:3 + 2 * nd], out[3 + 2 * nd:]


def _pool_bwd(pool_in, d_pool, pool_w, pool_scale):
    S = pool_in.shape[0]
    TM = 512
    HB = TM // POOL_HALO
    last = S // POOL_HALO - 1
    G = len(POOL_WINDOWS)

    def body(cur_ref, halo_ref, dcur_ref, dnext_ref, w_ref, sc_ref, dxin_ref, dw_ref, dsc_ref):
        i = pl.program_id(0)

        @pl.when(i == 0)
        def _():
            dw_ref[...] = jnp.zeros_like(dw_ref)
            dsc_ref[...] = jnp.zeros_like(dsc_ref)

        halo = jnp.where(i > 0, halo_ref[...], 0.0)
        pooled = _pooled_groups(halo, cur_ref[...], i * TM)
        dnext = jnp.where(i < S // TM - 1, dnext_ref[...], 0.0)
        dye = jnp.concatenate([dcur_ref[...], dnext], axis=0)
        for g, w in enumerate(POOL_WINDOWS):
            sl = slice(g * POOL_GROUP, (g + 1) * POOL_GROUP)
            wg = w_ref[g].astype(BF16)
            pb = pooled[g].astype(BF16)
            dsc_ref[:, sl] += jnp.sum(dye[:TM, sl] * _dot(pb, wg), axis=0, keepdims=True)
            dpre = (dye[:, sl] * sc_ref[:, sl]).astype(BF16)
            dw_ref[g] += _dot_tn(pb, dpre[:TM])
            dpooled = _dot_nt(dpre, wg)
            z = dpooled / _pool_counts(i * TM, TM + POOL_HALO, w)
            dxin_ref[:, sl] = (_leading_sums(z, w)[:TM] - dpooled[:TM]).astype(BF16)

    row = pl.BlockSpec((TM, POOL_WIDTH), lambda i: (i, 0))
    return pl.pallas_call(
        body, name="pool_bwd", grid=(S // TM,),
        in_specs=[row, pl.BlockSpec((POOL_HALO, POOL_WIDTH), lambda i: (jnp.maximum(i * HB - 1, 0), 0)),
                  row, pl.BlockSpec((POOL_HALO, POOL_WIDTH), lambda i: (jnp.minimum((i + 1) * HB, last), 0)),
                  _const_spec(pool_w.shape), _const_spec((1, POOL_WIDTH))],
        out_specs=[row, pl.BlockSpec((G, POOL_GROUP, POOL_GROUP), lambda i: (0, 0, 0)),
                   pl.BlockSpec((1, POOL_WIDTH), lambda i: (0, 0))],
        out_shape=[jax.ShapeDtypeStruct((S, POOL_WIDTH), BF16), jax.ShapeDtypeStruct((G, POOL_GROUP, POOL_GROUP), F32),
                   jax.ShapeDtypeStruct((1, POOL_WIDTH), F32)],
        compiler_params=_params(("arbitrary",)),
    )(pool_in, pool_in, d_pool, d_pool, pool_w, pool_scale)


def _attn_bwd(q, k, v, d_attn, lse, delta, d, cargo=()):
    L = q.shape[0]
    nb = L // ATTN_BLOCK
    group = min(d, RESIDUES_PER_STEP)
    width = group * ATTN_WIDTH
    nc = len(cargo)
    kinds = [kind for kind, _ in cargo]

    def body(*refs):
        q_ref, kp_ref, kc_ref, vp_ref, vc_ref, do_ref, lse_ref, dl_ref = refs[:8]
        dq_ref, dk_ref, dv_ref = refs[8 + nc:11 + nc]
        ck_s, cv_s = refs[11 + 2 * nc:13 + 2 * nc]
        cargo_refs = (kinds, refs[8:8 + nc], refs[11 + nc:11 + 2 * nc], refs[13 + 2 * nc:])
        r, n = pl.program_id(0), pl.program_id(1)
        if nc:
            _cargo_start(*cargo_refs, (r == 0) & (n == 0))

        @pl.when(n == 0)
        def _():
            ck_s[...] = jnp.zeros_like(ck_s)
            cv_s[...] = jnp.zeros_like(cv_s)

        @pl.when(n < nb)
        def _():
            valid = _band_mask(n)
            valid2 = jnp.concatenate([valid, valid], axis=0)
            first = _first_head_lanes()

            def stacked_column(ref, hp):
                lane = 2 * hp * STAT_LANES
                return jnp.concatenate([ref[:, lane:lane + 1], ref[:, lane + STAT_LANES:lane + STAT_LANES + 1]], axis=0)

            for hp in range(width // LANES):
                sl = slice(hp * LANES, (hp + 1) * LANES)
                qq = _stack_heads(q_ref[:, sl], first)
                dd = _stack_heads(do_ref[:, sl], first)
                kk = jnp.concatenate([kp_ref[:, sl], kc_ref[:, sl]], axis=0)
                vv = jnp.concatenate([vp_ref[:, sl], vc_ref[:, sl]], axis=0)
                s = _dot_nt(qq, kk)
                p = jnp.where(valid2, jnp.exp(s - stacked_column(lse_ref, hp)), 0.0)
                dp = _dot_nt(dd, vv)
                ds = (p * (dp - stacked_column(dl_ref, hp))).astype(BF16)
                dq_ref[:, sl] = (_unstack_heads(_dot(ds, kk), first) * ATTN_SCALE).astype(BF16)
                dk = _dot_tn(ds, qq)
                dv = _dot_tn(p.astype(BF16), dd)
                dk_ref[:, sl] = (ck_s[:, sl] + dk[:ATTN_BLOCK]).astype(BF16)
                dv_ref[:, sl] = (cv_s[:, sl] + dv[:ATTN_BLOCK]).astype(BF16)
                ck_s[:, sl] = dk[ATTN_BLOCK:]
                cv_s[:, sl] = dv[ATTN_BLOCK:]

        @pl.when(n == nb)
        def _():
            dk_ref[...] = ck_s[...].astype(BF16)
            dv_ref[...] = cv_s[...].astype(BF16)

        if nc:
            _cargo_finish(*cargo_refs, (r == d // group - 1) & (n == nb))

    blk = (ATTN_BLOCK, width)
    cur = pl.BlockSpec(blk, lambda r, n: (jnp.minimum(n, nb - 1), r))
    stat = pl.BlockSpec((ATTN_BLOCK, group * STAT_WIDTH), lambda r, n: (jnp.minimum(n, nb - 1), r))
    prev = pl.BlockSpec(blk, lambda r, n: (jnp.maximum(jnp.minimum(n, nb - 1) - 1, 0), r))
    done = pl.BlockSpec(blk, lambda r, n: (jnp.maximum(n - 1, 0), r))
    arrays, specs, shapes, aliases, sems = _cargo_call(cargo, 8, 3)
    out = pl.pallas_call(
        body, name=f"attn_bwd_d{d}", grid=(d // group, nb + 1),
        in_specs=[cur, prev, cur, prev, cur, cur, stat, stat] + specs, out_specs=[cur, done, done] + specs,
        out_shape=[jax.ShapeDtypeStruct((L, d * ATTN_WIDTH), BF16)] * 3 + shapes,
        input_output_aliases=aliases,
        scratch_shapes=[pltpu.VMEM(blk, F32), pltpu.VMEM(blk, F32)] + sems,
        compiler_params=_params(("arbitrary", "arbitrary")),
    )(q, k, k, v, v, d_attn, lse, delta, *arrays)
    return out[:3], out[3:]


def _attn_bwd_consecutive(q, k, v, d_attn, lse, delta, cargo=()):
    L = q.shape[0]
    qb = CONSECUTIVE_BLOCKS
    steps = L // (qb * ATTN_BLOCK)
    nc = len(cargo)
    kinds = [kind for kind, _ in cargo]

    def body(*refs):
        q_ref, kp_ref, kc_ref, vp_ref, vc_ref, do_ref, lse_ref, dl_ref = refs[:8]
        dq_ref, dk_ref, dv_ref, ek_ref, ev_ref = refs[8 + nc:13 + nc]
        cargo_refs = (kinds, refs[8:8 + nc], refs[13 + nc:13 + 2 * nc], refs[13 + 2 * nc:])
        n = pl.program_id(0)
        if nc:
            _cargo_start(*cargo_refs, n == 0)
        first = _first_head_lanes()
        for hp in range(ATTN_WIDTH // LANES):
            sl = slice(hp * LANES, (hp + 1) * LANES)
            for sub in range(qb):
                rows = slice(sub * ATTN_BLOCK, (sub + 1) * ATTN_BLOCK)
                valid = _band_mask(n if sub == 0 else 1)
                valid2 = jnp.concatenate([valid, valid], axis=0)
                if sub == 0:
                    kk = jnp.concatenate([kp_ref[:, sl], kc_ref[rows, sl]], axis=0)
                    vv = jnp.concatenate([vp_ref[:, sl], vc_ref[rows, sl]], axis=0)
                else:
                    keys = slice((sub - 1) * ATTN_BLOCK, (sub + 1) * ATTN_BLOCK)
                    kk, vv = kc_ref[keys, sl], vc_ref[keys, sl]
                qq = _stack_heads(q_ref[rows, sl], first)
                dd = _stack_heads(do_ref[rows, sl], first)
                lane = 2 * hp * STAT_LANES
                column = lambda ref: jnp.concatenate(
                    [ref[rows, lane:lane + 1], ref[rows, lane + STAT_LANES:lane + STAT_LANES + 1]], axis=0)
                p = jnp.where(valid2, jnp.exp(_dot_nt(qq, kk) - column(lse_ref)), 0.0)
                ds = (p * (_dot_nt(dd, vv) - column(dl_ref))).astype(BF16)
                dq_ref[rows, sl] = (_unstack_heads(_dot(ds, kk), first) * ATTN_SCALE).astype(BF16)
                dk = _dot_tn(ds, qq)
                dv = _dot_tn(p.astype(BF16), dd)
                if sub == 0:
                    ek_ref[:, sl] = dk[:ATTN_BLOCK].astype(BF16)
                    ev_ref[:, sl] = dv[:ATTN_BLOCK].astype(BF16)
                else:
                    before = slice((sub - 1) * ATTN_BLOCK, sub * ATTN_BLOCK)
                    dk_ref[before, sl] = (carry_k + dk[:ATTN_BLOCK]).astype(BF16)
                    dv_ref[before, sl] = (carry_v + dv[:ATTN_BLOCK]).astype(BF16)
                carry_k, carry_v = dk[ATTN_BLOCK:], dv[ATTN_BLOCK:]
            dk_ref[rows, sl] = carry_k.astype(BF16)
            dv_ref[rows, sl] = carry_v.astype(BF16)
        if nc:
            _cargo_finish(*cargo_refs, n == steps - 1)

    cur = pl.BlockSpec((qb * ATTN_BLOCK, ATTN_WIDTH), lambda n: (n, 0))
    prev = pl.BlockSpec((ATTN_BLOCK, ATTN_WIDTH), lambda n: (jnp.maximum(n * qb - 1, 0), 0))
    edge = pl.BlockSpec((ATTN_BLOCK, ATTN_WIDTH), lambda n: (n, 0))
    stat = pl.BlockSpec((qb * ATTN_BLOCK, STAT_WIDTH), lambda n: (n, 0))
    arrays, specs, shapes, aliases, sems = _cargo_call(cargo, 8, 5)
    out = pl.pallas_call(
        body, name="attn_bwd_d1", grid=(steps,),
        in_specs=[cur, prev, cur, prev, cur, cur, stat, stat] + specs, out_specs=[cur, cur, cur, edge, edge] + specs,
        out_shape=[jax.ShapeDtypeStruct((L, ATTN_WIDTH), BF16)] * 3
        + [jax.ShapeDtypeStruct((steps * ATTN_BLOCK, ATTN_WIDTH), BF16)] * 2 + shapes,
        input_output_aliases=aliases, scratch_shapes=sems,
        compiler_params=_params(("arbitrary",)),
    )(q, k, k, v, v, d_attn, lse, delta, *arrays)
    return out[:3], out[3:5], out[5:]


def _mix_in_bwd(dqkv, edges, d_pool_in, w_in, x, g_pre, dx1):
    S, D = x.shape
    TM = CONSECUTIVE_BLOCKS * ATTN_BLOCK
    nd = len(DILATIONS)
    n_tiles = S // TM

    def body(*refs):
        g_refs = refs[:3 * nd]
        e_refs = (None,) + refs[3 * nd:3 * nd + 2]
        dpi_ref, w_ref, x_ref, g_ref, dx1_ref, dproj_ref, gx_ref, dg_ref = refs[3 * nd + 2:3 * nd + 10]
        t_s = refs[3 * nd + 10:]

        @pl.when(pl.program_id(0) == 0)
        def _():
            dg_ref[...] = jnp.zeros_like(dg_ref)

        dh = jnp.zeros((TM, D), F32)
        for a in range(4):
            if a < 3:
                tot = g_refs[a][...].astype(F32)
                if a > 0:
                    late = jnp.where(pl.program_id(0) < n_tiles - 1, e_refs[a][...].astype(F32), 0.0)
                    tot = jnp.concatenate([tot[:TM - ATTN_BLOCK], tot[TM - ATTN_BLOCK:] + late], axis=0)
                for i, d in enumerate(DILATIONS[1:]):
                    tot = tot + _from_residue(g_refs[3 * (i + 1) + a], t_s, d)
                db = tot.astype(BF16)
            else:
                db = dpi_ref[...]
            dproj_ref[:, a * ATTN_WIDTH:(a + 1) * ATTN_WIDTH] = db
            dh = dh + _dot_nt(db, w_ref[a])
        n, r = _rms_stats(x_ref[...])
        dx, dg = _rms_bwd(dh, n, r, g_ref[...])
        dg_ref[...] += dg
        gx_ref[...] = dx1_ref[...] + dx

    row = lambda w: pl.BlockSpec((TM, w), lambda i: (i, 0))
    edge = pl.BlockSpec((ATTN_BLOCK, ATTN_WIDTH), lambda i: (jnp.minimum(i + 1, n_tiles - 1), 0))
    return pl.pallas_call(
        body, name="mix_in_bwd", grid=(S // TM,),
        in_specs=[_residue_spec(TM, d) for d in DILATIONS for _ in range(3)] + [edge, edge]
        + [row(POOL_WIDTH), _const_spec(w_in.shape), row(D), _const_spec((1, D)), row(D)],
        out_specs=[row(4 * ATTN_WIDTH), row(D), pl.BlockSpec((1, D), lambda i: (0, 0))],
        out_shape=[jax.ShapeDtypeStruct((S, 4 * ATTN_WIDTH), BF16), jax.ShapeDtypeStruct((S, D), F32),
                   jax.ShapeDtypeStruct((1, D), F32)],
        scratch_shapes=_token_scratch(TM),
        compiler_params=_params(("arbitrary",), VMEM_LIMIT),
    )(*[g for gs in dqkv for g in gs], *edges, d_pool_in, w_in, x, g_pre, dx1)


SMALL_EARLY = ("pool_w", "pool_scale", "g_mix_post", "g_ffn_pre", "conv_b", "g_ffn_post", "conv_w")
SMALL_LATE = ("g_mix_pre",)


def _pack_small(grads, names):
    parts = []
    for n in names:
        g = grads[n]
        if n == "conv_w":
            g = g.reshape(CONV_WIDTH, N_SHARD, -1).transpose(1, 0, 2)
        parts.append(g.reshape(-1, LANES))
    return jnp.concatenate(parts, axis=0) if len(parts) > 1 else parts[0]


def _unpack_small(packed, names, like, shard):
    out, row = {}, 0
    for n in names:
        size = like[n].size * (N_SHARD if n == "conv_w" else 1)
        g = packed[row:row + size // LANES]
        row += size // LANES
        if n == "conv_w":
            g = lax.dynamic_slice_in_dim(g.reshape((N_SHARD,) + like[n].shape), shard, 1, axis=0)[0]
        out[n] = g.reshape(like[n].shape)
    return out


def _local_step(x, target, g_mix_pre, w_in, pool_w, pool_scale, w_out, g_mix_post, g_ffn_pre,
                w_up, conv_w, conv_b, w_down, g_ffn_post, mesh_pos=None):
    on_mesh = mesh_pos is not None
    D = x.shape[1]
    CW = w_up.shape[2]
    qkv, pool_in, h1, got = _mix_in_fwd(x, g_mix_pre, w_in, [("ici", w_up)] if on_mesh else ())
    w_up = got[0] if on_mesh else w_up
    o1, l1, got = _attn_fwd(*qkv[0], 1, [("d2d", w_up), ("ici", w_out), ("ici", w_down)] if on_mesh else ())
    w_up, w_out, w_down = got if on_mesh else (w_up, w_out, w_down)
    o4, l4, got = _attn_fwd(*qkv[1], 4, [("d2d", w_out), ("d2d", w_down)] if on_mesh else ())
    w_out, w_down = got if on_mesh else (w_out, w_down)
    o16, l16, _ = _attn_fwd(*qkv[2], 16)
    w_out = w_out.reshape(D, D)
    w_down = w_down.reshape(2 * CW, D)
    attn, lse = _attn_mix((o1, o4, o16), (l1, l4, l16))
    pool = _pool_fwd(pool_in, pool_w, pool_scale)
    mixed, x1, h2, cat = _mix_out_fwd(attn, pool, w_out, x, g_mix_post, g_ffn_pre)

    yv, dy, df, dc, loss, d_g_ffn_post, d_conv_b, d_conv_w = _ffn_fwd(
        h2, x1, target, w_up, w_down, conv_w, conv_b, g_ffn_post)
    du, dx1, d_g_ffn_pre = _ffn_bwd(dc, conv_w, w_up, x1, g_ffn_pre, dy)
    d_w_up = _matmul_tn(h2, du, N_SHARD, "grad_w_up")
    d_w_down = _matmul_tn(yv, df, 1, "grad_w_down")[0].reshape(N_SHARD, CW // 2, D)
    swap = [("swap", d_w_up), ("swap", d_w_down)] if on_mesh else ()
    d_mixed, d_pool, d_g_mix_post, d_attn, delta, from_sibling = _mix_out_bwd(dx1, mixed, g_mix_post, w_out, attn, swap)
    d_w_out = _matmul_tn(cat, d_mixed, 1, "grad_w_out")[0].reshape(N_SHARD, D // N_SHARD, D)
    d_pool_in, d_pool_w, d_pool_scale = _pool_bwd(pool_in, d_pool, pool_w, pool_scale)
    grads = dict(pool_w=d_pool_w, pool_scale=d_pool_scale, w_out=d_w_out, g_mix_post=d_g_mix_post,
                 g_ffn_pre=d_g_ffn_pre, w_up=d_w_up, conv_w=d_conv_w, conv_b=d_conv_b, w_down=d_w_down,
                 g_ffn_post=d_g_ffn_post)
    cargo = [(), (), ()]
    if on_mesh:
        c_arr, device = mesh_pos
        up_f32, up_bf16 = _pair_sum(d_w_up, from_sibling[0], c_arr, "pair_sum_w_up")
        down_f32, down_bf16 = _pair_sum(d_w_down, from_sibling[1], c_arr, "pair_sum_w_down")
        early = _pack_small(grads, SMALL_EARLY)
        early_slots = lax.dynamic_update_index_in_dim(jnp.zeros((8,) + early.shape, F32), early, device, 0)
        cargo = [[("scatter", down_bf16)], [("scatter", up_bf16)], [("everyone", early_slots)]]

    dqkv1, edges, landed1 = _attn_bwd_consecutive(*qkv[0], d_attn[0], lse[0], delta[0], cargo[0])
    dqkv, landed = zip(*[_attn_bwd(*qkv[i], d_attn[i], lse[i], delta[i], DILATIONS[i], cargo[i]) for i in (1, 2)])
    if on_mesh:
        grads.update(w_down=(down_f32, landed1[0]), w_up=(up_f32, landed[0][0]), small_early=landed[1][0])
    d_proj, grad_x, grads["g_mix_pre"] = _mix_in_bwd((dqkv1,) + dqkv, edges, d_pool_in, w_in, x, g_mix_pre, dx1)
    grads["w_in"] = _matmul_tn(h1, d_proj, N_SHARD, "grad_w_in")
    return loss, grad_x, grads


ANY = pl.BlockSpec(memory_space=pl.ANY)


def _position():
    x, y, c = lax.axis_index("x"), lax.axis_index("y"), lax.axis_index("c")
    chips = [(1 - x, y), (x, 1 - y), (1 - x, 1 - y)]
    return x, y, c, chips


def _remote(src, dst, send_sem, recv_sem, to):
    return pltpu.make_async_remote_copy(src_ref=src, dst_ref=dst, send_sem=send_sem, recv_sem=recv_sem,
                                        device_id=to, device_id_type=MESH)


def _cast_bf16(w, shard_arr, name):
    R, C = w.shape
    tr = R // 2

    def body(s_ref, w_ref, o_ref):
        o_ref[0] = w_ref[...].astype(BF16)

    return pl.pallas_call(
        body, name=name,
        grid_spec=pltpu.PrefetchScalarGridSpec(
            num_scalar_prefetch=1, grid=(2,),
            in_specs=[pl.BlockSpec((tr, C), lambda i, s_ref: (i, 0))],
            out_specs=pl.BlockSpec((1, tr, C), lambda i, s_ref: (s_ref[0], i, 0))),
        out_shape=jax.ShapeDtypeStruct((N_SHARD, R, C), BF16),
        compiler_params=_params(("parallel",)))(shard_arr, w)


def _gather_weights(bufs):
    n = len(bufs) - 1

    def body(*refs):
        outs, cw_out = refs[n + 1:2 * n + 1], refs[2 * n + 1]
        ici_send, ici_recv, d2d_send, d2d_recv = refs[2 * n + 2:]
        x, y, c, chips = _position()
        s = 2 * x + y
        sibling = (x, y, 1 - c)

        def half(a, shard, h):
            rows = outs[a].shape[1] // 2
            return outs[a].at[shard, pl.ds(h * rows, rows), :]

        sends = []
        for a in range(n):
            for j, (px, py) in enumerate(chips):
                sends.append(_remote(half(a, s, c), half(a, s, c),
                                     ici_send.at[3 * a + j], ici_recv.at[3 * a + j], (px, py, c)))
        for j, (px, py) in enumerate(chips):
            sends.append(_remote(cw_out.at[s], cw_out.at[s], ici_send.at[3 * n + j], ici_recv.at[3 * n + j], (px, py, c)))
        for cp in sends:
            cp.start()
        passed = []
        for a in range(n):
            for j, (px, py) in enumerate(chips):
                sj = 2 * px + py
                got = half(a, sj, c)
                _remote(got, got, ici_send.at[3 * a + j], ici_recv.at[3 * a + j], (px, py, c)).wait_recv()
                fwd = _remote(got, got, d2d_send.at[3 * a + j], d2d_recv.at[3 * a + j], sibling)
                fwd.start()
                passed.append(fwd)
        for j, (px, py) in enumerate(chips):
            got = cw_out.at[2 * px + py]
            _remote(got, got, ici_send.at[3 * n + j], ici_recv.at[3 * n + j], (px, py, c)).wait_recv()
        for a in range(n):
            for j, (px, py) in enumerate(chips):
                got = half(a, 2 * px + py, 1 - c)
                _remote(got, got, d2d_send.at[3 * a + j], d2d_recv.at[3 * a + j], sibling).wait_recv()
        for cp in sends + passed:
            cp.wait_send()

    return pl.pallas_call(
        body, name="gather_weights",
        in_specs=[ANY] * (n + 1), out_specs=[ANY] * (n + 1),
        out_shape=[jax.ShapeDtypeStruct(b.shape, b.dtype) for b in bufs],
        input_output_aliases={i: i for i in range(n + 1)},
        scratch_shapes=[pltpu.SemaphoreType.DMA((3 * n + 3,)), pltpu.SemaphoreType.DMA((3 * n + 3,)),
                        pltpu.SemaphoreType.DMA((3 * n,)), pltpu.SemaphoreType.DMA((3 * n,))],
        compiler_params=pltpu.CompilerParams(has_side_effects=True),
    )(*bufs)


def _swap_halves(grads, tag):
    n = len(grads)

    def body(*refs):
        ins, outs, send_sem, recv_sem = refs[:n], refs[n:2 * n], refs[2 * n], refs[2 * n + 1]
        x, y, c, _ = _position()
        copies = []
        for a in range(n):
            rows = ins[a].shape[1] // 2
            copies.append(_remote(ins[a].at[:, pl.ds((1 - c) * rows, rows), :], outs[a],
                                  send_sem.at[a], recv_sem.at[a], (x, y, 1 - c)))
        for cp in copies:
            cp.start()
        for cp in copies:
            cp.wait()

    return pl.pallas_call(
        body, name="swap_grad_halves_" + tag,
        in_specs=[ANY] * n, out_specs=[ANY] * n,
        out_shape=[jax.ShapeDtypeStruct((g.shape[0], g.shape[1] // 2, g.shape[2]), F32) for g in grads],
        scratch_shapes=[pltpu.SemaphoreType.DMA((n,)), pltpu.SemaphoreType.DMA((n,))],
        compiler_params=pltpu.CompilerParams(has_side_effects=True),
    )(*grads)


def _pair_sum(g, got, c_arr, name):
    n_sh, R, C = g.shape
    rows = R // 2

    def body(c_ref, g_ref, r_ref, f_ref, b_ref):
        t = g_ref[...] + r_ref[...]
        f_ref[...] = t
        b_ref[...] = t.astype(BF16)

    blk = pl.BlockSpec((1, rows, C), lambda i, c_ref: (i, 0, 0))
    return pl.pallas_call(
        body, name=name,
        grid_spec=pltpu.PrefetchScalarGridSpec(
            num_scalar_prefetch=1, grid=(n_sh,),
            in_specs=[pl.BlockSpec((1, rows, C), lambda i, c_ref: (i, c_ref[0], 0)), blk],
            out_specs=[blk, blk]),
        out_shape=[jax.ShapeDtypeStruct((n_sh, rows, C), F32), jax.ShapeDtypeStruct((n_sh, rows, C), BF16)],
        compiler_params=_params(("parallel",)),
    )(c_arr, g, got)


def _scatter_grads(sums_bf16, small_all):
    n = len(sums_bf16)

    def body(*refs):
        b_ins = refs[:n]
        recvs, sm = refs[n + 1:2 * n + 1], refs[2 * n + 1]
        ici_send, ici_recv, sm_send, sm_recv = refs[2 * n + 2:]
        x, y, c, chips = _position()
        me = 4 * x + 2 * y + c
        copies = []
        for a in range(n):
            for j, (px, py) in enumerate(chips):
                copies.append(_remote(b_ins[a].at[2 * px + py], recvs[a].at[j],
                                      ici_send.at[3 * a + j], ici_recv.at[3 * a + j], (px, py, c)))
        for k in range(1, 8):
            peer = (x ^ (k >> 2), y ^ ((k >> 1) & 1), c ^ (k & 1))
            copies.append(_remote(sm.at[me], sm.at[me], sm_send.at[k - 1], sm_recv.at[k - 1], peer))
        for cp in copies:
            cp.start()
        for cp in copies:
            cp.wait_send()
        for a in range(n):
            for j, (px, py) in enumerate(chips):
                _remote(recvs[a].at[j], recvs[a].at[j], ici_send.at[3 * a + j], ici_recv.at[3 * a + j],
                        (px, py, c)).wait_recv()
        for k in range(1, 8):
            peer = (x ^ (k >> 2), y ^ ((k >> 1) & 1), c ^ (k & 1))
            theirs = sm.at[4 * peer[0] + 2 * peer[1] + peer[2]]
            _remote(theirs, theirs, sm_send.at[k - 1], sm_recv.at[k - 1], peer).wait_recv()

    out = pl.pallas_call(
        body, name="scatter_grads",
        in_specs=[ANY] * (n + 1), out_specs=[ANY] * (n + 1),
        out_shape=[jax.ShapeDtypeStruct((3,) + b.shape[1:], BF16) for b in sums_bf16]
        + [jax.ShapeDtypeStruct(small_all.shape, F32)],
        input_output_aliases={n: n},
        scratch_shapes=[pltpu.SemaphoreType.DMA((3 * n,)), pltpu.SemaphoreType.DMA((3 * n,)),
                        pltpu.SemaphoreType.DMA((7,)), pltpu.SemaphoreType.DMA((7,))],
        compiler_params=pltpu.CompilerParams(has_side_effects=True),
    )(*sums_bf16, small_all)
    return out[:n], out[n]


def _shard_sum(sums_f32, recv, shard_arr, c_arr, name):
    _, rows, C = sums_f32.shape

    def body(s_ref, c_ref, o_ref, r_ref, t_ref):
        t_ref[...] = ((o_ref[0] + r_ref[0].astype(F32)) + r_ref[1].astype(F32)) + r_ref[2].astype(F32)

    return pl.pallas_call(
        body, name=name,
        grid_spec=pltpu.PrefetchScalarGridSpec(
            num_scalar_prefetch=2, grid=(1,),
            in_specs=[pl.BlockSpec((1, rows, C), lambda i, s_ref, c_ref: (s_ref[0], 0, 0)),
                      pl.BlockSpec((3, rows, C), lambda i, s_ref, c_ref: (0, 0, 0))],
            out_specs=pl.BlockSpec((rows, C), lambda i, s_ref, c_ref: (c_ref[0], 0))),
        out_shape=jax.ShapeDtypeStruct((2 * rows, C), F32),
        compiler_params=_params(("arbitrary",)),
    )(shard_arr, c_arr, sums_f32, recv)


def _join_halves(bufs):
    n = len(bufs)

    def body(*refs):
        outs, send_sem, recv_sem = refs[n:2 * n], refs[2 * n], refs[2 * n + 1]
        x, y, c, _ = _position()
        copies = []
        for a in range(n):
            rows = outs[a].shape[0] // 2
            mine = outs[a].at[pl.ds(c * rows, rows), :]
            copies.append(_remote(mine, mine, send_sem.at[a], recv_sem.at[a], (x, y, 1 - c)))
        for cp in copies:
            cp.start()
        for a, cp in enumerate(copies):
            cp.wait_send()
            rows = outs[a].shape[0] // 2
            theirs = outs[a].at[pl.ds((1 - c) * rows, rows), :]
            _remote(theirs, theirs, send_sem.at[a], recv_sem.at[a], (x, y, 1 - c)).wait_recv()

    return pl.pallas_call(
        body, name="join_grad_halves",
        in_specs=[ANY] * n, out_specs=[ANY] * n,
        out_shape=[jax.ShapeDtypeStruct(b.shape, F32) for b in bufs],
        input_output_aliases={i: i for i in range(n)},
        scratch_shapes=[pltpu.SemaphoreType.DMA((n,)), pltpu.SemaphoreType.DMA((n,))],
        compiler_params=pltpu.CompilerParams(has_side_effects=True),
    )(*bufs)


def _small_sum(parts, tag):
    _, R, C = parts.shape

    def body(p_ref, o_ref):
        t = p_ref[0]
        for k in range(1, 8):
            t = t + p_ref[k]
        o_ref[...] = t

    return pl.pallas_call(
        body, name="small_grad_sum_" + tag, grid=(1,),
        in_specs=[pl.BlockSpec((8, R, C), lambda i: (0, 0, 0))], out_specs=pl.BlockSpec((R, C), lambda i: (0, 0)),
        out_shape=jax.ShapeDtypeStruct((R, C), F32), compiler_params=_params(("arbitrary",)),
    )(parts)


def _adamw_math(w, g, m, v):
    m = ADAM_B1 * m + (1.0 - ADAM_B1) * g
    v = ADAM_B2 * v + (1.0 - ADAM_B2) * (g * g)
    m_hat = m / (1.0 - ADAM_B1 ** ADAM_STEP)
    v_hat = v / (1.0 - ADAM_B2 ** ADAM_STEP)
    delta = -ADAM_LR * (m_hat / (jnp.sqrt(v_hat) + ADAM_EPS) + ADAM_WD * w)
    return delta, m, v


def _adamw_big(w, g, m, v, name):
    R, C = w.shape
    tr = R // 4

    def body(w_ref, g_ref, m_ref, v_ref, d_ref, nm_ref, nv_ref):
        d_ref[...], nm_ref[...], nv_ref[...] = _adamw_math(w_ref[...], g_ref[...], m_ref[...], v_ref[...])

    blk = pl.BlockSpec((tr, C), lambda i: (i, 0))
    return pl.pallas_call(
        body, name=name, grid=(4,), in_specs=[blk] * 4, out_specs=[blk] * 3,
        out_shape=[jax.ShapeDtypeStruct((R, C), F32)] * 3, compiler_params=_params(("parallel",)),
    )(w, g, m, v)


def _adamw_small(ws, gs, ms, vs):
    n = len(ws)

    def body(*refs):
        for a in range(n):
            w, g, m, v = (refs[k * n + a][...] for k in range(4))
            d, nm, nv = _adamw_math(w, g, m, v)
            refs[4 * n + a][...] = d
            refs[5 * n + a][...] = nm
            refs[6 * n + a][...] = nv

    shapes = [jax.ShapeDtypeStruct(w.shape, F32) for w in ws]
    out = pl.pallas_call(body, name="adamw_small", out_shape=shapes * 3)(*ws, *gs, *ms, *vs)
    return out[:n], out[n:2 * n], out[2 * n:]


BIG = ("w_in", "w_out", "w_up", "w_down")
SMALL = ("g_mix_pre", "pool_w", "pool_scale", "g_mix_post", "g_ffn_pre", "conv_b", "g_ffn_post", "conv_w")
ORDER = ("g_mix_pre", "w_in", "pool_w", "pool_scale", "w_out", "g_mix_post", "g_ffn_pre", "w_up", "conv_w", "conv_b",
         "w_down", "g_ffn_post")


def kernel(x, g_mix_pre, w_in, pool_w, pool_scale, w_out, g_mix_post, g_ffn_pre, w_up, conv_w, conv_b, w_down, g_ffn_post, loss_target, m_g_mix_pre, m_w_in, m_pool_w, m_pool_scale, m_w_out, m_g_mix_post, m_g_ffn_pre, m_w_up, m_conv_w, m_conv_b, m_w_down, m_g_ffn_post, v_g_mix_pre, v_w_in, v_pool_w, v_pool_scale, v_w_out, v_g_mix_post, v_g_ffn_pre, v_w_up, v_conv_w, v_conv_b, v_w_down, v_g_ffn_post):
    args = dict(locals())
    W = {n: args[n][0] for n in ORDER}
    M = {n: args["m_" + n][0] for n in ORDER}
    V = {n: args["v_" + n][0] for n in ORDER}
    for d in (W, M, V):
        d["pool_w"] = d["pool_w"].reshape(-1, POOL_GROUP)
        for n in ("g_mix_pre", "pool_scale", "g_mix_post", "g_ffn_pre", "conv_b", "g_ffn_post"):
            d[n] = d[n].reshape(1, -1)
    CW = W["w_up"].shape[1]
    c_arr = lax.axis_index("c").astype(jnp.int32).reshape(1)
    shard = 2 * lax.axis_index("x") + lax.axis_index("y")
    shard_arr = shard.astype(jnp.int32).reshape(1)
    device = 2 * shard + lax.axis_index("c")

    conv_w_slots = lax.dynamic_update_index_in_dim(jnp.zeros((N_SHARD,) + W["conv_w"].shape, F32), W["conv_w"], shard, 0)
    slots = {n: _cast_bf16(W[n], shard_arr, "cast_" + n) for n in BIG}
    w_in_g, conv_w_g = _gather_weights([slots["w_in"], conv_w_slots])
    conv_w_full = conv_w_g.transpose(1, 0, 2).reshape(CONV_WIDTH, 1, N_SHARD * CW)

    loss, grad_x, G = _local_step(
        x[0], loss_target[0], W["g_mix_pre"], w_in_g, W["pool_w"].reshape(-1, POOL_GROUP, POOL_GROUP), W["pool_scale"],
        slots["w_out"], W["g_mix_post"], W["g_ffn_pre"], slots["w_up"], conv_w_full, W["conv_b"],
        slots["w_down"], W["g_ffn_post"], (c_arr, device))

    late = ("w_in", "w_out")
    from_sibling = _swap_halves([G[n] for n in late], "mix")
    sums = {n: _pair_sum(G[n], r, c_arr, "pair_sum_" + n) for n, r in zip(late, from_sibling)}
    loss_rows = jnp.pad(loss, ((0, 7), (0, LANES - 1)))
    small = jnp.concatenate([_pack_small(G, SMALL_LATE), loss_rows], axis=0)
    small_slots = lax.dynamic_update_index_in_dim(jnp.zeros((8,) + small.shape, F32), small, device, 0)
    recvs, small_all = _scatter_grads([sums[n][1] for n in late], small_slots)
    reduced = {n: (sums[n][0], r) for n, r in zip(late, recvs)}
    reduced.update({n: G[n] for n in ("w_up", "w_down")})
    halves = [_shard_sum(*reduced[n], shard_arr, c_arr, "shard_sum_" + n) for n in BIG]
    full = dict(zip(BIG, _join_halves(halves)))
    full.update(_unpack_small(_small_sum(G["small_early"], "early"), SMALL_EARLY, W, shard))
    late_total = _small_sum(small_all, "late")
    full.update(_unpack_small(late_total, SMALL_LATE, W, shard))
    loss = late_total[-8, 0]

    delta, new_m, new_v = {}, {}, {}
    for n in BIG:
        delta[n], new_m[n], new_v[n] = _adamw_big(W[n], full[n], M[n], V[n], "adamw_" + n)
    ds, nms, nvs = _adamw_small([W[n] for n in SMALL], [full[n] for n in SMALL], [M[n] for n in SMALL],
                                [V[n] for n in SMALL])
    for n, d, nm, nv in zip(SMALL, ds, nms, nvs):
        delta[n], new_m[n], new_v[n] = d, nm, nv

    shaped = lambda d: [d[n].reshape(args[n].shape) for n in ORDER]
    return (loss, grad_x[None], *shaped(full), *shaped(delta), *shaped(new_m), *shaped(new_v))
```

```python
import functools

import jax
import jax.numpy as jnp
from jax import lax
from jax.experimental import pallas as pl
from jax.experimental.pallas import tpu as pltpu

F32 = jnp.float32
BF16 = jnp.bfloat16

RMS_EPS = 1e-6
NEG_INF = -1e30
N_HEADS = 8
HEAD_DIM = 64
ATTN_WIDTH = N_HEADS * HEAD_DIM
ATTN_SCALE = HEAD_DIM ** -0.5
ATTN_BLOCK = 128
DILATIONS = (1, 4, 16)
RESIDUES_PER_STEP = 4
CONSECUTIVE_BLOCKS = 4
POOL_WINDOWS = (2, 4, 8, 16)
POOL_GROUP = 128
POOL_WIDTH = POOL_GROUP * len(POOL_WINDOWS)
POOL_HALO = 16
CONV_WIDTH = 3
CONV_HALO = 8
N_SHARD = 4
LANES = 128
STAT_LANES = 16
STAT_WIDTH = N_HEADS * STAT_LANES

ADAM_LR = 0.001
ADAM_B1 = 0.9
ADAM_B2 = 0.999
ADAM_EPS = 1e-08
ADAM_WD = 0.01
ADAM_STEP = 10

VMEM_LIMIT = 60 * 1024 * 1024
MESH = pl.DeviceIdType.MESH
NT = (((1,), (1,)), ((), ()))
TN = (((0,), (0,)), ((), ()))


def _params(sem, vmem=None):
    return pltpu.CompilerParams(dimension_semantics=sem, vmem_limit_bytes=vmem)


def _const_spec(shape):
    zeros = (0,) * len(shape)
    return pl.BlockSpec(shape, lambda *_: zeros, pipeline_mode=pl.Buffered(1))


def _dot(a, b):
    return jnp.dot(a, b, preferred_element_type=F32)


def _dot_nt(a, b):
    return lax.dot_general(a, b, NT, preferred_element_type=F32)


def _dot_tn(a, b):
    return lax.dot_general(a, b, TN, preferred_element_type=F32)


def _rms_stats(x):
    r = lax.rsqrt(jnp.mean(x * x, axis=-1, keepdims=True) + RMS_EPS)
    return x * r, r


def _rms_bwd(dy, n, r, g):
    dg = jnp.sum(dy * n, axis=0, keepdims=True)
    dn = dy * g
    dx = r * (dn - n * jnp.mean(dn * n, axis=-1, keepdims=True))
    return dx, dg


def _gelu_tanh(g):
    k = 0.7978845608028654
    kc = k * 0.044715
    g2 = g * g
    t = jnp.tanh(g * (k + kc * g2))
    h = 0.5 * t + 0.5
    dh = (0.5 - 0.5 * (t * t)) * (k + (3.0 * kc) * g2)
    return g * h, h + g * dh


def _residue_shape(S, d, dtype, width=ATTN_WIDTH):
    return jax.ShapeDtypeStruct((S // d, d * width), dtype)


def _residue_spec(TM, d, width=ATTN_WIDTH):
    return pl.BlockSpec((TM // d, d * width), lambda i: (i, 0))


def _token_scratch(TM, width=ATTN_WIDTH):
    return [pltpu.VMEM((TM, LANES), F32)] * (width // LANES)


def _head_stat_matrix(pick_first_lane):
    r = lax.broadcasted_iota(jnp.int32, (ATTN_WIDTH, STAT_WIDTH), 0)
    c = lax.broadcasted_iota(jnp.int32, (ATTN_WIDTH, STAT_WIDTH), 1) // STAT_LANES
    return ((r == c * HEAD_DIM) if pick_first_lane else (r // HEAD_DIM == c)).astype(BF16)


def _bf16_pieces(x, n):
    pieces = []
    for _ in range(n):
        p = x.astype(BF16)
        pieces.append(p)
        x = x - p.astype(F32)
    return pieces


def _put_tokens(dst_s, val):
    for cb, chunk in enumerate(dst_s):
        chunk[...] = val[:, cb * LANES:(cb + 1) * LANES]


def _get_tokens(src_s):
    return jnp.concatenate([chunk[...] for chunk in src_s], axis=1)


def _to_residue(val, src_s, out_ref, d, dtype):
    if d == 1:
        out_ref[...] = val.astype(dtype)
        return
    rows = src_s[0].shape[0]
    for r in range(d):
        for cb, chunk in enumerate(src_s):
            col = (r * len(src_s) + cb) * LANES
            out_ref[:, col:col + LANES] = chunk[pl.ds(r, rows // d, stride=d), :].astype(dtype)


def _from_residue(in_ref, dst_s, d):
    if d == 1:
        return in_ref[...].astype(F32)
    rows = dst_s[0].shape[0]
    for r in range(d):
        for cb, chunk in enumerate(dst_s):
            col = (r * len(dst_s) + cb) * LANES
            chunk[pl.ds(r, rows // d, stride=d), :] = in_ref[:, col:col + LANES].astype(F32)
    return _get_tokens(dst_s)


def _mix_in_fwd(x, g_pre, w_in, cargo=()):
    S, D = x.shape
    TM = 512
    nc = len(cargo)
    kinds = [kind for kind, _ in cargo]
    n_chunks = ATTN_WIDTH // LANES

    def body(x_ref, g_ref, w_ref, *refs):
        cargo_in, refs = refs[:nc], refs[nc:]
        qkv_refs, p_ref, h_ref = refs[:9], refs[9], refs[10]
        t_s = refs[11 + nc:11 + nc + n_chunks]
        cargo_refs = (kinds, cargo_in, refs[11:11 + nc], refs[11 + nc + n_chunks:])
        if nc:
            _cargo_start(*cargo_refs, pl.program_id(0) == 0)
        n, _ = _rms_stats(x_ref[...])
        hb = (n * g_ref[...]).astype(BF16)
        h_ref[...] = hb
        for a in range(3):
            res = _dot(hb, w_ref[a])
            if a == 0:
                res = res * ATTN_SCALE
            _put_tokens(t_s, res)
            for i, d in enumerate(DILATIONS):
                _to_residue(res, t_s, qkv_refs[3 * i + a], d, BF16)
        p_ref[...] = _dot(hb, w_ref[3])
        if nc:
            _cargo_finish(*cargo_refs, pl.program_id(0) == S // TM - 1)

    row = lambda w: pl.BlockSpec((TM, w), lambda i: (i, 0))
    arrays, cargo_specs, shapes, aliases, sems = _cargo_call(cargo, 3, 11)
    out = pl.pallas_call(
        body, name="mix_in_fwd", grid=(S // TM,),
        in_specs=[row(D), _const_spec((1, D)), _const_spec(w_in.shape)] + cargo_specs,
        out_specs=[_residue_spec(TM, d) for d in DILATIONS for _ in range(3)] + [row(POOL_WIDTH), row(D)] + cargo_specs,
        out_shape=[_residue_shape(S, d, BF16) for d in DILATIONS for _ in range(3)]
        + [jax.ShapeDtypeStruct((S, POOL_WIDTH), F32), jax.ShapeDtypeStruct((S, D), BF16)] + shapes,
        input_output_aliases=aliases,
        scratch_shapes=_token_scratch(TM) + sems,
        compiler_params=_params(("arbitrary",), VMEM_LIMIT),
    )(x, g_pre, w_in, *arrays)
    return [out[0:3], out[3:6], out[6:9]], out[9], out[10], out[11:]


def _band_mask(n):
    qi = lax.broadcasted_iota(jnp.int32, (ATTN_BLOCK, 2 * ATTN_BLOCK), 0)
    ki = lax.broadcasted_iota(jnp.int32, (ATTN_BLOCK, 2 * ATTN_BLOCK), 1)
    dist = qi + ATTN_BLOCK - ki
    return (dist >= 0) & (dist <= ATTN_BLOCK) & ((ki >= ATTN_BLOCK) | (n > 0))


def _first_head_lanes():
    return lax.broadcasted_iota(jnp.int32, (1, LANES), 1) < HEAD_DIM


def _stack_heads(pair, first):
    zero = jnp.zeros_like(pair)
    return jnp.concatenate([jnp.where(first, pair, zero), jnp.where(first, zero, pair)], axis=0)


def _unstack_heads(stacked, first):
    return jnp.where(first, stacked[:ATTN_BLOCK], stacked[ATTN_BLOCK:])


CARGO_COPIES = {"ici": 3, "d2d": 3, "scatter": 3, "swap": 1, "everyone": 7}
CARGO_IN_PLACE = ("ici", "d2d", "everyone")


def _cargo_copies(kinds, ins, outs, send_sems, recv_sems, want_recvs=True):
    x, y, c, chips = _position()
    s = 2 * x + y
    me = 2 * s + c
    sibling = (x, y, 1 - c)
    sends, recvs = [], []

    def add(k, src, dst, landing, to):
        sends.append(_remote(src, dst, send_sems.at[k], recv_sems.at[k], to))
        if want_recvs:
            recvs.append(_remote(landing, landing, send_sems.at[k], recv_sems.at[k], to))

    k0 = 0
    for a, kind in enumerate(kinds):
        if kind == "swap":
            rows = ins[a].shape[1] // 2
            add(k0, ins[a].at[:, pl.ds((1 - c) * rows, rows), :], outs[a], outs[a], sibling)
        elif kind == "everyone":
            for m in range(1, 8):
                peer = (x ^ (m >> 2), y ^ ((m >> 1) & 1), c ^ (m & 1))
                add(k0 + m - 1, outs[a].at[me], outs[a].at[me], outs[a].at[4 * peer[0] + 2 * peer[1] + peer[2]], peer)
        else:
            for j, (px, py) in enumerate(chips):
                sj = 2 * px + py
                if kind == "scatter":
                    add(k0 + j, ins[a].at[sj], outs[a].at[j], outs[a].at[j], (px, py, c))
                    continue
                buf = outs[a]
                rows = buf.shape[1] // 2
                half = lambda shard, h: buf.at[shard, pl.ds(h * rows, rows), :]
                if kind == "ici":
                    add(k0 + j, half(s, c), half(s, c), half(sj, c), (px, py, c))
                else:
                    add(k0 + j, half(sj, c), half(sj, c), half(sj, 1 - c), sibling)
        k0 += CARGO_COPIES[kind]
    return sends, recvs


def _cargo_start(kinds, ins, outs, sems, first_step):
    @pl.when(first_step)
    def _():
        for cp in _cargo_copies(kinds, ins, outs, *sems, want_recvs=False)[0]:
            cp.start()


def _cargo_finish(kinds, ins, outs, sems, last_step):
    @pl.when(last_step)
    def _():
        sends, recvs = _cargo_copies(kinds, ins, outs, *sems)
        for cp in sends:
            cp.wait_send()
        for cp in recvs:
            cp.wait_recv()


def _cargo_call(cargo, n_in, n_out):
    arrays = [a for _, a in cargo]
    shapes = []
    for kind, a in cargo:
        shape = {"scatter": (3,) + a.shape[1:], "swap": (a.shape[0], a.shape[1] // 2, a.shape[2])}.get(kind, a.shape)
        shapes.append(jax.ShapeDtypeStruct(shape, a.dtype))
    aliases = {n_in + i: n_out + i for i, (kind, _) in enumerate(cargo) if kind in CARGO_IN_PLACE}
    n_sems = sum(CARGO_COPIES[kind] for kind, _ in cargo)
    sems = [pltpu.SemaphoreType.DMA((n_sems,))] * 2 if cargo else []
    return arrays, [ANY] * len(cargo), shapes, aliases, sems


def _attn_fwd(q, k, v, d, cargo=()):
    L = q.shape[0]
    group = min(d, RESIDUES_PER_STEP)
    width = group * ATTN_WIDTH
    qb = RESIDUES_PER_STEP // group
    steps = L // (qb * ATTN_BLOCK)
    nc = len(cargo)
    kinds = [kind for kind, _ in cargo]

    def body(*refs):
        q_ref, kp_ref, kc_ref, vp_ref, vc_ref = refs[:5]
        o_ref, lse_ref = refs[5 + nc:7 + nc]
        cargo_refs = (kinds, refs[5:5 + nc], refs[7 + nc:7 + 2 * nc], refs[7 + 2 * nc:])
        r, n = pl.program_id(0), pl.program_id(1)
        if nc:
            _cargo_start(*cargo_refs, (r == 0) & (n == 0))
        first = _first_head_lanes()
        for sub in range(qb):
            rows = slice(sub * ATTN_BLOCK, (sub + 1) * ATTN_BLOCK)
            valid = _band_mask(n if sub == 0 else 1)
            valid2 = jnp.concatenate([valid, valid], axis=0)
            for hp in range(width // LANES):
                sl = slice(hp * LANES, (hp + 1) * LANES)
                if sub == 0:
                    kk = jnp.concatenate([kp_ref[:, sl], kc_ref[rows, sl]], axis=0)
                    vv = jnp.concatenate([vp_ref[:, sl], vc_ref[rows, sl]], axis=0)
                else:
                    keys = slice((sub - 1) * ATTN_BLOCK, (sub + 1) * ATTN_BLOCK)
                    kk, vv = kc_ref[keys, sl], vc_ref[keys, sl]
                s = jnp.where(valid2, _dot_nt(_stack_heads(q_ref[rows, sl], first), kk), NEG_INF)
                m = jnp.max(s, axis=-1, keepdims=True)
                p = jnp.exp(s - m)
                den = jnp.sum(p, axis=-1, keepdims=True)
                o_ref[rows, sl] = _unstack_heads(_dot(p.astype(BF16), vv) / den, first).astype(BF16)
                lse_ref[rows, sl] = _unstack_heads(m + jnp.log(den), first)
        if nc:
            _cargo_finish(*cargo_refs, (r == d // group - 1) & (n == steps - 1))

    cur = pl.BlockSpec((qb * ATTN_BLOCK, width), lambda r, n: (n, r))
    prev = pl.BlockSpec((ATTN_BLOCK, width), lambda r, n: (jnp.maximum(n * qb - 1, 0), r))
    arrays, specs, shapes, aliases, sems = _cargo_call(cargo, 5, 2)
    out = pl.pallas_call(
        body, name=f"attn_fwd_d{d}", grid=(d // group, steps),
        in_specs=[cur, prev, cur, prev, cur] + specs,
        out_specs=[cur, cur] + specs,
        out_shape=[jax.ShapeDtypeStruct((L, d * ATTN_WIDTH), BF16), jax.ShapeDtypeStruct((L, d * ATTN_WIDTH), F32)] + shapes,
        input_output_aliases=aliases, scratch_shapes=sems,
        compiler_params=_params(("arbitrary", "arbitrary")),
    )(q, k, k, v, v, *arrays)
    return out[0], out[1], out[2:]


def _attn_mix(outs, lses):
    S = outs[0].shape[0]
    TM = 512
    n = len(DILATIONS)

    def body(*refs):
        o_refs, l_refs, attn_ref, lse_refs = refs[:n], refs[n:2 * n], refs[2 * n], refs[2 * n + 1:3 * n + 1]
        t_s, c_s = refs[3 * n + 1:-1], refs[-1:]
        os = [_from_residue(o_refs[i], t_s, d) for i, d in enumerate(DILATIONS)]
        ls = [_from_residue(l_refs[i], t_s, d) for i, d in enumerate(DILATIONS)]
        m = jnp.maximum(jnp.maximum(ls[0], ls[1]), ls[2])
        es = [jnp.exp(l - m) for l in ls]
        den = es[0] + es[1] + es[2]
        attn_ref[...] = (es[0] * os[0] + es[1] * os[1] + es[2] * os[2]) / den
        pick = _head_stat_matrix(pick_first_lane=True)
        lse = sum(_dot(p, pick) for p in _bf16_pieces(m + jnp.log(den), 3))
        _put_tokens(c_s, lse)
        for i, d in enumerate(DILATIONS):
            _to_residue(lse, c_s, lse_refs[i], d, F32)

    specs = [_residue_spec(TM, d) for d in DILATIONS]
    out = pl.pallas_call(
        body, name="attn_mix", grid=(S // TM,),
        in_specs=specs * 2, out_specs=[specs[0]] + [_residue_spec(TM, d, STAT_WIDTH) for d in DILATIONS],
        out_shape=[jax.ShapeDtypeStruct((S, ATTN_WIDTH), F32)]
        + [_residue_shape(S, d, F32, STAT_WIDTH) for d in DILATIONS],
        scratch_shapes=_token_scratch(TM) + _token_scratch(TM, STAT_WIDTH),
        compiler_params=_params(("parallel",)),
    )(*outs, *lses)
    return out[0], out[1:]


def _pool_counts(first_row, rows, w):
    t = first_row + lax.broadcasted_iota(jnp.int32, (rows, 1), 0)
    return jnp.minimum(t + 1, w).astype(F32)


def _trailing_sums(xe, w):
    s, k = xe, 1
    while k < w:
        s = s + pltpu.roll(s, k, 0)
        k *= 2
    return s


def _leading_sums(xe, w):
    rows = xe.shape[0]
    s, k = xe, 1
    while k < w:
        s = s + pltpu.roll(s, rows - k, 0)
        k *= 2
    return s


def _pooled_groups(halo, cur, first_row):
    TM = cur.shape[0]
    xe = jnp.concatenate([halo, cur], axis=0)
    out = []
    for g, w in enumerate(POOL_WINDOWS):
        a = xe[:, g * POOL_GROUP:(g + 1) * POOL_GROUP]
        s = _trailing_sums(a, w)[POOL_HALO:]
        out.append(s / _pool_counts(first_row, TM, w) - a[POOL_HALO:])
    return out


def _pool_fwd(pool_in, pool_w, pool_scale):
    S = pool_in.shape[0]
    TM = 512
    HB = TM // POOL_HALO

    def body(cur_ref, halo_ref, w_ref, sc_ref, y_ref):
        i = pl.program_id(0)
        halo = jnp.where(i > 0, halo_ref[...], 0.0)
        pooled = _pooled_groups(halo, cur_ref[...], i * TM)
        for g in range(len(POOL_WINDOWS)):
            sl = slice(g * POOL_GROUP, (g + 1) * POOL_GROUP)
            y = _dot(pooled[g].astype(BF16), w_ref[g].astype(BF16)) * sc_ref[:, sl]
            y_ref[:, sl] = y.astype(BF16)

    return pl.pallas_call(
        body, name="pool_fwd", grid=(S // TM,),
        in_specs=[pl.BlockSpec((TM, POOL_WIDTH), lambda i: (i, 0)),
                  pl.BlockSpec((POOL_HALO, POOL_WIDTH), lambda i: (jnp.maximum(i * HB - 1, 0), 0)),
                  _const_spec(pool_w.shape), _const_spec((1, POOL_WIDTH))],
        out_specs=pl.BlockSpec((TM, POOL_WIDTH), lambda i: (i, 0)),
        out_shape=jax.ShapeDtypeStruct((S, POOL_WIDTH), BF16),
        compiler_params=_params(("parallel",)),
    )(pool_in, pool_in, pool_w, pool_scale)


def _mix_out_fwd(attn, pool, w_out, x, g_post, g_ffn_pre):
    S, D = x.shape
    TM = 512

    def body(a_ref, p_ref, w_ref, x_ref, gp_ref, gf_ref, mixed_ref, x1_ref, h2_ref, cat_ref):
        ab = a_ref[...].astype(BF16)
        cat_ref[:, :ATTN_WIDTH] = ab
        cat_ref[:, ATTN_WIDTH:] = p_ref[...]
        mixed = _dot(ab, w_ref[:ATTN_WIDTH, :]) + _dot(p_ref[...], w_ref[ATTN_WIDTH:, :])
        mixed_ref[...] = mixed
        n, _ = _rms_stats(mixed)
        x1 = x_ref[...] + n * gp_ref[...]
        x1_ref[...] = x1
        n2, _ = _rms_stats(x1)
        h2_ref[...] = (n2 * gf_ref[...]).astype(BF16)

    row = lambda w: pl.BlockSpec((TM, w), lambda i: (i, 0))
    return pl.pallas_call(
        body, name="mix_out_fwd", grid=(S // TM,),
        in_specs=[row(ATTN_WIDTH), row(POOL_WIDTH), _const_spec(w_out.shape), row(D),
                  _const_spec((1, D)), _const_spec((1, D))],
        out_specs=[row(D), row(D), row(D), row(D)],
        out_shape=[jax.ShapeDtypeStruct((S, D), F32), jax.ShapeDtypeStruct((S, D), F32),
                   jax.ShapeDtypeStruct((S, D), BF16), jax.ShapeDtypeStruct((S, D), BF16)],
        compiler_params=_params(("parallel",), VMEM_LIMIT),
    )(attn, pool, w_out, x, g_post, g_ffn_pre)


def _ffn_fwd(h2, x1, target, w_up, w_down, conv_w, conv_b, g_post):
    S, D = x1.shape
    CW = w_up.shape[2]
    FF = 2 * CW
    TM = 256
    piece = 4 * LANES
    pieces = [(lo, min(lo + piece, CW)) for lo in range(0, CW, piece)]

    def body(h2_ref, x1_ref, t_ref, wu_ref, wd_ref, cw_ref, cb_ref, g_ref,
             yv_ref, dy_ref, df_ref, dc_ref, loss_ref, dg_ref, dcb_ref, dcw_ref,
             ue_s, dgate_s, dval_s):
        i = pl.program_id(0)

        @pl.when(i == 0)
        def _():
            loss_ref[...] = jnp.zeros_like(loss_ref)
            dg_ref[...] = jnp.zeros_like(dg_ref)
            dcb_ref[...] = jnp.zeros_like(dcb_ref)
            dcw_ref[...] = jnp.zeros_like(dcw_ref)
            ue_s[0:CONV_HALO, :] = jnp.zeros((CONV_HALO, 2 * FF), F32)

        @pl.when(i > 0)
        def _():
            ue_s[0:CONV_HALO, :] = ue_s[TM:TM + CONV_HALO, :]

        def shifted(cols, k):
            return pltpu.roll(ue_s[:, cols], k, 0)[CONV_HALO:]

        def conv(cols):
            return (cb_ref[:, cols] + cw_ref[2, :, cols] * ue_s[CONV_HALO:, cols]
                    + cw_ref[1, :, cols] * shifted(cols, 1) + cw_ref[0, :, cols] * shifted(cols, 2))

        hb = h2_ref[...]
        f = jnp.zeros((TM, D), F32)
        for j in range(2):
            jc = slice(j * CW, (j + 1) * CW)
            for half in range(2):
                blk = 2 * half + j
                cols = slice(blk * CW, (blk + 1) * CW)
                ue_s[CONV_HALO:, cols] = _dot(hb, wu_ref[blk])
            for lo, hi in pieces:
                pc = slice(j * CW + lo, j * CW + hi)
                gelu, dgelu = _gelu_tanh(conv(pc))
                val = conv(slice(FF + j * CW + lo, FF + j * CW + hi))
                dgate_s[:, pc] = val * dgelu
                dval_s[:, pc] = gelu
                yv_ref[:, pc] = (gelu * val).astype(BF16)
            f = f + _dot(yv_ref[:, jc], wd_ref[jc, :])

        n, r = _rms_stats(f)
        err = x1_ref[...] + n * g_ref[...] - t_ref[...]
        loss_ref[...] += 0.5 * jnp.sum(jnp.mean(err * err, axis=-1, keepdims=True), axis=0, keepdims=True)
        dy = err / D
        dy_ref[...] = dy
        df, dg = _rms_bwd(dy, n, r, g_ref[...])
        dg_ref[...] += dg
        dfb = df.astype(BF16)
        df_ref[...] = dfb

        for j in range(2):
            jc = slice(j * CW, (j + 1) * CW)
            dyv = _dot_nt(dfb, wd_ref[jc, :])
            for lo, hi in pieces:
                pc = slice(j * CW + lo, j * CW + hi)
                for half, scale_s in ((0, dgate_s), (1, dval_s)):
                    cols = slice(half * FF + j * CW + lo, half * FF + j * CW + hi)
                    dcv = dyv[:, lo:hi] * scale_s[:, pc]
                    dc_ref[:, cols] = dcv.astype(BF16)
                    dcb_ref[:, cols] += jnp.sum(dcv, axis=0, keepdims=True)
                    dcw_ref[2, :, cols] += jnp.sum(dcv * ue_s[CONV_HALO:, cols], axis=0, keepdims=True)
                    dcw_ref[1, :, cols] += jnp.sum(dcv * shifted(cols, 1), axis=0, keepdims=True)
                    dcw_ref[0, :, cols] += jnp.sum(dcv * shifted(cols, 2), axis=0, keepdims=True)

    row = lambda w: pl.BlockSpec((TM, w), lambda i: (i, 0))
    acc = lambda shape: pl.BlockSpec(shape, lambda i: (0,) * len(shape))
    return pl.pallas_call(
        body, name="ffn_fwd", grid=(S // TM,),
        in_specs=[row(D), row(D), row(D), _const_spec(w_up.shape), _const_spec(w_down.shape),
                  _const_spec(conv_w.shape), _const_spec((1, 2 * FF)), _const_spec((1, D))],
        out_specs=[row(FF), row(D), row(D), row(2 * FF),
                   acc((1, 1)), acc((1, D)), acc((1, 2 * FF)), acc((CONV_WIDTH, 1, 2 * FF))],
        out_shape=[jax.ShapeDtypeStruct((S, FF), BF16),
                   jax.ShapeDtypeStruct((S, D), F32), jax.ShapeDtypeStruct((S, D), BF16),
                   jax.ShapeDtypeStruct((S, 2 * FF), BF16),
                   jax.ShapeDtypeStruct((1, 1), F32), jax.ShapeDtypeStruct((1, D), F32),
                   jax.ShapeDtypeStruct((1, 2 * FF), F32), jax.ShapeDtypeStruct((CONV_WIDTH, 1, 2 * FF), F32)],
        scratch_shapes=[pltpu.VMEM((TM + CONV_HALO, 2 * FF), F32), pltpu.VMEM((TM, FF), F32),
                        pltpu.VMEM((TM, FF), F32)],
        compiler_params=_params(("arbitrary",), VMEM_LIMIT),
    )(h2, x1, target, w_up, w_down, conv_w, conv_b, g_post)


def _ffn_bwd(dc, conv_w, w_up, x1, g_ffn_pre, dy):
    S, D = x1.shape
    CW = w_up.shape[2]
    F2 = 4 * CW
    TM = 256
    HB = TM // CONV_HALO
    last = S // CONV_HALO - 1
    n_tiles = S // TM

    def body(dc_ref, halo_ref, cw_ref, wu_ref, x1_ref, g_ref, dy_ref, du_ref, dx1_ref, dg_ref):
        i = pl.program_id(0)

        @pl.when(i == 0)
        def _():
            dg_ref[...] = jnp.zeros_like(dg_ref)

        keep = i < n_tiles - 1
        dh2 = jnp.zeros((TM, D), F32)
        for blk in range(N_SHARD):
            cols = slice(blk * CW, (blk + 1) * CW)
            halo = jnp.where(keep, halo_ref[:, cols].astype(F32), 0.0)
            dce = jnp.concatenate([dc_ref[:, cols].astype(F32), halo], axis=0)
            rows = TM + CONV_HALO
            du = (cw_ref[2, :, cols] * dce[:TM]
                  + cw_ref[1, :, cols] * pltpu.roll(dce, rows - 1, 0)[:TM]
                  + cw_ref[0, :, cols] * pltpu.roll(dce, rows - 2, 0)[:TM])
            dub = du.astype(BF16)
            du_ref[:, cols] = dub
            dh2 = dh2 + _dot_nt(dub, wu_ref[blk])
        n2, r2 = _rms_stats(x1_ref[...])
        dx, dg = _rms_bwd(dh2, n2, r2, g_ref[...])
        dg_ref[...] += dg
        dx1_ref[...] = dy_ref[...] + dx

    row = lambda w: pl.BlockSpec((TM, w), lambda i: (i, 0))
    return pl.pallas_call(
        body, name="ffn_bwd", grid=(S // TM,),
        in_specs=[row(F2), pl.BlockSpec((CONV_HALO, F2), lambda i: (jnp.minimum((i + 1) * HB, last), 0)),
                  _const_spec(conv_w.shape), _const_spec(w_up.shape), row(D), _const_spec((1, D)), row(D)],
        out_specs=[row(F2), row(D), pl.BlockSpec((1, D), lambda i: (0, 0))],
        out_shape=[jax.ShapeDtypeStruct((S, F2), BF16), jax.ShapeDtypeStruct((S, D), F32),
                   jax.ShapeDtypeStruct((1, D), F32)],
        compiler_params=_params(("arbitrary",), VMEM_LIMIT),
    )(dc, dc, conv_w, w_up, x1, g_ffn_pre, dy)


def _matmul_tn(a, b, n_blocks, name):
    S, M = a.shape
    N = b.shape[1]
    tn = N // n_blocks
    tm = M if M <= 1024 else M // 2
    tk = 2048
    nk = S // tk

    def body(a_ref, b_ref, o_ref):
        @pl.when(pl.program_id(2) == 0)
        def _():
            o_ref[...] = jnp.zeros_like(o_ref)
        o_ref[0] += _dot_tn(a_ref[...], b_ref[...])

    return pl.pallas_call(
        body, name=name, grid=(M // tm, n_blocks, nk),
        in_specs=[pl.BlockSpec((tk, tm), lambda i, j, k: (k, i)), pl.BlockSpec((tk, tn), lambda i, j, k: (k, j))],
        out_specs=pl.BlockSpec((1, tm, tn), lambda i, j, k: (j, i, 0)),
        out_shape=jax.ShapeDtypeStruct((n_blocks, M, tn), F32),
        compiler_params=_params(("parallel", "parallel", "arbitrary"), VMEM_LIMIT),
    )(a, b)


def _mix_out_bwd(dx1, mixed, g_post, w_out, attn, cargo=()):
    S, D = dx1.shape
    TM = 512
    nd = len(DILATIONS)
    nc = len(cargo)
    kinds = [kind for kind, _ in cargo]
    n_chunks = ATTN_WIDTH // LANES

    def body(*refs):
        dx_ref, m_ref, g_ref, w_ref, a_ref = refs[:5]
        dm_ref, dp_ref, dg_ref = refs[5 + nc:8 + nc]
        da_refs, dl_refs = refs[8 + nc:8 + nc + nd], refs[8 + nc + nd:8 + nc + 2 * nd]
        n_out = 8 + nc + 2 * nd
        t_s = refs[n_out + nc:n_out + nc + n_chunks]
        c_s = refs[n_out + nc + n_chunks:n_out + nc + n_chunks + 1]
        cargo_refs = (kinds, refs[5:5 + nc], refs[n_out:n_out + nc], refs[n_out + nc + n_chunks + 1:])
        if nc:
            _cargo_start(*cargo_refs, pl.program_id(0) == 0)

        @pl.when(pl.program_id(0) == 0)
        def _():
            dg_ref[...] = jnp.zeros_like(dg_ref)

        n, r = _rms_stats(m_ref[...])
        dm, dg = _rms_bwd(dx_ref[...], n, r, g_ref[...])
        dg_ref[...] += dg
        dmb = dm.astype(BF16)
        dm_ref[...] = dmb
        da = _dot_nt(dmb, w_ref[:ATTN_WIDTH, :])
        _put_tokens(t_s, da)
        for i, d in enumerate(DILATIONS):
            _to_residue(da, t_s, da_refs[i], d, BF16)
        dp_ref[...] = _dot_nt(dmb, w_ref[ATTN_WIDTH:, :])
        gather = _head_stat_matrix(pick_first_lane=False)
        delta = sum(_dot(p, gather) for p in _bf16_pieces(da * a_ref[...], 2))
        _put_tokens(c_s, delta)
        for i, d in enumerate(DILATIONS):
            _to_residue(delta, c_s, dl_refs[i], d, F32)
        if nc:
            _cargo_finish(*cargo_refs, pl.program_id(0) == S // TM - 1)

    row = lambda w: pl.BlockSpec((TM, w), lambda i: (i, 0))
    specs = [_residue_spec(TM, d) for d in DILATIONS]
    arrays, cargo_specs, shapes, aliases, sems = _cargo_call(cargo, 5, 3 + 2 * nd)
    out = pl.pallas_call(
        body, name="mix_out_bwd", grid=(S // TM,),
        in_specs=[row(D), row(D), _const_spec((1, D)), _const_spec(w_out.shape), row(ATTN_WIDTH)] + cargo_specs,
        out_specs=[row(D), row(POOL_WIDTH), pl.BlockSpec((1, D), lambda i: (0, 0))] + specs
        + [_residue_spec(TM, d, STAT_WIDTH) for d in DILATIONS] + cargo_specs,
        out_shape=[jax.ShapeDtypeStruct((S, D), BF16), jax.ShapeDtypeStruct((S, POOL_WIDTH), F32),
                   jax.ShapeDtypeStruct((1, D), F32)]
        + [_residue_shape(S, d, BF16) for d in DILATIONS]
        + [_residue_shape(S, d, F32, STAT_WIDTH) for d in DILATIONS] + shapes,
        input_output_aliases=aliases,
        scratch_shapes=_token_scratch(TM) + _token_scratch(TM, STAT_WIDTH) + sems,
        compiler_params=_params(("arbitrary",), VMEM_LIMIT),
    )(dx1, mixed, g_post, w_out, attn, *arrays)
    return out[0], out[1], out[2], out[3:3 + nd], out[3 + nd:3 + 2 * nd], out[3 + 2 * nd:]


def _pool_bwd(pool_in, d_pool, pool_w, pool_scale):
    S = pool_in.shape[0]
    TM = 512
    HB = TM // POOL_HALO
    last = S // POOL_HALO - 1
    G = len(POOL_WINDOWS)

    def body(cur_ref, halo_ref, dcur_ref, dnext_ref, w_ref, sc_ref, dxin_ref, dw_ref, dsc_ref):
        i = pl.program_id(0)

        @pl.when(i == 0)
        def _():
            dw_ref[...] = jnp.zeros_like(dw_ref)
            dsc_ref[...] = jnp.zeros_like(dsc_ref)

        halo = jnp.where(i > 0, halo_ref[...], 0.0)
        pooled = _pooled_groups(halo, cur_ref[...], i * TM)
        dnext = jnp.where(i < S // TM - 1, dnext_ref[...], 0.0)
        dye = jnp.concatenate([dcur_ref[...], dnext], axis=0)
        for g, w in enumerate(POOL_WINDOWS):
            sl = slice(g * POOL_GROUP, (g + 1) * POOL_GROUP)
            wg = w_ref[g].astype(BF16)
            pb = pooled[g].astype(BF16)
            dsc_ref[:, sl] += jnp.sum(dye[:TM, sl] * _dot(pb, wg), axis=0, keepdims=True)
            dpre = (dye[:, sl] * sc_ref[:, sl]).astype(BF16)
            dw_ref[g] += _dot_tn(pb, dpre[:TM])
            dpooled = _dot_nt(dpre, wg)
            z = dpooled / _pool_counts(i * TM, TM + POOL_HALO, w)
            dxin_ref[:, sl] = (_leading_sums(z, w)[:TM] - dpooled[:TM]).astype(BF16)

    row = pl.BlockSpec((TM, POOL_WIDTH), lambda i: (i, 0))
    return pl.pallas_call(
        body, name="pool_bwd", grid=(S // TM,),
        in_specs=[row, pl.BlockSpec((POOL_HALO, POOL_WIDTH), lambda i: (jnp.maximum(i * HB - 1, 0), 0)),
                  row, pl.BlockSpec((POOL_HALO, POOL_WIDTH), lambda i: (jnp.minimum((i + 1) * HB, last), 0)),
                  _const_spec(pool_w.shape), _const_spec((1, POOL_WIDTH))],
        out_specs=[row, pl.BlockSpec((G, POOL_GROUP, POOL_GROUP), lambda i: (0, 0, 0)),
                   pl.BlockSpec((1, POOL_WIDTH), lambda i: (0, 0))],
        out_shape=[jax.ShapeDtypeStruct((S, POOL_WIDTH), BF16), jax.ShapeDtypeStruct((G, POOL_GROUP, POOL_GROUP), F32),
                   jax.ShapeDtypeStruct((1, POOL_WIDTH), F32)],
        compiler_params=_params(("arbitrary",)),
    )(pool_in, pool_in, d_pool, d_pool, pool_w, pool_scale)


def _attn_bwd(q, k, v, d_attn, lse, delta, d, cargo=()):
    L = q.shape[0]
    nb = L // ATTN_BLOCK
    group = min(d, RESIDUES_PER_STEP)
    width = group * ATTN_WIDTH
    nc = len(cargo)
    kinds = [kind for kind, _ in cargo]

    def body(*refs):
        q_ref, kp_ref, kc_ref, vp_ref, vc_ref, do_ref, lse_ref, dl_ref = refs[:8]
        dq_ref, dk_ref, dv_ref = refs[8 + nc:11 + nc]
        ck_s, cv_s = refs[11 + 2 * nc:13 + 2 * nc]
        cargo_refs = (kinds, refs[8:8 + nc], refs[11 + nc:11 + 2 * nc], refs[13 + 2 * nc:])
        r, n = pl.program_id(0), pl.program_id(1)
        if nc:
            _cargo_start(*cargo_refs, (r == 0) & (n == 0))

        @pl.when(n == 0)
        def _():
            ck_s[...] = jnp.zeros_like(ck_s)
            cv_s[...] = jnp.zeros_like(cv_s)

        @pl.when(n < nb)
        def _():
            valid = _band_mask(n)
            valid2 = jnp.concatenate([valid, valid], axis=0)
            first = _first_head_lanes()

            def stacked_column(ref, hp):
                lane = 2 * hp * STAT_LANES
                return jnp.concatenate([ref[:, lane:lane + 1], ref[:, lane + STAT_LANES:lane + STAT_LANES + 1]], axis=0)

            for hp in range(width // LANES):
                sl = slice(hp * LANES, (hp + 1) * LANES)
                qq = _stack_heads(q_ref[:, sl], first)
                dd = _stack_heads(do_ref[:, sl], first)
                kk = jnp.concatenate([kp_ref[:, sl], kc_ref[:, sl]], axis=0)
                vv = jnp.concatenate([vp_ref[:, sl], vc_ref[:, sl]], axis=0)
                s = _dot_nt(qq, kk)
                p = jnp.where(valid2, jnp.exp(s - stacked_column(lse_ref, hp)), 0.0)
                dp = _dot_nt(dd, vv)
                ds = (p * (dp - stacked_column(dl_ref, hp))).astype(BF16)
                dq_ref[:, sl] = (_unstack_heads(_dot(ds, kk), first) * ATTN_SCALE).astype(BF16)
                dk = _dot_tn(ds, qq)
                dv = _dot_tn(p.astype(BF16), dd)
                dk_ref[:, sl] = (ck_s[:, sl] + dk[:ATTN_BLOCK]).astype(BF16)
                dv_ref[:, sl] = (cv_s[:, sl] + dv[:ATTN_BLOCK]).astype(BF16)
                ck_s[:, sl] = dk[ATTN_BLOCK:]
                cv_s[:, sl] = dv[ATTN_BLOCK:]

        @pl.when(n == nb)
        def _():
            dk_ref[...] = ck_s[...].astype(BF16)
            dv_ref[...] = cv_s[...].astype(BF16)

        if nc:
            _cargo_finish(*cargo_refs, (r == d // group - 1) & (n == nb))

    blk = (ATTN_BLOCK, width)
    cur = pl.BlockSpec(blk, lambda r, n: (jnp.minimum(n, nb - 1), r))
    stat = pl.BlockSpec((ATTN_BLOCK, group * STAT_WIDTH), lambda r, n: (jnp.minimum(n, nb - 1), r))
    prev = pl.BlockSpec(blk, lambda r, n: (jnp.maximum(jnp.minimum(n, nb - 1) - 1, 0), r))
    done = pl.BlockSpec(blk, lambda r, n: (jnp.maximum(n - 1, 0), r))
    arrays, specs, shapes, aliases, sems = _cargo_call(cargo, 8, 3)
    out = pl.pallas_call(
        body, name=f"attn_bwd_d{d}", grid=(d // group, nb + 1),
        in_specs=[cur, prev, cur, prev, cur, cur, stat, stat] + specs, out_specs=[cur, done, done] + specs,
        out_shape=[jax.ShapeDtypeStruct((L, d * ATTN_WIDTH), BF16)] * 3 + shapes,
        input_output_aliases=aliases,
        scratch_shapes=[pltpu.VMEM(blk, F32), pltpu.VMEM(blk, F32)] + sems,
        compiler_params=_params(("arbitrary", "arbitrary")),
    )(q, k, k, v, v, d_attn, lse, delta, *arrays)
    return out[:3], out[3:]


def _attn_bwd_consecutive(q, k, v, d_attn, lse, delta, cargo=()):
    L = q.shape[0]
    qb = CONSECUTIVE_BLOCKS
    steps = L // (qb * ATTN_BLOCK)
    nc = len(cargo)
    kinds = [kind for kind, _ in cargo]

    def body(*refs):
        q_ref, kp_ref, kc_ref, vp_ref, vc_ref, do_ref, lse_ref, dl_ref = refs[:8]
        dq_ref, dk_ref, dv_ref, ek_ref, ev_ref = refs[8 + nc:13 + nc]
        cargo_refs = (kinds, refs[8:8 + nc], refs[13 + nc:13 + 2 * nc], refs[13 + 2 * nc:])
        n = pl.program_id(0)
        if nc:
            _cargo_start(*cargo_refs, n == 0)
        first = _first_head_lanes()
        for hp in range(ATTN_WIDTH // LANES):
            sl = slice(hp * LANES, (hp + 1) * LANES)
            for sub in range(qb):
                rows = slice(sub * ATTN_BLOCK, (sub + 1) * ATTN_BLOCK)
                valid = _band_mask(n if sub == 0 else 1)
                valid2 = jnp.concatenate([valid, valid], axis=0)
                if sub == 0:
                    kk = jnp.concatenate([kp_ref[:, sl], kc_ref[rows, sl]], axis=0)
                    vv = jnp.concatenate([vp_ref[:, sl], vc_ref[rows, sl]], axis=0)
                else:
                    keys = slice((sub - 1) * ATTN_BLOCK, (sub + 1) * ATTN_BLOCK)
                    kk, vv = kc_ref[keys, sl], vc_ref[keys, sl]
                qq = _stack_heads(q_ref[rows, sl], first)
                dd = _stack_heads(do_ref[rows, sl], first)
                lane = 2 * hp * STAT_LANES
                column = lambda ref: jnp.concatenate(
                    [ref[rows, lane:lane + 1], ref[rows, lane + STAT_LANES:lane + STAT_LANES + 1]], axis=0)
                p = jnp.where(valid2, jnp.exp(_dot_nt(qq, kk) - column(lse_ref)), 0.0)
                ds = (p * (_dot_nt(dd, vv) - column(dl_ref))).astype(BF16)
                dq_ref[rows, sl] = (_unstack_heads(_dot(ds, kk), first) * ATTN_SCALE).astype(BF16)
                dk = _dot_tn(ds, qq)
                dv = _dot_tn(p.astype(BF16), dd)
                if sub == 0:
                    ek_ref[:, sl] = dk[:ATTN_BLOCK].astype(BF16)
                    ev_ref[:, sl] = dv[:ATTN_BLOCK].astype(BF16)
                else:
                    before = slice((sub - 1) * ATTN_BLOCK, sub * ATTN_BLOCK)
                    dk_ref[before, sl] = (carry_k + dk[:ATTN_BLOCK]).astype(BF16)
                    dv_ref[before, sl] = (carry_v + dv[:ATTN_BLOCK]).astype(BF16)
                carry_k, carry_v = dk[ATTN_BLOCK:], dv[ATTN_BLOCK:]
            dk_ref[rows, sl] = carry_k.astype(BF16)
            dv_ref[rows, sl] = carry_v.astype(BF16)
        if nc:
            _cargo_finish(*cargo_refs, n == steps - 1)

    cur = pl.BlockSpec((qb * ATTN_BLOCK, ATTN_WIDTH), lambda n: (n, 0))
    prev = pl.BlockSpec((ATTN_BLOCK, ATTN_WIDTH), lambda n: (jnp.maximum(n * qb - 1, 0), 0))
    edge = pl.BlockSpec((ATTN_BLOCK, ATTN_WIDTH), lambda n: (n, 0))
    stat = pl.BlockSpec((qb * ATTN_BLOCK, STAT_WIDTH), lambda n: (n, 0))
    arrays, specs, shapes, aliases, sems = _cargo_call(cargo, 8, 5)
    out = pl.pallas_call(
        body, name="attn_bwd_d1", grid=(steps,),
        in_specs=[cur, prev, cur, prev, cur, cur, stat, stat] + specs, out_specs=[cur, cur, cur, edge, edge] + specs,
        out_shape=[jax.ShapeDtypeStruct((L, ATTN_WIDTH), BF16)] * 3
        + [jax.ShapeDtypeStruct((steps * ATTN_BLOCK, ATTN_WIDTH), BF16)] * 2 + shapes,
        input_output_aliases=aliases, scratch_shapes=sems,
        compiler_params=_params(("arbitrary",)),
    )(q, k, k, v, v, d_attn, lse, delta, *arrays)
    return out[:3], out[3:5], out[5:]


def _mix_in_bwd(dqkv, edges, d_pool_in, w_in, x, g_pre, dx1):
    S, D = x.shape
    TM = CONSECUTIVE_BLOCKS * ATTN_BLOCK
    nd = len(DILATIONS)
    n_tiles = S // TM

    def body(*refs):
        g_refs = refs[:3 * nd]
        e_refs = (None,) + refs[3 * nd:3 * nd + 2]
        dpi_ref, w_ref, x_ref, g_ref, dx1_ref, dproj_ref, gx_ref, dg_ref = refs[3 * nd + 2:3 * nd + 10]
        t_s = refs[3 * nd + 10:]

        @pl.when(pl.program_id(0) == 0)
        def _():
            dg_ref[...] = jnp.zeros_like(dg_ref)

        dh = jnp.zeros((TM, D), F32)
        for a in range(4):
            if a < 3:
                tot = g_refs[a][...].astype(F32)
                if a > 0:
                    late = jnp.where(pl.program_id(0) < n_tiles - 1, e_refs[a][...].astype(F32), 0.0)
                    tot = jnp.concatenate([tot[:TM - ATTN_BLOCK], tot[TM - ATTN_BLOCK:] + late], axis=0)
                for i, d in enumerate(DILATIONS[1:]):
                    tot = tot + _from_residue(g_refs[3 * (i + 1) + a], t_s, d)
                db = tot.astype(BF16)
            else:
                db = dpi_ref[...]
            dproj_ref[:, a * ATTN_WIDTH:(a + 1) * ATTN_WIDTH] = db
            dh = dh + _dot_nt(db, w_ref[a])
        n, r = _rms_stats(x_ref[...])
        dx, dg = _rms_bwd(dh, n, r, g_ref[...])
        dg_ref[...] += dg
        gx_ref[...] = dx1_ref[...] + dx

    row = lambda w: pl.BlockSpec((TM, w), lambda i: (i, 0))
    edge = pl.BlockSpec((ATTN_BLOCK, ATTN_WIDTH), lambda i: (jnp.minimum(i + 1, n_tiles - 1), 0))
    return pl.pallas_call(
        body, name="mix_in_bwd", grid=(S // TM,),
        in_specs=[_residue_spec(TM, d) for d in DILATIONS for _ in range(3)] + [edge, edge]
        + [row(POOL_WIDTH), _const_spec(w_in.shape), row(D), _const_spec((1, D)), row(D)],
        out_specs=[row(4 * ATTN_WIDTH), row(D), pl.BlockSpec((1, D), lambda i: (0, 0))],
        out_shape=[jax.ShapeDtypeStruct((S, 4 * ATTN_WIDTH), BF16), jax.ShapeDtypeStruct((S, D), F32),
                   jax.ShapeDtypeStruct((1, D), F32)],
        scratch_shapes=_token_scratch(TM),
        compiler_params=_params(("arbitrary",), VMEM_LIMIT),
    )(*[g for gs in dqkv for g in gs], *edges, d_pool_in, w_in, x, g_pre, dx1)


SMALL_EARLY = ("pool_w", "pool_scale", "g_mix_post", "g_ffn_pre", "conv_b", "g_ffn_post", "conv_w")
SMALL_LATE = ("g_mix_pre",)


def _pack_small(grads, names):
    parts = []
    for n in names:
        g = grads[n]
        if n == "conv_w":
            g = g.reshape(CONV_WIDTH, N_SHARD, -1).transpose(1, 0, 2)
        parts.append(g.reshape(-1, LANES))
    return jnp.concatenate(parts, axis=0) if len(parts) > 1 else parts[0]


def _unpack_small(packed, names, like, shard):
    out, row = {}, 0
    for n in names:
        size = like[n].size * (N_SHARD if n == "conv_w" else 1)
        g = packed[row:row + size // LANES]
        row += size // LANES
        if n == "conv_w":
            g = lax.dynamic_slice_in_dim(g.reshape((N_SHARD,) + like[n].shape), shard, 1, axis=0)[0]
        out[n] = g.reshape(like[n].shape)
    return out


def _local_step(x, target, g_mix_pre, w_in, pool_w, pool_scale, w_out, g_mix_post, g_ffn_pre,
                w_up, conv_w, conv_b, w_down, g_ffn_post, mesh_pos=None):
    on_mesh = mesh_pos is not None
    D = x.shape[1]
    CW = w_up.shape[2]
    qkv, pool_in, h1, got = _mix_in_fwd(x, g_mix_pre, w_in, [("ici", w_up)] if on_mesh else ())
    w_up = got[0] if on_mesh else w_up
    o1, l1, got = _attn_fwd(*qkv[0], 1, [("d2d", w_up), ("ici", w_out), ("ici", w_down)] if on_mesh else ())
    w_up, w_out, w_down = got if on_mesh else (w_up, w_out, w_down)
    o4, l4, got = _attn_fwd(*qkv[1], 4, [("d2d", w_out), ("d2d", w_down)] if on_mesh else ())
    w_out, w_down = got if on_mesh else (w_out, w_down)
    o16, l16, _ = _attn_fwd(*qkv[2], 16)
    w_out = w_out.reshape(D, D)
    w_down = w_down.reshape(2 * CW, D)
    attn, lse = _attn_mix((o1, o4, o16), (l1, l4, l16))
    pool = _pool_fwd(pool_in, pool_w, pool_scale)
    mixed, x1, h2, cat = _mix_out_fwd(attn, pool, w_out, x, g_mix_post, g_ffn_pre)

    yv, dy, df, dc, loss, d_g_ffn_post, d_conv_b, d_conv_w = _ffn_fwd(
        h2, x1, target, w_up, w_down, conv_w, conv_b, g_ffn_post)
    du, dx1, d_g_ffn_pre = _ffn_bwd(dc, conv_w, w_up, x1, g_ffn_pre, dy)
    d_w_up = _matmul_tn(h2, du, N_SHARD, "grad_w_up")
    d_w_down = _matmul_tn(yv, df, 1, "grad_w_down")[0].reshape(N_SHARD, CW // 2, D)
    swap = [("swap", d_w_up), ("swap", d_w_down)] if on_mesh else ()
    d_mixed, d_pool, d_g_mix_post, d_attn, delta, from_sibling = _mix_out_bwd(dx1, mixed, g_mix_post, w_out, attn, swap)
    d_w_out = _matmul_tn(cat, d_mixed, 1, "grad_w_out")[0].reshape(N_SHARD, D // N_SHARD, D)
    d_pool_in, d_pool_w, d_pool_scale = _pool_bwd(pool_in, d_pool, pool_w, pool_scale)
    grads = dict(pool_w=d_pool_w, pool_scale=d_pool_scale, w_out=d_w_out, g_mix_post=d_g_mix_post,
                 g_ffn_pre=d_g_ffn_pre, w_up=d_w_up, conv_w=d_conv_w, conv_b=d_conv_b, w_down=d_w_down,
                 g_ffn_post=d_g_ffn_post)
    cargo = [(), (), ()]
    if on_mesh:
        c_arr, device = mesh_pos
        up_f32, up_bf16 = _pair_sum(d_w_up, from_sibling[0], c_arr, "pair_sum_w_up")
        down_f32, down_bf16 = _pair_sum(d_w_down, from_sibling[1], c_arr, "pair_sum_w_down")
        early = _pack_small(grads, SMALL_EARLY)
        early_slots = lax.dynamic_update_index_in_dim(jnp.zeros((8,) + early.shape, F32), early, device, 0)
        cargo = [[("scatter", down_bf16)], [("scatter", up_bf16)], [("everyone", early_slots)]]

    dqkv1, edges, landed1 = _attn_bwd_consecutive(*qkv[0], d_attn[0], lse[0], delta[0], cargo[0])
    dqkv, landed = zip(*[_attn_bwd(*qkv[i], d_attn[i], lse[i], delta[i], DILATIONS[i], cargo[i]) for i in (1, 2)])
    if on_mesh:
        grads.update(w_down=(down_f32, landed1[0]), w_up=(up_f32, landed[0][0]), small_early=landed[1][0])
    d_proj, grad_x, grads["g_mix_pre"] = _mix_in_bwd((dqkv1,) + dqkv, edges, d_pool_in, w_in, x, g_mix_pre, dx1)
    grads["w_in"] = _matmul_tn(h1, d_proj, N_SHARD, "grad_w_in")
    return loss, grad_x, grads


ANY = pl.BlockSpec(memory_space=pl.ANY)


def _position():
    x, y, c = lax.axis_index("x"), lax.axis_index("y"), lax.axis_index("c")
    chips = [(1 - x, y), (x, 1 - y), (1 - x, 1 - y)]
    return x, y, c, chips


def _remote(src, dst, send_sem, recv_sem, to):
    return pltpu.make_async_remote_copy(src_ref=src, dst_ref=dst, send_sem=send_sem, recv_sem=recv_sem,
                                        device_id=to, device_id_type=MESH)


def _cast_bf16(w, shard_arr, name):
    R, C = w.shape
    tr = R // 2

    def body(s_ref, w_ref, o_ref):
        o_ref[0] = w_ref[...].astype(BF16)

    return pl.pallas_call(
        body, name=name,
        grid_spec=pltpu.PrefetchScalarGridSpec(
            num_scalar_prefetch=1, grid=(2,),
            in_specs=[pl.BlockSpec((tr, C), lambda i, s_ref: (i, 0))],
            out_specs=pl.BlockSpec((1, tr, C), lambda i, s_ref: (s_ref[0], i, 0))),
        out_shape=jax.ShapeDtypeStruct((N_SHARD, R, C), BF16),
        compiler_params=_params(("parallel",)))(shard_arr, w)


def _gather_weights(bufs):
    n = len(bufs) - 1

    def body(*refs):
        outs, cw_out = refs[n + 1:2 * n + 1], refs[2 * n + 1]
        ici_send, ici_recv, d2d_send, d2d_recv = refs[2 * n + 2:]
        x, y, c, chips = _position()
        s = 2 * x + y
        sibling = (x, y, 1 - c)

        def half(a, shard, h):
            rows = outs[a].shape[1] // 2
            return outs[a].at[shard, pl.ds(h * rows, rows), :]

        sends = []
        for a in range(n):
            for j, (px, py) in enumerate(chips):
                sends.append(_remote(half(a, s, c), half(a, s, c),
                                     ici_send.at[3 * a + j], ici_recv.at[3 * a + j], (px, py, c)))
        for j, (px, py) in enumerate(chips):
            sends.append(_remote(cw_out.at[s], cw_out.at[s], ici_send.at[3 * n + j], ici_recv.at[3 * n + j], (px, py, c)))
        for cp in sends:
            cp.start()
        passed = []
        for a in range(n):
            for j, (px, py) in enumerate(chips):
                sj = 2 * px + py
                got = half(a, sj, c)
                _remote(got, got, ici_send.at[3 * a + j], ici_recv.at[3 * a + j], (px, py, c)).wait_recv()
                fwd = _remote(got, got, d2d_send.at[3 * a + j], d2d_recv.at[3 * a + j], sibling)
                fwd.start()
                passed.append(fwd)
        for j, (px, py) in enumerate(chips):
            got = cw_out.at[2 * px + py]
            _remote(got, got, ici_send.at[3 * n + j], ici_recv.at[3 * n + j], (px, py, c)).wait_recv()
        for a in range(n):
            for j, (px, py) in enumerate(chips):
                got = half(a, 2 * px + py, 1 - c)
                _remote(got, got, d2d_send.at[3 * a + j], d2d_recv.at[3 * a + j], sibling).wait_recv()
        for cp in sends + passed:
            cp.wait_send()

    return pl.pallas_call(
        body, name="gather_weights",
        in_specs=[ANY] * (n + 1), out_specs=[ANY] * (n + 1),
        out_shape=[jax.ShapeDtypeStruct(b.shape, b.dtype) for b in bufs],
        input_output_aliases={i: i for i in range(n + 1)},
        scratch_shapes=[pltpu.SemaphoreType.DMA((3 * n + 3,)), pltpu.SemaphoreType.DMA((3 * n + 3,)),
                        pltpu.SemaphoreType.DMA((3 * n,)), pltpu.SemaphoreType.DMA((3 * n,))],
        compiler_params=pltpu.CompilerParams(has_side_effects=True),
    )(*bufs)


def _swap_halves(grads, tag):
    n = len(grads)

    def body(*refs):
        ins, outs, send_sem, recv_sem = refs[:n], refs[n:2 * n], refs[2 * n], refs[2 * n + 1]
        x, y, c, _ = _position()
        copies = []
        for a in range(n):
            rows = ins[a].shape[1] // 2
            copies.append(_remote(ins[a].at[:, pl.ds((1 - c) * rows, rows), :], outs[a],
                                  send_sem.at[a], recv_sem.at[a], (x, y, 1 - c)))
        for cp in copies:
            cp.start()
        for cp in copies:
            cp.wait()

    return pl.pallas_call(
        body, name="swap_grad_halves_" + tag,
        in_specs=[ANY] * n, out_specs=[ANY] * n,
        out_shape=[jax.ShapeDtypeStruct((g.shape[0], g.shape[1] // 2, g.shape[2]), F32) for g in grads],
        scratch_shapes=[pltpu.SemaphoreType.DMA((n,)), pltpu.SemaphoreType.DMA((n,))],
        compiler_params=pltpu.CompilerParams(has_side_effects=True),
    )(*grads)


def _pair_sum(g, got, c_arr, name):
    n_sh, R, C = g.shape
    rows = R // 2

    def body(c_ref, g_ref, r_ref, f_ref, b_ref):
        t = g_ref[...] + r_ref[...]
        f_ref[...] = t
        b_ref[...] = t.astype(BF16)

    blk = pl.BlockSpec((1, rows, C), lambda i, c_ref: (i, 0, 0))
    return pl.pallas_call(
        body, name=name,
        grid_spec=pltpu.PrefetchScalarGridSpec(
            num_scalar_prefetch=1, grid=(n_sh,),
            in_specs=[pl.BlockSpec((1, rows, C), lambda i, c_ref: (i, c_ref[0], 0)), blk],
            out_specs=[blk, blk]),
        out_shape=[jax.ShapeDtypeStruct((n_sh, rows, C), F32), jax.ShapeDtypeStruct((n_sh, rows, C), BF16)],
        compiler_params=_params(("parallel",)),
    )(c_arr, g, got)


def _scatter_grads(sums_bf16, small_all):
    n = len(sums_bf16)

    def body(*refs):
        b_ins = refs[:n]
        recvs, sm = refs[n + 1:2 * n + 1], refs[2 * n + 1]
        ici_send, ici_recv, sm_send, sm_recv = refs[2 * n + 2:]
        x, y, c, chips = _position()
        me = 4 * x + 2 * y + c
        copies = []
        for a in range(n):
            for j, (px, py) in enumerate(chips):
                copies.append(_remote(b_ins[a].at[2 * px + py], recvs[a].at[j],
                                      ici_send.at[3 * a + j], ici_recv.at[3 * a + j], (px, py, c)))
        for k in range(1, 8):
            peer = (x ^ (k >> 2), y ^ ((k >> 1) & 1), c ^ (k & 1))
            copies.append(_remote(sm.at[me], sm.at[me], sm_send.at[k - 1], sm_recv.at[k - 1], peer))
        for cp in copies:
            cp.start()
        for cp in copies:
            cp.wait_send()
        for a in range(n):
            for j, (px, py) in enumerate(chips):
                _remote(recvs[a].at[j], recvs[a].at[j], ici_send.at[3 * a + j], ici_recv.at[3 * a + j],
                        (px, py, c)).wait_recv()
        for k in range(1, 8):
            peer = (x ^ (k >> 2), y ^ ((k >> 1) & 1), c ^ (k & 1))
            theirs = sm.at[4 * peer[0] + 2 * peer[1] + peer[2]]
            _remote(theirs, theirs, sm_send.at[k - 1], sm_recv.at[k - 1], peer).wait_recv()

    out = pl.pallas_call(
        body, name="scatter_grads",
        in_specs=[ANY] * (n + 1), out_specs=[ANY] * (n + 1),
        out_shape=[jax.ShapeDtypeStruct((3,) + b.shape[1:], BF16) for b in sums_bf16]
        + [jax.ShapeDtypeStruct(small_all.shape, F32)],
        input_output_aliases={n: n},
        scratch_shapes=[pltpu.SemaphoreType.DMA((3 * n,)), pltpu.SemaphoreType.DMA((3 * n,)),
                        pltpu.SemaphoreType.DMA((7,)), pltpu.SemaphoreType.DMA((7,))],
        compiler_params=pltpu.CompilerParams(has_side_effects=True),
    )(*sums_bf16, small_all)
    return out[:n], out[n]


def _shard_sum(sums_f32, recv, shard_arr, c_arr, name):
    _, rows, C = sums_f32.shape

    def body(s_ref, c_ref, o_ref, r_ref, t_ref):
        t_ref[...] = ((o_ref[0] + r_ref[0].astype(F32)) + r_ref[1].astype(F32)) + r_ref[2].astype(F32)

    return pl.pallas_call(
        body, name=name,
        grid_spec=pltpu.PrefetchScalarGridSpec(
            num_scalar_prefetch=2, grid=(1,),
            in_specs=[pl.BlockSpec((1, rows, C), lambda i, s_ref, c_ref: (s_ref[0], 0, 0)),
                      pl.BlockSpec((3, rows, C), lambda i, s_ref, c_ref: (0, 0, 0))],
            out_specs=pl.BlockSpec((rows, C), lambda i, s_ref, c_ref: (c_ref[0], 0))),
        out_shape=jax.ShapeDtypeStruct((2 * rows, C), F32),
        compiler_params=_params(("arbitrary",)),
    )(shard_arr, c_arr, sums_f32, recv)


def _join_halves(bufs):
    n = len(bufs)

    def body(*refs):
        outs, send_sem, recv_sem = refs[n:2 * n], refs[2 * n], refs[2 * n + 1]
        x, y, c, _ = _position()
        copies = []
        for a in range(n):
            rows = outs[a].shape[0] // 2
            mine = outs[a].at[pl.ds(c * rows, rows), :]
            copies.append(_remote(mine, mine, send_sem.at[a], recv_sem.at[a], (x, y, 1 - c)))
        for cp in copies:
            cp.start()
        for a, cp in enumerate(copies):
            cp.wait_send()
            rows = outs[a].shape[0] // 2
            theirs = outs[a].at[pl.ds((1 - c) * rows, rows), :]
            _remote(theirs, theirs, send_sem.at[a], recv_sem.at[a], (x, y, 1 - c)).wait_recv()

    return pl.pallas_call(
        body, name="join_grad_halves",
        in_specs=[ANY] * n, out_specs=[ANY] * n,
        out_shape=[jax.ShapeDtypeStruct(b.shape, F32) for b in bufs],
        input_output_aliases={i: i for i in range(n)},
        scratch_shapes=[pltpu.SemaphoreType.DMA((n,)), pltpu.SemaphoreType.DMA((n,))],
        compiler_params=pltpu.CompilerParams(has_side_effects=True),
    )(*bufs)


def _small_sum(parts, tag):
    _, R, C = parts.shape

    def body(p_ref, o_ref):
        t = p_ref[0]
        for k in range(1, 8):
            t = t + p_ref[k]
        o_ref[...] = t

    return pl.pallas_call(
        body, name="small_grad_sum_" + tag, grid=(1,),
        in_specs=[pl.BlockSpec((8, R, C), lambda i: (0, 0, 0))], out_specs=pl.BlockSpec((R, C), lambda i: (0, 0)),
        out_shape=jax.ShapeDtypeStruct((R, C), F32), compiler_params=_params(("arbitrary",)),
    )(parts)


def _adamw_math(w, g, m, v):
    m = ADAM_B1 * m + (1.0 - ADAM_B1) * g
    v = ADAM_B2 * v + (1.0 - ADAM_B2) * (g * g)
    m_hat = m / (1.0 - ADAM_B1 ** ADAM_STEP)
    v_hat = v / (1.0 - ADAM_B2 ** ADAM_STEP)
    delta = -ADAM_LR * (m_hat / (jnp.sqrt(v_hat) + ADAM_EPS) + ADAM_WD * w)
    return delta, m, v


def _adamw_big(w, g, m, v, name):
    R, C = w.shape
    tr = R // 4

    def body(w_ref, g_ref, m_ref, v_ref, go_ref, d_ref, nm_ref, nv_ref):
        g = g_ref[...]
        go_ref[...] = g
        d_ref[...], nm_ref[...], nv_ref[...] = _adamw_math(w_ref[...], g, m_ref[...], v_ref[...])

    blk = pl.BlockSpec((tr, C), lambda i: (i, 0))
    return pl.pallas_call(
        body, name=name, grid=(4,), in_specs=[blk] * 4, out_specs=[blk] * 4,
        out_shape=[jax.ShapeDtypeStruct((R, C), F32)] * 4, compiler_params=_params(("parallel",)),
    )(w, g, m, v)


def _adamw_small(ws, gs, ms, vs):
    n = len(ws)

    def body(*refs):
        for a in range(n):
            w, g, m, v = (refs[k * n + a][...] for k in range(4))
            d, nm, nv = _adamw_math(w, g, m, v)
            refs[4 * n + a][...] = d
            refs[5 * n + a][...] = nm
            refs[6 * n + a][...] = nv

    shapes = [jax.ShapeDtypeStruct(w.shape, F32) for w in ws]
    out = pl.pallas_call(body, name="adamw_small", out_shape=shapes * 3)(*ws, *gs, *ms, *vs)
    return out[:n], out[n:2 * n], out[2 * n:]


BIG = ("w_in", "w_out", "w_up", "w_down")
SMALL = ("g_mix_pre", "pool_w", "pool_scale", "g_mix_post", "g_ffn_pre", "conv_b", "g_ffn_post", "conv_w")
ORDER = ("g_mix_pre", "w_in", "pool_w", "pool_scale", "w_out", "g_mix_post", "g_ffn_pre", "w_up", "conv_w", "conv_b",
         "w_down", "g_ffn_post")


def kernel(x, g_mix_pre, w_in, pool_w, pool_scale, w_out, g_mix_post, g_ffn_pre, w_up, conv_w, conv_b, w_down, g_ffn_post, loss_target, m_g_mix_pre, m_w_in, m_pool_w, m_pool_scale, m_w_out, m_g_mix_post, m_g_ffn_pre, m_w_up, m_conv_w, m_conv_b, m_w_down, m_g_ffn_post, v_g_mix_pre, v_w_in, v_pool_w, v_pool_scale, v_w_out, v_g_mix_post, v_g_ffn_pre, v_w_up, v_conv_w, v_conv_b, v_w_down, v_g_ffn_post):
    args = dict(locals())
    W = {n: args[n][0] for n in ORDER}
    M = {n: args["m_" + n][0] for n in ORDER}
    V = {n: args["v_" + n][0] for n in ORDER}
    for d in (W, M, V):
        d["pool_w"] = d["pool_w"].reshape(-1, POOL_GROUP)
        for n in ("g_mix_pre", "pool_scale", "g_mix_post", "g_ffn_pre", "conv_b", "g_ffn_post"):
            d[n] = d[n].reshape(1, -1)
    CW = W["w_up"].shape[1]
    c_arr = lax.axis_index("c").astype(jnp.int32).reshape(1)
    shard = 2 * lax.axis_index("x") + lax.axis_index("y")
    shard_arr = shard.astype(jnp.int32).reshape(1)
    device = 2 * shard + lax.axis_index("c")

    conv_w_slots = lax.dynamic_update_index_in_dim(jnp.zeros((N_SHARD,) + W["conv_w"].shape, F32), W["conv_w"], shard, 0)
    slots = {n: _cast_bf16(W[n], shard_arr, "cast_" + n) for n in BIG}
    w_in_g, conv_w_g = _gather_weights([slots["w_in"], conv_w_slots])
    conv_w_full = conv_w_g.transpose(1, 0, 2).reshape(CONV_WIDTH, 1, N_SHARD * CW)

    loss, grad_x, G = _local_step(
        x[0], loss_target[0], W["g_mix_pre"], w_in_g, W["pool_w"].reshape(-1, POOL_GROUP, POOL_GROUP), W["pool_scale"],
        slots["w_out"], W["g_mix_post"], W["g_ffn_pre"], slots["w_up"], conv_w_full, W["conv_b"],
        slots["w_down"], W["g_ffn_post"], (c_arr, device))

    late = ("w_in", "w_out")
    from_sibling = _swap_halves([G[n] for n in late], "mix")
    sums = {n: _pair_sum(G[n], r, c_arr, "pair_sum_" + n) for n, r in zip(late, from_sibling)}
    loss_rows = jnp.pad(loss, ((0, 7), (0, LANES - 1)))
    small = jnp.concatenate([_pack_small(G, SMALL_LATE), loss_rows], axis=0)
    small_slots = lax.dynamic_update_index_in_dim(jnp.zeros((8,) + small.shape, F32), small, device, 0)
    recvs, small_all = _scatter_grads([sums[n][1] for n in late], small_slots)
    reduced = {n: (sums[n][0], r) for n, r in zip(late, recvs)}
    reduced.update({n: G[n] for n in ("w_up", "w_down")})
    halves = [_shard_sum(*reduced[n], shard_arr, c_arr, "shard_sum_" + n) for n in BIG]
    full = dict(zip(BIG, _join_halves(halves)))
    full.update(_unpack_small(_small_sum(G["small_early"], "early"), SMALL_EARLY, W, shard))
    late_total = _small_sum(small_all, "late")
    full.update(_unpack_small(late_total, SMALL_LATE, W, shard))
    loss = late_total[-8, 0]

    delta, new_m, new_v = {}, {}, {}
    for n in BIG:
        full[n], delta[n], new_m[n], new_v[n] = _adamw_big(W[n], full[n], M[n], V[n], "adamw_" + n)
    ds, nms, nvs = _adamw_small([W[n] for n in SMALL], [full[n] for n in SMALL], [M[n] for n in SMALL],
                                [V[n] for n in SMALL])
    for n, d, nm, nv in zip(SMALL, ds, nms, nvs):
        delta[n], new_m[n], new_v[n] = d, nm, nv

    shaped = lambda d: [d[n].reshape(args[n].shape) for n in ORDER]
    return (loss, grad_x[None], *shaped(full), *shaped(delta), *shaped(new_m), *shaped(new_v))
```

```python
import functools

import jax
import jax.numpy as jnp
from jax import lax
from jax.experimental import pallas as pl
from jax.experimental.pallas import tpu as pltpu

F32 = jnp.float32
BF16 = jnp.bfloat16

RMS_EPS = 1e-6
NEG_INF = -1e30
N_HEADS = 8
HEAD_DIM = 64
ATTN_WIDTH = N_HEADS * HEAD_DIM
ATTN_SCALE = HEAD_DIM ** -0.5
ATTN_BLOCK = 128
DILATIONS = (1, 4, 16)
RESIDUES_PER_STEP = 4
CONSECUTIVE_BLOCKS = 4
POOL_WINDOWS = (2, 4, 8, 16)
POOL_GROUP = 128
POOL_WIDTH = POOL_GROUP * len(POOL_WINDOWS)
POOL_HALO = 16
CONV_WIDTH = 3
CONV_HALO = 8
N_SHARD = 4
LANES = 128
STAT_LANES = 16
STAT_WIDTH = N_HEADS * STAT_LANES

ADAM_LR = 0.001
ADAM_B1 = 0.9
ADAM_B2 = 0.999
ADAM_EPS = 1e-08
ADAM_WD = 0.01
ADAM_STEP = 10

VMEM_LIMIT = 60 * 1024 * 1024
MESH = pl.DeviceIdType.MESH
NT = (((1,), (1,)), ((), ()))
TN = (((0,), (0,)), ((), ()))


def _params(sem, vmem=None):
    return pltpu.CompilerParams(dimension_semantics=sem, vmem_limit_bytes=vmem)


def _const_spec(shape):
    zeros = (0,) * len(shape)
    return pl.BlockSpec(shape, lambda *_: zeros, pipeline_mode=pl.Buffered(1))


def _dot(a, b):
    return jnp.dot(a, b, preferred_element_type=F32)


def _dot_nt(a, b):
    return lax.dot_general(a, b, NT, preferred_element_type=F32)


def _dot_tn(a, b):
    return lax.dot_general(a, b, TN, preferred_element_type=F32)


def _rms_stats(x):
    r = lax.rsqrt(jnp.mean(x * x, axis=-1, keepdims=True) + RMS_EPS)
    return x * r, r


def _rms_bwd(dy, n, r, g):
    dg = jnp.sum(dy * n, axis=0, keepdims=True)
    dn = dy * g
    dx = r * (dn - n * jnp.mean(dn * n, axis=-1, keepdims=True))
    return dx, dg


def _gelu_tanh(g):
    k = 0.7978845608028654
    kc = k * 0.044715
    g2 = g * g
    t = jnp.tanh(g * (k + kc * g2))
    h = 0.5 * t + 0.5
    dh = (0.5 - 0.5 * (t * t)) * (k + (3.0 * kc) * g2)
    return g * h, h + g * dh


def _residue_shape(S, d, dtype, width=ATTN_WIDTH):
    return jax.ShapeDtypeStruct((S // d, d * width), dtype)


def _residue_spec(TM, d, width=ATTN_WIDTH):
    return pl.BlockSpec((TM // d, d * width), lambda i: (i, 0))


def _token_scratch(TM, width=ATTN_WIDTH):
    return [pltpu.VMEM((TM, LANES), F32)] * (width // LANES)


def _head_stat_matrix(pick_first_lane):
    r = lax.broadcasted_iota(jnp.int32, (ATTN_WIDTH, STAT_WIDTH), 0)
    c = lax.broadcasted_iota(jnp.int32, (ATTN_WIDTH, STAT_WIDTH), 1) // STAT_LANES
    return ((r == c * HEAD_DIM) if pick_first_lane else (r // HEAD_DIM == c)).astype(BF16)


def _bf16_pieces(x, n):
    pieces = []
    for _ in range(n):
        p = x.astype(BF16)
        pieces.append(p)
        x = x - p.astype(F32)
    return pieces


def _put_tokens(dst_s, val):
    for cb, chunk in enumerate(dst_s):
        chunk[...] = val[:, cb * LANES:(cb + 1) * LANES]


def _get_tokens(src_s):
    return jnp.concatenate([chunk[...] for chunk in src_s], axis=1)


def _to_residue(val, src_s, out_ref, d, dtype):
    if d == 1:
        out_ref[...] = val.astype(dtype)
        return
    rows = src_s[0].shape[0]
    for r in range(d):
        for cb, chunk in enumerate(src_s):
            col = (r * len(src_s) + cb) * LANES
            out_ref[:, col:col + LANES] = chunk[pl.ds(r, rows // d, stride=d), :].astype(dtype)


def _from_residue(in_ref, dst_s, d):
    if d == 1:
        return in_ref[...].astype(F32)
    rows = dst_s[0].shape[0]
    for r in range(d):
        for cb, chunk in enumerate(dst_s):
            col = (r * len(dst_s) + cb) * LANES
            chunk[pl.ds(r, rows // d, stride=d), :] = in_ref[:, col:col + LANES].astype(F32)
    return _get_tokens(dst_s)


def _mix_in_fwd(x, g_pre, w_in, cargo=()):
    S, D = x.shape
    TM = 512
    nc = len(cargo)
    kinds = [kind for kind, _ in cargo]
    n_chunks = ATTN_WIDTH // LANES

    def body(x_ref, g_ref, w_ref, *refs):
        cargo_in, refs = refs[:nc], refs[nc:]
        qkv_refs, p_ref, h_ref = refs[:9], refs[9], refs[10]
        t_s = refs[11 + nc:11 + nc + n_chunks]
        cargo_refs = (kinds, cargo_in, refs[11:11 + nc], refs[11 + nc + n_chunks:])
        if nc:
            _cargo_start(*cargo_refs, pl.program_id(0) == 0)
        n, _ = _rms_stats(x_ref[...])
        hb = (n * g_ref[...]).astype(BF16)
        h_ref[...] = hb
        for a in range(3):
            res = _dot(hb, w_ref[a])
            if a == 0:
                res = res * ATTN_SCALE
            _put_tokens(t_s, res)
            for i, d in enumerate(DILATIONS):
                _to_residue(res, t_s, qkv_refs[3 * i + a], d, BF16)
        p_ref[...] = _dot(hb, w_ref[3])
        if nc:
            _cargo_finish(*cargo_refs, pl.program_id(0) == S // TM - 1)

    row = lambda w: pl.BlockSpec((TM, w), lambda i: (i, 0))
    arrays, cargo_specs, shapes, aliases, sems = _cargo_call(cargo, 3, 11)
    out = pl.pallas_call(
        body, name="mix_in_fwd", grid=(S // TM,),
        in_specs=[row(D), _const_spec((1, D)), _const_spec(w_in.shape)] + cargo_specs,
        out_specs=[_residue_spec(TM, d) for d in DILATIONS for _ in range(3)] + [row(POOL_WIDTH), row(D)] + cargo_specs,
        out_shape=[_residue_shape(S, d, BF16) for d in DILATIONS for _ in range(3)]
        + [jax.ShapeDtypeStruct((S, POOL_WIDTH), F32), jax.ShapeDtypeStruct((S, D), BF16)] + shapes,
        input_output_aliases=aliases,
        scratch_shapes=_token_scratch(TM) + sems,
        compiler_params=_params(("arbitrary",), VMEM_LIMIT),
    )(x, g_pre, w_in, *arrays)
    return [out[0:3], out[3:6], out[6:9]], out[9], out[10], out[11:]


def _band_mask(n):
    qi = lax.broadcasted_iota(jnp.int32, (ATTN_BLOCK, 2 * ATTN_BLOCK), 0)
    ki = lax.broadcasted_iota(jnp.int32, (ATTN_BLOCK, 2 * ATTN_BLOCK), 1)
    dist = qi + ATTN_BLOCK - ki
    return (dist >= 0) & (dist <= ATTN_BLOCK) & ((ki >= ATTN_BLOCK) | (n > 0))


def _first_head_lanes():
    return lax.broadcasted_iota(jnp.int32, (1, LANES), 1) < HEAD_DIM


def _stack_heads(pair, first):
    zero = jnp.zeros_like(pair)
    return jnp.concatenate([jnp.where(first, pair, zero), jnp.where(first, zero, pair)], axis=0)


def _unstack_heads(stacked, first):
    return jnp.where(first, stacked[:ATTN_BLOCK], stacked[ATTN_BLOCK:])


CARGO_COPIES = {"ici": 3, "d2d": 3, "scatter": 3, "swap": 1, "everyone": 7}
CARGO_IN_PLACE = ("ici", "d2d", "everyone")


def _cargo_copies(kinds, ins, outs, send_sems, recv_sems, want_recvs=True):
    x, y, c, chips = _position()
    s = 2 * x + y
    me = 2 * s + c
    sibling = (x, y, 1 - c)
    sends, recvs = [], []

    def add(k, src, dst, landing, to):
        sends.append(_remote(src, dst, send_sems.at[k], recv_sems.at[k], to))
        if want_recvs:
            recvs.append(_remote(landing, landing, send_sems.at[k], recv_sems.at[k], to))

    k0 = 0
    for a, kind in enumerate(kinds):
        if kind == "swap":
            rows = ins[a].shape[1] // 2
            add(k0, ins[a].at[:, pl.ds((1 - c) * rows, rows), :], outs[a], outs[a], sibling)
        elif kind == "everyone":
            for m in range(1, 8):
                peer = (x ^ (m >> 2), y ^ ((m >> 1) & 1), c ^ (m & 1))
                add(k0 + m - 1, outs[a].at[me], outs[a].at[me], outs[a].at[4 * peer[0] + 2 * peer[1] + peer[2]], peer)
        else:
            for j, (px, py) in enumerate(chips):
                sj = 2 * px + py
                if kind == "scatter":
                    add(k0 + j, ins[a].at[sj], outs[a].at[j], outs[a].at[j], (px, py, c))
                    continue
                buf = outs[a]
                rows = buf.shape[1] // 2
                half = lambda shard, h: buf.at[shard, pl.ds(h * rows, rows), :]
                if kind == "ici":
                    add(k0 + j, half(s, c), half(s, c), half(sj, c), (px, py, c))
                else:
                    add(k0 + j, half(sj, c), half(sj, c), half(sj, 1 - c), sibling)
        k0 += CARGO_COPIES[kind]
    return sends, recvs


def _cargo_start(kinds, ins, outs, sems, first_step):
    @pl.when(first_step)
    def _():
        for cp in _cargo_copies(kinds, ins, outs, *sems, want_recvs=False)[0]:
            cp.start()


def _cargo_finish(kinds, ins, outs, sems, last_step):
    @pl.when(last_step)
    def _():
        sends, recvs = _cargo_copies(kinds, ins, outs, *sems)
        for cp in sends:
            cp.wait_send()
        for cp in recvs:
            cp.wait_recv()


def _cargo_call(cargo, n_in, n_out):
    arrays = [a for _, a in cargo]
    shapes = []
    for kind, a in cargo:
        shape = {"scatter": (3,) + a.shape[1:], "swap": (a.shape[0], a.shape[1] // 2, a.shape[2])}.get(kind, a.shape)
        shapes.append(jax.ShapeDtypeStruct(shape, a.dtype))
    aliases = {n_in + i: n_out + i for i, (kind, _) in enumerate(cargo) if kind in CARGO_IN_PLACE}
    n_sems = sum(CARGO_COPIES[kind] for kind, _ in cargo)
    sems = [pltpu.SemaphoreType.DMA((n_sems,))] * 2 if cargo else []
    return arrays, [ANY] * len(cargo), shapes, aliases, sems


def _attn_fwd(q, k, v, d, cargo=()):
    L = q.shape[0]
    group = min(d, RESIDUES_PER_STEP)
    width = group * ATTN_WIDTH
    qb = RESIDUES_PER_STEP // group
    steps = L // (qb * ATTN_BLOCK)
    nc = len(cargo)
    kinds = [kind for kind, _ in cargo]

    def body(*refs):
        q_ref, kp_ref, kc_ref, vp_ref, vc_ref = refs[:5]
        o_ref, lse_ref = refs[5 + nc:7 + nc]
        cargo_refs = (kinds, refs[5:5 + nc], refs[7 + nc:7 + 2 * nc], refs[7 + 2 * nc:])
        r, n = pl.program_id(0), pl.program_id(1)
        if nc:
            _cargo_start(*cargo_refs, (r == 0) & (n == 0))
        first = _first_head_lanes()
        for sub in range(qb):
            rows = slice(sub * ATTN_BLOCK, (sub + 1) * ATTN_BLOCK)
            valid = _band_mask(n if sub == 0 else 1)
            valid2 = jnp.concatenate([valid, valid], axis=0)
            for hp in range(width // LANES):
                sl = slice(hp * LANES, (hp + 1) * LANES)
                if sub == 0:
                    kk = jnp.concatenate([kp_ref[:, sl], kc_ref[rows, sl]], axis=0)
                    vv = jnp.concatenate([vp_ref[:, sl], vc_ref[rows, sl]], axis=0)
                else:
                    keys = slice((sub - 1) * ATTN_BLOCK, (sub + 1) * ATTN_BLOCK)
                    kk, vv = kc_ref[keys, sl], vc_ref[keys, sl]
                s = jnp.where(valid2, _dot_nt(_stack_heads(q_ref[rows, sl], first), kk), NEG_INF)
                m = jnp.max(s, axis=-1, keepdims=True)
                p = jnp.exp(s - m)
                den = jnp.sum(p, axis=-1, keepdims=True)
                o_ref[rows, sl] = _unstack_heads(_dot(p.astype(BF16), vv) / den, first).astype(BF16)
                lse_ref[rows, sl] = _unstack_heads(m + jnp.log(den), first)
        if nc:
            _cargo_finish(*cargo_refs, (r == d // group - 1) & (n == steps - 1))

    cur = pl.BlockSpec((qb * ATTN_BLOCK, width), lambda r, n: (n, r))
    prev = pl.BlockSpec((ATTN_BLOCK, width), lambda r, n: (jnp.maximum(n * qb - 1, 0), r))
    arrays, specs, shapes, aliases, sems = _cargo_call(cargo, 5, 2)
    out = pl.pallas_call(
        body, name=f"attn_fwd_d{d}", grid=(d // group, steps),
        in_specs=[cur, prev, cur, prev, cur] + specs,
        out_specs=[cur, cur] + specs,
        out_shape=[jax.ShapeDtypeStruct((L, d * ATTN_WIDTH), BF16), jax.ShapeDtypeStruct((L, d * ATTN_WIDTH), F32)] + shapes,
        input_output_aliases=aliases, scratch_shapes=sems,
        compiler_params=_params(("arbitrary", "arbitrary")),
    )(q, k, k, v, v, *arrays)
    return out[0], out[1], out[2:]


def _attn_mix(outs, lses):
    S = outs[0].shape[0]
    TM = 512
    n = len(DILATIONS)

    def body(*refs):
        o_refs, l_refs, attn_ref, lse_refs = refs[:n], refs[n:2 * n], refs[2 * n], refs[2 * n + 1:3 * n + 1]
        t_s, c_s = refs[3 * n + 1:-1], refs[-1:]
        os = [_from_residue(o_refs[i], t_s, d) for i, d in enumerate(DILATIONS)]
        ls = [_from_residue(l_refs[i], t_s, d) for i, d in enumerate(DILATIONS)]
        m = jnp.maximum(jnp.maximum(ls[0], ls[1]), ls[2])
        es = [jnp.exp(l - m) for l in ls]
        den = es[0] + es[1] + es[2]
        attn_ref[...] = ((es[0] * os[0] + es[1] * os[1] + es[2] * os[2]) / den).astype(BF16)
        pick = _head_stat_matrix(pick_first_lane=True)
        lse = sum(_dot(p, pick) for p in _bf16_pieces(m + jnp.log(den), 3))
        _put_tokens(c_s, lse)
        for i, d in enumerate(DILATIONS):
            _to_residue(lse, c_s, lse_refs[i], d, F32)

    specs = [_residue_spec(TM, d) for d in DILATIONS]
    out = pl.pallas_call(
        body, name="attn_mix", grid=(S // TM,),
        in_specs=specs * 2, out_specs=[specs[0]] + [_residue_spec(TM, d, STAT_WIDTH) for d in DILATIONS],
        out_shape=[jax.ShapeDtypeStruct((S, ATTN_WIDTH), BF16)]
        + [_residue_shape(S, d, F32, STAT_WIDTH) for d in DILATIONS],
        scratch_shapes=_token_scratch(TM) + _token_scratch(TM, STAT_WIDTH),
        compiler_params=_params(("parallel",)),
    )(*outs, *lses)
    return out[0], out[1:]


def _pool_counts(first_row, rows, w):
    t = first_row + lax.broadcasted_iota(jnp.int32, (rows, 1), 0)
    return jnp.minimum(t + 1, w).astype(F32)


def _trailing_sums(xe, w):
    s, k = xe, 1
    while k < w:
        s = s + pltpu.roll(s, k, 0)
        k *= 2
    return s


def _leading_sums(xe, w):
    rows = xe.shape[0]
    s, k = xe, 1
    while k < w:
        s = s + pltpu.roll(s, rows - k, 0)
        k *= 2
    return s


def _pooled_groups(halo, cur, first_row):
    TM = cur.shape[0]
    xe = jnp.concatenate([halo, cur], axis=0)
    out = []
    for g, w in enumerate(POOL_WINDOWS):
        a = xe[:, g * POOL_GROUP:(g + 1) * POOL_GROUP]
        s = _trailing_sums(a, w)[POOL_HALO:]
        out.append(s / _pool_counts(first_row, TM, w) - a[POOL_HALO:])
    return out


def _pool_fwd(pool_in, pool_w, pool_scale):
    S = pool_in.shape[0]
    TM = 512
    HB = TM // POOL_HALO

    def body(cur_ref, halo_ref, w_ref, sc_ref, y_ref):
        i = pl.program_id(0)
        halo = jnp.where(i > 0, halo_ref[...], 0.0)
        pooled = _pooled_groups(halo, cur_ref[...], i * TM)
        for g in range(len(POOL_WINDOWS)):
            sl = slice(g * POOL_GROUP, (g + 1) * POOL_GROUP)
            y = _dot(pooled[g].astype(BF16), w_ref[g].astype(BF16)) * sc_ref[:, sl]
            y_ref[:, sl] = y.astype(BF16)

    return pl.pallas_call(
        body, name="pool_fwd", grid=(S // TM,),
        in_specs=[pl.BlockSpec((TM, POOL_WIDTH), lambda i: (i, 0)),
                  pl.BlockSpec((POOL_HALO, POOL_WIDTH), lambda i: (jnp.maximum(i * HB - 1, 0), 0)),
                  _const_spec(pool_w.shape), _const_spec((1, POOL_WIDTH))],
        out_specs=pl.BlockSpec((TM, POOL_WIDTH), lambda i: (i, 0)),
        out_shape=jax.ShapeDtypeStruct((S, POOL_WIDTH), BF16),
        compiler_params=_params(("parallel",)),
    )(pool_in, pool_in, pool_w, pool_scale)


def _mix_out_fwd(attn, pool, w_out, x, g_post, g_ffn_pre):
    S, D = x.shape
    TM = 512

    def body(a_ref, p_ref, w_ref, x_ref, gp_ref, gf_ref, mixed_ref, x1_ref, h2_ref, cat_ref):
        ab = a_ref[...]
        cat_ref[:, :ATTN_WIDTH] = ab
        cat_ref[:, ATTN_WIDTH:] = p_ref[...]
        mixed = _dot(ab, w_ref[:ATTN_WIDTH, :]) + _dot(p_ref[...], w_ref[ATTN_WIDTH:, :])
        mixed_ref[...] = mixed.astype(BF16)
        n, _ = _rms_stats(mixed)
        x1 = x_ref[...] + n * gp_ref[...]
        x1_ref[...] = x1
        n2, _ = _rms_stats(x1)
        h2_ref[...] = (n2 * gf_ref[...]).astype(BF16)

    row = lambda w: pl.BlockSpec((TM, w), lambda i: (i, 0))
    return pl.pallas_call(
        body, name="mix_out_fwd", grid=(S // TM,),
        in_specs=[row(ATTN_WIDTH), row(POOL_WIDTH), _const_spec(w_out.shape), row(D),
                  _const_spec((1, D)), _const_spec((1, D))],
        out_specs=[row(D), row(D), row(D), row(D)],
        out_shape=[jax.ShapeDtypeStruct((S, D), BF16), jax.ShapeDtypeStruct((S, D), F32),
                   jax.ShapeDtypeStruct((S, D), BF16), jax.ShapeDtypeStruct((S, D), BF16)],
        compiler_params=_params(("parallel",), VMEM_LIMIT),
    )(attn, pool, w_out, x, g_post, g_ffn_pre)


def _ffn_fwd(h2, x1, target, w_up, w_down, conv_w, conv_b, g_post):
    S, D = x1.shape
    CW = w_up.shape[2]
    FF = 2 * CW
    TM = 256
    piece = 4 * LANES
    pieces = [(lo, min(lo + piece, CW)) for lo in range(0, CW, piece)]

    def body(h2_ref, x1_ref, t_ref, wu_ref, wd_ref, cw_ref, cb_ref, g_ref,
             yv_ref, dy_ref, df_ref, dc_ref, loss_ref, dg_ref, dcb_ref, dcw_ref,
             ue_s, dgate_s, dval_s):
        i = pl.program_id(0)

        @pl.when(i == 0)
        def _():
            loss_ref[...] = jnp.zeros_like(loss_ref)
            dg_ref[...] = jnp.zeros_like(dg_ref)
            dcb_ref[...] = jnp.zeros_like(dcb_ref)
            dcw_ref[...] = jnp.zeros_like(dcw_ref)
            ue_s[0:CONV_HALO, :] = jnp.zeros((CONV_HALO, 2 * FF), F32)

        @pl.when(i > 0)
        def _():
            ue_s[0:CONV_HALO, :] = ue_s[TM:TM + CONV_HALO, :]

        def shifted(cols, k):
            return pltpu.roll(ue_s[:, cols], k, 0)[CONV_HALO:]

        def conv(cols):
            return (cb_ref[:, cols] + cw_ref[2, :, cols] * ue_s[CONV_HALO:, cols]
                    + cw_ref[1, :, cols] * shifted(cols, 1) + cw_ref[0, :, cols] * shifted(cols, 2))

        hb = h2_ref[...]
        f = jnp.zeros((TM, D), F32)
        for j in range(2):
            jc = slice(j * CW, (j + 1) * CW)
            for half in range(2):
                blk = 2 * half + j
                cols = slice(blk * CW, (blk + 1) * CW)
                ue_s[CONV_HALO:, cols] = _dot(hb, wu_ref[blk])
            for lo, hi in pieces:
                pc = slice(j * CW + lo, j * CW + hi)
                gelu, dgelu = _gelu_tanh(conv(pc))
                val = conv(slice(FF + j * CW + lo, FF + j * CW + hi))
                dgate_s[:, pc] = val * dgelu
                dval_s[:, pc] = gelu
                yv_ref[:, pc] = (gelu * val).astype(BF16)
            f = f + _dot(yv_ref[:, jc], wd_ref[jc, :])

        n, r = _rms_stats(f)
        err = x1_ref[...] + n * g_ref[...] - t_ref[...]
        loss_ref[...] += 0.5 * jnp.sum(jnp.mean(err * err, axis=-1, keepdims=True), axis=0, keepdims=True)
        dy = err / D
        dy_ref[...] = dy
        df, dg = _rms_bwd(dy, n, r, g_ref[...])
        dg_ref[...] += dg
        dfb = df.astype(BF16)
        df_ref[...] = dfb

        for j in range(2):
            jc = slice(j * CW, (j + 1) * CW)
            dyv = _dot_nt(dfb, wd_ref[jc, :])
            for lo, hi in pieces:
                pc = slice(j * CW + lo, j * CW + hi)
                for half, scale_s in ((0, dgate_s), (1, dval_s)):
                    cols = slice(half * FF + j * CW + lo, half * FF + j * CW + hi)
                    dcv = dyv[:, lo:hi] * scale_s[:, pc]
                    dc_ref[:, cols] = dcv.astype(BF16)
                    dcb_ref[:, cols] += jnp.sum(dcv, axis=0, keepdims=True)
                    dcw_ref[2, :, cols] += jnp.sum(dcv * ue_s[CONV_HALO:, cols], axis=0, keepdims=True)
                    dcw_ref[1, :, cols] += jnp.sum(dcv * shifted(cols, 1), axis=0, keepdims=True)
                    dcw_ref[0, :, cols] += jnp.sum(dcv * shifted(cols, 2), axis=0, keepdims=True)

    row = lambda w: pl.BlockSpec((TM, w), lambda i: (i, 0))
    acc = lambda shape: pl.BlockSpec(shape, lambda i: (0,) * len(shape))
    return pl.pallas_call(
        body, name="ffn_fwd", grid=(S // TM,),
        in_specs=[row(D), row(D), row(D), _const_spec(w_up.shape), _const_spec(w_down.shape),
                  _const_spec(conv_w.shape), _const_spec((1, 2 * FF)), _const_spec((1, D))],
        out_specs=[row(FF), row(D), row(D), row(2 * FF),
                   acc((1, 1)), acc((1, D)), acc((1, 2 * FF)), acc((CONV_WIDTH, 1, 2 * FF))],
        out_shape=[jax.ShapeDtypeStruct((S, FF), BF16),
                   jax.ShapeDtypeStruct((S, D), F32), jax.ShapeDtypeStruct((S, D), BF16),
                   jax.ShapeDtypeStruct((S, 2 * FF), BF16),
                   jax.ShapeDtypeStruct((1, 1), F32), jax.ShapeDtypeStruct((1, D), F32),
                   jax.ShapeDtypeStruct((1, 2 * FF), F32), jax.ShapeDtypeStruct((CONV_WIDTH, 1, 2 * FF), F32)],
        scratch_shapes=[pltpu.VMEM((TM + CONV_HALO, 2 * FF), F32), pltpu.VMEM((TM, FF), F32),
                        pltpu.VMEM((TM, FF), F32)],
        compiler_params=_params(("arbitrary",), VMEM_LIMIT),
    )(h2, x1, target, w_up, w_down, conv_w, conv_b, g_post)


def _ffn_bwd(dc, conv_w, w_up, x1, g_ffn_pre, dy):
    S, D = x1.shape
    CW = w_up.shape[2]
    F2 = 4 * CW
    TM = 256
    HB = TM // CONV_HALO
    last = S // CONV_HALO - 1
    n_tiles = S // TM

    def body(dc_ref, halo_ref, cw_ref, wu_ref, x1_ref, g_ref, dy_ref, du_ref, dx1_ref, dg_ref):
        i = pl.program_id(0)

        @pl.when(i == 0)
        def _():
            dg_ref[...] = jnp.zeros_like(dg_ref)

        keep = i < n_tiles - 1
        dh2 = jnp.zeros((TM, D), F32)
        for blk in range(N_SHARD):
            cols = slice(blk * CW, (blk + 1) * CW)
            halo = jnp.where(keep, halo_ref[:, cols].astype(F32), 0.0)
            dce = jnp.concatenate([dc_ref[:, cols].astype(F32), halo], axis=0)
            rows = TM + CONV_HALO
            du = (cw_ref[2, :, cols] * dce[:TM]
                  + cw_ref[1, :, cols] * pltpu.roll(dce, rows - 1, 0)[:TM]
                  + cw_ref[0, :, cols] * pltpu.roll(dce, rows - 2, 0)[:TM])
            dub = du.astype(BF16)
            du_ref[:, cols] = dub
            dh2 = dh2 + _dot_nt(dub, wu_ref[blk])
        n2, r2 = _rms_stats(x1_ref[...])
        dx, dg = _rms_bwd(dh2, n2, r2, g_ref[...])
        dg_ref[...] += dg
        dx1_ref[...] = (dy_ref[...] + dx).astype(BF16)

    row = lambda w: pl.BlockSpec((TM, w), lambda i: (i, 0))
    return pl.pallas_call(
        body, name="ffn_bwd", grid=(S // TM,),
        in_specs=[row(F2), pl.BlockSpec((CONV_HALO, F2), lambda i: (jnp.minimum((i + 1) * HB, last), 0)),
                  _const_spec(conv_w.shape), _const_spec(w_up.shape), row(D), _const_spec((1, D)), row(D)],
        out_specs=[row(F2), row(D), pl.BlockSpec((1, D), lambda i: (0, 0))],
        out_shape=[jax.ShapeDtypeStruct((S, F2), BF16), jax.ShapeDtypeStruct((S, D), BF16),
                   jax.ShapeDtypeStruct((1, D), F32)],
        compiler_params=_params(("arbitrary",), VMEM_LIMIT),
    )(dc, dc, conv_w, w_up, x1, g_ffn_pre, dy)


def _matmul_tn(a, b, n_blocks, name):
    S, M = a.shape
    N = b.shape[1]
    tn = N // n_blocks
    tm = M if M <= 1024 else M // 2
    tk = 2048
    nk = S // tk

    def body(a_ref, b_ref, o_ref):
        @pl.when(pl.program_id(2) == 0)
        def _():
            o_ref[...] = jnp.zeros_like(o_ref)
        o_ref[0] += _dot_tn(a_ref[...], b_ref[...])

    return pl.pallas_call(
        body, name=name, grid=(M // tm, n_blocks, nk),
        in_specs=[pl.BlockSpec((tk, tm), lambda i, j, k: (k, i)), pl.BlockSpec((tk, tn), lambda i, j, k: (k, j))],
        out_specs=pl.BlockSpec((1, tm, tn), lambda i, j, k: (j, i, 0)),
        out_shape=jax.ShapeDtypeStruct((n_blocks, M, tn), F32),
        compiler_params=_params(("parallel", "parallel", "arbitrary"), VMEM_LIMIT),
    )(a, b)


def _mix_out_bwd(dx1, mixed, g_post, w_out, attn, cargo=()):
    S, D = dx1.shape
    TM = 512
    nd = len(DILATIONS)
    nc = len(cargo)
    kinds = [kind for kind, _ in cargo]
    n_chunks = ATTN_WIDTH // LANES

    def body(*refs):
        dx_ref, m_ref, g_ref, w_ref, a_ref = refs[:5]
        dm_ref, dp_ref, dg_ref = refs[5 + nc:8 + nc]
        da_refs, dl_refs = refs[8 + nc:8 + nc + nd], refs[8 + nc + nd:8 + nc + 2 * nd]
        n_out = 8 + nc + 2 * nd
        t_s = refs[n_out + nc:n_out + nc + n_chunks]
        c_s = refs[n_out + nc + n_chunks:n_out + nc + n_chunks + 1]
        cargo_refs = (kinds, refs[5:5 + nc], refs[n_out:n_out + nc], refs[n_out + nc + n_chunks + 1:])
        if nc:
            _cargo_start(*cargo_refs, pl.program_id(0) == 0)

        @pl.when(pl.program_id(0) == 0)
        def _():
            dg_ref[...] = jnp.zeros_like(dg_ref)

        n, r = _rms_stats(m_ref[...].astype(F32))
        dm, dg = _rms_bwd(dx_ref[...].astype(F32), n, r, g_ref[...])
        dg_ref[...] += dg
        dmb = dm.astype(BF16)
        dm_ref[...] = dmb
        da = _dot_nt(dmb, w_ref[:ATTN_WIDTH, :])
        _put_tokens(t_s, da)
        for i, d in enumerate(DILATIONS):
            _to_residue(da, t_s, da_refs[i], d, BF16)
        dp_ref[...] = _dot_nt(dmb, w_ref[ATTN_WIDTH:, :]).astype(BF16)
        gather = _head_stat_matrix(pick_first_lane=False)
        delta = sum(_dot(p, gather) for p in _bf16_pieces(da * a_ref[...].astype(F32), 2))
        _put_tokens(c_s, delta)
        for i, d in enumerate(DILATIONS):
            _to_residue(delta, c_s, dl_refs[i], d, F32)
        if nc:
            _cargo_finish(*cargo_refs, pl.program_id(0) == S // TM - 1)

    row = lambda w: pl.BlockSpec((TM, w), lambda i: (i, 0))
    specs = [_residue_spec(TM, d) for d in DILATIONS]
    arrays, cargo_specs, shapes, aliases, sems = _cargo_call(cargo, 5, 3 + 2 * nd)
    out = pl.pallas_call(
        body, name="mix_out_bwd", grid=(S // TM,),
        in_specs=[row(D), row(D), _const_spec((1, D)), _const_spec(w_out.shape), row(ATTN_WIDTH)] + cargo_specs,
        out_specs=[row(D), row(POOL_WIDTH), pl.BlockSpec((1, D), lambda i: (0, 0))] + specs
        + [_residue_spec(TM, d, STAT_WIDTH) for d in DILATIONS] + cargo_specs,
        out_shape=[jax.ShapeDtypeStruct((S, D), BF16), jax.ShapeDtypeStruct((S, POOL_WIDTH), BF16),
                   jax.ShapeDtypeStruct((1, D), F32)]
        + [_residue_shape(S, d, BF16) for d in DILATIONS]
        + [_residue_shape(S, d, F32, STAT_WIDTH) for d in DILATIONS] + shapes,
        input_output_aliases=aliases,
        scratch_shapes=_token_scratch(TM) + _token_scratch(TM, STAT_WIDTH) + sems,
        compiler_params=_params(("arbitrary",), VMEM_LIMIT),
    )(dx1, mixed, g_post, w_out, attn, *arrays)
    return out[0], out[1], out[2], out[3:3 + nd], out[3 + nd:3 + 2 * nd], out[3 + 2 * nd:]


def _pool_bwd(pool_in, d_pool, pool_w, pool_scale):
    S = pool_in.shape[0]
    TM = 512
    HB = TM // POOL_HALO
    last = S // POOL_HALO - 1
    G = len(POOL_WINDOWS)

    def body(cur_ref, halo_ref, dcur_ref, dnext_ref, w_ref, sc_ref, dxin_ref, dw_ref, dsc_ref):
        i = pl.program_id(0)

        @pl.when(i == 0)
        def _():
            dw_ref[...] = jnp.zeros_like(dw_ref)
            dsc_ref[...] = jnp.zeros_like(dsc_ref)

        halo = jnp.where(i > 0, halo_ref[...], 0.0)
        pooled = _pooled_groups(halo, cur_ref[...], i * TM)
        dnext = jnp.where(i < S // TM - 1, dnext_ref[...].astype(F32), 0.0)
        dye = jnp.concatenate([dcur_ref[...].astype(F32), dnext], axis=0)
        for g, w in enumerate(POOL_WINDOWS):
            sl = slice(g * POOL_GROUP, (g + 1) * POOL_GROUP)
            wg = w_ref[g].astype(BF16)
            pb = pooled[g].astype(BF16)
            dsc_ref[:, sl] += jnp.sum(dye[:TM, sl] * _dot(pb, wg), axis=0, keepdims=True)
            dpre = (dye[:, sl] * sc_ref[:, sl]).astype(BF16)
            dw_ref[g] += _dot_tn(pb, dpre[:TM])
            dpooled = _dot_nt(dpre, wg)
            z = dpooled / _pool_counts(i * TM, TM + POOL_HALO, w)
            dxin_ref[:, sl] = (_leading_sums(z, w)[:TM] - dpooled[:TM]).astype(BF16)

    row = pl.BlockSpec((TM, POOL_WIDTH), lambda i: (i, 0))
    return pl.pallas_call(
        body, name="pool_bwd", grid=(S // TM,),
        in_specs=[row, pl.BlockSpec((POOL_HALO, POOL_WIDTH), lambda i: (jnp.maximum(i * HB - 1, 0), 0)),
                  row, pl.BlockSpec((POOL_HALO, POOL_WIDTH), lambda i: (jnp.minimum((i + 1) * HB, last), 0)),
                  _const_spec(pool_w.shape), _const_spec((1, POOL_WIDTH))],
        out_specs=[row, pl.BlockSpec((G, POOL_GROUP, POOL_GROUP), lambda i: (0, 0, 0)),
                   pl.BlockSpec((1, POOL_WIDTH), lambda i: (0, 0))],
        out_shape=[jax.ShapeDtypeStruct((S, POOL_WIDTH), BF16), jax.ShapeDtypeStruct((G, POOL_GROUP, POOL_GROUP), F32),
                   jax.ShapeDtypeStruct((1, POOL_WIDTH), F32)],
        compiler_params=_params(("arbitrary",)),
    )(pool_in, pool_in, d_pool, d_pool, pool_w, pool_scale)


def _attn_bwd(q, k, v, d_attn, lse, delta, d, cargo=()):
    L = q.shape[0]
    nb = L // ATTN_BLOCK
    group = min(d, RESIDUES_PER_STEP)
    width = group * ATTN_WIDTH
    nc = len(cargo)
    kinds = [kind for kind, _ in cargo]

    def body(*refs):
        q_ref, kp_ref, kc_ref, vp_ref, vc_ref, do_ref, lse_ref, dl_ref = refs[:8]
        dq_ref, dk_ref, dv_ref = refs[8 + nc:11 + nc]
        ck_s, cv_s = refs[11 + 2 * nc:13 + 2 * nc]
        cargo_refs = (kinds, refs[8:8 + nc], refs[11 + nc:11 + 2 * nc], refs[13 + 2 * nc:])
        r, n = pl.program_id(0), pl.program_id(1)
        if nc:
            _cargo_start(*cargo_refs, (r == 0) & (n == 0))

        @pl.when(n == 0)
        def _():
            ck_s[...] = jnp.zeros_like(ck_s)
            cv_s[...] = jnp.zeros_like(cv_s)

        @pl.when(n < nb)
        def _():
            valid = _band_mask(n)
            valid2 = jnp.concatenate([valid, valid], axis=0)
            first = _first_head_lanes()

            def stacked_column(ref, hp):
                lane = 2 * hp * STAT_LANES
                return jnp.concatenate([ref[:, lane:lane + 1], ref[:, lane + STAT_LANES:lane + STAT_LANES + 1]], axis=0)

            for hp in range(width // LANES):
                sl = slice(hp * LANES, (hp + 1) * LANES)
                qq = _stack_heads(q_ref[:, sl], first)
                dd = _stack_heads(do_ref[:, sl], first)
                kk = jnp.concatenate([kp_ref[:, sl], kc_ref[:, sl]], axis=0)
                vv = jnp.concatenate([vp_ref[:, sl], vc_ref[:, sl]], axis=0)
                s = _dot_nt(qq, kk)
                p = jnp.where(valid2, jnp.exp(s - stacked_column(lse_ref, hp)), 0.0)
                dp = _dot_nt(dd, vv)
                ds = (p * (dp - stacked_column(dl_ref, hp))).astype(BF16)
                dq_ref[:, sl] = (_unstack_heads(_dot(ds, kk), first) * ATTN_SCALE).astype(BF16)
                dk = _dot_tn(ds, qq)
                dv = _dot_tn(p.astype(BF16), dd)
                dk_ref[:, sl] = (ck_s[:, sl] + dk[:ATTN_BLOCK]).astype(BF16)
                dv_ref[:, sl] = (cv_s[:, sl] + dv[:ATTN_BLOCK]).astype(BF16)
                ck_s[:, sl] = dk[ATTN_BLOCK:]
                cv_s[:, sl] = dv[ATTN_BLOCK:]

        @pl.when(n == nb)
        def _():
            dk_ref[...] = ck_s[...].astype(BF16)
            dv_ref[...] = cv_s[...].astype(BF16)

        if nc:
            _cargo_finish(*cargo_refs, (r == d // group - 1) & (n == nb))

    blk = (ATTN_BLOCK, width)
    cur = pl.BlockSpec(blk, lambda r, n: (jnp.minimum(n, nb - 1), r))
    stat = pl.BlockSpec((ATTN_BLOCK, group * STAT_WIDTH), lambda r, n: (jnp.minimum(n, nb - 1), r))
    prev = pl.BlockSpec(blk, lambda r, n: (jnp.maximum(jnp.minimum(n, nb - 1) - 1, 0), r))
    done = pl.BlockSpec(blk, lambda r, n: (jnp.maximum(n - 1, 0), r))
    arrays, specs, shapes, aliases, sems = _cargo_call(cargo, 8, 3)
    out = pl.pallas_call(
        body, name=f"attn_bwd_d{d}", grid=(d // group, nb + 1),
        in_specs=[cur, prev, cur, prev, cur, cur, stat, stat] + specs, out_specs=[cur, done, done] + specs,
        out_shape=[jax.ShapeDtypeStruct((L, d * ATTN_WIDTH), BF16)] * 3 + shapes,
        input_output_aliases=aliases,
        scratch_shapes=[pltpu.VMEM(blk, F32), pltpu.VMEM(blk, F32)] + sems,
        compiler_params=_params(("arbitrary", "arbitrary")),
    )(q, k, k, v, v, d_attn, lse, delta, *arrays)
    return out[:3], out[3:]


def _attn_bwd_consecutive(q, k, v, d_attn, lse, delta, cargo=()):
    L = q.shape[0]
    qb = CONSECUTIVE_BLOCKS
    steps = L // (qb * ATTN_BLOCK)
    nc = len(cargo)
    kinds = [kind for kind, _ in cargo]

    def body(*refs):
        q_ref, kp_ref, kc_ref, vp_ref, vc_ref, do_ref, lse_ref, dl_ref = refs[:8]
        dq_ref, dk_ref, dv_ref, ek_ref, ev_ref = refs[8 + nc:13 + nc]
        cargo_refs = (kinds, refs[8:8 + nc], refs[13 + nc:13 + 2 * nc], refs[13 + 2 * nc:])
        n = pl.program_id(0)
        if nc:
            _cargo_start(*cargo_refs, n == 0)
        first = _first_head_lanes()
        for hp in range(ATTN_WIDTH // LANES):
            sl = slice(hp * LANES, (hp + 1) * LANES)
            for sub in range(qb):
                rows = slice(sub * ATTN_BLOCK, (sub + 1) * ATTN_BLOCK)
                valid = _band_mask(n if sub == 0 else 1)
                valid2 = jnp.concatenate([valid, valid], axis=0)
                if sub == 0:
                    kk = jnp.concatenate([kp_ref[:, sl], kc_ref[rows, sl]], axis=0)
                    vv = jnp.concatenate([vp_ref[:, sl], vc_ref[rows, sl]], axis=0)
                else:
                    keys = slice((sub - 1) * ATTN_BLOCK, (sub + 1) * ATTN_BLOCK)
                    kk, vv = kc_ref[keys, sl], vc_ref[keys, sl]
                qq = _stack_heads(q_ref[rows, sl], first)
                dd = _stack_heads(do_ref[rows, sl], first)
                lane = 2 * hp * STAT_LANES
                column = lambda ref: jnp.concatenate(
                    [ref[rows, lane:lane + 1], ref[rows, lane + STAT_LANES:lane + STAT_LANES + 1]], axis=0)
                p = jnp.where(valid2, jnp.exp(_dot_nt(qq, kk) - column(lse_ref)), 0.0)
                ds = (p * (_dot_nt(dd, vv) - column(dl_ref))).astype(BF16)
                dq_ref[rows, sl] = (_unstack_heads(_dot(ds, kk), first) * ATTN_SCALE).astype(BF16)
                dk = _dot_tn(ds, qq)
                dv = _dot_tn(p.astype(BF16), dd)
                if sub == 0:
                    ek_ref[:, sl] = dk[:ATTN_BLOCK].astype(BF16)
                    ev_ref[:, sl] = dv[:ATTN_BLOCK].astype(BF16)
                else:
                    before = slice((sub - 1) * ATTN_BLOCK, sub * ATTN_BLOCK)
                    dk_ref[before, sl] = (carry_k + dk[:ATTN_BLOCK]).astype(BF16)
                    dv_ref[before, sl] = (carry_v + dv[:ATTN_BLOCK]).astype(BF16)
                carry_k, carry_v = dk[ATTN_BLOCK:], dv[ATTN_BLOCK:]
            dk_ref[rows, sl] = carry_k.astype(BF16)
            dv_ref[rows, sl] = carry_v.astype(BF16)
        if nc:
            _cargo_finish(*cargo_refs, n == steps - 1)

    cur = pl.BlockSpec((qb * ATTN_BLOCK, ATTN_WIDTH), lambda n: (n, 0))
    prev = pl.BlockSpec((ATTN_BLOCK, ATTN_WIDTH), lambda n: (jnp.maximum(n * qb - 1, 0), 0))
    edge = pl.BlockSpec((ATTN_BLOCK, ATTN_WIDTH), lambda n: (n, 0))
    stat = pl.BlockSpec((qb * ATTN_BLOCK, STAT_WIDTH), lambda n: (n, 0))
    arrays, specs, shapes, aliases, sems = _cargo_call(cargo, 8, 5)
    out = pl.pallas_call(
        body, name="attn_bwd_d1", grid=(steps,),
        in_specs=[cur, prev, cur, prev, cur, cur, stat, stat] + specs, out_specs=[cur, cur, cur, edge, edge] + specs,
        out_shape=[jax.ShapeDtypeStruct((L, ATTN_WIDTH), BF16)] * 3
        + [jax.ShapeDtypeStruct((steps * ATTN_BLOCK, ATTN_WIDTH), BF16)] * 2 + shapes,
        input_output_aliases=aliases, scratch_shapes=sems,
        compiler_params=_params(("arbitrary",)),
    )(q, k, k, v, v, d_attn, lse, delta, *arrays)
    return out[:3], out[3:5], out[5:]


def _mix_in_bwd(dqkv, edges, d_pool_in, w_in, x, g_pre, dx1):
    S, D = x.shape
    TM = CONSECUTIVE_BLOCKS * ATTN_BLOCK
    nd = len(DILATIONS)
    n_tiles = S // TM

    def body(*refs):
        g_refs = refs[:3 * nd]
        e_refs = (None,) + refs[3 * nd:3 * nd + 2]
        dpi_ref, w_ref, x_ref, g_ref, dx1_ref, dproj_ref, gx_ref, dg_ref = refs[3 * nd + 2:3 * nd + 10]
        t_s = refs[3 * nd + 10:]

        @pl.when(pl.program_id(0) == 0)
        def _():
            dg_ref[...] = jnp.zeros_like(dg_ref)

        dh = jnp.zeros((TM, D), F32)
        for a in range(4):
            if a < 3:
                tot = g_refs[a][...].astype(F32)
                if a > 0:
                    late = jnp.where(pl.program_id(0) < n_tiles - 1, e_refs[a][...].astype(F32), 0.0)
                    tot = jnp.concatenate([tot[:TM - ATTN_BLOCK], tot[TM - ATTN_BLOCK:] + late], axis=0)
                for i, d in enumerate(DILATIONS[1:]):
                    tot = tot + _from_residue(g_refs[3 * (i + 1) + a], t_s, d)
                db = tot.astype(BF16)
            else:
                db = dpi_ref[...]
            dproj_ref[:, a * ATTN_WIDTH:(a + 1) * ATTN_WIDTH] = db
            dh = dh + _dot_nt(db, w_ref[a])
        n, r = _rms_stats(x_ref[...])
        dx, dg = _rms_bwd(dh, n, r, g_ref[...])
        dg_ref[...] += dg
        gx_ref[...] = dx1_ref[...].astype(F32) + dx

    row = lambda w: pl.BlockSpec((TM, w), lambda i: (i, 0))
    edge = pl.BlockSpec((ATTN_BLOCK, ATTN_WIDTH), lambda i: (jnp.minimum(i + 1, n_tiles - 1), 0))
    return pl.pallas_call(
        body, name="mix_in_bwd", grid=(S // TM,),
        in_specs=[_residue_spec(TM, d) for d in DILATIONS for _ in range(3)] + [edge, edge]
        + [row(POOL_WIDTH), _const_spec(w_in.shape), row(D), _const_spec((1, D)), row(D)],
        out_specs=[row(4 * ATTN_WIDTH), row(D), pl.BlockSpec((1, D), lambda i: (0, 0))],
        out_shape=[jax.ShapeDtypeStruct((S, 4 * ATTN_WIDTH), BF16), jax.ShapeDtypeStruct((S, D), F32),
                   jax.ShapeDtypeStruct((1, D), F32)],
        scratch_shapes=_token_scratch(TM),
        compiler_params=_params(("arbitrary",), VMEM_LIMIT),
    )(*[g for gs in dqkv for g in gs], *edges, d_pool_in, w_in, x, g_pre, dx1)


SMALL_EARLY = ("pool_w", "pool_scale", "g_mix_post", "g_ffn_pre", "conv_b", "g_ffn_post", "conv_w")
SMALL_LATE = ("g_mix_pre",)


def _pack_small(grads, names):
    parts = []
    for n in names:
        g = grads[n]
        if n == "conv_w":
            g = g.reshape(CONV_WIDTH, N_SHARD, -1).transpose(1, 0, 2)
        parts.append(g.reshape(-1, LANES))
    return jnp.concatenate(parts, axis=0) if len(parts) > 1 else parts[0]


def _unpack_small(packed, names, like, shard):
    out, row = {}, 0
    for n in names:
        size = like[n].size * (N_SHARD if n == "conv_w" else 1)
        g = packed[row:row + size // LANES]
        row += size // LANES
        if n == "conv_w":
            g = lax.dynamic_slice_in_dim(g.reshape((N_SHARD,) + like[n].shape), shard, 1, axis=0)[0]
        out[n] = g.reshape(like[n].shape)
    return out


def _local_step(x, target, g_mix_pre, w_in, pool_w, pool_scale, w_out, g_mix_post, g_ffn_pre,
                w_up, conv_w, conv_b, w_down, g_ffn_post, mesh_pos=None):
    on_mesh = mesh_pos is not None
    D = x.shape[1]
    CW = w_up.shape[2]
    qkv, pool_in, h1, got = _mix_in_fwd(x, g_mix_pre, w_in, [("ici", w_up)] if on_mesh else ())
    w_up = got[0] if on_mesh else w_up
    o1, l1, got = _attn_fwd(*qkv[0], 1, [("d2d", w_up), ("ici", w_out), ("ici", w_down)] if on_mesh else ())
    w_up, w_out, w_down = got if on_mesh else (w_up, w_out, w_down)
    o4, l4, got = _attn_fwd(*qkv[1], 4, [("d2d", w_out), ("d2d", w_down)] if on_mesh else ())
    w_out, w_down = got if on_mesh else (w_out, w_down)
    o16, l16, _ = _attn_fwd(*qkv[2], 16)
    w_out = w_out.reshape(D, D)
    w_down = w_down.reshape(2 * CW, D)
    attn, lse = _attn_mix((o1, o4, o16), (l1, l4, l16))
    pool = _pool_fwd(pool_in, pool_w, pool_scale)
    mixed, x1, h2, cat = _mix_out_fwd(attn, pool, w_out, x, g_mix_post, g_ffn_pre)

    yv, dy, df, dc, loss, d_g_ffn_post, d_conv_b, d_conv_w = _ffn_fwd(
        h2, x1, target, w_up, w_down, conv_w, conv_b, g_ffn_post)
    du, dx1, d_g_ffn_pre = _ffn_bwd(dc, conv_w, w_up, x1, g_ffn_pre, dy)
    d_w_up = _matmul_tn(h2, du, N_SHARD, "grad_w_up")
    d_w_down = _matmul_tn(yv, df, 1, "grad_w_down")[0].reshape(N_SHARD, CW // 2, D)
    swap = [("swap", d_w_up), ("swap", d_w_down)] if on_mesh else ()
    d_mixed, d_pool, d_g_mix_post, d_attn, delta, from_sibling = _mix_out_bwd(dx1, mixed, g_mix_post, w_out, attn, swap)
    d_w_out = _matmul_tn(cat, d_mixed, 1, "grad_w_out")[0].reshape(N_SHARD, D // N_SHARD, D)
    d_pool_in, d_pool_w, d_pool_scale = _pool_bwd(pool_in, d_pool, pool_w, pool_scale)
    grads = dict(pool_w=d_pool_w, pool_scale=d_pool_scale, w_out=d_w_out, g_mix_post=d_g_mix_post,
                 g_ffn_pre=d_g_ffn_pre, w_up=d_w_up, conv_w=d_conv_w, conv_b=d_conv_b, w_down=d_w_down,
                 g_ffn_post=d_g_ffn_post)
    cargo = [(), (), ()]
    if on_mesh:
        c_arr, device = mesh_pos
        up_f32, up_bf16 = _pair_sum(d_w_up, from_sibling[0], c_arr, "pair_sum_w_up")
        down_f32, down_bf16 = _pair_sum(d_w_down, from_sibling[1], c_arr, "pair_sum_w_down")
        early = _pack_small(grads, SMALL_EARLY)
        early_slots = lax.dynamic_update_index_in_dim(jnp.zeros((8,) + early.shape, F32), early, device, 0)
        cargo = [[("scatter", down_bf16)], [("scatter", up_bf16)], [("everyone", early_slots)]]

    dqkv1, edges, landed1 = _attn_bwd_consecutive(*qkv[0], d_attn[0], lse[0], delta[0], cargo[0])
    dqkv, landed = zip(*[_attn_bwd(*qkv[i], d_attn[i], lse[i], delta[i], DILATIONS[i], cargo[i]) for i in (1, 2)])
    if on_mesh:
        grads.update(w_down=(down_f32, landed1[0]), w_up=(up_f32, landed[0][0]), small_early=landed[1][0])
    d_proj, grad_x, grads["g_mix_pre"] = _mix_in_bwd((dqkv1,) + dqkv, edges, d_pool_in, w_in, x, g_mix_pre, dx1)
    grads["w_in"] = _matmul_tn(h1, d_proj, N_SHARD, "grad_w_in")
    return loss, grad_x, grads


ANY = pl.BlockSpec(memory_space=pl.ANY)


def _position():
    x, y, c = lax.axis_index("x"), lax.axis_index("y"), lax.axis_index("c")
    chips = [(1 - x, y), (x, 1 - y), (1 - x, 1 - y)]
    return x, y, c, chips


def _remote(src, dst, send_sem, recv_sem, to):
    return pltpu.make_async_remote_copy(src_ref=src, dst_ref=dst, send_sem=send_sem, recv_sem=recv_sem,
                                        device_id=to, device_id_type=MESH)


def _cast_bf16(w, shard_arr, name):
    R, C = w.shape
    tr = R // 2

    def body(s_ref, w_ref, o_ref):
        o_ref[0] = w_ref[...].astype(BF16)

    return pl.pallas_call(
        body, name=name,
        grid_spec=pltpu.PrefetchScalarGridSpec(
            num_scalar_prefetch=1, grid=(2,),
            in_specs=[pl.BlockSpec((tr, C), lambda i, s_ref: (i, 0))],
            out_specs=pl.BlockSpec((1, tr, C), lambda i, s_ref: (s_ref[0], i, 0))),
        out_shape=jax.ShapeDtypeStruct((N_SHARD, R, C), BF16),
        compiler_params=_params(("parallel",)))(shard_arr, w)


def _gather_weights(bufs):
    n = len(bufs) - 1

    def body(*refs):
        outs, cw_out = refs[n + 1:2 * n + 1], refs[2 * n + 1]
        ici_send, ici_recv, d2d_send, d2d_recv = refs[2 * n + 2:]
        x, y, c, chips = _position()
        s = 2 * x + y
        sibling = (x, y, 1 - c)

        def half(a, shard, h):
            rows = outs[a].shape[1] // 2
            return outs[a].at[shard, pl.ds(h * rows, rows), :]

        sends = []
        for a in range(n):
            for j, (px, py) in enumerate(chips):
                sends.append(_remote(half(a, s, c), half(a, s, c),
                                     ici_send.at[3 * a + j], ici_recv.at[3 * a + j], (px, py, c)))
        for j, (px, py) in enumerate(chips):
            sends.append(_remote(cw_out.at[s], cw_out.at[s], ici_send.at[3 * n + j], ici_recv.at[3 * n + j], (px, py, c)))
        for cp in sends:
            cp.start()
        passed = []
        for a in range(n):
            for j, (px, py) in enumerate(chips):
                sj = 2 * px + py
                got = half(a, sj, c)
                _remote(got, got, ici_send.at[3 * a + j], ici_recv.at[3 * a + j], (px, py, c)).wait_recv()
                fwd = _remote(got, got, d2d_send.at[3 * a + j], d2d_recv.at[3 * a + j], sibling)
                fwd.start()
                passed.append(fwd)
        for j, (px, py) in enumerate(chips):
            got = cw_out.at[2 * px + py]
            _remote(got, got, ici_send.at[3 * n + j], ici_recv.at[3 * n + j], (px, py, c)).wait_recv()
        for a in range(n):
            for j, (px, py) in enumerate(chips):
                got = half(a, 2 * px + py, 1 - c)
                _remote(got, got, d2d_send.at[3 * a + j], d2d_recv.at[3 * a + j], sibling).wait_recv()
        for cp in sends + passed:
            cp.wait_send()

    return pl.pallas_call(
        body, name="gather_weights",
        in_specs=[ANY] * (n + 1), out_specs=[ANY] * (n + 1),
        out_shape=[jax.ShapeDtypeStruct(b.shape, b.dtype) for b in bufs],
        input_output_aliases={i: i for i in range(n + 1)},
        scratch_shapes=[pltpu.SemaphoreType.DMA((3 * n + 3,)), pltpu.SemaphoreType.DMA((3 * n + 3,)),
                        pltpu.SemaphoreType.DMA((3 * n,)), pltpu.SemaphoreType.DMA((3 * n,))],
        compiler_params=pltpu.CompilerParams(has_side_effects=True),
    )(*bufs)


def _swap_halves(grads, tag):
    n = len(grads)

    def body(*refs):
        ins, outs, send_sem, recv_sem = refs[:n], refs[n:2 * n], refs[2 * n], refs[2 * n + 1]
        x, y, c, _ = _position()
        copies = []
        for a in range(n):
            rows = ins[a].shape[1] // 2
            copies.append(_remote(ins[a].at[:, pl.ds((1 - c) * rows, rows), :], outs[a],
                                  send_sem.at[a], recv_sem.at[a], (x, y, 1 - c)))
        for cp in copies:
            cp.start()
        for cp in copies:
            cp.wait()

    return pl.pallas_call(
        body, name="swap_grad_halves_" + tag,
        in_specs=[ANY] * n, out_specs=[ANY] * n,
        out_shape=[jax.ShapeDtypeStruct((g.shape[0], g.shape[1] // 2, g.shape[2]), F32) for g in grads],
        scratch_shapes=[pltpu.SemaphoreType.DMA((n,)), pltpu.SemaphoreType.DMA((n,))],
        compiler_params=pltpu.CompilerParams(has_side_effects=True),
    )(*grads)


def _pair_sum(g, got, c_arr, name):
    n_sh, R, C = g.shape
    rows = R // 2

    def body(c_ref, g_ref, r_ref, f_ref, b_ref):
        t = g_ref[...] + r_ref[...]
        f_ref[...] = t
        b_ref[...] = t.astype(BF16)

    blk = pl.BlockSpec((1, rows, C), lambda i, c_ref: (i, 0, 0))
    return pl.pallas_call(
        body, name=name,
        grid_spec=pltpu.PrefetchScalarGridSpec(
            num_scalar_prefetch=1, grid=(n_sh,),
            in_specs=[pl.BlockSpec((1, rows, C), lambda i, c_ref: (i, c_ref[0], 0)), blk],
            out_specs=[blk, blk]),
        out_shape=[jax.ShapeDtypeStruct((n_sh, rows, C), F32), jax.ShapeDtypeStruct((n_sh, rows, C), BF16)],
        compiler_params=_params(("parallel",)),
    )(c_arr, g, got)


def _scatter_grads(sums_bf16, small_all):
    n = len(sums_bf16)

    def body(*refs):
        b_ins = refs[:n]
        recvs, sm = refs[n + 1:2 * n + 1], refs[2 * n + 1]
        ici_send, ici_recv, sm_send, sm_recv = refs[2 * n + 2:]
        x, y, c, chips = _position()
        me = 4 * x + 2 * y + c
        copies = []
        for a in range(n):
            for j, (px, py) in enumerate(chips):
                copies.append(_remote(b_ins[a].at[2 * px + py], recvs[a].at[j],
                                      ici_send.at[3 * a + j], ici_recv.at[3 * a + j], (px, py, c)))
        for k in range(1, 8):
            peer = (x ^ (k >> 2), y ^ ((k >> 1) & 1), c ^ (k & 1))
            copies.append(_remote(sm.at[me], sm.at[me], sm_send.at[k - 1], sm_recv.at[k - 1], peer))
        for cp in copies:
            cp.start()
        for cp in copies:
            cp.wait_send()
        for a in range(n):
            for j, (px, py) in enumerate(chips):
                _remote(recvs[a].at[j], recvs[a].at[j], ici_send.at[3 * a + j], ici_recv.at[3 * a + j],
                        (px, py, c)).wait_recv()
        for k in range(1, 8):
            peer = (x ^ (k >> 2), y ^ ((k >> 1) & 1), c ^ (k & 1))
            theirs = sm.at[4 * peer[0] + 2 * peer[1] + peer[2]]
            _remote(theirs, theirs, sm_send.at[k - 1], sm_recv.at[k - 1], peer).wait_recv()

    out = pl.pallas_call(
        body, name="scatter_grads",
        in_specs=[ANY] * (n + 1), out_specs=[ANY] * (n + 1),
        out_shape=[jax.ShapeDtypeStruct((3,) + b.shape[1:], BF16) for b in sums_bf16]
        + [jax.ShapeDtypeStruct(small_all.shape, F32)],
        input_output_aliases={n: n},
        scratch_shapes=[pltpu.SemaphoreType.DMA((3 * n,)), pltpu.SemaphoreType.DMA((3 * n,)),
                        pltpu.SemaphoreType.DMA((7,)), pltpu.SemaphoreType.DMA((7,))],
        compiler_params=pltpu.CompilerParams(has_side_effects=True),
    )(*sums_bf16, small_all)
    return out[:n], out[n]


def _shard_sum(sums_f32, recv, shard_arr, c_arr, name):
    _, rows, C = sums_f32.shape

    def body(s_ref, c_ref, o_ref, r_ref, t_ref):
        t_ref[...] = ((o_ref[0] + r_ref[0].astype(F32)) + r_ref[1].astype(F32)) + r_ref[2].astype(F32)

    return pl.pallas_call(
        body, name=name,
        grid_spec=pltpu.PrefetchScalarGridSpec(
            num_scalar_prefetch=2, grid=(1,),
            in_specs=[pl.BlockSpec((1, rows, C), lambda i, s_ref, c_ref: (s_ref[0], 0, 0)),
                      pl.BlockSpec((3, rows, C), lambda i, s_ref, c_ref: (0, 0, 0))],
            out_specs=pl.BlockSpec((rows, C), lambda i, s_ref, c_ref: (c_ref[0], 0))),
        out_shape=jax.ShapeDtypeStruct((2 * rows, C), F32),
        compiler_params=_params(("arbitrary",)),
    )(shard_arr, c_arr, sums_f32, recv)


def _join_halves(bufs):
    n = len(bufs)

    def body(*refs):
        outs, send_sem, recv_sem = refs[n:2 * n], refs[2 * n], refs[2 * n + 1]
        x, y, c, _ = _position()
        copies = []
        for a in range(n):
            rows = outs[a].shape[0] // 2
            mine = outs[a].at[pl.ds(c * rows, rows), :]
            copies.append(_remote(mine, mine, send_sem.at[a], recv_sem.at[a], (x, y, 1 - c)))
        for cp in copies:
            cp.start()
        for a, cp in enumerate(copies):
            cp.wait_send()
            rows = outs[a].shape[0] // 2
            theirs = outs[a].at[pl.ds((1 - c) * rows, rows), :]
            _remote(theirs, theirs, send_sem.at[a], recv_sem.at[a], (x, y, 1 - c)).wait_recv()

    return pl.pallas_call(
        body, name="join_grad_halves",
        in_specs=[ANY] * n, out_specs=[ANY] * n,
        out_shape=[jax.ShapeDtypeStruct(b.shape, F32) for b in bufs],
        input_output_aliases={i: i for i in range(n)},
        scratch_shapes=[pltpu.SemaphoreType.DMA((n,)), pltpu.SemaphoreType.DMA((n,))],
        compiler_params=pltpu.CompilerParams(has_side_effects=True),
    )(*bufs)


def _small_sum(parts, tag):
    _, R, C = parts.shape

    def body(p_ref, o_ref):
        t = p_ref[0]
        for k in range(1, 8):
            t = t + p_ref[k]
        o_ref[...] = t

    return pl.pallas_call(
        body, name="small_grad_sum_" + tag, grid=(1,),
        in_specs=[pl.BlockSpec((8, R, C), lambda i: (0, 0, 0))], out_specs=pl.BlockSpec((R, C), lambda i: (0, 0)),
        out_shape=jax.ShapeDtypeStruct((R, C), F32), compiler_params=_params(("arbitrary",)),
    )(parts)


def _adamw_math(w, g, m, v):
    m = ADAM_B1 * m + (1.0 - ADAM_B1) * g
    v = ADAM_B2 * v + (1.0 - ADAM_B2) * (g * g)
    m_hat = m / (1.0 - ADAM_B1 ** ADAM_STEP)
    v_hat = v / (1.0 - ADAM_B2 ** ADAM_STEP)
    delta = -ADAM_LR * (m_hat / (jnp.sqrt(v_hat) + ADAM_EPS) + ADAM_WD * w)
    return delta, m, v


def _adamw_big(w, g, m, v, name):
    R, C = w.shape
    tr = R // 4

    def body(w_ref, g_ref, m_ref, v_ref, d_ref, nm_ref, nv_ref):
        d_ref[...], nm_ref[...], nv_ref[...] = _adamw_math(w_ref[...], g_ref[...], m_ref[...], v_ref[...])

    blk = pl.BlockSpec((tr, C), lambda i: (i, 0))
    return pl.pallas_call(
        body, name=name, grid=(4,), in_specs=[blk] * 4, out_specs=[blk] * 3,
        out_shape=[jax.ShapeDtypeStruct((R, C), F32)] * 3, compiler_params=_params(("parallel",)),
    )(w, g, m, v)


def _adamw_small(ws, gs, ms, vs):
    n = len(ws)

    def body(*refs):
        for a in range(n):
            w, g, m, v = (refs[k * n + a][...] for k in range(4))
            d, nm, nv = _adamw_math(w, g, m, v)
            refs[4 * n + a][...] = d
            refs[5 * n + a][...] = nm
            refs[6 * n + a][...] = nv

    shapes = [jax.ShapeDtypeStruct(w.shape, F32) for w in ws]
    out = pl.pallas_call(body, name="adamw_small", out_shape=shapes * 3)(*ws, *gs, *ms, *vs)
    return out[:n], out[n:2 * n], out[2 * n:]


BIG = ("w_in", "w_out", "w_up", "w_down")
SMALL = ("g_mix_pre", "pool_w", "pool_scale", "g_mix_post", "g_ffn_pre", "conv_b", "g_ffn_post", "conv_w")
ORDER = ("g_mix_pre", "w_in", "pool_w", "pool_scale", "w_out", "g_mix_post", "g_ffn_pre", "w_up", "conv_w", "conv_b",
         "w_down", "g_ffn_post")


def kernel(x, g_mix_pre, w_in, pool_w, pool_scale, w_out, g_mix_post, g_ffn_pre, w_up, conv_w, conv_b, w_down, g_ffn_post, loss_target, m_g_mix_pre, m_w_in, m_pool_w, m_pool_scale, m_w_out, m_g_mix_post, m_g_ffn_pre, m_w_up, m_conv_w, m_conv_b, m_w_down, m_g_ffn_post, v_g_mix_pre, v_w_in, v_pool_w, v_pool_scale, v_w_out, v_g_mix_post, v_g_ffn_pre, v_w_up, v_conv_w, v_conv_b, v_w_down, v_g_ffn_post):
    args = dict(locals())
    W = {n: args[n][0] for n in ORDER}
    M = {n: args["m_" + n][0] for n in ORDER}
    V = {n: args["v_" + n][0] for n in ORDER}
    for d in (W, M, V):
        d["pool_w"] = d["pool_w"].reshape(-1, POOL_GROUP)
        for n in ("g_mix_pre", "pool_scale", "g_mix_post", "g_ffn_pre", "conv_b", "g_ffn_post"):
            d[n] = d[n].reshape(1, -1)
    CW = W["w_up"].shape[1]
    c_arr = lax.axis_index("c").astype(jnp.int32).reshape(1)
    shard = 2 * lax.axis_index("x") + lax.axis_index("y")
    shard_arr = shard.astype(jnp.int32).reshape(1)
    device = 2 * shard + lax.axis_index("c")

    conv_w_slots = lax.dynamic_update_index_in_dim(jnp.zeros((N_SHARD,) + W["conv_w"].shape, F32), W["conv_w"], shard, 0)
    slots = {n: _cast_bf16(W[n], shard_arr, "cast_" + n) for n in BIG}
    w_in_g, conv_w_g = _gather_weights([slots["w_in"], conv_w_slots])
    conv_w_full = conv_w_g.transpose(1, 0, 2).reshape(CONV_WIDTH, 1, N_SHARD * CW)

    loss, grad_x, G = _local_step(
        x[0], loss_target[0], W["g_mix_pre"], w_in_g, W["pool_w"].reshape(-1, POOL_GROUP, POOL_GROUP), W["pool_scale"],
        slots["w_out"], W["g_mix_post"], W["g_ffn_pre"], slots["w_up"], conv_w_full, W["conv_b"],
        slots["w_down"], W["g_ffn_post"], (c_arr, device))

    late = ("w_in", "w_out")
    from_sibling = _swap_halves([G[n] for n in late], "mix")
    sums = {n: _pair_sum(G[n], r, c_arr, "pair_sum_" + n) for n, r in zip(late, from_sibling)}
    loss_rows = jnp.pad(loss, ((0, 7), (0, LANES - 1)))
    small = jnp.concatenate([_pack_small(G, SMALL_LATE), loss_rows], axis=0)
    small_slots = lax.dynamic_update_index_in_dim(jnp.zeros((8,) + small.shape, F32), small, device, 0)
    recvs, small_all = _scatter_grads([sums[n][1] for n in late], small_slots)
    reduced = {n: (sums[n][0], r) for n, r in zip(late, recvs)}
    reduced.update({n: G[n] for n in ("w_up", "w_down")})
    halves = [_shard_sum(*reduced[n], shard_arr, c_arr, "shard_sum_" + n) for n in BIG]
    full = dict(zip(BIG, _join_halves(halves)))
    full.update(_unpack_small(_small_sum(G["small_early"], "early"), SMALL_EARLY, W, shard))
    late_total = _small_sum(small_all, "late")
    full.update(_unpack_small(late_total, SMALL_LATE, W, shard))
    loss = late_total[-8, 0]

    delta, new_m, new_v = {}, {}, {}
    for n in BIG:
        delta[n], new_m[n], new_v[n] = _adamw_big(W[n], full[n], M[n], V[n], "adamw_" + n)
    ds, nms, nvs = _adamw_small([W[n] for n in SMALL], [full[n] for n in SMALL], [M[n] for n in SMALL],
                                [V[n] for n in SMALL])
    for n, d, nm, nv in zip(SMALL, ds, nms, nvs):
        delta[n], new_m[n], new_v[n] = d, nm, nv

    shaped = lambda d: [d[n].reshape(args[n].shape) for n in ORDER]
    return (loss, grad_x[None], *shaped(full), *shaped(delta), *shaped(new_m), *shaped(new_v))
```

```python
import functools

import jax
import jax.numpy as jnp
from jax import lax
from jax.experimental import pallas as pl
from jax.experimental.pallas import tpu as pltpu

F32 = jnp.float32
BF16 = jnp.bfloat16

RMS_EPS = 1e-6
NEG_INF = -1e30
N_HEADS = 8
HEAD_DIM = 64
ATTN_WIDTH = N_HEADS * HEAD_DIM
ATTN_SCALE = HEAD_DIM ** -0.5
ATTN_BLOCK = 128
DILATIONS = (1, 4, 16)
RESIDUES_PER_STEP = 4
CONSECUTIVE_BLOCKS = 4
POOL_WINDOWS = (2, 4, 8, 16)
POOL_GROUP = 128
POOL_WIDTH = POOL_GROUP * len(POOL_WINDOWS)
POOL_HALO = 16
CONV_WIDTH = 3
CONV_HALO = 8
N_SHARD = 4
LANES = 128
STAT_LANES = 16
STAT_WIDTH = N_HEADS * STAT_LANES

ADAM_LR = 0.001
ADAM_B1 = 0.9
ADAM_B2 = 0.999
ADAM_EPS = 1e-08
ADAM_WD = 0.01
ADAM_STEP = 10

VMEM_LIMIT = 60 * 1024 * 1024
MESH = pl.DeviceIdType.MESH
NT = (((1,), (1,)), ((), ()))
TN = (((0,), (0,)), ((), ()))


def _params(sem, vmem=None):
    return pltpu.CompilerParams(dimension_semantics=sem, vmem_limit_bytes=vmem)


def _const_spec(shape):
    zeros = (0,) * len(shape)
    return pl.BlockSpec(shape, lambda *_: zeros, pipeline_mode=pl.Buffered(1))


def _dot(a, b):
    return jnp.dot(a, b, preferred_element_type=F32)


def _dot_nt(a, b):
    return lax.dot_general(a, b, NT, preferred_element_type=F32)


def _dot_tn(a, b):
    return lax.dot_general(a, b, TN, preferred_element_type=F32)


def _rms_stats(x):
    r = lax.rsqrt(jnp.mean(x * x, axis=-1, keepdims=True) + RMS_EPS)
    return x * r, r


def _rms_bwd(dy, n, r, g):
    dg = jnp.sum(dy * n, axis=0, keepdims=True)
    dn = dy * g
    dx = r * (dn - n * jnp.mean(dn * n, axis=-1, keepdims=True))
    return dx, dg


def _gelu_tanh(g):
    k = 0.7978845608028654
    kc = k * 0.044715
    g2 = g * g
    t = jnp.tanh(g * (k + kc * g2))
    h = 0.5 * t + 0.5
    dh = (0.5 - 0.5 * (t * t)) * (k + (3.0 * kc) * g2)
    return g * h, h + g * dh


def _residue_shape(S, d, dtype, width=ATTN_WIDTH):
    return jax.ShapeDtypeStruct((S // d, d * width), dtype)


def _residue_spec(TM, d, width=ATTN_WIDTH):
    return pl.BlockSpec((TM // d, d * width), lambda i: (i, 0))


def _token_scratch(TM, width=ATTN_WIDTH):
    return [pltpu.VMEM((TM, LANES), F32)] * (width // LANES)


def _head_stat_matrix(pick_first_lane):
    r = lax.broadcasted_iota(jnp.int32, (ATTN_WIDTH, STAT_WIDTH), 0)
    c = lax.broadcasted_iota(jnp.int32, (ATTN_WIDTH, STAT_WIDTH), 1) // STAT_LANES
    return ((r == c * HEAD_DIM) if pick_first_lane else (r // HEAD_DIM == c)).astype(BF16)


def _bf16_pieces(x, n):
    pieces = []
    for _ in range(n):
        p = x.astype(BF16)
        pieces.append(p)
        x = x - p.astype(F32)
    return pieces


def _put_tokens(dst_s, val):
    for cb, chunk in enumerate(dst_s):
        chunk[...] = val[:, cb * LANES:(cb + 1) * LANES]


def _get_tokens(src_s):
    return jnp.concatenate([chunk[...] for chunk in src_s], axis=1)


def _to_residue(val, src_s, out_ref, d, dtype):
    if d == 1:
        out_ref[...] = val.astype(dtype)
        return
    rows = src_s[0].shape[0]
    for r in range(d):
        for cb, chunk in enumerate(src_s):
            col = (r * len(src_s) + cb) * LANES
            out_ref[:, col:col + LANES] = chunk[pl.ds(r, rows // d, stride=d), :].astype(dtype)


def _from_residue(in_ref, dst_s, d):
    if d == 1:
        return in_ref[...].astype(F32)
    rows = dst_s[0].shape[0]
    for r in range(d):
        for cb, chunk in enumerate(dst_s):
            col = (r * len(dst_s) + cb) * LANES
            chunk[pl.ds(r, rows // d, stride=d), :] = in_ref[:, col:col + LANES].astype(F32)
    return _get_tokens(dst_s)


def _mix_in_fwd(x, g_pre, w_in, cargo=()):
    S, D = x.shape
    TM = 512
    nc = len(cargo)
    kinds = [kind for kind, _ in cargo]
    n_chunks = ATTN_WIDTH // LANES

    def body(x_ref, g_ref, w_ref, *refs):
        cargo_in, refs = refs[:nc], refs[nc:]
        qkv_refs, p_ref, h_ref = refs[:9], refs[9], refs[10]
        t_s = refs[11 + nc:11 + nc + n_chunks]
        cargo_refs = (kinds, cargo_in, refs[11:11 + nc], refs[11 + nc + n_chunks:])
        if nc:
            _cargo_start(*cargo_refs, pl.program_id(0) == 0)
        n, _ = _rms_stats(x_ref[...])
        hb = (n * g_ref[...]).astype(BF16)
        h_ref[...] = hb
        for a in range(3):
            res = _dot(hb, w_ref[a])
            if a == 0:
                res = res * ATTN_SCALE
            _put_tokens(t_s, res)
            for i, d in enumerate(DILATIONS):
                _to_residue(res, t_s, qkv_refs[3 * i + a], d, BF16)
        p_ref[...] = _dot(hb, w_ref[3])
        if nc:
            _cargo_finish(*cargo_refs, pl.program_id(0) == S // TM - 1)

    row = lambda w: pl.BlockSpec((TM, w), lambda i: (i, 0))
    arrays, cargo_specs, shapes, aliases, sems = _cargo_call(cargo, 3, 11)
    out = pl.pallas_call(
        body, name="mix_in_fwd", grid=(S // TM,),
        in_specs=[row(D), _const_spec((1, D)), _const_spec(w_in.shape)] + cargo_specs,
        out_specs=[_residue_spec(TM, d) for d in DILATIONS for _ in range(3)] + [row(POOL_WIDTH), row(D)] + cargo_specs,
        out_shape=[_residue_shape(S, d, BF16) for d in DILATIONS for _ in range(3)]
        + [jax.ShapeDtypeStruct((S, POOL_WIDTH), F32), jax.ShapeDtypeStruct((S, D), BF16)] + shapes,
        input_output_aliases=aliases,
        scratch_shapes=_token_scratch(TM) + sems,
        compiler_params=_params(("arbitrary",), VMEM_LIMIT),
    )(x, g_pre, w_in, *arrays)
    return [out[0:3], out[3:6], out[6:9]], out[9], out[10], out[11:]


def _band_mask(n):
    qi = lax.broadcasted_iota(jnp.int32, (ATTN_BLOCK, 2 * ATTN_BLOCK), 0)
    ki = lax.broadcasted_iota(jnp.int32, (ATTN_BLOCK, 2 * ATTN_BLOCK), 1)
    dist = qi + ATTN_BLOCK - ki
    return (dist >= 0) & (dist <= ATTN_BLOCK) & ((ki >= ATTN_BLOCK) | (n > 0))


def _first_head_lanes():
    return lax.broadcasted_iota(jnp.int32, (1, LANES), 1) < HEAD_DIM


def _stack_heads(pair, first):
    zero = jnp.zeros_like(pair)
    return jnp.concatenate([jnp.where(first, pair, zero), jnp.where(first, zero, pair)], axis=0)


def _unstack_heads(stacked, first):
    return jnp.where(first, stacked[:ATTN_BLOCK], stacked[ATTN_BLOCK:])


CARGO_COPIES = {"ici": 3, "d2d": 3, "scatter": 3, "swap": 1, "everyone": 7}
CARGO_IN_PLACE = ("ici", "d2d", "everyone")


def _cargo_copies(kinds, ins, outs, send_sems, recv_sems, want_recvs=True):
    x, y, c, chips = _position()
    s = 2 * x + y
    me = 2 * s + c
    sibling = (x, y, 1 - c)
    sends, recvs = [], []

    def add(k, src, dst, landing, to):
        sends.append(_remote(src, dst, send_sems.at[k], recv_sems.at[k], to))
        if want_recvs:
            recvs.append(_remote(landing, landing, send_sems.at[k], recv_sems.at[k], to))

    k0 = 0
    for a, kind in enumerate(kinds):
        if kind == "swap":
            rows = ins[a].shape[1] // 2
            add(k0, ins[a].at[:, pl.ds((1 - c) * rows, rows), :], outs[a], outs[a], sibling)
        elif kind == "everyone":
            for m in range(1, 8):
                peer = (x ^ (m >> 2), y ^ ((m >> 1) & 1), c ^ (m & 1))
                add(k0 + m - 1, outs[a].at[me], outs[a].at[me], outs[a].at[4 * peer[0] + 2 * peer[1] + peer[2]], peer)
        else:
            for j, (px, py) in enumerate(chips):
                sj = 2 * px + py
                if kind == "scatter":
                    add(k0 + j, ins[a].at[sj], outs[a].at[j], outs[a].at[j], (px, py, c))
                    continue
                buf = outs[a]
                rows = buf.shape[1] // 2
                half = lambda shard, h: buf.at[shard, pl.ds(h * rows, rows), :]
                if kind == "ici":
                    add(k0 + j, half(s, c), half(s, c), half(sj, c), (px, py, c))
                else:
                    add(k0 + j, half(sj, c), half(sj, c), half(sj, 1 - c), sibling)
        k0 += CARGO_COPIES[kind]
    return sends, recvs


def _cargo_start(kinds, ins, outs, sems, first_step):
    @pl.when(first_step)
    def _():
        for cp in _cargo_copies(kinds, ins, outs, *sems, want_recvs=False)[0]:
            cp.start()


def _cargo_finish(kinds, ins, outs, sems, last_step):
    @pl.when(last_step)
    def _():
        sends, recvs = _cargo_copies(kinds, ins, outs, *sems)
        for cp in sends:
            cp.wait_send()
        for cp in recvs:
            cp.wait_recv()


def _cargo_call(cargo, n_in, n_out):
    arrays = [a for _, a in cargo]
    shapes = []
    for kind, a in cargo:
        shape = {"scatter": (3,) + a.shape[1:], "swap": (a.shape[0], a.shape[1] // 2, a.shape[2])}.get(kind, a.shape)
        shapes.append(jax.ShapeDtypeStruct(shape, a.dtype))
    aliases = {n_in + i: n_out + i for i, (kind, _) in enumerate(cargo) if kind in CARGO_IN_PLACE}
    n_sems = sum(CARGO_COPIES[kind] for kind, _ in cargo)
    sems = [pltpu.SemaphoreType.DMA((n_sems,))] * 2 if cargo else []
    return arrays, [ANY] * len(cargo), shapes, aliases, sems


def _attn_fwd(q, k, v, d, cargo=()):
    L = q.shape[0]
    group = min(d, RESIDUES_PER_STEP)
    width = group * ATTN_WIDTH
    qb = RESIDUES_PER_STEP // group
    steps = L // (qb * ATTN_BLOCK)
    nc = len(cargo)
    kinds = [kind for kind, _ in cargo]

    def body(*refs):
        q_ref, kp_ref, kc_ref, vp_ref, vc_ref = refs[:5]
        o_ref, lse_ref = refs[5 + nc:7 + nc]
        cargo_refs = (kinds, refs[5:5 + nc], refs[7 + nc:7 + 2 * nc], refs[7 + 2 * nc:])
        r, n = pl.program_id(0), pl.program_id(1)
        if nc:
            _cargo_start(*cargo_refs, (r == 0) & (n == 0))
        first = _first_head_lanes()
        for sub in range(qb):
            rows = slice(sub * ATTN_BLOCK, (sub + 1) * ATTN_BLOCK)
            valid = _band_mask(n if sub == 0 else 1)
            valid2 = jnp.concatenate([valid, valid], axis=0)
            for hp in range(width // LANES):
                sl = slice(hp * LANES, (hp + 1) * LANES)
                if sub == 0:
                    kk = jnp.concatenate([kp_ref[:, sl], kc_ref[rows, sl]], axis=0)
                    vv = jnp.concatenate([vp_ref[:, sl], vc_ref[rows, sl]], axis=0)
                else:
                    keys = slice((sub - 1) * ATTN_BLOCK, (sub + 1) * ATTN_BLOCK)
                    kk, vv = kc_ref[keys, sl], vc_ref[keys, sl]
                s = jnp.where(valid2, _dot_nt(_stack_heads(q_ref[rows, sl], first), kk), NEG_INF)
                m = jnp.max(s, axis=-1, keepdims=True)
                p = jnp.exp(s - m)
                den = jnp.sum(p, axis=-1, keepdims=True)
                o_ref[rows, sl] = _unstack_heads(_dot(p.astype(BF16), vv) / den, first).astype(BF16)
                lse_ref[rows, sl] = _unstack_heads(m + jnp.log(den), first)
        if nc:
            _cargo_finish(*cargo_refs, (r == d // group - 1) & (n == steps - 1))

    cur = pl.BlockSpec((qb * ATTN_BLOCK, width), lambda r, n: (n, r))
    prev = pl.BlockSpec((ATTN_BLOCK, width), lambda r, n: (jnp.maximum(n * qb - 1, 0), r))
    arrays, specs, shapes, aliases, sems = _cargo_call(cargo, 5, 2)
    out = pl.pallas_call(
        body, name=f"attn_fwd_d{d}", grid=(d // group, steps),
        in_specs=[cur, prev, cur, prev, cur] + specs,
        out_specs=[cur, cur] + specs,
        out_shape=[jax.ShapeDtypeStruct((L, d * ATTN_WIDTH), BF16), jax.ShapeDtypeStruct((L, d * ATTN_WIDTH), F32)] + shapes,
        input_output_aliases=aliases, scratch_shapes=sems,
        compiler_params=_params(("arbitrary", "arbitrary")),
    )(q, k, k, v, v, *arrays)
    return out[0], out[1], out[2:]


def _attn_mix(outs, lses):
    S = outs[0].shape[0]
    TM = 512
    n = len(DILATIONS)

    def body(*refs):
        o_refs, l_refs, attn_ref, lse_refs = refs[:n], refs[n:2 * n], refs[2 * n], refs[2 * n + 1:3 * n + 1]
        t_s, c_s = refs[3 * n + 1:-1], refs[-1:]
        os = [_from_residue(o_refs[i], t_s, d) for i, d in enumerate(DILATIONS)]
        ls = [_from_residue(l_refs[i], t_s, d) for i, d in enumerate(DILATIONS)]
        m = jnp.maximum(jnp.maximum(ls[0], ls[1]), ls[2])
        es = [jnp.exp(l - m) for l in ls]
        den = es[0] + es[1] + es[2]
        attn_ref[...] = ((es[0] * os[0] + es[1] * os[1] + es[2] * os[2]) / den).astype(BF16)
        pick = _head_stat_matrix(pick_first_lane=True)
        lse = sum(_dot(p, pick) for p in _bf16_pieces(m + jnp.log(den), 3))
        _put_tokens(c_s, lse)
        for i, d in enumerate(DILATIONS):
            _to_residue(lse, c_s, lse_refs[i], d, F32)

    specs = [_residue_spec(TM, d) for d in DILATIONS]
    out = pl.pallas_call(
        body, name="attn_mix", grid=(S // TM,),
        in_specs=specs * 2, out_specs=[specs[0]] + [_residue_spec(TM, d, STAT_WIDTH) for d in DILATIONS],
        out_shape=[jax.ShapeDtypeStruct((S, ATTN_WIDTH), BF16)]
        + [_residue_shape(S, d, F32, STAT_WIDTH) for d in DILATIONS],
        scratch_shapes=_token_scratch(TM) + _token_scratch(TM, STAT_WIDTH),
        compiler_params=_params(("parallel",)),
    )(*outs, *lses)
    return out[0], out[1:]


def _pool_counts(first_row, rows, w):
    t = first_row + lax.broadcasted_iota(jnp.int32, (rows, 1), 0)
    return jnp.minimum(t + 1, w).astype(F32)


def _trailing_sums(xe, w):
    s, k = xe, 1
    while k < w:
        s = s + pltpu.roll(s, k, 0)
        k *= 2
    return s


def _leading_sums(xe, w):
    rows = xe.shape[0]
    s, k = xe, 1
    while k < w:
        s = s + pltpu.roll(s, rows - k, 0)
        k *= 2
    return s


def _pooled_groups(halo, cur, first_row):
    TM = cur.shape[0]
    xe = jnp.concatenate([halo, cur], axis=0)
    out = []
    for g, w in enumerate(POOL_WINDOWS):
        a = xe[:, g * POOL_GROUP:(g + 1) * POOL_GROUP]
        s = _trailing_sums(a, w)[POOL_HALO:]
        out.append(s / _pool_counts(first_row, TM, w) - a[POOL_HALO:])
    return out


def _pool_fwd(pool_in, pool_w, pool_scale):
    S = pool_in.shape[0]
    TM = 512
    HB = TM // POOL_HALO

    def body(cur_ref, halo_ref, w_ref, sc_ref, y_ref):
        i = pl.program_id(0)
        halo = jnp.where(i > 0, halo_ref[...], 0.0)
        pooled = _pooled_groups(halo, cur_ref[...], i * TM)
        for g in range(len(POOL_WINDOWS)):
            sl = slice(g * POOL_GROUP, (g + 1) * POOL_GROUP)
            y = _dot(pooled[g].astype(BF16), w_ref[g].astype(BF16)) * sc_ref[:, sl]
            y_ref[:, sl] = y.astype(BF16)

    return pl.pallas_call(
        body, name="pool_fwd", grid=(S // TM,),
        in_specs=[pl.BlockSpec((TM, POOL_WIDTH), lambda i: (i, 0)),
                  pl.BlockSpec((POOL_HALO, POOL_WIDTH), lambda i: (jnp.maximum(i * HB - 1, 0), 0)),
                  _const_spec(pool_w.shape), _const_spec((1, POOL_WIDTH))],
        out_specs=pl.BlockSpec((TM, POOL_WIDTH), lambda i: (i, 0)),
        out_shape=jax.ShapeDtypeStruct((S, POOL_WIDTH), BF16),
        compiler_params=_params(("parallel",)),
    )(pool_in, pool_in, pool_w, pool_scale)


def _mix_out_fwd(attn, pool, w_out, x, g_post, g_ffn_pre):
    S, D = x.shape
    TM = 512

    def body(a_ref, p_ref, w_ref, x_ref, gp_ref, gf_ref, mixed_ref, x1_ref, h2_ref, cat_ref):
        ab = a_ref[...]
        cat_ref[:, :ATTN_WIDTH] = ab
        cat_ref[:, ATTN_WIDTH:] = p_ref[...]
        mixed = _dot(ab, w_ref[:ATTN_WIDTH, :]) + _dot(p_ref[...], w_ref[ATTN_WIDTH:, :])
        mixed_ref[...] = mixed.astype(BF16)
        n, _ = _rms_stats(mixed)
        x1 = x_ref[...] + n * gp_ref[...]
        x1_ref[...] = x1
        n2, _ = _rms_stats(x1)
        h2_ref[...] = (n2 * gf_ref[...]).astype(BF16)

    row = lambda w: pl.BlockSpec((TM, w), lambda i: (i, 0))
    return pl.pallas_call(
        body, name="mix_out_fwd", grid=(S // TM,),
        in_specs=[row(ATTN_WIDTH), row(POOL_WIDTH), _const_spec(w_out.shape), row(D),
                  _const_spec((1, D)), _const_spec((1, D))],
        out_specs=[row(D), row(D), row(D), row(D)],
        out_shape=[jax.ShapeDtypeStruct((S, D), BF16), jax.ShapeDtypeStruct((S, D), F32),
                   jax.ShapeDtypeStruct((S, D), BF16), jax.ShapeDtypeStruct((S, D), BF16)],
        compiler_params=_params(("parallel",), VMEM_LIMIT),
    )(attn, pool, w_out, x, g_post, g_ffn_pre)


def _ffn_fwd(h2, x1, target, w_up, w_down, conv_w, conv_b, g_post):
    S, D = x1.shape
    CW = w_up.shape[2]
    FF = 2 * CW
    TM = 256
    piece = 4 * LANES
    pieces = [(lo, min(lo + piece, CW)) for lo in range(0, CW, piece)]

    def body(h2_ref, x1_ref, t_ref, wu_ref, wd_ref, cw_ref, cb_ref, g_ref,
             yv_ref, dy_ref, df_ref, dc_ref, loss_ref, dg_ref, dcb_ref, dcw_ref,
             ue_s, dgate_s, dval_s):
        i = pl.program_id(0)

        @pl.when(i == 0)
        def _():
            loss_ref[...] = jnp.zeros_like(loss_ref)
            dg_ref[...] = jnp.zeros_like(dg_ref)
            dcb_ref[...] = jnp.zeros_like(dcb_ref)
            dcw_ref[...] = jnp.zeros_like(dcw_ref)
            ue_s[0:CONV_HALO, :] = jnp.zeros((CONV_HALO, 2 * FF), F32)

        @pl.when(i > 0)
        def _():
            ue_s[0:CONV_HALO, :] = ue_s[TM:TM + CONV_HALO, :]

        def shifted(cols, k):
            return pltpu.roll(ue_s[:, cols], k, 0)[CONV_HALO:]

        def conv(cols):
            return (cb_ref[:, cols] + cw_ref[2, :, cols] * ue_s[CONV_HALO:, cols]
                    + cw_ref[1, :, cols] * shifted(cols, 1) + cw_ref[0, :, cols] * shifted(cols, 2))

        hb = h2_ref[...]
        f = jnp.zeros((TM, D), F32)
        for j in range(2):
            jc = slice(j * CW, (j + 1) * CW)
            for half in range(2):
                blk = 2 * half + j
                cols = slice(blk * CW, (blk + 1) * CW)
                ue_s[CONV_HALO:, cols] = _dot(hb, wu_ref[blk])
            for lo, hi in pieces:
                pc = slice(j * CW + lo, j * CW + hi)
                gelu, dgelu = _gelu_tanh(conv(pc).astype(BF16))
                val = conv(slice(FF + j * CW + lo, FF + j * CW + hi)).astype(BF16)
                dgate_s[:, pc] = val * dgelu
                dval_s[:, pc] = gelu
                yv_ref[:, pc] = gelu * val
            f = f + _dot(yv_ref[:, jc], wd_ref[jc, :])

        n, r = _rms_stats(f)
        err = x1_ref[...] + n * g_ref[...] - t_ref[...]
        loss_ref[...] += 0.5 * jnp.sum(jnp.mean(err * err, axis=-1, keepdims=True), axis=0, keepdims=True)
        dy = err / D
        dy_ref[...] = dy
        df, dg = _rms_bwd(dy, n, r, g_ref[...])
        dg_ref[...] += dg
        dfb = df.astype(BF16)
        df_ref[...] = dfb

        for j in range(2):
            jc = slice(j * CW, (j + 1) * CW)
            dyv = _dot_nt(dfb, wd_ref[jc, :])
            for lo, hi in pieces:
                pc = slice(j * CW + lo, j * CW + hi)
                for half, scale_s in ((0, dgate_s), (1, dval_s)):
                    cols = slice(half * FF + j * CW + lo, half * FF + j * CW + hi)
                    dcv = dyv[:, lo:hi] * scale_s[:, pc].astype(F32)
                    dc_ref[:, cols] = dcv.astype(BF16)
                    dcb_ref[:, cols] += jnp.sum(dcv, axis=0, keepdims=True)
                    dcw_ref[2, :, cols] += jnp.sum(dcv * ue_s[CONV_HALO:, cols], axis=0, keepdims=True)
                    dcw_ref[1, :, cols] += jnp.sum(dcv * shifted(cols, 1), axis=0, keepdims=True)
                    dcw_ref[0, :, cols] += jnp.sum(dcv * shifted(cols, 2), axis=0, keepdims=True)

    row = lambda w: pl.BlockSpec((TM, w), lambda i: (i, 0))
    acc = lambda shape: pl.BlockSpec(shape, lambda i: (0,) * len(shape))
    return pl.pallas_call(
        body, name="ffn_fwd", grid=(S // TM,),
        in_specs=[row(D), row(D), row(D), _const_spec(w_up.shape), _const_spec(w_down.shape),
                  _const_spec(conv_w.shape), _const_spec((1, 2 * FF)), _const_spec((1, D))],
        out_specs=[row(FF), row(D), row(D), row(2 * FF),
                   acc((1, 1)), acc((1, D)), acc((1, 2 * FF)), acc((CONV_WIDTH, 1, 2 * FF))],
        out_shape=[jax.ShapeDtypeStruct((S, FF), BF16),
                   jax.ShapeDtypeStruct((S, D), F32), jax.ShapeDtypeStruct((S, D), BF16),
                   jax.ShapeDtypeStruct((S, 2 * FF), BF16),
                   jax.ShapeDtypeStruct((1, 1), F32), jax.ShapeDtypeStruct((1, D), F32),
                   jax.ShapeDtypeStruct((1, 2 * FF), F32), jax.ShapeDtypeStruct((CONV_WIDTH, 1, 2 * FF), F32)],
        scratch_shapes=[pltpu.VMEM((TM + CONV_HALO, 2 * FF), F32), pltpu.VMEM((TM, FF), BF16),
                        pltpu.VMEM((TM, FF), BF16)],
        compiler_params=_params(("arbitrary",), VMEM_LIMIT),
    )(h2, x1, target, w_up, w_down, conv_w, conv_b, g_post)


def _ffn_bwd(dc, conv_w, w_up, x1, g_ffn_pre, dy):
    S, D = x1.shape
    CW = w_up.shape[2]
    F2 = 4 * CW
    TM = 256
    HB = TM // CONV_HALO
    last = S // CONV_HALO - 1
    n_tiles = S // TM

    def body(dc_ref, halo_ref, cw_ref, wu_ref, x1_ref, g_ref, dy_ref, du_ref, dx1_ref, dg_ref):
        i = pl.program_id(0)

        @pl.when(i == 0)
        def _():
            dg_ref[...] = jnp.zeros_like(dg_ref)

        keep = i < n_tiles - 1
        dh2 = jnp.zeros((TM, D), F32)
        for blk in range(N_SHARD):
            cols = slice(blk * CW, (blk + 1) * CW)
            halo = jnp.where(keep, halo_ref[:, cols].astype(F32), 0.0)
            dce = jnp.concatenate([dc_ref[:, cols].astype(F32), halo], axis=0)
            rows = TM + CONV_HALO
            du = (cw_ref[2, :, cols] * dce[:TM]
                  + cw_ref[1, :, cols] * pltpu.roll(dce, rows - 1, 0)[:TM]
                  + cw_ref[0, :, cols] * pltpu.roll(dce, rows - 2, 0)[:TM])
            dub = du.astype(BF16)
            du_ref[:, cols] = dub
            dh2 = dh2 + _dot_nt(dub, wu_ref[blk])
        n2, r2 = _rms_stats(x1_ref[...])
        dx, dg = _rms_bwd(dh2, n2, r2, g_ref[...])
        dg_ref[...] += dg
        dx1_ref[...] = (dy_ref[...] + dx).astype(BF16)

    row = lambda w: pl.BlockSpec((TM, w), lambda i: (i, 0))
    return pl.pallas_call(
        body, name="ffn_bwd", grid=(S // TM,),
        in_specs=[row(F2), pl.BlockSpec((CONV_HALO, F2), lambda i: (jnp.minimum((i + 1) * HB, last), 0)),
                  _const_spec(conv_w.shape), _const_spec(w_up.shape), row(D), _const_spec((1, D)), row(D)],
        out_specs=[row(F2), row(D), pl.BlockSpec((1, D), lambda i: (0, 0))],
        out_shape=[jax.ShapeDtypeStruct((S, F2), BF16), jax.ShapeDtypeStruct((S, D), BF16),
                   jax.ShapeDtypeStruct((1, D), F32)],
        compiler_params=_params(("arbitrary",), VMEM_LIMIT),
    )(dc, dc, conv_w, w_up, x1, g_ffn_pre, dy)


def _matmul_tn(a, b, n_blocks, name):
    S, M = a.shape
    N = b.shape[1]
    tn = N // n_blocks
    tm = M if M <= 1024 else M // 2
    tk = 2048
    nk = S // tk

    def body(a_ref, b_ref, o_ref):
        @pl.when(pl.program_id(2) == 0)
        def _():
            o_ref[...] = jnp.zeros_like(o_ref)
        o_ref[0] += _dot_tn(a_ref[...], b_ref[...])

    return pl.pallas_call(
        body, name=name, grid=(M // tm, n_blocks, nk),
        in_specs=[pl.BlockSpec((tk, tm), lambda i, j, k: (k, i)), pl.BlockSpec((tk, tn), lambda i, j, k: (k, j))],
        out_specs=pl.BlockSpec((1, tm, tn), lambda i, j, k: (j, i, 0)),
        out_shape=jax.ShapeDtypeStruct((n_blocks, M, tn), F32),
        compiler_params=_params(("parallel", "parallel", "arbitrary"), VMEM_LIMIT),
    )(a, b)


def _mix_out_bwd(dx1, mixed, g_post, w_out, attn, cargo=()):
    S, D = dx1.shape
    TM = 512
    nd = len(DILATIONS)
    nc = len(cargo)
    kinds = [kind for kind, _ in cargo]
    n_chunks = ATTN_WIDTH // LANES

    def body(*refs):
        dx_ref, m_ref, g_ref, w_ref, a_ref = refs[:5]
        dm_ref, dp_ref, dg_ref = refs[5 + nc:8 + nc]
        da_refs, dl_refs = refs[8 + nc:8 + nc + nd], refs[8 + nc + nd:8 + nc + 2 * nd]
        n_out = 8 + nc + 2 * nd
        t_s = refs[n_out + nc:n_out + nc + n_chunks]
        c_s = refs[n_out + nc + n_chunks:n_out + nc + n_chunks + 1]
        cargo_refs = (kinds, refs[5:5 + nc], refs[n_out:n_out + nc], refs[n_out + nc + n_chunks + 1:])
        if nc:
            _cargo_start(*cargo_refs, pl.program_id(0) == 0)

        @pl.when(pl.program_id(0) == 0)
        def _():
            dg_ref[...] = jnp.zeros_like(dg_ref)

        n, r = _rms_stats(m_ref[...].astype(F32))
        dm, dg = _rms_bwd(dx_ref[...].astype(F32), n, r, g_ref[...])
        dg_ref[...] += dg
        dmb = dm.astype(BF16)
        dm_ref[...] = dmb
        da = _dot_nt(dmb, w_ref[:ATTN_WIDTH, :])
        _put_tokens(t_s, da)
        for i, d in enumerate(DILATIONS):
            _to_residue(da, t_s, da_refs[i], d, BF16)
        dp_ref[...] = _dot_nt(dmb, w_ref[ATTN_WIDTH:, :]).astype(BF16)
        gather = _head_stat_matrix(pick_first_lane=False)
        delta = sum(_dot(p, gather) for p in _bf16_pieces(da * a_ref[...].astype(F32), 2))
        _put_tokens(c_s, delta)
        for i, d in enumerate(DILATIONS):
            _to_residue(delta, c_s, dl_refs[i], d, F32)
        if nc:
            _cargo_finish(*cargo_refs, pl.program_id(0) == S // TM - 1)

    row = lambda w: pl.BlockSpec((TM, w), lambda i: (i, 0))
    specs = [_residue_spec(TM, d) for d in DILATIONS]
    arrays, cargo_specs, shapes, aliases, sems = _cargo_call(cargo, 5, 3 + 2 * nd)
    out = pl.pallas_call(
        body, name="mix_out_bwd", grid=(S // TM,),
        in_specs=[row(D), row(D), _const_spec((1, D)), _const_spec(w_out.shape), row(ATTN_WIDTH)] + cargo_specs,
        out_specs=[row(D), row(POOL_WIDTH), pl.BlockSpec((1, D), lambda i: (0, 0))] + specs
        + [_residue_spec(TM, d, STAT_WIDTH) for d in DILATIONS] + cargo_specs,
        out_shape=[jax.ShapeDtypeStruct((S, D), BF16), jax.ShapeDtypeStruct((S, POOL_WIDTH), BF16),
                   jax.ShapeDtypeStruct((1, D), F32)]
        + [_residue_shape(S, d, BF16) for d in DILATIONS]
        + [_residue_shape(S, d, F32, STAT_WIDTH) for d in DILATIONS] + shapes,
        input_output_aliases=aliases,
        scratch_shapes=_token_scratch(TM) + _token_scratch(TM, STAT_WIDTH) + sems,
        compiler_params=_params(("arbitrary",), VMEM_LIMIT),
    )(dx1, mixed, g_post, w_out, attn, *arrays)
    return out[0], out[1], out[2], out[3:3 + nd], out[3 + nd:3 + 2 * nd], out[3 + 2 * nd:]


def _pool_bwd(pool_in, d_pool, pool_w, pool_scale):
    S = pool_in.shape[0]
    TM = 512
    HB = TM // POOL_HALO
    last = S // POOL_HALO - 1
    G = len(POOL_WINDOWS)

    def body(cur_ref, halo_ref, dcur_ref, dnext_ref, w_ref, sc_ref, dxin_ref, dw_ref, dsc_ref):
        i = pl.program_id(0)

        @pl.when(i == 0)
        def _():
            dw_ref[...] = jnp.zeros_like(dw_ref)
            dsc_ref[...] = jnp.zeros_like(dsc_ref)

        halo = jnp.where(i > 0, halo_ref[...], 0.0)
        pooled = _pooled_groups(halo, cur_ref[...], i * TM)
        dnext = jnp.where(i < S // TM - 1, dnext_ref[...].astype(F32), 0.0)
        dye = jnp.concatenate([dcur_ref[...].astype(F32), dnext], axis=0)
        for g, w in enumerate(POOL_WINDOWS):
            sl = slice(g * POOL_GROUP, (g + 1) * POOL_GROUP)
            wg = w_ref[g].astype(BF16)
            pb = pooled[g].astype(BF16)
            dsc_ref[:, sl] += jnp.sum(dye[:TM, sl] * _dot(pb, wg), axis=0, keepdims=True)
            dpre = (dye[:, sl] * sc_ref[:, sl]).astype(BF16)
            dw_ref[g] += _dot_tn(pb, dpre[:TM])
            dpooled = _dot_nt(dpre, wg)
            z = dpooled / _pool_counts(i * TM, TM + POOL_HALO, w)
            dxin_ref[:, sl] = (_leading_sums(z, w)[:TM] - dpooled[:TM]).astype(BF16)

    row = pl.BlockSpec((TM, POOL_WIDTH), lambda i: (i, 0))
    return pl.pallas_call(
        body, name="pool_bwd", grid=(S // TM,),
        in_specs=[row, pl.BlockSpec((POOL_HALO, POOL_WIDTH), lambda i: (jnp.maximum(i * HB - 1, 0), 0)),
                  row, pl.BlockSpec((POOL_HALO, POOL_WIDTH), lambda i: (jnp.minimum((i + 1) * HB, last), 0)),
                  _const_spec(pool_w.shape), _const_spec((1, POOL_WIDTH))],
        out_specs=[row, pl.BlockSpec((G, POOL_GROUP, POOL_GROUP), lambda i: (0, 0, 0)),
                   pl.BlockSpec((1, POOL_WIDTH), lambda i: (0, 0))],
        out_shape=[jax.ShapeDtypeStruct((S, POOL_WIDTH), BF16), jax.ShapeDtypeStruct((G, POOL_GROUP, POOL_GROUP), F32),
                   jax.ShapeDtypeStruct((1, POOL_WIDTH), F32)],
        compiler_params=_params(("arbitrary",)),
    )(pool_in, pool_in, d_pool, d_pool, pool_w, pool_scale)


def _attn_bwd(q, k, v, d_attn, lse, delta, d, cargo=()):
    L = q.shape[0]
    nb = L // ATTN_BLOCK
    group = min(d, RESIDUES_PER_STEP)
    width = group * ATTN_WIDTH
    nc = len(cargo)
    kinds = [kind for kind, _ in cargo]

    def body(*refs):
        q_ref, kp_ref, kc_ref, vp_ref, vc_ref, do_ref, lse_ref, dl_ref = refs[:8]
        dq_ref, dk_ref, dv_ref = refs[8 + nc:11 + nc]
        ck_s, cv_s = refs[11 + 2 * nc:13 + 2 * nc]
        cargo_refs = (kinds, refs[8:8 + nc], refs[11 + nc:11 + 2 * nc], refs[13 + 2 * nc:])
        r, n = pl.program_id(0), pl.program_id(1)
        if nc:
            _cargo_start(*cargo_refs, (r == 0) & (n == 0))

        @pl.when(n == 0)
        def _():
            ck_s[...] = jnp.zeros_like(ck_s)
            cv_s[...] = jnp.zeros_like(cv_s)

        @pl.when(n < nb)
        def _():
            valid = _band_mask(n)
            valid2 = jnp.concatenate([valid, valid], axis=0)
            first = _first_head_lanes()

            def stacked_column(ref, hp):
                lane = 2 * hp * STAT_LANES
                return jnp.concatenate([ref[:, lane:lane + 1], ref[:, lane + STAT_LANES:lane + STAT_LANES + 1]], axis=0)

            for hp in range(width // LANES):
                sl = slice(hp * LANES, (hp + 1) * LANES)
                qq = _stack_heads(q_ref[:, sl], first)
                dd = _stack_heads(do_ref[:, sl], first)
                kk = jnp.concatenate([kp_ref[:, sl], kc_ref[:, sl]], axis=0)
                vv = jnp.concatenate([vp_ref[:, sl], vc_ref[:, sl]], axis=0)
                s = _dot_nt(qq, kk)
                p = jnp.where(valid2, jnp.exp(s - stacked_column(lse_ref, hp)), 0.0)
                dp = _dot_nt(dd, vv)
                ds = (p * (dp - stacked_column(dl_ref, hp))).astype(BF16)
                dq_ref[:, sl] = (_unstack_heads(_dot(ds, kk), first) * ATTN_SCALE).astype(BF16)
                dk = _dot_tn(ds, qq)
                dv = _dot_tn(p.astype(BF16), dd)
                dk_ref[:, sl] = (ck_s[:, sl] + dk[:ATTN_BLOCK]).astype(BF16)
                dv_ref[:, sl] = (cv_s[:, sl] + dv[:ATTN_BLOCK]).astype(BF16)
                ck_s[:, sl] = dk[ATTN_BLOCK:]
                cv_s[:, sl] = dv[ATTN_BLOCK:]

        @pl.when(n == nb)
        def _():
            dk_ref[...] = ck_s[...].astype(BF16)
            dv_ref[...] = cv_s[...].astype(BF16)

        if nc:
            _cargo_finish(*cargo_refs, (r == d // group - 1) & (n == nb))

    blk = (ATTN_BLOCK, width)
    cur = pl.BlockSpec(blk, lambda r, n: (jnp.minimum(n, nb - 1), r))
    stat = pl.BlockSpec((ATTN_BLOCK, group * STAT_WIDTH), lambda r, n: (jnp.minimum(n, nb - 1), r))
    prev = pl.BlockSpec(blk, lambda r, n: (jnp.maximum(jnp.minimum(n, nb - 1) - 1, 0), r))
    done = pl.BlockSpec(blk, lambda r, n: (jnp.maximum(n - 1, 0), r))
    arrays, specs, shapes, aliases, sems = _cargo_call(cargo, 8, 3)
    out = pl.pallas_call(
        body, name=f"attn_bwd_d{d}", grid=(d // group, nb + 1),
        in_specs=[cur, prev, cur, prev, cur, cur, stat, stat] + specs, out_specs=[cur, done, done] + specs,
        out_shape=[jax.ShapeDtypeStruct((L, d * ATTN_WIDTH), BF16)] * 3 + shapes,
        input_output_aliases=aliases,
        scratch_shapes=[pltpu.VMEM(blk, F32), pltpu.VMEM(blk, F32)] + sems,
        compiler_params=_params(("arbitrary", "arbitrary")),
    )(q, k, k, v, v, d_attn, lse, delta, *arrays)
    return out[:3], out[3:]


def _attn_bwd_consecutive(q, k, v, d_attn, lse, delta, cargo=()):
    L = q.shape[0]
    qb = CONSECUTIVE_BLOCKS
    steps = L // (qb * ATTN_BLOCK)
    nc = len(cargo)
    kinds = [kind for kind, _ in cargo]

    def body(*refs):
        q_ref, kp_ref, kc_ref, vp_ref, vc_ref, do_ref, lse_ref, dl_ref = refs[:8]
        dq_ref, dk_ref, dv_ref, ek_ref, ev_ref = refs[8 + nc:13 + nc]
        cargo_refs = (kinds, refs[8:8 + nc], refs[13 + nc:13 + 2 * nc], refs[13 + 2 * nc:])
        n = pl.program_id(0)
        if nc:
            _cargo_start(*cargo_refs, n == 0)
        first = _first_head_lanes()
        for hp in range(ATTN_WIDTH // LANES):
            sl = slice(hp * LANES, (hp + 1) * LANES)
            for sub in range(qb):
                rows = slice(sub * ATTN_BLOCK, (sub + 1) * ATTN_BLOCK)
                valid = _band_mask(n if sub == 0 else 1)
                valid2 = jnp.concatenate([valid, valid], axis=0)
                if sub == 0:
                    kk = jnp.concatenate([kp_ref[:, sl], kc_ref[rows, sl]], axis=0)
                    vv = jnp.concatenate([vp_ref[:, sl], vc_ref[rows, sl]], axis=0)
                else:
                    keys = slice((sub - 1) * ATTN_BLOCK, (sub + 1) * ATTN_BLOCK)
                    kk, vv = kc_ref[keys, sl], vc_ref[keys, sl]
                qq = _stack_heads(q_ref[rows, sl], first)
                dd = _stack_heads(do_ref[rows, sl], first)
                lane = 2 * hp * STAT_LANES
                column = lambda ref: jnp.concatenate(
                    [ref[rows, lane:lane + 1], ref[rows, lane + STAT_LANES:lane + STAT_LANES + 1]], axis=0)
                p = jnp.where(valid2, jnp.exp(_dot_nt(qq, kk) - column(lse_ref)), 0.0)
                ds = (p * (_dot_nt(dd, vv) - column(dl_ref))).astype(BF16)
                dq_ref[rows, sl] = (_unstack_heads(_dot(ds, kk), first) * ATTN_SCALE).astype(BF16)
                dk = _dot_tn(ds, qq)
                dv = _dot_tn(p.astype(BF16), dd)
                if sub == 0:
                    ek_ref[:, sl] = dk[:ATTN_BLOCK].astype(BF16)
                    ev_ref[:, sl] = dv[:ATTN_BLOCK].astype(BF16)
                else:
                    before = slice((sub - 1) * ATTN_BLOCK, sub * ATTN_BLOCK)
                    dk_ref[before, sl] = (carry_k + dk[:ATTN_BLOCK]).astype(BF16)
                    dv_ref[before, sl] = (carry_v + dv[:ATTN_BLOCK]).astype(BF16)
                carry_k, carry_v = dk[ATTN_BLOCK:], dv[ATTN_BLOCK:]
            dk_ref[rows, sl] = carry_k.astype(BF16)
            dv_ref[rows, sl] = carry_v.astype(BF16)
        if nc:
            _cargo_finish(*cargo_refs, n == steps - 1)

    cur = pl.BlockSpec((qb * ATTN_BLOCK, ATTN_WIDTH), lambda n: (n, 0))
    prev = pl.BlockSpec((ATTN_BLOCK, ATTN_WIDTH), lambda n: (jnp.maximum(n * qb - 1, 0), 0))
    edge = pl.BlockSpec((ATTN_BLOCK, ATTN_WIDTH), lambda n: (n, 0))
    stat = pl.BlockSpec((qb * ATTN_BLOCK, STAT_WIDTH), lambda n: (n, 0))
    arrays, specs, shapes, aliases, sems = _cargo_call(cargo, 8, 5)
    out = pl.pallas_call(
        body, name="attn_bwd_d1", grid=(steps,),
        in_specs=[cur, prev, cur, prev, cur, cur, stat, stat] + specs, out_specs=[cur, cur, cur, edge, edge] + specs,
        out_shape=[jax.ShapeDtypeStruct((L, ATTN_WIDTH), BF16)] * 3
        + [jax.ShapeDtypeStruct((steps * ATTN_BLOCK, ATTN_WIDTH), BF16)] * 2 + shapes,
        input_output_aliases=aliases, scratch_shapes=sems,
        compiler_params=_params(("arbitrary",)),
    )(q, k, k, v, v, d_attn, lse, delta, *arrays)
    return out[:3], out[3:5], out[5:]


def _mix_in_bwd(dqkv, edges, d_pool_in, w_in, x, g_pre, dx1):
    S, D = x.shape
    TM = CONSECUTIVE_BLOCKS * ATTN_BLOCK
    nd = len(DILATIONS)
    n_tiles = S // TM

    def body(*refs):
        g_refs = refs[:3 * nd]
        e_refs = (None,) + refs[3 * nd:3 * nd + 2]
        dpi_ref, w_ref, x_ref, g_ref, dx1_ref, dproj_ref, gx_ref, dg_ref = refs[3 * nd + 2:3 * nd + 10]
        t_s = refs[3 * nd + 10:]

        @pl.when(pl.program_id(0) == 0)
        def _():
            dg_ref[...] = jnp.zeros_like(dg_ref)

        dh = jnp.zeros((TM, D), F32)
        for a in range(4):
            if a < 3:
                tot = g_refs[a][...].astype(F32)
                if a > 0:
                    late = jnp.where(pl.program_id(0) < n_tiles - 1, e_refs[a][...].astype(F32), 0.0)
                    tot = jnp.concatenate([tot[:TM - ATTN_BLOCK], tot[TM - ATTN_BLOCK:] + late], axis=0)
                for i, d in enumerate(DILATIONS[1:]):
                    tot = tot + _from_residue(g_refs[3 * (i + 1) + a], t_s, d)
                db = tot.astype(BF16)
            else:
                db = dpi_ref[...]
            dproj_ref[:, a * ATTN_WIDTH:(a + 1) * ATTN_WIDTH] = db
            dh = dh + _dot_nt(db, w_ref[a])
        n, r = _rms_stats(x_ref[...])
        dx, dg = _rms_bwd(dh, n, r, g_ref[...])
        dg_ref[...] += dg
        gx_ref[...] = dx1_ref[...].astype(F32) + dx

    row = lambda w: pl.BlockSpec((TM, w), lambda i: (i, 0))
    edge = pl.BlockSpec((ATTN_BLOCK, ATTN_WIDTH), lambda i: (jnp.minimum(i + 1, n_tiles - 1), 0))
    return pl.pallas_call(
        body, name="mix_in_bwd", grid=(S // TM,),
        in_specs=[_residue_spec(TM, d) for d in DILATIONS for _ in range(3)] + [edge, edge]
        + [row(POOL_WIDTH), _const_spec(w_in.shape), row(D), _const_spec((1, D)), row(D)],
        out_specs=[row(4 * ATTN_WIDTH), row(D), pl.BlockSpec((1, D), lambda i: (0, 0))],
        out_shape=[jax.ShapeDtypeStruct((S, 4 * ATTN_WIDTH), BF16), jax.ShapeDtypeStruct((S, D), F32),
                   jax.ShapeDtypeStruct((1, D), F32)],
        scratch_shapes=_token_scratch(TM),
        compiler_params=_params(("arbitrary",), VMEM_LIMIT),
    )(*[g for gs in dqkv for g in gs], *edges, d_pool_in, w_in, x, g_pre, dx1)


SMALL_EARLY = ("pool_w", "pool_scale", "g_mix_post", "g_ffn_pre", "conv_b", "g_ffn_post", "conv_w")
SMALL_LATE = ("g_mix_pre",)


def _pack_small(grads, names):
    parts = []
    for n in names:
        g = grads[n]
        if n == "conv_w":
            g = g.reshape(CONV_WIDTH, N_SHARD, -1).transpose(1, 0, 2)
        parts.append(g.reshape(-1, LANES))
    return jnp.concatenate(parts, axis=0) if len(parts) > 1 else parts[0]


def _unpack_small(packed, names, like, shard):
    out, row = {}, 0
    for n in names:
        size = like[n].size * (N_SHARD if n == "conv_w" else 1)
        g = packed[row:row + size // LANES]
        row += size // LANES
        if n == "conv_w":
            g = lax.dynamic_slice_in_dim(g.reshape((N_SHARD,) + like[n].shape), shard, 1, axis=0)[0]
        out[n] = g.reshape(like[n].shape)
    return out


def _local_step(x, target, g_mix_pre, w_in, pool_w, pool_scale, w_out, g_mix_post, g_ffn_pre,
                w_up, conv_w, conv_b, w_down, g_ffn_post, mesh_pos=None):
    on_mesh = mesh_pos is not None
    D = x.shape[1]
    CW = w_up.shape[2]
    qkv, pool_in, h1, got = _mix_in_fwd(x, g_mix_pre, w_in, [("ici", w_up)] if on_mesh else ())
    w_up = got[0] if on_mesh else w_up
    o1, l1, got = _attn_fwd(*qkv[0], 1, [("d2d", w_up), ("ici", w_out), ("ici", w_down)] if on_mesh else ())
    w_up, w_out, w_down = got if on_mesh else (w_up, w_out, w_down)
    o4, l4, got = _attn_fwd(*qkv[1], 4, [("d2d", w_out), ("d2d", w_down)] if on_mesh else ())
    w_out, w_down = got if on_mesh else (w_out, w_down)
    o16, l16, _ = _attn_fwd(*qkv[2], 16)
    w_out = w_out.reshape(D, D)
    w_down = w_down.reshape(2 * CW, D)
    attn, lse = _attn_mix((o1, o4, o16), (l1, l4, l16))
    pool = _pool_fwd(pool_in, pool_w, pool_scale)
    mixed, x1, h2, cat = _mix_out_fwd(attn, pool, w_out, x, g_mix_post, g_ffn_pre)

    yv, dy, df, dc, loss, d_g_ffn_post, d_conv_b, d_conv_w = _ffn_fwd(
        h2, x1, target, w_up, w_down, conv_w, conv_b, g_ffn_post)
    du, dx1, d_g_ffn_pre = _ffn_bwd(dc, conv_w, w_up, x1, g_ffn_pre, dy)
    d_w_up = _matmul_tn(h2, du, N_SHARD, "grad_w_up")
    d_w_down = _matmul_tn(yv, df, 1, "grad_w_down")[0].reshape(N_SHARD, CW // 2, D)
    swap = [("swap", d_w_up), ("swap", d_w_down)] if on_mesh else ()
    d_mixed, d_pool, d_g_mix_post, d_attn, delta, from_sibling = _mix_out_bwd(dx1, mixed, g_mix_post, w_out, attn, swap)
    d_w_out = _matmul_tn(cat, d_mixed, 1, "grad_w_out")[0].reshape(N_SHARD, D // N_SHARD, D)
    d_pool_in, d_pool_w, d_pool_scale = _pool_bwd(pool_in, d_pool, pool_w, pool_scale)
    grads = dict(pool_w=d_pool_w, pool_scale=d_pool_scale, w_out=d_w_out, g_mix_post=d_g_mix_post,
                 g_ffn_pre=d_g_ffn_pre, w_up=d_w_up, conv_w=d_conv_w, conv_b=d_conv_b, w_down=d_w_down,
                 g_ffn_post=d_g_ffn_post)
    cargo = [(), (), ()]
    if on_mesh:
        c_arr, device = mesh_pos
        up_f32, up_bf16 = _pair_sum(d_w_up, from_sibling[0], c_arr, "pair_sum_w_up")
        down_f32, down_bf16 = _pair_sum(d_w_down, from_sibling[1], c_arr, "pair_sum_w_down")
        early = _pack_small(grads, SMALL_EARLY)
        early_slots = lax.dynamic_update_index_in_dim(jnp.zeros((8,) + early.shape, F32), early, device, 0)
        cargo = [[("scatter", down_bf16)], [("scatter", up_bf16)], [("everyone", early_slots)]]

    dqkv1, edges, landed1 = _attn_bwd_consecutive(*qkv[0], d_attn[0], lse[0], delta[0], cargo[0])
    dqkv, landed = zip(*[_attn_bwd(*qkv[i], d_attn[i], lse[i], delta[i], DILATIONS[i], cargo[i]) for i in (1, 2)])
    if on_mesh:
        grads.update(w_down=(down_f32, landed1[0]), w_up=(up_f32, landed[0][0]), small_early=landed[1][0])
    d_proj, grad_x, grads["g_mix_pre"] = _mix_in_bwd((dqkv1,) + dqkv, edges, d_pool_in, w_in, x, g_mix_pre, dx1)
    grads["w_in"] = _matmul_tn(h1, d_proj, N_SHARD, "grad_w_in")
    return loss, grad_x, grads


ANY = pl.BlockSpec(memory_space=pl.ANY)


def _position():
    x, y, c = lax.axis_index("x"), lax.axis_index("y"), lax.axis_index("c")
    chips = [(1 - x, y), (x, 1 - y), (1 - x, 1 - y)]
    return x, y, c, chips


def _remote(src, dst, send_sem, recv_sem, to):
    return pltpu.make_async_remote_copy(src_ref=src, dst_ref=dst, send_sem=send_sem, recv_sem=recv_sem,
                                        device_id=to, device_id_type=MESH)


def _cast_bf16(w, shard_arr, name):
    R, C = w.shape
    tr = R // 2

    def body(s_ref, w_ref, o_ref):
        o_ref[0] = w_ref[...].astype(BF16)

    return pl.pallas_call(
        body, name=name,
        grid_spec=pltpu.PrefetchScalarGridSpec(
            num_scalar_prefetch=1, grid=(2,),
            in_specs=[pl.BlockSpec((tr, C), lambda i, s_ref: (i, 0))],
            out_specs=pl.BlockSpec((1, tr, C), lambda i, s_ref: (s_ref[0], i, 0))),
        out_shape=jax.ShapeDtypeStruct((N_SHARD, R, C), BF16),
        compiler_params=_params(("parallel",)))(shard_arr, w)


def _gather_weights(bufs):
    n = len(bufs) - 1

    def body(*refs):
        outs, cw_out = refs[n + 1:2 * n + 1], refs[2 * n + 1]
        ici_send, ici_recv, d2d_send, d2d_recv = refs[2 * n + 2:]
        x, y, c, chips = _position()
        s = 2 * x + y
        sibling = (x, y, 1 - c)

        def half(a, shard, h):
            rows = outs[a].shape[1] // 2
            return outs[a].at[shard, pl.ds(h * rows, rows), :]

        sends = []
        for a in range(n):
            for j, (px, py) in enumerate(chips):
                sends.append(_remote(half(a, s, c), half(a, s, c),
                                     ici_send.at[3 * a + j], ici_recv.at[3 * a + j], (px, py, c)))
        for j, (px, py) in enumerate(chips):
            sends.append(_remote(cw_out.at[s], cw_out.at[s], ici_send.at[3 * n + j], ici_recv.at[3 * n + j], (px, py, c)))
        for cp in sends:
            cp.start()
        passed = []
        for a in range(n):
            for j, (px, py) in enumerate(chips):
                sj = 2 * px + py
                got = half(a, sj, c)
                _remote(got, got, ici_send.at[3 * a + j], ici_recv.at[3 * a + j], (px, py, c)).wait_recv()
                fwd = _remote(got, got, d2d_send.at[3 * a + j], d2d_recv.at[3 * a + j], sibling)
                fwd.start()
                passed.append(fwd)
        for j, (px, py) in enumerate(chips):
            got = cw_out.at[2 * px + py]
            _remote(got, got, ici_send.at[3 * n + j], ici_recv.at[3 * n + j], (px, py, c)).wait_recv()
        for a in range(n):
            for j, (px, py) in enumerate(chips):
                got = half(a, 2 * px + py, 1 - c)
                _remote(got, got, d2d_send.at[3 * a + j], d2d_recv.at[3 * a + j], sibling).wait_recv()
        for cp in sends + passed:
            cp.wait_send()

    return pl.pallas_call(
        body, name="gather_weights",
        in_specs=[ANY] * (n + 1), out_specs=[ANY] * (n + 1),
        out_shape=[jax.ShapeDtypeStruct(b.shape, b.dtype) for b in bufs],
        input_output_aliases={i: i for i in range(n + 1)},
        scratch_shapes=[pltpu.SemaphoreType.DMA((3 * n + 3,)), pltpu.SemaphoreType.DMA((3 * n + 3,)),
                        pltpu.SemaphoreType.DMA((3 * n,)), pltpu.SemaphoreType.DMA((3 * n,))],
        compiler_params=pltpu.CompilerParams(has_side_effects=True),
    )(*bufs)


def _swap_halves(grads, tag):
    n = len(grads)

    def body(*refs):
        ins, outs, send_sem, recv_sem = refs[:n], refs[n:2 * n], refs[2 * n], refs[2 * n + 1]
        x, y, c, _ = _position()
        copies = []
        for a in range(n):
            rows = ins[a].shape[1] // 2
            copies.append(_remote(ins[a].at[:, pl.ds((1 - c) * rows, rows), :], outs[a],
                                  send_sem.at[a], recv_sem.at[a], (x, y, 1 - c)))
        for cp in copies:
            cp.start()
        for cp in copies:
            cp.wait()

    return pl.pallas_call(
        body, name="swap_grad_halves_" + tag,
        in_specs=[ANY] * n, out_specs=[ANY] * n,
        out_shape=[jax.ShapeDtypeStruct((g.shape[0], g.shape[1] // 2, g.shape[2]), F32) for g in grads],
        scratch_shapes=[pltpu.SemaphoreType.DMA((n,)), pltpu.SemaphoreType.DMA((n,))],
        compiler_params=pltpu.CompilerParams(has_side_effects=True),
    )(*grads)


def _pair_sum(g, got, c_arr, name):
    n_sh, R, C = g.shape
    rows = R // 2

    def body(c_ref, g_ref, r_ref, f_ref, b_ref):
        t = g_ref[...] + r_ref[...]
        f_ref[...] = t
        b_ref[...] = t.astype(BF16)

    blk = pl.BlockSpec((1, rows, C), lambda i, c_ref: (i, 0, 0))
    return pl.pallas_call(
        body, name=name,
        grid_spec=pltpu.PrefetchScalarGridSpec(
            num_scalar_prefetch=1, grid=(n_sh,),
            in_specs=[pl.BlockSpec((1, rows, C), lambda i, c_ref: (i, c_ref[0], 0)), blk],
            out_specs=[blk, blk]),
        out_shape=[jax.ShapeDtypeStruct((n_sh, rows, C), F32), jax.ShapeDtypeStruct((n_sh, rows, C), BF16)],
        compiler_params=_params(("parallel",)),
    )(c_arr, g, got)


def _scatter_grads(sums_bf16, small_all):
    n = len(sums_bf16)

    def body(*refs):
        b_ins = refs[:n]
        recvs, sm = refs[n + 1:2 * n + 1], refs[2 * n + 1]
        ici_send, ici_recv, sm_send, sm_recv = refs[2 * n + 2:]
        x, y, c, chips = _position()
        me = 4 * x + 2 * y + c
        copies = []
        for a in range(n):
            for j, (px, py) in enumerate(chips):
                copies.append(_remote(b_ins[a].at[2 * px + py], recvs[a].at[j],
                                      ici_send.at[3 * a + j], ici_recv.at[3 * a + j], (px, py, c)))
        for k in range(1, 8):
            peer = (x ^ (k >> 2), y ^ ((k >> 1) & 1), c ^ (k & 1))
            copies.append(_remote(sm.at[me], sm.at[me], sm_send.at[k - 1], sm_recv.at[k - 1], peer))
        for cp in copies:
            cp.start()
        for cp in copies:
            cp.wait_send()
        for a in range(n):
            for j, (px, py) in enumerate(chips):
                _remote(recvs[a].at[j], recvs[a].at[j], ici_send.at[3 * a + j], ici_recv.at[3 * a + j],
                        (px, py, c)).wait_recv()
        for k in range(1, 8):
            peer = (x ^ (k >> 2), y ^ ((k >> 1) & 1), c ^ (k & 1))
            theirs = sm.at[4 * peer[0] + 2 * peer[1] + peer[2]]
            _remote(theirs, theirs, sm_send.at[k - 1], sm_recv.at[k - 1], peer).wait_recv()

    out = pl.pallas_call(
        body, name="scatter_grads",
        in_specs=[ANY] * (n + 1), out_specs=[ANY] * (n + 1),
        out_shape=[jax.ShapeDtypeStruct((3,) + b.shape[1:], BF16) for b in sums_bf16]
        + [jax.ShapeDtypeStruct(small_all.shape, F32)],
        input_output_aliases={n: n},
        scratch_shapes=[pltpu.SemaphoreType.DMA((3 * n,)), pltpu.SemaphoreType.DMA((3 * n,)),
                        pltpu.SemaphoreType.DMA((7,)), pltpu.SemaphoreType.DMA((7,))],
        compiler_params=pltpu.CompilerParams(has_side_effects=True),
    )(*sums_bf16, small_all)
    return out[:n], out[n]


def _shard_sum(sums_f32, recv, shard_arr, c_arr, name):
    _, rows, C = sums_f32.shape

    def body(s_ref, c_ref, o_ref, r_ref, t_ref):
        t_ref[...] = ((o_ref[0] + r_ref[0].astype(F32)) + r_ref[1].astype(F32)) + r_ref[2].astype(F32)

    return pl.pallas_call(
        body, name=name,
        grid_spec=pltpu.PrefetchScalarGridSpec(
            num_scalar_prefetch=2, grid=(1,),
            in_specs=[pl.BlockSpec((1, rows, C), lambda i, s_ref, c_ref: (s_ref[0], 0, 0)),
                      pl.BlockSpec((3, rows, C), lambda i, s_ref, c_ref: (0, 0, 0))],
            out_specs=pl.BlockSpec((rows, C), lambda i, s_ref, c_ref: (c_ref[0], 0))),
        out_shape=jax.ShapeDtypeStruct((2 * rows, C), F32),
        compiler_params=_params(("arbitrary",)),
    )(shard_arr, c_arr, sums_f32, recv)


def _join_halves(bufs):
    n = len(bufs)

    def body(*refs):
        outs, send_sem, recv_sem = refs[n:2 * n], refs[2 * n], refs[2 * n + 1]
        x, y, c, _ = _position()
        copies = []
        for a in range(n):
            rows = outs[a].shape[0] // 2
            mine = outs[a].at[pl.ds(c * rows, rows), :]
            copies.append(_remote(mine, mine, send_sem.at[a], recv_sem.at[a], (x, y, 1 - c)))
        for cp in copies:
            cp.start()
        for a, cp in enumerate(copies):
            cp.wait_send()
            rows = outs[a].shape[0] // 2
            theirs = outs[a].at[pl.ds((1 - c) * rows, rows), :]
            _remote(theirs, theirs, send_sem.at[a], recv_sem.at[a], (x, y, 1 - c)).wait_recv()

    return pl.pallas_call(
        body, name="join_grad_halves",
        in_specs=[ANY] * n, out_specs=[ANY] * n,
        out_shape=[jax.ShapeDtypeStruct(b.shape, F32) for b in bufs],
        input_output_aliases={i: i for i in range(n)},
        scratch_shapes=[pltpu.SemaphoreType.DMA((n,)), pltpu.SemaphoreType.DMA((n,))],
        compiler_params=pltpu.CompilerParams(has_side_effects=True),
    )(*bufs)


def _small_sum(parts, tag):
    _, R, C = parts.shape

    def body(p_ref, o_ref):
        t = p_ref[0]
        for k in range(1, 8):
            t = t + p_ref[k]
        o_ref[...] = t

    return pl.pallas_call(
        body, name="small_grad_sum_" + tag, grid=(1,),
        in_specs=[pl.BlockSpec((8, R, C), lambda i: (0, 0, 0))], out_specs=pl.BlockSpec((R, C), lambda i: (0, 0)),
        out_shape=jax.ShapeDtypeStruct((R, C), F32), compiler_params=_params(("arbitrary",)),
    )(parts)


def _adamw_math(w, g, m, v):
    m = ADAM_B1 * m + (1.0 - ADAM_B1) * g
    v = ADAM_B2 * v + (1.0 - ADAM_B2) * (g * g)
    m_hat = m / (1.0 - ADAM_B1 ** ADAM_STEP)
    v_hat = v / (1.0 - ADAM_B2 ** ADAM_STEP)
    delta = -ADAM_LR * (m_hat / (jnp.sqrt(v_hat) + ADAM_EPS) + ADAM_WD * w)
    return delta, m, v


def _adamw_big(w, g, m, v, name):
    R, C = w.shape
    tr = R // 4

    def body(w_ref, g_ref, m_ref, v_ref, d_ref, nm_ref, nv_ref):
        d_ref[...], nm_ref[...], nv_ref[...] = _adamw_math(w_ref[...], g_ref[...], m_ref[...], v_ref[...])

    blk = pl.BlockSpec((tr, C), lambda i: (i, 0))
    return pl.pallas_call(
        body, name=name, grid=(4,), in_specs=[blk] * 4, out_specs=[blk] * 3,
        out_shape=[jax.ShapeDtypeStruct((R, C), F32)] * 3, compiler_params=_params(("parallel",)),
    )(w, g, m, v)


def _adamw_small(ws, gs, ms, vs):
    n = len(ws)

    def body(*refs):
        for a in range(n):
            w, g, m, v = (refs[k * n + a][...] for k in range(4))
            d, nm, nv = _adamw_math(w, g, m, v)
            refs[4 * n + a][...] = d
            refs[5 * n + a][...] = nm
            refs[6 * n + a][...] = nv

    shapes = [jax.ShapeDtypeStruct(w.shape, F32) for w in ws]
    out = pl.pallas_call(body, name="adamw_small", out_shape=shapes * 3)(*ws, *gs, *ms, *vs)
    return out[:n], out[n:2 * n], out[2 * n:]


BIG = ("w_in", "w_out", "w_up", "w_down")
SMALL = ("g_mix_pre", "pool_w", "pool_scale", "g_mix_post", "g_ffn_pre", "conv_b", "g_ffn_post", "conv_w")
ORDER = ("g_mix_pre", "w_in", "pool_w", "pool_scale", "w_out", "g_mix_post", "g_ffn_pre", "w_up", "conv_w", "conv_b",
         "w_down", "g_ffn_post")


def kernel(x, g_mix_pre, w_in, pool_w, pool_scale, w_out, g_mix_post, g_ffn_pre, w_up, conv_w, conv_b, w_down, g_ffn_post, loss_target, m_g_mix_pre, m_w_in, m_pool_w, m_pool_scale, m_w_out, m_g_mix_post, m_g_ffn_pre, m_w_up, m_conv_w, m_conv_b, m_w_down, m_g_ffn_post, v_g_mix_pre, v_w_in, v_pool_w, v_pool_scale, v_w_out, v_g_mix_post, v_g_ffn_pre, v_w_up, v_conv_w, v_conv_b, v_w_down, v_g_ffn_post):
    args = dict(locals())
    W = {n: args[n][0] for n in ORDER}
    M = {n: args["m_" + n][0] for n in ORDER}
    V = {n: args["v_" + n][0] for n in ORDER}
    for d in (W, M, V):
        d["pool_w"] = d["pool_w"].reshape(-1, POOL_GROUP)
        for n in ("g_mix_pre", "pool_scale", "g_mix_post", "g_ffn_pre", "conv_b", "g_ffn_post"):
            d[n] = d[n].reshape(1, -1)
    CW = W["w_up"].shape[1]
    c_arr = lax.axis_index("c").astype(jnp.int32).reshape(1)
    shard = 2 * lax.axis_index("x") + lax.axis_index("y")
    shard_arr = shard.astype(jnp.int32).reshape(1)
    device = 2 * shard + lax.axis_index("c")

    conv_w_slots = lax.dynamic_update_index_in_dim(jnp.zeros((N_SHARD,) + W["conv_w"].shape, F32), W["conv_w"], shard, 0)
    slots = {n: _cast_bf16(W[n], shard_arr, "cast_" + n) for n in BIG}
    w_in_g, conv_w_g = _gather_weights([slots["w_in"], conv_w_slots])
    conv_w_full = conv_w_g.transpose(1, 0, 2).reshape(CONV_WIDTH, 1, N_SHARD * CW)

    loss, grad_x, G = _local_step(
        x[0], loss_target[0], W["g_mix_pre"], w_in_g, W["pool_w"].reshape(-1, POOL_GROUP, POOL_GROUP), W["pool_scale"],
        slots["w_out"], W["g_mix_post"], W["g_ffn_pre"], slots["w_up"], conv_w_full, W["conv_b"],
        slots["w_down"], W["g_ffn_post"], (c_arr, device))

    late = ("w_in", "w_out")
    from_sibling = _swap_halves([G[n] for n in late], "mix")
    sums = {n: _pair_sum(G[n], r, c_arr, "pair_sum_" + n) for n, r in zip(late, from_sibling)}
    loss_rows = jnp.pad(loss, ((0, 7), (0, LANES - 1)))
    small = jnp.concatenate([_pack_small(G, SMALL_LATE), loss_rows], axis=0)
    small_slots = lax.dynamic_update_index_in_dim(jnp.zeros((8,) + small.shape, F32), small, device, 0)
    recvs, small_all = _scatter_grads([sums[n][1] for n in late], small_slots)
    reduced = {n: (sums[n][0], r) for n, r in zip(late, recvs)}
    reduced.update({n: G[n] for n in ("w_up", "w_down")})
    halves = [_shard_sum(*reduced[n], shard_arr, c_arr, "shard_sum_" + n) for n in BIG]
    full = dict(zip(BIG, _join_halves(halves)))
    full.update(_unpack_small(_small_sum(G["small_early"], "early"), SMALL_EARLY, W, shard))
    late_total = _small_sum(small_all, "late")
    full.update(_unpack_small(late_total, SMALL_LATE, W, shard))
    loss = late_total[-8, 0]

    delta, new_m, new_v = {}, {}, {}
    for n in BIG:
        delta[n], new_m[n], new_v[n] = _adamw_big(W[n], full[n], M[n], V[n], "adamw_" + n)
    ds, nms, nvs = _adamw_small([W[n] for n in SMALL], [full[n] for n in SMALL], [M[n] for n in SMALL],
                                [V[n] for n in SMALL])
    for n, d, nm, nv in zip(SMALL, ds, nms, nvs):
        delta[n], new_m[n], new_v[n] = d, nm, nv

    shaped = lambda d: [d[n].reshape(args[n].shape) for n in ORDER]
    return (loss, grad_x[None], *shaped(full), *shaped(delta), *shaped(new_m), *shaped(new_v))
```

```python
import functools

import jax
import jax.numpy as jnp
from jax import lax
from jax.experimental import pallas as pl
from jax.experimental.pallas import tpu as pltpu

F32 = jnp.float32
BF16 = jnp.bfloat16

RMS_EPS = 1e-6
NEG_INF = -1e30
N_HEADS = 8
HEAD_DIM = 64
ATTN_WIDTH = N_HEADS * HEAD_DIM
ATTN_SCALE = HEAD_DIM ** -0.5
ATTN_BLOCK = 128
DILATIONS = (1, 4, 16)
RESIDUES_PER_STEP = 4
CONSECUTIVE_BLOCKS = 4
POOL_WINDOWS = (2, 4, 8, 16)
POOL_GROUP = 128
POOL_WIDTH = POOL_GROUP * len(POOL_WINDOWS)
POOL_HALO = 16
CONV_WIDTH = 3
CONV_HALO = 8
N_SHARD = 4
LANES = 128
STAT_LANES = 16
STAT_WIDTH = N_HEADS * STAT_LANES

ADAM_LR = 0.001
ADAM_B1 = 0.9
ADAM_B2 = 0.999
ADAM_EPS = 1e-08
ADAM_WD = 0.01
ADAM_STEP = 10

VMEM_LIMIT = 60 * 1024 * 1024
MESH = pl.DeviceIdType.MESH
NT = (((1,), (1,)), ((), ()))
TN = (((0,), (0,)), ((), ()))


def _params(sem, vmem=None):
    return pltpu.CompilerParams(dimension_semantics=sem, vmem_limit_bytes=vmem)


def _const_spec(shape):
    zeros = (0,) * len(shape)
    return pl.BlockSpec(shape, lambda *_: zeros, pipeline_mode=pl.Buffered(1))


def _dot(a, b):
    return jnp.dot(a, b, preferred_element_type=F32)


def _dot_nt(a, b):
    return lax.dot_general(a, b, NT, preferred_element_type=F32)


def _dot_tn(a, b):
    return lax.dot_general(a, b, TN, preferred_element_type=F32)


def _rms_stats(x):
    r = lax.rsqrt(jnp.mean(x * x, axis=-1, keepdims=True) + RMS_EPS)
    return x * r, r


def _rms_bwd(dy, n, r, g):
    dg = jnp.sum(dy * n, axis=0, keepdims=True)
    dn = dy * g
    dx = r * (dn - n * jnp.mean(dn * n, axis=-1, keepdims=True))
    return dx, dg


def _gelu_tanh(g):
    k = 0.7978845608028654
    kc = k * 0.044715
    g2 = g * g
    t = jnp.tanh(g * (k + kc * g2))
    h = 0.5 * t + 0.5
    dh = (0.5 - 0.5 * (t * t)) * (k + (3.0 * kc) * g2)
    return g * h, h + g * dh


def _residue_shape(S, d, dtype, width=ATTN_WIDTH):
    return jax.ShapeDtypeStruct((S // d, d * width), dtype)


def _residue_spec(TM, d, width=ATTN_WIDTH):
    return pl.BlockSpec((TM // d, d * width), lambda i: (i, 0))


def _token_scratch(TM, width=ATTN_WIDTH):
    return [pltpu.VMEM((TM, LANES), F32)] * (width // LANES)


def _head_stat_matrix(pick_first_lane):
    r = lax.broadcasted_iota(jnp.int32, (ATTN_WIDTH, STAT_WIDTH), 0)
    c = lax.broadcasted_iota(jnp.int32, (ATTN_WIDTH, STAT_WIDTH), 1) // STAT_LANES
    return ((r == c * HEAD_DIM) if pick_first_lane else (r // HEAD_DIM == c)).astype(BF16)


def _bf16_pieces(x, n):
    pieces = []
    for _ in range(n):
        p = x.astype(BF16)
        pieces.append(p)
        x = x - p.astype(F32)
    return pieces


def _put_tokens(dst_s, val):
    for cb, chunk in enumerate(dst_s):
        chunk[...] = val[:, cb * LANES:(cb + 1) * LANES]


def _get_tokens(src_s):
    return jnp.concatenate([chunk[...] for chunk in src_s], axis=1)


def _to_residue(val, src_s, out_ref, d, dtype):
    if d == 1:
        out_ref[...] = val.astype(dtype)
        return
    rows = src_s[0].shape[0]
    for r in range(d):
        for cb, chunk in enumerate(src_s):
            col = (r * len(src_s) + cb) * LANES
            out_ref[:, col:col + LANES] = chunk[pl.ds(r, rows // d, stride=d), :].astype(dtype)


def _from_residue(in_ref, dst_s, d):
    if d == 1:
        return in_ref[...].astype(F32)
    rows = dst_s[0].shape[0]
    for r in range(d):
        for cb, chunk in enumerate(dst_s):
            col = (r * len(dst_s) + cb) * LANES
            chunk[pl.ds(r, rows // d, stride=d), :] = in_ref[:, col:col + LANES].astype(F32)
    return _get_tokens(dst_s)


def _mix_in_fwd(x, g_pre, w_in, cargo=()):
    S, D = x.shape
    TM = 512
    nc = len(cargo)
    kinds = [kind for kind, _ in cargo]
    n_chunks = ATTN_WIDTH // LANES

    def body(x_ref, g_ref, w_ref, *refs):
        cargo_in, refs = refs[:nc], refs[nc:]
        qkv_refs, p_ref, h_ref = refs[:9], refs[9], refs[10]
        t_s = refs[11 + nc:11 + nc + n_chunks]
        cargo_refs = (kinds, cargo_in, refs[11:11 + nc], refs[11 + nc + n_chunks:])
        if nc:
            _cargo_start(*cargo_refs, pl.program_id(0) == 0)
        n, _ = _rms_stats(x_ref[...])
        hb = (n * g_ref[...]).astype(BF16)
        h_ref[...] = hb
        for a in range(3):
            res = _dot(hb, w_ref[a])
            if a == 0:
                res = res * ATTN_SCALE
            _put_tokens(t_s, res)
            for i, d in enumerate(DILATIONS):
                _to_residue(res, t_s, qkv_refs[3 * i + a], d, BF16)
        p_ref[...] = _dot(hb, w_ref[3])
        if nc:
            _cargo_finish(*cargo_refs, pl.program_id(0) == S // TM - 1)

    row = lambda w: pl.BlockSpec((TM, w), lambda i: (i, 0))
    arrays, cargo_specs, shapes, aliases, sems = _cargo_call(cargo, 3, 11)
    out = pl.pallas_call(
        body, name="mix_in_fwd", grid=(S // TM,),
        in_specs=[row(D), _const_spec((1, D)), _const_spec(w_in.shape)] + cargo_specs,
        out_specs=[_residue_spec(TM, d) for d in DILATIONS for _ in range(3)] + [row(POOL_WIDTH), row(D)] + cargo_specs,
        out_shape=[_residue_shape(S, d, BF16) for d in DILATIONS for _ in range(3)]
        + [jax.ShapeDtypeStruct((S, POOL_WIDTH), F32), jax.ShapeDtypeStruct((S, D), BF16)] + shapes,
        input_output_aliases=aliases,
        scratch_shapes=_token_scratch(TM) + sems,
        compiler_params=_params(("arbitrary",), VMEM_LIMIT),
    )(x, g_pre, w_in, *arrays)
    return [out[0:3], out[3:6], out[6:9]], out[9], out[10], out[11:]


def _band_mask(n):
    qi = lax.broadcasted_iota(jnp.int32, (ATTN_BLOCK, 2 * ATTN_BLOCK), 0)
    ki = lax.broadcasted_iota(jnp.int32, (ATTN_BLOCK, 2 * ATTN_BLOCK), 1)
    dist = qi + ATTN_BLOCK - ki
    return (dist >= 0) & (dist <= ATTN_BLOCK) & ((ki >= ATTN_BLOCK) | (n > 0))


def _first_head_lanes():
    return lax.broadcasted_iota(jnp.int32, (1, LANES), 1) < HEAD_DIM


def _stack_heads(pair, first):
    zero = jnp.zeros_like(pair)
    return jnp.concatenate([jnp.where(first, pair, zero), jnp.where(first, zero, pair)], axis=0)


def _unstack_heads(stacked, first):
    return jnp.where(first, stacked[:ATTN_BLOCK], stacked[ATTN_BLOCK:])


CARGO_COPIES = {"ici": 3, "d2d": 3, "scatter": 3, "swap": 1, "everyone": 7, "join": 1}
CARGO_IN_PLACE = ("ici", "d2d", "everyone", "join")


def _cargo_copies(kinds, ins, outs, send_sems, recv_sems, want_recvs=True):
    x, y, c, chips = _position()
    s = 2 * x + y
    me = 2 * s + c
    sibling = (x, y, 1 - c)
    sends, recvs = [], []

    def add(k, src, dst, landing, to):
        sends.append(_remote(src, dst, send_sems.at[k], recv_sems.at[k], to))
        if want_recvs:
            recvs.append(_remote(landing, landing, send_sems.at[k], recv_sems.at[k], to))

    k0 = 0
    for a, kind in enumerate(kinds):
        if kind == "swap":
            rows = ins[a].shape[1] // 2
            add(k0, ins[a].at[:, pl.ds((1 - c) * rows, rows), :], outs[a], outs[a], sibling)
        elif kind == "join":
            rows = outs[a].shape[0] // 2
            mine = outs[a].at[pl.ds(c * rows, rows), :]
            add(k0, mine, mine, outs[a].at[pl.ds((1 - c) * rows, rows), :], sibling)
        elif kind == "everyone":
            for m in range(1, 8):
                peer = (x ^ (m >> 2), y ^ ((m >> 1) & 1), c ^ (m & 1))
                add(k0 + m - 1, outs[a].at[me], outs[a].at[me], outs[a].at[4 * peer[0] + 2 * peer[1] + peer[2]], peer)
        else:
            for j, (px, py) in enumerate(chips):
                sj = 2 * px + py
                if kind == "scatter":
                    add(k0 + j, ins[a].at[sj], outs[a].at[j], outs[a].at[j], (px, py, c))
                    continue
                buf = outs[a]
                rows = buf.shape[1] // 2
                half = lambda shard, h: buf.at[shard, pl.ds(h * rows, rows), :]
                if kind == "ici":
                    add(k0 + j, half(s, c), half(s, c), half(sj, c), (px, py, c))
                else:
                    add(k0 + j, half(sj, c), half(sj, c), half(sj, 1 - c), sibling)
        k0 += CARGO_COPIES[kind]
    return sends, recvs


def _cargo_start(kinds, ins, outs, sems, first_step):
    @pl.when(first_step)
    def _():
        for cp in _cargo_copies(kinds, ins, outs, *sems, want_recvs=False)[0]:
            cp.start()


def _cargo_finish(kinds, ins, outs, sems, last_step):
    @pl.when(last_step)
    def _():
        sends, recvs = _cargo_copies(kinds, ins, outs, *sems)
        for cp in sends:
            cp.wait_send()
        for cp in recvs:
            cp.wait_recv()


def _cargo_call(cargo, n_in, n_out):
    arrays = [a for _, a in cargo]
    shapes = []
    for kind, a in cargo:
        if kind == "scatter":
            shape = (3,) + a.shape[1:]
        elif kind == "swap":
            shape = (a.shape[0], a.shape[1] // 2, a.shape[2])
        else:
            shape = a.shape
        shapes.append(jax.ShapeDtypeStruct(shape, a.dtype))
    aliases = {n_in + i: n_out + i for i, (kind, _) in enumerate(cargo) if kind in CARGO_IN_PLACE}
    n_sems = sum(CARGO_COPIES[kind] for kind, _ in cargo)
    sems = [pltpu.SemaphoreType.DMA((n_sems,))] * 2 if cargo else []
    return arrays, [ANY] * len(cargo), shapes, aliases, sems


def _attn_fwd(q, k, v, d, cargo=()):
    L = q.shape[0]
    group = min(d, RESIDUES_PER_STEP)
    width = group * ATTN_WIDTH
    qb = RESIDUES_PER_STEP // group
    steps = L // (qb * ATTN_BLOCK)
    nc = len(cargo)
    kinds = [kind for kind, _ in cargo]

    def body(*refs):
        q_ref, kp_ref, kc_ref, vp_ref, vc_ref = refs[:5]
        o_ref, lse_ref = refs[5 + nc:7 + nc]
        cargo_refs = (kinds, refs[5:5 + nc], refs[7 + nc:7 + 2 * nc], refs[7 + 2 * nc:])
        r, n = pl.program_id(0), pl.program_id(1)
        if nc:
            _cargo_start(*cargo_refs, (r == 0) & (n == 0))
        first = _first_head_lanes()
        for sub in range(qb):
            rows = slice(sub * ATTN_BLOCK, (sub + 1) * ATTN_BLOCK)
            valid = _band_mask(n if sub == 0 else 1)
            valid2 = jnp.concatenate([valid, valid], axis=0)
            for hp in range(width // LANES):
                sl = slice(hp * LANES, (hp + 1) * LANES)
                if sub == 0:
                    kk = jnp.concatenate([kp_ref[:, sl], kc_ref[rows, sl]], axis=0)
                    vv = jnp.concatenate([vp_ref[:, sl], vc_ref[rows, sl]], axis=0)
                else:
                    keys = slice((sub - 1) * ATTN_BLOCK, (sub + 1) * ATTN_BLOCK)
                    kk, vv = kc_ref[keys, sl], vc_ref[keys, sl]
                s = jnp.where(valid2, _dot_nt(_stack_heads(q_ref[rows, sl], first), kk), NEG_INF)
                m = jnp.max(s, axis=-1, keepdims=True)
                p = jnp.exp(s - m)
                den = jnp.sum(p, axis=-1, keepdims=True)
                o_ref[rows, sl] = _unstack_heads(_dot(p.astype(BF16), vv) / den, first).astype(BF16)
                lse_ref[rows, sl] = _unstack_heads(m + jnp.log(den), first)
        if nc:
            _cargo_finish(*cargo_refs, (r == d // group - 1) & (n == steps - 1))

    cur = pl.BlockSpec((qb * ATTN_BLOCK, width), lambda r, n: (n, r))
    prev = pl.BlockSpec((ATTN_BLOCK, width), lambda r, n: (jnp.maximum(n * qb - 1, 0), r))
    arrays, specs, shapes, aliases, sems = _cargo_call(cargo, 5, 2)
    out = pl.pallas_call(
        body, name=f"attn_fwd_d{d}", grid=(d // group, steps),
        in_specs=[cur, prev, cur, prev, cur] + specs,
        out_specs=[cur, cur] + specs,
        out_shape=[jax.ShapeDtypeStruct((L, d * ATTN_WIDTH), BF16), jax.ShapeDtypeStruct((L, d * ATTN_WIDTH), F32)] + shapes,
        input_output_aliases=aliases, scratch_shapes=sems,
        compiler_params=_params(("arbitrary", "arbitrary")),
    )(q, k, k, v, v, *arrays)
    return out[0], out[1], out[2:]


def _attn_mix(outs, lses):
    S = outs[0].shape[0]
    TM = 512
    n = len(DILATIONS)

    def body(*refs):
        o_refs, l_refs, attn_ref, lse_refs = refs[:n], refs[n:2 * n], refs[2 * n], refs[2 * n + 1:3 * n + 1]
        t_s, c_s = refs[3 * n + 1:-1], refs[-1:]
        os = [_from_residue(o_refs[i], t_s, d) for i, d in enumerate(DILATIONS)]
        ls = [_from_residue(l_refs[i], t_s, d) for i, d in enumerate(DILATIONS)]
        m = jnp.maximum(jnp.maximum(ls[0], ls[1]), ls[2])
        es = [jnp.exp(l - m) for l in ls]
        den = es[0] + es[1] + es[2]
        attn_ref[...] = ((es[0] * os[0] + es[1] * os[1] + es[2] * os[2]) / den).astype(BF16)
        pick = _head_stat_matrix(pick_first_lane=True)
        lse = sum(_dot(p, pick) for p in _bf16_pieces(m + jnp.log(den), 3))
        _put_tokens(c_s, lse)
        for i, d in enumerate(DILATIONS):
            _to_residue(lse, c_s, lse_refs[i], d, F32)

    specs = [_residue_spec(TM, d) for d in DILATIONS]
    out = pl.pallas_call(
        body, name="attn_mix", grid=(S // TM,),
        in_specs=specs * 2, out_specs=[specs[0]] + [_residue_spec(TM, d, STAT_WIDTH) for d in DILATIONS],
        out_shape=[jax.ShapeDtypeStruct((S, ATTN_WIDTH), BF16)]
        + [_residue_shape(S, d, F32, STAT_WIDTH) for d in DILATIONS],
        scratch_shapes=_token_scratch(TM) + _token_scratch(TM, STAT_WIDTH),
        compiler_params=_params(("parallel",)),
    )(*outs, *lses)
    return out[0], out[1:]


def _pool_counts(first_row, rows, w):
    t = first_row + lax.broadcasted_iota(jnp.int32, (rows, 1), 0)
    return jnp.minimum(t + 1, w).astype(F32)


def _trailing_sums(xe, w):
    s, k = xe, 1
    while k < w:
        s = s + pltpu.roll(s, k, 0)
        k *= 2
    return s


def _leading_sums(xe, w):
    rows = xe.shape[0]
    s, k = xe, 1
    while k < w:
        s = s + pltpu.roll(s, rows - k, 0)
        k *= 2
    return s


def _pooled_groups(halo, cur, first_row):
    TM = cur.shape[0]
    xe = jnp.concatenate([halo, cur], axis=0)
    out = []
    for g, w in enumerate(POOL_WINDOWS):
        a = xe[:, g * POOL_GROUP:(g + 1) * POOL_GROUP]
        s = _trailing_sums(a, w)[POOL_HALO:]
        out.append(s / _pool_counts(first_row, TM, w) - a[POOL_HALO:])
    return out


def _pool_fwd(pool_in, pool_w, pool_scale):
    S = pool_in.shape[0]
    TM = 512
    HB = TM // POOL_HALO

    def body(cur_ref, halo_ref, w_ref, sc_ref, y_ref):
        i = pl.program_id(0)
        halo = jnp.where(i > 0, halo_ref[...], 0.0)
        pooled = _pooled_groups(halo, cur_ref[...], i * TM)
        for g in range(len(POOL_WINDOWS)):
            sl = slice(g * POOL_GROUP, (g + 1) * POOL_GROUP)
            y = _dot(pooled[g].astype(BF16), w_ref[g].astype(BF16)) * sc_ref[:, sl]
            y_ref[:, sl] = y.astype(BF16)

    return pl.pallas_call(
        body, name="pool_fwd", grid=(S // TM,),
        in_specs=[pl.BlockSpec((TM, POOL_WIDTH), lambda i: (i, 0)),
                  pl.BlockSpec((POOL_HALO, POOL_WIDTH), lambda i: (jnp.maximum(i * HB - 1, 0), 0)),
                  _const_spec(pool_w.shape), _const_spec((1, POOL_WIDTH))],
        out_specs=pl.BlockSpec((TM, POOL_WIDTH), lambda i: (i, 0)),
        out_shape=jax.ShapeDtypeStruct((S, POOL_WIDTH), BF16),
        compiler_params=_params(("parallel",)),
    )(pool_in, pool_in, pool_w, pool_scale)


def _mix_out_fwd(attn, pool, w_out, x, g_post, g_ffn_pre):
    S, D = x.shape
    TM = 512

    def body(a_ref, p_ref, w_ref, x_ref, gp_ref, gf_ref, mixed_ref, x1_ref, h2_ref, cat_ref):
        ab = a_ref[...]
        cat_ref[:, :ATTN_WIDTH] = ab
        cat_ref[:, ATTN_WIDTH:] = p_ref[...]
        mixed = _dot(ab, w_ref[:ATTN_WIDTH, :]) + _dot(p_ref[...], w_ref[ATTN_WIDTH:, :])
        mixed_ref[...] = mixed.astype(BF16)
        n, _ = _rms_stats(mixed)
        x1 = x_ref[...] + n * gp_ref[...]
        x1_ref[...] = x1
        n2, _ = _rms_stats(x1)
        h2_ref[...] = (n2 * gf_ref[...]).astype(BF16)

    row = lambda w: pl.BlockSpec((TM, w), lambda i: (i, 0))
    return pl.pallas_call(
        body, name="mix_out_fwd", grid=(S // TM,),
        in_specs=[row(ATTN_WIDTH), row(POOL_WIDTH), _const_spec(w_out.shape), row(D),
                  _const_spec((1, D)), _const_spec((1, D))],
        out_specs=[row(D), row(D), row(D), row(D)],
        out_shape=[jax.ShapeDtypeStruct((S, D), BF16), jax.ShapeDtypeStruct((S, D), F32),
                   jax.ShapeDtypeStruct((S, D), BF16), jax.ShapeDtypeStruct((S, D), BF16)],
        compiler_params=_params(("parallel",), VMEM_LIMIT),
    )(attn, pool, w_out, x, g_post, g_ffn_pre)


def _ffn_fwd(h2, x1, target, w_up, w_down, conv_w, conv_b, g_post):
    S, D = x1.shape
    CW = w_up.shape[2]
    FF = 2 * CW
    TM = 256
    piece = 4 * LANES
    pieces = [(lo, min(lo + piece, CW)) for lo in range(0, CW, piece)]

    def body(h2_ref, x1_ref, t_ref, wu_ref, wd_ref, cw_ref, cb_ref, g_ref,
             yv_ref, dy_ref, df_ref, dc_ref, loss_ref, dg_ref, dcb_ref, dcw_ref,
             ue_s, dgate_s, dval_s):
        i = pl.program_id(0)

        @pl.when(i == 0)
        def _():
            loss_ref[...] = jnp.zeros_like(loss_ref)
            dg_ref[...] = jnp.zeros_like(dg_ref)
            dcb_ref[...] = jnp.zeros_like(dcb_ref)
            dcw_ref[...] = jnp.zeros_like(dcw_ref)
            ue_s[0:CONV_HALO, :] = jnp.zeros((CONV_HALO, 2 * FF), F32)

        @pl.when(i > 0)
        def _():
            ue_s[0:CONV_HALO, :] = ue_s[TM:TM + CONV_HALO, :]

        def shifted(cols, k):
            return pltpu.roll(ue_s[:, cols], k, 0)[CONV_HALO:]

        def conv(cols):
            return (cb_ref[:, cols] + cw_ref[2, :, cols] * ue_s[CONV_HALO:, cols]
                    + cw_ref[1, :, cols] * shifted(cols, 1) + cw_ref[0, :, cols] * shifted(cols, 2))

        hb = h2_ref[...]
        f = jnp.zeros((TM, D), F32)
        for j in range(2):
            jc = slice(j * CW, (j + 1) * CW)
            for half in range(2):
                blk = 2 * half + j
                cols = slice(blk * CW, (blk + 1) * CW)
                ue_s[CONV_HALO:, cols] = _dot(hb, wu_ref[blk])
            for lo, hi in pieces:
                pc = slice(j * CW + lo, j * CW + hi)
                gelu, dgelu = _gelu_tanh(conv(pc).astype(BF16))
                val = conv(slice(FF + j * CW + lo, FF + j * CW + hi)).astype(BF16)
                dgate_s[:, pc] = val * dgelu
                dval_s[:, pc] = gelu
                yv_ref[:, pc] = gelu * val
            f = f + _dot(yv_ref[:, jc], wd_ref[jc, :])

        n, r = _rms_stats(f)
        err = x1_ref[...] + n * g_ref[...] - t_ref[...]
        loss_ref[...] += 0.5 * jnp.sum(jnp.mean(err * err, axis=-1, keepdims=True), axis=0, keepdims=True)
        dy = err / D
        dy_ref[...] = dy
        df, dg = _rms_bwd(dy, n, r, g_ref[...])
        dg_ref[...] += dg
        dfb = df.astype(BF16)
        df_ref[...] = dfb

        for j in range(2):
            jc = slice(j * CW, (j + 1) * CW)
            dyv = _dot_nt(dfb, wd_ref[jc, :])
            for lo, hi in pieces:
                pc = slice(j * CW + lo, j * CW + hi)
                for half, scale_s in ((0, dgate_s), (1, dval_s)):
                    cols = slice(half * FF + j * CW + lo, half * FF + j * CW + hi)
                    dcv = dyv[:, lo:hi] * scale_s[:, pc].astype(F32)
                    dc_ref[:, cols] = dcv.astype(BF16)
                    dcb_ref[:, cols] += jnp.sum(dcv, axis=0, keepdims=True)
                    dcw_ref[2, :, cols] += jnp.sum(dcv * ue_s[CONV_HALO:, cols], axis=0, keepdims=True)
                    dcw_ref[1, :, cols] += jnp.sum(dcv * shifted(cols, 1), axis=0, keepdims=True)
                    dcw_ref[0, :, cols] += jnp.sum(dcv * shifted(cols, 2), axis=0, keepdims=True)

    row = lambda w: pl.BlockSpec((TM, w), lambda i: (i, 0))
    acc = lambda shape: pl.BlockSpec(shape, lambda i: (0,) * len(shape))
    return pl.pallas_call(
        body, name="ffn_fwd", grid=(S // TM,),
        in_specs=[row(D), row(D), row(D), _const_spec(w_up.shape), _const_spec(w_down.shape),
                  _const_spec(conv_w.shape), _const_spec((1, 2 * FF)), _const_spec((1, D))],
        out_specs=[row(FF), row(D), row(D), row(2 * FF),
                   acc((1, 1)), acc((1, D)), acc((1, 2 * FF)), acc((CONV_WIDTH, 1, 2 * FF))],
        out_shape=[jax.ShapeDtypeStruct((S, FF), BF16),
                   jax.ShapeDtypeStruct((S, D), F32), jax.ShapeDtypeStruct((S, D), BF16),
                   jax.ShapeDtypeStruct((S, 2 * FF), BF16),
                   jax.ShapeDtypeStruct((1, 1), F32), jax.ShapeDtypeStruct((1, D), F32),
                   jax.ShapeDtypeStruct((1, 2 * FF), F32), jax.ShapeDtypeStruct((CONV_WIDTH, 1, 2 * FF), F32)],
        scratch_shapes=[pltpu.VMEM((TM + CONV_HALO, 2 * FF), F32), pltpu.VMEM((TM, FF), BF16),
                        pltpu.VMEM((TM, FF), BF16)],
        compiler_params=_params(("arbitrary",), VMEM_LIMIT),
    )(h2, x1, target, w_up, w_down, conv_w, conv_b, g_post)


def _ffn_bwd(dc, conv_w, w_up, x1, g_ffn_pre, dy):
    S, D = x1.shape
    CW = w_up.shape[2]
    F2 = 4 * CW
    TM = 256
    HB = TM // CONV_HALO
    last = S // CONV_HALO - 1
    n_tiles = S // TM

    def body(dc_ref, halo_ref, cw_ref, wu_ref, x1_ref, g_ref, dy_ref, du_ref, dx1_ref, dg_ref):
        i = pl.program_id(0)

        @pl.when(i == 0)
        def _():
            dg_ref[...] = jnp.zeros_like(dg_ref)

        keep = i < n_tiles - 1
        dh2 = jnp.zeros((TM, D), F32)
        for blk in range(N_SHARD):
            cols = slice(blk * CW, (blk + 1) * CW)
            halo = jnp.where(keep, halo_ref[:, cols].astype(F32), 0.0)
            dce = jnp.concatenate([dc_ref[:, cols].astype(F32), halo], axis=0)
            rows = TM + CONV_HALO
            du = (cw_ref[2, :, cols] * dce[:TM]
                  + cw_ref[1, :, cols] * pltpu.roll(dce, rows - 1, 0)[:TM]
                  + cw_ref[0, :, cols] * pltpu.roll(dce, rows - 2, 0)[:TM])
            dub = du.astype(BF16)
            du_ref[:, cols] = dub
            dh2 = dh2 + _dot_nt(dub, wu_ref[blk])
        n2, r2 = _rms_stats(x1_ref[...])
        dx, dg = _rms_bwd(dh2, n2, r2, g_ref[...])
        dg_ref[...] += dg
        dx1_ref[...] = (dy_ref[...] + dx).astype(BF16)

    row = lambda w: pl.BlockSpec((TM, w), lambda i: (i, 0))
    return pl.pallas_call(
        body, name="ffn_bwd", grid=(S // TM,),
        in_specs=[row(F2), pl.BlockSpec((CONV_HALO, F2), lambda i: (jnp.minimum((i + 1) * HB, last), 0)),
                  _const_spec(conv_w.shape), _const_spec(w_up.shape), row(D), _const_spec((1, D)), row(D)],
        out_specs=[row(F2), row(D), pl.BlockSpec((1, D), lambda i: (0, 0))],
        out_shape=[jax.ShapeDtypeStruct((S, F2), BF16), jax.ShapeDtypeStruct((S, D), BF16),
                   jax.ShapeDtypeStruct((1, D), F32)],
        compiler_params=_params(("arbitrary",), VMEM_LIMIT),
    )(dc, dc, conv_w, w_up, x1, g_ffn_pre, dy)


def _matmul_tn(a, b, n_blocks, name):
    S, M = a.shape
    N = b.shape[1]
    tn = N // n_blocks
    tm = M if M <= 1024 else M // 2
    tk = 2048
    nk = S // tk

    def body(a_ref, b_ref, o_ref):
        @pl.when(pl.program_id(2) == 0)
        def _():
            o_ref[...] = jnp.zeros_like(o_ref)
        o_ref[0] += _dot_tn(a_ref[...], b_ref[...])

    return pl.pallas_call(
        body, name=name, grid=(M // tm, n_blocks, nk),
        in_specs=[pl.BlockSpec((tk, tm), lambda i, j, k: (k, i)), pl.BlockSpec((tk, tn), lambda i, j, k: (k, j))],
        out_specs=pl.BlockSpec((1, tm, tn), lambda i, j, k: (j, i, 0)),
        out_shape=jax.ShapeDtypeStruct((n_blocks, M, tn), F32),
        compiler_params=_params(("parallel", "parallel", "arbitrary"), VMEM_LIMIT),
    )(a, b)


def _mix_out_bwd(dx1, mixed, g_post, w_out, attn, cargo=()):
    S, D = dx1.shape
    TM = 512
    nd = len(DILATIONS)
    nc = len(cargo)
    kinds = [kind for kind, _ in cargo]
    n_chunks = ATTN_WIDTH // LANES

    def body(*refs):
        dx_ref, m_ref, g_ref, w_ref, a_ref = refs[:5]
        dm_ref, dp_ref, dg_ref = refs[5 + nc:8 + nc]
        da_refs, dl_refs = refs[8 + nc:8 + nc + nd], refs[8 + nc + nd:8 + nc + 2 * nd]
        n_out = 8 + nc + 2 * nd
        t_s = refs[n_out + nc:n_out + nc + n_chunks]
        c_s = refs[n_out + nc + n_chunks:n_out + nc + n_chunks + 1]
        cargo_refs = (kinds, refs[5:5 + nc], refs[n_out:n_out + nc], refs[n_out + nc + n_chunks + 1:])
        if nc:
            _cargo_start(*cargo_refs, pl.program_id(0) == 0)

        @pl.when(pl.program_id(0) == 0)
        def _():
            dg_ref[...] = jnp.zeros_like(dg_ref)

        n, r = _rms_stats(m_ref[...].astype(F32))
        dm, dg = _rms_bwd(dx_ref[...].astype(F32), n, r, g_ref[...])
        dg_ref[...] += dg
        dmb = dm.astype(BF16)
        dm_ref[...] = dmb
        da = _dot_nt(dmb, w_ref[:ATTN_WIDTH, :])
        _put_tokens(t_s, da)
        for i, d in enumerate(DILATIONS):
            _to_residue(da, t_s, da_refs[i], d, BF16)
        dp_ref[...] = _dot_nt(dmb, w_ref[ATTN_WIDTH:, :]).astype(BF16)
        gather = _head_stat_matrix(pick_first_lane=False)
        delta = sum(_dot(p, gather) for p in _bf16_pieces(da * a_ref[...].astype(F32), 2))
        _put_tokens(c_s, delta)
        for i, d in enumerate(DILATIONS):
            _to_residue(delta, c_s, dl_refs[i], d, F32)
        if nc:
            _cargo_finish(*cargo_refs, pl.program_id(0) == S // TM - 1)

    row = lambda w: pl.BlockSpec((TM, w), lambda i: (i, 0))
    specs = [_residue_spec(TM, d) for d in DILATIONS]
    arrays, cargo_specs, shapes, aliases, sems = _cargo_call(cargo, 5, 3 + 2 * nd)
    out = pl.pallas_call(
        body, name="mix_out_bwd", grid=(S // TM,),
        in_specs=[row(D), row(D), _const_spec((1, D)), _const_spec(w_out.shape), row(ATTN_WIDTH)] + cargo_specs,
        out_specs=[row(D), row(POOL_WIDTH), pl.BlockSpec((1, D), lambda i: (0, 0))] + specs
        + [_residue_spec(TM, d, STAT_WIDTH) for d in DILATIONS] + cargo_specs,
        out_shape=[jax.ShapeDtypeStruct((S, D), BF16), jax.ShapeDtypeStruct((S, POOL_WIDTH), BF16),
                   jax.ShapeDtypeStruct((1, D), F32)]
        + [_residue_shape(S, d, BF16) for d in DILATIONS]
        + [_residue_shape(S, d, F32, STAT_WIDTH) for d in DILATIONS] + shapes,
        input_output_aliases=aliases,
        scratch_shapes=_token_scratch(TM) + _token_scratch(TM, STAT_WIDTH) + sems,
        compiler_params=_params(("arbitrary",), VMEM_LIMIT),
    )(dx1, mixed, g_post, w_out, attn, *arrays)
    return out[0], out[1], out[2], out[3:3 + nd], out[3 + nd:3 + 2 * nd], out[3 + 2 * nd:]


def _pool_bwd(pool_in, d_pool, pool_w, pool_scale):
    S = pool_in.shape[0]
    TM = 512
    HB = TM // POOL_HALO
    last = S // POOL_HALO - 1
    G = len(POOL_WINDOWS)

    def body(cur_ref, halo_ref, dcur_ref, dnext_ref, w_ref, sc_ref, dxin_ref, dw_ref, dsc_ref):
        i = pl.program_id(0)

        @pl.when(i == 0)
        def _():
            dw_ref[...] = jnp.zeros_like(dw_ref)
            dsc_ref[...] = jnp.zeros_like(dsc_ref)

        halo = jnp.where(i > 0, halo_ref[...], 0.0)
        pooled = _pooled_groups(halo, cur_ref[...], i * TM)
        dnext = jnp.where(i < S // TM - 1, dnext_ref[...].astype(F32), 0.0)
        dye = jnp.concatenate([dcur_ref[...].astype(F32), dnext], axis=0)
        for g, w in enumerate(POOL_WINDOWS):
            sl = slice(g * POOL_GROUP, (g + 1) * POOL_GROUP)
            wg = w_ref[g].astype(BF16)
            pb = pooled[g].astype(BF16)
            dsc_ref[:, sl] += jnp.sum(dye[:TM, sl] * _dot(pb, wg), axis=0, keepdims=True)
            dpre = (dye[:, sl] * sc_ref[:, sl]).astype(BF16)
            dw_ref[g] += _dot_tn(pb, dpre[:TM])
            dpooled = _dot_nt(dpre, wg)
            z = dpooled / _pool_counts(i * TM, TM + POOL_HALO, w)
            dxin_ref[:, sl] = (_leading_sums(z, w)[:TM] - dpooled[:TM]).astype(BF16)

    row = pl.BlockSpec((TM, POOL_WIDTH), lambda i: (i, 0))
    return pl.pallas_call(
        body, name="pool_bwd", grid=(S // TM,),
        in_specs=[row, pl.BlockSpec((POOL_HALO, POOL_WIDTH), lambda i: (jnp.maximum(i * HB - 1, 0), 0)),
                  row, pl.BlockSpec((POOL_HALO, POOL_WIDTH), lambda i: (jnp.minimum((i + 1) * HB, last), 0)),
                  _const_spec(pool_w.shape), _const_spec((1, POOL_WIDTH))],
        out_specs=[row, pl.BlockSpec((G, POOL_GROUP, POOL_GROUP), lambda i: (0, 0, 0)),
                   pl.BlockSpec((1, POOL_WIDTH), lambda i: (0, 0))],
        out_shape=[jax.ShapeDtypeStruct((S, POOL_WIDTH), BF16), jax.ShapeDtypeStruct((G, POOL_GROUP, POOL_GROUP), F32),
                   jax.ShapeDtypeStruct((1, POOL_WIDTH), F32)],
        compiler_params=_params(("arbitrary",)),
    )(pool_in, pool_in, d_pool, d_pool, pool_w, pool_scale)


def _attn_bwd(q, k, v, d_attn, lse, delta, d, cargo=()):
    L = q.shape[0]
    nb = L // ATTN_BLOCK
    group = min(d, RESIDUES_PER_STEP)
    width = group * ATTN_WIDTH
    nc = len(cargo)
    kinds = [kind for kind, _ in cargo]

    def body(*refs):
        q_ref, kp_ref, kc_ref, vp_ref, vc_ref, do_ref, lse_ref, dl_ref = refs[:8]
        dq_ref, dk_ref, dv_ref = refs[8 + nc:11 + nc]
        ck_s, cv_s = refs[11 + 2 * nc:13 + 2 * nc]
        cargo_refs = (kinds, refs[8:8 + nc], refs[11 + nc:11 + 2 * nc], refs[13 + 2 * nc:])
        r, n = pl.program_id(0), pl.program_id(1)
        if nc:
            _cargo_start(*cargo_refs, (r == 0) & (n == 0))

        @pl.when(n == 0)
        def _():
            ck_s[...] = jnp.zeros_like(ck_s)
            cv_s[...] = jnp.zeros_like(cv_s)

        @pl.when(n < nb)
        def _():
            valid = _band_mask(n)
            valid2 = jnp.concatenate([valid, valid], axis=0)
            first = _first_head_lanes()

            def stacked_column(ref, hp):
                lane = 2 * hp * STAT_LANES
                return jnp.concatenate([ref[:, lane:lane + 1], ref[:, lane + STAT_LANES:lane + STAT_LANES + 1]], axis=0)

            for hp in range(width // LANES):
                sl = slice(hp * LANES, (hp + 1) * LANES)
                qq = _stack_heads(q_ref[:, sl], first)
                dd = _stack_heads(do_ref[:, sl], first)
                kk = jnp.concatenate([kp_ref[:, sl], kc_ref[:, sl]], axis=0)
                vv = jnp.concatenate([vp_ref[:, sl], vc_ref[:, sl]], axis=0)
                s = _dot_nt(qq, kk)
                p = jnp.where(valid2, jnp.exp(s - stacked_column(lse_ref, hp)), 0.0)
                dp = _dot_nt(dd, vv)
                ds = (p * (dp - stacked_column(dl_ref, hp))).astype(BF16)
                dq_ref[:, sl] = (_unstack_heads(_dot(ds, kk), first) * ATTN_SCALE).astype(BF16)
                dk = _dot_tn(ds, qq)
                dv = _dot_tn(p.astype(BF16), dd)
                dk_ref[:, sl] = (ck_s[:, sl] + dk[:ATTN_BLOCK]).astype(BF16)
                dv_ref[:, sl] = (cv_s[:, sl] + dv[:ATTN_BLOCK]).astype(BF16)
                ck_s[:, sl] = dk[ATTN_BLOCK:]
                cv_s[:, sl] = dv[ATTN_BLOCK:]

        @pl.when(n == nb)
        def _():
            dk_ref[...] = ck_s[...].astype(BF16)
            dv_ref[...] = cv_s[...].astype(BF16)

        if nc:
            _cargo_finish(*cargo_refs, (r == d // group - 1) & (n == nb))

    blk = (ATTN_BLOCK, width)
    cur = pl.BlockSpec(blk, lambda r, n: (jnp.minimum(n, nb - 1), r))
    stat = pl.BlockSpec((ATTN_BLOCK, group * STAT_WIDTH), lambda r, n: (jnp.minimum(n, nb - 1), r))
    prev = pl.BlockSpec(blk, lambda r, n: (jnp.maximum(jnp.minimum(n, nb - 1) - 1, 0), r))
    done = pl.BlockSpec(blk, lambda r, n: (jnp.maximum(n - 1, 0), r))
    arrays, specs, shapes, aliases, sems = _cargo_call(cargo, 8, 3)
    out = pl.pallas_call(
        body, name=f"attn_bwd_d{d}", grid=(d // group, nb + 1),
        in_specs=[cur, prev, cur, prev, cur, cur, stat, stat] + specs, out_specs=[cur, done, done] + specs,
        out_shape=[jax.ShapeDtypeStruct((L, d * ATTN_WIDTH), BF16)] * 3 + shapes,
        input_output_aliases=aliases,
        scratch_shapes=[pltpu.VMEM(blk, F32), pltpu.VMEM(blk, F32)] + sems,
        compiler_params=_params(("arbitrary", "arbitrary")),
    )(q, k, k, v, v, d_attn, lse, delta, *arrays)
    return out[:3], out[3:]


def _attn_bwd_consecutive(q, k, v, d_attn, lse, delta, cargo=()):
    L = q.shape[0]
    qb = CONSECUTIVE_BLOCKS
    steps = L // (qb * ATTN_BLOCK)
    nc = len(cargo)
    kinds = [kind for kind, _ in cargo]

    def body(*refs):
        q_ref, kp_ref, kc_ref, vp_ref, vc_ref, do_ref, lse_ref, dl_ref = refs[:8]
        dq_ref, dk_ref, dv_ref, ek_ref, ev_ref = refs[8 + nc:13 + nc]
        cargo_refs = (kinds, refs[8:8 + nc], refs[13 + nc:13 + 2 * nc], refs[13 + 2 * nc:])
        n = pl.program_id(0)
        if nc:
            _cargo_start(*cargo_refs, n == 0)
        first = _first_head_lanes()
        for hp in range(ATTN_WIDTH // LANES):
            sl = slice(hp * LANES, (hp + 1) * LANES)
            for sub in range(qb):
                rows = slice(sub * ATTN_BLOCK, (sub + 1) * ATTN_BLOCK)
                valid = _band_mask(n if sub == 0 else 1)
                valid2 = jnp.concatenate([valid, valid], axis=0)
                if sub == 0:
                    kk = jnp.concatenate([kp_ref[:, sl], kc_ref[rows, sl]], axis=0)
                    vv = jnp.concatenate([vp_ref[:, sl], vc_ref[rows, sl]], axis=0)
                else:
                    keys = slice((sub - 1) * ATTN_BLOCK, (sub + 1) * ATTN_BLOCK)
                    kk, vv = kc_ref[keys, sl], vc_ref[keys, sl]
                qq = _stack_heads(q_ref[rows, sl], first)
                dd = _stack_heads(do_ref[rows, sl], first)
                lane = 2 * hp * STAT_LANES
                column = lambda ref: jnp.concatenate(
                    [ref[rows, lane:lane + 1], ref[rows, lane + STAT_LANES:lane + STAT_LANES + 1]], axis=0)
                p = jnp.where(valid2, jnp.exp(_dot_nt(qq, kk) - column(lse_ref)), 0.0)
                ds = (p * (_dot_nt(dd, vv) - column(dl_ref))).astype(BF16)
                dq_ref[rows, sl] = (_unstack_heads(_dot(ds, kk), first) * ATTN_SCALE).astype(BF16)
                dk = _dot_tn(ds, qq)
                dv = _dot_tn(p.astype(BF16), dd)
                if sub == 0:
                    ek_ref[:, sl] = dk[:ATTN_BLOCK].astype(BF16)
                    ev_ref[:, sl] = dv[:ATTN_BLOCK].astype(BF16)
                else:
                    before = slice((sub - 1) * ATTN_BLOCK, sub * ATTN_BLOCK)
                    dk_ref[before, sl] = (carry_k + dk[:ATTN_BLOCK]).astype(BF16)
                    dv_ref[before, sl] = (carry_v + dv[:ATTN_BLOCK]).astype(BF16)
                carry_k, carry_v = dk[ATTN_BLOCK:], dv[ATTN_BLOCK:]
            dk_ref[rows, sl] = carry_k.astype(BF16)
            dv_ref[rows, sl] = carry_v.astype(BF16)
        if nc:
            _cargo_finish(*cargo_refs, n == steps - 1)

    cur = pl.BlockSpec((qb * ATTN_BLOCK, ATTN_WIDTH), lambda n: (n, 0))
    prev = pl.BlockSpec((ATTN_BLOCK, ATTN_WIDTH), lambda n: (jnp.maximum(n * qb - 1, 0), 0))
    edge = pl.BlockSpec((ATTN_BLOCK, ATTN_WIDTH), lambda n: (n, 0))
    stat = pl.BlockSpec((qb * ATTN_BLOCK, STAT_WIDTH), lambda n: (n, 0))
    arrays, specs, shapes, aliases, sems = _cargo_call(cargo, 8, 5)
    out = pl.pallas_call(
        body, name="attn_bwd_d1", grid=(steps,),
        in_specs=[cur, prev, cur, prev, cur, cur, stat, stat] + specs, out_specs=[cur, cur, cur, edge, edge] + specs,
        out_shape=[jax.ShapeDtypeStruct((L, ATTN_WIDTH), BF16)] * 3
        + [jax.ShapeDtypeStruct((steps * ATTN_BLOCK, ATTN_WIDTH), BF16)] * 2 + shapes,
        input_output_aliases=aliases, scratch_shapes=sems,
        compiler_params=_params(("arbitrary",)),
    )(q, k, k, v, v, d_attn, lse, delta, *arrays)
    return out[:3], out[3:5], out[5:]


def _mix_in_bwd(dqkv, edges, d_pool_in, w_in, x, g_pre, dx1):
    S, D = x.shape
    TM = CONSECUTIVE_BLOCKS * ATTN_BLOCK
    nd = len(DILATIONS)
    n_tiles = S // TM

    def body(*refs):
        g_refs = refs[:3 * nd]
        e_refs = (None,) + refs[3 * nd:3 * nd + 2]
        dpi_ref, w_ref, x_ref, g_ref, dx1_ref, dproj_ref, gx_ref, dg_ref = refs[3 * nd + 2:3 * nd + 10]
        t_s = refs[3 * nd + 10:]

        @pl.when(pl.program_id(0) == 0)
        def _():
            dg_ref[...] = jnp.zeros_like(dg_ref)

        dh = jnp.zeros((TM, D), F32)
        for a in range(4):
            if a < 3:
                tot = g_refs[a][...].astype(F32)
                if a > 0:
                    late = jnp.where(pl.program_id(0) < n_tiles - 1, e_refs[a][...].astype(F32), 0.0)
                    tot = jnp.concatenate([tot[:TM - ATTN_BLOCK], tot[TM - ATTN_BLOCK:] + late], axis=0)
                for i, d in enumerate(DILATIONS[1:]):
                    tot = tot + _from_residue(g_refs[3 * (i + 1) + a], t_s, d)
                db = tot.astype(BF16)
            else:
                db = dpi_ref[...]
            dproj_ref[:, a * ATTN_WIDTH:(a + 1) * ATTN_WIDTH] = db
            dh = dh + _dot_nt(db, w_ref[a])
        n, r = _rms_stats(x_ref[...])
        dx, dg = _rms_bwd(dh, n, r, g_ref[...])
        dg_ref[...] += dg
        gx_ref[...] = dx1_ref[...].astype(F32) + dx

    row = lambda w: pl.BlockSpec((TM, w), lambda i: (i, 0))
    edge = pl.BlockSpec((ATTN_BLOCK, ATTN_WIDTH), lambda i: (jnp.minimum(i + 1, n_tiles - 1), 0))
    return pl.pallas_call(
        body, name="mix_in_bwd", grid=(S // TM,),
        in_specs=[_residue_spec(TM, d) for d in DILATIONS for _ in range(3)] + [edge, edge]
        + [row(POOL_WIDTH), _const_spec(w_in.shape), row(D), _const_spec((1, D)), row(D)],
        out_specs=[row(4 * ATTN_WIDTH), row(D), pl.BlockSpec((1, D), lambda i: (0, 0))],
        out_shape=[jax.ShapeDtypeStruct((S, 4 * ATTN_WIDTH), BF16), jax.ShapeDtypeStruct((S, D), F32),
                   jax.ShapeDtypeStruct((1, D), F32)],
        scratch_shapes=_token_scratch(TM),
        compiler_params=_params(("arbitrary",), VMEM_LIMIT),
    )(*[g for gs in dqkv for g in gs], *edges, d_pool_in, w_in, x, g_pre, dx1)


SMALL_EARLY = ("pool_w", "pool_scale", "g_mix_post", "g_ffn_pre", "conv_b", "g_ffn_post", "conv_w")
SMALL_LATE = ("g_mix_pre",)


def _pack_small(grads, names):
    parts = []
    for n in names:
        g = grads[n]
        if n == "conv_w":
            g = g.reshape(CONV_WIDTH, N_SHARD, -1).transpose(1, 0, 2)
        parts.append(g.reshape(-1, LANES))
    return jnp.concatenate(parts, axis=0) if len(parts) > 1 else parts[0]


def _unpack_small(packed, names, like, shard):
    out, row = {}, 0
    for n in names:
        size = like[n].size * (N_SHARD if n == "conv_w" else 1)
        g = packed[row:row + size // LANES]
        row += size // LANES
        if n == "conv_w":
            g = lax.dynamic_slice_in_dim(g.reshape((N_SHARD,) + like[n].shape), shard, 1, axis=0)[0]
        out[n] = g.reshape(like[n].shape)
    return out


def _local_step(x, target, g_mix_pre, w_in, pool_w, pool_scale, w_out, g_mix_post, g_ffn_pre,
                w_up, conv_w, conv_b, w_down, g_ffn_post, mesh_pos=None):
    on_mesh = mesh_pos is not None
    D = x.shape[1]
    CW = w_up.shape[2]
    qkv, pool_in, h1, got = _mix_in_fwd(x, g_mix_pre, w_in, [("ici", w_up)] if on_mesh else ())
    w_up = got[0] if on_mesh else w_up
    o1, l1, got = _attn_fwd(*qkv[0], 1, [("d2d", w_up), ("ici", w_out), ("ici", w_down)] if on_mesh else ())
    w_up, w_out, w_down = got if on_mesh else (w_up, w_out, w_down)
    o4, l4, got = _attn_fwd(*qkv[1], 4, [("d2d", w_out), ("d2d", w_down)] if on_mesh else ())
    w_out, w_down = got if on_mesh else (w_out, w_down)
    o16, l16, _ = _attn_fwd(*qkv[2], 16)
    w_out = w_out.reshape(D, D)
    w_down = w_down.reshape(2 * CW, D)
    attn, lse = _attn_mix((o1, o4, o16), (l1, l4, l16))
    pool = _pool_fwd(pool_in, pool_w, pool_scale)
    mixed, x1, h2, cat = _mix_out_fwd(attn, pool, w_out, x, g_mix_post, g_ffn_pre)

    yv, dy, df, dc, loss, d_g_ffn_post, d_conv_b, d_conv_w = _ffn_fwd(
        h2, x1, target, w_up, w_down, conv_w, conv_b, g_ffn_post)
    du, dx1, d_g_ffn_pre = _ffn_bwd(dc, conv_w, w_up, x1, g_ffn_pre, dy)
    d_w_up = _matmul_tn(h2, du, N_SHARD, "grad_w_up")
    d_w_down = _matmul_tn(yv, df, 1, "grad_w_down")[0].reshape(N_SHARD, CW // 2, D)
    swap = [("swap", d_w_up), ("swap", d_w_down)] if on_mesh else ()
    d_mixed, d_pool, d_g_mix_post, d_attn, delta, from_sibling = _mix_out_bwd(dx1, mixed, g_mix_post, w_out, attn, swap)
    d_w_out = _matmul_tn(cat, d_mixed, 1, "grad_w_out")[0].reshape(N_SHARD, D // N_SHARD, D)
    d_pool_in, d_pool_w, d_pool_scale = _pool_bwd(pool_in, d_pool, pool_w, pool_scale)
    grads = dict(pool_w=d_pool_w, pool_scale=d_pool_scale, w_out=d_w_out, g_mix_post=d_g_mix_post,
                 g_ffn_pre=d_g_ffn_pre, w_up=d_w_up, conv_w=d_conv_w, conv_b=d_conv_b, w_down=d_w_down,
                 g_ffn_post=d_g_ffn_post)
    cargo = [(), (), ()]
    if on_mesh:
        c_arr, device, shard_arr = mesh_pos
        up_f32, up_bf16 = _pair_sum(d_w_up, from_sibling[0], c_arr, "pair_sum_w_up")
        down_f32, down_bf16 = _pair_sum(d_w_down, from_sibling[1], c_arr, "pair_sum_w_down")
        early = _pack_small(grads, SMALL_EARLY)
        early_slots = lax.dynamic_update_index_in_dim(jnp.zeros((8,) + early.shape, F32), early, device, 0)
        cargo = [[("scatter", down_bf16)], [("scatter", up_bf16)], [("everyone", early_slots)]]

    dqkv1, edges, landed1 = _attn_bwd_consecutive(*qkv[0], d_attn[0], lse[0], delta[0], cargo[0])
    dqkv4, landed4 = _attn_bwd(*qkv[1], d_attn[1], lse[1], delta[1], 4, cargo[1])
    if on_mesh:
        halves = [_shard_sum(up_f32, landed4[0], shard_arr, c_arr, "shard_sum_w_up"),
                  _shard_sum(down_f32, landed1[0], shard_arr, c_arr, "shard_sum_w_down")]
        cargo[2] = cargo[2] + [("join", h) for h in halves]
    dqkv16, landed16 = _attn_bwd(*qkv[2], d_attn[2], lse[2], delta[2], 16, cargo[2])
    if on_mesh:
        grads.update(small_early=landed16[0], w_up=landed16[1], w_down=landed16[2])
    d_proj, grad_x, grads["g_mix_pre"] = _mix_in_bwd((dqkv1, dqkv4, dqkv16), edges, d_pool_in, w_in, x, g_mix_pre, dx1)
    grads["w_in"] = _matmul_tn(h1, d_proj, N_SHARD, "grad_w_in")
    return loss, grad_x, grads


ANY = pl.BlockSpec(memory_space=pl.ANY)


def _position():
    x, y, c = lax.axis_index("x"), lax.axis_index("y"), lax.axis_index("c")
    chips = [(1 - x, y), (x, 1 - y), (1 - x, 1 - y)]
    return x, y, c, chips


def _remote(src, dst, send_sem, recv_sem, to):
    return pltpu.make_async_remote_copy(src_ref=src, dst_ref=dst, send_sem=send_sem, recv_sem=recv_sem,
                                        device_id=to, device_id_type=MESH)


def _cast_bf16(w, shard_arr, name):
    R, C = w.shape
    tr = R // 2

    def body(s_ref, w_ref, o_ref):
        o_ref[0] = w_ref[...].astype(BF16)

    return pl.pallas_call(
        body, name=name,
        grid_spec=pltpu.PrefetchScalarGridSpec(
            num_scalar_prefetch=1, grid=(2,),
            in_specs=[pl.BlockSpec((tr, C), lambda i, s_ref: (i, 0))],
            out_specs=pl.BlockSpec((1, tr, C), lambda i, s_ref: (s_ref[0], i, 0))),
        out_shape=jax.ShapeDtypeStruct((N_SHARD, R, C), BF16),
        compiler_params=_params(("parallel",)))(shard_arr, w)


def _gather_weights(bufs):
    n = len(bufs) - 1

    def body(*refs):
        outs, cw_out = refs[n + 1:2 * n + 1], refs[2 * n + 1]
        ici_send, ici_recv, d2d_send, d2d_recv = refs[2 * n + 2:]
        x, y, c, chips = _position()
        s = 2 * x + y
        sibling = (x, y, 1 - c)

        def half(a, shard, h):
            rows = outs[a].shape[1] // 2
            return outs[a].at[shard, pl.ds(h * rows, rows), :]

        sends = []
        for a in range(n):
            for j, (px, py) in enumerate(chips):
                sends.append(_remote(half(a, s, c), half(a, s, c),
                                     ici_send.at[3 * a + j], ici_recv.at[3 * a + j], (px, py, c)))
        for j, (px, py) in enumerate(chips):
            sends.append(_remote(cw_out.at[s], cw_out.at[s], ici_send.at[3 * n + j], ici_recv.at[3 * n + j], (px, py, c)))
        for cp in sends:
            cp.start()
        passed = []
        for a in range(n):
            for j, (px, py) in enumerate(chips):
                sj = 2 * px + py
                got = half(a, sj, c)
                _remote(got, got, ici_send.at[3 * a + j], ici_recv.at[3 * a + j], (px, py, c)).wait_recv()
                fwd = _remote(got, got, d2d_send.at[3 * a + j], d2d_recv.at[3 * a + j], sibling)
                fwd.start()
                passed.append(fwd)
        for j, (px, py) in enumerate(chips):
            got = cw_out.at[2 * px + py]
            _remote(got, got, ici_send.at[3 * n + j], ici_recv.at[3 * n + j], (px, py, c)).wait_recv()
        for a in range(n):
            for j, (px, py) in enumerate(chips):
                got = half(a, 2 * px + py, 1 - c)
                _remote(got, got, d2d_send.at[3 * a + j], d2d_recv.at[3 * a + j], sibling).wait_recv()
        for cp in sends + passed:
            cp.wait_send()

    return pl.pallas_call(
        body, name="gather_weights",
        in_specs=[ANY] * (n + 1), out_specs=[ANY] * (n + 1),
        out_shape=[jax.ShapeDtypeStruct(b.shape, b.dtype) for b in bufs],
        input_output_aliases={i: i for i in range(n + 1)},
        scratch_shapes=[pltpu.SemaphoreType.DMA((3 * n + 3,)), pltpu.SemaphoreType.DMA((3 * n + 3,)),
                        pltpu.SemaphoreType.DMA((3 * n,)), pltpu.SemaphoreType.DMA((3 * n,))],
        compiler_params=pltpu.CompilerParams(has_side_effects=True),
    )(*bufs)


def _swap_halves(grads, tag):
    n = len(grads)

    def body(*refs):
        ins, outs, send_sem, recv_sem = refs[:n], refs[n:2 * n], refs[2 * n], refs[2 * n + 1]
        x, y, c, _ = _position()
        copies = []
        for a in range(n):
            rows = ins[a].shape[1] // 2
            copies.append(_remote(ins[a].at[:, pl.ds((1 - c) * rows, rows), :], outs[a],
                                  send_sem.at[a], recv_sem.at[a], (x, y, 1 - c)))
        for cp in copies:
            cp.start()
        for cp in copies:
            cp.wait()

    return pl.pallas_call(
        body, name="swap_grad_halves_" + tag,
        in_specs=[ANY] * n, out_specs=[ANY] * n,
        out_shape=[jax.ShapeDtypeStruct((g.shape[0], g.shape[1] // 2, g.shape[2]), F32) for g in grads],
        scratch_shapes=[pltpu.SemaphoreType.DMA((n,)), pltpu.SemaphoreType.DMA((n,))],
        compiler_params=pltpu.CompilerParams(has_side_effects=True),
    )(*grads)


def _pair_sum(g, got, c_arr, name):
    n_sh, R, C = g.shape
    rows = R // 2

    def body(c_ref, g_ref, r_ref, f_ref, b_ref):
        t = g_ref[...] + r_ref[...]
        f_ref[...] = t
        b_ref[...] = t.astype(BF16)

    blk = pl.BlockSpec((1, rows, C), lambda i, c_ref: (i, 0, 0))
    return pl.pallas_call(
        body, name=name,
        grid_spec=pltpu.PrefetchScalarGridSpec(
            num_scalar_prefetch=1, grid=(n_sh,),
            in_specs=[pl.BlockSpec((1, rows, C), lambda i, c_ref: (i, c_ref[0], 0)), blk],
            out_specs=[blk, blk]),
        out_shape=[jax.ShapeDtypeStruct((n_sh, rows, C), F32), jax.ShapeDtypeStruct((n_sh, rows, C), BF16)],
        compiler_params=_params(("parallel",)),
    )(c_arr, g, got)


def _shard_sum(sums_f32, recv, shard_arr, c_arr, name):
    _, rows, C = sums_f32.shape

    def body(s_ref, c_ref, o_ref, r_ref, t_ref):
        t_ref[...] = ((o_ref[0] + r_ref[0].astype(F32)) + r_ref[1].astype(F32)) + r_ref[2].astype(F32)

    return pl.pallas_call(
        body, name=name,
        grid_spec=pltpu.PrefetchScalarGridSpec(
            num_scalar_prefetch=2, grid=(1,),
            in_specs=[pl.BlockSpec((1, rows, C), lambda i, s_ref, c_ref: (s_ref[0], 0, 0)),
                      pl.BlockSpec((3, rows, C), lambda i, s_ref, c_ref: (0, 0, 0))],
            out_specs=pl.BlockSpec((rows, C), lambda i, s_ref, c_ref: (c_ref[0], 0))),
        out_shape=jax.ShapeDtypeStruct((2 * rows, C), F32),
        compiler_params=_params(("arbitrary",)),
    )(shard_arr, c_arr, sums_f32, recv)


def _join_halves(bufs):
    n = len(bufs)

    def body(*refs):
        outs, send_sem, recv_sem = refs[n:2 * n], refs[2 * n], refs[2 * n + 1]
        x, y, c, _ = _position()
        copies = []
        for a in range(n):
            rows = outs[a].shape[0] // 2
            mine = outs[a].at[pl.ds(c * rows, rows), :]
            copies.append(_remote(mine, mine, send_sem.at[a], recv_sem.at[a], (x, y, 1 - c)))
        for cp in copies:
            cp.start()
        for a, cp in enumerate(copies):
            cp.wait_send()
            rows = outs[a].shape[0] // 2
            theirs = outs[a].at[pl.ds((1 - c) * rows, rows), :]
            _remote(theirs, theirs, send_sem.at[a], recv_sem.at[a], (x, y, 1 - c)).wait_recv()

    return pl.pallas_call(
        body, name="join_grad_halves",
        in_specs=[ANY] * n, out_specs=[ANY] * n,
        out_shape=[jax.ShapeDtypeStruct(b.shape, F32) for b in bufs],
        input_output_aliases={i: i for i in range(n)},
        scratch_shapes=[pltpu.SemaphoreType.DMA((n,)), pltpu.SemaphoreType.DMA((n,))],
        compiler_params=pltpu.CompilerParams(has_side_effects=True),
    )(*bufs)


def _small_sum(parts, tag):
    _, R, C = parts.shape

    def body(p_ref, o_ref):
        t = p_ref[0]
        for k in range(1, 8):
            t = t + p_ref[k]
        o_ref[...] = t

    return pl.pallas_call(
        body, name="small_grad_sum_" + tag, grid=(1,),
        in_specs=[pl.BlockSpec((8, R, C), lambda i: (0, 0, 0))], out_specs=pl.BlockSpec((R, C), lambda i: (0, 0)),
        out_shape=jax.ShapeDtypeStruct((R, C), F32), compiler_params=_params(("arbitrary",)),
    )(parts)


def _adamw_math(w, g, m, v):
    m = ADAM_B1 * m + (1.0 - ADAM_B1) * g
    v = ADAM_B2 * v + (1.0 - ADAM_B2) * (g * g)
    m_hat = m / (1.0 - ADAM_B1 ** ADAM_STEP)
    v_hat = v / (1.0 - ADAM_B2 ** ADAM_STEP)
    delta = -ADAM_LR * (m_hat / (jnp.sqrt(v_hat) + ADAM_EPS) + ADAM_WD * w)
    return delta, m, v


def _adamw_big(ws, gs, ms, vs, name, cargo=()):
    n = len(ws)
    nc = len(cargo)
    kinds = [kind for kind, _ in cargo]

    def body(*refs):
        ins, outs = refs[:4 * n], refs[4 * n + nc:7 * n + nc]
        cargo_refs = (kinds, refs[4 * n:4 * n + nc], refs[7 * n + nc:7 * n + 2 * nc], refs[7 * n + 2 * nc:])
        if nc:
            _cargo_start(*cargo_refs, pl.program_id(0) == 0)
        for a in range(n):
            w, g, m, v = (ins[k * n + a][...] for k in range(4))
            outs[a][...], outs[n + a][...], outs[2 * n + a][...] = _adamw_math(w, g, m, v)
        if nc:
            _cargo_finish(*cargo_refs, pl.program_id(0) == 3)

    blks = [pl.BlockSpec((w.shape[0] // 4, w.shape[1]), lambda i: (i, 0)) for w in ws]
    arrays, cargo_specs, shapes, aliases, sems = _cargo_call(cargo, 4 * n, 3 * n)
    out = pl.pallas_call(
        body, name=name, grid=(4,), in_specs=blks * 4 + cargo_specs, out_specs=blks * 3 + cargo_specs,
        out_shape=[jax.ShapeDtypeStruct(w.shape, F32) for w in ws] * 3 + shapes,
        input_output_aliases=aliases, scratch_shapes=sems,
        compiler_params=_params(("arbitrary",)),
    )(*ws, *gs, *ms, *vs, *arrays)
    return (out[:n], out[n:2 * n], out[2 * n:3 * n]), out[3 * n:]


def _adamw_small(ws, gs, ms, vs):
    n = len(ws)

    def body(*refs):
        for a in range(n):
            w, g, m, v = (refs[k * n + a][...] for k in range(4))
            d, nm, nv = _adamw_math(w, g, m, v)
            refs[4 * n + a][...] = d
            refs[5 * n + a][...] = nm
            refs[6 * n + a][...] = nv

    shapes = [jax.ShapeDtypeStruct(w.shape, F32) for w in ws]
    out = pl.pallas_call(body, name="adamw_small", out_shape=shapes * 3)(*ws, *gs, *ms, *vs)
    return out[:n], out[n:2 * n], out[2 * n:]


BIG = ("w_in", "w_out", "w_up", "w_down")
SMALL = ("g_mix_pre", "pool_w", "pool_scale", "g_mix_post", "g_ffn_pre", "conv_b", "g_ffn_post", "conv_w")
ORDER = ("g_mix_pre", "w_in", "pool_w", "pool_scale", "w_out", "g_mix_post", "g_ffn_pre", "w_up", "conv_w", "conv_b",
         "w_down", "g_ffn_post")


def kernel(x, g_mix_pre, w_in, pool_w, pool_scale, w_out, g_mix_post, g_ffn_pre, w_up, conv_w, conv_b, w_down, g_ffn_post, loss_target, m_g_mix_pre, m_w_in, m_pool_w, m_pool_scale, m_w_out, m_g_mix_post, m_g_ffn_pre, m_w_up, m_conv_w, m_conv_b, m_w_down, m_g_ffn_post, v_g_mix_pre, v_w_in, v_pool_w, v_pool_scale, v_w_out, v_g_mix_post, v_g_ffn_pre, v_w_up, v_conv_w, v_conv_b, v_w_down, v_g_ffn_post):
    args = dict(locals())
    W = {n: args[n][0] for n in ORDER}
    M = {n: args["m_" + n][0] for n in ORDER}
    V = {n: args["v_" + n][0] for n in ORDER}
    for d in (W, M, V):
        d["pool_w"] = d["pool_w"].reshape(-1, POOL_GROUP)
        for n in ("g_mix_pre", "pool_scale", "g_mix_post", "g_ffn_pre", "conv_b", "g_ffn_post"):
            d[n] = d[n].reshape(1, -1)
    CW = W["w_up"].shape[1]
    c_arr = lax.axis_index("c").astype(jnp.int32).reshape(1)
    shard = 2 * lax.axis_index("x") + lax.axis_index("y")
    shard_arr = shard.astype(jnp.int32).reshape(1)
    device = 2 * shard + lax.axis_index("c")

    conv_w_slots = lax.dynamic_update_index_in_dim(jnp.zeros((N_SHARD,) + W["conv_w"].shape, F32), W["conv_w"], shard, 0)
    slots = {n: _cast_bf16(W[n], shard_arr, "cast_" + n) for n in BIG}
    w_in_g, conv_w_g = _gather_weights([slots["w_in"], conv_w_slots])
    conv_w_full = conv_w_g.transpose(1, 0, 2).reshape(CONV_WIDTH, 1, N_SHARD * CW)

    loss, grad_x, G = _local_step(
        x[0], loss_target[0], W["g_mix_pre"], w_in_g, W["pool_w"].reshape(-1, POOL_GROUP, POOL_GROUP), W["pool_scale"],
        slots["w_out"], W["g_mix_post"], W["g_ffn_pre"], slots["w_up"], conv_w_full, W["conv_b"],
        slots["w_down"], W["g_ffn_post"], (c_arr, device, shard_arr))

    late, ffn = ("w_in", "w_out"), ("w_up", "w_down")
    from_sibling = _swap_halves([G[n] for n in late], "mix")
    sums = {n: _pair_sum(G[n], r, c_arr, "pair_sum_" + n) for n, r in zip(late, from_sibling)}
    loss_rows = jnp.pad(loss, ((0, 7), (0, LANES - 1)))
    small = jnp.concatenate([_pack_small(G, SMALL_LATE), loss_rows], axis=0)
    small_slots = lax.dynamic_update_index_in_dim(jnp.zeros((8,) + small.shape, F32), small, device, 0)
    pick = lambda d, names: [d[n] for n in names]
    delta, new_m, new_v = {}, {}, {}
    updates, landed = _adamw_big(pick(W, ffn), pick(G, ffn), pick(M, ffn), pick(V, ffn), "adamw_ffn",
                                 [("scatter", sums[n][1]) for n in late] + [("everyone", small_slots)])
    halves = [_shard_sum(sums[n][0], r, shard_arr, c_arr, "shard_sum_" + n) for n, r in zip(late, landed[:2])]
    full = dict(zip(late, _join_halves(halves)))
    full.update({n: G[n] for n in ffn})
    full.update(_unpack_small(_small_sum(G["small_early"], "early"), SMALL_EARLY, W, shard))
    late_total = _small_sum(landed[2], "late")
    full.update(_unpack_small(late_total, SMALL_LATE, W, shard))
    loss = late_total[-8, 0]

    for names, (ds, nms, nvs) in ((ffn, updates),
                                  (late, _adamw_big(pick(W, late), pick(full, late), pick(M, late), pick(V, late), "adamw_mix")[0]),
                                  (SMALL, _adamw_small(pick(W, SMALL), pick(full, SMALL), pick(M, SMALL), pick(V, SMALL)))):
        for n, d, nm, nv in zip(names, ds, nms, nvs):
            delta[n], new_m[n], new_v[n] = d, nm, nv

    shaped = lambda d: [d[n].reshape(args[n].shape) for n in ORDER]
    return (loss, grad_x[None], *shaped(full), *shaped(delta), *shaped(new_m), *shaped(new_v))
```

```python
import functools

import jax
import jax.numpy as jnp
from jax import lax
from jax.experimental import pallas as pl
from jax.experimental.pallas import tpu as pltpu

F32 = jnp.float32
BF16 = jnp.bfloat16

RMS_EPS = 1e-6
NEG_INF = -1e30
N_HEADS = 8
HEAD_DIM = 64
ATTN_WIDTH = N_HEADS * HEAD_DIM
ATTN_SCALE = HEAD_DIM ** -0.5
ATTN_BLOCK = 128
DILATIONS = (1, 4, 16)
RESIDUES_PER_STEP = 4
CONSECUTIVE_BLOCKS = 4
POOL_WINDOWS = (2, 4, 8, 16)
POOL_GROUP = 128
POOL_WIDTH = POOL_GROUP * len(POOL_WINDOWS)
POOL_HALO = 16
CONV_WIDTH = 3
CONV_HALO = 8
N_SHARD = 4
LANES = 128
STAT_LANES = 16
STAT_WIDTH = N_HEADS * STAT_LANES

ADAM_LR = 0.001
ADAM_B1 = 0.9
ADAM_B2 = 0.999
ADAM_EPS = 1e-08
ADAM_WD = 0.01
ADAM_STEP = 10

VMEM_LIMIT = 60 * 1024 * 1024
MESH = pl.DeviceIdType.MESH
NT = (((1,), (1,)), ((), ()))
TN = (((0,), (0,)), ((), ()))


def _params(sem, vmem=None):
    return pltpu.CompilerParams(dimension_semantics=sem, vmem_limit_bytes=vmem)


def _const_spec(shape):
    zeros = (0,) * len(shape)
    return pl.BlockSpec(shape, lambda *_: zeros, pipeline_mode=pl.Buffered(1))


def _dot(a, b):
    return jnp.dot(a, b, preferred_element_type=F32)


def _dot_nt(a, b):
    return lax.dot_general(a, b, NT, preferred_element_type=F32)


def _dot_tn(a, b):
    return lax.dot_general(a, b, TN, preferred_element_type=F32)


def _rms_stats(x):
    r = lax.rsqrt(jnp.mean(x * x, axis=-1, keepdims=True) + RMS_EPS)
    return x * r, r


def _rms_bwd(dy, n, r, g):
    dg = jnp.sum(dy * n, axis=0, keepdims=True)
    dn = dy * g
    dx = r * (dn - n * jnp.mean(dn * n, axis=-1, keepdims=True))
    return dx, dg


def _gelu_tanh(g):
    k = 0.7978845608028654
    kc = k * 0.044715
    g2 = g * g
    t = jnp.tanh(g * (k + kc * g2))
    h = 0.5 * t + 0.5
    dh = (0.5 - 0.5 * (t * t)) * (k + (3.0 * kc) * g2)
    return g * h, h + g * dh


def _residue_shape(S, d, dtype, width=ATTN_WIDTH):
    return jax.ShapeDtypeStruct((S // d, d * width), dtype)


def _residue_spec(TM, d, width=ATTN_WIDTH):
    return pl.BlockSpec((TM // d, d * width), lambda i: (i, 0))


def _token_scratch(TM, width=ATTN_WIDTH):
    return [pltpu.VMEM((TM, LANES), F32)] * (width // LANES)


def _head_stat_matrix(pick_first_lane):
    r = lax.broadcasted_iota(jnp.int32, (ATTN_WIDTH, STAT_WIDTH), 0)
    c = lax.broadcasted_iota(jnp.int32, (ATTN_WIDTH, STAT_WIDTH), 1) // STAT_LANES
    return ((r == c * HEAD_DIM) if pick_first_lane else (r // HEAD_DIM == c)).astype(BF16)


def _bf16_pieces(x, n):
    pieces = []
    for _ in range(n):
        p = x.astype(BF16)
        pieces.append(p)
        x = x - p.astype(F32)
    return pieces


def _put_tokens(dst_s, val):
    for cb, chunk in enumerate(dst_s):
        chunk[...] = val[:, cb * LANES:(cb + 1) * LANES]


def _get_tokens(src_s):
    return jnp.concatenate([chunk[...] for chunk in src_s], axis=1)


def _to_residue(val, src_s, out_ref, d, dtype):
    if d == 1:
        out_ref[...] = val.astype(dtype)
        return
    rows = src_s[0].shape[0]
    for r in range(d):
        for cb, chunk in enumerate(src_s):
            col = (r * len(src_s) + cb) * LANES
            out_ref[:, col:col + LANES] = chunk[pl.ds(r, rows // d, stride=d), :].astype(dtype)


def _from_residue(in_ref, dst_s, d):
    if d == 1:
        return in_ref[...].astype(F32)
    rows = dst_s[0].shape[0]
    for r in range(d):
        for cb, chunk in enumerate(dst_s):
            col = (r * len(dst_s) + cb) * LANES
            chunk[pl.ds(r, rows // d, stride=d), :] = in_ref[:, col:col + LANES].astype(F32)
    return _get_tokens(dst_s)


def _mix_in_fwd(x, g_pre, w_in, cargo=()):
    S, D = x.shape
    TM = 512
    nc = len(cargo)
    kinds = [kind for kind, _ in cargo]
    n_chunks = ATTN_WIDTH // LANES

    def body(x_ref, g_ref, w_ref, *refs):
        cargo_in, refs = refs[:nc], refs[nc:]
        qkv_refs, p_ref, h_ref = refs[:9], refs[9], refs[10]
        t_s = refs[11 + nc:11 + nc + n_chunks]
        cargo_refs = (kinds, cargo_in, refs[11:11 + nc], refs[11 + nc + n_chunks:])
        if nc:
            _cargo_start(*cargo_refs, pl.program_id(0) == 0)
        n, _ = _rms_stats(x_ref[...])
        hb = (n * g_ref[...]).astype(BF16)
        h_ref[...] = hb
        for a in range(3):
            res = _dot(hb, w_ref[a])
            if a == 0:
                res = res * ATTN_SCALE
            _put_tokens(t_s, res)
            for i, d in enumerate(DILATIONS):
                _to_residue(res, t_s, qkv_refs[3 * i + a], d, BF16)
        p_ref[...] = _dot(hb, w_ref[3])
        if nc:
            _cargo_finish(*cargo_refs, pl.program_id(0) == S // TM - 1)

    row = lambda w: pl.BlockSpec((TM, w), lambda i: (i, 0))
    arrays, cargo_specs, shapes, aliases, sems = _cargo_call(cargo, 3, 11)
    out = pl.pallas_call(
        body, name="mix_in_fwd", grid=(S // TM,),
        in_specs=[row(D), _const_spec((1, D)), _const_spec(w_in.shape)] + cargo_specs,
        out_specs=[_residue_spec(TM, d) for d in DILATIONS for _ in range(3)] + [row(POOL_WIDTH), row(D)] + cargo_specs,
        out_shape=[_residue_shape(S, d, BF16) for d in DILATIONS for _ in range(3)]
        + [jax.ShapeDtypeStruct((S, POOL_WIDTH), F32), jax.ShapeDtypeStruct((S, D), BF16)] + shapes,
        input_output_aliases=aliases,
        scratch_shapes=_token_scratch(TM) + sems,
        compiler_params=_params(("arbitrary",), VMEM_LIMIT),
    )(x, g_pre, w_in, *arrays)
    return [out[0:3], out[3:6], out[6:9]], out[9], out[10], out[11:]


def _band_mask(n):
    qi = lax.broadcasted_iota(jnp.int32, (ATTN_BLOCK, 2 * ATTN_BLOCK), 0)
    ki = lax.broadcasted_iota(jnp.int32, (ATTN_BLOCK, 2 * ATTN_BLOCK), 1)
    dist = qi + ATTN_BLOCK - ki
    return (dist >= 0) & (dist <= ATTN_BLOCK) & ((ki >= ATTN_BLOCK) | (n > 0))


def _first_head_lanes():
    return lax.broadcasted_iota(jnp.int32, (1, LANES), 1) < HEAD_DIM


def _stack_heads(pair, first):
    zero = jnp.zeros_like(pair)
    return jnp.concatenate([jnp.where(first, pair, zero), jnp.where(first, zero, pair)], axis=0)


def _unstack_heads(stacked, first):
    return jnp.where(first, stacked[:ATTN_BLOCK], stacked[ATTN_BLOCK:])


CARGO_COPIES = {"ici": 3, "d2d": 3, "scatter": 3, "swap": 1, "everyone": 7, "join": 1}
CARGO_IN_PLACE = ("ici", "d2d", "everyone", "join")


def _cargo_copies(kinds, ins, outs, send_sems, recv_sems, want_recvs=True):
    x, y, c, chips = _position()
    s = 2 * x + y
    me = 2 * s + c
    sibling = (x, y, 1 - c)
    sends, recvs = [], []

    def add(k, src, dst, landing, to):
        sends.append(_remote(src, dst, send_sems.at[k], recv_sems.at[k], to))
        if want_recvs:
            recvs.append(_remote(landing, landing, send_sems.at[k], recv_sems.at[k], to))

    k0 = 0
    for a, kind in enumerate(kinds):
        if kind == "swap":
            rows = ins[a].shape[1] // 2
            add(k0, ins[a].at[:, pl.ds((1 - c) * rows, rows), :], outs[a], outs[a], sibling)
        elif kind == "join":
            rows = outs[a].shape[0] // 2
            mine = outs[a].at[pl.ds(c * rows, rows), :]
            add(k0, mine, mine, outs[a].at[pl.ds((1 - c) * rows, rows), :], sibling)
        elif kind == "everyone":
            for m in range(1, 8):
                peer = (x ^ (m >> 2), y ^ ((m >> 1) & 1), c ^ (m & 1))
                add(k0 + m - 1, outs[a].at[me], outs[a].at[me], outs[a].at[4 * peer[0] + 2 * peer[1] + peer[2]], peer)
        else:
            for j, (px, py) in enumerate(chips):
                sj = 2 * px + py
                if kind == "scatter":
                    add(k0 + j, ins[a].at[sj], outs[a].at[j], outs[a].at[j], (px, py, c))
                    continue
                buf = outs[a]
                rows = buf.shape[1] // 2
                half = lambda shard, h: buf.at[shard, pl.ds(h * rows, rows), :]
                if kind == "ici":
                    add(k0 + j, half(s, c), half(s, c), half(sj, c), (px, py, c))
                else:
                    add(k0 + j, half(sj, c), half(sj, c), half(sj, 1 - c), sibling)
        k0 += CARGO_COPIES[kind]
    return sends, recvs


def _cargo_start(kinds, ins, outs, sems, first_step):
    @pl.when(first_step)
    def _():
        for cp in _cargo_copies(kinds, ins, outs, *sems, want_recvs=False)[0]:
            cp.start()


def _cargo_finish(kinds, ins, outs, sems, last_step):
    @pl.when(last_step)
    def _():
        sends, recvs = _cargo_copies(kinds, ins, outs, *sems)
        for cp in sends:
            cp.wait_send()
        for cp in recvs:
            cp.wait_recv()


def _cargo_call(cargo, n_in, n_out):
    arrays = [a for _, a in cargo]
    shapes = []
    for kind, a in cargo:
        if kind == "scatter":
            shape = (3,) + a.shape[1:]
        elif kind == "swap":
            shape = (a.shape[0], a.shape[1] // 2, a.shape[2])
        else:
            shape = a.shape
        shapes.append(jax.ShapeDtypeStruct(shape, a.dtype))
    aliases = {n_in + i: n_out + i for i, (kind, _) in enumerate(cargo) if kind in CARGO_IN_PLACE}
    n_sems = sum(CARGO_COPIES[kind] for kind, _ in cargo)
    sems = [pltpu.SemaphoreType.DMA((n_sems,))] * 2 if cargo else []
    return arrays, [ANY] * len(cargo), shapes, aliases, sems


def _attn_fwd(q, k, v, d, cargo=()):
    L = q.shape[0]
    group = min(d, RESIDUES_PER_STEP)
    width = group * ATTN_WIDTH
    qb = RESIDUES_PER_STEP // group
    steps = L // (qb * ATTN_BLOCK)
    nc = len(cargo)
    kinds = [kind for kind, _ in cargo]

    def body(*refs):
        q_ref, kp_ref, kc_ref, vp_ref, vc_ref = refs[:5]
        o_ref, lse_ref = refs[5 + nc:7 + nc]
        cargo_refs = (kinds, refs[5:5 + nc], refs[7 + nc:7 + 2 * nc], refs[7 + 2 * nc:])
        r, n = pl.program_id(0), pl.program_id(1)
        if nc:
            _cargo_start(*cargo_refs, (r == 0) & (n == 0))
        first = _first_head_lanes()
        for sub in range(qb):
            rows = slice(sub * ATTN_BLOCK, (sub + 1) * ATTN_BLOCK)
            valid = _band_mask(n if sub == 0 else 1)
            valid2 = jnp.concatenate([valid, valid], axis=0)
            for hp in range(width // LANES):
                sl = slice(hp * LANES, (hp + 1) * LANES)
                if sub == 0:
                    kk = jnp.concatenate([kp_ref[:, sl], kc_ref[rows, sl]], axis=0)
                    vv = jnp.concatenate([vp_ref[:, sl], vc_ref[rows, sl]], axis=0)
                else:
                    keys = slice((sub - 1) * ATTN_BLOCK, (sub + 1) * ATTN_BLOCK)
                    kk, vv = kc_ref[keys, sl], vc_ref[keys, sl]
                s = jnp.where(valid2, _dot_nt(_stack_heads(q_ref[rows, sl], first), kk), NEG_INF)
                m = jnp.max(s, axis=-1, keepdims=True)
                p = jnp.exp(s - m)
                den = jnp.sum(p, axis=-1, keepdims=True)
                o_ref[rows, sl] = _unstack_heads(_dot(p.astype(BF16), vv) / den, first).astype(BF16)
                lse_ref[rows, sl] = _unstack_heads(m + jnp.log(den), first)
        if nc:
            _cargo_finish(*cargo_refs, (r == d // group - 1) & (n == steps - 1))

    cur = pl.BlockSpec((qb * ATTN_BLOCK, width), lambda r, n: (n, r))
    prev = pl.BlockSpec((ATTN_BLOCK, width), lambda r, n: (jnp.maximum(n * qb - 1, 0), r))
    arrays, specs, shapes, aliases, sems = _cargo_call(cargo, 5, 2)
    out = pl.pallas_call(
        body, name=f"attn_fwd_d{d}", grid=(d // group, steps),
        in_specs=[cur, prev, cur, prev, cur] + specs,
        out_specs=[cur, cur] + specs,
        out_shape=[jax.ShapeDtypeStruct((L, d * ATTN_WIDTH), BF16), jax.ShapeDtypeStruct((L, d * ATTN_WIDTH), F32)] + shapes,
        input_output_aliases=aliases, scratch_shapes=sems,
        compiler_params=_params(("arbitrary", "arbitrary")),
    )(q, k, k, v, v, *arrays)
    return out[0], out[1], out[2:]


def _attn_mix(outs, lses):
    S = outs[0].shape[0]
    TM = 512
    n = len(DILATIONS)

    def body(*refs):
        o_refs, l_refs, attn_ref, lse_refs = refs[:n], refs[n:2 * n], refs[2 * n], refs[2 * n + 1:3 * n + 1]
        t_s, c_s = refs[3 * n + 1:-1], refs[-1:]
        os = [_from_residue(o_refs[i], t_s, d) for i, d in enumerate(DILATIONS)]
        ls = [_from_residue(l_refs[i], t_s, d) for i, d in enumerate(DILATIONS)]
        m = jnp.maximum(jnp.maximum(ls[0], ls[1]), ls[2])
        es = [jnp.exp(l - m) for l in ls]
        den = es[0] + es[1] + es[2]
        attn_ref[...] = ((es[0] * os[0] + es[1] * os[1] + es[2] * os[2]) / den).astype(BF16)
        pick = _head_stat_matrix(pick_first_lane=True)
        lse = sum(_dot(p, pick) for p in _bf16_pieces(m + jnp.log(den), 3))
        _put_tokens(c_s, lse)
        for i, d in enumerate(DILATIONS):
            _to_residue(lse, c_s, lse_refs[i], d, F32)

    specs = [_residue_spec(TM, d) for d in DILATIONS]
    out = pl.pallas_call(
        body, name="attn_mix", grid=(S // TM,),
        in_specs=specs * 2, out_specs=[specs[0]] + [_residue_spec(TM, d, STAT_WIDTH) for d in DILATIONS],
        out_shape=[jax.ShapeDtypeStruct((S, ATTN_WIDTH), BF16)]
        + [_residue_shape(S, d, F32, STAT_WIDTH) for d in DILATIONS],
        scratch_shapes=_token_scratch(TM) + _token_scratch(TM, STAT_WIDTH),
        compiler_params=_params(("parallel",)),
    )(*outs, *lses)
    return out[0], out[1:]


def _pool_counts(first_row, rows, w):
    t = first_row + lax.broadcasted_iota(jnp.int32, (rows, 1), 0)
    return jnp.minimum(t + 1, w).astype(F32)


def _trailing_sums(xe, w):
    s, k = xe, 1
    while k < w:
        s = s + pltpu.roll(s, k, 0)
        k *= 2
    return s


def _leading_sums(xe, w):
    rows = xe.shape[0]
    s, k = xe, 1
    while k < w:
        s = s + pltpu.roll(s, rows - k, 0)
        k *= 2
    return s


def _pooled_groups(halo, cur, first_row):
    TM = cur.shape[0]
    xe = jnp.concatenate([halo, cur], axis=0)
    out = []
    for g, w in enumerate(POOL_WINDOWS):
        a = xe[:, g * POOL_GROUP:(g + 1) * POOL_GROUP]
        s = _trailing_sums(a, w)[POOL_HALO:]
        out.append(s / _pool_counts(first_row, TM, w) - a[POOL_HALO:])
    return out


def _pool_fwd(pool_in, pool_w, pool_scale):
    S = pool_in.shape[0]
    TM = 512
    HB = TM // POOL_HALO

    def body(cur_ref, halo_ref, w_ref, sc_ref, y_ref):
        i = pl.program_id(0)
        halo = jnp.where(i > 0, halo_ref[...], 0.0)
        pooled = _pooled_groups(halo, cur_ref[...], i * TM)
        for g in range(len(POOL_WINDOWS)):
            sl = slice(g * POOL_GROUP, (g + 1) * POOL_GROUP)
            y = _dot(pooled[g].astype(BF16), w_ref[g].astype(BF16)) * sc_ref[:, sl]
            y_ref[:, sl] = y.astype(BF16)

    return pl.pallas_call(
        body, name="pool_fwd", grid=(S // TM,),
        in_specs=[pl.BlockSpec((TM, POOL_WIDTH), lambda i: (i, 0)),
                  pl.BlockSpec((POOL_HALO, POOL_WIDTH), lambda i: (jnp.maximum(i * HB - 1, 0), 0)),
                  _const_spec(pool_w.shape), _const_spec((1, POOL_WIDTH))],
        out_specs=pl.BlockSpec((TM, POOL_WIDTH), lambda i: (i, 0)),
        out_shape=jax.ShapeDtypeStruct((S, POOL_WIDTH), BF16),
        compiler_params=_params(("parallel",)),
    )(pool_in, pool_in, pool_w, pool_scale)


def _mix_out_fwd(attn, pool, w_out, x, g_post, g_ffn_pre):
    S, D = x.shape
    TM = 512

    def body(a_ref, p_ref, w_ref, x_ref, gp_ref, gf_ref, mixed_ref, x1_ref, h2_ref, cat_ref):
        ab = a_ref[...]
        cat_ref[:, :ATTN_WIDTH] = ab
        cat_ref[:, ATTN_WIDTH:] = p_ref[...]
        mixed = _dot(ab, w_ref[:ATTN_WIDTH, :]) + _dot(p_ref[...], w_ref[ATTN_WIDTH:, :])
        mixed_ref[...] = mixed.astype(BF16)
        n, _ = _rms_stats(mixed)
        x1 = x_ref[...] + n * gp_ref[...]
        x1_ref[...] = x1
        n2, _ = _rms_stats(x1)
        h2_ref[...] = (n2 * gf_ref[...]).astype(BF16)

    row = lambda w: pl.BlockSpec((TM, w), lambda i: (i, 0))
    return pl.pallas_call(
        body, name="mix_out_fwd", grid=(S // TM,),
        in_specs=[row(ATTN_WIDTH), row(POOL_WIDTH), _const_spec(w_out.shape), row(D),
                  _const_spec((1, D)), _const_spec((1, D))],
        out_specs=[row(D), row(D), row(D), row(D)],
        out_shape=[jax.ShapeDtypeStruct((S, D), BF16), jax.ShapeDtypeStruct((S, D), F32),
                   jax.ShapeDtypeStruct((S, D), BF16), jax.ShapeDtypeStruct((S, D), BF16)],
        compiler_params=_params(("parallel",), VMEM_LIMIT),
    )(attn, pool, w_out, x, g_post, g_ffn_pre)


def _ffn_fwd(h2, x1, target, w_up, w_down, conv_w, conv_b, g_post):
    S, D = x1.shape
    CW = w_up.shape[2]
    FF = 2 * CW
    TM = 256
    piece = 4 * LANES
    pieces = [(lo, min(lo + piece, CW)) for lo in range(0, CW, piece)]

    def body(h2_ref, x1_ref, t_ref, wu_ref, wd_ref, cw_ref, cb_ref, g_ref,
             yv_ref, dy_ref, df_ref, dc_ref, loss_ref, dg_ref, dcb_ref, dcw_ref,
             ue_s, dgate_s, dval_s):
        i = pl.program_id(0)

        @pl.when(i == 0)
        def _():
            loss_ref[...] = jnp.zeros_like(loss_ref)
            dg_ref[...] = jnp.zeros_like(dg_ref)
            dcb_ref[...] = jnp.zeros_like(dcb_ref)
            dcw_ref[...] = jnp.zeros_like(dcw_ref)
            ue_s[0:CONV_HALO, :] = jnp.zeros((CONV_HALO, 2 * FF), F32)

        @pl.when(i > 0)
        def _():
            ue_s[0:CONV_HALO, :] = ue_s[TM:TM + CONV_HALO, :]

        def shifted(cols, k):
            return pltpu.roll(ue_s[:, cols], k, 0)[CONV_HALO:]

        def conv(cols):
            return (cb_ref[:, cols] + cw_ref[2, :, cols] * ue_s[CONV_HALO:, cols]
                    + cw_ref[1, :, cols] * shifted(cols, 1) + cw_ref[0, :, cols] * shifted(cols, 2))

        hb = h2_ref[...]
        f = jnp.zeros((TM, D), F32)
        for j in range(2):
            jc = slice(j * CW, (j + 1) * CW)
            for half in range(2):
                blk = 2 * half + j
                cols = slice(blk * CW, (blk + 1) * CW)
                ue_s[CONV_HALO:, cols] = _dot(hb, wu_ref[blk])
            for lo, hi in pieces:
                pc = slice(j * CW + lo, j * CW + hi)
                gelu, dgelu = _gelu_tanh(conv(pc).astype(BF16))
                val = conv(slice(FF + j * CW + lo, FF + j * CW + hi)).astype(BF16)
                dgate_s[:, pc] = val * dgelu
                dval_s[:, pc] = gelu
                yv_ref[:, pc] = gelu * val
            f = f + _dot(yv_ref[:, jc], wd_ref[jc, :])

        n, r = _rms_stats(f)
        err = x1_ref[...] + n * g_ref[...] - t_ref[...]
        loss_ref[...] += 0.5 * jnp.sum(jnp.mean(err * err, axis=-1, keepdims=True), axis=0, keepdims=True)
        dy = err / D
        dy_ref[...] = dy
        df, dg = _rms_bwd(dy, n, r, g_ref[...])
        dg_ref[...] += dg
        dfb = df.astype(BF16)
        df_ref[...] = dfb

        for j in range(2):
            jc = slice(j * CW, (j + 1) * CW)
            dyv = _dot_nt(dfb, wd_ref[jc, :])
            for lo, hi in pieces:
                pc = slice(j * CW + lo, j * CW + hi)
                for half, scale_s in ((0, dgate_s), (1, dval_s)):
                    cols = slice(half * FF + j * CW + lo, half * FF + j * CW + hi)
                    dcv = dyv[:, lo:hi] * scale_s[:, pc].astype(F32)
                    dc_ref[:, cols] = dcv.astype(BF16)
                    dcb_ref[:, cols] += jnp.sum(dcv, axis=0, keepdims=True)
                    dcw_ref[2, :, cols] += jnp.sum(dcv * ue_s[CONV_HALO:, cols], axis=0, keepdims=True)
                    dcw_ref[1, :, cols] += jnp.sum(dcv * shifted(cols, 1), axis=0, keepdims=True)
                    dcw_ref[0, :, cols] += jnp.sum(dcv * shifted(cols, 2), axis=0, keepdims=True)

    row = lambda w: pl.BlockSpec((TM, w), lambda i: (i, 0))
    acc = lambda shape: pl.BlockSpec(shape, lambda i: (0,) * len(shape))
    return pl.pallas_call(
        body, name="ffn_fwd", grid=(S // TM,),
        in_specs=[row(D), row(D), row(D), _const_spec(w_up.shape), _const_spec(w_down.shape),
                  _const_spec(conv_w.shape), _const_spec((1, 2 * FF)), _const_spec((1, D))],
        out_specs=[row(FF), row(D), row(D), row(2 * FF),
                   acc((1, 1)), acc((1, D)), acc((1, 2 * FF)), acc((CONV_WIDTH, 1, 2 * FF))],
        out_shape=[jax.ShapeDtypeStruct((S, FF), BF16),
                   jax.ShapeDtypeStruct((S, D), F32), jax.ShapeDtypeStruct((S, D), BF16),
                   jax.ShapeDtypeStruct((S, 2 * FF), BF16),
                   jax.ShapeDtypeStruct((1, 1), F32), jax.ShapeDtypeStruct((1, D), F32),
                   jax.ShapeDtypeStruct((1, 2 * FF), F32), jax.ShapeDtypeStruct((CONV_WIDTH, 1, 2 * FF), F32)],
        scratch_shapes=[pltpu.VMEM((TM + CONV_HALO, 2 * FF), F32), pltpu.VMEM((TM, FF), BF16),
                        pltpu.VMEM((TM, FF), BF16)],
        compiler_params=_params(("arbitrary",), VMEM_LIMIT),
    )(h2, x1, target, w_up, w_down, conv_w, conv_b, g_post)


def _ffn_bwd(dc, conv_w, w_up, x1, g_ffn_pre, dy):
    S, D = x1.shape
    CW = w_up.shape[2]
    F2 = 4 * CW
    TM = 256
    HB = TM // CONV_HALO
    last = S // CONV_HALO - 1
    n_tiles = S // TM

    def body(dc_ref, halo_ref, cw_ref, wu_ref, x1_ref, g_ref, dy_ref, du_ref, dx1_ref, dg_ref):
        i = pl.program_id(0)

        @pl.when(i == 0)
        def _():
            dg_ref[...] = jnp.zeros_like(dg_ref)

        keep = i < n_tiles - 1
        dh2 = jnp.zeros((TM, D), F32)
        for blk in range(N_SHARD):
            cols = slice(blk * CW, (blk + 1) * CW)
            halo = jnp.where(keep, halo_ref[:, cols].astype(F32), 0.0)
            dce = jnp.concatenate([dc_ref[:, cols].astype(F32), halo], axis=0)
            rows = TM + CONV_HALO
            du = (cw_ref[2, :, cols] * dce[:TM]
                  + cw_ref[1, :, cols] * pltpu.roll(dce, rows - 1, 0)[:TM]
                  + cw_ref[0, :, cols] * pltpu.roll(dce, rows - 2, 0)[:TM])
            dub = du.astype(BF16)
            du_ref[:, cols] = dub
            dh2 = dh2 + _dot_nt(dub, wu_ref[blk])
        n2, r2 = _rms_stats(x1_ref[...])
        dx, dg = _rms_bwd(dh2, n2, r2, g_ref[...])
        dg_ref[...] += dg
        dx1_ref[...] = (dy_ref[...] + dx).astype(BF16)

    row = lambda w: pl.BlockSpec((TM, w), lambda i: (i, 0))
    return pl.pallas_call(
        body, name="ffn_bwd", grid=(S // TM,),
        in_specs=[row(F2), pl.BlockSpec((CONV_HALO, F2), lambda i: (jnp.minimum((i + 1) * HB, last), 0)),
                  _const_spec(conv_w.shape), _const_spec(w_up.shape), row(D), _const_spec((1, D)), row(D)],
        out_specs=[row(F2), row(D), pl.BlockSpec((1, D), lambda i: (0, 0))],
        out_shape=[jax.ShapeDtypeStruct((S, F2), BF16), jax.ShapeDtypeStruct((S, D), BF16),
                   jax.ShapeDtypeStruct((1, D), F32)],
        compiler_params=_params(("arbitrary",), VMEM_LIMIT),
    )(dc, dc, conv_w, w_up, x1, g_ffn_pre, dy)


def _matmul_tn(a, b, n_blocks, name):
    S, M = a.shape
    N = b.shape[1]
    tn = N // n_blocks
    tm = M if M <= 1024 else M // 2
    tk = 2048
    nk = S // tk

    def body(a_ref, b_ref, o_ref):
        @pl.when(pl.program_id(2) == 0)
        def _():
            o_ref[...] = jnp.zeros_like(o_ref)
        o_ref[0] += _dot_tn(a_ref[...], b_ref[...])

    return pl.pallas_call(
        body, name=name, grid=(M // tm, n_blocks, nk),
        in_specs=[pl.BlockSpec((tk, tm), lambda i, j, k: (k, i)), pl.BlockSpec((tk, tn), lambda i, j, k: (k, j))],
        out_specs=pl.BlockSpec((1, tm, tn), lambda i, j, k: (j, i, 0)),
        out_shape=jax.ShapeDtypeStruct((n_blocks, M, tn), F32),
        compiler_params=_params(("parallel", "parallel", "arbitrary"), VMEM_LIMIT),
    )(a, b)


def _mix_out_bwd(dx1, mixed, g_post, w_out, attn, cargo=()):
    S, D = dx1.shape
    TM = 512
    nd = len(DILATIONS)
    nc = len(cargo)
    kinds = [kind for kind, _ in cargo]
    n_chunks = ATTN_WIDTH // LANES

    def body(*refs):
        dx_ref, m_ref, g_ref, w_ref, a_ref = refs[:5]
        dm_ref, dp_ref, dg_ref = refs[5 + nc:8 + nc]
        da_refs, dl_refs = refs[8 + nc:8 + nc + nd], refs[8 + nc + nd:8 + nc + 2 * nd]
        n_out = 8 + nc + 2 * nd
        t_s = refs[n_out + nc:n_out + nc + n_chunks]
        c_s = refs[n_out + nc + n_chunks:n_out + nc + n_chunks + 1]
        cargo_refs = (kinds, refs[5:5 + nc], refs[n_out:n_out + nc], refs[n_out + nc + n_chunks + 1:])
        if nc:
            _cargo_start(*cargo_refs, pl.program_id(0) == 0)

        @pl.when(pl.program_id(0) == 0)
        def _():
            dg_ref[...] = jnp.zeros_like(dg_ref)

        n, r = _rms_stats(m_ref[...].astype(F32))
        dm, dg = _rms_bwd(dx_ref[...].astype(F32), n, r, g_ref[...])
        dg_ref[...] += dg
        dmb = dm.astype(BF16)
        dm_ref[...] = dmb
        da = _dot_nt(dmb, w_ref[:ATTN_WIDTH, :])
        _put_tokens(t_s, da)
        for i, d in enumerate(DILATIONS):
            _to_residue(da, t_s, da_refs[i], d, BF16)
        dp_ref[...] = _dot_nt(dmb, w_ref[ATTN_WIDTH:, :]).astype(BF16)
        gather = _head_stat_matrix(pick_first_lane=False)
        delta = sum(_dot(p, gather) for p in _bf16_pieces(da * a_ref[...].astype(F32), 2))
        _put_tokens(c_s, delta)
        for i, d in enumerate(DILATIONS):
            _to_residue(delta, c_s, dl_refs[i], d, F32)
        if nc:
            _cargo_finish(*cargo_refs, pl.program_id(0) == S // TM - 1)

    row = lambda w: pl.BlockSpec((TM, w), lambda i: (i, 0))
    specs = [_residue_spec(TM, d) for d in DILATIONS]
    arrays, cargo_specs, shapes, aliases, sems = _cargo_call(cargo, 5, 3 + 2 * nd)
    out = pl.pallas_call(
        body, name="mix_out_bwd", grid=(S // TM,),
        in_specs=[row(D), row(D), _const_spec((1, D)), _const_spec(w_out.shape), row(ATTN_WIDTH)] + cargo_specs,
        out_specs=[row(D), row(POOL_WIDTH), pl.BlockSpec((1, D), lambda i: (0, 0))] + specs
        + [_residue_spec(TM, d, STAT_WIDTH) for d in DILATIONS] + cargo_specs,
        out_shape=[jax.ShapeDtypeStruct((S, D), BF16), jax.ShapeDtypeStruct((S, POOL_WIDTH), BF16),
                   jax.ShapeDtypeStruct((1, D), F32)]
        + [_residue_shape(S, d, BF16) for d in DILATIONS]
        + [_residue_shape(S, d, F32, STAT_WIDTH) for d in DILATIONS] + shapes,
        input_output_aliases=aliases,
        scratch_shapes=_token_scratch(TM) + _token_scratch(TM, STAT_WIDTH) + sems,
        compiler_params=_params(("arbitrary",), VMEM_LIMIT),
    )(dx1, mixed, g_post, w_out, attn, *arrays)
    return out[0], out[1], out[2], out[3:3 + nd], out[3 + nd:3 + 2 * nd], out[3 + 2 * nd:]


def _pool_bwd(pool_in, d_pool, pool_w, pool_scale):
    S = pool_in.shape[0]
    TM = 512
    HB = TM // POOL_HALO
    last = S // POOL_HALO - 1
    G = len(POOL_WINDOWS)

    def body(cur_ref, halo_ref, dcur_ref, dnext_ref, w_ref, sc_ref, dxin_ref, dw_ref, dsc_ref):
        i = pl.program_id(0)

        @pl.when(i == 0)
        def _():
            dw_ref[...] = jnp.zeros_like(dw_ref)
            dsc_ref[...] = jnp.zeros_like(dsc_ref)

        halo = jnp.where(i > 0, halo_ref[...], 0.0)
        pooled = _pooled_groups(halo, cur_ref[...], i * TM)
        dnext = jnp.where(i < S // TM - 1, dnext_ref[...].astype(F32), 0.0)
        dye = jnp.concatenate([dcur_ref[...].astype(F32), dnext], axis=0)
        for g, w in enumerate(POOL_WINDOWS):
            sl = slice(g * POOL_GROUP, (g + 1) * POOL_GROUP)
            wg = w_ref[g].astype(BF16)
            pb = pooled[g].astype(BF16)
            dsc_ref[:, sl] += jnp.sum(dye[:TM, sl] * _dot(pb, wg), axis=0, keepdims=True)
            dpre = (dye[:, sl] * sc_ref[:, sl]).astype(BF16)
            dw_ref[g] += _dot_tn(pb, dpre[:TM])
            dpooled = _dot_nt(dpre, wg)
            z = dpooled / _pool_counts(i * TM, TM + POOL_HALO, w)
            dxin_ref[:, sl] = (_leading_sums(z, w)[:TM] - dpooled[:TM]).astype(BF16)

    row = pl.BlockSpec((TM, POOL_WIDTH), lambda i: (i, 0))
    return pl.pallas_call(
        body, name="pool_bwd", grid=(S // TM,),
        in_specs=[row, pl.BlockSpec((POOL_HALO, POOL_WIDTH), lambda i: (jnp.maximum(i * HB - 1, 0), 0)),
                  row, pl.BlockSpec((POOL_HALO, POOL_WIDTH), lambda i: (jnp.minimum((i + 1) * HB, last), 0)),
                  _const_spec(pool_w.shape), _const_spec((1, POOL_WIDTH))],
        out_specs=[row, pl.BlockSpec((G, POOL_GROUP, POOL_GROUP), lambda i: (0, 0, 0)),
                   pl.BlockSpec((1, POOL_WIDTH), lambda i: (0, 0))],
        out_shape=[jax.ShapeDtypeStruct((S, POOL_WIDTH), BF16), jax.ShapeDtypeStruct((G, POOL_GROUP, POOL_GROUP), F32),
                   jax.ShapeDtypeStruct((1, POOL_WIDTH), F32)],
        compiler_params=_params(("arbitrary",)),
    )(pool_in, pool_in, d_pool, d_pool, pool_w, pool_scale)


def _attn_bwd(q, k, v, d_attn, lse, delta, d, cargo=()):
    L = q.shape[0]
    nb = L // ATTN_BLOCK
    group = min(d, RESIDUES_PER_STEP)
    width = group * ATTN_WIDTH
    nc = len(cargo)
    kinds = [kind for kind, _ in cargo]

    def body(*refs):
        q_ref, kp_ref, kc_ref, vp_ref, vc_ref, do_ref, lse_ref, dl_ref = refs[:8]
        dq_ref, dk_ref, dv_ref = refs[8 + nc:11 + nc]
        ck_s, cv_s = refs[11 + 2 * nc:13 + 2 * nc]
        cargo_refs = (kinds, refs[8:8 + nc], refs[11 + nc:11 + 2 * nc], refs[13 + 2 * nc:])
        r, n = pl.program_id(0), pl.program_id(1)
        if nc:
            _cargo_start(*cargo_refs, (r == 0) & (n == 0))

        @pl.when(n == 0)
        def _():
            ck_s[...] = jnp.zeros_like(ck_s)
            cv_s[...] = jnp.zeros_like(cv_s)

        @pl.when(n < nb)
        def _():
            valid = _band_mask(n)
            valid2 = jnp.concatenate([valid, valid], axis=0)
            first = _first_head_lanes()

            def stacked_column(ref, hp):
                lane = 2 * hp * STAT_LANES
                return jnp.concatenate([ref[:, lane:lane + 1], ref[:, lane + STAT_LANES:lane + STAT_LANES + 1]], axis=0)

            for hp in range(width // LANES):
                sl = slice(hp * LANES, (hp + 1) * LANES)
                qq = _stack_heads(q_ref[:, sl], first)
                dd = _stack_heads(do_ref[:, sl], first)
                kk = jnp.concatenate([kp_ref[:, sl], kc_ref[:, sl]], axis=0)
                vv = jnp.concatenate([vp_ref[:, sl], vc_ref[:, sl]], axis=0)
                s = _dot_nt(qq, kk)
                p = jnp.where(valid2, jnp.exp(s - stacked_column(lse_ref, hp)), 0.0)
                dp = _dot_nt(dd, vv)
                ds = (p * (dp - stacked_column(dl_ref, hp))).astype(BF16)
                dq_ref[:, sl] = (_unstack_heads(_dot(ds, kk), first) * ATTN_SCALE).astype(BF16)
                dk = _dot_tn(ds, qq)
                dv = _dot_tn(p.astype(BF16), dd)
                dk_ref[:, sl] = (ck_s[:, sl] + dk[:ATTN_BLOCK]).astype(BF16)
                dv_ref[:, sl] = (cv_s[:, sl] + dv[:ATTN_BLOCK]).astype(BF16)
                ck_s[:, sl] = dk[ATTN_BLOCK:]
                cv_s[:, sl] = dv[ATTN_BLOCK:]

        @pl.when(n == nb)
        def _():
            dk_ref[...] = ck_s[...].astype(BF16)
            dv_ref[...] = cv_s[...].astype(BF16)

        if nc:
            _cargo_finish(*cargo_refs, (r == d // group - 1) & (n == nb))

    blk = (ATTN_BLOCK, width)
    cur = pl.BlockSpec(blk, lambda r, n: (jnp.minimum(n, nb - 1), r))
    stat = pl.BlockSpec((ATTN_BLOCK, group * STAT_WIDTH), lambda r, n: (jnp.minimum(n, nb - 1), r))
    prev = pl.BlockSpec(blk, lambda r, n: (jnp.maximum(jnp.minimum(n, nb - 1) - 1, 0), r))
    done = pl.BlockSpec(blk, lambda r, n: (jnp.maximum(n - 1, 0), r))
    arrays, specs, shapes, aliases, sems = _cargo_call(cargo, 8, 3)
    out = pl.pallas_call(
        body, name=f"attn_bwd_d{d}", grid=(d // group, nb + 1),
        in_specs=[cur, prev, cur, prev, cur, cur, stat, stat] + specs, out_specs=[cur, done, done] + specs,
        out_shape=[jax.ShapeDtypeStruct((L, d * ATTN_WIDTH), BF16)] * 3 + shapes,
        input_output_aliases=aliases,
        scratch_shapes=[pltpu.VMEM(blk, F32), pltpu.VMEM(blk, F32)] + sems,
        compiler_params=_params(("arbitrary", "arbitrary")),
    )(q, k, k, v, v, d_attn, lse, delta, *arrays)
    return out[:3], out[3:]


def _attn_bwd_consecutive(q, k, v, d_attn, lse, delta, cargo=()):
    L = q.shape[0]
    qb = CONSECUTIVE_BLOCKS
    steps = L // (qb * ATTN_BLOCK)
    nc = len(cargo)
    kinds = [kind for kind, _ in cargo]

    def body(*refs):
        q_ref, kp_ref, kc_ref, vp_ref, vc_ref, do_ref, lse_ref, dl_ref = refs[:8]
        dq_ref, dk_ref, dv_ref, ek_ref, ev_ref = refs[8 + nc:13 + nc]
        cargo_refs = (kinds, refs[8:8 + nc], refs[13 + nc:13 + 2 * nc], refs[13 + 2 * nc:])
        n = pl.program_id(0)
        if nc:
            _cargo_start(*cargo_refs, n == 0)
        first = _first_head_lanes()
        for hp in range(ATTN_WIDTH // LANES):
            sl = slice(hp * LANES, (hp + 1) * LANES)
            for sub in range(qb):
                rows = slice(sub * ATTN_BLOCK, (sub + 1) * ATTN_BLOCK)
                valid = _band_mask(n if sub == 0 else 1)
                valid2 = jnp.concatenate([valid, valid], axis=0)
                if sub == 0:
                    kk = jnp.concatenate([kp_ref[:, sl], kc_ref[rows, sl]], axis=0)
                    vv = jnp.concatenate([vp_ref[:, sl], vc_ref[rows, sl]], axis=0)
                else:
                    keys = slice((sub - 1) * ATTN_BLOCK, (sub + 1) * ATTN_BLOCK)
                    kk, vv = kc_ref[keys, sl], vc_ref[keys, sl]
                qq = _stack_heads(q_ref[rows, sl], first)
                dd = _stack_heads(do_ref[rows, sl], first)
                lane = 2 * hp * STAT_LANES
                column = lambda ref: jnp.concatenate(
                    [ref[rows, lane:lane + 1], ref[rows, lane + STAT_LANES:lane + STAT_LANES + 1]], axis=0)
                p = jnp.where(valid2, jnp.exp(_dot_nt(qq, kk) - column(lse_ref)), 0.0)
                ds = (p * (_dot_nt(dd, vv) - column(dl_ref))).astype(BF16)
                dq_ref[rows, sl] = (_unstack_heads(_dot(ds, kk), first) * ATTN_SCALE).astype(BF16)
                dk = _dot_tn(ds, qq)
                dv = _dot_tn(p.astype(BF16), dd)
                if sub == 0:
                    ek_ref[:, sl] = dk[:ATTN_BLOCK].astype(BF16)
                    ev_ref[:, sl] = dv[:ATTN_BLOCK].astype(BF16)
                else:
                    before = slice((sub - 1) * ATTN_BLOCK, sub * ATTN_BLOCK)
                    dk_ref[before, sl] = (carry_k + dk[:ATTN_BLOCK]).astype(BF16)
                    dv_ref[before, sl] = (carry_v + dv[:ATTN_BLOCK]).astype(BF16)
                carry_k, carry_v = dk[ATTN_BLOCK:], dv[ATTN_BLOCK:]
            dk_ref[rows, sl] = carry_k.astype(BF16)
            dv_ref[rows, sl] = carry_v.astype(BF16)
        if nc:
            _cargo_finish(*cargo_refs, n == steps - 1)

    cur = pl.BlockSpec((qb * ATTN_BLOCK, ATTN_WIDTH), lambda n: (n, 0))
    prev = pl.BlockSpec((ATTN_BLOCK, ATTN_WIDTH), lambda n: (jnp.maximum(n * qb - 1, 0), 0))
    edge = pl.BlockSpec((ATTN_BLOCK, ATTN_WIDTH), lambda n: (n, 0))
    stat = pl.BlockSpec((qb * ATTN_BLOCK, STAT_WIDTH), lambda n: (n, 0))
    arrays, specs, shapes, aliases, sems = _cargo_call(cargo, 8, 5)
    out = pl.pallas_call(
        body, name="attn_bwd_d1", grid=(steps,),
        in_specs=[cur, prev, cur, prev, cur, cur, stat, stat] + specs, out_specs=[cur, cur, cur, edge, edge] + specs,
        out_shape=[jax.ShapeDtypeStruct((L, ATTN_WIDTH), BF16)] * 3
        + [jax.ShapeDtypeStruct((steps * ATTN_BLOCK, ATTN_WIDTH), BF16)] * 2 + shapes,
        input_output_aliases=aliases, scratch_shapes=sems,
        compiler_params=_params(("arbitrary",)),
    )(q, k, k, v, v, d_attn, lse, delta, *arrays)
    return out[:3], out[3:5], out[5:]


def _mix_in_bwd(dqkv, edges, d_pool_in, w_in, x, g_pre, dx1):
    S, D = x.shape
    TM = CONSECUTIVE_BLOCKS * ATTN_BLOCK
    nd = len(DILATIONS)
    n_tiles = S // TM

    def body(*refs):
        g_refs = refs[:3 * nd]
        e_refs = (None,) + refs[3 * nd:3 * nd + 2]
        dpi_ref, w_ref, x_ref, g_ref, dx1_ref, dproj_ref, gx_ref, dg_ref = refs[3 * nd + 2:3 * nd + 10]
        t_s = refs[3 * nd + 10:]

        @pl.when(pl.program_id(0) == 0)
        def _():
            dg_ref[...] = jnp.zeros_like(dg_ref)

        dh = jnp.zeros((TM, D), F32)
        for a in range(4):
            if a < 3:
                tot = g_refs[a][...].astype(F32)
                if a > 0:
                    late = jnp.where(pl.program_id(0) < n_tiles - 1, e_refs[a][...].astype(F32), 0.0)
                    tot = jnp.concatenate([tot[:TM - ATTN_BLOCK], tot[TM - ATTN_BLOCK:] + late], axis=0)
                for i, d in enumerate(DILATIONS[1:]):
                    tot = tot + _from_residue(g_refs[3 * (i + 1) + a], t_s, d)
                db = tot.astype(BF16)
            else:
                db = dpi_ref[...]
            dproj_ref[:, a * ATTN_WIDTH:(a + 1) * ATTN_WIDTH] = db
            dh = dh + _dot_nt(db, w_ref[a])
        n, r = _rms_stats(x_ref[...])
        dx, dg = _rms_bwd(dh, n, r, g_ref[...])
        dg_ref[...] += dg
        gx_ref[...] = dx1_ref[...].astype(F32) + dx

    row = lambda w: pl.BlockSpec((TM, w), lambda i: (i, 0))
    edge = pl.BlockSpec((ATTN_BLOCK, ATTN_WIDTH), lambda i: (jnp.minimum(i + 1, n_tiles - 1), 0))
    return pl.pallas_call(
        body, name="mix_in_bwd", grid=(S // TM,),
        in_specs=[_residue_spec(TM, d) for d in DILATIONS for _ in range(3)] + [edge, edge]
        + [row(POOL_WIDTH), _const_spec(w_in.shape), row(D), _const_spec((1, D)), row(D)],
        out_specs=[row(4 * ATTN_WIDTH), row(D), pl.BlockSpec((1, D), lambda i: (0, 0))],
        out_shape=[jax.ShapeDtypeStruct((S, 4 * ATTN_WIDTH), BF16), jax.ShapeDtypeStruct((S, D), F32),
                   jax.ShapeDtypeStruct((1, D), F32)],
        scratch_shapes=_token_scratch(TM),
        compiler_params=_params(("arbitrary",), VMEM_LIMIT),
    )(*[g for gs in dqkv for g in gs], *edges, d_pool_in, w_in, x, g_pre, dx1)


SMALL_EARLY = ("pool_w", "pool_scale", "g_mix_post", "g_ffn_pre", "conv_b", "g_ffn_post", "conv_w")
SMALL_LATE = ("g_mix_pre",)


def _pack_small(grads, names):
    parts = []
    for n in names:
        g = grads[n]
        if n == "conv_w":
            g = g.reshape(CONV_WIDTH, N_SHARD, -1).transpose(1, 0, 2)
        parts.append(g.reshape(-1, LANES))
    return jnp.concatenate(parts, axis=0) if len(parts) > 1 else parts[0]


def _unpack_small(packed, names, like, shard):
    out, row = {}, 0
    for n in names:
        size = like[n].size * (N_SHARD if n == "conv_w" else 1)
        g = packed[row:row + size // LANES]
        row += size // LANES
        if n == "conv_w":
            g = lax.dynamic_slice_in_dim(g.reshape((N_SHARD,) + like[n].shape), shard, 1, axis=0)[0]
        out[n] = g.reshape(like[n].shape)
    return out


def _local_step(x, target, g_mix_pre, w_in, pool_w, pool_scale, w_out, g_mix_post, g_ffn_pre,
                w_up, conv_w, conv_b, w_down, g_ffn_post, mesh_pos=None):
    on_mesh = mesh_pos is not None
    D = x.shape[1]
    CW = w_up.shape[2]
    qkv, pool_in, h1, got = _mix_in_fwd(x, g_mix_pre, w_in, [("ici", w_up)] if on_mesh else ())
    w_up = got[0] if on_mesh else w_up
    o1, l1, got = _attn_fwd(*qkv[0], 1, [("d2d", w_up), ("ici", w_out)] if on_mesh else ())
    w_up, w_out = got if on_mesh else (w_up, w_out)
    o4, l4, got = _attn_fwd(*qkv[1], 4, [("d2d", w_out), ("ici", w_down)] if on_mesh else ())
    w_out, w_down = got if on_mesh else (w_out, w_down)
    o16, l16, got = _attn_fwd(*qkv[2], 16, [("d2d", w_down)] if on_mesh else ())
    w_down = got[0] if on_mesh else w_down
    w_out = w_out.reshape(D, D)
    w_down = w_down.reshape(2 * CW, D)
    attn, lse = _attn_mix((o1, o4, o16), (l1, l4, l16))
    pool = _pool_fwd(pool_in, pool_w, pool_scale)
    mixed, x1, h2, cat = _mix_out_fwd(attn, pool, w_out, x, g_mix_post, g_ffn_pre)

    yv, dy, df, dc, loss, d_g_ffn_post, d_conv_b, d_conv_w = _ffn_fwd(
        h2, x1, target, w_up, w_down, conv_w, conv_b, g_ffn_post)
    du, dx1, d_g_ffn_pre = _ffn_bwd(dc, conv_w, w_up, x1, g_ffn_pre, dy)
    d_w_up = _matmul_tn(h2, du, N_SHARD, "grad_w_up")
    d_w_down = _matmul_tn(yv, df, 1, "grad_w_down")[0].reshape(N_SHARD, CW // 2, D)
    swap = [("swap", d_w_up), ("swap", d_w_down)] if on_mesh else ()
    d_mixed, d_pool, d_g_mix_post, d_attn, delta, from_sibling = _mix_out_bwd(dx1, mixed, g_mix_post, w_out, attn, swap)
    d_w_out = _matmul_tn(cat, d_mixed, 1, "grad_w_out")[0].reshape(N_SHARD, D // N_SHARD, D)
    d_pool_in, d_pool_w, d_pool_scale = _pool_bwd(pool_in, d_pool, pool_w, pool_scale)
    grads = dict(pool_w=d_pool_w, pool_scale=d_pool_scale, w_out=d_w_out, g_mix_post=d_g_mix_post,
                 g_ffn_pre=d_g_ffn_pre, w_up=d_w_up, conv_w=d_conv_w, conv_b=d_conv_b, w_down=d_w_down,
                 g_ffn_post=d_g_ffn_post)
    cargo = [(), (), ()]
    if on_mesh:
        c_arr, device, shard_arr = mesh_pos
        up_f32, up_bf16 = _pair_sum(d_w_up, from_sibling[0], c_arr, "pair_sum_w_up")
        down_f32, down_bf16 = _pair_sum(d_w_down, from_sibling[1], c_arr, "pair_sum_w_down")
        early = _pack_small(grads, SMALL_EARLY)
        early_slots = lax.dynamic_update_index_in_dim(jnp.zeros((8,) + early.shape, F32), early, device, 0)
        cargo = [[("scatter", down_bf16), ("everyone", early_slots)], [("scatter", up_bf16)], []]

    dqkv1, edges, landed1 = _attn_bwd_consecutive(*qkv[0], d_attn[0], lse[0], delta[0], cargo[0])
    dqkv4, landed4 = _attn_bwd(*qkv[1], d_attn[1], lse[1], delta[1], 4, cargo[1])
    if on_mesh:
        halves = [_shard_sum(up_f32, landed4[0], shard_arr, c_arr, "shard_sum_w_up"),
                  _shard_sum(down_f32, landed1[0], shard_arr, c_arr, "shard_sum_w_down")]
        cargo[2] = cargo[2] + [("join", h) for h in halves]
    dqkv16, landed16 = _attn_bwd(*qkv[2], d_attn[2], lse[2], delta[2], 16, cargo[2])
    if on_mesh:
        grads.update(small_early=landed1[1], w_up=landed16[0], w_down=landed16[1])
    d_proj, grad_x, grads["g_mix_pre"] = _mix_in_bwd((dqkv1, dqkv4, dqkv16), edges, d_pool_in, w_in, x, g_mix_pre, dx1)
    grads["w_in"] = _matmul_tn(h1, d_proj, N_SHARD, "grad_w_in")
    return loss, grad_x, grads


ANY = pl.BlockSpec(memory_space=pl.ANY)


def _position():
    x, y, c = lax.axis_index("x"), lax.axis_index("y"), lax.axis_index("c")
    chips = [(1 - x, y), (x, 1 - y), (1 - x, 1 - y)]
    return x, y, c, chips


def _remote(src, dst, send_sem, recv_sem, to):
    return pltpu.make_async_remote_copy(src_ref=src, dst_ref=dst, send_sem=send_sem, recv_sem=recv_sem,
                                        device_id=to, device_id_type=MESH)


def _cast_bf16(w, shard_arr, name):
    R, C = w.shape
    tr = R // 2

    def body(s_ref, w_ref, o_ref):
        o_ref[0] = w_ref[...].astype(BF16)

    return pl.pallas_call(
        body, name=name,
        grid_spec=pltpu.PrefetchScalarGridSpec(
            num_scalar_prefetch=1, grid=(2,),
            in_specs=[pl.BlockSpec((tr, C), lambda i, s_ref: (i, 0))],
            out_specs=pl.BlockSpec((1, tr, C), lambda i, s_ref: (s_ref[0], i, 0))),
        out_shape=jax.ShapeDtypeStruct((N_SHARD, R, C), BF16),
        compiler_params=_params(("parallel",)))(shard_arr, w)


def _gather_weights(bufs):
    n = len(bufs) - 1

    def body(*refs):
        outs, cw_out = refs[n + 1:2 * n + 1], refs[2 * n + 1]
        ici_send, ici_recv, d2d_send, d2d_recv = refs[2 * n + 2:]
        x, y, c, chips = _position()
        s = 2 * x + y
        sibling = (x, y, 1 - c)

        def half(a, shard, h):
            rows = outs[a].shape[1] // 2
            return outs[a].at[shard, pl.ds(h * rows, rows), :]

        sends = []
        for a in range(n):
            for j, (px, py) in enumerate(chips):
                sends.append(_remote(half(a, s, c), half(a, s, c),
                                     ici_send.at[3 * a + j], ici_recv.at[3 * a + j], (px, py, c)))
        for j, (px, py) in enumerate(chips):
            sends.append(_remote(cw_out.at[s], cw_out.at[s], ici_send.at[3 * n + j], ici_recv.at[3 * n + j], (px, py, c)))
        for cp in sends:
            cp.start()
        passed = []
        for a in range(n):
            for j, (px, py) in enumerate(chips):
                sj = 2 * px + py
                got = half(a, sj, c)
                _remote(got, got, ici_send.at[3 * a + j], ici_recv.at[3 * a + j], (px, py, c)).wait_recv()
                fwd = _remote(got, got, d2d_send.at[3 * a + j], d2d_recv.at[3 * a + j], sibling)
                fwd.start()
                passed.append(fwd)
        for j, (px, py) in enumerate(chips):
            got = cw_out.at[2 * px + py]
            _remote(got, got, ici_send.at[3 * n + j], ici_recv.at[3 * n + j], (px, py, c)).wait_recv()
        for a in range(n):
            for j, (px, py) in enumerate(chips):
                got = half(a, 2 * px + py, 1 - c)
                _remote(got, got, d2d_send.at[3 * a + j], d2d_recv.at[3 * a + j], sibling).wait_recv()
        for cp in sends + passed:
            cp.wait_send()

    return pl.pallas_call(
        body, name="gather_weights",
        in_specs=[ANY] * (n + 1), out_specs=[ANY] * (n + 1),
        out_shape=[jax.ShapeDtypeStruct(b.shape, b.dtype) for b in bufs],
        input_output_aliases={i: i for i in range(n + 1)},
        scratch_shapes=[pltpu.SemaphoreType.DMA((3 * n + 3,)), pltpu.SemaphoreType.DMA((3 * n + 3,)),
                        pltpu.SemaphoreType.DMA((3 * n,)), pltpu.SemaphoreType.DMA((3 * n,))],
        compiler_params=pltpu.CompilerParams(has_side_effects=True),
    )(*bufs)


def _swap_halves(grads, tag):
    n = len(grads)

    def body(*refs):
        ins, outs, send_sem, recv_sem = refs[:n], refs[n:2 * n], refs[2 * n], refs[2 * n + 1]
        x, y, c, _ = _position()
        copies = []
        for a in range(n):
            rows = ins[a].shape[1] // 2
            copies.append(_remote(ins[a].at[:, pl.ds((1 - c) * rows, rows), :], outs[a],
                                  send_sem.at[a], recv_sem.at[a], (x, y, 1 - c)))
        for cp in copies:
            cp.start()
        for cp in copies:
            cp.wait()

    return pl.pallas_call(
        body, name="swap_grad_halves_" + tag,
        in_specs=[ANY] * n, out_specs=[ANY] * n,
        out_shape=[jax.ShapeDtypeStruct((g.shape[0], g.shape[1] // 2, g.shape[2]), F32) for g in grads],
        scratch_shapes=[pltpu.SemaphoreType.DMA((n,)), pltpu.SemaphoreType.DMA((n,))],
        compiler_params=pltpu.CompilerParams(has_side_effects=True),
    )(*grads)


def _pair_sum(g, got, c_arr, name):
    n_sh, R, C = g.shape
    rows = R // 2

    def body(c_ref, g_ref, r_ref, f_ref, b_ref):
        t = g_ref[...] + r_ref[...]
        f_ref[...] = t
        b_ref[...] = t.astype(BF16)

    blk = pl.BlockSpec((1, rows, C), lambda i, c_ref: (i, 0, 0))
    return pl.pallas_call(
        body, name=name,
        grid_spec=pltpu.PrefetchScalarGridSpec(
            num_scalar_prefetch=1, grid=(n_sh,),
            in_specs=[pl.BlockSpec((1, rows, C), lambda i, c_ref: (i, c_ref[0], 0)), blk],
            out_specs=[blk, blk]),
        out_shape=[jax.ShapeDtypeStruct((n_sh, rows, C), F32), jax.ShapeDtypeStruct((n_sh, rows, C), BF16)],
        compiler_params=_params(("parallel",)),
    )(c_arr, g, got)


def _shard_sum(sums_f32, recv, shard_arr, c_arr, name):
    _, rows, C = sums_f32.shape

    def body(s_ref, c_ref, o_ref, r_ref, t_ref):
        t_ref[...] = ((o_ref[0] + r_ref[0].astype(F32)) + r_ref[1].astype(F32)) + r_ref[2].astype(F32)

    return pl.pallas_call(
        body, name=name,
        grid_spec=pltpu.PrefetchScalarGridSpec(
            num_scalar_prefetch=2, grid=(1,),
            in_specs=[pl.BlockSpec((1, rows, C), lambda i, s_ref, c_ref: (s_ref[0], 0, 0)),
                      pl.BlockSpec((3, rows, C), lambda i, s_ref, c_ref: (0, 0, 0))],
            out_specs=pl.BlockSpec((rows, C), lambda i, s_ref, c_ref: (c_ref[0], 0))),
        out_shape=jax.ShapeDtypeStruct((2 * rows, C), F32),
        compiler_params=_params(("arbitrary",)),
    )(shard_arr, c_arr, sums_f32, recv)


def _join_halves(bufs):
    n = len(bufs)

    def body(*refs):
        outs, send_sem, recv_sem = refs[n:2 * n], refs[2 * n], refs[2 * n + 1]
        x, y, c, _ = _position()
        copies = []
        for a in range(n):
            rows = outs[a].shape[0] // 2
            mine = outs[a].at[pl.ds(c * rows, rows), :]
            copies.append(_remote(mine, mine, send_sem.at[a], recv_sem.at[a], (x, y, 1 - c)))
        for cp in copies:
            cp.start()
        for a, cp in enumerate(copies):
            cp.wait_send()
            rows = outs[a].shape[0] // 2
            theirs = outs[a].at[pl.ds((1 - c) * rows, rows), :]
            _remote(theirs, theirs, send_sem.at[a], recv_sem.at[a], (x, y, 1 - c)).wait_recv()

    return pl.pallas_call(
        body, name="join_grad_halves",
        in_specs=[ANY] * n, out_specs=[ANY] * n,
        out_shape=[jax.ShapeDtypeStruct(b.shape, F32) for b in bufs],
        input_output_aliases={i: i for i in range(n)},
        scratch_shapes=[pltpu.SemaphoreType.DMA((n,)), pltpu.SemaphoreType.DMA((n,))],
        compiler_params=pltpu.CompilerParams(has_side_effects=True),
    )(*bufs)


def _small_sum(parts, tag):
    _, R, C = parts.shape

    def body(p_ref, o_ref):
        t = p_ref[0]
        for k in range(1, 8):
            t = t + p_ref[k]
        o_ref[...] = t

    return pl.pallas_call(
        body, name="small_grad_sum_" + tag, grid=(1,),
        in_specs=[pl.BlockSpec((8, R, C), lambda i: (0, 0, 0))], out_specs=pl.BlockSpec((R, C), lambda i: (0, 0)),
        out_shape=jax.ShapeDtypeStruct((R, C), F32), compiler_params=_params(("arbitrary",)),
    )(parts)


def _adamw_math(w, g, m, v):
    m = ADAM_B1 * m + (1.0 - ADAM_B1) * g
    v = ADAM_B2 * v + (1.0 - ADAM_B2) * (g * g)
    m_hat = m / (1.0 - ADAM_B1 ** ADAM_STEP)
    v_hat = v / (1.0 - ADAM_B2 ** ADAM_STEP)
    delta = -ADAM_LR * (m_hat / (jnp.sqrt(v_hat) + ADAM_EPS) + ADAM_WD * w)
    return delta, m, v


def _adamw_big(ws, gs, ms, vs, name, cargo=()):
    n = len(ws)
    nc = len(cargo)
    kinds = [kind for kind, _ in cargo]

    def body(*refs):
        ins, outs = refs[:4 * n], refs[4 * n + nc:7 * n + nc]
        cargo_refs = (kinds, refs[4 * n:4 * n + nc], refs[7 * n + nc:7 * n + 2 * nc], refs[7 * n + 2 * nc:])
        if nc:
            _cargo_start(*cargo_refs, pl.program_id(0) == 0)
        for a in range(n):
            w, g, m, v = (ins[k * n + a][...] for k in range(4))
            outs[a][...], outs[n + a][...], outs[2 * n + a][...] = _adamw_math(w, g, m, v)
        if nc:
            _cargo_finish(*cargo_refs, pl.program_id(0) == 3)

    blks = [pl.BlockSpec((w.shape[0] // 4, w.shape[1]), lambda i: (i, 0)) for w in ws]
    arrays, cargo_specs, shapes, aliases, sems = _cargo_call(cargo, 4 * n, 3 * n)
    out = pl.pallas_call(
        body, name=name, grid=(4,), in_specs=blks * 4 + cargo_specs, out_specs=blks * 3 + cargo_specs,
        out_shape=[jax.ShapeDtypeStruct(w.shape, F32) for w in ws] * 3 + shapes,
        input_output_aliases=aliases, scratch_shapes=sems,
        compiler_params=_params(("arbitrary",)),
    )(*ws, *gs, *ms, *vs, *arrays)
    return (out[:n], out[n:2 * n], out[2 * n:3 * n]), out[3 * n:]


def _adamw_small(ws, gs, ms, vs):
    n = len(ws)

    def body(*refs):
        for a in range(n):
            w, g, m, v = (refs[k * n + a][...] for k in range(4))
            d, nm, nv = _adamw_math(w, g, m, v)
            refs[4 * n + a][...] = d
            refs[5 * n + a][...] = nm
            refs[6 * n + a][...] = nv

    shapes = [jax.ShapeDtypeStruct(w.shape, F32) for w in ws]
    out = pl.pallas_call(body, name="adamw_small", out_shape=shapes * 3)(*ws, *gs, *ms, *vs)
    return out[:n], out[n:2 * n], out[2 * n:]


BIG = ("w_in", "w_out", "w_up", "w_down")
SMALL = ("g_mix_pre", "pool_w", "pool_scale", "g_mix_post", "g_ffn_pre", "conv_b", "g_ffn_post", "conv_w")
ORDER = ("g_mix_pre", "w_in", "pool_w", "pool_scale", "w_out", "g_mix_post", "g_ffn_pre", "w_up", "conv_w", "conv_b",
         "w_down", "g_ffn_post")


def kernel(x, g_mix_pre, w_in, pool_w, pool_scale, w_out, g_mix_post, g_ffn_pre, w_up, conv_w, conv_b, w_down, g_ffn_post, loss_target, m_g_mix_pre, m_w_in, m_pool_w, m_pool_scale, m_w_out, m_g_mix_post, m_g_ffn_pre, m_w_up, m_conv_w, m_conv_b, m_w_down, m_g_ffn_post, v_g_mix_pre, v_w_in, v_pool_w, v_pool_scale, v_w_out, v_g_mix_post, v_g_ffn_pre, v_w_up, v_conv_w, v_conv_b, v_w_down, v_g_ffn_post):
    args = dict(locals())
    W = {n: args[n][0] for n in ORDER}
    M = {n: args["m_" + n][0] for n in ORDER}
    V = {n: args["v_" + n][0] for n in ORDER}
    for d in (W, M, V):
        d["pool_w"] = d["pool_w"].reshape(-1, POOL_GROUP)
        for n in ("g_mix_pre", "pool_scale", "g_mix_post", "g_ffn_pre", "conv_b", "g_ffn_post"):
            d[n] = d[n].reshape(1, -1)
    CW = W["w_up"].shape[1]
    c_arr = lax.axis_index("c").astype(jnp.int32).reshape(1)
    shard = 2 * lax.axis_index("x") + lax.axis_index("y")
    shard_arr = shard.astype(jnp.int32).reshape(1)
    device = 2 * shard + lax.axis_index("c")

    conv_w_slots = lax.dynamic_update_index_in_dim(jnp.zeros((N_SHARD,) + W["conv_w"].shape, F32), W["conv_w"], shard, 0)
    slots = {n: _cast_bf16(W[n], shard_arr, "cast_" + n) for n in BIG}
    w_in_g, conv_w_g = _gather_weights([slots["w_in"], conv_w_slots])
    conv_w_full = conv_w_g.transpose(1, 0, 2).reshape(CONV_WIDTH, 1, N_SHARD * CW)

    loss, grad_x, G = _local_step(
        x[0], loss_target[0], W["g_mix_pre"], w_in_g, W["pool_w"].reshape(-1, POOL_GROUP, POOL_GROUP), W["pool_scale"],
        slots["w_out"], W["g_mix_post"], W["g_ffn_pre"], slots["w_up"], conv_w_full, W["conv_b"],
        slots["w_down"], W["g_ffn_post"], (c_arr, device, shard_arr))

    late, ffn = ("w_in", "w_out"), ("w_up", "w_down")
    from_sibling = _swap_halves([G[n] for n in late], "mix")
    sums = {n: _pair_sum(G[n], r, c_arr, "pair_sum_" + n) for n, r in zip(late, from_sibling)}
    loss_rows = jnp.pad(loss, ((0, 7), (0, LANES - 1)))
    small = jnp.concatenate([_pack_small(G, SMALL_LATE), loss_rows], axis=0)
    small_slots = lax.dynamic_update_index_in_dim(jnp.zeros((8,) + small.shape, F32), small, device, 0)
    pick = lambda d, names: [d[n] for n in names]
    delta, new_m, new_v = {}, {}, {}
    updates, landed = _adamw_big(pick(W, ffn), pick(G, ffn), pick(M, ffn), pick(V, ffn), "adamw_ffn",
                                 [("scatter", sums[n][1]) for n in late] + [("everyone", small_slots)])
    halves = [_shard_sum(sums[n][0], r, shard_arr, c_arr, "shard_sum_" + n) for n, r in zip(late, landed[:2])]
    full = dict(zip(late, _join_halves(halves)))
    full.update({n: G[n] for n in ffn})
    full.update(_unpack_small(_small_sum(G["small_early"], "early"), SMALL_EARLY, W, shard))
    late_total = _small_sum(landed[2], "late")
    full.update(_unpack_small(late_total, SMALL_LATE, W, shard))
    loss = late_total[-8, 0]

    for names, (ds, nms, nvs) in ((ffn, updates),
                                  (late, _adamw_big(pick(W, late), pick(full, late), pick(M, late), pick(V, late), "adamw_mix")[0]),
                                  (SMALL, _adamw_small(pick(W, SMALL), pick(full, SMALL), pick(M, SMALL), pick(V, SMALL)))):
        for n, d, nm, nv in zip(names, ds, nms, nvs):
            delta[n], new_m[n], new_v[n] = d, nm, nv

    shaped = lambda d: [d[n].reshape(args[n].shape) for n in ORDER]
    return (loss, grad_x[None], *shaped(full), *shaped(delta), *shaped(new_m), *shaped(new_v))
```

```python
import functools

import jax
import jax.numpy as jnp
from jax import lax
from jax.experimental import pallas as pl
from jax.experimental.pallas import tpu as pltpu

F32 = jnp.float32
BF16 = jnp.bfloat16

RMS_EPS = 1e-6
NEG_INF = -1e30
N_HEADS = 8
HEAD_DIM = 64
ATTN_WIDTH = N_HEADS * HEAD_DIM
ATTN_SCALE = HEAD_DIM ** -0.5
ATTN_BLOCK = 128
DILATIONS = (1, 4, 16)
RESIDUES_PER_STEP = 4
CONSECUTIVE_BLOCKS = 4
POOL_WINDOWS = (2, 4, 8, 16)
POOL_GROUP = 128
POOL_WIDTH = POOL_GROUP * len(POOL_WINDOWS)
POOL_HALO = 16
CONV_WIDTH = 3
CONV_HALO = 8
N_SHARD = 4
LANES = 128
STAT_LANES = 16
STAT_WIDTH = N_HEADS * STAT_LANES

ADAM_LR = 0.001
ADAM_B1 = 0.9
ADAM_B2 = 0.999
ADAM_EPS = 1e-08
ADAM_WD = 0.01
ADAM_STEP = 10

VMEM_LIMIT = 60 * 1024 * 1024
MESH = pl.DeviceIdType.MESH
NT = (((1,), (1,)), ((), ()))
TN = (((0,), (0,)), ((), ()))


def _params(sem, vmem=None):
    return pltpu.CompilerParams(dimension_semantics=sem, vmem_limit_bytes=vmem)


def _const_spec(shape):
    zeros = (0,) * len(shape)
    return pl.BlockSpec(shape, lambda *_: zeros, pipeline_mode=pl.Buffered(1))


def _dot(a, b):
    return jnp.dot(a, b, preferred_element_type=F32)


def _dot_nt(a, b):
    return lax.dot_general(a, b, NT, preferred_element_type=F32)


def _dot_tn(a, b):
    return lax.dot_general(a, b, TN, preferred_element_type=F32)


def _rms_stats(x):
    r = lax.rsqrt(jnp.mean(x * x, axis=-1, keepdims=True) + RMS_EPS)
    return x * r, r


def _rms_bwd(dy, n, r, g):
    dg = jnp.sum(dy * n, axis=0, keepdims=True)
    dn = dy * g
    dx = r * (dn - n * jnp.mean(dn * n, axis=-1, keepdims=True))
    return dx, dg


def _gelu_tanh(g):
    k = 0.7978845608028654
    kc = k * 0.044715
    g2 = g * g
    t = jnp.tanh(g * (k + kc * g2))
    h = 0.5 * t + 0.5
    dh = (0.5 - 0.5 * (t * t)) * (k + (3.0 * kc) * g2)
    return g * h, h + g * dh


def _residue_shape(S, d, dtype, width=ATTN_WIDTH):
    return jax.ShapeDtypeStruct((S // d, d * width), dtype)


def _residue_spec(TM, d, width=ATTN_WIDTH):
    return pl.BlockSpec((TM // d, d * width), lambda i: (i, 0))


def _token_scratch(TM, width=ATTN_WIDTH):
    return [pltpu.VMEM((TM, LANES), F32)] * (width // LANES)


def _head_stat_matrix(pick_first_lane):
    r = lax.broadcasted_iota(jnp.int32, (ATTN_WIDTH, STAT_WIDTH), 0)
    c = lax.broadcasted_iota(jnp.int32, (ATTN_WIDTH, STAT_WIDTH), 1) // STAT_LANES
    return ((r == c * HEAD_DIM) if pick_first_lane else (r // HEAD_DIM == c)).astype(BF16)


def _bf16_pieces(x, n):
    pieces = []
    for _ in range(n):
        p = x.astype(BF16)
        pieces.append(p)
        x = x - p.astype(F32)
    return pieces


def _put_tokens(dst_s, val):
    for cb, chunk in enumerate(dst_s):
        chunk[...] = val[:, cb * LANES:(cb + 1) * LANES]


def _get_tokens(src_s):
    return jnp.concatenate([chunk[...] for chunk in src_s], axis=1)


def _to_residue(val, src_s, out_ref, d, dtype):
    if d == 1:
        out_ref[...] = val.astype(dtype)
        return
    rows = src_s[0].shape[0]
    for r in range(d):
        for cb, chunk in enumerate(src_s):
            col = (r * len(src_s) + cb) * LANES
            out_ref[:, col:col + LANES] = chunk[pl.ds(r, rows // d, stride=d), :].astype(dtype)


def _from_residue(in_ref, dst_s, d):
    if d == 1:
        return in_ref[...].astype(F32)
    rows = dst_s[0].shape[0]
    for r in range(d):
        for cb, chunk in enumerate(dst_s):
            col = (r * len(dst_s) + cb) * LANES
            chunk[pl.ds(r, rows // d, stride=d), :] = in_ref[:, col:col + LANES].astype(F32)
    return _get_tokens(dst_s)


def _mix_in_fwd(x, g_pre, w_in, cargo=()):
    S, D = x.shape
    TM = 512
    nc = len(cargo)
    kinds = [kind for kind, _ in cargo]
    n_chunks = ATTN_WIDTH // LANES

    def body(x_ref, g_ref, w_ref, *refs):
        cargo_in, refs = refs[:nc], refs[nc:]
        qkv_refs, p_ref, h_ref = refs[:9], refs[9], refs[10]
        t_s = refs[11 + nc:11 + nc + n_chunks]
        cargo_refs = (kinds, cargo_in, refs[11:11 + nc], refs[11 + nc + n_chunks:])
        if nc:
            _cargo_start(*cargo_refs, pl.program_id(0) == 0)
        n, _ = _rms_stats(x_ref[...])
        hb = (n * g_ref[...]).astype(BF16)
        h_ref[...] = hb
        for a in range(3):
            res = _dot(hb, w_ref[a])
            if a == 0:
                res = res * ATTN_SCALE
            _put_tokens(t_s, res)
            for i, d in enumerate(DILATIONS):
                _to_residue(res, t_s, qkv_refs[3 * i + a], d, BF16)
        p_ref[...] = _dot(hb, w_ref[3]).astype(BF16)
        if nc:
            _cargo_finish(*cargo_refs, pl.program_id(0) == S // TM - 1)

    row = lambda w: pl.BlockSpec((TM, w), lambda i: (i, 0))
    arrays, cargo_specs, shapes, aliases, sems = _cargo_call(cargo, 3, 11)
    out = pl.pallas_call(
        body, name="mix_in_fwd", grid=(S // TM,),
        in_specs=[row(D), _const_spec((1, D)), _const_spec(w_in.shape)] + cargo_specs,
        out_specs=[_residue_spec(TM, d) for d in DILATIONS for _ in range(3)] + [row(POOL_WIDTH), row(D)] + cargo_specs,
        out_shape=[_residue_shape(S, d, BF16) for d in DILATIONS for _ in range(3)]
        + [jax.ShapeDtypeStruct((S, POOL_WIDTH), BF16), jax.ShapeDtypeStruct((S, D), BF16)] + shapes,
        input_output_aliases=aliases,
        scratch_shapes=_token_scratch(TM) + sems,
        compiler_params=_params(("arbitrary",), VMEM_LIMIT),
    )(x, g_pre, w_in, *arrays)
    return [out[0:3], out[3:6], out[6:9]], out[9], out[10], out[11:]


def _band_mask(n):
    qi = lax.broadcasted_iota(jnp.int32, (ATTN_BLOCK, 2 * ATTN_BLOCK), 0)
    ki = lax.broadcasted_iota(jnp.int32, (ATTN_BLOCK, 2 * ATTN_BLOCK), 1)
    dist = qi + ATTN_BLOCK - ki
    return (dist >= 0) & (dist <= ATTN_BLOCK) & ((ki >= ATTN_BLOCK) | (n > 0))


def _first_head_lanes():
    return lax.broadcasted_iota(jnp.int32, (1, LANES), 1) < HEAD_DIM


def _stack_heads(pair, first):
    zero = jnp.zeros_like(pair)
    return jnp.concatenate([jnp.where(first, pair, zero), jnp.where(first, zero, pair)], axis=0)


def _unstack_heads(stacked, first):
    return jnp.where(first, stacked[:ATTN_BLOCK], stacked[ATTN_BLOCK:])


CARGO_COPIES = {"ici": 3, "d2d": 3, "scatter": 3, "swap": 1, "everyone": 7, "join": 1}
CARGO_IN_PLACE = ("ici", "d2d", "everyone", "join")


def _cargo_copies(kinds, ins, outs, send_sems, recv_sems, want_recvs=True):
    x, y, c, chips = _position()
    s = 2 * x + y
    me = 2 * s + c
    sibling = (x, y, 1 - c)
    sends, recvs = [], []

    def add(k, src, dst, landing, to):
        sends.append(_remote(src, dst, send_sems.at[k], recv_sems.at[k], to))
        if want_recvs:
            recvs.append(_remote(landing, landing, send_sems.at[k], recv_sems.at[k], to))

    k0 = 0
    for a, kind in enumerate(kinds):
        if kind == "swap":
            rows = ins[a].shape[1] // 2
            add(k0, ins[a].at[:, pl.ds((1 - c) * rows, rows), :], outs[a], outs[a], sibling)
        elif kind == "join":
            rows = outs[a].shape[0] // 2
            mine = outs[a].at[pl.ds(c * rows, rows), :]
            add(k0, mine, mine, outs[a].at[pl.ds((1 - c) * rows, rows), :], sibling)
        elif kind == "everyone":
            for m in range(1, 8):
                peer = (x ^ (m >> 2), y ^ ((m >> 1) & 1), c ^ (m & 1))
                add(k0 + m - 1, outs[a].at[me], outs[a].at[me], outs[a].at[4 * peer[0] + 2 * peer[1] + peer[2]], peer)
        else:
            for j, (px, py) in enumerate(chips):
                sj = 2 * px + py
                if kind == "scatter":
                    add(k0 + j, ins[a].at[sj], outs[a].at[j], outs[a].at[j], (px, py, c))
                    continue
                buf = outs[a]
                rows = buf.shape[1] // 2
                half = lambda shard, h: buf.at[shard, pl.ds(h * rows, rows), :]
                if kind == "ici":
                    add(k0 + j, half(s, c), half(s, c), half(sj, c), (px, py, c))
                else:
                    add(k0 + j, half(sj, c), half(sj, c), half(sj, 1 - c), sibling)
        k0 += CARGO_COPIES[kind]
    return sends, recvs


def _cargo_start(kinds, ins, outs, sems, first_step):
    @pl.when(first_step)
    def _():
        for cp in _cargo_copies(kinds, ins, outs, *sems, want_recvs=False)[0]:
            cp.start()


def _cargo_finish(kinds, ins, outs, sems, last_step):
    @pl.when(last_step)
    def _():
        sends, recvs = _cargo_copies(kinds, ins, outs, *sems)
        for cp in sends:
            cp.wait_send()
        for cp in recvs:
            cp.wait_recv()


def _cargo_call(cargo, n_in, n_out):
    arrays = [a for _, a in cargo]
    shapes = []
    for kind, a in cargo:
        if kind == "scatter":
            shape = (3,) + a.shape[1:]
        elif kind == "swap":
            shape = (a.shape[0], a.shape[1] // 2, a.shape[2])
        else:
            shape = a.shape
        shapes.append(jax.ShapeDtypeStruct(shape, a.dtype))
    aliases = {n_in + i: n_out + i for i, (kind, _) in enumerate(cargo) if kind in CARGO_IN_PLACE}
    n_sems = sum(CARGO_COPIES[kind] for kind, _ in cargo)
    sems = [pltpu.SemaphoreType.DMA((n_sems,))] * 2 if cargo else []
    return arrays, [ANY] * len(cargo), shapes, aliases, sems


def _attn_fwd(q, k, v, d, cargo=()):
    L = q.shape[0]
    group = min(d, RESIDUES_PER_STEP)
    width = group * ATTN_WIDTH
    qb = RESIDUES_PER_STEP // group
    steps = L // (qb * ATTN_BLOCK)
    nc = len(cargo)
    kinds = [kind for kind, _ in cargo]

    def body(*refs):
        q_ref, kp_ref, kc_ref, vp_ref, vc_ref = refs[:5]
        o_ref, lse_ref = refs[5 + nc:7 + nc]
        cargo_refs = (kinds, refs[5:5 + nc], refs[7 + nc:7 + 2 * nc], refs[7 + 2 * nc:])
        r, n = pl.program_id(0), pl.program_id(1)
        if nc:
            _cargo_start(*cargo_refs, (r == 0) & (n == 0))
        first = _first_head_lanes()
        for sub in range(qb):
            rows = slice(sub * ATTN_BLOCK, (sub + 1) * ATTN_BLOCK)
            valid = _band_mask(n if sub == 0 else 1)
            valid2 = jnp.concatenate([valid, valid], axis=0)
            for hp in range(width // LANES):
                sl = slice(hp * LANES, (hp + 1) * LANES)
                if sub == 0:
                    kk = jnp.concatenate([kp_ref[:, sl], kc_ref[rows, sl]], axis=0)
                    vv = jnp.concatenate([vp_ref[:, sl], vc_ref[rows, sl]], axis=0)
                else:
                    keys = slice((sub - 1) * ATTN_BLOCK, (sub + 1) * ATTN_BLOCK)
                    kk, vv = kc_ref[keys, sl], vc_ref[keys, sl]
                s = jnp.where(valid2, _dot_nt(_stack_heads(q_ref[rows, sl], first), kk), NEG_INF)
                m = jnp.max(s, axis=-1, keepdims=True)
                p = jnp.exp(s - m)
                den = jnp.sum(p, axis=-1, keepdims=True)
                o_ref[rows, sl] = _unstack_heads(_dot(p.astype(BF16), vv) / den, first).astype(BF16)
                lse_ref[rows, sl] = _unstack_heads(m + jnp.log(den), first)
        if nc:
            _cargo_finish(*cargo_refs, (r == d // group - 1) & (n == steps - 1))

    cur = pl.BlockSpec((qb * ATTN_BLOCK, width), lambda r, n: (n, r))
    prev = pl.BlockSpec((ATTN_BLOCK, width), lambda r, n: (jnp.maximum(n * qb - 1, 0), r))
    arrays, specs, shapes, aliases, sems = _cargo_call(cargo, 5, 2)
    out = pl.pallas_call(
        body, name=f"attn_fwd_d{d}", grid=(d // group, steps),
        in_specs=[cur, prev, cur, prev, cur] + specs,
        out_specs=[cur, cur] + specs,
        out_shape=[jax.ShapeDtypeStruct((L, d * ATTN_WIDTH), BF16), jax.ShapeDtypeStruct((L, d * ATTN_WIDTH), F32)] + shapes,
        input_output_aliases=aliases, scratch_shapes=sems,
        compiler_params=_params(("arbitrary", "arbitrary")),
    )(q, k, k, v, v, *arrays)
    return out[0], out[1], out[2:]


def _attn_mix(outs, lses):
    S = outs[0].shape[0]
    TM = 512
    n = len(DILATIONS)

    def body(*refs):
        o_refs, l_refs, attn_ref, lse_refs = refs[:n], refs[n:2 * n], refs[2 * n], refs[2 * n + 1:3 * n + 1]
        t_s, c_s = refs[3 * n + 1:-1], refs[-1:]
        os = [_from_residue(o_refs[i], t_s, d) for i, d in enumerate(DILATIONS)]
        ls = [_from_residue(l_refs[i], t_s, d) for i, d in enumerate(DILATIONS)]
        m = jnp.maximum(jnp.maximum(ls[0], ls[1]), ls[2])
        es = [jnp.exp(l - m) for l in ls]
        den = es[0] + es[1] + es[2]
        attn_ref[...] = ((es[0] * os[0] + es[1] * os[1] + es[2] * os[2]) / den).astype(BF16)
        pick = _head_stat_matrix(pick_first_lane=True)
        lse = sum(_dot(p, pick) for p in _bf16_pieces(m + jnp.log(den), 3))
        _put_tokens(c_s, lse)
        for i, d in enumerate(DILATIONS):
            _to_residue(lse, c_s, lse_refs[i], d, F32)

    specs = [_residue_spec(TM, d) for d in DILATIONS]
    out = pl.pallas_call(
        body, name="attn_mix", grid=(S // TM,),
        in_specs=specs * 2, out_specs=[specs[0]] + [_residue_spec(TM, d, STAT_WIDTH) for d in DILATIONS],
        out_shape=[jax.ShapeDtypeStruct((S, ATTN_WIDTH), BF16)]
        + [_residue_shape(S, d, F32, STAT_WIDTH) for d in DILATIONS],
        scratch_shapes=_token_scratch(TM) + _token_scratch(TM, STAT_WIDTH),
        compiler_params=_params(("parallel",)),
    )(*outs, *lses)
    return out[0], out[1:]


def _pool_counts(first_row, rows, w):
    t = first_row + lax.broadcasted_iota(jnp.int32, (rows, 1), 0)
    return jnp.minimum(t + 1, w).astype(F32)


def _trailing_sums(xe, w):
    s, k = xe, 1
    while k < w:
        s = s + pltpu.roll(s, k, 0)
        k *= 2
    return s


def _leading_sums(xe, w):
    rows = xe.shape[0]
    s, k = xe, 1
    while k < w:
        s = s + pltpu.roll(s, rows - k, 0)
        k *= 2
    return s


def _pooled_groups(halo, cur, first_row):
    TM = cur.shape[0]
    xe = jnp.concatenate([halo, cur], axis=0)
    out = []
    for g, w in enumerate(POOL_WINDOWS):
        a = xe[:, g * POOL_GROUP:(g + 1) * POOL_GROUP]
        s = _trailing_sums(a, w)[POOL_HALO:]
        out.append(s / _pool_counts(first_row, TM, w) - a[POOL_HALO:])
    return out


def _pool_fwd(pool_in, pool_w, pool_scale):
    S = pool_in.shape[0]
    TM = 512
    HB = TM // POOL_HALO

    def body(cur_ref, halo_ref, w_ref, sc_ref, y_ref):
        i = pl.program_id(0)
        halo = jnp.where(i > 0, halo_ref[...].astype(F32), 0.0)
        pooled = _pooled_groups(halo, cur_ref[...].astype(F32), i * TM)
        for g in range(len(POOL_WINDOWS)):
            sl = slice(g * POOL_GROUP, (g + 1) * POOL_GROUP)
            y = _dot(pooled[g].astype(BF16), w_ref[g].astype(BF16)) * sc_ref[:, sl]
            y_ref[:, sl] = y.astype(BF16)

    return pl.pallas_call(
        body, name="pool_fwd", grid=(S // TM,),
        in_specs=[pl.BlockSpec((TM, POOL_WIDTH), lambda i: (i, 0)),
                  pl.BlockSpec((POOL_HALO, POOL_WIDTH), lambda i: (jnp.maximum(i * HB - 1, 0), 0)),
                  _const_spec(pool_w.shape), _const_spec((1, POOL_WIDTH))],
        out_specs=pl.BlockSpec((TM, POOL_WIDTH), lambda i: (i, 0)),
        out_shape=jax.ShapeDtypeStruct((S, POOL_WIDTH), BF16),
        compiler_params=_params(("parallel",)),
    )(pool_in, pool_in, pool_w, pool_scale)


def _mix_out_fwd(attn, pool, w_out, x, g_post, g_ffn_pre):
    S, D = x.shape
    TM = 512

    def body(a_ref, p_ref, w_ref, x_ref, gp_ref, gf_ref, mixed_ref, x1_ref, h2_ref, cat_ref):
        ab = a_ref[...]
        cat_ref[:, :ATTN_WIDTH] = ab
        cat_ref[:, ATTN_WIDTH:] = p_ref[...]
        mixed = _dot(ab, w_ref[:ATTN_WIDTH, :]) + _dot(p_ref[...], w_ref[ATTN_WIDTH:, :])
        mixed_ref[...] = mixed.astype(BF16)
        n, _ = _rms_stats(mixed)
        x1 = x_ref[...] + n * gp_ref[...]
        x1_ref[...] = x1
        n2, _ = _rms_stats(x1)
        h2_ref[...] = (n2 * gf_ref[...]).astype(BF16)

    row = lambda w: pl.BlockSpec((TM, w), lambda i: (i, 0))
    return pl.pallas_call(
        body, name="mix_out_fwd", grid=(S // TM,),
        in_specs=[row(ATTN_WIDTH), row(POOL_WIDTH), _const_spec(w_out.shape), row(D),
                  _const_spec((1, D)), _const_spec((1, D))],
        out_specs=[row(D), row(D), row(D), row(D)],
        out_shape=[jax.ShapeDtypeStruct((S, D), BF16), jax.ShapeDtypeStruct((S, D), F32),
                   jax.ShapeDtypeStruct((S, D), BF16), jax.ShapeDtypeStruct((S, D), BF16)],
        compiler_params=_params(("parallel",), VMEM_LIMIT),
    )(attn, pool, w_out, x, g_post, g_ffn_pre)


def _ffn_fwd(h2, x1, target, w_up, w_down, conv_w, conv_b, g_post):
    S, D = x1.shape
    CW = w_up.shape[2]
    FF = 2 * CW
    TM = 256
    piece = 4 * LANES
    pieces = [(lo, min(lo + piece, CW)) for lo in range(0, CW, piece)]

    def body(h2_ref, x1_ref, t_ref, wu_ref, wd_ref, cw_ref, cb_ref, g_ref,
             yv_ref, dy_ref, df_ref, dc_ref, loss_ref, dg_ref, dcb_ref, dcw_ref,
             ue_s, dgate_s, dval_s):
        i = pl.program_id(0)

        @pl.when(i == 0)
        def _():
            loss_ref[...] = jnp.zeros_like(loss_ref)
            dg_ref[...] = jnp.zeros_like(dg_ref)
            dcb_ref[...] = jnp.zeros_like(dcb_ref)
            dcw_ref[...] = jnp.zeros_like(dcw_ref)
            ue_s[0:CONV_HALO, :] = jnp.zeros((CONV_HALO, 2 * FF), F32)

        @pl.when(i > 0)
        def _():
            ue_s[0:CONV_HALO, :] = ue_s[TM:TM + CONV_HALO, :]

        def shifted(cols, k):
            return pltpu.roll(ue_s[:, cols], k, 0)[CONV_HALO:]

        def conv(cols):
            return (cb_ref[:, cols] + cw_ref[2, :, cols] * ue_s[CONV_HALO:, cols]
                    + cw_ref[1, :, cols] * shifted(cols, 1) + cw_ref[0, :, cols] * shifted(cols, 2))

        hb = h2_ref[...]
        f = jnp.zeros((TM, D), F32)
        for j in range(2):
            jc = slice(j * CW, (j + 1) * CW)
            for half in range(2):
                blk = 2 * half + j
                cols = slice(blk * CW, (blk + 1) * CW)
                ue_s[CONV_HALO:, cols] = _dot(hb, wu_ref[blk])
            for lo, hi in pieces:
                pc = slice(j * CW + lo, j * CW + hi)
                gelu, dgelu = _gelu_tanh(conv(pc).astype(BF16))
                val = conv(slice(FF + j * CW + lo, FF + j * CW + hi)).astype(BF16)
                dgate_s[:, pc] = val * dgelu
                dval_s[:, pc] = gelu
                yv_ref[:, pc] = gelu * val
            f = f + _dot(yv_ref[:, jc], wd_ref[jc, :])

        n, r = _rms_stats(f)
        err = x1_ref[...] + n * g_ref[...] - t_ref[...]
        loss_ref[...] += 0.5 * jnp.sum(jnp.mean(err * err, axis=-1, keepdims=True), axis=0, keepdims=True)
        dy = err / D
        dy_ref[...] = dy
        df, dg = _rms_bwd(dy, n, r, g_ref[...])
        dg_ref[...] += dg
        dfb = df.astype(BF16)
        df_ref[...] = dfb

        for j in range(2):
            jc = slice(j * CW, (j + 1) * CW)
            dyv = _dot_nt(dfb, wd_ref[jc, :])
            for lo, hi in pieces:
                pc = slice(j * CW + lo, j * CW + hi)
                for half, scale_s in ((0, dgate_s), (1, dval_s)):
                    cols = slice(half * FF + j * CW + lo, half * FF + j * CW + hi)
                    dcv = dyv[:, lo:hi] * scale_s[:, pc].astype(F32)
                    dc_ref[:, cols] = dcv.astype(BF16)
                    dcb_ref[:, cols] += jnp.sum(dcv, axis=0, keepdims=True)
                    dcw_ref[2, :, cols] += jnp.sum(dcv * ue_s[CONV_HALO:, cols], axis=0, keepdims=True)
                    dcw_ref[1, :, cols] += jnp.sum(dcv * shifted(cols, 1), axis=0, keepdims=True)
                    dcw_ref[0, :, cols] += jnp.sum(dcv * shifted(cols, 2), axis=0, keepdims=True)

    row = lambda w: pl.BlockSpec((TM, w), lambda i: (i, 0))
    acc = lambda shape: pl.BlockSpec(shape, lambda i: (0,) * len(shape))
    return pl.pallas_call(
        body, name="ffn_fwd", grid=(S // TM,),
        in_specs=[row(D), row(D), row(D), _const_spec(w_up.shape), _const_spec(w_down.shape),
                  _const_spec(conv_w.shape), _const_spec((1, 2 * FF)), _const_spec((1, D))],
        out_specs=[row(FF), row(D), row(D), row(2 * FF),
                   acc((1, 1)), acc((1, D)), acc((1, 2 * FF)), acc((CONV_WIDTH, 1, 2 * FF))],
        out_shape=[jax.ShapeDtypeStruct((S, FF), BF16),
                   jax.ShapeDtypeStruct((S, D), F32), jax.ShapeDtypeStruct((S, D), BF16),
                   jax.ShapeDtypeStruct((S, 2 * FF), BF16),
                   jax.ShapeDtypeStruct((1, 1), F32), jax.ShapeDtypeStruct((1, D), F32),
                   jax.ShapeDtypeStruct((1, 2 * FF), F32), jax.ShapeDtypeStruct((CONV_WIDTH, 1, 2 * FF), F32)],
        scratch_shapes=[pltpu.VMEM((TM + CONV_HALO, 2 * FF), F32), pltpu.VMEM((TM, FF), BF16),
                        pltpu.VMEM((TM, FF), BF16)],
        compiler_params=_params(("arbitrary",), VMEM_LIMIT),
    )(h2, x1, target, w_up, w_down, conv_w, conv_b, g_post)


def _ffn_bwd(dc, conv_w, w_up, x1, g_ffn_pre, dy):
    S, D = x1.shape
    CW = w_up.shape[2]
    F2 = 4 * CW
    TM = 256
    HB = TM // CONV_HALO
    last = S // CONV_HALO - 1
    n_tiles = S // TM

    def body(dc_ref, halo_ref, cw_ref, wu_ref, x1_ref, g_ref, dy_ref, du_ref, dx1_ref, dg_ref):
        i = pl.program_id(0)

        @pl.when(i == 0)
        def _():
            dg_ref[...] = jnp.zeros_like(dg_ref)

        keep = i < n_tiles - 1
        dh2 = jnp.zeros((TM, D), F32)
        for blk in range(N_SHARD):
            cols = slice(blk * CW, (blk + 1) * CW)
            halo = jnp.where(keep, halo_ref[:, cols].astype(F32), 0.0)
            dce = jnp.concatenate([dc_ref[:, cols].astype(F32), halo], axis=0)
            rows = TM + CONV_HALO
            du = (cw_ref[2, :, cols] * dce[:TM]
                  + cw_ref[1, :, cols] * pltpu.roll(dce, rows - 1, 0)[:TM]
                  + cw_ref[0, :, cols] * pltpu.roll(dce, rows - 2, 0)[:TM])
            dub = du.astype(BF16)
            du_ref[:, cols] = dub
            dh2 = dh2 + _dot_nt(dub, wu_ref[blk])
        n2, r2 = _rms_stats(x1_ref[...])
        dx, dg = _rms_bwd(dh2, n2, r2, g_ref[...])
        dg_ref[...] += dg
        dx1_ref[...] = (dy_ref[...] + dx).astype(BF16)

    row = lambda w: pl.BlockSpec((TM, w), lambda i: (i, 0))
    return pl.pallas_call(
        body, name="ffn_bwd", grid=(S // TM,),
        in_specs=[row(F2), pl.BlockSpec((CONV_HALO, F2), lambda i: (jnp.minimum((i + 1) * HB, last), 0)),
                  _const_spec(conv_w.shape), _const_spec(w_up.shape), row(D), _const_spec((1, D)), row(D)],
        out_specs=[row(F2), row(D), pl.BlockSpec((1, D), lambda i: (0, 0))],
        out_shape=[jax.ShapeDtypeStruct((S, F2), BF16), jax.ShapeDtypeStruct((S, D), BF16),
                   jax.ShapeDtypeStruct((1, D), F32)],
        compiler_params=_params(("arbitrary",), VMEM_LIMIT),
    )(dc, dc, conv_w, w_up, x1, g_ffn_pre, dy)


def _matmul_tn(a, b, n_blocks, name):
    S, M = a.shape
    N = b.shape[1]
    tn = N // n_blocks
    tm = M if M <= 1024 else M // 2
    tk = 2048
    nk = S // tk

    def body(a_ref, b_ref, o_ref):
        @pl.when(pl.program_id(2) == 0)
        def _():
            o_ref[...] = jnp.zeros_like(o_ref)
        o_ref[0] += _dot_tn(a_ref[...], b_ref[...])

    return pl.pallas_call(
        body, name=name, grid=(M // tm, n_blocks, nk),
        in_specs=[pl.BlockSpec((tk, tm), lambda i, j, k: (k, i)), pl.BlockSpec((tk, tn), lambda i, j, k: (k, j))],
        out_specs=pl.BlockSpec((1, tm, tn), lambda i, j, k: (j, i, 0)),
        out_shape=jax.ShapeDtypeStruct((n_blocks, M, tn), F32),
        compiler_params=_params(("parallel", "parallel", "arbitrary"), VMEM_LIMIT),
    )(a, b)


def _mix_out_bwd(dx1, mixed, g_post, w_out, attn, cargo=()):
    S, D = dx1.shape
    TM = 512
    nd = len(DILATIONS)
    nc = len(cargo)
    kinds = [kind for kind, _ in cargo]
    n_chunks = ATTN_WIDTH // LANES

    def body(*refs):
        dx_ref, m_ref, g_ref, w_ref, a_ref = refs[:5]
        dm_ref, dp_ref, dg_ref = refs[5 + nc:8 + nc]
        da_refs, dl_refs = refs[8 + nc:8 + nc + nd], refs[8 + nc + nd:8 + nc + 2 * nd]
        n_out = 8 + nc + 2 * nd
        t_s = refs[n_out + nc:n_out + nc + n_chunks]
        c_s = refs[n_out + nc + n_chunks:n_out + nc + n_chunks + 1]
        cargo_refs = (kinds, refs[5:5 + nc], refs[n_out:n_out + nc], refs[n_out + nc + n_chunks + 1:])
        if nc:
            _cargo_start(*cargo_refs, pl.program_id(0) == 0)

        @pl.when(pl.program_id(0) == 0)
        def _():
            dg_ref[...] = jnp.zeros_like(dg_ref)

        n, r = _rms_stats(m_ref[...].astype(F32))
        dm, dg = _rms_bwd(dx_ref[...].astype(F32), n, r, g_ref[...])
        dg_ref[...] += dg
        dmb = dm.astype(BF16)
        dm_ref[...] = dmb
        da = _dot_nt(dmb, w_ref[:ATTN_WIDTH, :])
        _put_tokens(t_s, da)
        for i, d in enumerate(DILATIONS):
            _to_residue(da, t_s, da_refs[i], d, BF16)
        dp_ref[...] = _dot_nt(dmb, w_ref[ATTN_WIDTH:, :]).astype(BF16)
        gather = _head_stat_matrix(pick_first_lane=False)
        delta = sum(_dot(p, gather) for p in _bf16_pieces(da * a_ref[...].astype(F32), 2))
        _put_tokens(c_s, delta)
        for i, d in enumerate(DILATIONS):
            _to_residue(delta, c_s, dl_refs[i], d, F32)
        if nc:
            _cargo_finish(*cargo_refs, pl.program_id(0) == S // TM - 1)

    row = lambda w: pl.BlockSpec((TM, w), lambda i: (i, 0))
    specs = [_residue_spec(TM, d) for d in DILATIONS]
    arrays, cargo_specs, shapes, aliases, sems = _cargo_call(cargo, 5, 3 + 2 * nd)
    out = pl.pallas_call(
        body, name="mix_out_bwd", grid=(S // TM,),
        in_specs=[row(D), row(D), _const_spec((1, D)), _const_spec(w_out.shape), row(ATTN_WIDTH)] + cargo_specs,
        out_specs=[row(D), row(POOL_WIDTH), pl.BlockSpec((1, D), lambda i: (0, 0))] + specs
        + [_residue_spec(TM, d, STAT_WIDTH) for d in DILATIONS] + cargo_specs,
        out_shape=[jax.ShapeDtypeStruct((S, D), BF16), jax.ShapeDtypeStruct((S, POOL_WIDTH), BF16),
                   jax.ShapeDtypeStruct((1, D), F32)]
        + [_residue_shape(S, d, BF16) for d in DILATIONS]
        + [_residue_shape(S, d, F32, STAT_WIDTH) for d in DILATIONS] + shapes,
        input_output_aliases=aliases,
        scratch_shapes=_token_scratch(TM) + _token_scratch(TM, STAT_WIDTH) + sems,
        compiler_params=_params(("arbitrary",), VMEM_LIMIT),
    )(dx1, mixed, g_post, w_out, attn, *arrays)
    return out[0], out[1], out[2], out[3:3 + nd], out[3 + nd:3 + 2 * nd], out[3 + 2 * nd:]


def _pool_bwd(pool_in, d_pool, pool_w, pool_scale):
    S = pool_in.shape[0]
    TM = 512
    HB = TM // POOL_HALO
    last = S // POOL_HALO - 1
    G = len(POOL_WINDOWS)

    def body(cur_ref, halo_ref, dcur_ref, dnext_ref, w_ref, sc_ref, dxin_ref, dw_ref, dsc_ref):
        i = pl.program_id(0)

        @pl.when(i == 0)
        def _():
            dw_ref[...] = jnp.zeros_like(dw_ref)
            dsc_ref[...] = jnp.zeros_like(dsc_ref)

        halo = jnp.where(i > 0, halo_ref[...].astype(F32), 0.0)
        pooled = _pooled_groups(halo, cur_ref[...].astype(F32), i * TM)
        dnext = jnp.where(i < S // TM - 1, dnext_ref[...].astype(F32), 0.0)
        dye = jnp.concatenate([dcur_ref[...].astype(F32), dnext], axis=0)
        for g, w in enumerate(POOL_WINDOWS):
            sl = slice(g * POOL_GROUP, (g + 1) * POOL_GROUP)
            wg = w_ref[g].astype(BF16)
            pb = pooled[g].astype(BF16)
            dsc_ref[:, sl] += jnp.sum(dye[:TM, sl] * _dot(pb, wg), axis=0, keepdims=True)
            dpre = (dye[:, sl] * sc_ref[:, sl]).astype(BF16)
            dw_ref[g] += _dot_tn(pb, dpre[:TM])
            dpooled = _dot_nt(dpre, wg)
            z = dpooled / _pool_counts(i * TM, TM + POOL_HALO, w)
            dxin_ref[:, sl] = (_leading_sums(z, w)[:TM] - dpooled[:TM]).astype(BF16)

    row = pl.BlockSpec((TM, POOL_WIDTH), lambda i: (i, 0))
    return pl.pallas_call(
        body, name="pool_bwd", grid=(S // TM,),
        in_specs=[row, pl.BlockSpec((POOL_HALO, POOL_WIDTH), lambda i: (jnp.maximum(i * HB - 1, 0), 0)),
                  row, pl.BlockSpec((POOL_HALO, POOL_WIDTH), lambda i: (jnp.minimum((i + 1) * HB, last), 0)),
                  _const_spec(pool_w.shape), _const_spec((1, POOL_WIDTH))],
        out_specs=[row, pl.BlockSpec((G, POOL_GROUP, POOL_GROUP), lambda i: (0, 0, 0)),
                   pl.BlockSpec((1, POOL_WIDTH), lambda i: (0, 0))],
        out_shape=[jax.ShapeDtypeStruct((S, POOL_WIDTH), BF16), jax.ShapeDtypeStruct((G, POOL_GROUP, POOL_GROUP), F32),
                   jax.ShapeDtypeStruct((1, POOL_WIDTH), F32)],
        compiler_params=_params(("arbitrary",)),
    )(pool_in, pool_in, d_pool, d_pool, pool_w, pool_scale)


def _attn_bwd(q, k, v, d_attn, lse, delta, d, cargo=()):
    L = q.shape[0]
    nb = L // ATTN_BLOCK
    group = min(d, RESIDUES_PER_STEP)
    width = group * ATTN_WIDTH
    nc = len(cargo)
    kinds = [kind for kind, _ in cargo]

    def body(*refs):
        q_ref, kp_ref, kc_ref, vp_ref, vc_ref, do_ref, lse_ref, dl_ref = refs[:8]
        dq_ref, dk_ref, dv_ref = refs[8 + nc:11 + nc]
        ck_s, cv_s = refs[11 + 2 * nc:13 + 2 * nc]
        cargo_refs = (kinds, refs[8:8 + nc], refs[11 + nc:11 + 2 * nc], refs[13 + 2 * nc:])
        r, n = pl.program_id(0), pl.program_id(1)
        if nc:
            _cargo_start(*cargo_refs, (r == 0) & (n == 0))

        @pl.when(n == 0)
        def _():
            ck_s[...] = jnp.zeros_like(ck_s)
            cv_s[...] = jnp.zeros_like(cv_s)

        @pl.when(n < nb)
        def _():
            valid = _band_mask(n)
            valid2 = jnp.concatenate([valid, valid], axis=0)
            first = _first_head_lanes()

            def stacked_column(ref, hp):
                lane = 2 * hp * STAT_LANES
                return jnp.concatenate([ref[:, lane:lane + 1], ref[:, lane + STAT_LANES:lane + STAT_LANES + 1]], axis=0)

            for hp in range(width // LANES):
                sl = slice(hp * LANES, (hp + 1) * LANES)
                qq = _stack_heads(q_ref[:, sl], first)
                dd = _stack_heads(do_ref[:, sl], first)
                kk = jnp.concatenate([kp_ref[:, sl], kc_ref[:, sl]], axis=0)
                vv = jnp.concatenate([vp_ref[:, sl], vc_ref[:, sl]], axis=0)
                s = _dot_nt(qq, kk)
                p = jnp.where(valid2, jnp.exp(s - stacked_column(lse_ref, hp)), 0.0)
                dp = _dot_nt(dd, vv)
                ds = (p * (dp - stacked_column(dl_ref, hp))).astype(BF16)
                dq_ref[:, sl] = (_unstack_heads(_dot(ds, kk), first) * ATTN_SCALE).astype(BF16)
                dk = _dot_tn(ds, qq)
                dv = _dot_tn(p.astype(BF16), dd)
                dk_ref[:, sl] = (ck_s[:, sl] + dk[:ATTN_BLOCK]).astype(BF16)
                dv_ref[:, sl] = (cv_s[:, sl] + dv[:ATTN_BLOCK]).astype(BF16)
                ck_s[:, sl] = dk[ATTN_BLOCK:]
                cv_s[:, sl] = dv[ATTN_BLOCK:]

        @pl.when(n == nb)
        def _():
            dk_ref[...] = ck_s[...].astype(BF16)
            dv_ref[...] = cv_s[...].astype(BF16)

        if nc:
            _cargo_finish(*cargo_refs, (r == d // group - 1) & (n == nb))

    blk = (ATTN_BLOCK, width)
    cur = pl.BlockSpec(blk, lambda r, n: (jnp.minimum(n, nb - 1), r))
    stat = pl.BlockSpec((ATTN_BLOCK, group * STAT_WIDTH), lambda r, n: (jnp.minimum(n, nb - 1), r))
    prev = pl.BlockSpec(blk, lambda r, n: (jnp.maximum(jnp.minimum(n, nb - 1) - 1, 0), r))
    done = pl.BlockSpec(blk, lambda r, n: (jnp.maximum(n - 1, 0), r))
    arrays, specs, shapes, aliases, sems = _cargo_call(cargo, 8, 3)
    out = pl.pallas_call(
        body, name=f"attn_bwd_d{d}", grid=(d // group, nb + 1),
        in_specs=[cur, prev, cur, prev, cur, cur, stat, stat] + specs, out_specs=[cur, done, done] + specs,
        out_shape=[jax.ShapeDtypeStruct((L, d * ATTN_WIDTH), BF16)] * 3 + shapes,
        input_output_aliases=aliases,
        scratch_shapes=[pltpu.VMEM(blk, F32), pltpu.VMEM(blk, F32)] + sems,
        compiler_params=_params(("arbitrary", "arbitrary")),
    )(q, k, k, v, v, d_attn, lse, delta, *arrays)
    return out[:3], out[3:]


def _attn_bwd_consecutive(q, k, v, d_attn, lse, delta, cargo=()):
    L = q.shape[0]
    qb = CONSECUTIVE_BLOCKS
    steps = L // (qb * ATTN_BLOCK)
    nc = len(cargo)
    kinds = [kind for kind, _ in cargo]

    def body(*refs):
        q_ref, kp_ref, kc_ref, vp_ref, vc_ref, do_ref, lse_ref, dl_ref = refs[:8]
        dq_ref, dk_ref, dv_ref, ek_ref, ev_ref = refs[8 + nc:13 + nc]
        cargo_refs = (kinds, refs[8:8 + nc], refs[13 + nc:13 + 2 * nc], refs[13 + 2 * nc:])
        n = pl.program_id(0)
        if nc:
            _cargo_start(*cargo_refs, n == 0)
        first = _first_head_lanes()
        for hp in range(ATTN_WIDTH // LANES):
            sl = slice(hp * LANES, (hp + 1) * LANES)
            for sub in range(qb):
                rows = slice(sub * ATTN_BLOCK, (sub + 1) * ATTN_BLOCK)
                valid = _band_mask(n if sub == 0 else 1)
                valid2 = jnp.concatenate([valid, valid], axis=0)
                if sub == 0:
                    kk = jnp.concatenate([kp_ref[:, sl], kc_ref[rows, sl]], axis=0)
                    vv = jnp.concatenate([vp_ref[:, sl], vc_ref[rows, sl]], axis=0)
                else:
                    keys = slice((sub - 1) * ATTN_BLOCK, (sub + 1) * ATTN_BLOCK)
                    kk, vv = kc_ref[keys, sl], vc_ref[keys, sl]
                qq = _stack_heads(q_ref[rows, sl], first)
                dd = _stack_heads(do_ref[rows, sl], first)
                lane = 2 * hp * STAT_LANES
                column = lambda ref: jnp.concatenate(
                    [ref[rows, lane:lane + 1], ref[rows, lane + STAT_LANES:lane + STAT_LANES + 1]], axis=0)
                p = jnp.where(valid2, jnp.exp(_dot_nt(qq, kk) - column(lse_ref)), 0.0)
                ds = (p * (_dot_nt(dd, vv) - column(dl_ref))).astype(BF16)
                dq_ref[rows, sl] = (_unstack_heads(_dot(ds, kk), first) * ATTN_SCALE).astype(BF16)
                dk = _dot_tn(ds, qq)
                dv = _dot_tn(p.astype(BF16), dd)
                if sub == 0:
                    ek_ref[:, sl] = dk[:ATTN_BLOCK].astype(BF16)
                    ev_ref[:, sl] = dv[:ATTN_BLOCK].astype(BF16)
                else:
                    before = slice((sub - 1) * ATTN_BLOCK, sub * ATTN_BLOCK)
                    dk_ref[before, sl] = (carry_k + dk[:ATTN_BLOCK]).astype(BF16)
                    dv_ref[before, sl] = (carry_v + dv[:ATTN_BLOCK]).astype(BF16)
                carry_k, carry_v = dk[ATTN_BLOCK:], dv[ATTN_BLOCK:]
            dk_ref[rows, sl] = carry_k.astype(BF16)
            dv_ref[rows, sl] = carry_v.astype(BF16)
        if nc:
            _cargo_finish(*cargo_refs, n == steps - 1)

    cur = pl.BlockSpec((qb * ATTN_BLOCK, ATTN_WIDTH), lambda n: (n, 0))
    prev = pl.BlockSpec((ATTN_BLOCK, ATTN_WIDTH), lambda n: (jnp.maximum(n * qb - 1, 0), 0))
    edge = pl.BlockSpec((ATTN_BLOCK, ATTN_WIDTH), lambda n: (n, 0))
    stat = pl.BlockSpec((qb * ATTN_BLOCK, STAT_WIDTH), lambda n: (n, 0))
    arrays, specs, shapes, aliases, sems = _cargo_call(cargo, 8, 5)
    out = pl.pallas_call(
        body, name="attn_bwd_d1", grid=(steps,),
        in_specs=[cur, prev, cur, prev, cur, cur, stat, stat] + specs, out_specs=[cur, cur, cur, edge, edge] + specs,
        out_shape=[jax.ShapeDtypeStruct((L, ATTN_WIDTH), BF16)] * 3
        + [jax.ShapeDtypeStruct((steps * ATTN_BLOCK, ATTN_WIDTH), BF16)] * 2 + shapes,
        input_output_aliases=aliases, scratch_shapes=sems,
        compiler_params=_params(("arbitrary",)),
    )(q, k, k, v, v, d_attn, lse, delta, *arrays)
    return out[:3], out[3:5], out[5:]


def _mix_in_bwd(dqkv, edges, d_pool_in, w_in, x, g_pre, dx1):
    S, D = x.shape
    TM = CONSECUTIVE_BLOCKS * ATTN_BLOCK
    nd = len(DILATIONS)
    n_tiles = S // TM

    def body(*refs):
        g_refs = refs[:3 * nd]
        e_refs = (None,) + refs[3 * nd:3 * nd + 2]
        dpi_ref, w_ref, x_ref, g_ref, dx1_ref, dproj_ref, gx_ref, dg_ref = refs[3 * nd + 2:3 * nd + 10]
        t_s = refs[3 * nd + 10:]

        @pl.when(pl.program_id(0) == 0)
        def _():
            dg_ref[...] = jnp.zeros_like(dg_ref)

        dh = jnp.zeros((TM, D), F32)
        for a in range(4):
            if a < 3:
                tot = g_refs[a][...].astype(F32)
                if a > 0:
                    late = jnp.where(pl.program_id(0) < n_tiles - 1, e_refs[a][...].astype(F32), 0.0)
                    tot = jnp.concatenate([tot[:TM - ATTN_BLOCK], tot[TM - ATTN_BLOCK:] + late], axis=0)
                for i, d in enumerate(DILATIONS[1:]):
                    tot = tot + _from_residue(g_refs[3 * (i + 1) + a], t_s, d)
                db = tot.astype(BF16)
            else:
                db = dpi_ref[...]
            dproj_ref[:, a * ATTN_WIDTH:(a + 1) * ATTN_WIDTH] = db
            dh = dh + _dot_nt(db, w_ref[a])
        n, r = _rms_stats(x_ref[...])
        dx, dg = _rms_bwd(dh, n, r, g_ref[...])
        dg_ref[...] += dg
        gx_ref[...] = dx1_ref[...].astype(F32) + dx

    row = lambda w: pl.BlockSpec((TM, w), lambda i: (i, 0))
    edge = pl.BlockSpec((ATTN_BLOCK, ATTN_WIDTH), lambda i: (jnp.minimum(i + 1, n_tiles - 1), 0))
    return pl.pallas_call(
        body, name="mix_in_bwd", grid=(S // TM,),
        in_specs=[_residue_spec(TM, d) for d in DILATIONS for _ in range(3)] + [edge, edge]
        + [row(POOL_WIDTH), _const_spec(w_in.shape), row(D), _const_spec((1, D)), row(D)],
        out_specs=[row(4 * ATTN_WIDTH), row(D), pl.BlockSpec((1, D), lambda i: (0, 0))],
        out_shape=[jax.ShapeDtypeStruct((S, 4 * ATTN_WIDTH), BF16), jax.ShapeDtypeStruct((S, D), F32),
                   jax.ShapeDtypeStruct((1, D), F32)],
        scratch_shapes=_token_scratch(TM),
        compiler_params=_params(("arbitrary",), VMEM_LIMIT),
    )(*[g for gs in dqkv for g in gs], *edges, d_pool_in, w_in, x, g_pre, dx1)


SMALL_EARLY = ("pool_w", "pool_scale", "g_mix_post", "g_ffn_pre", "conv_b", "g_ffn_post", "conv_w")
SMALL_LATE = ("g_mix_pre",)


def _pack_small(grads, names):
    parts = []
    for n in names:
        g = grads[n]
        if n == "conv_w":
            g = g.reshape(CONV_WIDTH, N_SHARD, -1).transpose(1, 0, 2)
        parts.append(g.reshape(-1, LANES))
    return jnp.concatenate(parts, axis=0) if len(parts) > 1 else parts[0]


def _unpack_small(packed, names, like, shard):
    out, row = {}, 0
    for n in names:
        size = like[n].size * (N_SHARD if n == "conv_w" else 1)
        g = packed[row:row + size // LANES]
        row += size // LANES
        if n == "conv_w":
            g = lax.dynamic_slice_in_dim(g.reshape((N_SHARD,) + like[n].shape), shard, 1, axis=0)[0]
        out[n] = g.reshape(like[n].shape)
    return out


def _local_step(x, target, g_mix_pre, w_in, pool_w, pool_scale, w_out, g_mix_post, g_ffn_pre,
                w_up, conv_w, conv_b, w_down, g_ffn_post, mesh_pos=None):
    on_mesh = mesh_pos is not None
    D = x.shape[1]
    CW = w_up.shape[2]
    qkv, pool_in, h1, got = _mix_in_fwd(x, g_mix_pre, w_in, [("ici", w_up)] if on_mesh else ())
    w_up = got[0] if on_mesh else w_up
    o1, l1, got = _attn_fwd(*qkv[0], 1, [("d2d", w_up), ("ici", w_out)] if on_mesh else ())
    w_up, w_out = got if on_mesh else (w_up, w_out)
    o4, l4, got = _attn_fwd(*qkv[1], 4, [("d2d", w_out), ("ici", w_down)] if on_mesh else ())
    w_out, w_down = got if on_mesh else (w_out, w_down)
    o16, l16, got = _attn_fwd(*qkv[2], 16, [("d2d", w_down)] if on_mesh else ())
    w_down = got[0] if on_mesh else w_down
    w_out = w_out.reshape(D, D)
    w_down = w_down.reshape(2 * CW, D)
    attn, lse = _attn_mix((o1, o4, o16), (l1, l4, l16))
    pool = _pool_fwd(pool_in, pool_w, pool_scale)
    mixed, x1, h2, cat = _mix_out_fwd(attn, pool, w_out, x, g_mix_post, g_ffn_pre)

    yv, dy, df, dc, loss, d_g_ffn_post, d_conv_b, d_conv_w = _ffn_fwd(
        h2, x1, target, w_up, w_down, conv_w, conv_b, g_ffn_post)
    du, dx1, d_g_ffn_pre = _ffn_bwd(dc, conv_w, w_up, x1, g_ffn_pre, dy)
    d_w_up = _matmul_tn(h2, du, N_SHARD, "grad_w_up")
    d_w_down = _matmul_tn(yv, df, 1, "grad_w_down")[0].reshape(N_SHARD, CW // 2, D)
    swap = [("swap", d_w_up), ("swap", d_w_down)] if on_mesh else ()
    d_mixed, d_pool, d_g_mix_post, d_attn, delta, from_sibling = _mix_out_bwd(dx1, mixed, g_mix_post, w_out, attn, swap)
    d_w_out = _matmul_tn(cat, d_mixed, 1, "grad_w_out")[0].reshape(N_SHARD, D // N_SHARD, D)
    d_pool_in, d_pool_w, d_pool_scale = _pool_bwd(pool_in, d_pool, pool_w, pool_scale)
    grads = dict(pool_w=d_pool_w, pool_scale=d_pool_scale, w_out=d_w_out, g_mix_post=d_g_mix_post,
                 g_ffn_pre=d_g_ffn_pre, w_up=d_w_up, conv_w=d_conv_w, conv_b=d_conv_b, w_down=d_w_down,
                 g_ffn_post=d_g_ffn_post)
    cargo = [(), (), ()]
    if on_mesh:
        c_arr, device, shard_arr = mesh_pos
        up_f32, up_bf16 = _pair_sum(d_w_up, from_sibling[0], c_arr, "pair_sum_w_up")
        down_f32, down_bf16 = _pair_sum(d_w_down, from_sibling[1], c_arr, "pair_sum_w_down")
        early = _pack_small(grads, SMALL_EARLY)
        early_slots = lax.dynamic_update_index_in_dim(jnp.zeros((8,) + early.shape, F32), early, device, 0)
        cargo = [[("scatter", down_bf16), ("everyone", early_slots)], [("scatter", up_bf16)], []]

    dqkv1, edges, landed1 = _attn_bwd_consecutive(*qkv[0], d_attn[0], lse[0], delta[0], cargo[0])
    dqkv4, landed4 = _attn_bwd(*qkv[1], d_attn[1], lse[1], delta[1], 4, cargo[1])
    if on_mesh:
        halves = [_shard_sum(up_f32, landed4[0], shard_arr, c_arr, "shard_sum_w_up"),
                  _shard_sum(down_f32, landed1[0], shard_arr, c_arr, "shard_sum_w_down")]
        cargo[2] = cargo[2] + [("join", h) for h in halves]
    dqkv16, landed16 = _attn_bwd(*qkv[2], d_attn[2], lse[2], delta[2], 16, cargo[2])
    if on_mesh:
        grads.update(small_early=landed1[1], w_up=landed16[0], w_down=landed16[1])
    d_proj, grad_x, grads["g_mix_pre"] = _mix_in_bwd((dqkv1, dqkv4, dqkv16), edges, d_pool_in, w_in, x, g_mix_pre, dx1)
    grads["w_in"] = _matmul_tn(h1, d_proj, N_SHARD, "grad_w_in")
    return loss, grad_x, grads


ANY = pl.BlockSpec(memory_space=pl.ANY)


def _position():
    x, y, c = lax.axis_index("x"), lax.axis_index("y"), lax.axis_index("c")
    chips = [(1 - x, y), (x, 1 - y), (1 - x, 1 - y)]
    return x, y, c, chips


def _remote(src, dst, send_sem, recv_sem, to):
    return pltpu.make_async_remote_copy(src_ref=src, dst_ref=dst, send_sem=send_sem, recv_sem=recv_sem,
                                        device_id=to, device_id_type=MESH)


def _cast_bf16(w, shard_arr, name):
    R, C = w.shape
    tr = R // 2

    def body(s_ref, w_ref, o_ref):
        o_ref[0] = w_ref[...].astype(BF16)

    return pl.pallas_call(
        body, name=name,
        grid_spec=pltpu.PrefetchScalarGridSpec(
            num_scalar_prefetch=1, grid=(2,),
            in_specs=[pl.BlockSpec((tr, C), lambda i, s_ref: (i, 0))],
            out_specs=pl.BlockSpec((1, tr, C), lambda i, s_ref: (s_ref[0], i, 0))),
        out_shape=jax.ShapeDtypeStruct((N_SHARD, R, C), BF16),
        compiler_params=_params(("parallel",)))(shard_arr, w)


def _gather_weights(bufs):
    n = len(bufs) - 1

    def body(*refs):
        outs, cw_out = refs[n + 1:2 * n + 1], refs[2 * n + 1]
        ici_send, ici_recv, d2d_send, d2d_recv = refs[2 * n + 2:]
        x, y, c, chips = _position()
        s = 2 * x + y
        sibling = (x, y, 1 - c)

        def half(a, shard, h):
            rows = outs[a].shape[1] // 2
            return outs[a].at[shard, pl.ds(h * rows, rows), :]

        sends = []
        for a in range(n):
            for j, (px, py) in enumerate(chips):
                sends.append(_remote(half(a, s, c), half(a, s, c),
                                     ici_send.at[3 * a + j], ici_recv.at[3 * a + j], (px, py, c)))
        for j, (px, py) in enumerate(chips):
            sends.append(_remote(cw_out.at[s], cw_out.at[s], ici_send.at[3 * n + j], ici_recv.at[3 * n + j], (px, py, c)))
        for cp in sends:
            cp.start()
        passed = []
        for a in range(n):
            for j, (px, py) in enumerate(chips):
                sj = 2 * px + py
                got = half(a, sj, c)
                _remote(got, got, ici_send.at[3 * a + j], ici_recv.at[3 * a + j], (px, py, c)).wait_recv()
                fwd = _remote(got, got, d2d_send.at[3 * a + j], d2d_recv.at[3 * a + j], sibling)
                fwd.start()
                passed.append(fwd)
        for j, (px, py) in enumerate(chips):
            got = cw_out.at[2 * px + py]
            _remote(got, got, ici_send.at[3 * n + j], ici_recv.at[3 * n + j], (px, py, c)).wait_recv()
        for a in range(n):
            for j, (px, py) in enumerate(chips):
                got = half(a, 2 * px + py, 1 - c)
                _remote(got, got, d2d_send.at[3 * a + j], d2d_recv.at[3 * a + j], sibling).wait_recv()
        for cp in sends + passed:
            cp.wait_send()

    return pl.pallas_call(
        body, name="gather_weights",
        in_specs=[ANY] * (n + 1), out_specs=[ANY] * (n + 1),
        out_shape=[jax.ShapeDtypeStruct(b.shape, b.dtype) for b in bufs],
        input_output_aliases={i: i for i in range(n + 1)},
        scratch_shapes=[pltpu.SemaphoreType.DMA((3 * n + 3,)), pltpu.SemaphoreType.DMA((3 * n + 3,)),
                        pltpu.SemaphoreType.DMA((3 * n,)), pltpu.SemaphoreType.DMA((3 * n,))],
        compiler_params=pltpu.CompilerParams(has_side_effects=True),
    )(*bufs)


def _swap_halves(grads, tag):
    n = len(grads)

    def body(*refs):
        ins, outs, send_sem, recv_sem = refs[:n], refs[n:2 * n], refs[2 * n], refs[2 * n + 1]
        x, y, c, _ = _position()
        copies = []
        for a in range(n):
            rows = ins[a].shape[1] // 2
            copies.append(_remote(ins[a].at[:, pl.ds((1 - c) * rows, rows), :], outs[a],
                                  send_sem.at[a], recv_sem.at[a], (x, y, 1 - c)))
        for cp in copies:
            cp.start()
        for cp in copies:
            cp.wait()

    return pl.pallas_call(
        body, name="swap_grad_halves_" + tag,
        in_specs=[ANY] * n, out_specs=[ANY] * n,
        out_shape=[jax.ShapeDtypeStruct((g.shape[0], g.shape[1] // 2, g.shape[2]), F32) for g in grads],
        scratch_shapes=[pltpu.SemaphoreType.DMA((n,)), pltpu.SemaphoreType.DMA((n,))],
        compiler_params=pltpu.CompilerParams(has_side_effects=True),
    )(*grads)


def _pair_sum(g, got, c_arr, name):
    n_sh, R, C = g.shape
    rows = R // 2

    def body(c_ref, g_ref, r_ref, f_ref, b_ref):
        t = g_ref[...] + r_ref[...]
        f_ref[...] = t
        b_ref[...] = t.astype(BF16)

    blk = pl.BlockSpec((1, rows, C), lambda i, c_ref: (i, 0, 0))
    return pl.pallas_call(
        body, name=name,
        grid_spec=pltpu.PrefetchScalarGridSpec(
            num_scalar_prefetch=1, grid=(n_sh,),
            in_specs=[pl.BlockSpec((1, rows, C), lambda i, c_ref: (i, c_ref[0], 0)), blk],
            out_specs=[blk, blk]),
        out_shape=[jax.ShapeDtypeStruct((n_sh, rows, C), F32), jax.ShapeDtypeStruct((n_sh, rows, C), BF16)],
        compiler_params=_params(("parallel",)),
    )(c_arr, g, got)


def _shard_sum(sums_f32, recv, shard_arr, c_arr, name):
    _, rows, C = sums_f32.shape

    def body(s_ref, c_ref, o_ref, r_ref, t_ref):
        t_ref[...] = ((o_ref[0] + r_ref[0].astype(F32)) + r_ref[1].astype(F32)) + r_ref[2].astype(F32)

    return pl.pallas_call(
        body, name=name,
        grid_spec=pltpu.PrefetchScalarGridSpec(
            num_scalar_prefetch=2, grid=(1,),
            in_specs=[pl.BlockSpec((1, rows, C), lambda i, s_ref, c_ref: (s_ref[0], 0, 0)),
                      pl.BlockSpec((3, rows, C), lambda i, s_ref, c_ref: (0, 0, 0))],
            out_specs=pl.BlockSpec((rows, C), lambda i, s_ref, c_ref: (c_ref[0], 0))),
        out_shape=jax.ShapeDtypeStruct((2 * rows, C), F32),
        compiler_params=_params(("arbitrary",)),
    )(shard_arr, c_arr, sums_f32, recv)


def _join_halves(bufs):
    n = len(bufs)

    def body(*refs):
        outs, send_sem, recv_sem = refs[n:2 * n], refs[2 * n], refs[2 * n + 1]
        x, y, c, _ = _position()
        copies = []
        for a in range(n):
            rows = outs[a].shape[0] // 2
            mine = outs[a].at[pl.ds(c * rows, rows), :]
            copies.append(_remote(mine, mine, send_sem.at[a], recv_sem.at[a], (x, y, 1 - c)))
        for cp in copies:
            cp.start()
        for a, cp in enumerate(copies):
            cp.wait_send()
            rows = outs[a].shape[0] // 2
            theirs = outs[a].at[pl.ds((1 - c) * rows, rows), :]
            _remote(theirs, theirs, send_sem.at[a], recv_sem.at[a], (x, y, 1 - c)).wait_recv()

    return pl.pallas_call(
        body, name="join_grad_halves",
        in_specs=[ANY] * n, out_specs=[ANY] * n,
        out_shape=[jax.ShapeDtypeStruct(b.shape, F32) for b in bufs],
        input_output_aliases={i: i for i in range(n)},
        scratch_shapes=[pltpu.SemaphoreType.DMA((n,)), pltpu.SemaphoreType.DMA((n,))],
        compiler_params=pltpu.CompilerParams(has_side_effects=True),
    )(*bufs)


def _small_sum(parts, tag):
    _, R, C = parts.shape

    def body(p_ref, o_ref):
        t = p_ref[0]
        for k in range(1, 8):
            t = t + p_ref[k]
        o_ref[...] = t

    return pl.pallas_call(
        body, name="small_grad_sum_" + tag, grid=(1,),
        in_specs=[pl.BlockSpec((8, R, C), lambda i: (0, 0, 0))], out_specs=pl.BlockSpec((R, C), lambda i: (0, 0)),
        out_shape=jax.ShapeDtypeStruct((R, C), F32), compiler_params=_params(("arbitrary",)),
    )(parts)


def _adamw_math(w, g, m, v):
    m = ADAM_B1 * m + (1.0 - ADAM_B1) * g
    v = ADAM_B2 * v + (1.0 - ADAM_B2) * (g * g)
    m_hat = m / (1.0 - ADAM_B1 ** ADAM_STEP)
    v_hat = v / (1.0 - ADAM_B2 ** ADAM_STEP)
    delta = -ADAM_LR * (m_hat / (jnp.sqrt(v_hat) + ADAM_EPS) + ADAM_WD * w)
    return delta, m, v


def _adamw_big(ws, gs, ms, vs, name, cargo=()):
    n = len(ws)
    nc = len(cargo)
    kinds = [kind for kind, _ in cargo]

    def body(*refs):
        ins, outs = refs[:4 * n], refs[4 * n + nc:7 * n + nc]
        cargo_refs = (kinds, refs[4 * n:4 * n + nc], refs[7 * n + nc:7 * n + 2 * nc], refs[7 * n + 2 * nc:])
        if nc:
            _cargo_start(*cargo_refs, pl.program_id(0) == 0)
        for a in range(n):
            w, g, m, v = (ins[k * n + a][...] for k in range(4))
            outs[a][...], outs[n + a][...], outs[2 * n + a][...] = _adamw_math(w, g, m, v)
        if nc:
            _cargo_finish(*cargo_refs, pl.program_id(0) == 3)

    blks = [pl.BlockSpec((w.shape[0] // 4, w.shape[1]), lambda i: (i, 0)) for w in ws]
    arrays, cargo_specs, shapes, aliases, sems = _cargo_call(cargo, 4 * n, 3 * n)
    out = pl.pallas_call(
        body, name=name, grid=(4,), in_specs=blks * 4 + cargo_specs, out_specs=blks * 3 + cargo_specs,
        out_shape=[jax.ShapeDtypeStruct(w.shape, F32) for w in ws] * 3 + shapes,
        input_output_aliases=aliases, scratch_shapes=sems,
        compiler_params=_params(("arbitrary",)),
    )(*ws, *gs, *ms, *vs, *arrays)
    return (out[:n], out[n:2 * n], out[2 * n:3 * n]), out[3 * n:]


def _adamw_small(ws, gs, ms, vs):
    n = len(ws)

    def body(*refs):
        for a in range(n):
            w, g, m, v = (refs[k * n + a][...] for k in range(4))
            d, nm, nv = _adamw_math(w, g, m, v)
            refs[4 * n + a][...] = d
            refs[5 * n + a][...] = nm
            refs[6 * n + a][...] = nv

    shapes = [jax.ShapeDtypeStruct(w.shape, F32) for w in ws]
    out = pl.pallas_call(body, name="adamw_small", out_shape=shapes * 3)(*ws, *gs, *ms, *vs)
    return out[:n], out[n:2 * n], out[2 * n:]


BIG = ("w_in", "w_out", "w_up", "w_down")
SMALL = ("g_mix_pre", "pool_w", "pool_scale", "g_mix_post", "g_ffn_pre", "conv_b", "g_ffn_post", "conv_w")
ORDER = ("g_mix_pre", "w_in", "pool_w", "pool_scale", "w_out", "g_mix_post", "g_ffn_pre", "w_up", "conv_w", "conv_b",
         "w_down", "g_ffn_post")


def kernel(x, g_mix_pre, w_in, pool_w, pool_scale, w_out, g_mix_post, g_ffn_pre, w_up, conv_w, conv_b, w_down, g_ffn_post, loss_target, m_g_mix_pre, m_w_in, m_pool_w, m_pool_scale, m_w_out, m_g_mix_post, m_g_ffn_pre, m_w_up, m_conv_w, m_conv_b, m_w_down, m_g_ffn_post, v_g_mix_pre, v_w_in, v_pool_w, v_pool_scale, v_w_out, v_g_mix_post, v_g_ffn_pre, v_w_up, v_conv_w, v_conv_b, v_w_down, v_g_ffn_post):
    args = dict(locals())
    W = {n: args[n][0] for n in ORDER}
    M = {n: args["m_" + n][0] for n in ORDER}
    V = {n: args["v_" + n][0] for n in ORDER}
    for d in (W, M, V):
        d["pool_w"] = d["pool_w"].reshape(-1, POOL_GROUP)
        for n in ("g_mix_pre", "pool_scale", "g_mix_post", "g_ffn_pre", "conv_b", "g_ffn_post"):
            d[n] = d[n].reshape(1, -1)
    CW = W["w_up"].shape[1]
    c_arr = lax.axis_index("c").astype(jnp.int32).reshape(1)
    shard = 2 * lax.axis_index("x") + lax.axis_index("y")
    shard_arr = shard.astype(jnp.int32).reshape(1)
    device = 2 * shard + lax.axis_index("c")

    conv_w_slots = lax.dynamic_update_index_in_dim(jnp.zeros((N_SHARD,) + W["conv_w"].shape, F32), W["conv_w"], shard, 0)
    slots = {n: _cast_bf16(W[n], shard_arr, "cast_" + n) for n in BIG}
    w_in_g, conv_w_g = _gather_weights([slots["w_in"], conv_w_slots])
    conv_w_full = conv_w_g.transpose(1, 0, 2).reshape(CONV_WIDTH, 1, N_SHARD * CW)

    loss, grad_x, G = _local_step(
        x[0], loss_target[0], W["g_mix_pre"], w_in_g, W["pool_w"].reshape(-1, POOL_GROUP, POOL_GROUP), W["pool_scale"],
        slots["w_out"], W["g_mix_post"], W["g_ffn_pre"], slots["w_up"], conv_w_full, W["conv_b"],
        slots["w_down"], W["g_ffn_post"], (c_arr, device, shard_arr))

    late, ffn = ("w_in", "w_out"), ("w_up", "w_down")
    from_sibling = _swap_halves([G[n] for n in late], "mix")
    sums = {n: _pair_sum(G[n], r, c_arr, "pair_sum_" + n) for n, r in zip(late, from_sibling)}
    loss_rows = jnp.pad(loss, ((0, 7), (0, LANES - 1)))
    small = jnp.concatenate([_pack_small(G, SMALL_LATE), loss_rows], axis=0)
    small_slots = lax.dynamic_update_index_in_dim(jnp.zeros((8,) + small.shape, F32), small, device, 0)
    pick = lambda d, names: [d[n] for n in names]
    delta, new_m, new_v = {}, {}, {}
    updates, landed = _adamw_big(pick(W, ffn), pick(G, ffn), pick(M, ffn), pick(V, ffn), "adamw_ffn",
                                 [("scatter", sums[n][1]) for n in late] + [("everyone", small_slots)])
    halves = [_shard_sum(sums[n][0], r, shard_arr, c_arr, "shard_sum_" + n) for n, r in zip(late, landed[:2])]
    full = dict(zip(late, _join_halves(halves)))
    full.update({n: G[n] for n in ffn})
    full.update(_unpack_small(_small_sum(G["small_early"], "early"), SMALL_EARLY, W, shard))
    late_total = _small_sum(landed[2], "late")
    full.update(_unpack_small(late_total, SMALL_LATE, W, shard))
    loss = late_total[-8, 0]

    for names, (ds, nms, nvs) in ((ffn, updates),
                                  (late, _adamw_big(pick(W, late), pick(full, late), pick(M, late), pick(V, late), "adamw_mix")[0]),
                                  (SMALL, _adamw_small(pick(W, SMALL), pick(full, SMALL), pick(M, SMALL), pick(V, SMALL)))):
        for n, d, nm, nv in zip(names, ds, nms, nvs):
            delta[n], new_m[n], new_v[n] = d, nm, nv

    shaped = lambda d: [d[n].reshape(args[n].shape) for n in ORDER]
    return (loss, grad_x[None], *shaped(full), *shaped(delta), *shaped(new_m), *shaped(new_v))
```

```python
import functools

import jax
import jax.numpy as jnp
from jax import lax
from jax.experimental import pallas as pl
from jax.experimental.pallas import tpu as pltpu

F32 = jnp.float32
BF16 = jnp.bfloat16

RMS_EPS = 1e-6
NEG_INF = -1e30
N_HEADS = 8
HEAD_DIM = 64
ATTN_WIDTH = N_HEADS * HEAD_DIM
ATTN_SCALE = HEAD_DIM ** -0.5
ATTN_BLOCK = 128
DILATIONS = (1, 4, 16)
RESIDUES_PER_STEP = 4
CONSECUTIVE_BLOCKS = 4
POOL_WINDOWS = (2, 4, 8, 16)
POOL_GROUP = 128
POOL_WIDTH = POOL_GROUP * len(POOL_WINDOWS)
POOL_HALO = 16
CONV_WIDTH = 3
CONV_HALO = 8
N_SHARD = 4
LANES = 128
STAT_LANES = 16
STAT_WIDTH = N_HEADS * STAT_LANES

ADAM_LR = 0.001
ADAM_B1 = 0.9
ADAM_B2 = 0.999
ADAM_EPS = 1e-08
ADAM_WD = 0.01
ADAM_STEP = 10

VMEM_LIMIT = 60 * 1024 * 1024
MESH = pl.DeviceIdType.MESH
NT = (((1,), (1,)), ((), ()))
TN = (((0,), (0,)), ((), ()))


def _params(sem, vmem=None):
    return pltpu.CompilerParams(dimension_semantics=sem, vmem_limit_bytes=vmem)


def _const_spec(shape):
    zeros = (0,) * len(shape)
    return pl.BlockSpec(shape, lambda *_: zeros, pipeline_mode=pl.Buffered(1))


def _dot(a, b):
    return jnp.dot(a, b, preferred_element_type=F32)


def _dot_nt(a, b):
    return lax.dot_general(a, b, NT, preferred_element_type=F32)


def _dot_tn(a, b):
    return lax.dot_general(a, b, TN, preferred_element_type=F32)


def _rms_stats(x):
    r = lax.rsqrt(jnp.mean(x * x, axis=-1, keepdims=True) + RMS_EPS)
    return x * r, r


def _rms_bwd(dy, n, r, g):
    dg = jnp.sum(dy * n, axis=0, keepdims=True)
    dn = dy * g
    dx = r * (dn - n * jnp.mean(dn * n, axis=-1, keepdims=True))
    return dx, dg


def _gelu_tanh(g):
    k = 0.7978845608028654
    kc = k * 0.044715
    g2 = g * g
    t = jnp.tanh(g * (k + kc * g2))
    h = 0.5 * t + 0.5
    dh = (0.5 - 0.5 * (t * t)) * (k + (3.0 * kc) * g2)
    return g * h, h + g * dh


def _residue_shape(S, d, dtype, width=ATTN_WIDTH):
    return jax.ShapeDtypeStruct((S // d, d * width), dtype)


def _residue_spec(TM, d, width=ATTN_WIDTH):
    return pl.BlockSpec((TM // d, d * width), lambda i: (i, 0))


def _token_scratch(TM, width=ATTN_WIDTH):
    return [pltpu.VMEM((TM, LANES), F32)] * (width // LANES)


def _head_stat_matrix(pick_first_lane):
    r = lax.broadcasted_iota(jnp.int32, (ATTN_WIDTH, STAT_WIDTH), 0)
    c = lax.broadcasted_iota(jnp.int32, (ATTN_WIDTH, STAT_WIDTH), 1) // STAT_LANES
    return ((r == c * HEAD_DIM) if pick_first_lane else (r // HEAD_DIM == c)).astype(BF16)


def _bf16_pieces(x, n):
    pieces = []
    for _ in range(n):
        p = x.astype(BF16)
        pieces.append(p)
        x = x - p.astype(F32)
    return pieces


def _put_tokens(dst_s, val):
    for cb, chunk in enumerate(dst_s):
        chunk[...] = val[:, cb * LANES:(cb + 1) * LANES]


def _get_tokens(src_s):
    return jnp.concatenate([chunk[...] for chunk in src_s], axis=1)


def _to_residue(val, src_s, out_ref, d, dtype):
    if d == 1:
        out_ref[...] = val.astype(dtype)
        return
    rows = src_s[0].shape[0]
    for r in range(d):
        for cb, chunk in enumerate(src_s):
            col = (r * len(src_s) + cb) * LANES
            out_ref[:, col:col + LANES] = chunk[pl.ds(r, rows // d, stride=d), :].astype(dtype)


def _from_residue(in_ref, dst_s, d):
    if d == 1:
        return in_ref[...].astype(F32)
    rows = dst_s[0].shape[0]
    for r in range(d):
        for cb, chunk in enumerate(dst_s):
            col = (r * len(dst_s) + cb) * LANES
            chunk[pl.ds(r, rows // d, stride=d), :] = in_ref[:, col:col + LANES].astype(F32)
    return _get_tokens(dst_s)


def _mix_in_fwd(x, g_pre, w_in, cargo=()):
    S, D = x.shape
    TM = 512
    nc = len(cargo)
    kinds = [kind for kind, _ in cargo]
    n_chunks = ATTN_WIDTH // LANES

    def body(x_ref, g_ref, w_ref, *refs):
        cargo_in, refs = refs[:nc], refs[nc:]
        qkv_refs, p_ref, h_ref = refs[:9], refs[9], refs[10]
        t_s = refs[11 + nc:11 + nc + n_chunks]
        cargo_refs = (kinds, cargo_in, refs[11:11 + nc], refs[11 + nc + n_chunks:])
        if nc:
            _cargo_start(*cargo_refs, pl.program_id(0) == 0)
        n, _ = _rms_stats(x_ref[...])
        hb = (n * g_ref[...]).astype(BF16)
        h_ref[...] = hb
        for a in range(3):
            res = _dot(hb, w_ref[a])
            if a == 0:
                res = res * ATTN_SCALE
            _put_tokens(t_s, res)
            for i, d in enumerate(DILATIONS):
                _to_residue(res, t_s, qkv_refs[3 * i + a], d, BF16)
        p_ref[...] = _dot(hb, w_ref[3])
        if nc:
            _cargo_finish(*cargo_refs, pl.program_id(0) == S // TM - 1)

    row = lambda w: pl.BlockSpec((TM, w), lambda i: (i, 0))
    arrays, cargo_specs, shapes, aliases, sems = _cargo_call(cargo, 3, 11)
    out = pl.pallas_call(
        body, name="mix_in_fwd", grid=(S // TM,),
        in_specs=[row(D), _const_spec((1, D)), _const_spec(w_in.shape)] + cargo_specs,
        out_specs=[_residue_spec(TM, d) for d in DILATIONS for _ in range(3)] + [row(POOL_WIDTH), row(D)] + cargo_specs,
        out_shape=[_residue_shape(S, d, BF16) for d in DILATIONS for _ in range(3)]
        + [jax.ShapeDtypeStruct((S, POOL_WIDTH), F32), jax.ShapeDtypeStruct((S, D), BF16)] + shapes,
        input_output_aliases=aliases,
        scratch_shapes=_token_scratch(TM) + sems,
        compiler_params=_params(("arbitrary",), VMEM_LIMIT),
    )(x, g_pre, w_in, *arrays)
    return [out[0:3], out[3:6], out[6:9]], out[9], out[10], out[11:]


def _band_mask(n):
    qi = lax.broadcasted_iota(jnp.int32, (ATTN_BLOCK, 2 * ATTN_BLOCK), 0)
    ki = lax.broadcasted_iota(jnp.int32, (ATTN_BLOCK, 2 * ATTN_BLOCK), 1)
    dist = qi + ATTN_BLOCK - ki
    return (dist >= 0) & (dist <= ATTN_BLOCK) & ((ki >= ATTN_BLOCK) | (n > 0))


def _first_head_lanes():
    return lax.broadcasted_iota(jnp.int32, (1, LANES), 1) < HEAD_DIM


def _stack_heads(pair, first):
    zero = jnp.zeros_like(pair)
    return jnp.concatenate([jnp.where(first, pair, zero), jnp.where(first, zero, pair)], axis=0)


def _unstack_heads(stacked, first):
    return jnp.where(first, stacked[:ATTN_BLOCK], stacked[ATTN_BLOCK:])


CARGO_COPIES = {"ici": 3, "d2d": 3, "scatter": 3, "swap": 1, "everyone": 7, "join": 1}
CARGO_IN_PLACE = ("ici", "d2d", "everyone", "join")


def _cargo_copies(kinds, ins, outs, send_sems, recv_sems, want_recvs=True):
    x, y, c, chips = _position()
    s = 2 * x + y
    me = 2 * s + c
    sibling = (x, y, 1 - c)
    sends, recvs = [], []

    def add(k, src, dst, landing, to):
        sends.append(_remote(src, dst, send_sems.at[k], recv_sems.at[k], to))
        if want_recvs:
            recvs.append(_remote(landing, landing, send_sems.at[k], recv_sems.at[k], to))

    k0 = 0
    for a, kind in enumerate(kinds):
        if kind == "swap":
            rows = ins[a].shape[1] // 2
            add(k0, ins[a].at[:, pl.ds((1 - c) * rows, rows), :], outs[a], outs[a], sibling)
        elif kind == "join":
            rows = outs[a].shape[0] // 2
            mine = outs[a].at[pl.ds(c * rows, rows), :]
            add(k0, mine, mine, outs[a].at[pl.ds((1 - c) * rows, rows), :], sibling)
        elif kind == "everyone":
            for m in range(1, 8):
                peer = (x ^ (m >> 2), y ^ ((m >> 1) & 1), c ^ (m & 1))
                add(k0 + m - 1, outs[a].at[me], outs[a].at[me], outs[a].at[4 * peer[0] + 2 * peer[1] + peer[2]], peer)
        else:
            for j, (px, py) in enumerate(chips):
                sj = 2 * px + py
                if kind == "scatter":
                    add(k0 + j, ins[a].at[sj], outs[a].at[j], outs[a].at[j], (px, py, c))
                    continue
                buf = outs[a]
                rows = buf.shape[1] // 2
                half = lambda shard, h: buf.at[shard, pl.ds(h * rows, rows), :]
                if kind == "ici":
                    add(k0 + j, half(s, c), half(s, c), half(sj, c), (px, py, c))
                else:
                    add(k0 + j, half(sj, c), half(sj, c), half(sj, 1 - c), sibling)
        k0 += CARGO_COPIES[kind]
    return sends, recvs


def _cargo_start(kinds, ins, outs, sems, first_step):
    @pl.when(first_step)
    def _():
        for cp in _cargo_copies(kinds, ins, outs, *sems, want_recvs=False)[0]:
            cp.start()


def _cargo_finish(kinds, ins, outs, sems, last_step):
    @pl.when(last_step)
    def _():
        sends, recvs = _cargo_copies(kinds, ins, outs, *sems)
        for cp in sends:
            cp.wait_send()
        for cp in recvs:
            cp.wait_recv()


def _cargo_call(cargo, n_in, n_out):
    arrays = [a for _, a in cargo]
    shapes = []
    for kind, a in cargo:
        if kind == "scatter":
            shape = (3,) + a.shape[1:]
        elif kind == "swap":
            shape = (a.shape[0], a.shape[1] // 2, a.shape[2])
        else:
            shape = a.shape
        shapes.append(jax.ShapeDtypeStruct(shape, a.dtype))
    aliases = {n_in + i: n_out + i for i, (kind, _) in enumerate(cargo) if kind in CARGO_IN_PLACE}
    n_sems = sum(CARGO_COPIES[kind] for kind, _ in cargo)
    sems = [pltpu.SemaphoreType.DMA((n_sems,))] * 2 if cargo else []
    return arrays, [ANY] * len(cargo), shapes, aliases, sems


def _attn_fwd(q, k, v, d, cargo=()):
    L = q.shape[0]
    group = min(d, RESIDUES_PER_STEP)
    width = group * ATTN_WIDTH
    qb = RESIDUES_PER_STEP // group
    steps = L // (qb * ATTN_BLOCK)
    nc = len(cargo)
    kinds = [kind for kind, _ in cargo]

    def body(*refs):
        q_ref, kp_ref, kc_ref, vp_ref, vc_ref = refs[:5]
        o_ref, lse_ref = refs[5 + nc:7 + nc]
        cargo_refs = (kinds, refs[5:5 + nc], refs[7 + nc:7 + 2 * nc], refs[7 + 2 * nc:])
        r, n = pl.program_id(0), pl.program_id(1)
        if nc:
            _cargo_start(*cargo_refs, (r == 0) & (n == 0))
        first = _first_head_lanes()
        for sub in range(qb):
            rows = slice(sub * ATTN_BLOCK, (sub + 1) * ATTN_BLOCK)
            valid = _band_mask(n if sub == 0 else 1)
            valid2 = jnp.concatenate([valid, valid], axis=0)
            for hp in range(width // LANES):
                sl = slice(hp * LANES, (hp + 1) * LANES)
                if sub == 0:
                    kk = jnp.concatenate([kp_ref[:, sl], kc_ref[rows, sl]], axis=0)
                    vv = jnp.concatenate([vp_ref[:, sl], vc_ref[rows, sl]], axis=0)
                else:
                    keys = slice((sub - 1) * ATTN_BLOCK, (sub + 1) * ATTN_BLOCK)
                    kk, vv = kc_ref[keys, sl], vc_ref[keys, sl]
                s = jnp.where(valid2, _dot_nt(_stack_heads(q_ref[rows, sl], first), kk), NEG_INF)
                m = jnp.max(s, axis=-1, keepdims=True)
                p = jnp.exp(s - m)
                den = jnp.sum(p, axis=-1, keepdims=True)
                o_ref[rows, sl] = _unstack_heads(_dot(p.astype(BF16), vv) / den, first).astype(BF16)
                lse_ref[rows, sl] = _unstack_heads(m + jnp.log(den), first)
        if nc:
            _cargo_finish(*cargo_refs, (r == d // group - 1) & (n == steps - 1))

    cur = pl.BlockSpec((qb * ATTN_BLOCK, width), lambda r, n: (n, r))
    prev = pl.BlockSpec((ATTN_BLOCK, width), lambda r, n: (jnp.maximum(n * qb - 1, 0), r))
    arrays, specs, shapes, aliases, sems = _cargo_call(cargo, 5, 2)
    out = pl.pallas_call(
        body, name=f"attn_fwd_d{d}", grid=(d // group, steps),
        in_specs=[cur, prev, cur, prev, cur] + specs,
        out_specs=[cur, cur] + specs,
        out_shape=[jax.ShapeDtypeStruct((L, d * ATTN_WIDTH), BF16), jax.ShapeDtypeStruct((L, d * ATTN_WIDTH), F32)] + shapes,
        input_output_aliases=aliases, scratch_shapes=sems,
        compiler_params=_params(("arbitrary", "arbitrary")),
    )(q, k, k, v, v, *arrays)
    return out[0], out[1], out[2:]


def _attn_mix(outs, lses):
    S = outs[0].shape[0]
    TM = 512
    n = len(DILATIONS)

    def body(*refs):
        o_refs, l_refs, attn_ref, lse_refs = refs[:n], refs[n:2 * n], refs[2 * n], refs[2 * n + 1:3 * n + 1]
        t_s, c_s = refs[3 * n + 1:-1], refs[-1:]
        os = [_from_residue(o_refs[i], t_s, d) for i, d in enumerate(DILATIONS)]
        ls = [_from_residue(l_refs[i], t_s, d) for i, d in enumerate(DILATIONS)]
        m = jnp.maximum(jnp.maximum(ls[0], ls[1]), ls[2])
        es = [jnp.exp(l - m) for l in ls]
        den = es[0] + es[1] + es[2]
        attn_ref[...] = ((es[0] * os[0] + es[1] * os[1] + es[2] * os[2]) / den).astype(BF16)
        pick = _head_stat_matrix(pick_first_lane=True)
        lse = sum(_dot(p, pick) for p in _bf16_pieces(m + jnp.log(den), 3))
        _put_tokens(c_s, lse)
        for i, d in enumerate(DILATIONS):
            _to_residue(lse, c_s, lse_refs[i], d, F32)

    specs = [_residue_spec(TM, d) for d in DILATIONS]
    out = pl.pallas_call(
        body, name="attn_mix", grid=(S // TM,),
        in_specs=specs * 2, out_specs=[specs[0]] + [_residue_spec(TM, d, STAT_WIDTH) for d in DILATIONS],
        out_shape=[jax.ShapeDtypeStruct((S, ATTN_WIDTH), BF16)]
        + [_residue_shape(S, d, F32, STAT_WIDTH) for d in DILATIONS],
        scratch_shapes=_token_scratch(TM) + _token_scratch(TM, STAT_WIDTH),
        compiler_params=_params(("parallel",)),
    )(*outs, *lses)
    return out[0], out[1:]


def _pool_counts(first_row, rows, w):
    t = first_row + lax.broadcasted_iota(jnp.int32, (rows, 1), 0)
    return jnp.minimum(t + 1, w).astype(F32)


def _trailing_sums(xe, w):
    s, k = xe, 1
    while k < w:
        s = s + pltpu.roll(s, k, 0)
        k *= 2
    return s


def _leading_sums(xe, w):
    rows = xe.shape[0]
    s, k = xe, 1
    while k < w:
        s = s + pltpu.roll(s, rows - k, 0)
        k *= 2
    return s


def _pooled_groups(halo, cur, first_row):
    TM = cur.shape[0]
    xe = jnp.concatenate([halo, cur], axis=0)
    out = []
    for g, w in enumerate(POOL_WINDOWS):
        a = xe[:, g * POOL_GROUP:(g + 1) * POOL_GROUP]
        s = _trailing_sums(a, w)[POOL_HALO:]
        out.append(s / _pool_counts(first_row, TM, w) - a[POOL_HALO:])
    return out


def _pool_fwd(pool_in, pool_w, pool_scale):
    S = pool_in.shape[0]
    TM = 512
    HB = TM // POOL_HALO

    def body(cur_ref, halo_ref, w_ref, sc_ref, y_ref):
        i = pl.program_id(0)
        halo = jnp.where(i > 0, halo_ref[...], 0.0)
        pooled = _pooled_groups(halo, cur_ref[...], i * TM)
        for g in range(len(POOL_WINDOWS)):
            sl = slice(g * POOL_GROUP, (g + 1) * POOL_GROUP)
            y = _dot(pooled[g].astype(BF16), w_ref[g].astype(BF16)) * sc_ref[:, sl]
            y_ref[:, sl] = y.astype(BF16)

    return pl.pallas_call(
        body, name="pool_fwd", grid=(S // TM,),
        in_specs=[pl.BlockSpec((TM, POOL_WIDTH), lambda i: (i, 0)),
                  pl.BlockSpec((POOL_HALO, POOL_WIDTH), lambda i: (jnp.maximum(i * HB - 1, 0), 0)),
                  _const_spec(pool_w.shape), _const_spec((1, POOL_WIDTH))],
        out_specs=pl.BlockSpec((TM, POOL_WIDTH), lambda i: (i, 0)),
        out_shape=jax.ShapeDtypeStruct((S, POOL_WIDTH), BF16),
        compiler_params=_params(("parallel",)),
    )(pool_in, pool_in, pool_w, pool_scale)


def _mix_out_fwd(attn, pool, w_out, x, g_post, g_ffn_pre):
    S, D = x.shape
    TM = 512

    def body(a_ref, p_ref, w_ref, x_ref, gp_ref, gf_ref, mixed_ref, x1_ref, h2_ref, cat_ref):
        ab = a_ref[...]
        cat_ref[:, :ATTN_WIDTH] = ab
        cat_ref[:, ATTN_WIDTH:] = p_ref[...]
        mixed = _dot(ab, w_ref[:ATTN_WIDTH, :]) + _dot(p_ref[...], w_ref[ATTN_WIDTH:, :])
        mixed_ref[...] = mixed.astype(BF16)
        n, _ = _rms_stats(mixed)
        x1 = x_ref[...] + n * gp_ref[...]
        x1_ref[...] = x1
        n2, _ = _rms_stats(x1)
        h2_ref[...] = (n2 * gf_ref[...]).astype(BF16)

    row = lambda w: pl.BlockSpec((TM, w), lambda i: (i, 0))
    return pl.pallas_call(
        body, name="mix_out_fwd", grid=(S // TM,),
        in_specs=[row(ATTN_WIDTH), row(POOL_WIDTH), _const_spec(w_out.shape), row(D),
                  _const_spec((1, D)), _const_spec((1, D))],
        out_specs=[row(D), row(D), row(D), row(D)],
        out_shape=[jax.ShapeDtypeStruct((S, D), BF16), jax.ShapeDtypeStruct((S, D), F32),
                   jax.ShapeDtypeStruct((S, D), BF16), jax.ShapeDtypeStruct((S, D), BF16)],
        compiler_params=_params(("parallel",), VMEM_LIMIT),
    )(attn, pool, w_out, x, g_post, g_ffn_pre)


def _ffn_fwd(h2, x1, target, w_up, w_down, conv_w, conv_b, g_post):
    S, D = x1.shape
    CW = w_up.shape[2]
    FF = 2 * CW
    TM = 256
    piece = 4 * LANES
    pieces = [(lo, min(lo + piece, CW)) for lo in range(0, CW, piece)]

    def body(h2_ref, x1_ref, t_ref, wu_ref, wd_ref, cw_ref, cb_ref, g_ref,
             yv_ref, dy_ref, df_ref, dc_ref, loss_ref, dg_ref, dcb_ref, dcw_ref,
             ue_s, dgate_s, dval_s):
        i = pl.program_id(0)

        @pl.when(i == 0)
        def _():
            loss_ref[...] = jnp.zeros_like(loss_ref)
            dg_ref[...] = jnp.zeros_like(dg_ref)
            dcb_ref[...] = jnp.zeros_like(dcb_ref)
            dcw_ref[...] = jnp.zeros_like(dcw_ref)
            ue_s[0:CONV_HALO, :] = jnp.zeros((CONV_HALO, 2 * FF), F32)

        @pl.when(i > 0)
        def _():
            ue_s[0:CONV_HALO, :] = ue_s[TM:TM + CONV_HALO, :]

        def shifted(cols, k):
            return pltpu.roll(ue_s[:, cols], k, 0)[CONV_HALO:]

        def conv(cols):
            return (cb_ref[:, cols] + cw_ref[2, :, cols] * ue_s[CONV_HALO:, cols]
                    + cw_ref[1, :, cols] * shifted(cols, 1) + cw_ref[0, :, cols] * shifted(cols, 2))

        hb = h2_ref[...]
        f = jnp.zeros((TM, D), F32)
        for j in range(2):
            jc = slice(j * CW, (j + 1) * CW)
            for half in range(2):
                blk = 2 * half + j
                cols = slice(blk * CW, (blk + 1) * CW)
                ue_s[CONV_HALO:, cols] = _dot(hb, wu_ref[blk])
            for lo, hi in pieces:
                pc = slice(j * CW + lo, j * CW + hi)
                gelu, dgelu = _gelu_tanh(conv(pc).astype(BF16))
                val = conv(slice(FF + j * CW + lo, FF + j * CW + hi)).astype(BF16)
                dgate_s[:, pc] = val * dgelu
                dval_s[:, pc] = gelu
                yv_ref[:, pc] = gelu * val
            f = f + _dot(yv_ref[:, jc], wd_ref[jc, :])

        n, r = _rms_stats(f)
        err = x1_ref[...] + n * g_ref[...] - t_ref[...]
        loss_ref[...] += 0.5 * jnp.sum(jnp.mean(err * err, axis=-1, keepdims=True), axis=0, keepdims=True)
        dy = err / D
        dy_ref[...] = dy
        df, dg = _rms_bwd(dy, n, r, g_ref[...])
        dg_ref[...] += dg
        dfb = df.astype(BF16)
        df_ref[...] = dfb

        for j in range(2):
            jc = slice(j * CW, (j + 1) * CW)
            dyv = _dot_nt(dfb, wd_ref[jc, :])
            for lo, hi in pieces:
                pc = slice(j * CW + lo, j * CW + hi)
                for half, scale_s in ((0, dgate_s), (1, dval_s)):
                    cols = slice(half * FF + j * CW + lo, half * FF + j * CW + hi)
                    dcv = dyv[:, lo:hi] * scale_s[:, pc].astype(F32)
                    dc_ref[:, cols] = dcv.astype(BF16)
                    dcb_ref[:, cols] += jnp.sum(dcv, axis=0, keepdims=True)
                    dcw_ref[2, :, cols] += jnp.sum(dcv * ue_s[CONV_HALO:, cols], axis=0, keepdims=True)
                    dcw_ref[1, :, cols] += jnp.sum(dcv * shifted(cols, 1), axis=0, keepdims=True)
                    dcw_ref[0, :, cols] += jnp.sum(dcv * shifted(cols, 2), axis=0, keepdims=True)

    row = lambda w: pl.BlockSpec((TM, w), lambda i: (i, 0))
    acc = lambda shape: pl.BlockSpec(shape, lambda i: (0,) * len(shape))
    return pl.pallas_call(
        body, name="ffn_fwd", grid=(S // TM,),
        in_specs=[row(D), row(D), row(D), _const_spec(w_up.shape), _const_spec(w_down.shape),
                  _const_spec(conv_w.shape), _const_spec((1, 2 * FF)), _const_spec((1, D))],
        out_specs=[row(FF), row(D), row(D), row(2 * FF),
                   acc((1, 1)), acc((1, D)), acc((1, 2 * FF)), acc((CONV_WIDTH, 1, 2 * FF))],
        out_shape=[jax.ShapeDtypeStruct((S, FF), BF16),
                   jax.ShapeDtypeStruct((S, D), F32), jax.ShapeDtypeStruct((S, D), BF16),
                   jax.ShapeDtypeStruct((S, 2 * FF), BF16),
                   jax.ShapeDtypeStruct((1, 1), F32), jax.ShapeDtypeStruct((1, D), F32),
                   jax.ShapeDtypeStruct((1, 2 * FF), F32), jax.ShapeDtypeStruct((CONV_WIDTH, 1, 2 * FF), F32)],
        scratch_shapes=[pltpu.VMEM((TM + CONV_HALO, 2 * FF), F32), pltpu.VMEM((TM, FF), BF16),
                        pltpu.VMEM((TM, FF), BF16)],
        compiler_params=_params(("arbitrary",), VMEM_LIMIT),
    )(h2, x1, target, w_up, w_down, conv_w, conv_b, g_post)


def _ffn_bwd(dc, conv_w, w_up, x1, g_ffn_pre, dy):
    S, D = x1.shape
    CW = w_up.shape[2]
    F2 = 4 * CW
    TM = 256
    HB = TM // CONV_HALO
    last = S // CONV_HALO - 1
    n_tiles = S // TM

    def body(dc_ref, halo_ref, cw_ref, wu_ref, x1_ref, g_ref, dy_ref, du_ref, dx1_ref, dg_ref):
        i = pl.program_id(0)

        @pl.when(i == 0)
        def _():
            dg_ref[...] = jnp.zeros_like(dg_ref)

        keep = i < n_tiles - 1
        dh2 = jnp.zeros((TM, D), F32)
        for blk in range(N_SHARD):
            cols = slice(blk * CW, (blk + 1) * CW)
            halo = jnp.where(keep, halo_ref[:, cols].astype(F32), 0.0)
            dce = jnp.concatenate([dc_ref[:, cols].astype(F32), halo], axis=0)
            rows = TM + CONV_HALO
            du = (cw_ref[2, :, cols] * dce[:TM]
                  + cw_ref[1, :, cols] * pltpu.roll(dce, rows - 1, 0)[:TM]
                  + cw_ref[0, :, cols] * pltpu.roll(dce, rows - 2, 0)[:TM])
            dub = du.astype(BF16)
            du_ref[:, cols] = dub
            dh2 = dh2 + _dot_nt(dub, wu_ref[blk])
        n2, r2 = _rms_stats(x1_ref[...])
        dx, dg = _rms_bwd(dh2, n2, r2, g_ref[...])
        dg_ref[...] += dg
        dx1_ref[...] = (dy_ref[...] + dx).astype(BF16)

    row = lambda w: pl.BlockSpec((TM, w), lambda i: (i, 0))
    return pl.pallas_call(
        body, name="ffn_bwd", grid=(S // TM,),
        in_specs=[row(F2), pl.BlockSpec((CONV_HALO, F2), lambda i: (jnp.minimum((i + 1) * HB, last), 0)),
                  _const_spec(conv_w.shape), _const_spec(w_up.shape), row(D), _const_spec((1, D)), row(D)],
        out_specs=[row(F2), row(D), pl.BlockSpec((1, D), lambda i: (0, 0))],
        out_shape=[jax.ShapeDtypeStruct((S, F2), BF16), jax.ShapeDtypeStruct((S, D), BF16),
                   jax.ShapeDtypeStruct((1, D), F32)],
        compiler_params=_params(("arbitrary",), VMEM_LIMIT),
    )(dc, dc, conv_w, w_up, x1, g_ffn_pre, dy)


def _matmul_tn(a, b, n_blocks, name):
    S, M = a.shape
    N = b.shape[1]
    tn = N // n_blocks
    tm = M if M <= 1024 else M // 2
    tk = 2048
    nk = S // tk

    def body(a_ref, b_ref, o_ref):
        @pl.when(pl.program_id(2) == 0)
        def _():
            o_ref[...] = jnp.zeros_like(o_ref)
        o_ref[0] += _dot_tn(a_ref[...], b_ref[...])

    return pl.pallas_call(
        body, name=name, grid=(M // tm, n_blocks, nk),
        in_specs=[pl.BlockSpec((tk, tm), lambda i, j, k: (k, i)), pl.BlockSpec((tk, tn), lambda i, j, k: (k, j))],
        out_specs=pl.BlockSpec((1, tm, tn), lambda i, j, k: (j, i, 0)),
        out_shape=jax.ShapeDtypeStruct((n_blocks, M, tn), F32),
        compiler_params=_params(("parallel", "parallel", "arbitrary"), VMEM_LIMIT),
    )(a, b)


def _mix_out_bwd(dx1, mixed, g_post, w_out, attn, cargo=()):
    S, D = dx1.shape
    TM = 512
    nd = len(DILATIONS)
    nc = len(cargo)
    kinds = [kind for kind, _ in cargo]
    n_chunks = ATTN_WIDTH // LANES

    def body(*refs):
        dx_ref, m_ref, g_ref, w_ref, a_ref = refs[:5]
        dm_ref, dp_ref, dg_ref = refs[5 + nc:8 + nc]
        da_refs, dl_refs = refs[8 + nc:8 + nc + nd], refs[8 + nc + nd:8 + nc + 2 * nd]
        n_out = 8 + nc + 2 * nd
        t_s = refs[n_out + nc:n_out + nc + n_chunks]
        c_s = refs[n_out + nc + n_chunks:n_out + nc + n_chunks + 1]
        cargo_refs = (kinds, refs[5:5 + nc], refs[n_out:n_out + nc], refs[n_out + nc + n_chunks + 1:])
        if nc:
            _cargo_start(*cargo_refs, pl.program_id(0) == 0)

        @pl.when(pl.program_id(0) == 0)
        def _():
            dg_ref[...] = jnp.zeros_like(dg_ref)

        n, r = _rms_stats(m_ref[...].astype(F32))
        dm, dg = _rms_bwd(dx_ref[...].astype(F32), n, r, g_ref[...])
        dg_ref[...] += dg
        dmb = dm.astype(BF16)
        dm_ref[...] = dmb
        da = _dot_nt(dmb, w_ref[:ATTN_WIDTH, :])
        _put_tokens(t_s, da)
        for i, d in enumerate(DILATIONS):
            _to_residue(da, t_s, da_refs[i], d, BF16)
        dp_ref[...] = _dot_nt(dmb, w_ref[ATTN_WIDTH:, :]).astype(BF16)
        gather = _head_stat_matrix(pick_first_lane=False)
        delta = sum(_dot(p, gather) for p in _bf16_pieces(da * a_ref[...].astype(F32), 2))
        _put_tokens(c_s, delta)
        for i, d in enumerate(DILATIONS):
            _to_residue(delta, c_s, dl_refs[i], d, F32)
        if nc:
            _cargo_finish(*cargo_refs, pl.program_id(0) == S // TM - 1)

    row = lambda w: pl.BlockSpec((TM, w), lambda i: (i, 0))
    specs = [_residue_spec(TM, d) for d in DILATIONS]
    arrays, cargo_specs, shapes, aliases, sems = _cargo_call(cargo, 5, 3 + 2 * nd)
    out = pl.pallas_call(
        body, name="mix_out_bwd", grid=(S // TM,),
        in_specs=[row(D), row(D), _const_spec((1, D)), _const_spec(w_out.shape), row(ATTN_WIDTH)] + cargo_specs,
        out_specs=[row(D), row(POOL_WIDTH), pl.BlockSpec((1, D), lambda i: (0, 0))] + specs
        + [_residue_spec(TM, d, STAT_WIDTH) for d in DILATIONS] + cargo_specs,
        out_shape=[jax.ShapeDtypeStruct((S, D), BF16), jax.ShapeDtypeStruct((S, POOL_WIDTH), BF16),
                   jax.ShapeDtypeStruct((1, D), F32)]
        + [_residue_shape(S, d, BF16) for d in DILATIONS]
        + [_residue_shape(S, d, F32, STAT_WIDTH) for d in DILATIONS] + shapes,
        input_output_aliases=aliases,
        scratch_shapes=_token_scratch(TM) + _token_scratch(TM, STAT_WIDTH) + sems,
        compiler_params=_params(("arbitrary",), VMEM_LIMIT),
    )(dx1, mixed, g_post, w_out, attn, *arrays)
    return out[0], out[1], out[2], out[3:3 + nd], out[3 + nd:3 + 2 * nd], out[3 + 2 * nd:]


def _pool_bwd(pool_in, d_pool, pool_w, pool_scale):
    S = pool_in.shape[0]
    TM = 512
    HB = TM // POOL_HALO
    last = S // POOL_HALO - 1
    G = len(POOL_WINDOWS)

    def body(cur_ref, halo_ref, dcur_ref, dnext_ref, w_ref, sc_ref, dxin_ref, dw_ref, dsc_ref):
        i = pl.program_id(0)

        @pl.when(i == 0)
        def _():
            dw_ref[...] = jnp.zeros_like(dw_ref)
            dsc_ref[...] = jnp.zeros_like(dsc_ref)

        halo = jnp.where(i > 0, halo_ref[...], 0.0)
        pooled = _pooled_groups(halo, cur_ref[...], i * TM)
        dnext = jnp.where(i < S // TM - 1, dnext_ref[...].astype(F32), 0.0)
        dye = jnp.concatenate([dcur_ref[...].astype(F32), dnext], axis=0)
        for g, w in enumerate(POOL_WINDOWS):
            sl = slice(g * POOL_GROUP, (g + 1) * POOL_GROUP)
            wg = w_ref[g].astype(BF16)
            pb = pooled[g].astype(BF16)
            dsc_ref[:, sl] += jnp.sum(dye[:TM, sl] * _dot(pb, wg), axis=0, keepdims=True)
            dpre = (dye[:, sl] * sc_ref[:, sl]).astype(BF16)
            dw_ref[g] += _dot_tn(pb, dpre[:TM])
            dpooled = _dot_nt(dpre, wg)
            z = dpooled / _pool_counts(i * TM, TM + POOL_HALO, w)
            dxin_ref[:, sl] = (_leading_sums(z, w)[:TM] - dpooled[:TM]).astype(BF16)

    row = pl.BlockSpec((TM, POOL_WIDTH), lambda i: (i, 0))
    return pl.pallas_call(
        body, name="pool_bwd", grid=(S // TM,),
        in_specs=[row, pl.BlockSpec((POOL_HALO, POOL_WIDTH), lambda i: (jnp.maximum(i * HB - 1, 0), 0)),
                  row, pl.BlockSpec((POOL_HALO, POOL_WIDTH), lambda i: (jnp.minimum((i + 1) * HB, last), 0)),
                  _const_spec(pool_w.shape), _const_spec((1, POOL_WIDTH))],
        out_specs=[row, pl.BlockSpec((G, POOL_GROUP, POOL_GROUP), lambda i: (0, 0, 0)),
                   pl.BlockSpec((1, POOL_WIDTH), lambda i: (0, 0))],
        out_shape=[jax.ShapeDtypeStruct((S, POOL_WIDTH), BF16), jax.ShapeDtypeStruct((G, POOL_GROUP, POOL_GROUP), F32),
                   jax.ShapeDtypeStruct((1, POOL_WIDTH), F32)],
        compiler_params=_params(("arbitrary",)),
    )(pool_in, pool_in, d_pool, d_pool, pool_w, pool_scale)


def _attn_bwd(q, k, v, d_attn, lse, delta, d, cargo=()):
    L = q.shape[0]
    nb = L // ATTN_BLOCK
    group = min(d, RESIDUES_PER_STEP)
    width = group * ATTN_WIDTH
    nc = len(cargo)
    kinds = [kind for kind, _ in cargo]

    def body(*refs):
        q_ref, kp_ref, kc_ref, vp_ref, vc_ref, do_ref, lse_ref, dl_ref = refs[:8]
        dq_ref, dk_ref, dv_ref = refs[8 + nc:11 + nc]
        ck_s, cv_s = refs[11 + 2 * nc:13 + 2 * nc]
        cargo_refs = (kinds, refs[8:8 + nc], refs[11 + nc:11 + 2 * nc], refs[13 + 2 * nc:])
        r, n = pl.program_id(0), pl.program_id(1)
        if nc:
            _cargo_start(*cargo_refs, (r == 0) & (n == 0))

        @pl.when(n == 0)
        def _():
            ck_s[...] = jnp.zeros_like(ck_s)
            cv_s[...] = jnp.zeros_like(cv_s)

        @pl.when(n < nb)
        def _():
            valid = _band_mask(n)
            valid2 = jnp.concatenate([valid, valid], axis=0)
            first = _first_head_lanes()

            def stacked_column(ref, hp):
                lane = 2 * hp * STAT_LANES
                return jnp.concatenate([ref[:, lane:lane + 1], ref[:, lane + STAT_LANES:lane + STAT_LANES + 1]], axis=0)

            for hp in range(width // LANES):
                sl = slice(hp * LANES, (hp + 1) * LANES)
                qq = _stack_heads(q_ref[:, sl], first)
                dd = _stack_heads(do_ref[:, sl], first)
                kk = jnp.concatenate([kp_ref[:, sl], kc_ref[:, sl]], axis=0)
                vv = jnp.concatenate([vp_ref[:, sl], vc_ref[:, sl]], axis=0)
                s = _dot_nt(qq, kk)
                p = jnp.where(valid2, jnp.exp(s - stacked_column(lse_ref, hp)), 0.0)
                dp = _dot_nt(dd, vv)
                ds = (p * (dp - stacked_column(dl_ref, hp))).astype(BF16)
                dq_ref[:, sl] = (_unstack_heads(_dot(ds, kk), first) * ATTN_SCALE).astype(BF16)
                dk = _dot_tn(ds, qq)
                dv = _dot_tn(p.astype(BF16), dd)
                dk_ref[:, sl] = (ck_s[:, sl] + dk[:ATTN_BLOCK]).astype(BF16)
                dv_ref[:, sl] = (cv_s[:, sl] + dv[:ATTN_BLOCK]).astype(BF16)
                ck_s[:, sl] = dk[ATTN_BLOCK:]
                cv_s[:, sl] = dv[ATTN_BLOCK:]

        @pl.when(n == nb)
        def _():
            dk_ref[...] = ck_s[...].astype(BF16)
            dv_ref[...] = cv_s[...].astype(BF16)

        if nc:
            _cargo_finish(*cargo_refs, (r == d // group - 1) & (n == nb))

    blk = (ATTN_BLOCK, width)
    cur = pl.BlockSpec(blk, lambda r, n: (jnp.minimum(n, nb - 1), r))
    stat = pl.BlockSpec((ATTN_BLOCK, group * STAT_WIDTH), lambda r, n: (jnp.minimum(n, nb - 1), r))
    prev = pl.BlockSpec(blk, lambda r, n: (jnp.maximum(jnp.minimum(n, nb - 1) - 1, 0), r))
    done = pl.BlockSpec(blk, lambda r, n: (jnp.maximum(n - 1, 0), r))
    arrays, specs, shapes, aliases, sems = _cargo_call(cargo, 8, 3)
    out = pl.pallas_call(
        body, name=f"attn_bwd_d{d}", grid=(d // group, nb + 1),
        in_specs=[cur, prev, cur, prev, cur, cur, stat, stat] + specs, out_specs=[cur, done, done] + specs,
        out_shape=[jax.ShapeDtypeStruct((L, d * ATTN_WIDTH), BF16)] * 3 + shapes,
        input_output_aliases=aliases,
        scratch_shapes=[pltpu.VMEM(blk, F32), pltpu.VMEM(blk, F32)] + sems,
        compiler_params=_params(("arbitrary", "arbitrary")),
    )(q, k, k, v, v, d_attn, lse, delta, *arrays)
    return out[:3], out[3:]


def _attn_bwd_consecutive(q, k, v, d_attn, lse, delta, d, cargo=()):
    L = q.shape[0]
    qb = CONSECUTIVE_BLOCKS
    steps = L // (qb * ATTN_BLOCK)
    nc = len(cargo)
    kinds = [kind for kind, _ in cargo]

    def body(*refs):
        q_ref, kp_ref, kc_ref, vp_ref, vc_ref, do_ref, lse_ref, dl_ref = refs[:8]
        dq_ref, dk_ref, dv_ref, ek_ref, ev_ref = refs[8 + nc:13 + nc]
        cargo_refs = (kinds, refs[8:8 + nc], refs[13 + nc:13 + 2 * nc], refs[13 + 2 * nc:])
        r, n = pl.program_id(0), pl.program_id(1)
        if nc:
            _cargo_start(*cargo_refs, (r == 0) & (n == 0))
        first = _first_head_lanes()
        for hp in range(ATTN_WIDTH // LANES):
            sl = slice(hp * LANES, (hp + 1) * LANES)
            for sub in range(qb):
                rows = slice(sub * ATTN_BLOCK, (sub + 1) * ATTN_BLOCK)
                valid = _band_mask(n if sub == 0 else 1)
                valid2 = jnp.concatenate([valid, valid], axis=0)
                if sub == 0:
                    kk = jnp.concatenate([kp_ref[:, sl], kc_ref[rows, sl]], axis=0)
                    vv = jnp.concatenate([vp_ref[:, sl], vc_ref[rows, sl]], axis=0)
                else:
                    keys = slice((sub - 1) * ATTN_BLOCK, (sub + 1) * ATTN_BLOCK)
                    kk, vv = kc_ref[keys, sl], vc_ref[keys, sl]
                qq = _stack_heads(q_ref[rows, sl], first)
                dd = _stack_heads(do_ref[rows, sl], first)
                lane = 2 * hp * STAT_LANES
                column = lambda ref: jnp.concatenate(
                    [ref[rows, lane:lane + 1], ref[rows, lane + STAT_LANES:lane + STAT_LANES + 1]], axis=0)
                p = jnp.where(valid2, jnp.exp(_dot_nt(qq, kk) - column(lse_ref)), 0.0)
                ds = (p * (_dot_nt(dd, vv) - column(dl_ref))).astype(BF16)
                dq_ref[rows, sl] = (_unstack_heads(_dot(ds, kk), first) * ATTN_SCALE).astype(BF16)
                dk = _dot_tn(ds, qq)
                dv = _dot_tn(p.astype(BF16), dd)
                if sub == 0:
                    ek_ref[:, sl] = dk[:ATTN_BLOCK].astype(BF16)
                    ev_ref[:, sl] = dv[:ATTN_BLOCK].astype(BF16)
                else:
                    before = slice((sub - 1) * ATTN_BLOCK, sub * ATTN_BLOCK)
                    dk_ref[before, sl] = (carry_k + dk[:ATTN_BLOCK]).astype(BF16)
                    dv_ref[before, sl] = (carry_v + dv[:ATTN_BLOCK]).astype(BF16)
                carry_k, carry_v = dk[ATTN_BLOCK:], dv[ATTN_BLOCK:]
            dk_ref[rows, sl] = carry_k.astype(BF16)
            dv_ref[rows, sl] = carry_v.astype(BF16)
        if nc:
            _cargo_finish(*cargo_refs, (r == d - 1) & (n == steps - 1))

    cur = pl.BlockSpec((qb * ATTN_BLOCK, ATTN_WIDTH), lambda r, n: (n, r))
    prev = pl.BlockSpec((ATTN_BLOCK, ATTN_WIDTH), lambda r, n: (jnp.maximum(n * qb - 1, 0), r))
    edge = pl.BlockSpec((ATTN_BLOCK, ATTN_WIDTH), lambda r, n: (n, r))
    stat = pl.BlockSpec((qb * ATTN_BLOCK, STAT_WIDTH), lambda r, n: (n, r))
    arrays, specs, shapes, aliases, sems = _cargo_call(cargo, 8, 5)
    out = pl.pallas_call(
        body, name=f"attn_bwd_d{d}", grid=(d, steps),
        in_specs=[cur, prev, cur, prev, cur, cur, stat, stat] + specs, out_specs=[cur, cur, cur, edge, edge] + specs,
        out_shape=[jax.ShapeDtypeStruct((L, d * ATTN_WIDTH), BF16)] * 3
        + [jax.ShapeDtypeStruct((steps * ATTN_BLOCK, d * ATTN_WIDTH), BF16)] * 2 + shapes,
        input_output_aliases=aliases, scratch_shapes=sems,
        compiler_params=_params(("arbitrary", "arbitrary")),
    )(q, k, k, v, v, d_attn, lse, delta, *arrays)
    return out[:3], out[3:5], out[5:]


def _mix_in_bwd(dqkv, edges, d_pool_in, w_in, x, g_pre, dx1):
    S, D = x.shape
    TM = CONSECUTIVE_BLOCKS * ATTN_BLOCK
    nd = len(DILATIONS)
    n_tiles = S // TM

    def body(*refs):
        g_refs = refs[:3 * nd]
        e_refs = (None,) + refs[3 * nd:3 * nd + 2]
        dpi_ref, w_ref, x_ref, g_ref, dx1_ref, dproj_ref, gx_ref, dg_ref = refs[3 * nd + 2:3 * nd + 10]
        t_s = refs[3 * nd + 10:]

        @pl.when(pl.program_id(0) == 0)
        def _():
            dg_ref[...] = jnp.zeros_like(dg_ref)

        dh = jnp.zeros((TM, D), F32)
        for a in range(4):
            if a < 3:
                tot = g_refs[a][...].astype(F32)
                if a > 0:
                    late = jnp.where(pl.program_id(0) < n_tiles - 1, e_refs[a][...].astype(F32), 0.0)
                    tot = jnp.concatenate([tot[:TM - ATTN_BLOCK], tot[TM - ATTN_BLOCK:] + late], axis=0)
                for i, d in enumerate(DILATIONS[1:]):
                    tot = tot + _from_residue(g_refs[3 * (i + 1) + a], t_s, d)
                db = tot.astype(BF16)
            else:
                db = dpi_ref[...]
            dproj_ref[:, a * ATTN_WIDTH:(a + 1) * ATTN_WIDTH] = db
            dh = dh + _dot_nt(db, w_ref[a])
        n, r = _rms_stats(x_ref[...])
        dx, dg = _rms_bwd(dh, n, r, g_ref[...])
        dg_ref[...] += dg
        gx_ref[...] = dx1_ref[...].astype(F32) + dx

    row = lambda w: pl.BlockSpec((TM, w), lambda i: (i, 0))
    edge = pl.BlockSpec((ATTN_BLOCK, ATTN_WIDTH), lambda i: (jnp.minimum(i + 1, n_tiles - 1), 0))
    return pl.pallas_call(
        body, name="mix_in_bwd", grid=(S // TM,),
        in_specs=[_residue_spec(TM, d) for d in DILATIONS for _ in range(3)] + [edge, edge]
        + [row(POOL_WIDTH), _const_spec(w_in.shape), row(D), _const_spec((1, D)), row(D)],
        out_specs=[row(4 * ATTN_WIDTH), row(D), pl.BlockSpec((1, D), lambda i: (0, 0))],
        out_shape=[jax.ShapeDtypeStruct((S, 4 * ATTN_WIDTH), BF16), jax.ShapeDtypeStruct((S, D), F32),
                   jax.ShapeDtypeStruct((1, D), F32)],
        scratch_shapes=_token_scratch(TM),
        compiler_params=_params(("arbitrary",), VMEM_LIMIT),
    )(*[g for gs in dqkv for g in gs], *edges, d_pool_in, w_in, x, g_pre, dx1)


SMALL_EARLY = ("pool_w", "pool_scale", "g_mix_post", "g_ffn_pre", "conv_b", "g_ffn_post", "conv_w")
SMALL_LATE = ("g_mix_pre",)


def _pack_small(grads, names):
    parts = []
    for n in names:
        g = grads[n]
        if n == "conv_w":
            g = g.reshape(CONV_WIDTH, N_SHARD, -1).transpose(1, 0, 2)
        parts.append(g.reshape(-1, LANES))
    return jnp.concatenate(parts, axis=0) if len(parts) > 1 else parts[0]


def _unpack_small(packed, names, like, shard):
    out, row = {}, 0
    for n in names:
        size = like[n].size * (N_SHARD if n == "conv_w" else 1)
        g = packed[row:row + size // LANES]
        row += size // LANES
        if n == "conv_w":
            g = lax.dynamic_slice_in_dim(g.reshape((N_SHARD,) + like[n].shape), shard, 1, axis=0)[0]
        out[n] = g.reshape(like[n].shape)
    return out


def _local_step(x, target, g_mix_pre, w_in, pool_w, pool_scale, w_out, g_mix_post, g_ffn_pre,
                w_up, conv_w, conv_b, w_down, g_ffn_post, mesh_pos=None):
    on_mesh = mesh_pos is not None
    D = x.shape[1]
    CW = w_up.shape[2]
    qkv, pool_in, h1, got = _mix_in_fwd(x, g_mix_pre, w_in, [("ici", w_up)] if on_mesh else ())
    w_up = got[0] if on_mesh else w_up
    o1, l1, got = _attn_fwd(*qkv[0], 1, [("d2d", w_up), ("ici", w_out)] if on_mesh else ())
    w_up, w_out = got if on_mesh else (w_up, w_out)
    o4, l4, got = _attn_fwd(*qkv[1], 4, [("d2d", w_out), ("ici", w_down)] if on_mesh else ())
    w_out, w_down = got if on_mesh else (w_out, w_down)
    o16, l16, got = _attn_fwd(*qkv[2], 16, [("d2d", w_down)] if on_mesh else ())
    w_down = got[0] if on_mesh else w_down
    w_out = w_out.reshape(D, D)
    w_down = w_down.reshape(2 * CW, D)
    attn, lse = _attn_mix((o1, o4, o16), (l1, l4, l16))
    pool = _pool_fwd(pool_in, pool_w, pool_scale)
    mixed, x1, h2, cat = _mix_out_fwd(attn, pool, w_out, x, g_mix_post, g_ffn_pre)

    yv, dy, df, dc, loss, d_g_ffn_post, d_conv_b, d_conv_w = _ffn_fwd(
        h2, x1, target, w_up, w_down, conv_w, conv_b, g_ffn_post)
    du, dx1, d_g_ffn_pre = _ffn_bwd(dc, conv_w, w_up, x1, g_ffn_pre, dy)
    d_w_up = _matmul_tn(h2, du, N_SHARD, "grad_w_up")
    d_w_down = _matmul_tn(yv, df, 1, "grad_w_down")[0].reshape(N_SHARD, CW // 2, D)
    swap = [("swap", d_w_up), ("swap", d_w_down)] if on_mesh else ()
    d_mixed, d_pool, d_g_mix_post, d_attn, delta, from_sibling = _mix_out_bwd(dx1, mixed, g_mix_post, w_out, attn, swap)
    d_w_out = _matmul_tn(cat, d_mixed, 1, "grad_w_out")[0].reshape(N_SHARD, D // N_SHARD, D)
    d_pool_in, d_pool_w, d_pool_scale = _pool_bwd(pool_in, d_pool, pool_w, pool_scale)
    grads = dict(pool_w=d_pool_w, pool_scale=d_pool_scale, w_out=d_w_out, g_mix_post=d_g_mix_post,
                 g_ffn_pre=d_g_ffn_pre, w_up=d_w_up, conv_w=d_conv_w, conv_b=d_conv_b, w_down=d_w_down,
                 g_ffn_post=d_g_ffn_post)
    cargo = [(), (), ()]
    if on_mesh:
        c_arr, device, shard_arr = mesh_pos
        up_f32, up_bf16 = _pair_sum(d_w_up, from_sibling[0], c_arr, "pair_sum_w_up")
        down_f32, down_bf16 = _pair_sum(d_w_down, from_sibling[1], c_arr, "pair_sum_w_down")
        early = _pack_small(grads, SMALL_EARLY)
        early_slots = lax.dynamic_update_index_in_dim(jnp.zeros((8,) + early.shape, F32), early, device, 0)
        cargo = [[("scatter", down_bf16), ("everyone", early_slots)], [("scatter", up_bf16)], []]

    dqkv1, edges, landed1 = _attn_bwd_consecutive(*qkv[0], d_attn[0], lse[0], delta[0], 1, cargo[0])
    dqkv4, landed4 = _attn_bwd(*qkv[1], d_attn[1], lse[1], delta[1], 4, cargo[1])
    if on_mesh:
        halves = [_shard_sum(up_f32, landed4[0], shard_arr, c_arr, "shard_sum_w_up"),
                  _shard_sum(down_f32, landed1[0], shard_arr, c_arr, "shard_sum_w_down")]
        cargo[2] = cargo[2] + [("join", h) for h in halves]
    if qkv[2][0].shape[0] == CONSECUTIVE_BLOCKS * ATTN_BLOCK:
        dqkv16, _, landed16 = _attn_bwd_consecutive(*qkv[2], d_attn[2], lse[2], delta[2], 16, cargo[2])
    else:
        dqkv16, landed16 = _attn_bwd(*qkv[2], d_attn[2], lse[2], delta[2], 16, cargo[2])
    if on_mesh:
        grads.update(small_early=landed1[1], w_up=landed16[0], w_down=landed16[1])
    d_proj, grad_x, grads["g_mix_pre"] = _mix_in_bwd((dqkv1, dqkv4, dqkv16), edges, d_pool_in, w_in, x, g_mix_pre, dx1)
    grads["w_in"] = _matmul_tn(h1, d_proj, N_SHARD, "grad_w_in")
    return loss, grad_x, grads


ANY = pl.BlockSpec(memory_space=pl.ANY)


def _position():
    x, y, c = lax.axis_index("x"), lax.axis_index("y"), lax.axis_index("c")
    chips = [(1 - x, y), (x, 1 - y), (1 - x, 1 - y)]
    return x, y, c, chips


def _remote(src, dst, send_sem, recv_sem, to):
    return pltpu.make_async_remote_copy(src_ref=src, dst_ref=dst, send_sem=send_sem, recv_sem=recv_sem,
                                        device_id=to, device_id_type=MESH)


def _cast_bf16(w, shard_arr, name):
    R, C = w.shape
    tr = R // 2

    def body(s_ref, w_ref, o_ref):
        o_ref[0] = w_ref[...].astype(BF16)

    return pl.pallas_call(
        body, name=name,
        grid_spec=pltpu.PrefetchScalarGridSpec(
            num_scalar_prefetch=1, grid=(2,),
            in_specs=[pl.BlockSpec((tr, C), lambda i, s_ref: (i, 0))],
            out_specs=pl.BlockSpec((1, tr, C), lambda i, s_ref: (s_ref[0], i, 0))),
        out_shape=jax.ShapeDtypeStruct((N_SHARD, R, C), BF16),
        compiler_params=_params(("parallel",)))(shard_arr, w)


def _gather_weights(bufs):
    n = len(bufs) - 1

    def body(*refs):
        outs, cw_out = refs[n + 1:2 * n + 1], refs[2 * n + 1]
        ici_send, ici_recv, d2d_send, d2d_recv = refs[2 * n + 2:]
        x, y, c, chips = _position()
        s = 2 * x + y
        sibling = (x, y, 1 - c)

        def half(a, shard, h):
            rows = outs[a].shape[1] // 2
            return outs[a].at[shard, pl.ds(h * rows, rows), :]

        sends = []
        for a in range(n):
            for j, (px, py) in enumerate(chips):
                sends.append(_remote(half(a, s, c), half(a, s, c),
                                     ici_send.at[3 * a + j], ici_recv.at[3 * a + j], (px, py, c)))
        for j, (px, py) in enumerate(chips):
            sends.append(_remote(cw_out.at[s], cw_out.at[s], ici_send.at[3 * n + j], ici_recv.at[3 * n + j], (px, py, c)))
        for cp in sends:
            cp.start()
        passed = []
        for a in range(n):
            for j, (px, py) in enumerate(chips):
                sj = 2 * px + py
                got = half(a, sj, c)
                _remote(got, got, ici_send.at[3 * a + j], ici_recv.at[3 * a + j], (px, py, c)).wait_recv()
                fwd = _remote(got, got, d2d_send.at[3 * a + j], d2d_recv.at[3 * a + j], sibling)
                fwd.start()
                passed.append(fwd)
        for j, (px, py) in enumerate(chips):
            got = cw_out.at[2 * px + py]
            _remote(got, got, ici_send.at[3 * n + j], ici_recv.at[3 * n + j], (px, py, c)).wait_recv()
        for a in range(n):
            for j, (px, py) in enumerate(chips):
                got = half(a, 2 * px + py, 1 - c)
                _remote(got, got, d2d_send.at[3 * a + j], d2d_recv.at[3 * a + j], sibling).wait_recv()
        for cp in sends + passed:
            cp.wait_send()

    return pl.pallas_call(
        body, name="gather_weights",
        in_specs=[ANY] * (n + 1), out_specs=[ANY] * (n + 1),
        out_shape=[jax.ShapeDtypeStruct(b.shape, b.dtype) for b in bufs],
        input_output_aliases={i: i for i in range(n + 1)},
        scratch_shapes=[pltpu.SemaphoreType.DMA((3 * n + 3,)), pltpu.SemaphoreType.DMA((3 * n + 3,)),
                        pltpu.SemaphoreType.DMA((3 * n,)), pltpu.SemaphoreType.DMA((3 * n,))],
        compiler_params=pltpu.CompilerParams(has_side_effects=True),
    )(*bufs)


def _swap_halves(grads, tag):
    n = len(grads)

    def body(*refs):
        ins, outs, send_sem, recv_sem = refs[:n], refs[n:2 * n], refs[2 * n], refs[2 * n + 1]
        x, y, c, _ = _position()
        copies = []
        for a in range(n):
            rows = ins[a].shape[1] // 2
            copies.append(_remote(ins[a].at[:, pl.ds((1 - c) * rows, rows), :], outs[a],
                                  send_sem.at[a], recv_sem.at[a], (x, y, 1 - c)))
        for cp in copies:
            cp.start()
        for cp in copies:
            cp.wait()

    return pl.pallas_call(
        body, name="swap_grad_halves_" + tag,
        in_specs=[ANY] * n, out_specs=[ANY] * n,
        out_shape=[jax.ShapeDtypeStruct((g.shape[0], g.shape[1] // 2, g.shape[2]), F32) for g in grads],
        scratch_shapes=[pltpu.SemaphoreType.DMA((n,)), pltpu.SemaphoreType.DMA((n,))],
        compiler_params=pltpu.CompilerParams(has_side_effects=True),
    )(*grads)


def _pair_sum(g, got, c_arr, name):
    n_sh, R, C = g.shape
    rows = R // 2

    def body(c_ref, g_ref, r_ref, f_ref, b_ref):
        t = g_ref[...] + r_ref[...]
        f_ref[...] = t
        b_ref[...] = t.astype(BF16)

    blk = pl.BlockSpec((1, rows, C), lambda i, c_ref: (i, 0, 0))
    return pl.pallas_call(
        body, name=name,
        grid_spec=pltpu.PrefetchScalarGridSpec(
            num_scalar_prefetch=1, grid=(n_sh,),
            in_specs=[pl.BlockSpec((1, rows, C), lambda i, c_ref: (i, c_ref[0], 0)), blk],
            out_specs=[blk, blk]),
        out_shape=[jax.ShapeDtypeStruct((n_sh, rows, C), F32), jax.ShapeDtypeStruct((n_sh, rows, C), BF16)],
        compiler_params=_params(("parallel",)),
    )(c_arr, g, got)


def _shard_sum(sums_f32, recv, shard_arr, c_arr, name):
    _, rows, C = sums_f32.shape

    def body(s_ref, c_ref, o_ref, r_ref, t_ref):
        t_ref[...] = ((o_ref[0] + r_ref[0].astype(F32)) + r_ref[1].astype(F32)) + r_ref[2].astype(F32)

    return pl.pallas_call(
        body, name=name,
        grid_spec=pltpu.PrefetchScalarGridSpec(
            num_scalar_prefetch=2, grid=(1,),
            in_specs=[pl.BlockSpec((1, rows, C), lambda i, s_ref, c_ref: (s_ref[0], 0, 0)),
                      pl.BlockSpec((3, rows, C), lambda i, s_ref, c_ref: (0, 0, 0))],
            out_specs=pl.BlockSpec((rows, C), lambda i, s_ref, c_ref: (c_ref[0], 0))),
        out_shape=jax.ShapeDtypeStruct((2 * rows, C), F32),
        compiler_params=_params(("arbitrary",)),
    )(shard_arr, c_arr, sums_f32, recv)


def _join_halves(bufs):
    n = len(bufs)

    def body(*refs):
        outs, send_sem, recv_sem = refs[n:2 * n], refs[2 * n], refs[2 * n + 1]
        x, y, c, _ = _position()
        copies = []
        for a in range(n):
            rows = outs[a].shape[0] // 2
            mine = outs[a].at[pl.ds(c * rows, rows), :]
            copies.append(_remote(mine, mine, send_sem.at[a], recv_sem.at[a], (x, y, 1 - c)))
        for cp in copies:
            cp.start()
        for a, cp in enumerate(copies):
            cp.wait_send()
            rows = outs[a].shape[0] // 2
            theirs = outs[a].at[pl.ds((1 - c) * rows, rows), :]
            _remote(theirs, theirs, send_sem.at[a], recv_sem.at[a], (x, y, 1 - c)).wait_recv()

    return pl.pallas_call(
        body, name="join_grad_halves",
        in_specs=[ANY] * n, out_specs=[ANY] * n,
        out_shape=[jax.ShapeDtypeStruct(b.shape, F32) for b in bufs],
        input_output_aliases={i: i for i in range(n)},
        scratch_shapes=[pltpu.SemaphoreType.DMA((n,)), pltpu.SemaphoreType.DMA((n,))],
        compiler_params=pltpu.CompilerParams(has_side_effects=True),
    )(*bufs)


def _small_sum(parts, tag):
    _, R, C = parts.shape

    def body(p_ref, o_ref):
        t = p_ref[0]
        for k in range(1, 8):
            t = t + p_ref[k]
        o_ref[...] = t

    return pl.pallas_call(
        body, name="small_grad_sum_" + tag, grid=(1,),
        in_specs=[pl.BlockSpec((8, R, C), lambda i: (0, 0, 0))], out_specs=pl.BlockSpec((R, C), lambda i: (0, 0)),
        out_shape=jax.ShapeDtypeStruct((R, C), F32), compiler_params=_params(("arbitrary",)),
    )(parts)


def _adamw_math(w, g, m, v):
    m = ADAM_B1 * m + (1.0 - ADAM_B1) * g
    v = ADAM_B2 * v + (1.0 - ADAM_B2) * (g * g)
    m_hat = m / (1.0 - ADAM_B1 ** ADAM_STEP)
    v_hat = v / (1.0 - ADAM_B2 ** ADAM_STEP)
    delta = -ADAM_LR * (m_hat / (jnp.sqrt(v_hat) + ADAM_EPS) + ADAM_WD * w)
    return delta, m, v


def _adamw_big(ws, gs, ms, vs, name, cargo=()):
    n = len(ws)
    nc = len(cargo)
    kinds = [kind for kind, _ in cargo]

    def body(*refs):
        ins, outs = refs[:4 * n], refs[4 * n + nc:7 * n + nc]
        cargo_refs = (kinds, refs[4 * n:4 * n + nc], refs[7 * n + nc:7 * n + 2 * nc], refs[7 * n + 2 * nc:])
        if nc:
            _cargo_start(*cargo_refs, pl.program_id(0) == 0)
        for a in range(n):
            w, g, m, v = (ins[k * n + a][...] for k in range(4))
            outs[a][...], outs[n + a][...], outs[2 * n + a][...] = _adamw_math(w, g, m, v)
        if nc:
            _cargo_finish(*cargo_refs, pl.program_id(0) == 3)

    blks = [pl.BlockSpec((w.shape[0] // 4, w.shape[1]), lambda i: (i, 0)) for w in ws]
    arrays, cargo_specs, shapes, aliases, sems = _cargo_call(cargo, 4 * n, 3 * n)
    out = pl.pallas_call(
        body, name=name, grid=(4,), in_specs=blks * 4 + cargo_specs, out_specs=blks * 3 + cargo_specs,
        out_shape=[jax.ShapeDtypeStruct(w.shape, F32) for w in ws] * 3 + shapes,
        input_output_aliases=aliases, scratch_shapes=sems,
        compiler_params=_params(("arbitrary",)),
    )(*ws, *gs, *ms, *vs, *arrays)
    return (out[:n], out[n:2 * n], out[2 * n:3 * n]), out[3 * n:]


def _adamw_small(ws, gs, ms, vs):
    n = len(ws)

    def body(*refs):
        for a in range(n):
            w, g, m, v = (refs[k * n + a][...] for k in range(4))
            d, nm, nv = _adamw_math(w, g, m, v)
            refs[4 * n + a][...] = d
            refs[5 * n + a][...] = nm
            refs[6 * n + a][...] = nv

    shapes = [jax.ShapeDtypeStruct(w.shape, F32) for w in ws]
    out = pl.pallas_call(body, name="adamw_small", out_shape=shapes * 3)(*ws, *gs, *ms, *vs)
    return out[:n], out[n:2 * n], out[2 * n:]


BIG = ("w_in", "w_out", "w_up", "w_down")
SMALL = ("g_mix_pre", "pool_w", "pool_scale", "g_mix_post", "g_ffn_pre", "conv_b", "g_ffn_post", "conv_w")
ORDER = ("g_mix_pre", "w_in", "pool_w", "pool_scale", "w_out", "g_mix_post", "g_ffn_pre", "w_up", "conv_w", "conv_b",
         "w_down", "g_ffn_post")


def kernel(x, g_mix_pre, w_in, pool_w, pool_scale, w_out, g_mix_post, g_ffn_pre, w_up, conv_w, conv_b, w_down, g_ffn_post, loss_target, m_g_mix_pre, m_w_in, m_pool_w, m_pool_scale, m_w_out, m_g_mix_post, m_g_ffn_pre, m_w_up, m_conv_w, m_conv_b, m_w_down, m_g_ffn_post, v_g_mix_pre, v_w_in, v_pool_w, v_pool_scale, v_w_out, v_g_mix_post, v_g_ffn_pre, v_w_up, v_conv_w, v_conv_b, v_w_down, v_g_ffn_post):
    args = dict(locals())
    W = {n: args[n][0] for n in ORDER}
    M = {n: args["m_" + n][0] for n in ORDER}
    V = {n: args["v_" + n][0] for n in ORDER}
    for d in (W, M, V):
        d["pool_w"] = d["pool_w"].reshape(-1, POOL_GROUP)
        for n in ("g_mix_pre", "pool_scale", "g_mix_post", "g_ffn_pre", "conv_b", "g_ffn_post"):
            d[n] = d[n].reshape(1, -1)
    CW = W["w_up"].shape[1]
    c_arr = lax.axis_index("c").astype(jnp.int32).reshape(1)
    shard = 2 * lax.axis_index("x") + lax.axis_index("y")
    shard_arr = shard.astype(jnp.int32).reshape(1)
    device = 2 * shard + lax.axis_index("c")

    conv_w_slots = lax.dynamic_update_index_in_dim(jnp.zeros((N_SHARD,) + W["conv_w"].shape, F32), W["conv_w"], shard, 0)
    slots = {n: _cast_bf16(W[n], shard_arr, "cast_" + n) for n in BIG}
    w_in_g, conv_w_g = _gather_weights([slots["w_in"], conv_w_slots])
    conv_w_full = conv_w_g.transpose(1, 0, 2).reshape(CONV_WIDTH, 1, N_SHARD * CW)

    loss, grad_x, G = _local_step(
        x[0], loss_target[0], W["g_mix_pre"], w_in_g, W["pool_w"].reshape(-1, POOL_GROUP, POOL_GROUP), W["pool_scale"],
        slots["w_out"], W["g_mix_post"], W["g_ffn_pre"], slots["w_up"], conv_w_full, W["conv_b"],
        slots["w_down"], W["g_ffn_post"], (c_arr, device, shard_arr))

    late, ffn = ("w_in", "w_out"), ("w_up", "w_down")
    from_sibling = _swap_halves([G[n] for n in late], "mix")
    sums = {n: _pair_sum(G[n], r, c_arr, "pair_sum_" + n) for n, r in zip(late, from_sibling)}
    loss_rows = jnp.pad(loss, ((0, 7), (0, LANES - 1)))
    small = jnp.concatenate([_pack_small(G, SMALL_LATE), loss_rows], axis=0)
    small_slots = lax.dynamic_update_index_in_dim(jnp.zeros((8,) + small.shape, F32), small, device, 0)
    pick = lambda d, names: [d[n] for n in names]
    delta, new_m, new_v = {}, {}, {}
    updates, landed = _adamw_big(pick(W, ffn), pick(G, ffn), pick(M, ffn), pick(V, ffn), "adamw_ffn",
                                 [("scatter", sums[n][1]) for n in late] + [("everyone", small_slots)])
    halves = [_shard_sum(sums[n][0], r, shard_arr, c_arr, "shard_sum_" + n) for n, r in zip(late, landed[:2])]
    full = dict(zip(late, _join_halves(halves)))
    full.update({n: G[n] for n in ffn})
    full.update(_unpack_small(_small_sum(G["small_early"], "early"), SMALL_EARLY, W, shard))
    late_total = _small_sum(landed[2], "late")
    full.update(_unpack_small(late_total, SMALL_LATE, W, shard))
    loss = late_total[-8, 0]

    for names, (ds, nms, nvs) in ((ffn, updates),
                                  (late, _adamw_big(pick(W, late), pick(full, late), pick(M, late), pick(V, late), "adamw_mix")[0]),
                                  (SMALL, _adamw_small(pick(W, SMALL), pick(full, SMALL), pick(M, SMALL), pick(V, SMALL)))):
        for n, d, nm, nv in zip(names, ds, nms, nvs):
            delta[n], new_m[n], new_v[n] = d, nm, nv

    shaped = lambda d: [d[n].reshape(args[n].shape) for n in ORDER]
    return (loss, grad_x[None], *shaped(full), *shaped(delta), *shaped(new_m), *shaped(new_v))
```

```python
import functools

import jax
import jax.numpy as jnp
from jax import lax
from jax.experimental import pallas as pl
from jax.experimental.pallas import tpu as pltpu

F32 = jnp.float32
BF16 = jnp.bfloat16

RMS_EPS = 1e-6
NEG_INF = -1e30
N_HEADS = 8
HEAD_DIM = 64
ATTN_WIDTH = N_HEADS * HEAD_DIM
ATTN_SCALE = HEAD_DIM ** -0.5
ATTN_BLOCK = 128
DILATIONS = (1, 4, 16)
RESIDUES_PER_STEP = 4
CONSECUTIVE_BLOCKS = 4
POOL_WINDOWS = (2, 4, 8, 16)
POOL_GROUP = 128
POOL_WIDTH = POOL_GROUP * len(POOL_WINDOWS)
POOL_HALO = 16
CONV_WIDTH = 3
CONV_HALO = 8
N_SHARD = 4
LANES = 128
STAT_LANES = 16
STAT_WIDTH = N_HEADS * STAT_LANES

ADAM_LR = 0.001
ADAM_B1 = 0.9
ADAM_B2 = 0.999
ADAM_EPS = 1e-08
ADAM_WD = 0.01
ADAM_STEP = 10

VMEM_LIMIT = 60 * 1024 * 1024
MESH = pl.DeviceIdType.MESH
NT = (((1,), (1,)), ((), ()))
TN = (((0,), (0,)), ((), ()))


def _params(sem, vmem=None):
    return pltpu.CompilerParams(dimension_semantics=sem, vmem_limit_bytes=vmem)


def _const_spec(shape):
    zeros = (0,) * len(shape)
    return pl.BlockSpec(shape, lambda *_: zeros, pipeline_mode=pl.Buffered(1))


def _dot(a, b):
    return jnp.dot(a, b, preferred_element_type=F32)


def _dot_nt(a, b):
    return lax.dot_general(a, b, NT, preferred_element_type=F32)


def _dot_tn(a, b):
    return lax.dot_general(a, b, TN, preferred_element_type=F32)


def _rms_stats(x):
    r = lax.rsqrt(jnp.mean(x * x, axis=-1, keepdims=True) + RMS_EPS)
    return x * r, r


def _rms_bwd(dy, n, r, g):
    dg = jnp.sum(dy * n, axis=0, keepdims=True)
    dn = dy * g
    dx = r * (dn - n * jnp.mean(dn * n, axis=-1, keepdims=True))
    return dx, dg


def _gelu_tanh(g):
    k = 0.7978845608028654
    kc = k * 0.044715
    g2 = g * g
    t = jnp.tanh(g * (k + kc * g2))
    h = 0.5 * t + 0.5
    dh = (0.5 - 0.5 * (t * t)) * (k + (3.0 * kc) * g2)
    return g * h, h + g * dh


def _residue_shape(S, d, dtype, width=ATTN_WIDTH):
    return jax.ShapeDtypeStruct((S // d, d * width), dtype)


def _residue_spec(TM, d, width=ATTN_WIDTH):
    return pl.BlockSpec((TM // d, d * width), lambda i: (i, 0))


def _token_scratch(TM, width=ATTN_WIDTH):
    return [pltpu.VMEM((TM, LANES), F32)] * (width // LANES)


def _head_sum_matrix():
    r = lax.broadcasted_iota(jnp.int32, (ATTN_WIDTH, STAT_WIDTH), 0)
    c = lax.broadcasted_iota(jnp.int32, (ATTN_WIDTH, STAT_WIDTH), 1)
    return (r // HEAD_DIM == c // STAT_LANES).astype(BF16)


def _head_spread_matrix():
    c = lax.broadcasted_iota(jnp.int32, (STAT_WIDTH, ATTN_WIDTH), 0)
    r = lax.broadcasted_iota(jnp.int32, (STAT_WIDTH, ATTN_WIDTH), 1)
    return (c == (r // HEAD_DIM) * STAT_LANES).astype(BF16)


def _bf16_pieces(x, n):
    pieces = []
    for _ in range(n):
        p = x.astype(BF16)
        pieces.append(p)
        x = x - p.astype(F32)
    return pieces


def _put_tokens(dst_s, val):
    for cb, chunk in enumerate(dst_s):
        chunk[...] = val[:, cb * LANES:(cb + 1) * LANES]


def _get_tokens(src_s):
    return jnp.concatenate([chunk[...] for chunk in src_s], axis=1)


def _to_residue(val, src_s, out_ref, d, dtype):
    if d == 1:
        out_ref[...] = val.astype(dtype)
        return
    rows = src_s[0].shape[0]
    for r in range(d):
        for cb, chunk in enumerate(src_s):
            col = (r * len(src_s) + cb) * LANES
            out_ref[:, col:col + LANES] = chunk[pl.ds(r, rows // d, stride=d), :].astype(dtype)


def _from_residue(in_ref, dst_s, d):
    if d == 1:
        return in_ref[...].astype(F32)
    rows = dst_s[0].shape[0]
    for r in range(d):
        for cb, chunk in enumerate(dst_s):
            col = (r * len(dst_s) + cb) * LANES
            chunk[pl.ds(r, rows // d, stride=d), :] = in_ref[:, col:col + LANES].astype(F32)
    return _get_tokens(dst_s)


def _mix_in_fwd(x, g_pre, w_in, cargo=()):
    S, D = x.shape
    TM = 512
    nc = len(cargo)
    kinds = [kind for kind, _ in cargo]
    n_chunks = ATTN_WIDTH // LANES

    def body(x_ref, g_ref, w_ref, *refs):
        cargo_in, refs = refs[:nc], refs[nc:]
        qkv_refs, p_ref, h_ref = refs[:9], refs[9], refs[10]
        t_s = refs[11 + nc:11 + nc + n_chunks]
        cargo_refs = (kinds, cargo_in, refs[11:11 + nc], refs[11 + nc + n_chunks:])
        if nc:
            _cargo_start(*cargo_refs, pl.program_id(0) == 0)
        n, _ = _rms_stats(x_ref[...])
        hb = (n * g_ref[...]).astype(BF16)
        h_ref[...] = hb
        for a in range(3):
            res = _dot(hb, w_ref[a])
            if a == 0:
                res = res * ATTN_SCALE
            _put_tokens(t_s, res)
            for i, d in enumerate(DILATIONS):
                _to_residue(res, t_s, qkv_refs[3 * i + a], d, BF16)
        p_ref[...] = _dot(hb, w_ref[3])
        if nc:
            _cargo_finish(*cargo_refs, pl.program_id(0) == S // TM - 1)

    row = lambda w: pl.BlockSpec((TM, w), lambda i: (i, 0))
    arrays, cargo_specs, shapes, aliases, sems = _cargo_call(cargo, 3, 11)
    out = pl.pallas_call(
        body, name="mix_in_fwd", grid=(S // TM,),
        in_specs=[row(D), _const_spec((1, D)), _const_spec(w_in.shape)] + cargo_specs,
        out_specs=[_residue_spec(TM, d) for d in DILATIONS for _ in range(3)] + [row(POOL_WIDTH), row(D)] + cargo_specs,
        out_shape=[_residue_shape(S, d, BF16) for d in DILATIONS for _ in range(3)]
        + [jax.ShapeDtypeStruct((S, POOL_WIDTH), F32), jax.ShapeDtypeStruct((S, D), BF16)] + shapes,
        input_output_aliases=aliases,
        scratch_shapes=_token_scratch(TM) + sems,
        compiler_params=_params(("arbitrary",), VMEM_LIMIT),
    )(x, g_pre, w_in, *arrays)
    return [out[0:3], out[3:6], out[6:9]], out[9], out[10], out[11:]


def _band_mask(n):
    qi = lax.broadcasted_iota(jnp.int32, (ATTN_BLOCK, 2 * ATTN_BLOCK), 0)
    ki = lax.broadcasted_iota(jnp.int32, (ATTN_BLOCK, 2 * ATTN_BLOCK), 1)
    dist = qi + ATTN_BLOCK - ki
    return (dist >= 0) & (dist <= ATTN_BLOCK) & ((ki >= ATTN_BLOCK) | (n > 0))


def _first_head_lanes():
    return lax.broadcasted_iota(jnp.int32, (1, LANES), 1) < HEAD_DIM


def _stack_heads(pair, first):
    zero = jnp.zeros_like(pair)
    return jnp.concatenate([jnp.where(first, pair, zero), jnp.where(first, zero, pair)], axis=0)


def _unstack_heads(stacked, first):
    return jnp.where(first, stacked[:ATTN_BLOCK], stacked[ATTN_BLOCK:])


CARGO_COPIES = {"ici": 3, "d2d": 3, "scatter": 3, "swap": 1, "everyone": 7, "join": 1}
CARGO_IN_PLACE = ("ici", "d2d", "everyone", "join")


def _cargo_copies(kinds, ins, outs, send_sems, recv_sems, want_recvs=True):
    x, y, c, chips = _position()
    s = 2 * x + y
    me = 2 * s + c
    sibling = (x, y, 1 - c)
    sends, recvs = [], []

    def add(k, src, dst, landing, to):
        sends.append(_remote(src, dst, send_sems.at[k], recv_sems.at[k], to))
        if want_recvs:
            recvs.append(_remote(landing, landing, send_sems.at[k], recv_sems.at[k], to))

    k0 = 0
    for a, kind in enumerate(kinds):
        if kind == "swap":
            rows = ins[a].shape[1] // 2
            add(k0, ins[a].at[:, pl.ds((1 - c) * rows, rows), :], outs[a], outs[a], sibling)
        elif kind == "join":
            rows = outs[a].shape[0] // 2
            mine = outs[a].at[pl.ds(c * rows, rows), :]
            add(k0, mine, mine, outs[a].at[pl.ds((1 - c) * rows, rows), :], sibling)
        elif kind == "everyone":
            for m in range(1, 8):
                peer = (x ^ (m >> 2), y ^ ((m >> 1) & 1), c ^ (m & 1))
                add(k0 + m - 1, outs[a].at[me], outs[a].at[me], outs[a].at[4 * peer[0] + 2 * peer[1] + peer[2]], peer)
        else:
            for j, (px, py) in enumerate(chips):
                sj = 2 * px + py
                if kind == "scatter":
                    add(k0 + j, ins[a].at[sj], outs[a].at[j], outs[a].at[j], (px, py, c))
                    continue
                buf = outs[a]
                rows = buf.shape[1] // 2
                half = lambda shard, h: buf.at[shard, pl.ds(h * rows, rows), :]
                if kind == "ici":
                    add(k0 + j, half(s, c), half(s, c), half(sj, c), (px, py, c))
                else:
                    add(k0 + j, half(sj, c), half(sj, c), half(sj, 1 - c), sibling)
        k0 += CARGO_COPIES[kind]
    return sends, recvs


def _cargo_start(kinds, ins, outs, sems, first_step):
    @pl.when(first_step)
    def _():
        for cp in _cargo_copies(kinds, ins, outs, *sems, want_recvs=False)[0]:
            cp.start()


def _cargo_finish(kinds, ins, outs, sems, last_step):
    @pl.when(last_step)
    def _():
        sends, recvs = _cargo_copies(kinds, ins, outs, *sems)
        for cp in sends:
            cp.wait_send()
        for cp in recvs:
            cp.wait_recv()


def _cargo_call(cargo, n_in, n_out):
    arrays = [a for _, a in cargo]
    shapes = []
    for kind, a in cargo:
        if kind == "scatter":
            shape = (3,) + a.shape[1:]
        elif kind == "swap":
            shape = (a.shape[0], a.shape[1] // 2, a.shape[2])
        else:
            shape = a.shape
        shapes.append(jax.ShapeDtypeStruct(shape, a.dtype))
    aliases = {n_in + i: n_out + i for i, (kind, _) in enumerate(cargo) if kind in CARGO_IN_PLACE}
    n_sems = sum(CARGO_COPIES[kind] for kind, _ in cargo)
    sems = [pltpu.SemaphoreType.DMA((n_sems,))] * 2 if cargo else []
    return arrays, [ANY] * len(cargo), shapes, aliases, sems


def _attn_fwd(q, k, v, d, cargo=()):
    L = q.shape[0]
    group = min(d, RESIDUES_PER_STEP)
    width = group * ATTN_WIDTH
    qb = RESIDUES_PER_STEP // group
    steps = L // (qb * ATTN_BLOCK)
    nc = len(cargo)
    kinds = [kind for kind, _ in cargo]

    def body(*refs):
        q_ref, kp_ref, kc_ref, vp_ref, vc_ref = refs[:5]
        o_ref, lse_ref = refs[5 + nc:7 + nc]
        cargo_refs = (kinds, refs[5:5 + nc], refs[7 + nc:7 + 2 * nc], refs[7 + 2 * nc:])
        r, n = pl.program_id(0), pl.program_id(1)
        if nc:
            _cargo_start(*cargo_refs, (r == 0) & (n == 0))
        first = _first_head_lanes()
        for sub in range(qb):
            rows = slice(sub * ATTN_BLOCK, (sub + 1) * ATTN_BLOCK)
            valid = _band_mask(n if sub == 0 else 1)
            valid2 = jnp.concatenate([valid, valid], axis=0)
            for hp in range(width // LANES):
                sl = slice(hp * LANES, (hp + 1) * LANES)
                if sub == 0:
                    kk = jnp.concatenate([kp_ref[:, sl], kc_ref[rows, sl]], axis=0)
                    vv = jnp.concatenate([vp_ref[:, sl], vc_ref[rows, sl]], axis=0)
                else:
                    keys = slice((sub - 1) * ATTN_BLOCK, (sub + 1) * ATTN_BLOCK)
                    kk, vv = kc_ref[keys, sl], vc_ref[keys, sl]
                s = jnp.where(valid2, _dot_nt(_stack_heads(q_ref[rows, sl], first), kk), NEG_INF)
                m = jnp.max(s, axis=-1, keepdims=True)
                p = jnp.exp(s - m)
                den = jnp.sum(p, axis=-1, keepdims=True)
                o_ref[rows, sl] = _unstack_heads(_dot(p.astype(BF16), vv) / den, first).astype(BF16)
                lse = m + jnp.log(den)
                lane = 2 * hp * STAT_LANES
                for half in range(2):
                    lse_ref[rows, lane + half * STAT_LANES:lane + (half + 1) * STAT_LANES] = jnp.broadcast_to(
                        lse[half * ATTN_BLOCK:(half + 1) * ATTN_BLOCK], (ATTN_BLOCK, STAT_LANES))
        if nc:
            _cargo_finish(*cargo_refs, (r == d // group - 1) & (n == steps - 1))

    cur = pl.BlockSpec((qb * ATTN_BLOCK, width), lambda r, n: (n, r))
    prev = pl.BlockSpec((ATTN_BLOCK, width), lambda r, n: (jnp.maximum(n * qb - 1, 0), r))
    arrays, specs, shapes, aliases, sems = _cargo_call(cargo, 5, 2)
    out = pl.pallas_call(
        body, name=f"attn_fwd_d{d}", grid=(d // group, steps),
        in_specs=[cur, prev, cur, prev, cur] + specs,
        out_specs=[cur, pl.BlockSpec((qb * ATTN_BLOCK, group * STAT_WIDTH), lambda r, n: (n, r))] + specs,
        out_shape=[jax.ShapeDtypeStruct((L, d * ATTN_WIDTH), BF16), jax.ShapeDtypeStruct((L, d * STAT_WIDTH), F32)] + shapes,
        input_output_aliases=aliases, scratch_shapes=sems,
        compiler_params=_params(("arbitrary", "arbitrary")),
    )(q, k, k, v, v, *arrays)
    return out[0], out[1], out[2:]


def _attn_mix(outs, lses):
    S = outs[0].shape[0]
    TM = 512
    n = len(DILATIONS)

    def body(*refs):
        o_refs, l_refs, attn_ref, lse_refs = refs[:n], refs[n:2 * n], refs[2 * n], refs[2 * n + 1:3 * n + 1]
        t_s, c_s = refs[3 * n + 1:-1], refs[-1:]
        os = [_from_residue(o_refs[i], t_s, d) for i, d in enumerate(DILATIONS)]
        ls = [_from_residue(l_refs[i], c_s, d) for i, d in enumerate(DILATIONS)]
        m = jnp.maximum(jnp.maximum(ls[0], ls[1]), ls[2])
        es = [jnp.exp(l - m) for l in ls]
        den = es[0] + es[1] + es[2]
        spread = _head_spread_matrix()
        ws = [sum(_dot(p, spread) for p in _bf16_pieces(e / den, 2)) for e in es]
        attn_ref[...] = (ws[0] * os[0] + ws[1] * os[1] + ws[2] * os[2]).astype(BF16)
        lse = m + jnp.log(den)
        _put_tokens(c_s, lse)
        for i, d in enumerate(DILATIONS):
            _to_residue(lse, c_s, lse_refs[i], d, F32)

    specs = [_residue_spec(TM, d) for d in DILATIONS]
    stats = [_residue_spec(TM, d, STAT_WIDTH) for d in DILATIONS]
    out = pl.pallas_call(
        body, name="attn_mix", grid=(S // TM,),
        in_specs=specs + stats, out_specs=[specs[0]] + stats,
        out_shape=[jax.ShapeDtypeStruct((S, ATTN_WIDTH), BF16)]
        + [_residue_shape(S, d, F32, STAT_WIDTH) for d in DILATIONS],
        scratch_shapes=_token_scratch(TM) + _token_scratch(TM, STAT_WIDTH),
        compiler_params=_params(("parallel",)),
    )(*outs, *lses)
    return out[0], out[1:]


def _pool_counts(first_row, rows, w):
    t = first_row + lax.broadcasted_iota(jnp.int32, (rows, 1), 0)
    return jnp.minimum(t + 1, w).astype(F32)


def _trailing_sums(xe, w):
    s, k = xe, 1
    while k < w:
        s = s + pltpu.roll(s, k, 0)
        k *= 2
    return s


def _leading_sums(xe, w):
    rows = xe.shape[0]
    s, k = xe, 1
    while k < w:
        s = s + pltpu.roll(s, rows - k, 0)
        k *= 2
    return s


def _pooled_groups(halo, cur, first_row):
    TM = cur.shape[0]
    xe = jnp.concatenate([halo, cur], axis=0)
    out = []
    for g, w in enumerate(POOL_WINDOWS):
        a = xe[:, g * POOL_GROUP:(g + 1) * POOL_GROUP]
        s = _trailing_sums(a, w)[POOL_HALO:]
        out.append(s / _pool_counts(first_row, TM, w) - a[POOL_HALO:])
    return out


def _pool_fwd(pool_in, pool_w, pool_scale):
    S = pool_in.shape[0]
    TM = 512
    HB = TM // POOL_HALO

    def body(cur_ref, halo_ref, w_ref, sc_ref, y_ref):
        i = pl.program_id(0)
        halo = jnp.where(i > 0, halo_ref[...], 0.0)
        pooled = _pooled_groups(halo, cur_ref[...], i * TM)
        for g in range(len(POOL_WINDOWS)):
            sl = slice(g * POOL_GROUP, (g + 1) * POOL_GROUP)
            y = _dot(pooled[g].astype(BF16), w_ref[g].astype(BF16)) * sc_ref[:, sl]
            y_ref[:, sl] = y.astype(BF16)

    return pl.pallas_call(
        body, name="pool_fwd", grid=(S // TM,),
        in_specs=[pl.BlockSpec((TM, POOL_WIDTH), lambda i: (i, 0)),
                  pl.BlockSpec((POOL_HALO, POOL_WIDTH), lambda i: (jnp.maximum(i * HB - 1, 0), 0)),
                  _const_spec(pool_w.shape), _const_spec((1, POOL_WIDTH))],
        out_specs=pl.BlockSpec((TM, POOL_WIDTH), lambda i: (i, 0)),
        out_shape=jax.ShapeDtypeStruct((S, POOL_WIDTH), BF16),
        compiler_params=_params(("parallel",)),
    )(pool_in, pool_in, pool_w, pool_scale)


def _mix_out_fwd(attn, pool, w_out, x, g_post, g_ffn_pre):
    S, D = x.shape
    TM = 512

    def body(a_ref, p_ref, w_ref, x_ref, gp_ref, gf_ref, mixed_ref, x1_ref, h2_ref, cat_ref):
        ab = a_ref[...]
        cat_ref[:, :ATTN_WIDTH] = ab
        cat_ref[:, ATTN_WIDTH:] = p_ref[...]
        mixed = _dot(ab, w_ref[:ATTN_WIDTH, :]) + _dot(p_ref[...], w_ref[ATTN_WIDTH:, :])
        mixed_ref[...] = mixed.astype(BF16)
        n, _ = _rms_stats(mixed)
        x1 = x_ref[...] + n * gp_ref[...]
        x1_ref[...] = x1
        n2, _ = _rms_stats(x1)
        h2_ref[...] = (n2 * gf_ref[...]).astype(BF16)

    row = lambda w: pl.BlockSpec((TM, w), lambda i: (i, 0))
    return pl.pallas_call(
        body, name="mix_out_fwd", grid=(S // TM,),
        in_specs=[row(ATTN_WIDTH), row(POOL_WIDTH), _const_spec(w_out.shape), row(D),
                  _const_spec((1, D)), _const_spec((1, D))],
        out_specs=[row(D), row(D), row(D), row(D)],
        out_shape=[jax.ShapeDtypeStruct((S, D), BF16), jax.ShapeDtypeStruct((S, D), F32),
                   jax.ShapeDtypeStruct((S, D), BF16), jax.ShapeDtypeStruct((S, D), BF16)],
        compiler_params=_params(("parallel",), VMEM_LIMIT),
    )(attn, pool, w_out, x, g_post, g_ffn_pre)


def _ffn_fwd(h2, x1, target, w_up, w_down, conv_w, conv_b, g_post):
    S, D = x1.shape
    CW = w_up.shape[2]
    FF = 2 * CW
    TM = 256
    piece = 4 * LANES
    pieces = [(lo, min(lo + piece, CW)) for lo in range(0, CW, piece)]

    def body(h2_ref, x1_ref, t_ref, wu_ref, wd_ref, cw_ref, cb_ref, g_ref,
             yv_ref, dy_ref, df_ref, dc_ref, loss_ref, dg_ref, dcb_ref, dcw_ref,
             ue_s, dgate_s, dval_s):
        i = pl.program_id(0)

        @pl.when(i == 0)
        def _():
            loss_ref[...] = jnp.zeros_like(loss_ref)
            dg_ref[...] = jnp.zeros_like(dg_ref)
            dcb_ref[...] = jnp.zeros_like(dcb_ref)
            dcw_ref[...] = jnp.zeros_like(dcw_ref)
            ue_s[0:CONV_HALO, :] = jnp.zeros((CONV_HALO, 2 * FF), F32)

        @pl.when(i > 0)
        def _():
            ue_s[0:CONV_HALO, :] = ue_s[TM:TM + CONV_HALO, :]

        def shifted(cols, k):
            return pltpu.roll(ue_s[:, cols], k, 0)[CONV_HALO:]

        def conv(cols):
            return (cb_ref[:, cols] + cw_ref[2, :, cols] * ue_s[CONV_HALO:, cols]
                    + cw_ref[1, :, cols] * shifted(cols, 1) + cw_ref[0, :, cols] * shifted(cols, 2))

        hb = h2_ref[...]
        f = jnp.zeros((TM, D), F32)
        for j in range(2):
            jc = slice(j * CW, (j + 1) * CW)
            for half in range(2):
                blk = 2 * half + j
                cols = slice(blk * CW, (blk + 1) * CW)
                ue_s[CONV_HALO:, cols] = _dot(hb, wu_ref[blk])
            for lo, hi in pieces:
                pc = slice(j * CW + lo, j * CW + hi)
                gelu, dgelu = _gelu_tanh(conv(pc).astype(BF16))
                val = conv(slice(FF + j * CW + lo, FF + j * CW + hi)).astype(BF16)
                dgate_s[:, pc] = val * dgelu
                dval_s[:, pc] = gelu
                yv_ref[:, pc] = gelu * val
            f = f + _dot(yv_ref[:, jc], wd_ref[jc, :])

        n, r = _rms_stats(f)
        err = x1_ref[...] + n * g_ref[...] - t_ref[...]
        loss_ref[...] += 0.5 * jnp.sum(jnp.mean(err * err, axis=-1, keepdims=True), axis=0, keepdims=True)
        dy = err / D
        dy_ref[...] = dy
        df, dg = _rms_bwd(dy, n, r, g_ref[...])
        dg_ref[...] += dg
        dfb = df.astype(BF16)
        df_ref[...] = dfb

        for j in range(2):
            jc = slice(j * CW, (j + 1) * CW)
            dyv = _dot_nt(dfb, wd_ref[jc, :])
            for lo, hi in pieces:
                pc = slice(j * CW + lo, j * CW + hi)
                for half, scale_s in ((0, dgate_s), (1, dval_s)):
                    cols = slice(half * FF + j * CW + lo, half * FF + j * CW + hi)
                    dcv = dyv[:, lo:hi] * scale_s[:, pc].astype(F32)
                    dc_ref[:, cols] = dcv.astype(BF16)
                    dcb_ref[:, cols] += jnp.sum(dcv, axis=0, keepdims=True)
                    dcw_ref[2, :, cols] += jnp.sum(dcv * ue_s[CONV_HALO:, cols], axis=0, keepdims=True)
                    dcw_ref[1, :, cols] += jnp.sum(dcv * shifted(cols, 1), axis=0, keepdims=True)
                    dcw_ref[0, :, cols] += jnp.sum(dcv * shifted(cols, 2), axis=0, keepdims=True)

    row = lambda w: pl.BlockSpec((TM, w), lambda i: (i, 0))
    acc = lambda shape: pl.BlockSpec(shape, lambda i: (0,) * len(shape))
    return pl.pallas_call(
        body, name="ffn_fwd", grid=(S // TM,),
        in_specs=[row(D), row(D), row(D), _const_spec(w_up.shape), _const_spec(w_down.shape),
                  _const_spec(conv_w.shape), _const_spec((1, 2 * FF)), _const_spec((1, D))],
        out_specs=[row(FF), row(D), row(D), row(2 * FF),
                   acc((1, 1)), acc((1, D)), acc((1, 2 * FF)), acc((CONV_WIDTH, 1, 2 * FF))],
        out_shape=[jax.ShapeDtypeStruct((S, FF), BF16),
                   jax.ShapeDtypeStruct((S, D), F32), jax.ShapeDtypeStruct((S, D), BF16),
                   jax.ShapeDtypeStruct((S, 2 * FF), BF16),
                   jax.ShapeDtypeStruct((1, 1), F32), jax.ShapeDtypeStruct((1, D), F32),
                   jax.ShapeDtypeStruct((1, 2 * FF), F32), jax.ShapeDtypeStruct((CONV_WIDTH, 1, 2 * FF), F32)],
        scratch_shapes=[pltpu.VMEM((TM + CONV_HALO, 2 * FF), F32), pltpu.VMEM((TM, FF), BF16),
                        pltpu.VMEM((TM, FF), BF16)],
        compiler_params=_params(("arbitrary",), VMEM_LIMIT),
    )(h2, x1, target, w_up, w_down, conv_w, conv_b, g_post)


def _ffn_bwd(dc, conv_w, w_up, x1, g_ffn_pre, dy):
    S, D = x1.shape
    CW = w_up.shape[2]
    F2 = 4 * CW
    TM = 256
    HB = TM // CONV_HALO
    last = S // CONV_HALO - 1
    n_tiles = S // TM

    def body(dc_ref, halo_ref, cw_ref, wu_ref, x1_ref, g_ref, dy_ref, du_ref, dx1_ref, dg_ref):
        i = pl.program_id(0)

        @pl.when(i == 0)
        def _():
            dg_ref[...] = jnp.zeros_like(dg_ref)

        keep = i < n_tiles - 1
        dh2 = jnp.zeros((TM, D), F32)
        for blk in range(N_SHARD):
            cols = slice(blk * CW, (blk + 1) * CW)
            halo = jnp.where(keep, halo_ref[:, cols].astype(F32), 0.0)
            dce = jnp.concatenate([dc_ref[:, cols].astype(F32), halo], axis=0)
            rows = TM + CONV_HALO
            du = (cw_ref[2, :, cols] * dce[:TM]
                  + cw_ref[1, :, cols] * pltpu.roll(dce, rows - 1, 0)[:TM]
                  + cw_ref[0, :, cols] * pltpu.roll(dce, rows - 2, 0)[:TM])
            dub = du.astype(BF16)
            du_ref[:, cols] = dub
            dh2 = dh2 + _dot_nt(dub, wu_ref[blk])
        n2, r2 = _rms_stats(x1_ref[...])
        dx, dg = _rms_bwd(dh2, n2, r2, g_ref[...])
        dg_ref[...] += dg
        dx1_ref[...] = (dy_ref[...] + dx).astype(BF16)

    row = lambda w: pl.BlockSpec((TM, w), lambda i: (i, 0))
    return pl.pallas_call(
        body, name="ffn_bwd", grid=(S // TM,),
        in_specs=[row(F2), pl.BlockSpec((CONV_HALO, F2), lambda i: (jnp.minimum((i + 1) * HB, last), 0)),
                  _const_spec(conv_w.shape), _const_spec(w_up.shape), row(D), _const_spec((1, D)), row(D)],
        out_specs=[row(F2), row(D), pl.BlockSpec((1, D), lambda i: (0, 0))],
        out_shape=[jax.ShapeDtypeStruct((S, F2), BF16), jax.ShapeDtypeStruct((S, D), BF16),
                   jax.ShapeDtypeStruct((1, D), F32)],
        compiler_params=_params(("arbitrary",), VMEM_LIMIT),
    )(dc, dc, conv_w, w_up, x1, g_ffn_pre, dy)


def _matmul_tn(a, b, n_blocks, name):
    S, M = a.shape
    N = b.shape[1]
    tn = N // n_blocks
    tm = M if M <= 1024 else M // 2
    tk = 2048
    nk = S // tk

    def body(a_ref, b_ref, o_ref):
        @pl.when(pl.program_id(2) == 0)
        def _():
            o_ref[...] = jnp.zeros_like(o_ref)
        o_ref[0] += _dot_tn(a_ref[...], b_ref[...])

    return pl.pallas_call(
        body, name=name, grid=(M // tm, n_blocks, nk),
        in_specs=[pl.BlockSpec((tk, tm), lambda i, j, k: (k, i)), pl.BlockSpec((tk, tn), lambda i, j, k: (k, j))],
        out_specs=pl.BlockSpec((1, tm, tn), lambda i, j, k: (j, i, 0)),
        out_shape=jax.ShapeDtypeStruct((n_blocks, M, tn), F32),
        compiler_params=_params(("parallel", "parallel", "arbitrary"), VMEM_LIMIT),
    )(a, b)


def _mix_out_bwd(dx1, mixed, g_post, w_out, attn, cargo=()):
    S, D = dx1.shape
    TM = 512
    nd = len(DILATIONS)
    nc = len(cargo)
    kinds = [kind for kind, _ in cargo]
    n_chunks = ATTN_WIDTH // LANES

    def body(*refs):
        dx_ref, m_ref, g_ref, w_ref, a_ref = refs[:5]
        dm_ref, dp_ref, dg_ref = refs[5 + nc:8 + nc]
        da_refs, dl_refs = refs[8 + nc:8 + nc + nd], refs[8 + nc + nd:8 + nc + 2 * nd]
        n_out = 8 + nc + 2 * nd
        t_s = refs[n_out + nc:n_out + nc + n_chunks]
        c_s = refs[n_out + nc + n_chunks:n_out + nc + n_chunks + 1]
        cargo_refs = (kinds, refs[5:5 + nc], refs[n_out:n_out + nc], refs[n_out + nc + n_chunks + 1:])
        if nc:
            _cargo_start(*cargo_refs, pl.program_id(0) == 0)

        @pl.when(pl.program_id(0) == 0)
        def _():
            dg_ref[...] = jnp.zeros_like(dg_ref)

        n, r = _rms_stats(m_ref[...].astype(F32))
        dm, dg = _rms_bwd(dx_ref[...].astype(F32), n, r, g_ref[...])
        dg_ref[...] += dg
        dmb = dm.astype(BF16)
        dm_ref[...] = dmb
        da = _dot_nt(dmb, w_ref[:ATTN_WIDTH, :])
        _put_tokens(t_s, da)
        for i, d in enumerate(DILATIONS):
            _to_residue(da, t_s, da_refs[i], d, BF16)
        dp_ref[...] = _dot_nt(dmb, w_ref[ATTN_WIDTH:, :]).astype(BF16)
        gather = _head_sum_matrix()
        delta = sum(_dot(p, gather) for p in _bf16_pieces(da * a_ref[...].astype(F32), 2))
        _put_tokens(c_s, delta)
        for i, d in enumerate(DILATIONS):
            _to_residue(delta, c_s, dl_refs[i], d, F32)
        if nc:
            _cargo_finish(*cargo_refs, pl.program_id(0) == S // TM - 1)

    row = lambda w: pl.BlockSpec((TM, w), lambda i: (i, 0))
    specs = [_residue_spec(TM, d) for d in DILATIONS]
    arrays, cargo_specs, shapes, aliases, sems = _cargo_call(cargo, 5, 3 + 2 * nd)
    out = pl.pallas_call(
        body, name="mix_out_bwd", grid=(S // TM,),
        in_specs=[row(D), row(D), _const_spec((1, D)), _const_spec(w_out.shape), row(ATTN_WIDTH)] + cargo_specs,
        out_specs=[row(D), row(POOL_WIDTH), pl.BlockSpec((1, D), lambda i: (0, 0))] + specs
        + [_residue_spec(TM, d, STAT_WIDTH) for d in DILATIONS] + cargo_specs,
        out_shape=[jax.ShapeDtypeStruct((S, D), BF16), jax.ShapeDtypeStruct((S, POOL_WIDTH), BF16),
                   jax.ShapeDtypeStruct((1, D), F32)]
        + [_residue_shape(S, d, BF16) for d in DILATIONS]
        + [_residue_shape(S, d, F32, STAT_WIDTH) for d in DILATIONS] + shapes,
        input_output_aliases=aliases,
        scratch_shapes=_token_scratch(TM) + _token_scratch(TM, STAT_WIDTH) + sems,
        compiler_params=_params(("arbitrary",), VMEM_LIMIT),
    )(dx1, mixed, g_post, w_out, attn, *arrays)
    return out[0], out[1], out[2], out[3:3 + nd], out[3 + nd:3 + 2 * nd], out[3 + 2 * nd:]


def _pool_bwd(pool_in, d_pool, pool_w, pool_scale):
    S = pool_in.shape[0]
    TM = 512
    HB = TM // POOL_HALO
    last = S // POOL_HALO - 1
    G = len(POOL_WINDOWS)

    def body(cur_ref, halo_ref, dcur_ref, dnext_ref, w_ref, sc_ref, dxin_ref, dw_ref, dsc_ref):
        i = pl.program_id(0)

        @pl.when(i == 0)
        def _():
            dw_ref[...] = jnp.zeros_like(dw_ref)
            dsc_ref[...] = jnp.zeros_like(dsc_ref)

        halo = jnp.where(i > 0, halo_ref[...], 0.0)
        pooled = _pooled_groups(halo, cur_ref[...], i * TM)
        dnext = jnp.where(i < S // TM - 1, dnext_ref[...].astype(F32), 0.0)
        dye = jnp.concatenate([dcur_ref[...].astype(F32), dnext], axis=0)
        for g, w in enumerate(POOL_WINDOWS):
            sl = slice(g * POOL_GROUP, (g + 1) * POOL_GROUP)
            wg = w_ref[g].astype(BF16)
            pb = pooled[g].astype(BF16)
            dsc_ref[:, sl] += jnp.sum(dye[:TM, sl] * _dot(pb, wg), axis=0, keepdims=True)
            dpre = (dye[:, sl] * sc_ref[:, sl]).astype(BF16)
            dw_ref[g] += _dot_tn(pb, dpre[:TM])
            dpooled = _dot_nt(dpre, wg)
            z = dpooled / _pool_counts(i * TM, TM + POOL_HALO, w)
            dxin_ref[:, sl] = (_leading_sums(z, w)[:TM] - dpooled[:TM]).astype(BF16)

    row = pl.BlockSpec((TM, POOL_WIDTH), lambda i: (i, 0))
    return pl.pallas_call(
        body, name="pool_bwd", grid=(S // TM,),
        in_specs=[row, pl.BlockSpec((POOL_HALO, POOL_WIDTH), lambda i: (jnp.maximum(i * HB - 1, 0), 0)),
                  row, pl.BlockSpec((POOL_HALO, POOL_WIDTH), lambda i: (jnp.minimum((i + 1) * HB, last), 0)),
                  _const_spec(pool_w.shape), _const_spec((1, POOL_WIDTH))],
        out_specs=[row, pl.BlockSpec((G, POOL_GROUP, POOL_GROUP), lambda i: (0, 0, 0)),
                   pl.BlockSpec((1, POOL_WIDTH), lambda i: (0, 0))],
        out_shape=[jax.ShapeDtypeStruct((S, POOL_WIDTH), BF16), jax.ShapeDtypeStruct((G, POOL_GROUP, POOL_GROUP), F32),
                   jax.ShapeDtypeStruct((1, POOL_WIDTH), F32)],
        compiler_params=_params(("arbitrary",)),
    )(pool_in, pool_in, d_pool, d_pool, pool_w, pool_scale)


def _attn_bwd(q, k, v, d_attn, lse, delta, d, cargo=()):
    L = q.shape[0]
    nb = L // ATTN_BLOCK
    group = min(d, RESIDUES_PER_STEP)
    width = group * ATTN_WIDTH
    nc = len(cargo)
    kinds = [kind for kind, _ in cargo]

    def body(*refs):
        q_ref, kp_ref, kc_ref, vp_ref, vc_ref, do_ref, lse_ref, dl_ref = refs[:8]
        dq_ref, dk_ref, dv_ref = refs[8 + nc:11 + nc]
        ck_s, cv_s = refs[11 + 2 * nc:13 + 2 * nc]
        cargo_refs = (kinds, refs[8:8 + nc], refs[11 + nc:11 + 2 * nc], refs[13 + 2 * nc:])
        r, n = pl.program_id(0), pl.program_id(1)
        if nc:
            _cargo_start(*cargo_refs, (r == 0) & (n == 0))

        @pl.when(n == 0)
        def _():
            ck_s[...] = jnp.zeros_like(ck_s)
            cv_s[...] = jnp.zeros_like(cv_s)

        @pl.when(n < nb)
        def _():
            valid = _band_mask(n)
            valid2 = jnp.concatenate([valid, valid], axis=0)
            first = _first_head_lanes()

            def stacked_column(ref, hp):
                lane = 2 * hp * STAT_LANES
                return jnp.concatenate([ref[:, lane:lane + 1], ref[:, lane + STAT_LANES:lane + STAT_LANES + 1]], axis=0)

            for hp in range(width // LANES):
                sl = slice(hp * LANES, (hp + 1) * LANES)
                qq = _stack_heads(q_ref[:, sl], first)
                dd = _stack_heads(do_ref[:, sl], first)
                kk = jnp.concatenate([kp_ref[:, sl], kc_ref[:, sl]], axis=0)
                vv = jnp.concatenate([vp_ref[:, sl], vc_ref[:, sl]], axis=0)
                s = _dot_nt(qq, kk)
                p = jnp.where(valid2, jnp.exp(s - stacked_column(lse_ref, hp)), 0.0)
                dp = _dot_nt(dd, vv)
                ds = (p * (dp - stacked_column(dl_ref, hp))).astype(BF16)
                dq_ref[:, sl] = (_unstack_heads(_dot(ds, kk), first) * ATTN_SCALE).astype(BF16)
                dk = _dot_tn(ds, qq)
                dv = _dot_tn(p.astype(BF16), dd)
                dk_ref[:, sl] = (ck_s[:, sl] + dk[:ATTN_BLOCK]).astype(BF16)
                dv_ref[:, sl] = (cv_s[:, sl] + dv[:ATTN_BLOCK]).astype(BF16)
                ck_s[:, sl] = dk[ATTN_BLOCK:]
                cv_s[:, sl] = dv[ATTN_BLOCK:]

        @pl.when(n == nb)
        def _():
            dk_ref[...] = ck_s[...].astype(BF16)
            dv_ref[...] = cv_s[...].astype(BF16)

        if nc:
            _cargo_finish(*cargo_refs, (r == d // group - 1) & (n == nb))

    blk = (ATTN_BLOCK, width)
    cur = pl.BlockSpec(blk, lambda r, n: (jnp.minimum(n, nb - 1), r))
    stat = pl.BlockSpec((ATTN_BLOCK, group * STAT_WIDTH), lambda r, n: (jnp.minimum(n, nb - 1), r))
    prev = pl.BlockSpec(blk, lambda r, n: (jnp.maximum(jnp.minimum(n, nb - 1) - 1, 0), r))
    done = pl.BlockSpec(blk, lambda r, n: (jnp.maximum(n - 1, 0), r))
    arrays, specs, shapes, aliases, sems = _cargo_call(cargo, 8, 3)
    out = pl.pallas_call(
        body, name=f"attn_bwd_d{d}", grid=(d // group, nb + 1),
        in_specs=[cur, prev, cur, prev, cur, cur, stat, stat] + specs, out_specs=[cur, done, done] + specs,
        out_shape=[jax.ShapeDtypeStruct((L, d * ATTN_WIDTH), BF16)] * 3 + shapes,
        input_output_aliases=aliases,
        scratch_shapes=[pltpu.VMEM(blk, F32), pltpu.VMEM(blk, F32)] + sems,
        compiler_params=_params(("arbitrary", "arbitrary")),
    )(q, k, k, v, v, d_attn, lse, delta, *arrays)
    return out[:3], out[3:]


def _attn_bwd_consecutive(q, k, v, d_attn, lse, delta, d, cargo=()):
    L = q.shape[0]
    qb = CONSECUTIVE_BLOCKS
    steps = L // (qb * ATTN_BLOCK)
    nc = len(cargo)
    kinds = [kind for kind, _ in cargo]

    def body(*refs):
        q_ref, kp_ref, kc_ref, vp_ref, vc_ref, do_ref, lse_ref, dl_ref = refs[:8]
        dq_ref, dk_ref, dv_ref, ek_ref, ev_ref = refs[8 + nc:13 + nc]
        cargo_refs = (kinds, refs[8:8 + nc], refs[13 + nc:13 + 2 * nc], refs[13 + 2 * nc:])
        r, n = pl.program_id(0), pl.program_id(1)
        if nc:
            _cargo_start(*cargo_refs, (r == 0) & (n == 0))
        first = _first_head_lanes()
        for hp in range(ATTN_WIDTH // LANES):
            sl = slice(hp * LANES, (hp + 1) * LANES)
            for sub in range(qb):
                rows = slice(sub * ATTN_BLOCK, (sub + 1) * ATTN_BLOCK)
                valid = _band_mask(n if sub == 0 else 1)
                valid2 = jnp.concatenate([valid, valid], axis=0)
                if sub == 0:
                    kk = jnp.concatenate([kp_ref[:, sl], kc_ref[rows, sl]], axis=0)
                    vv = jnp.concatenate([vp_ref[:, sl], vc_ref[rows, sl]], axis=0)
                else:
                    keys = slice((sub - 1) * ATTN_BLOCK, (sub + 1) * ATTN_BLOCK)
                    kk, vv = kc_ref[keys, sl], vc_ref[keys, sl]
                qq = _stack_heads(q_ref[rows, sl], first)
                dd = _stack_heads(do_ref[rows, sl], first)
                lane = 2 * hp * STAT_LANES
                column = lambda ref: jnp.concatenate(
                    [ref[rows, lane:lane + 1], ref[rows, lane + STAT_LANES:lane + STAT_LANES + 1]], axis=0)
                p = jnp.where(valid2, jnp.exp(_dot_nt(qq, kk) - column(lse_ref)), 0.0)
                ds = (p * (_dot_nt(dd, vv) - column(dl_ref))).astype(BF16)
                dq_ref[rows, sl] = (_unstack_heads(_dot(ds, kk), first) * ATTN_SCALE).astype(BF16)
                dk = _dot_tn(ds, qq)
                dv = _dot_tn(p.astype(BF16), dd)
                if sub == 0:
                    ek_ref[:, sl] = dk[:ATTN_BLOCK].astype(BF16)
                    ev_ref[:, sl] = dv[:ATTN_BLOCK].astype(BF16)
                else:
                    before = slice((sub - 1) * ATTN_BLOCK, sub * ATTN_BLOCK)
                    dk_ref[before, sl] = (carry_k + dk[:ATTN_BLOCK]).astype(BF16)
                    dv_ref[before, sl] = (carry_v + dv[:ATTN_BLOCK]).astype(BF16)
                carry_k, carry_v = dk[ATTN_BLOCK:], dv[ATTN_BLOCK:]
            dk_ref[rows, sl] = carry_k.astype(BF16)
            dv_ref[rows, sl] = carry_v.astype(BF16)
        if nc:
            _cargo_finish(*cargo_refs, (r == d - 1) & (n == steps - 1))

    cur = pl.BlockSpec((qb * ATTN_BLOCK, ATTN_WIDTH), lambda r, n: (n, r))
    prev = pl.BlockSpec((ATTN_BLOCK, ATTN_WIDTH), lambda r, n: (jnp.maximum(n * qb - 1, 0), r))
    edge = pl.BlockSpec((ATTN_BLOCK, ATTN_WIDTH), lambda r, n: (n, r))
    stat = pl.BlockSpec((qb * ATTN_BLOCK, STAT_WIDTH), lambda r, n: (n, r))
    arrays, specs, shapes, aliases, sems = _cargo_call(cargo, 8, 5)
    out = pl.pallas_call(
        body, name=f"attn_bwd_d{d}", grid=(d, steps),
        in_specs=[cur, prev, cur, prev, cur, cur, stat, stat] + specs, out_specs=[cur, cur, cur, edge, edge] + specs,
        out_shape=[jax.ShapeDtypeStruct((L, d * ATTN_WIDTH), BF16)] * 3
        + [jax.ShapeDtypeStruct((steps * ATTN_BLOCK, d * ATTN_WIDTH), BF16)] * 2 + shapes,
        input_output_aliases=aliases, scratch_shapes=sems,
        compiler_params=_params(("arbitrary", "arbitrary")),
    )(q, k, k, v, v, d_attn, lse, delta, *arrays)
    return out[:3], out[3:5], out[5:]


def _mix_in_bwd(dqkv, edges, d_pool_in, w_in, x, g_pre, dx1):
    S, D = x.shape
    TM = CONSECUTIVE_BLOCKS * ATTN_BLOCK
    nd = len(DILATIONS)
    n_tiles = S // TM

    def body(*refs):
        g_refs = refs[:3 * nd]
        e_refs = (None,) + refs[3 * nd:3 * nd + 2]
        dpi_ref, w_ref, x_ref, g_ref, dx1_ref, dproj_ref, gx_ref, dg_ref = refs[3 * nd + 2:3 * nd + 10]
        t_s = refs[3 * nd + 10:]

        @pl.when(pl.program_id(0) == 0)
        def _():
            dg_ref[...] = jnp.zeros_like(dg_ref)

        dh = jnp.zeros((TM, D), F32)
        for a in range(4):
            if a < 3:
                tot = g_refs[a][...].astype(F32)
                if a > 0:
                    late = jnp.where(pl.program_id(0) < n_tiles - 1, e_refs[a][...].astype(F32), 0.0)
                    tot = jnp.concatenate([tot[:TM - ATTN_BLOCK], tot[TM - ATTN_BLOCK:] + late], axis=0)
                for i, d in enumerate(DILATIONS[1:]):
                    tot = tot + _from_residue(g_refs[3 * (i + 1) + a], t_s, d)
                db = tot.astype(BF16)
            else:
                db = dpi_ref[...]
            dproj_ref[:, a * ATTN_WIDTH:(a + 1) * ATTN_WIDTH] = db
            dh = dh + _dot_nt(db, w_ref[a])
        n, r = _rms_stats(x_ref[...])
        dx, dg = _rms_bwd(dh, n, r, g_ref[...])
        dg_ref[...] += dg
        gx_ref[...] = dx1_ref[...].astype(F32) + dx

    row = lambda w: pl.BlockSpec((TM, w), lambda i: (i, 0))
    edge = pl.BlockSpec((ATTN_BLOCK, ATTN_WIDTH), lambda i: (jnp.minimum(i + 1, n_tiles - 1), 0))
    return pl.pallas_call(
        body, name="mix_in_bwd", grid=(S // TM,),
        in_specs=[_residue_spec(TM, d) for d in DILATIONS for _ in range(3)] + [edge, edge]
        + [row(POOL_WIDTH), _const_spec(w_in.shape), row(D), _const_spec((1, D)), row(D)],
        out_specs=[row(4 * ATTN_WIDTH), row(D), pl.BlockSpec((1, D), lambda i: (0, 0))],
        out_shape=[jax.ShapeDtypeStruct((S, 4 * ATTN_WIDTH), BF16), jax.ShapeDtypeStruct((S, D), F32),
                   jax.ShapeDtypeStruct((1, D), F32)],
        scratch_shapes=_token_scratch(TM),
        compiler_params=_params(("arbitrary",), VMEM_LIMIT),
    )(*[g for gs in dqkv for g in gs], *edges, d_pool_in, w_in, x, g_pre, dx1)


SMALL_EARLY = ("pool_w", "pool_scale", "g_mix_post", "g_ffn_pre", "conv_b", "g_ffn_post", "conv_w")
SMALL_LATE = ("g_mix_pre",)


def _pack_small(grads, names):
    parts = []
    for n in names:
        g = grads[n]
        if n == "conv_w":
            g = g.reshape(CONV_WIDTH, N_SHARD, -1).transpose(1, 0, 2)
        parts.append(g.reshape(-1, LANES))
    return jnp.concatenate(parts, axis=0) if len(parts) > 1 else parts[0]


def _unpack_small(packed, names, like, shard):
    out, row = {}, 0
    for n in names:
        size = like[n].size * (N_SHARD if n == "conv_w" else 1)
        g = packed[row:row + size // LANES]
        row += size // LANES
        if n == "conv_w":
            g = lax.dynamic_slice_in_dim(g.reshape((N_SHARD,) + like[n].shape), shard, 1, axis=0)[0]
        out[n] = g.reshape(like[n].shape)
    return out


def _local_step(x, target, g_mix_pre, w_in, pool_w, pool_scale, w_out, g_mix_post, g_ffn_pre,
                w_up, conv_w, conv_b, w_down, g_ffn_post, mesh_pos=None):
    on_mesh = mesh_pos is not None
    D = x.shape[1]
    CW = w_up.shape[2]
    qkv, pool_in, h1, got = _mix_in_fwd(x, g_mix_pre, w_in, [("ici", w_up)] if on_mesh else ())
    w_up = got[0] if on_mesh else w_up
    o1, l1, got = _attn_fwd(*qkv[0], 1, [("d2d", w_up), ("ici", w_out)] if on_mesh else ())
    w_up, w_out = got if on_mesh else (w_up, w_out)
    o4, l4, got = _attn_fwd(*qkv[1], 4, [("d2d", w_out), ("ici", w_down)] if on_mesh else ())
    w_out, w_down = got if on_mesh else (w_out, w_down)
    o16, l16, got = _attn_fwd(*qkv[2], 16, [("d2d", w_down)] if on_mesh else ())
    w_down = got[0] if on_mesh else w_down
    w_out = w_out.reshape(D, D)
    w_down = w_down.reshape(2 * CW, D)
    attn, lse = _attn_mix((o1, o4, o16), (l1, l4, l16))
    pool = _pool_fwd(pool_in, pool_w, pool_scale)
    mixed, x1, h2, cat = _mix_out_fwd(attn, pool, w_out, x, g_mix_post, g_ffn_pre)

    yv, dy, df, dc, loss, d_g_ffn_post, d_conv_b, d_conv_w = _ffn_fwd(
        h2, x1, target, w_up, w_down, conv_w, conv_b, g_ffn_post)
    du, dx1, d_g_ffn_pre = _ffn_bwd(dc, conv_w, w_up, x1, g_ffn_pre, dy)
    d_w_up = _matmul_tn(h2, du, N_SHARD, "grad_w_up")
    d_w_down = _matmul_tn(yv, df, 1, "grad_w_down")[0].reshape(N_SHARD, CW // 2, D)
    swap = [("swap", d_w_up), ("swap", d_w_down)] if on_mesh else ()
    d_mixed, d_pool, d_g_mix_post, d_attn, delta, from_sibling = _mix_out_bwd(dx1, mixed, g_mix_post, w_out, attn, swap)
    d_w_out = _matmul_tn(cat, d_mixed, 1, "grad_w_out")[0].reshape(N_SHARD, D // N_SHARD, D)
    d_pool_in, d_pool_w, d_pool_scale = _pool_bwd(pool_in, d_pool, pool_w, pool_scale)
    grads = dict(pool_w=d_pool_w, pool_scale=d_pool_scale, w_out=d_w_out, g_mix_post=d_g_mix_post,
                 g_ffn_pre=d_g_ffn_pre, w_up=d_w_up, conv_w=d_conv_w, conv_b=d_conv_b, w_down=d_w_down,
                 g_ffn_post=d_g_ffn_post)
    cargo = [(), (), ()]
    if on_mesh:
        c_arr, device, shard_arr = mesh_pos
        up_f32, up_bf16 = _pair_sum(d_w_up, from_sibling[0], c_arr, "pair_sum_w_up")
        down_f32, down_bf16 = _pair_sum(d_w_down, from_sibling[1], c_arr, "pair_sum_w_down")
        early = _pack_small(grads, SMALL_EARLY)
        early_slots = lax.dynamic_update_index_in_dim(jnp.zeros((8,) + early.shape, F32), early, device, 0)
        cargo = [[("scatter", down_bf16), ("everyone", early_slots)], [("scatter", up_bf16)], []]

    dqkv1, edges, landed1 = _attn_bwd_consecutive(*qkv[0], d_attn[0], lse[0], delta[0], 1, cargo[0])
    dqkv4, landed4 = _attn_bwd(*qkv[1], d_attn[1], lse[1], delta[1], 4, cargo[1])
    if on_mesh:
        halves = [_shard_sum(up_f32, landed4[0], shard_arr, c_arr, "shard_sum_w_up"),
                  _shard_sum(down_f32, landed1[0], shard_arr, c_arr, "shard_sum_w_down")]
        cargo[2] = cargo[2] + [("join", h) for h in halves]
    if qkv[2][0].shape[0] == CONSECUTIVE_BLOCKS * ATTN_BLOCK:
        dqkv16, _, landed16 = _attn_bwd_consecutive(*qkv[2], d_attn[2], lse[2], delta[2], 16, cargo[2])
    else:
        dqkv16, landed16 = _attn_bwd(*qkv[2], d_attn[2], lse[2], delta[2], 16, cargo[2])
    if on_mesh:
        grads.update(small_early=landed1[1], w_up=landed16[0], w_down=landed16[1])
    d_proj, grad_x, grads["g_mix_pre"] = _mix_in_bwd((dqkv1, dqkv4, dqkv16), edges, d_pool_in, w_in, x, g_mix_pre, dx1)
    grads["w_in"] = _matmul_tn(h1, d_proj, N_SHARD, "grad_w_in")
    return loss, grad_x, grads


ANY = pl.BlockSpec(memory_space=pl.ANY)


def _position():
    x, y, c = lax.axis_index("x"), lax.axis_index("y"), lax.axis_index("c")
    chips = [(1 - x, y), (x, 1 - y), (1 - x, 1 - y)]
    return x, y, c, chips


def _remote(src, dst, send_sem, recv_sem, to):
    return pltpu.make_async_remote_copy(src_ref=src, dst_ref=dst, send_sem=send_sem, recv_sem=recv_sem,
                                        device_id=to, device_id_type=MESH)


def _cast_bf16(w, shard_arr, name):
    R, C = w.shape
    tr = R // 2

    def body(s_ref, w_ref, o_ref):
        o_ref[0] = w_ref[...].astype(BF16)

    return pl.pallas_call(
        body, name=name,
        grid_spec=pltpu.PrefetchScalarGridSpec(
            num_scalar_prefetch=1, grid=(2,),
            in_specs=[pl.BlockSpec((tr, C), lambda i, s_ref: (i, 0))],
            out_specs=pl.BlockSpec((1, tr, C), lambda i, s_ref: (s_ref[0], i, 0))),
        out_shape=jax.ShapeDtypeStruct((N_SHARD, R, C), BF16),
        compiler_params=_params(("parallel",)))(shard_arr, w)


def _gather_weights(bufs):
    n = len(bufs) - 1

    def body(*refs):
        outs, cw_out = refs[n + 1:2 * n + 1], refs[2 * n + 1]
        ici_send, ici_recv, d2d_send, d2d_recv = refs[2 * n + 2:]
        x, y, c, chips = _position()
        s = 2 * x + y
        sibling = (x, y, 1 - c)

        def half(a, shard, h):
            rows = outs[a].shape[1] // 2
            return outs[a].at[shard, pl.ds(h * rows, rows), :]

        sends = []
        for a in range(n):
            for j, (px, py) in enumerate(chips):
                sends.append(_remote(half(a, s, c), half(a, s, c),
                                     ici_send.at[3 * a + j], ici_recv.at[3 * a + j], (px, py, c)))
        for j, (px, py) in enumerate(chips):
            sends.append(_remote(cw_out.at[s], cw_out.at[s], ici_send.at[3 * n + j], ici_recv.at[3 * n + j], (px, py, c)))
        for cp in sends:
            cp.start()
        passed = []
        for a in range(n):
            for j, (px, py) in enumerate(chips):
                sj = 2 * px + py
                got = half(a, sj, c)
                _remote(got, got, ici_send.at[3 * a + j], ici_recv.at[3 * a + j], (px, py, c)).wait_recv()
                fwd = _remote(got, got, d2d_send.at[3 * a + j], d2d_recv.at[3 * a + j], sibling)
                fwd.start()
                passed.append(fwd)
        for j, (px, py) in enumerate(chips):
            got = cw_out.at[2 * px + py]
            _remote(got, got, ici_send.at[3 * n + j], ici_recv.at[3 * n + j], (px, py, c)).wait_recv()
        for a in range(n):
            for j, (px, py) in enumerate(chips):
                got = half(a, 2 * px + py, 1 - c)
                _remote(got, got, d2d_send.at[3 * a + j], d2d_recv.at[3 * a + j], sibling).wait_recv()
        for cp in sends + passed:
            cp.wait_send()

    return pl.pallas_call(
        body, name="gather_weights",
        in_specs=[ANY] * (n + 1), out_specs=[ANY] * (n + 1),
        out_shape=[jax.ShapeDtypeStruct(b.shape, b.dtype) for b in bufs],
        input_output_aliases={i: i for i in range(n + 1)},
        scratch_shapes=[pltpu.SemaphoreType.DMA((3 * n + 3,)), pltpu.SemaphoreType.DMA((3 * n + 3,)),
                        pltpu.SemaphoreType.DMA((3 * n,)), pltpu.SemaphoreType.DMA((3 * n,))],
        compiler_params=pltpu.CompilerParams(has_side_effects=True),
    )(*bufs)


def _swap_halves(grads, tag):
    n = len(grads)

    def body(*refs):
        ins, outs, send_sem, recv_sem = refs[:n], refs[n:2 * n], refs[2 * n], refs[2 * n + 1]
        x, y, c, _ = _position()
        copies = []
        for a in range(n):
            rows = ins[a].shape[1] // 2
            copies.append(_remote(ins[a].at[:, pl.ds((1 - c) * rows, rows), :], outs[a],
                                  send_sem.at[a], recv_sem.at[a], (x, y, 1 - c)))
        for cp in copies:
            cp.start()
        for cp in copies:
            cp.wait()

    return pl.pallas_call(
        body, name="swap_grad_halves_" + tag,
        in_specs=[ANY] * n, out_specs=[ANY] * n,
        out_shape=[jax.ShapeDtypeStruct((g.shape[0], g.shape[1] // 2, g.shape[2]), F32) for g in grads],
        scratch_shapes=[pltpu.SemaphoreType.DMA((n,)), pltpu.SemaphoreType.DMA((n,))],
        compiler_params=pltpu.CompilerParams(has_side_effects=True),
    )(*grads)


def _pair_sum(g, got, c_arr, name):
    n_sh, R, C = g.shape
    rows = R // 2

    def body(c_ref, g_ref, r_ref, f_ref, b_ref):
        t = g_ref[...] + r_ref[...]
        f_ref[...] = t
        b_ref[...] = t.astype(BF16)

    blk = pl.BlockSpec((1, rows, C), lambda i, c_ref: (i, 0, 0))
    return pl.pallas_call(
        body, name=name,
        grid_spec=pltpu.PrefetchScalarGridSpec(
            num_scalar_prefetch=1, grid=(n_sh,),
            in_specs=[pl.BlockSpec((1, rows, C), lambda i, c_ref: (i, c_ref[0], 0)), blk],
            out_specs=[blk, blk]),
        out_shape=[jax.ShapeDtypeStruct((n_sh, rows, C), F32), jax.ShapeDtypeStruct((n_sh, rows, C), BF16)],
        compiler_params=_params(("parallel",)),
    )(c_arr, g, got)


def _shard_sum(sums_f32, recv, shard_arr, c_arr, name):
    _, rows, C = sums_f32.shape

    def body(s_ref, c_ref, o_ref, r_ref, t_ref):
        t_ref[...] = ((o_ref[0] + r_ref[0].astype(F32)) + r_ref[1].astype(F32)) + r_ref[2].astype(F32)

    return pl.pallas_call(
        body, name=name,
        grid_spec=pltpu.PrefetchScalarGridSpec(
            num_scalar_prefetch=2, grid=(1,),
            in_specs=[pl.BlockSpec((1, rows, C), lambda i, s_ref, c_ref: (s_ref[0], 0, 0)),
                      pl.BlockSpec((3, rows, C), lambda i, s_ref, c_ref: (0, 0, 0))],
            out_specs=pl.BlockSpec((rows, C), lambda i, s_ref, c_ref: (c_ref[0], 0))),
        out_shape=jax.ShapeDtypeStruct((2 * rows, C), F32),
        compiler_params=_params(("arbitrary",)),
    )(shard_arr, c_arr, sums_f32, recv)


def _join_halves(bufs):
    n = len(bufs)

    def body(*refs):
        outs, send_sem, recv_sem = refs[n:2 * n], refs[2 * n], refs[2 * n + 1]
        x, y, c, _ = _position()
        copies = []
        for a in range(n):
            rows = outs[a].shape[0] // 2
            mine = outs[a].at[pl.ds(c * rows, rows), :]
            copies.append(_remote(mine, mine, send_sem.at[a], recv_sem.at[a], (x, y, 1 - c)))
        for cp in copies:
            cp.start()
        for a, cp in enumerate(copies):
            cp.wait_send()
            rows = outs[a].shape[0] // 2
            theirs = outs[a].at[pl.ds((1 - c) * rows, rows), :]
            _remote(theirs, theirs, send_sem.at[a], recv_sem.at[a], (x, y, 1 - c)).wait_recv()

    return pl.pallas_call(
        body, name="join_grad_halves",
        in_specs=[ANY] * n, out_specs=[ANY] * n,
        out_shape=[jax.ShapeDtypeStruct(b.shape, F32) for b in bufs],
        input_output_aliases={i: i for i in range(n)},
        scratch_shapes=[pltpu.SemaphoreType.DMA((n,)), pltpu.SemaphoreType.DMA((n,))],
        compiler_params=pltpu.CompilerParams(has_side_effects=True),
    )(*bufs)


def _small_sum(parts, tag):
    _, R, C = parts.shape

    def body(p_ref, o_ref):
        t = p_ref[0]
        for k in range(1, 8):
            t = t + p_ref[k]
        o_ref[...] = t

    return pl.pallas_call(
        body, name="small_grad_sum_" + tag, grid=(1,),
        in_specs=[pl.BlockSpec((8, R, C), lambda i: (0, 0, 0))], out_specs=pl.BlockSpec((R, C), lambda i: (0, 0)),
        out_shape=jax.ShapeDtypeStruct((R, C), F32), compiler_params=_params(("arbitrary",)),
    )(parts)


def _adamw_math(w, g, m, v):
    m = ADAM_B1 * m + (1.0 - ADAM_B1) * g
    v = ADAM_B2 * v + (1.0 - ADAM_B2) * (g * g)
    m_hat = m / (1.0 - ADAM_B1 ** ADAM_STEP)
    v_hat = v / (1.0 - ADAM_B2 ** ADAM_STEP)
    delta = -ADAM_LR * (m_hat / (jnp.sqrt(v_hat) + ADAM_EPS) + ADAM_WD * w)
    return delta, m, v


def _adamw_big(ws, gs, ms, vs, name, cargo=()):
    n = len(ws)
    nc = len(cargo)
    kinds = [kind for kind, _ in cargo]

    def body(*refs):
        ins, outs = refs[:4 * n], refs[4 * n + nc:7 * n + nc]
        cargo_refs = (kinds, refs[4 * n:4 * n + nc], refs[7 * n + nc:7 * n + 2 * nc], refs[7 * n + 2 * nc:])
        if nc:
            _cargo_start(*cargo_refs, pl.program_id(0) == 0)
        for a in range(n):
            w, g, m, v = (ins[k * n + a][...] for k in range(4))
            outs[a][...], outs[n + a][...], outs[2 * n + a][...] = _adamw_math(w, g, m, v)
        if nc:
            _cargo_finish(*cargo_refs, pl.program_id(0) == 3)

    blks = [pl.BlockSpec((w.shape[0] // 4, w.shape[1]), lambda i: (i, 0)) for w in ws]
    arrays, cargo_specs, shapes, aliases, sems = _cargo_call(cargo, 4 * n, 3 * n)
    out = pl.pallas_call(
        body, name=name, grid=(4,), in_specs=blks * 4 + cargo_specs, out_specs=blks * 3 + cargo_specs,
        out_shape=[jax.ShapeDtypeStruct(w.shape, F32) for w in ws] * 3 + shapes,
        input_output_aliases=aliases, scratch_shapes=sems,
        compiler_params=_params(("arbitrary",)),
    )(*ws, *gs, *ms, *vs, *arrays)
    return (out[:n], out[n:2 * n], out[2 * n:3 * n]), out[3 * n:]


def _adamw_small(ws, gs, ms, vs):
    n = len(ws)

    def body(*refs):
        for a in range(n):
            w, g, m, v = (refs[k * n + a][...] for k in range(4))
            d, nm, nv = _adamw_math(w, g, m, v)
            refs[4 * n + a][...] = d
            refs[5 * n + a][...] = nm
            refs[6 * n + a][...] = nv

    shapes = [jax.ShapeDtypeStruct(w.shape, F32) for w in ws]
    out = pl.pallas_call(body, name="adamw_small", out_shape=shapes * 3)(*ws, *gs, *ms, *vs)
    return out[:n], out[n:2 * n], out[2 * n:]


BIG = ("w_in", "w_out", "w_up", "w_down")
SMALL = ("g_mix_pre", "pool_w", "pool_scale", "g_mix_post", "g_ffn_pre", "conv_b", "g_ffn_post", "conv_w")
ORDER = ("g_mix_pre", "w_in", "pool_w", "pool_scale", "w_out", "g_mix_post", "g_ffn_pre", "w_up", "conv_w", "conv_b",
         "w_down", "g_ffn_post")


def kernel(x, g_mix_pre, w_in, pool_w, pool_scale, w_out, g_mix_post, g_ffn_pre, w_up, conv_w, conv_b, w_down, g_ffn_post, loss_target, m_g_mix_pre, m_w_in, m_pool_w, m_pool_scale, m_w_out, m_g_mix_post, m_g_ffn_pre, m_w_up, m_conv_w, m_conv_b, m_w_down, m_g_ffn_post, v_g_mix_pre, v_w_in, v_pool_w, v_pool_scale, v_w_out, v_g_mix_post, v_g_ffn_pre, v_w_up, v_conv_w, v_conv_b, v_w_down, v_g_ffn_post):
    args = dict(locals())
    W = {n: args[n][0] for n in ORDER}
    M = {n: args["m_" + n][0] for n in ORDER}
    V = {n: args["v_" + n][0] for n in ORDER}
    for d in (W, M, V):
        d["pool_w"] = d["pool_w"].reshape(-1, POOL_GROUP)
        for n in ("g_mix_pre", "pool_scale", "g_mix_post", "g_ffn_pre", "conv_b", "g_ffn_post"):
            d[n] = d[n].reshape(1, -1)
    CW = W["w_up"].shape[1]
    c_arr = lax.axis_index("c").astype(jnp.int32).reshape(1)
    shard = 2 * lax.axis_index("x") + lax.axis_index("y")
    shard_arr = shard.astype(jnp.int32).reshape(1)
    device = 2 * shard + lax.axis_index("c")

    conv_w_slots = lax.dynamic_update_index_in_dim(jnp.zeros((N_SHARD,) + W["conv_w"].shape, F32), W["conv_w"], shard, 0)
    slots = {n: _cast_bf16(W[n], shard_arr, "cast_" + n) for n in BIG}
    w_in_g, conv_w_g = _gather_weights([slots["w_in"], conv_w_slots])
    conv_w_full = conv_w_g.transpose(1, 0, 2).reshape(CONV_WIDTH, 1, N_SHARD * CW)

    loss, grad_x, G = _local_step(
        x[0], loss_target[0], W["g_mix_pre"], w_in_g, W["pool_w"].reshape(-1, POOL_GROUP, POOL_GROUP), W["pool_scale"],
        slots["w_out"], W["g_mix_post"], W["g_ffn_pre"], slots["w_up"], conv_w_full, W["conv_b"],
        slots["w_down"], W["g_ffn_post"], (c_arr, device, shard_arr))

    late, ffn = ("w_in", "w_out"), ("w_up", "w_down")
    from_sibling = _swap_halves([G[n] for n in late], "mix")
    sums = {n: _pair_sum(G[n], r, c_arr, "pair_sum_" + n) for n, r in zip(late, from_sibling)}
    loss_rows = jnp.pad(loss, ((0, 7), (0, LANES - 1)))
    small = jnp.concatenate([_pack_small(G, SMALL_LATE), loss_rows], axis=0)
    small_slots = lax.dynamic_update_index_in_dim(jnp.zeros((8,) + small.shape, F32), small, device, 0)
    pick = lambda d, names: [d[n] for n in names]
    delta, new_m, new_v = {}, {}, {}
    updates, landed = _adamw_big(pick(W, ffn), pick(G, ffn), pick(M, ffn), pick(V, ffn), "adamw_ffn",
                                 [("scatter", sums[n][1]) for n in late] + [("everyone", small_slots)])
    halves = [_shard_sum(sums[n][0], r, shard_arr, c_arr, "shard_sum_" + n) for n, r in zip(late, landed[:2])]
    full = dict(zip(late, _join_halves(halves)))
    full.update({n: G[n] for n in ffn})
    full.update(_unpack_small(_small_sum(G["small_early"], "early"), SMALL_EARLY, W, shard))
    late_total = _small_sum(landed[2], "late")
    full.update(_unpack_small(late_total, SMALL_LATE, W, shard))
    loss = late_total[-8, 0]

    for names, (ds, nms, nvs) in ((ffn, updates),
                                  (late, _adamw_big(pick(W, late), pick(full, late), pick(M, late), pick(V, late), "adamw_mix")[0]),
                                  (SMALL, _adamw_small(pick(W, SMALL), pick(full, SMALL), pick(M, SMALL), pick(V, SMALL)))):
        for n, d, nm, nv in zip(names, ds, nms, nvs):
            delta[n], new_m[n], new_v[n] = d, nm, nv

    shaped = lambda d: [d[n].reshape(args[n].shape) for n in ORDER]
    return (loss, grad_x[None], *shaped(full), *shaped(delta), *shaped(new_m), *shaped(new_v))
```

```python
import functools

import jax
import jax.numpy as jnp
from jax import lax
from jax.experimental import pallas as pl
from jax.experimental.pallas import tpu as pltpu

F32 = jnp.float32
BF16 = jnp.bfloat16

RMS_EPS = 1e-6
NEG_INF = -1e30
N_HEADS = 8
HEAD_DIM = 64
ATTN_WIDTH = N_HEADS * HEAD_DIM
ATTN_SCALE = HEAD_DIM ** -0.5
ATTN_BLOCK = 128
DILATIONS = (1, 4, 16)
RESIDUES_PER_STEP = 4
CONSECUTIVE_BLOCKS = 4
POOL_WINDOWS = (2, 4, 8, 16)
POOL_GROUP = 128
POOL_WIDTH = POOL_GROUP * len(POOL_WINDOWS)
POOL_HALO = 16
CONV_WIDTH = 3
CONV_HALO = 8
N_SHARD = 4
LANES = 128
STAT_LANES = 16
STAT_WIDTH = N_HEADS * STAT_LANES

ADAM_LR = 0.001
ADAM_B1 = 0.9
ADAM_B2 = 0.999
ADAM_EPS = 1e-08
ADAM_WD = 0.01
ADAM_STEP = 10

VMEM_LIMIT = 60 * 1024 * 1024
MESH = pl.DeviceIdType.MESH
NT = (((1,), (1,)), ((), ()))
TN = (((0,), (0,)), ((), ()))


def _params(sem, vmem=None):
    return pltpu.CompilerParams(dimension_semantics=sem, vmem_limit_bytes=vmem)


def _const_spec(shape):
    zeros = (0,) * len(shape)
    return pl.BlockSpec(shape, lambda *_: zeros, pipeline_mode=pl.Buffered(1))


def _dot(a, b):
    return jnp.dot(a, b, preferred_element_type=F32)


def _dot_nt(a, b):
    return lax.dot_general(a, b, NT, preferred_element_type=F32)


def _dot_tn(a, b):
    return lax.dot_general(a, b, TN, preferred_element_type=F32)


def _rms_stats(x):
    r = lax.rsqrt(jnp.mean(x * x, axis=-1, keepdims=True) + RMS_EPS)
    return x * r, r


def _rms_bwd(dy, n, r, g):
    dg = jnp.sum(dy * n, axis=0, keepdims=True)
    dn = dy * g
    dx = r * (dn - n * jnp.mean(dn * n, axis=-1, keepdims=True))
    return dx, dg


def _gelu_tanh(g):
    k = 0.7978845608028654
    kc = k * 0.044715
    g2 = g * g
    t = jnp.tanh(g * (k + kc * g2))
    h = 0.5 * t + 0.5
    dh = (0.5 - 0.5 * (t * t)) * (k + (3.0 * kc) * g2)
    return g * h, h + g * dh


def _residue_shape(S, d, dtype, width=ATTN_WIDTH):
    return jax.ShapeDtypeStruct((S // d, d * width), dtype)


def _residue_spec(TM, d, width=ATTN_WIDTH):
    return pl.BlockSpec((TM // d, d * width), lambda i: (i, 0))


def _token_scratch(TM, width=ATTN_WIDTH):
    return [pltpu.VMEM((TM, LANES), F32)] * (width // LANES)


def _head_sum_matrix():
    r = lax.broadcasted_iota(jnp.int32, (ATTN_WIDTH, STAT_WIDTH), 0)
    c = lax.broadcasted_iota(jnp.int32, (ATTN_WIDTH, STAT_WIDTH), 1)
    return (r // HEAD_DIM == c // STAT_LANES).astype(BF16)


def _head_spread_matrix():
    c = lax.broadcasted_iota(jnp.int32, (STAT_WIDTH, ATTN_WIDTH), 0)
    r = lax.broadcasted_iota(jnp.int32, (STAT_WIDTH, ATTN_WIDTH), 1)
    return (c == (r // HEAD_DIM) * STAT_LANES).astype(BF16)


def _bf16_pieces(x, n):
    pieces = []
    for _ in range(n):
        p = x.astype(BF16)
        pieces.append(p)
        x = x - p.astype(F32)
    return pieces


def _put_tokens(dst_s, val):
    for cb, chunk in enumerate(dst_s):
        chunk[...] = val[:, cb * LANES:(cb + 1) * LANES]


def _get_tokens(src_s):
    return jnp.concatenate([chunk[...] for chunk in src_s], axis=1)


def _to_residue(val, src_s, out_ref, d, dtype):
    if d == 1:
        out_ref[...] = val.astype(dtype)
        return
    rows = src_s[0].shape[0]
    for r in range(d):
        for cb, chunk in enumerate(src_s):
            col = (r * len(src_s) + cb) * LANES
            out_ref[:, col:col + LANES] = chunk[pl.ds(r, rows // d, stride=d), :].astype(dtype)


def _from_residue(in_ref, dst_s, d):
    if d == 1:
        return in_ref[...].astype(F32)
    rows = dst_s[0].shape[0]
    for r in range(d):
        for cb, chunk in enumerate(dst_s):
            col = (r * len(dst_s) + cb) * LANES
            chunk[pl.ds(r, rows // d, stride=d), :] = in_ref[:, col:col + LANES].astype(F32)
    return _get_tokens(dst_s)


def _mix_in_fwd(x, g_pre, w_in, cargo=()):
    S, D = x.shape
    TM = 512
    nc = len(cargo)
    kinds = [kind for kind, _ in cargo]
    n_chunks = ATTN_WIDTH // LANES

    def body(x_ref, g_ref, w_ref, *refs):
        cargo_in, refs = refs[:nc], refs[nc:]
        qkv_refs, p_ref, h_ref = refs[:9], refs[9], refs[10]
        t_s = refs[11 + nc:11 + nc + n_chunks]
        cargo_refs = (kinds, cargo_in, refs[11:11 + nc], refs[11 + nc + n_chunks:])
        if nc:
            _cargo_start(*cargo_refs, pl.program_id(0) == 0)
        n, _ = _rms_stats(x_ref[...])
        hb = (n * g_ref[...]).astype(BF16)
        h_ref[...] = hb
        for a in range(3):
            res = _dot(hb, w_ref[a])
            if a == 0:
                res = res * ATTN_SCALE
            _put_tokens(t_s, res)
            for i, d in enumerate(DILATIONS):
                _to_residue(res, t_s, qkv_refs[3 * i + a], d, BF16)
        p_ref[...] = _dot(hb, w_ref[3])
        if nc:
            _cargo_finish(*cargo_refs, pl.program_id(0) == S // TM - 1)

    row = lambda w: pl.BlockSpec((TM, w), lambda i: (i, 0))
    arrays, cargo_specs, shapes, aliases, sems = _cargo_call(cargo, 3, 11)
    out = pl.pallas_call(
        body, name="mix_in_fwd", grid=(S // TM,),
        in_specs=[row(D), _const_spec((1, D)), _const_spec(w_in.shape)] + cargo_specs,
        out_specs=[_residue_spec(TM, d) for d in DILATIONS for _ in range(3)] + [row(POOL_WIDTH), row(D)] + cargo_specs,
        out_shape=[_residue_shape(S, d, BF16) for d in DILATIONS for _ in range(3)]
        + [jax.ShapeDtypeStruct((S, POOL_WIDTH), F32), jax.ShapeDtypeStruct((S, D), BF16)] + shapes,
        input_output_aliases=aliases,
        scratch_shapes=_token_scratch(TM) + sems,
        compiler_params=_params(("arbitrary",), VMEM_LIMIT),
    )(x, g_pre, w_in, *arrays)
    return [out[0:3], out[3:6], out[6:9]], out[9], out[10], out[11:]


def _band_mask(n):
    qi = lax.broadcasted_iota(jnp.int32, (ATTN_BLOCK, 2 * ATTN_BLOCK), 0)
    ki = lax.broadcasted_iota(jnp.int32, (ATTN_BLOCK, 2 * ATTN_BLOCK), 1)
    dist = qi + ATTN_BLOCK - ki
    return (dist >= 0) & (dist <= ATTN_BLOCK) & ((ki >= ATTN_BLOCK) | (n > 0))


def _first_head_lanes():
    return lax.broadcasted_iota(jnp.int32, (1, LANES), 1) < HEAD_DIM


def _stack_heads(pair, first):
    zero = jnp.zeros_like(pair)
    return jnp.concatenate([jnp.where(first, pair, zero), jnp.where(first, zero, pair)], axis=0)


def _unstack_heads(stacked, first):
    return jnp.where(first, stacked[:ATTN_BLOCK], stacked[ATTN_BLOCK:])


CARGO_COPIES = {"ici": 3, "d2d": 3, "scatter": 3, "swap": 1, "everyone": 7, "join": 1}
CARGO_IN_PLACE = ("ici", "d2d", "everyone", "join")


def _cargo_copies(kinds, ins, outs, send_sems, recv_sems, want_recvs=True):
    x, y, c, chips = _position()
    s = 2 * x + y
    me = 2 * s + c
    sibling = (x, y, 1 - c)
    sends, recvs = [], []

    def add(k, src, dst, landing, to):
        sends.append(_remote(src, dst, send_sems.at[k], recv_sems.at[k], to))
        if want_recvs:
            recvs.append(_remote(landing, landing, send_sems.at[k], recv_sems.at[k], to))

    k0 = 0
    for a, kind in enumerate(kinds):
        if kind == "swap":
            rows = ins[a].shape[1] // 2
            add(k0, ins[a].at[:, pl.ds((1 - c) * rows, rows), :], outs[a], outs[a], sibling)
        elif kind == "join":
            rows = outs[a].shape[0] // 2
            mine = outs[a].at[pl.ds(c * rows, rows), :]
            add(k0, mine, mine, outs[a].at[pl.ds((1 - c) * rows, rows), :], sibling)
        elif kind == "everyone":
            for m in range(1, 8):
                peer = (x ^ (m >> 2), y ^ ((m >> 1) & 1), c ^ (m & 1))
                add(k0 + m - 1, outs[a].at[me], outs[a].at[me], outs[a].at[4 * peer[0] + 2 * peer[1] + peer[2]], peer)
        else:
            for j, (px, py) in enumerate(chips):
                sj = 2 * px + py
                if kind == "scatter":
                    add(k0 + j, ins[a].at[sj], outs[a].at[j], outs[a].at[j], (px, py, c))
                    continue
                buf = outs[a]
                rows = buf.shape[1] // 2
                half = lambda shard, h: buf.at[shard, pl.ds(h * rows, rows), :]
                if kind == "ici":
                    add(k0 + j, half(s, c), half(s, c), half(sj, c), (px, py, c))
                else:
                    add(k0 + j, half(sj, c), half(sj, c), half(sj, 1 - c), sibling)
        k0 += CARGO_COPIES[kind]
    return sends, recvs


def _cargo_start(kinds, ins, outs, sems, first_step):
    @pl.when(first_step)
    def _():
        for cp in _cargo_copies(kinds, ins, outs, *sems, want_recvs=False)[0]:
            cp.start()


def _cargo_finish(kinds, ins, outs, sems, last_step):
    @pl.when(last_step)
    def _():
        sends, recvs = _cargo_copies(kinds, ins, outs, *sems)
        for cp in sends:
            cp.wait_send()
        for cp in recvs:
            cp.wait_recv()


def _cargo_call(cargo, n_in, n_out):
    arrays = [a for _, a in cargo]
    shapes = []
    for kind, a in cargo:
        if kind == "scatter":
            shape = (3,) + a.shape[1:]
        elif kind == "swap":
            shape = (a.shape[0], a.shape[1] // 2, a.shape[2])
        else:
            shape = a.shape
        shapes.append(jax.ShapeDtypeStruct(shape, a.dtype))
    aliases = {n_in + i: n_out + i for i, (kind, _) in enumerate(cargo) if kind in CARGO_IN_PLACE}
    n_sems = sum(CARGO_COPIES[kind] for kind, _ in cargo)
    sems = [pltpu.SemaphoreType.DMA((n_sems,))] * 2 if cargo else []
    return arrays, [ANY] * len(cargo), shapes, aliases, sems


def _attn_fwd(q, k, v, d, cargo=()):
    L = q.shape[0]
    group = min(d, RESIDUES_PER_STEP)
    width = group * ATTN_WIDTH
    qb = RESIDUES_PER_STEP // group
    steps = L // (qb * ATTN_BLOCK)
    nc = len(cargo)
    kinds = [kind for kind, _ in cargo]

    def body(*refs):
        q_ref, kp_ref, kc_ref, vp_ref, vc_ref = refs[:5]
        o_ref, lse_ref = refs[5 + nc:7 + nc]
        cargo_refs = (kinds, refs[5:5 + nc], refs[7 + nc:7 + 2 * nc], refs[7 + 2 * nc:])
        r, n = pl.program_id(0), pl.program_id(1)
        if nc:
            _cargo_start(*cargo_refs, (r == 0) & (n == 0))
        first = _first_head_lanes()
        for sub in range(qb):
            rows = slice(sub * ATTN_BLOCK, (sub + 1) * ATTN_BLOCK)
            valid = _band_mask(n if sub == 0 else 1)
            valid2 = jnp.concatenate([valid, valid], axis=0)
            for hp in range(width // LANES):
                sl = slice(hp * LANES, (hp + 1) * LANES)
                if sub == 0:
                    kk = jnp.concatenate([kp_ref[:, sl], kc_ref[rows, sl]], axis=0)
                    vv = jnp.concatenate([vp_ref[:, sl], vc_ref[rows, sl]], axis=0)
                else:
                    keys = slice((sub - 1) * ATTN_BLOCK, (sub + 1) * ATTN_BLOCK)
                    kk, vv = kc_ref[keys, sl], vc_ref[keys, sl]
                s = jnp.where(valid2, _dot_nt(_stack_heads(q_ref[rows, sl], first), kk), NEG_INF)
                m = jnp.max(s, axis=-1, keepdims=True)
                p = jnp.exp(s - m)
                den = jnp.sum(p, axis=-1, keepdims=True)
                o_ref[rows, sl] = _unstack_heads(_dot(p.astype(BF16), vv) / den, first).astype(BF16)
                lse = m + jnp.log(den)
                lane = 2 * hp * STAT_LANES
                for half in range(2):
                    lse_ref[rows, lane + half * STAT_LANES:lane + (half + 1) * STAT_LANES] = jnp.broadcast_to(
                        lse[half * ATTN_BLOCK:(half + 1) * ATTN_BLOCK], (ATTN_BLOCK, STAT_LANES))
        if nc:
            _cargo_finish(*cargo_refs, (r == d // group - 1) & (n == steps - 1))

    cur = pl.BlockSpec((qb * ATTN_BLOCK, width), lambda r, n: (n, r))
    prev = pl.BlockSpec((ATTN_BLOCK, width), lambda r, n: (jnp.maximum(n * qb - 1, 0), r))
    arrays, specs, shapes, aliases, sems = _cargo_call(cargo, 5, 2)
    out = pl.pallas_call(
        body, name=f"attn_fwd_d{d}", grid=(d // group, steps),
        in_specs=[cur, prev, cur, prev, cur] + specs,
        out_specs=[cur, pl.BlockSpec((qb * ATTN_BLOCK, group * STAT_WIDTH), lambda r, n: (n, r))] + specs,
        out_shape=[jax.ShapeDtypeStruct((L, d * ATTN_WIDTH), BF16), jax.ShapeDtypeStruct((L, d * STAT_WIDTH), F32)] + shapes,
        input_output_aliases=aliases, scratch_shapes=sems,
        compiler_params=_params(("arbitrary", "arbitrary")),
    )(q, k, k, v, v, *arrays)
    return out[0], out[1], out[2:]


def _attn_mix(outs, lses):
    S = outs[0].shape[0]
    TM = 512
    n = len(DILATIONS)

    def body(*refs):
        o_refs, l_refs, attn_ref, lse_refs = refs[:n], refs[n:2 * n], refs[2 * n], refs[2 * n + 1:3 * n + 1]
        t_s, c_s = refs[3 * n + 1:-1], refs[-1:]
        os = [_from_residue(o_refs[i], t_s, d) for i, d in enumerate(DILATIONS)]
        ls = [_from_residue(l_refs[i], c_s, d) for i, d in enumerate(DILATIONS)]
        m = jnp.maximum(jnp.maximum(ls[0], ls[1]), ls[2])
        es = [jnp.exp(l - m) for l in ls]
        den = es[0] + es[1] + es[2]
        spread = _head_spread_matrix()
        ws = [sum(_dot(p, spread) for p in _bf16_pieces(e / den, 2)) for e in es]
        attn_ref[...] = (ws[0] * os[0] + ws[1] * os[1] + ws[2] * os[2]).astype(BF16)
        lse = m + jnp.log(den)
        _put_tokens(c_s, lse)
        for i, d in enumerate(DILATIONS):
            _to_residue(lse, c_s, lse_refs[i], d, F32)

    specs = [_residue_spec(TM, d) for d in DILATIONS]
    stats = [_residue_spec(TM, d, STAT_WIDTH) for d in DILATIONS]
    out = pl.pallas_call(
        body, name="attn_mix", grid=(S // TM,),
        in_specs=specs + stats, out_specs=[specs[0]] + stats,
        out_shape=[jax.ShapeDtypeStruct((S, ATTN_WIDTH), BF16)]
        + [_residue_shape(S, d, F32, STAT_WIDTH) for d in DILATIONS],
        scratch_shapes=_token_scratch(TM) + _token_scratch(TM, STAT_WIDTH),
        compiler_params=_params(("parallel",)),
    )(*outs, *lses)
    return out[0], out[1:]


def _pool_counts(first_row, rows, w):
    t = first_row + lax.broadcasted_iota(jnp.int32, (rows, 1), 0)
    return jnp.minimum(t + 1, w).astype(F32)


def _trailing_sums(xe, w):
    s, k = xe, 1
    while k < w:
        s = s + pltpu.roll(s, k, 0)
        k *= 2
    return s


def _leading_sums(xe, w):
    rows = xe.shape[0]
    s, k = xe, 1
    while k < w:
        s = s + pltpu.roll(s, rows - k, 0)
        k *= 2
    return s


def _pooled_groups(halo, cur, first_row):
    TM = cur.shape[0]
    xe = jnp.concatenate([halo, cur], axis=0)
    out = []
    for g, w in enumerate(POOL_WINDOWS):
        a = xe[:, g * POOL_GROUP:(g + 1) * POOL_GROUP]
        s = _trailing_sums(a, w)[POOL_HALO:]
        out.append(s / _pool_counts(first_row, TM, w) - a[POOL_HALO:])
    return out


def _pool_fwd(pool_in, pool_w, pool_scale):
    S = pool_in.shape[0]
    TM = 512
    HB = TM // POOL_HALO

    def body(cur_ref, halo_ref, w_ref, sc_ref, y_ref):
        i = pl.program_id(0)
        halo = jnp.where(i > 0, halo_ref[...], 0.0)
        pooled = _pooled_groups(halo, cur_ref[...], i * TM)
        for g in range(len(POOL_WINDOWS)):
            sl = slice(g * POOL_GROUP, (g + 1) * POOL_GROUP)
            y = _dot(pooled[g].astype(BF16), w_ref[g].astype(BF16)) * sc_ref[:, sl]
            y_ref[:, sl] = y.astype(BF16)

    return pl.pallas_call(
        body, name="pool_fwd", grid=(S // TM,),
        in_specs=[pl.BlockSpec((TM, POOL_WIDTH), lambda i: (i, 0)),
                  pl.BlockSpec((POOL_HALO, POOL_WIDTH), lambda i: (jnp.maximum(i * HB - 1, 0), 0)),
                  _const_spec(pool_w.shape), _const_spec((1, POOL_WIDTH))],
        out_specs=pl.BlockSpec((TM, POOL_WIDTH), lambda i: (i, 0)),
        out_shape=jax.ShapeDtypeStruct((S, POOL_WIDTH), BF16),
        compiler_params=_params(("parallel",)),
    )(pool_in, pool_in, pool_w, pool_scale)


def _mix_out_fwd(attn, pool, w_out, x, g_post, g_ffn_pre):
    S, D = x.shape
    TM = 512

    def body(a_ref, p_ref, w_ref, x_ref, gp_ref, gf_ref, mixed_ref, x1_ref, h2_ref, cat_ref):
        ab = a_ref[...]
        cat_ref[:, :ATTN_WIDTH] = ab
        cat_ref[:, ATTN_WIDTH:] = p_ref[...]
        mixed = _dot(ab, w_ref[:ATTN_WIDTH, :]) + _dot(p_ref[...], w_ref[ATTN_WIDTH:, :])
        mixed_ref[...] = mixed.astype(BF16)
        n, _ = _rms_stats(mixed)
        x1 = x_ref[...] + n * gp_ref[...]
        x1_ref[...] = x1
        n2, _ = _rms_stats(x1)
        h2_ref[...] = (n2 * gf_ref[...]).astype(BF16)

    row = lambda w: pl.BlockSpec((TM, w), lambda i: (i, 0))
    return pl.pallas_call(
        body, name="mix_out_fwd", grid=(S // TM,),
        in_specs=[row(ATTN_WIDTH), row(POOL_WIDTH), _const_spec(w_out.shape), row(D),
                  _const_spec((1, D)), _const_spec((1, D))],
        out_specs=[row(D), row(D), row(D), row(D)],
        out_shape=[jax.ShapeDtypeStruct((S, D), BF16), jax.ShapeDtypeStruct((S, D), F32),
                   jax.ShapeDtypeStruct((S, D), BF16), jax.ShapeDtypeStruct((S, D), BF16)],
        compiler_params=_params(("parallel",), VMEM_LIMIT),
    )(attn, pool, w_out, x, g_post, g_ffn_pre)


def _ffn_fwd(h2, x1, target, w_up, w_down, conv_w, conv_b, g_post):
    S, D = x1.shape
    CW = w_up.shape[2]
    FF = 2 * CW
    TM = 256
    piece = 4 * LANES
    pieces = [(lo, min(lo + piece, CW)) for lo in range(0, CW, piece)]

    def body(h2_ref, x1_ref, t_ref, wu_ref, wd_ref, cw_ref, cb_ref, g_ref,
             yv_ref, dy_ref, df_ref, dc_ref, loss_ref, dg_ref, dcb_ref, dcw_ref,
             ue_s, dgate_s, dval_s):
        i = pl.program_id(0)

        @pl.when(i == 0)
        def _():
            loss_ref[...] = jnp.zeros_like(loss_ref)
            dg_ref[...] = jnp.zeros_like(dg_ref)
            dcb_ref[...] = jnp.zeros_like(dcb_ref)
            dcw_ref[...] = jnp.zeros_like(dcw_ref)
            ue_s[0:CONV_HALO, :] = jnp.zeros((CONV_HALO, 2 * FF), F32)

        @pl.when(i > 0)
        def _():
            ue_s[0:CONV_HALO, :] = ue_s[TM:TM + CONV_HALO, :]

        def shifted(cols, k):
            return pltpu.roll(ue_s[:, cols], k, 0)[CONV_HALO:]

        def conv(cols):
            return (cb_ref[:, cols] + cw_ref[2, :, cols] * ue_s[CONV_HALO:, cols]
                    + cw_ref[1, :, cols] * shifted(cols, 1) + cw_ref[0, :, cols] * shifted(cols, 2))

        hb = h2_ref[...]
        f = jnp.zeros((TM, D), F32)
        for j in range(2):
            jc = slice(j * CW, (j + 1) * CW)
            for half in range(2):
                blk = 2 * half + j
                cols = slice(blk * CW, (blk + 1) * CW)
                ue_s[CONV_HALO:, cols] = _dot(hb, wu_ref[blk])
            for lo, hi in pieces:
                pc = slice(j * CW + lo, j * CW + hi)
                gelu, dgelu = _gelu_tanh(conv(pc).astype(BF16))
                val = conv(slice(FF + j * CW + lo, FF + j * CW + hi)).astype(BF16)
                dgate_s[:, pc] = val * dgelu
                dval_s[:, pc] = gelu
                yv_ref[:, pc] = gelu * val
            f = f + _dot(yv_ref[:, jc], wd_ref[jc, :])

        n, r = _rms_stats(f)
        err = x1_ref[...] + n * g_ref[...] - t_ref[...]
        loss_ref[...] += 0.5 * jnp.sum(jnp.mean(err * err, axis=-1, keepdims=True), axis=0, keepdims=True)
        dy = err / D
        dy_ref[...] = dy
        df, dg = _rms_bwd(dy, n, r, g_ref[...])
        dg_ref[...] += dg
        dfb = df.astype(BF16)
        df_ref[...] = dfb

        for j in range(2):
            jc = slice(j * CW, (j + 1) * CW)
            dyv = _dot_nt(dfb, wd_ref[jc, :])
            for lo, hi in pieces:
                pc = slice(j * CW + lo, j * CW + hi)
                for half, scale_s in ((0, dgate_s), (1, dval_s)):
                    cols = slice(half * FF + j * CW + lo, half * FF + j * CW + hi)
                    dcv = dyv[:, lo:hi] * scale_s[:, pc].astype(F32)
                    dc_ref[:, cols] = dcv.astype(BF16)
                    dcb_ref[:, cols] += jnp.sum(dcv, axis=0, keepdims=True)
                    dcw_ref[2, :, cols] += jnp.sum(dcv * ue_s[CONV_HALO:, cols], axis=0, keepdims=True)
                    dcw_ref[1, :, cols] += jnp.sum(dcv * shifted(cols, 1), axis=0, keepdims=True)
                    dcw_ref[0, :, cols] += jnp.sum(dcv * shifted(cols, 2), axis=0, keepdims=True)

    row = lambda w: pl.BlockSpec((TM, w), lambda i: (i, 0))
    acc = lambda shape: pl.BlockSpec(shape, lambda i: (0,) * len(shape))
    return pl.pallas_call(
        body, name="ffn_fwd", grid=(S // TM,),
        in_specs=[row(D), row(D), row(D), _const_spec(w_up.shape), _const_spec(w_down.shape),
                  _const_spec(conv_w.shape), _const_spec((1, 2 * FF)), _const_spec((1, D))],
        out_specs=[row(FF), row(D), row(D), row(2 * FF),
                   acc((1, 1)), acc((1, D)), acc((1, 2 * FF)), acc((CONV_WIDTH, 1, 2 * FF))],
        out_shape=[jax.ShapeDtypeStruct((S, FF), BF16),
                   jax.ShapeDtypeStruct((S, D), F32), jax.ShapeDtypeStruct((S, D), BF16),
                   jax.ShapeDtypeStruct((S, 2 * FF), BF16),
                   jax.ShapeDtypeStruct((1, 1), F32), jax.ShapeDtypeStruct((1, D), F32),
                   jax.ShapeDtypeStruct((1, 2 * FF), F32), jax.ShapeDtypeStruct((CONV_WIDTH, 1, 2 * FF), F32)],
        scratch_shapes=[pltpu.VMEM((TM + CONV_HALO, 2 * FF), F32), pltpu.VMEM((TM, FF), BF16),
                        pltpu.VMEM((TM, FF), BF16)],
        compiler_params=_params(("arbitrary",), VMEM_LIMIT),
    )(h2, x1, target, w_up, w_down, conv_w, conv_b, g_post)


def _ffn_bwd(dc, conv_w, w_up, x1, g_ffn_pre, dy):
    S, D = x1.shape
    CW = w_up.shape[2]
    F2 = 4 * CW
    TM = 256
    HB = TM // CONV_HALO
    last = S // CONV_HALO - 1
    n_tiles = S // TM

    def body(dc_ref, halo_ref, cw_ref, wu_ref, x1_ref, g_ref, dy_ref, du_ref, dx1_ref, dg_ref):
        i = pl.program_id(0)

        @pl.when(i == 0)
        def _():
            dg_ref[...] = jnp.zeros_like(dg_ref)

        keep = i < n_tiles - 1
        dh2 = jnp.zeros((TM, D), F32)
        for blk in range(N_SHARD):
            cols = slice(blk * CW, (blk + 1) * CW)
            halo = jnp.where(keep, halo_ref[:, cols].astype(F32), 0.0)
            dce = jnp.concatenate([dc_ref[:, cols].astype(F32), halo], axis=0)
            rows = TM + CONV_HALO
            du = (cw_ref[2, :, cols] * dce[:TM]
                  + cw_ref[1, :, cols] * pltpu.roll(dce, rows - 1, 0)[:TM]
                  + cw_ref[0, :, cols] * pltpu.roll(dce, rows - 2, 0)[:TM])
            dub = du.astype(BF16)
            du_ref[:, cols] = dub
            dh2 = dh2 + _dot_nt(dub, wu_ref[blk])
        n2, r2 = _rms_stats(x1_ref[...])
        dx, dg = _rms_bwd(dh2, n2, r2, g_ref[...])
        dg_ref[...] += dg
        dx1_ref[...] = (dy_ref[...] + dx).astype(BF16)

    row = lambda w: pl.BlockSpec((TM, w), lambda i: (i, 0))
    return pl.pallas_call(
        body, name="ffn_bwd", grid=(S // TM,),
        in_specs=[row(F2), pl.BlockSpec((CONV_HALO, F2), lambda i: (jnp.minimum((i + 1) * HB, last), 0)),
                  _const_spec(conv_w.shape), _const_spec(w_up.shape), row(D), _const_spec((1, D)), row(D)],
        out_specs=[row(F2), row(D), pl.BlockSpec((1, D), lambda i: (0, 0))],
        out_shape=[jax.ShapeDtypeStruct((S, F2), BF16), jax.ShapeDtypeStruct((S, D), BF16),
                   jax.ShapeDtypeStruct((1, D), F32)],
        compiler_params=_params(("arbitrary",), VMEM_LIMIT),
    )(dc, dc, conv_w, w_up, x1, g_ffn_pre, dy)


def _matmul_tn(a, b, n_blocks, name, cargo=()):
    S, M = a.shape
    N = b.shape[1]
    tn = N // n_blocks
    tm = M if M <= 1024 else M // 2
    tk = 2048
    nk = S // tk
    nc = len(cargo)
    kinds = [kind for kind, _ in cargo]

    def body(*refs):
        a_ref, b_ref, o_ref = refs[0], refs[1], refs[2 + nc]
        cargo_refs = (kinds, refs[2:2 + nc], refs[3 + nc:3 + 2 * nc], refs[3 + 2 * nc:])
        i, j, k = pl.program_id(0), pl.program_id(1), pl.program_id(2)
        if nc:
            _cargo_start(*cargo_refs, (i == 0) & (j == 0) & (k == 0))

        @pl.when(k == 0)
        def _():
            o_ref[...] = jnp.zeros_like(o_ref)
        o_ref[0] += _dot_tn(a_ref[...], b_ref[...])
        if nc:
            _cargo_finish(*cargo_refs, (i == M // tm - 1) & (j == n_blocks - 1) & (k == nk - 1))

    arrays, specs, shapes, aliases, sems = _cargo_call(cargo, 2, 1)
    semantics = ("arbitrary",) * 3 if nc else ("parallel", "parallel", "arbitrary")
    out = pl.pallas_call(
        body, name=name, grid=(M // tm, n_blocks, nk),
        in_specs=[pl.BlockSpec((tk, tm), lambda i, j, k: (k, i)), pl.BlockSpec((tk, tn), lambda i, j, k: (k, j))] + specs,
        out_specs=[pl.BlockSpec((1, tm, tn), lambda i, j, k: (j, i, 0))] + specs,
        out_shape=[jax.ShapeDtypeStruct((n_blocks, M, tn), F32)] + shapes,
        input_output_aliases=aliases, scratch_shapes=sems,
        compiler_params=_params(semantics, VMEM_LIMIT),
    )(a, b, *arrays)
    return out[0], out[1:]


def _mix_out_bwd(dx1, mixed, g_post, w_out, attn, cargo=()):
    S, D = dx1.shape
    TM = 512
    nd = len(DILATIONS)
    nc = len(cargo)
    kinds = [kind for kind, _ in cargo]
    n_chunks = ATTN_WIDTH // LANES

    def body(*refs):
        dx_ref, m_ref, g_ref, w_ref, a_ref = refs[:5]
        dm_ref, dp_ref, dg_ref = refs[5 + nc:8 + nc]
        da_refs, dl_refs = refs[8 + nc:8 + nc + nd], refs[8 + nc + nd:8 + nc + 2 * nd]
        n_out = 8 + nc + 2 * nd
        t_s = refs[n_out + nc:n_out + nc + n_chunks]
        c_s = refs[n_out + nc + n_chunks:n_out + nc + n_chunks + 1]
        cargo_refs = (kinds, refs[5:5 + nc], refs[n_out:n_out + nc], refs[n_out + nc + n_chunks + 1:])
        if nc:
            _cargo_start(*cargo_refs, pl.program_id(0) == 0)

        @pl.when(pl.program_id(0) == 0)
        def _():
            dg_ref[...] = jnp.zeros_like(dg_ref)

        n, r = _rms_stats(m_ref[...].astype(F32))
        dm, dg = _rms_bwd(dx_ref[...].astype(F32), n, r, g_ref[...])
        dg_ref[...] += dg
        dmb = dm.astype(BF16)
        dm_ref[...] = dmb
        da = _dot_nt(dmb, w_ref[:ATTN_WIDTH, :])
        _put_tokens(t_s, da)
        for i, d in enumerate(DILATIONS):
            _to_residue(da, t_s, da_refs[i], d, BF16)
        dp_ref[...] = _dot_nt(dmb, w_ref[ATTN_WIDTH:, :]).astype(BF16)
        gather = _head_sum_matrix()
        delta = sum(_dot(p, gather) for p in _bf16_pieces(da * a_ref[...].astype(F32), 2))
        _put_tokens(c_s, delta)
        for i, d in enumerate(DILATIONS):
            _to_residue(delta, c_s, dl_refs[i], d, F32)
        if nc:
            _cargo_finish(*cargo_refs, pl.program_id(0) == S // TM - 1)

    row = lambda w: pl.BlockSpec((TM, w), lambda i: (i, 0))
    specs = [_residue_spec(TM, d) for d in DILATIONS]
    arrays, cargo_specs, shapes, aliases, sems = _cargo_call(cargo, 5, 3 + 2 * nd)
    out = pl.pallas_call(
        body, name="mix_out_bwd", grid=(S // TM,),
        in_specs=[row(D), row(D), _const_spec((1, D)), _const_spec(w_out.shape), row(ATTN_WIDTH)] + cargo_specs,
        out_specs=[row(D), row(POOL_WIDTH), pl.BlockSpec((1, D), lambda i: (0, 0))] + specs
        + [_residue_spec(TM, d, STAT_WIDTH) for d in DILATIONS] + cargo_specs,
        out_shape=[jax.ShapeDtypeStruct((S, D), BF16), jax.ShapeDtypeStruct((S, POOL_WIDTH), BF16),
                   jax.ShapeDtypeStruct((1, D), F32)]
        + [_residue_shape(S, d, BF16) for d in DILATIONS]
        + [_residue_shape(S, d, F32, STAT_WIDTH) for d in DILATIONS] + shapes,
        input_output_aliases=aliases,
        scratch_shapes=_token_scratch(TM) + _token_scratch(TM, STAT_WIDTH) + sems,
        compiler_params=_params(("arbitrary",), VMEM_LIMIT),
    )(dx1, mixed, g_post, w_out, attn, *arrays)
    return out[0], out[1], out[2], out[3:3 + nd], out[3 + nd:3 + 2 * nd], out[3 + 2 * nd:]


def _pool_bwd(pool_in, d_pool, pool_w, pool_scale):
    S = pool_in.shape[0]
    TM = 512
    HB = TM // POOL_HALO
    last = S // POOL_HALO - 1
    G = len(POOL_WINDOWS)

    def body(cur_ref, halo_ref, dcur_ref, dnext_ref, w_ref, sc_ref, dxin_ref, dw_ref, dsc_ref):
        i = pl.program_id(0)

        @pl.when(i == 0)
        def _():
            dw_ref[...] = jnp.zeros_like(dw_ref)
            dsc_ref[...] = jnp.zeros_like(dsc_ref)

        halo = jnp.where(i > 0, halo_ref[...], 0.0)
        pooled = _pooled_groups(halo, cur_ref[...], i * TM)
        dnext = jnp.where(i < S // TM - 1, dnext_ref[...].astype(F32), 0.0)
        dye = jnp.concatenate([dcur_ref[...].astype(F32), dnext], axis=0)
        for g, w in enumerate(POOL_WINDOWS):
            sl = slice(g * POOL_GROUP, (g + 1) * POOL_GROUP)
            wg = w_ref[g].astype(BF16)
            pb = pooled[g].astype(BF16)
            dsc_ref[:, sl] += jnp.sum(dye[:TM, sl] * _dot(pb, wg), axis=0, keepdims=True)
            dpre = (dye[:, sl] * sc_ref[:, sl]).astype(BF16)
            dw_ref[g] += _dot_tn(pb, dpre[:TM])
            dpooled = _dot_nt(dpre, wg)
            z = dpooled / _pool_counts(i * TM, TM + POOL_HALO, w)
            dxin_ref[:, sl] = (_leading_sums(z, w)[:TM] - dpooled[:TM]).astype(BF16)

    row = pl.BlockSpec((TM, POOL_WIDTH), lambda i: (i, 0))
    return pl.pallas_call(
        body, name="pool_bwd", grid=(S // TM,),
        in_specs=[row, pl.BlockSpec((POOL_HALO, POOL_WIDTH), lambda i: (jnp.maximum(i * HB - 1, 0), 0)),
                  row, pl.BlockSpec((POOL_HALO, POOL_WIDTH), lambda i: (jnp.minimum((i + 1) * HB, last), 0)),
                  _const_spec(pool_w.shape), _const_spec((1, POOL_WIDTH))],
        out_specs=[row, pl.BlockSpec((G, POOL_GROUP, POOL_GROUP), lambda i: (0, 0, 0)),
                   pl.BlockSpec((1, POOL_WIDTH), lambda i: (0, 0))],
        out_shape=[jax.ShapeDtypeStruct((S, POOL_WIDTH), BF16), jax.ShapeDtypeStruct((G, POOL_GROUP, POOL_GROUP), F32),
                   jax.ShapeDtypeStruct((1, POOL_WIDTH), F32)],
        compiler_params=_params(("arbitrary",)),
    )(pool_in, pool_in, d_pool, d_pool, pool_w, pool_scale)


def _attn_bwd(q, k, v, d_attn, lse, delta, d, cargo=()):
    L = q.shape[0]
    nb = L // ATTN_BLOCK
    group = min(d, RESIDUES_PER_STEP)
    width = group * ATTN_WIDTH
    nc = len(cargo)
    kinds = [kind for kind, _ in cargo]

    def body(*refs):
        q_ref, kp_ref, kc_ref, vp_ref, vc_ref, do_ref, lse_ref, dl_ref = refs[:8]
        dq_ref, dk_ref, dv_ref = refs[8 + nc:11 + nc]
        ck_s, cv_s = refs[11 + 2 * nc:13 + 2 * nc]
        cargo_refs = (kinds, refs[8:8 + nc], refs[11 + nc:11 + 2 * nc], refs[13 + 2 * nc:])
        r, n = pl.program_id(0), pl.program_id(1)
        if nc:
            _cargo_start(*cargo_refs, (r == 0) & (n == 0))

        @pl.when(n == 0)
        def _():
            ck_s[...] = jnp.zeros_like(ck_s)
            cv_s[...] = jnp.zeros_like(cv_s)

        @pl.when(n < nb)
        def _():
            valid = _band_mask(n)
            valid2 = jnp.concatenate([valid, valid], axis=0)
            first = _first_head_lanes()

            def stacked_column(ref, hp):
                lane = 2 * hp * STAT_LANES
                return jnp.concatenate([ref[:, lane:lane + 1], ref[:, lane + STAT_LANES:lane + STAT_LANES + 1]], axis=0)

            for hp in range(width // LANES):
                sl = slice(hp * LANES, (hp + 1) * LANES)
                qq = _stack_heads(q_ref[:, sl], first)
                dd = _stack_heads(do_ref[:, sl], first)
                kk = jnp.concatenate([kp_ref[:, sl], kc_ref[:, sl]], axis=0)
                vv = jnp.concatenate([vp_ref[:, sl], vc_ref[:, sl]], axis=0)
                s = _dot_nt(qq, kk)
                p = jnp.where(valid2, jnp.exp(s - stacked_column(lse_ref, hp)), 0.0)
                dp = _dot_nt(dd, vv)
                ds = (p * (dp - stacked_column(dl_ref, hp))).astype(BF16)
                dq_ref[:, sl] = (_unstack_heads(_dot(ds, kk), first) * ATTN_SCALE).astype(BF16)
                dk = _dot_tn(ds, qq)
                dv = _dot_tn(p.astype(BF16), dd)
                dk_ref[:, sl] = (ck_s[:, sl] + dk[:ATTN_BLOCK]).astype(BF16)
                dv_ref[:, sl] = (cv_s[:, sl] + dv[:ATTN_BLOCK]).astype(BF16)
                ck_s[:, sl] = dk[ATTN_BLOCK:]
                cv_s[:, sl] = dv[ATTN_BLOCK:]

        @pl.when(n == nb)
        def _():
            dk_ref[...] = ck_s[...].astype(BF16)
            dv_ref[...] = cv_s[...].astype(BF16)

        if nc:
            _cargo_finish(*cargo_refs, (r == d // group - 1) & (n == nb))

    blk = (ATTN_BLOCK, width)
    cur = pl.BlockSpec(blk, lambda r, n: (jnp.minimum(n, nb - 1), r))
    stat = pl.BlockSpec((ATTN_BLOCK, group * STAT_WIDTH), lambda r, n: (jnp.minimum(n, nb - 1), r))
    prev = pl.BlockSpec(blk, lambda r, n: (jnp.maximum(jnp.minimum(n, nb - 1) - 1, 0), r))
    done = pl.BlockSpec(blk, lambda r, n: (jnp.maximum(n - 1, 0), r))
    arrays, specs, shapes, aliases, sems = _cargo_call(cargo, 8, 3)
    out = pl.pallas_call(
        body, name=f"attn_bwd_d{d}", grid=(d // group, nb + 1),
        in_specs=[cur, prev, cur, prev, cur, cur, stat, stat] + specs, out_specs=[cur, done, done] + specs,
        out_shape=[jax.ShapeDtypeStruct((L, d * ATTN_WIDTH), BF16)] * 3 + shapes,
        input_output_aliases=aliases,
        scratch_shapes=[pltpu.VMEM(blk, F32), pltpu.VMEM(blk, F32)] + sems,
        compiler_params=_params(("arbitrary", "arbitrary")),
    )(q, k, k, v, v, d_attn, lse, delta, *arrays)
    return out[:3], out[3:]


def _attn_bwd_consecutive(q, k, v, d_attn, lse, delta, d, cargo=()):
    L = q.shape[0]
    qb = CONSECUTIVE_BLOCKS
    steps = L // (qb * ATTN_BLOCK)
    nc = len(cargo)
    kinds = [kind for kind, _ in cargo]

    def body(*refs):
        q_ref, kp_ref, kc_ref, vp_ref, vc_ref, do_ref, lse_ref, dl_ref = refs[:8]
        dq_ref, dk_ref, dv_ref, ek_ref, ev_ref = refs[8 + nc:13 + nc]
        cargo_refs = (kinds, refs[8:8 + nc], refs[13 + nc:13 + 2 * nc], refs[13 + 2 * nc:])
        r, n = pl.program_id(0), pl.program_id(1)
        if nc:
            _cargo_start(*cargo_refs, (r == 0) & (n == 0))
        first = _first_head_lanes()
        for hp in range(ATTN_WIDTH // LANES):
            sl = slice(hp * LANES, (hp + 1) * LANES)
            for sub in range(qb):
                rows = slice(sub * ATTN_BLOCK, (sub + 1) * ATTN_BLOCK)
                valid = _band_mask(n if sub == 0 else 1)
                valid2 = jnp.concatenate([valid, valid], axis=0)
                if sub == 0:
                    kk = jnp.concatenate([kp_ref[:, sl], kc_ref[rows, sl]], axis=0)
                    vv = jnp.concatenate([vp_ref[:, sl], vc_ref[rows, sl]], axis=0)
                else:
                    keys = slice((sub - 1) * ATTN_BLOCK, (sub + 1) * ATTN_BLOCK)
                    kk, vv = kc_ref[keys, sl], vc_ref[keys, sl]
                qq = _stack_heads(q_ref[rows, sl], first)
                dd = _stack_heads(do_ref[rows, sl], first)
                lane = 2 * hp * STAT_LANES
                column = lambda ref: jnp.concatenate(
                    [ref[rows, lane:lane + 1], ref[rows, lane + STAT_LANES:lane + STAT_LANES + 1]], axis=0)
                p = jnp.where(valid2, jnp.exp(_dot_nt(qq, kk) - column(lse_ref)), 0.0)
                ds = (p * (_dot_nt(dd, vv) - column(dl_ref))).astype(BF16)
                dq_ref[rows, sl] = (_unstack_heads(_dot(ds, kk), first) * ATTN_SCALE).astype(BF16)
                dk = _dot_tn(ds, qq)
                dv = _dot_tn(p.astype(BF16), dd)
                if sub == 0:
                    ek_ref[:, sl] = dk[:ATTN_BLOCK].astype(BF16)
                    ev_ref[:, sl] = dv[:ATTN_BLOCK].astype(BF16)
                else:
                    before = slice((sub - 1) * ATTN_BLOCK, sub * ATTN_BLOCK)
                    dk_ref[before, sl] = (carry_k + dk[:ATTN_BLOCK]).astype(BF16)
                    dv_ref[before, sl] = (carry_v + dv[:ATTN_BLOCK]).astype(BF16)
                carry_k, carry_v = dk[ATTN_BLOCK:], dv[ATTN_BLOCK:]
            dk_ref[rows, sl] = carry_k.astype(BF16)
            dv_ref[rows, sl] = carry_v.astype(BF16)
        if nc:
            _cargo_finish(*cargo_refs, (r == d - 1) & (n == steps - 1))

    cur = pl.BlockSpec((qb * ATTN_BLOCK, ATTN_WIDTH), lambda r, n: (n, r))
    prev = pl.BlockSpec((ATTN_BLOCK, ATTN_WIDTH), lambda r, n: (jnp.maximum(n * qb - 1, 0), r))
    edge = pl.BlockSpec((ATTN_BLOCK, ATTN_WIDTH), lambda r, n: (n, r))
    stat = pl.BlockSpec((qb * ATTN_BLOCK, STAT_WIDTH), lambda r, n: (n, r))
    arrays, specs, shapes, aliases, sems = _cargo_call(cargo, 8, 5)
    out = pl.pallas_call(
        body, name=f"attn_bwd_d{d}", grid=(d, steps),
        in_specs=[cur, prev, cur, prev, cur, cur, stat, stat] + specs, out_specs=[cur, cur, cur, edge, edge] + specs,
        out_shape=[jax.ShapeDtypeStruct((L, d * ATTN_WIDTH), BF16)] * 3
        + [jax.ShapeDtypeStruct((steps * ATTN_BLOCK, d * ATTN_WIDTH), BF16)] * 2 + shapes,
        input_output_aliases=aliases, scratch_shapes=sems,
        compiler_params=_params(("arbitrary", "arbitrary")),
    )(q, k, k, v, v, d_attn, lse, delta, *arrays)
    return out[:3], out[3:5], out[5:]


def _mix_in_bwd(dqkv, edges, d_pool_in, w_in, x, g_pre, dx1):
    S, D = x.shape
    TM = CONSECUTIVE_BLOCKS * ATTN_BLOCK
    nd = len(DILATIONS)
    n_tiles = S // TM

    def body(*refs):
        g_refs = refs[:3 * nd]
        e_refs = (None,) + refs[3 * nd:3 * nd + 2]
        dpi_ref, w_ref, x_ref, g_ref, dx1_ref, dproj_ref, gx_ref, dg_ref = refs[3 * nd + 2:3 * nd + 10]
        t_s = refs[3 * nd + 10:]

        @pl.when(pl.program_id(0) == 0)
        def _():
            dg_ref[...] = jnp.zeros_like(dg_ref)

        dh = jnp.zeros((TM, D), F32)
        for a in range(4):
            if a < 3:
                tot = g_refs[a][...].astype(F32)
                if a > 0:
                    late = jnp.where(pl.program_id(0) < n_tiles - 1, e_refs[a][...].astype(F32), 0.0)
                    tot = jnp.concatenate([tot[:TM - ATTN_BLOCK], tot[TM - ATTN_BLOCK:] + late], axis=0)
                for i, d in enumerate(DILATIONS[1:]):
                    tot = tot + _from_residue(g_refs[3 * (i + 1) + a], t_s, d)
                db = tot.astype(BF16)
            else:
                db = dpi_ref[...]
            dproj_ref[:, a * ATTN_WIDTH:(a + 1) * ATTN_WIDTH] = db
            dh = dh + _dot_nt(db, w_ref[a])
        n, r = _rms_stats(x_ref[...])
        dx, dg = _rms_bwd(dh, n, r, g_ref[...])
        dg_ref[...] += dg
        gx_ref[...] = dx1_ref[...].astype(F32) + dx

    row = lambda w: pl.BlockSpec((TM, w), lambda i: (i, 0))
    edge = pl.BlockSpec((ATTN_BLOCK, ATTN_WIDTH), lambda i: (jnp.minimum(i + 1, n_tiles - 1), 0))
    return pl.pallas_call(
        body, name="mix_in_bwd", grid=(S // TM,),
        in_specs=[_residue_spec(TM, d) for d in DILATIONS for _ in range(3)] + [edge, edge]
        + [row(POOL_WIDTH), _const_spec(w_in.shape), row(D), _const_spec((1, D)), row(D)],
        out_specs=[row(4 * ATTN_WIDTH), row(D), pl.BlockSpec((1, D), lambda i: (0, 0))],
        out_shape=[jax.ShapeDtypeStruct((S, 4 * ATTN_WIDTH), BF16), jax.ShapeDtypeStruct((S, D), F32),
                   jax.ShapeDtypeStruct((1, D), F32)],
        scratch_shapes=_token_scratch(TM),
        compiler_params=_params(("arbitrary",), VMEM_LIMIT),
    )(*[g for gs in dqkv for g in gs], *edges, d_pool_in, w_in, x, g_pre, dx1)


SMALL_EARLY = ("pool_w", "pool_scale", "g_mix_post", "g_ffn_pre", "conv_b", "g_ffn_post", "conv_w")
SMALL_LATE = ("g_mix_pre",)


def _pack_small(grads, names):
    parts = []
    for n in names:
        g = grads[n]
        if n == "conv_w":
            g = g.reshape(CONV_WIDTH, N_SHARD, -1).transpose(1, 0, 2)
        parts.append(g.reshape(-1, LANES))
    return jnp.concatenate(parts, axis=0) if len(parts) > 1 else parts[0]


def _unpack_small(packed, names, like, shard):
    out, row = {}, 0
    for n in names:
        size = like[n].size * (N_SHARD if n == "conv_w" else 1)
        g = packed[row:row + size // LANES]
        row += size // LANES
        if n == "conv_w":
            g = lax.dynamic_slice_in_dim(g.reshape((N_SHARD,) + like[n].shape), shard, 1, axis=0)[0]
        out[n] = g.reshape(like[n].shape)
    return out


def _local_step(x, target, g_mix_pre, w_in, pool_w, pool_scale, w_out, g_mix_post, g_ffn_pre,
                w_up, conv_w, conv_b, w_down, g_ffn_post, mesh_pos=None):
    on_mesh = mesh_pos is not None
    D = x.shape[1]
    CW = w_up.shape[2]
    qkv, pool_in, h1, got = _mix_in_fwd(x, g_mix_pre, w_in, [("ici", w_up)] if on_mesh else ())
    w_up = got[0] if on_mesh else w_up
    o1, l1, got = _attn_fwd(*qkv[0], 1, [("d2d", w_up), ("ici", w_out)] if on_mesh else ())
    w_up, w_out = got if on_mesh else (w_up, w_out)
    o4, l4, got = _attn_fwd(*qkv[1], 4, [("d2d", w_out), ("ici", w_down)] if on_mesh else ())
    w_out, w_down = got if on_mesh else (w_out, w_down)
    o16, l16, got = _attn_fwd(*qkv[2], 16, [("d2d", w_down)] if on_mesh else ())
    w_down = got[0] if on_mesh else w_down
    w_out = w_out.reshape(D, D)
    w_down = w_down.reshape(2 * CW, D)
    attn, lse = _attn_mix((o1, o4, o16), (l1, l4, l16))
    pool = _pool_fwd(pool_in, pool_w, pool_scale)
    mixed, x1, h2, cat = _mix_out_fwd(attn, pool, w_out, x, g_mix_post, g_ffn_pre)

    yv, dy, df, dc, loss, d_g_ffn_post, d_conv_b, d_conv_w = _ffn_fwd(
        h2, x1, target, w_up, w_down, conv_w, conv_b, g_ffn_post)
    du, dx1, d_g_ffn_pre = _ffn_bwd(dc, conv_w, w_up, x1, g_ffn_pre, dy)
    d_w_up = _matmul_tn(h2, du, N_SHARD, "grad_w_up")[0]
    d_w_down = _matmul_tn(yv, df, 1, "grad_w_down")[0][0].reshape(N_SHARD, CW // 2, D)
    swap = [("swap", d_w_up), ("swap", d_w_down)] if on_mesh else ()
    d_mixed, d_pool, d_g_mix_post, d_attn, delta, from_sibling = _mix_out_bwd(dx1, mixed, g_mix_post, w_out, attn, swap)
    d_w_out = _matmul_tn(cat, d_mixed, 1, "grad_w_out")[0][0].reshape(N_SHARD, D // N_SHARD, D)
    d_pool_in, d_pool_w, d_pool_scale = _pool_bwd(pool_in, d_pool, pool_w, pool_scale)
    grads = dict(pool_w=d_pool_w, pool_scale=d_pool_scale, w_out=d_w_out, g_mix_post=d_g_mix_post,
                 g_ffn_pre=d_g_ffn_pre, w_up=d_w_up, conv_w=d_conv_w, conv_b=d_conv_b, w_down=d_w_down,
                 g_ffn_post=d_g_ffn_post)
    cargo = [(), (), ()]
    if on_mesh:
        c_arr, device, shard_arr = mesh_pos
        up_f32, up_bf16 = _pair_sum(d_w_up, from_sibling[0], c_arr, "pair_sum_w_up")
        down_f32, down_bf16 = _pair_sum(d_w_down, from_sibling[1], c_arr, "pair_sum_w_down")
        early = _pack_small(grads, SMALL_EARLY)
        early_slots = lax.dynamic_update_index_in_dim(jnp.zeros((8,) + early.shape, F32), early, device, 0)
        cargo = [[("scatter", down_bf16), ("everyone", early_slots)], [("scatter", up_bf16)], []]

    dqkv1, edges, landed1 = _attn_bwd_consecutive(*qkv[0], d_attn[0], lse[0], delta[0], 1, cargo[0])
    dqkv4, landed4 = _attn_bwd(*qkv[1], d_attn[1], lse[1], delta[1], 4, cargo[1])
    if on_mesh:
        halves = [_shard_sum(up_f32, landed4[0], shard_arr, c_arr, "shard_sum_w_up"),
                  _shard_sum(down_f32, landed1[0], shard_arr, c_arr, "shard_sum_w_down")]
        cargo[2] = cargo[2] + [("join", h) for h in halves]
    if qkv[2][0].shape[0] == CONSECUTIVE_BLOCKS * ATTN_BLOCK:
        dqkv16, _, landed16 = _attn_bwd_consecutive(*qkv[2], d_attn[2], lse[2], delta[2], 16, cargo[2])
    else:
        dqkv16, landed16 = _attn_bwd(*qkv[2], d_attn[2], lse[2], delta[2], 16, cargo[2])
    if on_mesh:
        grads.update(small_early=landed1[1], w_up=landed16[0], w_down=landed16[1])
    d_proj, grad_x, grads["g_mix_pre"] = _mix_in_bwd((dqkv1, dqkv4, dqkv16), edges, d_pool_in, w_in, x, g_mix_pre, dx1)
    late = []
    if on_mesh:
        small = jnp.concatenate([_pack_small(grads, SMALL_LATE), jnp.pad(loss, ((0, 7), (0, LANES - 1)))], axis=0)
        late = [("everyone", lax.dynamic_update_index_in_dim(jnp.zeros((8,) + small.shape, F32), small, device, 0))]
    grads["w_in"], landed_late = _matmul_tn(h1, d_proj, N_SHARD, "grad_w_in", late)
    if on_mesh:
        grads["small_late"] = landed_late[0]
    return loss, grad_x, grads


ANY = pl.BlockSpec(memory_space=pl.ANY)


def _position():
    x, y, c = lax.axis_index("x"), lax.axis_index("y"), lax.axis_index("c")
    chips = [(1 - x, y), (x, 1 - y), (1 - x, 1 - y)]
    return x, y, c, chips


def _remote(src, dst, send_sem, recv_sem, to):
    return pltpu.make_async_remote_copy(src_ref=src, dst_ref=dst, send_sem=send_sem, recv_sem=recv_sem,
                                        device_id=to, device_id_type=MESH)


def _cast_bf16(w, shard_arr, name):
    R, C = w.shape
    tr = R // 2

    def body(s_ref, w_ref, o_ref):
        o_ref[0] = w_ref[...].astype(BF16)

    return pl.pallas_call(
        body, name=name,
        grid_spec=pltpu.PrefetchScalarGridSpec(
            num_scalar_prefetch=1, grid=(2,),
            in_specs=[pl.BlockSpec((tr, C), lambda i, s_ref: (i, 0))],
            out_specs=pl.BlockSpec((1, tr, C), lambda i, s_ref: (s_ref[0], i, 0))),
        out_shape=jax.ShapeDtypeStruct((N_SHARD, R, C), BF16),
        compiler_params=_params(("parallel",)))(shard_arr, w)


def _gather_weights(bufs):
    n = len(bufs) - 1

    def body(*refs):
        outs, cw_out = refs[n + 1:2 * n + 1], refs[2 * n + 1]
        ici_send, ici_recv, d2d_send, d2d_recv = refs[2 * n + 2:]
        x, y, c, chips = _position()
        s = 2 * x + y
        sibling = (x, y, 1 - c)

        def half(a, shard, h):
            rows = outs[a].shape[1] // 2
            return outs[a].at[shard, pl.ds(h * rows, rows), :]

        sends = []
        for a in range(n):
            for j, (px, py) in enumerate(chips):
                sends.append(_remote(half(a, s, c), half(a, s, c),
                                     ici_send.at[3 * a + j], ici_recv.at[3 * a + j], (px, py, c)))
        for j, (px, py) in enumerate(chips):
            sends.append(_remote(cw_out.at[s], cw_out.at[s], ici_send.at[3 * n + j], ici_recv.at[3 * n + j], (px, py, c)))
        for cp in sends:
            cp.start()
        passed = []
        for a in range(n):
            for j, (px, py) in enumerate(chips):
                sj = 2 * px + py
                got = half(a, sj, c)
                _remote(got, got, ici_send.at[3 * a + j], ici_recv.at[3 * a + j], (px, py, c)).wait_recv()
                fwd = _remote(got, got, d2d_send.at[3 * a + j], d2d_recv.at[3 * a + j], sibling)
                fwd.start()
                passed.append(fwd)
        for j, (px, py) in enumerate(chips):
            got = cw_out.at[2 * px + py]
            _remote(got, got, ici_send.at[3 * n + j], ici_recv.at[3 * n + j], (px, py, c)).wait_recv()
        for a in range(n):
            for j, (px, py) in enumerate(chips):
                got = half(a, 2 * px + py, 1 - c)
                _remote(got, got, d2d_send.at[3 * a + j], d2d_recv.at[3 * a + j], sibling).wait_recv()
        for cp in sends + passed:
            cp.wait_send()

    return pl.pallas_call(
        body, name="gather_weights",
        in_specs=[ANY] * (n + 1), out_specs=[ANY] * (n + 1),
        out_shape=[jax.ShapeDtypeStruct(b.shape, b.dtype) for b in bufs],
        input_output_aliases={i: i for i in range(n + 1)},
        scratch_shapes=[pltpu.SemaphoreType.DMA((3 * n + 3,)), pltpu.SemaphoreType.DMA((3 * n + 3,)),
                        pltpu.SemaphoreType.DMA((3 * n,)), pltpu.SemaphoreType.DMA((3 * n,))],
        compiler_params=pltpu.CompilerParams(has_side_effects=True),
    )(*bufs)


def _swap_halves(grads, tag):
    n = len(grads)

    def body(*refs):
        ins, outs, send_sem, recv_sem = refs[:n], refs[n:2 * n], refs[2 * n], refs[2 * n + 1]
        x, y, c, _ = _position()
        copies = []
        for a in range(n):
            rows = ins[a].shape[1] // 2
            copies.append(_remote(ins[a].at[:, pl.ds((1 - c) * rows, rows), :], outs[a],
                                  send_sem.at[a], recv_sem.at[a], (x, y, 1 - c)))
        for cp in copies:
            cp.start()
        for cp in copies:
            cp.wait()

    return pl.pallas_call(
        body, name="swap_grad_halves_" + tag,
        in_specs=[ANY] * n, out_specs=[ANY] * n,
        out_shape=[jax.ShapeDtypeStruct((g.shape[0], g.shape[1] // 2, g.shape[2]), F32) for g in grads],
        scratch_shapes=[pltpu.SemaphoreType.DMA((n,)), pltpu.SemaphoreType.DMA((n,))],
        compiler_params=pltpu.CompilerParams(has_side_effects=True),
    )(*grads)


def _pair_sum(g, got, c_arr, name):
    n_sh, R, C = g.shape
    rows = R // 2

    def body(c_ref, g_ref, r_ref, f_ref, b_ref):
        t = g_ref[...] + r_ref[...]
        f_ref[...] = t
        b_ref[...] = t.astype(BF16)

    blk = pl.BlockSpec((1, rows, C), lambda i, c_ref: (i, 0, 0))
    return pl.pallas_call(
        body, name=name,
        grid_spec=pltpu.PrefetchScalarGridSpec(
            num_scalar_prefetch=1, grid=(n_sh,),
            in_specs=[pl.BlockSpec((1, rows, C), lambda i, c_ref: (i, c_ref[0], 0)), blk],
            out_specs=[blk, blk]),
        out_shape=[jax.ShapeDtypeStruct((n_sh, rows, C), F32), jax.ShapeDtypeStruct((n_sh, rows, C), BF16)],
        compiler_params=_params(("parallel",)),
    )(c_arr, g, got)


def _shard_sum(sums_f32, recv, shard_arr, c_arr, name):
    _, rows, C = sums_f32.shape

    def body(s_ref, c_ref, o_ref, r_ref, t_ref):
        t_ref[...] = ((o_ref[0] + r_ref[0].astype(F32)) + r_ref[1].astype(F32)) + r_ref[2].astype(F32)

    return pl.pallas_call(
        body, name=name,
        grid_spec=pltpu.PrefetchScalarGridSpec(
            num_scalar_prefetch=2, grid=(1,),
            in_specs=[pl.BlockSpec((1, rows, C), lambda i, s_ref, c_ref: (s_ref[0], 0, 0)),
                      pl.BlockSpec((3, rows, C), lambda i, s_ref, c_ref: (0, 0, 0))],
            out_specs=pl.BlockSpec((rows, C), lambda i, s_ref, c_ref: (c_ref[0], 0))),
        out_shape=jax.ShapeDtypeStruct((2 * rows, C), F32),
        compiler_params=_params(("arbitrary",)),
    )(shard_arr, c_arr, sums_f32, recv)


def _join_halves(bufs):
    n = len(bufs)

    def body(*refs):
        outs, send_sem, recv_sem = refs[n:2 * n], refs[2 * n], refs[2 * n + 1]
        x, y, c, _ = _position()
        copies = []
        for a in range(n):
            rows = outs[a].shape[0] // 2
            mine = outs[a].at[pl.ds(c * rows, rows), :]
            copies.append(_remote(mine, mine, send_sem.at[a], recv_sem.at[a], (x, y, 1 - c)))
        for cp in copies:
            cp.start()
        for a, cp in enumerate(copies):
            cp.wait_send()
            rows = outs[a].shape[0] // 2
            theirs = outs[a].at[pl.ds((1 - c) * rows, rows), :]
            _remote(theirs, theirs, send_sem.at[a], recv_sem.at[a], (x, y, 1 - c)).wait_recv()

    return pl.pallas_call(
        body, name="join_grad_halves",
        in_specs=[ANY] * n, out_specs=[ANY] * n,
        out_shape=[jax.ShapeDtypeStruct(b.shape, F32) for b in bufs],
        input_output_aliases={i: i for i in range(n)},
        scratch_shapes=[pltpu.SemaphoreType.DMA((n,)), pltpu.SemaphoreType.DMA((n,))],
        compiler_params=pltpu.CompilerParams(has_side_effects=True),
    )(*bufs)


def _cargo_alone(cargo, name):
    kinds = [kind for kind, _ in cargo]
    nc = len(cargo)

    def body(*refs):
        cargo_refs = (kinds, refs[:nc], refs[nc:2 * nc], refs[2 * nc:])
        _cargo_start(*cargo_refs, pl.program_id(0) == 0)
        _cargo_finish(*cargo_refs, pl.program_id(0) == 0)

    arrays, specs, shapes, aliases, sems = _cargo_call(cargo, 0, 0)
    return pl.pallas_call(
        body, name=name, grid=(1,), in_specs=specs, out_specs=specs, out_shape=shapes,
        input_output_aliases=aliases, scratch_shapes=sems,
        compiler_params=pltpu.CompilerParams(has_side_effects=True),
    )(*arrays)


def _small_sum(parts, tag):
    _, R, C = parts.shape

    def body(p_ref, o_ref):
        t = p_ref[0]
        for k in range(1, 8):
            t = t + p_ref[k]
        o_ref[...] = t

    return pl.pallas_call(
        body, name="small_grad_sum_" + tag, grid=(1,),
        in_specs=[pl.BlockSpec((8, R, C), lambda i: (0, 0, 0))], out_specs=pl.BlockSpec((R, C), lambda i: (0, 0)),
        out_shape=jax.ShapeDtypeStruct((R, C), F32), compiler_params=_params(("arbitrary",)),
    )(parts)


def _adamw_math(w, g, m, v):
    m = ADAM_B1 * m + (1.0 - ADAM_B1) * g
    v = ADAM_B2 * v + (1.0 - ADAM_B2) * (g * g)
    m_hat = m / (1.0 - ADAM_B1 ** ADAM_STEP)
    v_hat = v / (1.0 - ADAM_B2 ** ADAM_STEP)
    delta = -ADAM_LR * (m_hat / (jnp.sqrt(v_hat) + ADAM_EPS) + ADAM_WD * w)
    return delta, m, v


def _adamw_big(ws, gs, ms, vs, name, cargo=()):
    n = len(ws)
    nc = len(cargo)
    kinds = [kind for kind, _ in cargo]

    def body(*refs):
        ins, outs = refs[:4 * n], refs[4 * n + nc:7 * n + nc]
        cargo_refs = (kinds, refs[4 * n:4 * n + nc], refs[7 * n + nc:7 * n + 2 * nc], refs[7 * n + 2 * nc:])
        if nc:
            _cargo_start(*cargo_refs, pl.program_id(0) == 0)
        for a in range(n):
            w, g, m, v = (ins[k * n + a][...] for k in range(4))
            outs[a][...], outs[n + a][...], outs[2 * n + a][...] = _adamw_math(w, g, m, v)
        if nc:
            _cargo_finish(*cargo_refs, pl.program_id(0) == 3)

    blks = [pl.BlockSpec((w.shape[0] // 4, w.shape[1]), lambda i: (i, 0)) for w in ws]
    arrays, cargo_specs, shapes, aliases, sems = _cargo_call(cargo, 4 * n, 3 * n)
    out = pl.pallas_call(
        body, name=name, grid=(4,), in_specs=blks * 4 + cargo_specs, out_specs=blks * 3 + cargo_specs,
        out_shape=[jax.ShapeDtypeStruct(w.shape, F32) for w in ws] * 3 + shapes,
        input_output_aliases=aliases, scratch_shapes=sems,
        compiler_params=_params(("arbitrary",)),
    )(*ws, *gs, *ms, *vs, *arrays)
    return (out[:n], out[n:2 * n], out[2 * n:3 * n]), out[3 * n:]


def _adamw_small(ws, gs, ms, vs):
    n = len(ws)

    def body(*refs):
        for a in range(n):
            w, g, m, v = (refs[k * n + a][...] for k in range(4))
            d, nm, nv = _adamw_math(w, g, m, v)
            refs[4 * n + a][...] = d
            refs[5 * n + a][...] = nm
            refs[6 * n + a][...] = nv

    shapes = [jax.ShapeDtypeStruct(w.shape, F32) for w in ws]
    out = pl.pallas_call(body, name="adamw_small", out_shape=shapes * 3)(*ws, *gs, *ms, *vs)
    return out[:n], out[n:2 * n], out[2 * n:]


BIG = ("w_in", "w_out", "w_up", "w_down")
SMALL = ("g_mix_pre", "pool_w", "pool_scale", "g_mix_post", "g_ffn_pre", "conv_b", "g_ffn_post", "conv_w")
ORDER = ("g_mix_pre", "w_in", "pool_w", "pool_scale", "w_out", "g_mix_post", "g_ffn_pre", "w_up", "conv_w", "conv_b",
         "w_down", "g_ffn_post")


def kernel(x, g_mix_pre, w_in, pool_w, pool_scale, w_out, g_mix_post, g_ffn_pre, w_up, conv_w, conv_b, w_down, g_ffn_post, loss_target, m_g_mix_pre, m_w_in, m_pool_w, m_pool_scale, m_w_out, m_g_mix_post, m_g_ffn_pre, m_w_up, m_conv_w, m_conv_b, m_w_down, m_g_ffn_post, v_g_mix_pre, v_w_in, v_pool_w, v_pool_scale, v_w_out, v_g_mix_post, v_g_ffn_pre, v_w_up, v_conv_w, v_conv_b, v_w_down, v_g_ffn_post):
    args = dict(locals())
    W = {n: args[n][0] for n in ORDER}
    M = {n: args["m_" + n][0] for n in ORDER}
    V = {n: args["v_" + n][0] for n in ORDER}
    for d in (W, M, V):
        d["pool_w"] = d["pool_w"].reshape(-1, POOL_GROUP)
        for n in ("g_mix_pre", "pool_scale", "g_mix_post", "g_ffn_pre", "conv_b", "g_ffn_post"):
            d[n] = d[n].reshape(1, -1)
    CW = W["w_up"].shape[1]
    c_arr = lax.axis_index("c").astype(jnp.int32).reshape(1)
    shard = 2 * lax.axis_index("x") + lax.axis_index("y")
    shard_arr = shard.astype(jnp.int32).reshape(1)
    device = 2 * shard + lax.axis_index("c")

    conv_w_slots = lax.dynamic_update_index_in_dim(jnp.zeros((N_SHARD,) + W["conv_w"].shape, F32), W["conv_w"], shard, 0)
    slots = {n: _cast_bf16(W[n], shard_arr, "cast_" + n) for n in BIG}
    w_in_g, conv_w_g = _gather_weights([slots["w_in"], conv_w_slots])
    conv_w_full = conv_w_g.transpose(1, 0, 2).reshape(CONV_WIDTH, 1, N_SHARD * CW)

    loss, grad_x, G = _local_step(
        x[0], loss_target[0], W["g_mix_pre"], w_in_g, W["pool_w"].reshape(-1, POOL_GROUP, POOL_GROUP), W["pool_scale"],
        slots["w_out"], W["g_mix_post"], W["g_ffn_pre"], slots["w_up"], conv_w_full, W["conv_b"],
        slots["w_down"], W["g_ffn_post"], (c_arr, device, shard_arr))

    late, ffn = ("w_in", "w_out"), ("w_up", "w_down")
    from_sibling = _swap_halves([G[n] for n in late], "mix")
    sums = {n: _pair_sum(G[n], r, c_arr, "pair_sum_" + n) for n, r in zip(late, from_sibling)}
    pick = lambda d, names: [d[n] for n in names]
    delta, new_m, new_v = {}, {}, {}
    updates = _adamw_big(pick(W, ffn), pick(G, ffn), pick(M, ffn), pick(V, ffn), "adamw_ffn")[0]
    landed = _cargo_alone([("scatter", sums[n][1]) for n in late], "scatter_mix_grads")
    halves = [_shard_sum(sums[n][0], r, shard_arr, c_arr, "shard_sum_" + n) for n, r in zip(late, landed)]
    full = dict(zip(late, _join_halves(halves)))
    full.update({n: G[n] for n in ffn})
    full.update(_unpack_small(_small_sum(G["small_early"], "early"), SMALL_EARLY, W, shard))
    late_total = _small_sum(G["small_late"], "late")
    full.update(_unpack_small(late_total, SMALL_LATE, W, shard))
    loss = late_total[-8, 0]

    for names, (ds, nms, nvs) in ((ffn, updates),
                                  (late, _adamw_big(pick(W, late), pick(full, late), pick(M, late), pick(V, late), "adamw_mix")[0]),
                                  (SMALL, _adamw_small(pick(W, SMALL), pick(full, SMALL), pick(M, SMALL), pick(V, SMALL)))):
        for n, d, nm, nv in zip(names, ds, nms, nvs):
            delta[n], new_m[n], new_v[n] = d, nm, nv

    shaped = lambda d: [d[n].reshape(args[n].shape) for n in ORDER]
    return (loss, grad_x[None], *shaped(full), *shaped(delta), *shaped(new_m), *shaped(new_v))
```

```python
import functools

import jax
import jax.numpy as jnp
from jax import lax
from jax.experimental import pallas as pl
from jax.experimental.pallas import tpu as pltpu

F32 = jnp.float32
BF16 = jnp.bfloat16

RMS_EPS = 1e-6
NEG_INF = -1e30
N_HEADS = 8
HEAD_DIM = 64
ATTN_WIDTH = N_HEADS * HEAD_DIM
ATTN_SCALE = HEAD_DIM ** -0.5
ATTN_BLOCK = 128
DILATIONS = (1, 4, 16)
RESIDUES_PER_STEP = 4
CONSECUTIVE_BLOCKS = 4
POOL_WINDOWS = (2, 4, 8, 16)
POOL_GROUP = 128
POOL_WIDTH = POOL_GROUP * len(POOL_WINDOWS)
POOL_HALO = 16
CONV_WIDTH = 3
CONV_HALO = 8
N_SHARD = 4
LANES = 128
STAT_LANES = 16
STAT_WIDTH = N_HEADS * STAT_LANES

ADAM_LR = 0.001
ADAM_B1 = 0.9
ADAM_B2 = 0.999
ADAM_EPS = 1e-08
ADAM_WD = 0.01
ADAM_STEP = 10

VMEM_LIMIT = 60 * 1024 * 1024
MESH = pl.DeviceIdType.MESH
NT = (((1,), (1,)), ((), ()))
TN = (((0,), (0,)), ((), ()))


def _params(sem, vmem=None):
    return pltpu.CompilerParams(dimension_semantics=sem, vmem_limit_bytes=vmem)


def _const_spec(shape):
    zeros = (0,) * len(shape)
    return pl.BlockSpec(shape, lambda *_: zeros, pipeline_mode=pl.Buffered(1))


def _dot(a, b):
    return jnp.dot(a, b, preferred_element_type=F32)


def _dot_nt(a, b):
    return lax.dot_general(a, b, NT, preferred_element_type=F32)


def _dot_tn(a, b):
    return lax.dot_general(a, b, TN, preferred_element_type=F32)


def _rms_stats(x):
    r = lax.rsqrt(jnp.mean(x * x, axis=-1, keepdims=True) + RMS_EPS)
    return x * r, r


def _rms_bwd(dy, n, r, g):
    dg = jnp.sum(dy * n, axis=0, keepdims=True)
    dn = dy * g
    dx = r * (dn - n * jnp.mean(dn * n, axis=-1, keepdims=True))
    return dx, dg


def _gelu_tanh(g):
    k = 0.7978845608028654
    kc = k * 0.044715
    g2 = g * g
    t = jnp.tanh(g * (k + kc * g2))
    h = 0.5 * t + 0.5
    dh = (0.5 - 0.5 * (t * t)) * (k + (3.0 * kc) * g2)
    return g * h, h + g * dh


def _residue_shape(S, d, dtype, width=ATTN_WIDTH):
    return jax.ShapeDtypeStruct((S // d, d * width), dtype)


def _residue_spec(TM, d, width=ATTN_WIDTH):
    return pl.BlockSpec((TM // d, d * width), lambda i: (i, 0))


def _token_scratch(TM, width=ATTN_WIDTH):
    return [pltpu.VMEM((TM, LANES), F32)] * (width // LANES)


def _head_sum_matrix():
    r = lax.broadcasted_iota(jnp.int32, (ATTN_WIDTH, STAT_WIDTH), 0)
    c = lax.broadcasted_iota(jnp.int32, (ATTN_WIDTH, STAT_WIDTH), 1)
    return (r // HEAD_DIM == c // STAT_LANES).astype(BF16)


def _head_spread_matrix():
    c = lax.broadcasted_iota(jnp.int32, (STAT_WIDTH, ATTN_WIDTH), 0)
    r = lax.broadcasted_iota(jnp.int32, (STAT_WIDTH, ATTN_WIDTH), 1)
    return (c == (r // HEAD_DIM) * STAT_LANES).astype(BF16)


def _bf16_pieces(x, n):
    pieces = []
    for _ in range(n):
        p = x.astype(BF16)
        pieces.append(p)
        x = x - p.astype(F32)
    return pieces


def _put_tokens(dst_s, val):
    for cb, chunk in enumerate(dst_s):
        chunk[...] = val[:, cb * LANES:(cb + 1) * LANES]


def _get_tokens(src_s):
    return jnp.concatenate([chunk[...] for chunk in src_s], axis=1)


def _to_residue(val, src_s, out_ref, d, dtype):
    if d == 1:
        out_ref[...] = val.astype(dtype)
        return
    rows = src_s[0].shape[0]
    for r in range(d):
        for cb, chunk in enumerate(src_s):
            col = (r * len(src_s) + cb) * LANES
            out_ref[:, col:col + LANES] = chunk[pl.ds(r, rows // d, stride=d), :].astype(dtype)


def _from_residue(in_ref, dst_s, d):
    if d == 1:
        return in_ref[...].astype(F32)
    rows = dst_s[0].shape[0]
    for r in range(d):
        for cb, chunk in enumerate(dst_s):
            col = (r * len(dst_s) + cb) * LANES
            chunk[pl.ds(r, rows // d, stride=d), :] = in_ref[:, col:col + LANES].astype(F32)
    return _get_tokens(dst_s)


def _mix_in_fwd(x, g_pre, w_in, cargo=()):
    S, D = x.shape
    TM = 512
    nc = len(cargo)
    kinds = [kind for kind, _ in cargo]
    n_chunks = ATTN_WIDTH // LANES

    def body(x_ref, g_ref, w_ref, *refs):
        cargo_in, refs = refs[:nc], refs[nc:]
        qkv_refs, p_ref, h_ref = refs[:9], refs[9], refs[10]
        t_s = refs[11 + nc:11 + nc + n_chunks]
        cargo_refs = (kinds, cargo_in, refs[11:11 + nc], refs[11 + nc + n_chunks:])
        if nc:
            _cargo_start(*cargo_refs, pl.program_id(0) == 0)
        n, _ = _rms_stats(x_ref[...])
        hb = (n * g_ref[...]).astype(BF16)
        h_ref[...] = hb
        for a in range(3):
            res = _dot(hb, w_ref[a])
            if a == 0:
                res = res * ATTN_SCALE
            _put_tokens(t_s, res)
            for i, d in enumerate(DILATIONS):
                _to_residue(res, t_s, qkv_refs[3 * i + a], d, BF16)
        p_ref[...] = _dot(hb, w_ref[3])
        if nc:
            _cargo_finish(*cargo_refs, pl.program_id(0) == S // TM - 1)

    row = lambda w: pl.BlockSpec((TM, w), lambda i: (i, 0))
    arrays, cargo_specs, shapes, aliases, sems = _cargo_call(cargo, 3, 11)
    out = pl.pallas_call(
        body, name="mix_in_fwd", grid=(S // TM,),
        in_specs=[row(D), _const_spec((1, D)), _const_spec(w_in.shape)] + cargo_specs,
        out_specs=[_residue_spec(TM, d) for d in DILATIONS for _ in range(3)] + [row(POOL_WIDTH), row(D)] + cargo_specs,
        out_shape=[_residue_shape(S, d, BF16) for d in DILATIONS for _ in range(3)]
        + [jax.ShapeDtypeStruct((S, POOL_WIDTH), F32), jax.ShapeDtypeStruct((S, D), BF16)] + shapes,
        input_output_aliases=aliases,
        scratch_shapes=_token_scratch(TM) + sems,
        compiler_params=_params(("arbitrary",), VMEM_LIMIT),
    )(x, g_pre, w_in, *arrays)
    return [out[0:3], out[3:6], out[6:9]], out[9], out[10], out[11:]


def _band_mask(n):
    qi = lax.broadcasted_iota(jnp.int32, (ATTN_BLOCK, 2 * ATTN_BLOCK), 0)
    ki = lax.broadcasted_iota(jnp.int32, (ATTN_BLOCK, 2 * ATTN_BLOCK), 1)
    dist = qi + ATTN_BLOCK - ki
    return (dist >= 0) & (dist <= ATTN_BLOCK) & ((ki >= ATTN_BLOCK) | (n > 0))


def _first_head_lanes():
    return lax.broadcasted_iota(jnp.int32, (1, LANES), 1) < HEAD_DIM


def _stack_heads(pair, first):
    zero = jnp.zeros_like(pair)
    return jnp.concatenate([jnp.where(first, pair, zero), jnp.where(first, zero, pair)], axis=0)


def _unstack_heads(stacked, first):
    return jnp.where(first, stacked[:ATTN_BLOCK], stacked[ATTN_BLOCK:])


CARGO_COPIES = {"ici": 3, "d2d": 3, "scatter": 3, "swap": 1, "everyone": 7, "join": 1}
CARGO_IN_PLACE = ("ici", "d2d", "everyone", "join")


def _cargo_copies(kinds, ins, outs, send_sems, recv_sems, want_recvs=True):
    x, y, c, chips = _position()
    s = 2 * x + y
    me = 2 * s + c
    sibling = (x, y, 1 - c)
    sends, recvs = [], []

    def add(k, src, dst, landing, to):
        sends.append(_remote(src, dst, send_sems.at[k], recv_sems.at[k], to))
        if want_recvs:
            recvs.append(_remote(landing, landing, send_sems.at[k], recv_sems.at[k], to))

    k0 = 0
    for a, kind in enumerate(kinds):
        if kind == "swap":
            rows = ins[a].shape[1] // 2
            add(k0, ins[a].at[:, pl.ds((1 - c) * rows, rows), :], outs[a], outs[a], sibling)
        elif kind == "join":
            rows = outs[a].shape[0] // 2
            mine = outs[a].at[pl.ds(c * rows, rows), :]
            add(k0, mine, mine, outs[a].at[pl.ds((1 - c) * rows, rows), :], sibling)
        elif kind == "everyone":
            for m in range(1, 8):
                peer = (x ^ (m >> 2), y ^ ((m >> 1) & 1), c ^ (m & 1))
                add(k0 + m - 1, outs[a].at[me], outs[a].at[me], outs[a].at[4 * peer[0] + 2 * peer[1] + peer[2]], peer)
        else:
            for j, (px, py) in enumerate(chips):
                sj = 2 * px + py
                if kind == "scatter":
                    add(k0 + j, ins[a].at[sj], outs[a].at[j], outs[a].at[j], (px, py, c))
                    continue
                buf = outs[a]
                rows = buf.shape[1] // 2
                half = lambda shard, h: buf.at[shard, pl.ds(h * rows, rows), :]
                if kind == "ici":
                    add(k0 + j, half(s, c), half(s, c), half(sj, c), (px, py, c))
                else:
                    add(k0 + j, half(sj, c), half(sj, c), half(sj, 1 - c), sibling)
        k0 += CARGO_COPIES[kind]
    return sends, recvs


def _cargo_start(kinds, ins, outs, sems, first_step):
    @pl.when(first_step)
    def _():
        for cp in _cargo_copies(kinds, ins, outs, *sems, want_recvs=False)[0]:
            cp.start()


def _cargo_finish(kinds, ins, outs, sems, last_step):
    @pl.when(last_step)
    def _():
        sends, recvs = _cargo_copies(kinds, ins, outs, *sems)
        for cp in sends:
            cp.wait_send()
        for cp in recvs:
            cp.wait_recv()


def _cargo_call(cargo, n_in, n_out):
    arrays = [a for _, a in cargo]
    shapes = []
    for kind, a in cargo:
        if kind == "scatter":
            shape = (3,) + a.shape[1:]
        elif kind == "swap":
            shape = (a.shape[0], a.shape[1] // 2, a.shape[2])
        else:
            shape = a.shape
        shapes.append(jax.ShapeDtypeStruct(shape, a.dtype))
    aliases = {n_in + i: n_out + i for i, (kind, _) in enumerate(cargo) if kind in CARGO_IN_PLACE}
    n_sems = sum(CARGO_COPIES[kind] for kind, _ in cargo)
    sems = [pltpu.SemaphoreType.DMA((n_sems,))] * 2 if cargo else []
    return arrays, [ANY] * len(cargo), shapes, aliases, sems


def _attn_fwd(q, k, v, d, cargo=()):
    L = q.shape[0]
    group = min(d, RESIDUES_PER_STEP)
    width = group * ATTN_WIDTH
    qb = RESIDUES_PER_STEP // group
    steps = L // (qb * ATTN_BLOCK)
    nc = len(cargo)
    kinds = [kind for kind, _ in cargo]

    def body(*refs):
        q_ref, kp_ref, kc_ref, vp_ref, vc_ref = refs[:5]
        o_ref, lse_ref = refs[5 + nc:7 + nc]
        cargo_refs = (kinds, refs[5:5 + nc], refs[7 + nc:7 + 2 * nc], refs[7 + 2 * nc:])
        r, n = pl.program_id(0), pl.program_id(1)
        if nc:
            _cargo_start(*cargo_refs, (r == 0) & (n == 0))
        first = _first_head_lanes()
        for sub in range(qb):
            rows = slice(sub * ATTN_BLOCK, (sub + 1) * ATTN_BLOCK)
            valid = _band_mask(n if sub == 0 else 1)
            valid2 = jnp.concatenate([valid, valid], axis=0)
            for hp in range(width // LANES):
                sl = slice(hp * LANES, (hp + 1) * LANES)
                if sub == 0:
                    kk = jnp.concatenate([kp_ref[:, sl], kc_ref[rows, sl]], axis=0)
                    vv = jnp.concatenate([vp_ref[:, sl], vc_ref[rows, sl]], axis=0)
                else:
                    keys = slice((sub - 1) * ATTN_BLOCK, (sub + 1) * ATTN_BLOCK)
                    kk, vv = kc_ref[keys, sl], vc_ref[keys, sl]
                s = jnp.where(valid2, _dot_nt(_stack_heads(q_ref[rows, sl], first), kk), NEG_INF)
                m = jnp.max(s, axis=-1, keepdims=True)
                p = jnp.exp(s - m)
                den = jnp.sum(p, axis=-1, keepdims=True)
                o_ref[rows, sl] = _unstack_heads(_dot(p.astype(BF16), vv) / den, first).astype(BF16)
                lse = m + jnp.log(den)
                lane = 2 * hp * STAT_LANES
                for half in range(2):
                    lse_ref[rows, lane + half * STAT_LANES:lane + (half + 1) * STAT_LANES] = jnp.broadcast_to(
                        lse[half * ATTN_BLOCK:(half + 1) * ATTN_BLOCK], (ATTN_BLOCK, STAT_LANES))
        if nc:
            _cargo_finish(*cargo_refs, (r == d // group - 1) & (n == steps - 1))

    cur = pl.BlockSpec((qb * ATTN_BLOCK, width), lambda r, n: (n, r))
    prev = pl.BlockSpec((ATTN_BLOCK, width), lambda r, n: (jnp.maximum(n * qb - 1, 0), r))
    arrays, specs, shapes, aliases, sems = _cargo_call(cargo, 5, 2)
    out = pl.pallas_call(
        body, name=f"attn_fwd_d{d}", grid=(d // group, steps),
        in_specs=[cur, prev, cur, prev, cur] + specs,
        out_specs=[cur, pl.BlockSpec((qb * ATTN_BLOCK, group * STAT_WIDTH), lambda r, n: (n, r))] + specs,
        out_shape=[jax.ShapeDtypeStruct((L, d * ATTN_WIDTH), BF16), jax.ShapeDtypeStruct((L, d * STAT_WIDTH), F32)] + shapes,
        input_output_aliases=aliases, scratch_shapes=sems,
        compiler_params=_params(("arbitrary", "arbitrary")),
    )(q, k, k, v, v, *arrays)
    return out[0], out[1], out[2:]


def _attn_mix(outs, lses):
    S = outs[0].shape[0]
    TM = 512
    n = len(DILATIONS)

    def body(*refs):
        o_refs, l_refs, attn_ref, lse_refs = refs[:n], refs[n:2 * n], refs[2 * n], refs[2 * n + 1:3 * n + 1]
        t_s, c_s = refs[3 * n + 1:-1], refs[-1:]
        os = [_from_residue(o_refs[i], t_s, d) for i, d in enumerate(DILATIONS)]
        ls = [_from_residue(l_refs[i], c_s, d) for i, d in enumerate(DILATIONS)]
        m = jnp.maximum(jnp.maximum(ls[0], ls[1]), ls[2])
        es = [jnp.exp(l - m) for l in ls]
        den = es[0] + es[1] + es[2]
        spread = _head_spread_matrix()
        ws = [sum(_dot(p, spread) for p in _bf16_pieces(e / den, 2)) for e in es]
        attn_ref[...] = (ws[0] * os[0] + ws[1] * os[1] + ws[2] * os[2]).astype(BF16)
        lse = m + jnp.log(den)
        _put_tokens(c_s, lse)
        for i, d in enumerate(DILATIONS):
            _to_residue(lse, c_s, lse_refs[i], d, F32)

    specs = [_residue_spec(TM, d) for d in DILATIONS]
    stats = [_residue_spec(TM, d, STAT_WIDTH) for d in DILATIONS]
    out = pl.pallas_call(
        body, name="attn_mix", grid=(S // TM,),
        in_specs=specs + stats, out_specs=[specs[0]] + stats,
        out_shape=[jax.ShapeDtypeStruct((S, ATTN_WIDTH), BF16)]
        + [_residue_shape(S, d, F32, STAT_WIDTH) for d in DILATIONS],
        scratch_shapes=_token_scratch(TM) + _token_scratch(TM, STAT_WIDTH),
        compiler_params=_params(("parallel",)),
    )(*outs, *lses)
    return out[0], out[1:]


def _pool_counts(first_row, rows, w):
    t = first_row + lax.broadcasted_iota(jnp.int32, (rows, 1), 0)
    return jnp.minimum(t + 1, w).astype(F32)


def _trailing_sums(xe, w):
    s, k = xe, 1
    while k < w:
        s = s + pltpu.roll(s, k, 0)
        k *= 2
    return s


def _leading_sums(xe, w):
    rows = xe.shape[0]
    s, k = xe, 1
    while k < w:
        s = s + pltpu.roll(s, rows - k, 0)
        k *= 2
    return s


def _pooled_groups(halo, cur, first_row):
    TM = cur.shape[0]
    xe = jnp.concatenate([halo, cur], axis=0)
    out = []
    for g, w in enumerate(POOL_WINDOWS):
        a = xe[:, g * POOL_GROUP:(g + 1) * POOL_GROUP]
        s = _trailing_sums(a, w)[POOL_HALO:]
        out.append(s / _pool_counts(first_row, TM, w) - a[POOL_HALO:])
    return out


def _pool_fwd(pool_in, pool_w, pool_scale):
    S = pool_in.shape[0]
    TM = 512
    HB = TM // POOL_HALO

    def body(cur_ref, halo_ref, w_ref, sc_ref, y_ref):
        i = pl.program_id(0)
        halo = jnp.where(i > 0, halo_ref[...], 0.0)
        pooled = _pooled_groups(halo, cur_ref[...], i * TM)
        for g in range(len(POOL_WINDOWS)):
            sl = slice(g * POOL_GROUP, (g + 1) * POOL_GROUP)
            y = _dot(pooled[g].astype(BF16), w_ref[g].astype(BF16)) * sc_ref[:, sl]
            y_ref[:, sl] = y.astype(BF16)

    return pl.pallas_call(
        body, name="pool_fwd", grid=(S // TM,),
        in_specs=[pl.BlockSpec((TM, POOL_WIDTH), lambda i: (i, 0)),
                  pl.BlockSpec((POOL_HALO, POOL_WIDTH), lambda i: (jnp.maximum(i * HB - 1, 0), 0)),
                  _const_spec(pool_w.shape), _const_spec((1, POOL_WIDTH))],
        out_specs=pl.BlockSpec((TM, POOL_WIDTH), lambda i: (i, 0)),
        out_shape=jax.ShapeDtypeStruct((S, POOL_WIDTH), BF16),
        compiler_params=_params(("parallel",)),
    )(pool_in, pool_in, pool_w, pool_scale)


def _mix_out_fwd(attn, pool, w_out, x, g_post, g_ffn_pre):
    S, D = x.shape
    TM = 512

    def body(a_ref, p_ref, w_ref, x_ref, gp_ref, gf_ref, mixed_ref, x1_ref, h2_ref, cat_ref):
        ab = a_ref[...]
        cat_ref[:, :ATTN_WIDTH] = ab
        cat_ref[:, ATTN_WIDTH:] = p_ref[...]
        mixed = _dot(ab, w_ref[:ATTN_WIDTH, :]) + _dot(p_ref[...], w_ref[ATTN_WIDTH:, :])
        mixed_ref[...] = mixed.astype(BF16)
        n, _ = _rms_stats(mixed)
        x1 = x_ref[...] + n * gp_ref[...]
        x1_ref[...] = x1
        n2, _ = _rms_stats(x1)
        h2_ref[...] = (n2 * gf_ref[...]).astype(BF16)

    row = lambda w: pl.BlockSpec((TM, w), lambda i: (i, 0))
    return pl.pallas_call(
        body, name="mix_out_fwd", grid=(S // TM,),
        in_specs=[row(ATTN_WIDTH), row(POOL_WIDTH), _const_spec(w_out.shape), row(D),
                  _const_spec((1, D)), _const_spec((1, D))],
        out_specs=[row(D), row(D), row(D), row(D)],
        out_shape=[jax.ShapeDtypeStruct((S, D), BF16), jax.ShapeDtypeStruct((S, D), F32),
                   jax.ShapeDtypeStruct((S, D), BF16), jax.ShapeDtypeStruct((S, D), BF16)],
        compiler_params=_params(("parallel",), VMEM_LIMIT),
    )(attn, pool, w_out, x, g_post, g_ffn_pre)


def _ffn_fwd(h2, x1, target, w_up, w_down, conv_w, conv_b, g_post):
    S, D = x1.shape
    CW = w_up.shape[2]
    FF = 2 * CW
    TM = 256
    piece = 4 * LANES
    pieces = [(lo, min(lo + piece, CW)) for lo in range(0, CW, piece)]

    def body(h2_ref, x1_ref, t_ref, wu_ref, wd_ref, cw_ref, cb_ref, g_ref,
             yv_ref, dy_ref, df_ref, dc_ref, loss_ref, dg_ref, dcb_ref, dcw_ref,
             ue_s, dgate_s, dval_s):
        i = pl.program_id(0)

        @pl.when(i == 0)
        def _():
            loss_ref[...] = jnp.zeros_like(loss_ref)
            dg_ref[...] = jnp.zeros_like(dg_ref)
            dcb_ref[...] = jnp.zeros_like(dcb_ref)
            dcw_ref[...] = jnp.zeros_like(dcw_ref)
            ue_s[0:CONV_HALO, :] = jnp.zeros((CONV_HALO, 2 * FF), F32)

        @pl.when(i > 0)
        def _():
            ue_s[0:CONV_HALO, :] = ue_s[TM:TM + CONV_HALO, :]

        def shifted(cols, k):
            return pltpu.roll(ue_s[:, cols], k, 0)[CONV_HALO:]

        def conv(cols):
            return (cb_ref[:, cols] + cw_ref[2, :, cols] * ue_s[CONV_HALO:, cols]
                    + cw_ref[1, :, cols] * shifted(cols, 1) + cw_ref[0, :, cols] * shifted(cols, 2))

        hb = h2_ref[...]
        f = jnp.zeros((TM, D), F32)
        for j in range(2):
            jc = slice(j * CW, (j + 1) * CW)
            for half in range(2):
                blk = 2 * half + j
                cols = slice(blk * CW, (blk + 1) * CW)
                ue_s[CONV_HALO:, cols] = _dot(hb, wu_ref[blk])
            for lo, hi in pieces:
                pc = slice(j * CW + lo, j * CW + hi)
                gelu, dgelu = _gelu_tanh(conv(pc).astype(BF16))
                val = conv(slice(FF + j * CW + lo, FF + j * CW + hi)).astype(BF16)
                dgate_s[:, pc] = val * dgelu
                dval_s[:, pc] = gelu
                yv_ref[:, pc] = gelu * val
            f = f + _dot(yv_ref[:, jc], wd_ref[jc, :])

        n, r = _rms_stats(f)
        err = x1_ref[...] + n * g_ref[...] - t_ref[...]
        loss_ref[...] += 0.5 * jnp.sum(jnp.mean(err * err, axis=-1, keepdims=True), axis=0, keepdims=True)
        dy = err / D
        dy_ref[...] = dy
        df, dg = _rms_bwd(dy, n, r, g_ref[...])
        dg_ref[...] += dg
        dfb = df.astype(BF16)
        df_ref[...] = dfb

        for j in range(2):
            jc = slice(j * CW, (j + 1) * CW)
            dyv = _dot_nt(dfb, wd_ref[jc, :])
            for lo, hi in pieces:
                pc = slice(j * CW + lo, j * CW + hi)
                for half, scale_s in ((0, dgate_s), (1, dval_s)):
                    cols = slice(half * FF + j * CW + lo, half * FF + j * CW + hi)
                    dcv = dyv[:, lo:hi] * scale_s[:, pc].astype(F32)
                    dc_ref[:, cols] = dcv.astype(BF16)
                    dcb_ref[:, cols] += jnp.sum(dcv, axis=0, keepdims=True)
                    dcw_ref[2, :, cols] += jnp.sum(dcv * ue_s[CONV_HALO:, cols], axis=0, keepdims=True)
                    dcw_ref[1, :, cols] += jnp.sum(dcv * shifted(cols, 1), axis=0, keepdims=True)
                    dcw_ref[0, :, cols] += jnp.sum(dcv * shifted(cols, 2), axis=0, keepdims=True)

    row = lambda w: pl.BlockSpec((TM, w), lambda i: (i, 0))
    acc = lambda shape: pl.BlockSpec(shape, lambda i: (0,) * len(shape))
    return pl.pallas_call(
        body, name="ffn_fwd", grid=(S // TM,),
        in_specs=[row(D), row(D), row(D), _const_spec(w_up.shape), _const_spec(w_down.shape),
                  _const_spec(conv_w.shape), _const_spec((1, 2 * FF)), _const_spec((1, D))],
        out_specs=[row(FF), row(D), row(D), row(2 * FF),
                   acc((1, 1)), acc((1, D)), acc((1, 2 * FF)), acc((CONV_WIDTH, 1, 2 * FF))],
        out_shape=[jax.ShapeDtypeStruct((S, FF), BF16),
                   jax.ShapeDtypeStruct((S, D), F32), jax.ShapeDtypeStruct((S, D), BF16),
                   jax.ShapeDtypeStruct((S, 2 * FF), BF16),
                   jax.ShapeDtypeStruct((1, 1), F32), jax.ShapeDtypeStruct((1, D), F32),
                   jax.ShapeDtypeStruct((1, 2 * FF), F32), jax.ShapeDtypeStruct((CONV_WIDTH, 1, 2 * FF), F32)],
        scratch_shapes=[pltpu.VMEM((TM + CONV_HALO, 2 * FF), F32), pltpu.VMEM((TM, FF), BF16),
                        pltpu.VMEM((TM, FF), BF16)],
        compiler_params=_params(("arbitrary",), VMEM_LIMIT),
    )(h2, x1, target, w_up, w_down, conv_w, conv_b, g_post)


def _ffn_bwd(dc, conv_w, w_up, x1, g_ffn_pre, dy):
    S, D = x1.shape
    CW = w_up.shape[2]
    F2 = 4 * CW
    TM = 256
    HB = TM // CONV_HALO
    last = S // CONV_HALO - 1
    n_tiles = S // TM

    def body(dc_ref, halo_ref, cw_ref, wu_ref, x1_ref, g_ref, dy_ref, du_ref, dx1_ref, dg_ref):
        i = pl.program_id(0)

        @pl.when(i == 0)
        def _():
            dg_ref[...] = jnp.zeros_like(dg_ref)

        keep = i < n_tiles - 1
        dh2 = jnp.zeros((TM, D), F32)
        for blk in range(N_SHARD):
            cols = slice(blk * CW, (blk + 1) * CW)
            halo = jnp.where(keep, halo_ref[:, cols].astype(F32), 0.0)
            dce = jnp.concatenate([dc_ref[:, cols].astype(F32), halo], axis=0)
            rows = TM + CONV_HALO
            du = (cw_ref[2, :, cols] * dce[:TM]
                  + cw_ref[1, :, cols] * pltpu.roll(dce, rows - 1, 0)[:TM]
                  + cw_ref[0, :, cols] * pltpu.roll(dce, rows - 2, 0)[:TM])
            dub = du.astype(BF16)
            du_ref[:, cols] = dub
            dh2 = dh2 + _dot_nt(dub, wu_ref[blk])
        n2, r2 = _rms_stats(x1_ref[...])
        dx, dg = _rms_bwd(dh2, n2, r2, g_ref[...])
        dg_ref[...] += dg
        dx1_ref[...] = (dy_ref[...] + dx).astype(BF16)

    row = lambda w: pl.BlockSpec((TM, w), lambda i: (i, 0))
    return pl.pallas_call(
        body, name="ffn_bwd", grid=(S // TM,),
        in_specs=[row(F2), pl.BlockSpec((CONV_HALO, F2), lambda i: (jnp.minimum((i + 1) * HB, last), 0)),
                  _const_spec(conv_w.shape), _const_spec(w_up.shape), row(D), _const_spec((1, D)), row(D)],
        out_specs=[row(F2), row(D), pl.BlockSpec((1, D), lambda i: (0, 0))],
        out_shape=[jax.ShapeDtypeStruct((S, F2), BF16), jax.ShapeDtypeStruct((S, D), BF16),
                   jax.ShapeDtypeStruct((1, D), F32)],
        compiler_params=_params(("arbitrary",), VMEM_LIMIT),
    )(dc, dc, conv_w, w_up, x1, g_ffn_pre, dy)


def _matmul_tn(a, b, n_blocks, name):
    S, M = a.shape
    N = b.shape[1]
    tn = N // n_blocks
    tm = M if M <= 1024 else M // 2
    tk = 2048
    nk = S // tk

    def body(a_ref, b_ref, o_ref):
        @pl.when(pl.program_id(2) == 0)
        def _():
            o_ref[...] = jnp.zeros_like(o_ref)
        o_ref[0] += _dot_tn(a_ref[...], b_ref[...])

    return pl.pallas_call(
        body, name=name, grid=(M // tm, n_blocks, nk),
        in_specs=[pl.BlockSpec((tk, tm), lambda i, j, k: (k, i)), pl.BlockSpec((tk, tn), lambda i, j, k: (k, j))],
        out_specs=pl.BlockSpec((1, tm, tn), lambda i, j, k: (j, i, 0)),
        out_shape=jax.ShapeDtypeStruct((n_blocks, M, tn), F32),
        compiler_params=_params(("parallel", "parallel", "arbitrary"), VMEM_LIMIT),
    )(a, b)


def _mix_out_bwd(dx1, mixed, g_post, w_out, attn, cargo=()):
    S, D = dx1.shape
    TM = 512
    nd = len(DILATIONS)
    nc = len(cargo)
    kinds = [kind for kind, _ in cargo]
    n_chunks = ATTN_WIDTH // LANES

    def body(*refs):
        dx_ref, m_ref, g_ref, w_ref, a_ref = refs[:5]
        dm_ref, dp_ref, dg_ref = refs[5 + nc:8 + nc]
        da_refs, dl_refs = refs[8 + nc:8 + nc + nd], refs[8 + nc + nd:8 + nc + 2 * nd]
        n_out = 8 + nc + 2 * nd
        t_s = refs[n_out + nc:n_out + nc + n_chunks]
        c_s = refs[n_out + nc + n_chunks:n_out + nc + n_chunks + 1]
        cargo_refs = (kinds, refs[5:5 + nc], refs[n_out:n_out + nc], refs[n_out + nc + n_chunks + 1:])
        if nc:
            _cargo_start(*cargo_refs, pl.program_id(0) == 0)

        @pl.when(pl.program_id(0) == 0)
        def _():
            dg_ref[...] = jnp.zeros_like(dg_ref)

        n, r = _rms_stats(m_ref[...].astype(F32))
        dm, dg = _rms_bwd(dx_ref[...].astype(F32), n, r, g_ref[...])
        dg_ref[...] += dg
        dmb = dm.astype(BF16)
        dm_ref[...] = dmb
        da = _dot_nt(dmb, w_ref[:ATTN_WIDTH, :])
        _put_tokens(t_s, da)
        for i, d in enumerate(DILATIONS):
            _to_residue(da, t_s, da_refs[i], d, BF16)
        dp_ref[...] = _dot_nt(dmb, w_ref[ATTN_WIDTH:, :]).astype(BF16)
        gather = _head_sum_matrix()
        delta = sum(_dot(p, gather) for p in _bf16_pieces(da * a_ref[...].astype(F32), 2))
        _put_tokens(c_s, delta)
        for i, d in enumerate(DILATIONS):
            _to_residue(delta, c_s, dl_refs[i], d, F32)
        if nc:
            _cargo_finish(*cargo_refs, pl.program_id(0) == S // TM - 1)

    row = lambda w: pl.BlockSpec((TM, w), lambda i: (i, 0))
    specs = [_residue_spec(TM, d) for d in DILATIONS]
    arrays, cargo_specs, shapes, aliases, sems = _cargo_call(cargo, 5, 3 + 2 * nd)
    out = pl.pallas_call(
        body, name="mix_out_bwd", grid=(S // TM,),
        in_specs=[row(D), row(D), _const_spec((1, D)), _const_spec(w_out.shape), row(ATTN_WIDTH)] + cargo_specs,
        out_specs=[row(D), row(POOL_WIDTH), pl.BlockSpec((1, D), lambda i: (0, 0))] + specs
        + [_residue_spec(TM, d, STAT_WIDTH) for d in DILATIONS] + cargo_specs,
        out_shape=[jax.ShapeDtypeStruct((S, D), BF16), jax.ShapeDtypeStruct((S, POOL_WIDTH), BF16),
                   jax.ShapeDtypeStruct((1, D), F32)]
        + [_residue_shape(S, d, BF16) for d in DILATIONS]
        + [_residue_shape(S, d, F32, STAT_WIDTH) for d in DILATIONS] + shapes,
        input_output_aliases=aliases,
        scratch_shapes=_token_scratch(TM) + _token_scratch(TM, STAT_WIDTH) + sems,
        compiler_params=_params(("arbitrary",), VMEM_LIMIT),
    )(dx1, mixed, g_post, w_out, attn, *arrays)
    return out[0], out[1], out[2], out[3:3 + nd], out[3 + nd:3 + 2 * nd], out[3 + 2 * nd:]


def _pool_bwd(pool_in, d_pool, pool_w, pool_scale):
    S = pool_in.shape[0]
    TM = 512
    HB = TM // POOL_HALO
    last = S // POOL_HALO - 1
    G = len(POOL_WINDOWS)

    def body(cur_ref, halo_ref, dcur_ref, dnext_ref, w_ref, sc_ref, dxin_ref, dw_ref, dsc_ref):
        i = pl.program_id(0)

        @pl.when(i == 0)
        def _():
            dw_ref[...] = jnp.zeros_like(dw_ref)
            dsc_ref[...] = jnp.zeros_like(dsc_ref)

        halo = jnp.where(i > 0, halo_ref[...], 0.0)
        pooled = _pooled_groups(halo, cur_ref[...], i * TM)
        dnext = jnp.where(i < S // TM - 1, dnext_ref[...].astype(F32), 0.0)
        dye = jnp.concatenate([dcur_ref[...].astype(F32), dnext], axis=0)
        for g, w in enumerate(POOL_WINDOWS):
            sl = slice(g * POOL_GROUP, (g + 1) * POOL_GROUP)
            wg = w_ref[g].astype(BF16)
            pb = pooled[g].astype(BF16)
            dsc_ref[:, sl] += jnp.sum(dye[:TM, sl] * _dot(pb, wg), axis=0, keepdims=True)
            dpre = (dye[:, sl] * sc_ref[:, sl]).astype(BF16)
            dw_ref[g] += _dot_tn(pb, dpre[:TM])
            dpooled = _dot_nt(dpre, wg)
            z = dpooled / _pool_counts(i * TM, TM + POOL_HALO, w)
            dxin_ref[:, sl] = (_leading_sums(z, w)[:TM] - dpooled[:TM]).astype(BF16)

    row = pl.BlockSpec((TM, POOL_WIDTH), lambda i: (i, 0))
    return pl.pallas_call(
        body, name="pool_bwd", grid=(S // TM,),
        in_specs=[row, pl.BlockSpec((POOL_HALO, POOL_WIDTH), lambda i: (jnp.maximum(i * HB - 1, 0), 0)),
                  row, pl.BlockSpec((POOL_HALO, POOL_WIDTH), lambda i: (jnp.minimum((i + 1) * HB, last), 0)),
                  _const_spec(pool_w.shape), _const_spec((1, POOL_WIDTH))],
        out_specs=[row, pl.BlockSpec((G, POOL_GROUP, POOL_GROUP), lambda i: (0, 0, 0)),
                   pl.BlockSpec((1, POOL_WIDTH), lambda i: (0, 0))],
        out_shape=[jax.ShapeDtypeStruct((S, POOL_WIDTH), BF16), jax.ShapeDtypeStruct((G, POOL_GROUP, POOL_GROUP), F32),
                   jax.ShapeDtypeStruct((1, POOL_WIDTH), F32)],
        compiler_params=_params(("arbitrary",)),
    )(pool_in, pool_in, d_pool, d_pool, pool_w, pool_scale)


def _attn_bwd(q, k, v, d_attn, lse, delta, d, cargo=()):
    L = q.shape[0]
    nb = L // ATTN_BLOCK
    group = min(d, RESIDUES_PER_STEP)
    width = group * ATTN_WIDTH
    nc = len(cargo)
    kinds = [kind for kind, _ in cargo]

    def body(*refs):
        q_ref, kp_ref, kc_ref, vp_ref, vc_ref, do_ref, lse_ref, dl_ref = refs[:8]
        dq_ref, dk_ref, dv_ref = refs[8 + nc:11 + nc]
        ck_s, cv_s = refs[11 + 2 * nc:13 + 2 * nc]
        cargo_refs = (kinds, refs[8:8 + nc], refs[11 + nc:11 + 2 * nc], refs[13 + 2 * nc:])
        r, n = pl.program_id(0), pl.program_id(1)
        if nc:
            _cargo_start(*cargo_refs, (r == 0) & (n == 0))

        @pl.when(n == 0)
        def _():
            ck_s[...] = jnp.zeros_like(ck_s)
            cv_s[...] = jnp.zeros_like(cv_s)

        @pl.when(n < nb)
        def _():
            valid = _band_mask(n)
            valid2 = jnp.concatenate([valid, valid], axis=0)
            first = _first_head_lanes()

            def stacked_column(ref, hp):
                lane = 2 * hp * STAT_LANES
                return jnp.concatenate([ref[:, lane:lane + 1], ref[:, lane + STAT_LANES:lane + STAT_LANES + 1]], axis=0)

            for hp in range(width // LANES):
                sl = slice(hp * LANES, (hp + 1) * LANES)
                qq = _stack_heads(q_ref[:, sl], first)
                dd = _stack_heads(do_ref[:, sl], first)
                kk = jnp.concatenate([kp_ref[:, sl], kc_ref[:, sl]], axis=0)
                vv = jnp.concatenate([vp_ref[:, sl], vc_ref[:, sl]], axis=0)
                s = _dot_nt(qq, kk)
                p = jnp.where(valid2, jnp.exp(s - stacked_column(lse_ref, hp)), 0.0)
                dp = _dot_nt(dd, vv)
                ds = (p * (dp - stacked_column(dl_ref, hp))).astype(BF16)
                dq_ref[:, sl] = (_unstack_heads(_dot(ds, kk), first) * ATTN_SCALE).astype(BF16)
                dk = _dot_tn(ds, qq)
                dv = _dot_tn(p.astype(BF16), dd)
                dk_ref[:, sl] = (ck_s[:, sl] + dk[:ATTN_BLOCK]).astype(BF16)
                dv_ref[:, sl] = (cv_s[:, sl] + dv[:ATTN_BLOCK]).astype(BF16)
                ck_s[:, sl] = dk[ATTN_BLOCK:]
                cv_s[:, sl] = dv[ATTN_BLOCK:]

        @pl.when(n == nb)
        def _():
            dk_ref[...] = ck_s[...].astype(BF16)
            dv_ref[...] = cv_s[...].astype(BF16)

        if nc:
            _cargo_finish(*cargo_refs, (r == d // group - 1) & (n == nb))

    blk = (ATTN_BLOCK, width)
    cur = pl.BlockSpec(blk, lambda r, n: (jnp.minimum(n, nb - 1), r))
    stat = pl.BlockSpec((ATTN_BLOCK, group * STAT_WIDTH), lambda r, n: (jnp.minimum(n, nb - 1), r))
    prev = pl.BlockSpec(blk, lambda r, n: (jnp.maximum(jnp.minimum(n, nb - 1) - 1, 0), r))
    done = pl.BlockSpec(blk, lambda r, n: (jnp.maximum(n - 1, 0), r))
    arrays, specs, shapes, aliases, sems = _cargo_call(cargo, 8, 3)
    out = pl.pallas_call(
        body, name=f"attn_bwd_d{d}", grid=(d // group, nb + 1),
        in_specs=[cur, prev, cur, prev, cur, cur, stat, stat] + specs, out_specs=[cur, done, done] + specs,
        out_shape=[jax.ShapeDtypeStruct((L, d * ATTN_WIDTH), BF16)] * 3 + shapes,
        input_output_aliases=aliases,
        scratch_shapes=[pltpu.VMEM(blk, F32), pltpu.VMEM(blk, F32)] + sems,
        compiler_params=_params(("arbitrary", "arbitrary")),
    )(q, k, k, v, v, d_attn, lse, delta, *arrays)
    return out[:3], out[3:]


def _attn_bwd_consecutive(q, k, v, d_attn, lse, delta, d, cargo=()):
    L = q.shape[0]
    qb = CONSECUTIVE_BLOCKS
    steps = L // (qb * ATTN_BLOCK)
    nc = len(cargo)
    kinds = [kind for kind, _ in cargo]

    def body(*refs):
        q_ref, kp_ref, kc_ref, vp_ref, vc_ref, do_ref, lse_ref, dl_ref = refs[:8]
        dq_ref, dk_ref, dv_ref, ek_ref, ev_ref = refs[8 + nc:13 + nc]
        cargo_refs = (kinds, refs[8:8 + nc], refs[13 + nc:13 + 2 * nc], refs[13 + 2 * nc:])
        r, n = pl.program_id(0), pl.program_id(1)
        if nc:
            _cargo_start(*cargo_refs, (r == 0) & (n == 0))
        first = _first_head_lanes()
        for hp in range(ATTN_WIDTH // LANES):
            sl = slice(hp * LANES, (hp + 1) * LANES)
            for sub in range(qb):
                rows = slice(sub * ATTN_BLOCK, (sub + 1) * ATTN_BLOCK)
                valid = _band_mask(n if sub == 0 else 1)
                valid2 = jnp.concatenate([valid, valid], axis=0)
                if sub == 0:
                    kk = jnp.concatenate([kp_ref[:, sl], kc_ref[rows, sl]], axis=0)
                    vv = jnp.concatenate([vp_ref[:, sl], vc_ref[rows, sl]], axis=0)
                else:
                    keys = slice((sub - 1) * ATTN_BLOCK, (sub + 1) * ATTN_BLOCK)
                    kk, vv = kc_ref[keys, sl], vc_ref[keys, sl]
                qq = _stack_heads(q_ref[rows, sl], first)
                dd = _stack_heads(do_ref[rows, sl], first)
                lane = 2 * hp * STAT_LANES
                column = lambda ref: jnp.concatenate(
                    [ref[rows, lane:lane + 1], ref[rows, lane + STAT_LANES:lane + STAT_LANES + 1]], axis=0)
                p = jnp.where(valid2, jnp.exp(_dot_nt(qq, kk) - column(lse_ref)), 0.0)
                ds = (p * (_dot_nt(dd, vv) - column(dl_ref))).astype(BF16)
                dq_ref[rows, sl] = (_unstack_heads(_dot(ds, kk), first) * ATTN_SCALE).astype(BF16)
                dk = _dot_tn(ds, qq)
                dv = _dot_tn(p.astype(BF16), dd)
                if sub == 0:
                    ek_ref[:, sl] = dk[:ATTN_BLOCK].astype(BF16)
                    ev_ref[:, sl] = dv[:ATTN_BLOCK].astype(BF16)
                else:
                    before = slice((sub - 1) * ATTN_BLOCK, sub * ATTN_BLOCK)
                    dk_ref[before, sl] = (carry_k + dk[:ATTN_BLOCK]).astype(BF16)
                    dv_ref[before, sl] = (carry_v + dv[:ATTN_BLOCK]).astype(BF16)
                carry_k, carry_v = dk[ATTN_BLOCK:], dv[ATTN_BLOCK:]
            dk_ref[rows, sl] = carry_k.astype(BF16)
            dv_ref[rows, sl] = carry_v.astype(BF16)
        if nc:
            _cargo_finish(*cargo_refs, (r == d - 1) & (n == steps - 1))

    cur = pl.BlockSpec((qb * ATTN_BLOCK, ATTN_WIDTH), lambda r, n: (n, r))
    prev = pl.BlockSpec((ATTN_BLOCK, ATTN_WIDTH), lambda r, n: (jnp.maximum(n * qb - 1, 0), r))
    edge = pl.BlockSpec((ATTN_BLOCK, ATTN_WIDTH), lambda r, n: (n, r))
    stat = pl.BlockSpec((qb * ATTN_BLOCK, STAT_WIDTH), lambda r, n: (n, r))
    arrays, specs, shapes, aliases, sems = _cargo_call(cargo, 8, 5)
    out = pl.pallas_call(
        body, name=f"attn_bwd_d{d}", grid=(d, steps),
        in_specs=[cur, prev, cur, prev, cur, cur, stat, stat] + specs, out_specs=[cur, cur, cur, edge, edge] + specs,
        out_shape=[jax.ShapeDtypeStruct((L, d * ATTN_WIDTH), BF16)] * 3
        + [jax.ShapeDtypeStruct((steps * ATTN_BLOCK, d * ATTN_WIDTH), BF16)] * 2 + shapes,
        input_output_aliases=aliases, scratch_shapes=sems,
        compiler_params=_params(("arbitrary", "arbitrary")),
    )(q, k, k, v, v, d_attn, lse, delta, *arrays)
    return out[:3], out[3:5], out[5:]


def _mix_in_bwd(dqkv, edges, d_pool_in, w_in, x, g_pre, dx1):
    S, D = x.shape
    TM = CONSECUTIVE_BLOCKS * ATTN_BLOCK
    nd = len(DILATIONS)
    n_tiles = S // TM

    def body(*refs):
        g_refs = refs[:3 * nd]
        e_refs = (None,) + refs[3 * nd:3 * nd + 2]
        dpi_ref, w_ref, x_ref, g_ref, dx1_ref, dproj_ref, gx_ref, dg_ref = refs[3 * nd + 2:3 * nd + 10]
        t_s = refs[3 * nd + 10:]

        @pl.when(pl.program_id(0) == 0)
        def _():
            dg_ref[...] = jnp.zeros_like(dg_ref)

        dh = jnp.zeros((TM, D), F32)
        for a in range(4):
            if a < 3:
                tot = g_refs[a][...].astype(F32)
                if a > 0:
                    late = jnp.where(pl.program_id(0) < n_tiles - 1, e_refs[a][...].astype(F32), 0.0)
                    tot = jnp.concatenate([tot[:TM - ATTN_BLOCK], tot[TM - ATTN_BLOCK:] + late], axis=0)
                for i, d in enumerate(DILATIONS[1:]):
                    tot = tot + _from_residue(g_refs[3 * (i + 1) + a], t_s, d)
                db = tot.astype(BF16)
            else:
                db = dpi_ref[...]
            dproj_ref[:, a * ATTN_WIDTH:(a + 1) * ATTN_WIDTH] = db
            dh = dh + _dot_nt(db, w_ref[a])
        n, r = _rms_stats(x_ref[...])
        dx, dg = _rms_bwd(dh, n, r, g_ref[...])
        dg_ref[...] += dg
        gx_ref[...] = dx1_ref[...].astype(F32) + dx

    row = lambda w: pl.BlockSpec((TM, w), lambda i: (i, 0))
    edge = pl.BlockSpec((ATTN_BLOCK, ATTN_WIDTH), lambda i: (jnp.minimum(i + 1, n_tiles - 1), 0))
    return pl.pallas_call(
        body, name="mix_in_bwd", grid=(S // TM,),
        in_specs=[_residue_spec(TM, d) for d in DILATIONS for _ in range(3)] + [edge, edge]
        + [row(POOL_WIDTH), _const_spec(w_in.shape), row(D), _const_spec((1, D)), row(D)],
        out_specs=[row(4 * ATTN_WIDTH), row(D), pl.BlockSpec((1, D), lambda i: (0, 0))],
        out_shape=[jax.ShapeDtypeStruct((S, 4 * ATTN_WIDTH), BF16), jax.ShapeDtypeStruct((S, D), F32),
                   jax.ShapeDtypeStruct((1, D), F32)],
        scratch_shapes=_token_scratch(TM),
        compiler_params=_params(("arbitrary",), VMEM_LIMIT),
    )(*[g for gs in dqkv for g in gs], *edges, d_pool_in, w_in, x, g_pre, dx1)


SMALL_EARLY = ("pool_w", "pool_scale", "g_mix_post", "g_ffn_pre", "conv_b", "g_ffn_post", "conv_w")
SMALL_LATE = ("g_mix_pre",)


def _pack_small(grads, names):
    parts = []
    for n in names:
        g = grads[n]
        if n == "conv_w":
            g = g.reshape(CONV_WIDTH, N_SHARD, -1).transpose(1, 0, 2)
        parts.append(g.reshape(-1, LANES))
    return jnp.concatenate(parts, axis=0) if len(parts) > 1 else parts[0]


def _unpack_small(packed, names, like, shard):
    out, row = {}, 0
    for n in names:
        size = like[n].size * (N_SHARD if n == "conv_w" else 1)
        g = packed[row:row + size // LANES]
        row += size // LANES
        if n == "conv_w":
            g = lax.dynamic_slice_in_dim(g.reshape((N_SHARD,) + like[n].shape), shard, 1, axis=0)[0]
        out[n] = g.reshape(like[n].shape)
    return out


def _local_step(x, target, g_mix_pre, w_in, pool_w, pool_scale, w_out, g_mix_post, g_ffn_pre,
                w_up, conv_w, conv_b, w_down, g_ffn_post, mesh_pos=None):
    on_mesh = mesh_pos is not None
    D = x.shape[1]
    CW = w_up.shape[2]
    qkv, pool_in, h1, got = _mix_in_fwd(x, g_mix_pre, w_in, [("ici", w_up)] if on_mesh else ())
    w_up = got[0] if on_mesh else w_up
    o1, l1, got = _attn_fwd(*qkv[0], 1, [("d2d", w_up), ("ici", w_out)] if on_mesh else ())
    w_up, w_out = got if on_mesh else (w_up, w_out)
    o4, l4, got = _attn_fwd(*qkv[1], 4, [("d2d", w_out), ("ici", w_down)] if on_mesh else ())
    w_out, w_down = got if on_mesh else (w_out, w_down)
    o16, l16, got = _attn_fwd(*qkv[2], 16, [("d2d", w_down)] if on_mesh else ())
    w_down = got[0] if on_mesh else w_down
    w_out = w_out.reshape(D, D)
    w_down = w_down.reshape(2 * CW, D)
    attn, lse = _attn_mix((o1, o4, o16), (l1, l4, l16))
    pool = _pool_fwd(pool_in, pool_w, pool_scale)
    mixed, x1, h2, cat = _mix_out_fwd(attn, pool, w_out, x, g_mix_post, g_ffn_pre)

    yv, dy, df, dc, loss, d_g_ffn_post, d_conv_b, d_conv_w = _ffn_fwd(
        h2, x1, target, w_up, w_down, conv_w, conv_b, g_ffn_post)
    du, dx1, d_g_ffn_pre = _ffn_bwd(dc, conv_w, w_up, x1, g_ffn_pre, dy)
    d_w_up = _matmul_tn(h2, du, N_SHARD, "grad_w_up")
    d_w_down = _matmul_tn(yv, df, 1, "grad_w_down")[0].reshape(N_SHARD, CW // 2, D)
    swap = [("swap", d_w_up), ("swap", d_w_down)] if on_mesh else ()
    d_mixed, d_pool, d_g_mix_post, d_attn, delta, from_sibling = _mix_out_bwd(dx1, mixed, g_mix_post, w_out, attn, swap)
    d_w_out = _matmul_tn(cat, d_mixed, 1, "grad_w_out")[0].reshape(N_SHARD, D // N_SHARD, D)
    d_pool_in, d_pool_w, d_pool_scale = _pool_bwd(pool_in, d_pool, pool_w, pool_scale)
    grads = dict(pool_w=d_pool_w, pool_scale=d_pool_scale, w_out=d_w_out, g_mix_post=d_g_mix_post,
                 g_ffn_pre=d_g_ffn_pre, w_up=d_w_up, conv_w=d_conv_w, conv_b=d_conv_b, w_down=d_w_down,
                 g_ffn_post=d_g_ffn_post)
    cargo = [(), (), ()]
    if on_mesh:
        c_arr, device, shard_arr = mesh_pos
        up_f32, up_bf16 = _pair_sum(d_w_up, from_sibling[0], c_arr, "pair_sum_w_up")
        down_f32, down_bf16 = _pair_sum(d_w_down, from_sibling[1], c_arr, "pair_sum_w_down")
        early = _pack_small(grads, SMALL_EARLY)
        early_slots = lax.dynamic_update_index_in_dim(jnp.zeros((8,) + early.shape, F32), early, device, 0)
        cargo = [[("scatter", down_bf16), ("everyone", early_slots), ("swap", d_w_out)], [("scatter", up_bf16)], []]

    dqkv1, edges, landed1 = _attn_bwd_consecutive(*qkv[0], d_attn[0], lse[0], delta[0], 1, cargo[0])
    dqkv4, landed4 = _attn_bwd(*qkv[1], d_attn[1], lse[1], delta[1], 4, cargo[1])
    if on_mesh:
        halves = [_shard_sum(up_f32, landed4[0], shard_arr, c_arr, "shard_sum_w_up"),
                  _shard_sum(down_f32, landed1[0], shard_arr, c_arr, "shard_sum_w_down")]
        out_f32, out_bf16 = _pair_sum(d_w_out, landed1[2], c_arr, "pair_sum_w_out")
        cargo[2] = cargo[2] + [("join", h) for h in halves] + [("scatter", out_bf16)]
    if qkv[2][0].shape[0] == CONSECUTIVE_BLOCKS * ATTN_BLOCK:
        dqkv16, _, landed16 = _attn_bwd_consecutive(*qkv[2], d_attn[2], lse[2], delta[2], 16, cargo[2])
    else:
        dqkv16, landed16 = _attn_bwd(*qkv[2], d_attn[2], lse[2], delta[2], 16, cargo[2])
    if on_mesh:
        grads.update(small_early=landed1[1], w_up=landed16[0], w_down=landed16[1],
                     w_out=_shard_sum(out_f32, landed16[2], shard_arr, c_arr, "shard_sum_w_out"))
    d_proj, grad_x, grads["g_mix_pre"] = _mix_in_bwd((dqkv1, dqkv4, dqkv16), edges, d_pool_in, w_in, x, g_mix_pre, dx1)
    grads["w_in"] = _matmul_tn(h1, d_proj, N_SHARD, "grad_w_in")
    return loss, grad_x, grads


ANY = pl.BlockSpec(memory_space=pl.ANY)


def _position():
    x, y, c = lax.axis_index("x"), lax.axis_index("y"), lax.axis_index("c")
    chips = [(1 - x, y), (x, 1 - y), (1 - x, 1 - y)]
    return x, y, c, chips


def _remote(src, dst, send_sem, recv_sem, to):
    return pltpu.make_async_remote_copy(src_ref=src, dst_ref=dst, send_sem=send_sem, recv_sem=recv_sem,
                                        device_id=to, device_id_type=MESH)


def _cast_bf16(w, shard_arr, name):
    R, C = w.shape
    tr = R // 2

    def body(s_ref, w_ref, o_ref):
        o_ref[0] = w_ref[...].astype(BF16)

    return pl.pallas_call(
        body, name=name,
        grid_spec=pltpu.PrefetchScalarGridSpec(
            num_scalar_prefetch=1, grid=(2,),
            in_specs=[pl.BlockSpec((tr, C), lambda i, s_ref: (i, 0))],
            out_specs=pl.BlockSpec((1, tr, C), lambda i, s_ref: (s_ref[0], i, 0))),
        out_shape=jax.ShapeDtypeStruct((N_SHARD, R, C), BF16),
        compiler_params=_params(("parallel",)))(shard_arr, w)


def _gather_weights(bufs):
    n = len(bufs) - 1

    def body(*refs):
        outs, cw_out = refs[n + 1:2 * n + 1], refs[2 * n + 1]
        ici_send, ici_recv, d2d_send, d2d_recv = refs[2 * n + 2:]
        x, y, c, chips = _position()
        s = 2 * x + y
        sibling = (x, y, 1 - c)

        def half(a, shard, h):
            rows = outs[a].shape[1] // 2
            return outs[a].at[shard, pl.ds(h * rows, rows), :]

        sends = []
        for a in range(n):
            for j, (px, py) in enumerate(chips):
                sends.append(_remote(half(a, s, c), half(a, s, c),
                                     ici_send.at[3 * a + j], ici_recv.at[3 * a + j], (px, py, c)))
        for j, (px, py) in enumerate(chips):
            sends.append(_remote(cw_out.at[s], cw_out.at[s], ici_send.at[3 * n + j], ici_recv.at[3 * n + j], (px, py, c)))
        for cp in sends:
            cp.start()
        passed = []
        for a in range(n):
            for j, (px, py) in enumerate(chips):
                sj = 2 * px + py
                got = half(a, sj, c)
                _remote(got, got, ici_send.at[3 * a + j], ici_recv.at[3 * a + j], (px, py, c)).wait_recv()
                fwd = _remote(got, got, d2d_send.at[3 * a + j], d2d_recv.at[3 * a + j], sibling)
                fwd.start()
                passed.append(fwd)
        for j, (px, py) in enumerate(chips):
            got = cw_out.at[2 * px + py]
            _remote(got, got, ici_send.at[3 * n + j], ici_recv.at[3 * n + j], (px, py, c)).wait_recv()
        for a in range(n):
            for j, (px, py) in enumerate(chips):
                got = half(a, 2 * px + py, 1 - c)
                _remote(got, got, d2d_send.at[3 * a + j], d2d_recv.at[3 * a + j], sibling).wait_recv()
        for cp in sends + passed:
            cp.wait_send()

    return pl.pallas_call(
        body, name="gather_weights",
        in_specs=[ANY] * (n + 1), out_specs=[ANY] * (n + 1),
        out_shape=[jax.ShapeDtypeStruct(b.shape, b.dtype) for b in bufs],
        input_output_aliases={i: i for i in range(n + 1)},
        scratch_shapes=[pltpu.SemaphoreType.DMA((3 * n + 3,)), pltpu.SemaphoreType.DMA((3 * n + 3,)),
                        pltpu.SemaphoreType.DMA((3 * n,)), pltpu.SemaphoreType.DMA((3 * n,))],
        compiler_params=pltpu.CompilerParams(has_side_effects=True),
    )(*bufs)


def _swap_halves(grads, tag):
    n = len(grads)

    def body(*refs):
        ins, outs, send_sem, recv_sem = refs[:n], refs[n:2 * n], refs[2 * n], refs[2 * n + 1]
        x, y, c, _ = _position()
        copies = []
        for a in range(n):
            rows = ins[a].shape[1] // 2
            copies.append(_remote(ins[a].at[:, pl.ds((1 - c) * rows, rows), :], outs[a],
                                  send_sem.at[a], recv_sem.at[a], (x, y, 1 - c)))
        for cp in copies:
            cp.start()
        for cp in copies:
            cp.wait()

    return pl.pallas_call(
        body, name="swap_grad_halves_" + tag,
        in_specs=[ANY] * n, out_specs=[ANY] * n,
        out_shape=[jax.ShapeDtypeStruct((g.shape[0], g.shape[1] // 2, g.shape[2]), F32) for g in grads],
        scratch_shapes=[pltpu.SemaphoreType.DMA((n,)), pltpu.SemaphoreType.DMA((n,))],
        compiler_params=pltpu.CompilerParams(has_side_effects=True),
    )(*grads)


def _pair_sum(g, got, c_arr, name):
    n_sh, R, C = g.shape
    rows = R // 2

    def body(c_ref, g_ref, r_ref, f_ref, b_ref):
        t = g_ref[...] + r_ref[...]
        f_ref[...] = t
        b_ref[...] = t.astype(BF16)

    blk = pl.BlockSpec((1, rows, C), lambda i, c_ref: (i, 0, 0))
    return pl.pallas_call(
        body, name=name,
        grid_spec=pltpu.PrefetchScalarGridSpec(
            num_scalar_prefetch=1, grid=(n_sh,),
            in_specs=[pl.BlockSpec((1, rows, C), lambda i, c_ref: (i, c_ref[0], 0)), blk],
            out_specs=[blk, blk]),
        out_shape=[jax.ShapeDtypeStruct((n_sh, rows, C), F32), jax.ShapeDtypeStruct((n_sh, rows, C), BF16)],
        compiler_params=_params(("parallel",)),
    )(c_arr, g, got)


def _shard_sum(sums_f32, recv, shard_arr, c_arr, name):
    _, rows, C = sums_f32.shape

    def body(s_ref, c_ref, o_ref, r_ref, t_ref):
        t_ref[...] = ((o_ref[0] + r_ref[0].astype(F32)) + r_ref[1].astype(F32)) + r_ref[2].astype(F32)

    return pl.pallas_call(
        body, name=name,
        grid_spec=pltpu.PrefetchScalarGridSpec(
            num_scalar_prefetch=2, grid=(1,),
            in_specs=[pl.BlockSpec((1, rows, C), lambda i, s_ref, c_ref: (s_ref[0], 0, 0)),
                      pl.BlockSpec((3, rows, C), lambda i, s_ref, c_ref: (0, 0, 0))],
            out_specs=pl.BlockSpec((rows, C), lambda i, s_ref, c_ref: (c_ref[0], 0))),
        out_shape=jax.ShapeDtypeStruct((2 * rows, C), F32),
        compiler_params=_params(("arbitrary",)),
    )(shard_arr, c_arr, sums_f32, recv)


def _join_halves(bufs):
    n = len(bufs)

    def body(*refs):
        outs, send_sem, recv_sem = refs[n:2 * n], refs[2 * n], refs[2 * n + 1]
        x, y, c, _ = _position()
        copies = []
        for a in range(n):
            rows = outs[a].shape[0] // 2
            mine = outs[a].at[pl.ds(c * rows, rows), :]
            copies.append(_remote(mine, mine, send_sem.at[a], recv_sem.at[a], (x, y, 1 - c)))
        for cp in copies:
            cp.start()
        for a, cp in enumerate(copies):
            cp.wait_send()
            rows = outs[a].shape[0] // 2
            theirs = outs[a].at[pl.ds((1 - c) * rows, rows), :]
            _remote(theirs, theirs, send_sem.at[a], recv_sem.at[a], (x, y, 1 - c)).wait_recv()

    return pl.pallas_call(
        body, name="join_grad_halves",
        in_specs=[ANY] * n, out_specs=[ANY] * n,
        out_shape=[jax.ShapeDtypeStruct(b.shape, F32) for b in bufs],
        input_output_aliases={i: i for i in range(n)},
        scratch_shapes=[pltpu.SemaphoreType.DMA((n,)), pltpu.SemaphoreType.DMA((n,))],
        compiler_params=pltpu.CompilerParams(has_side_effects=True),
    )(*bufs)


def _small_sum(parts, tag):
    _, R, C = parts.shape

    def body(p_ref, o_ref):
        t = p_ref[0]
        for k in range(1, 8):
            t = t + p_ref[k]
        o_ref[...] = t

    return pl.pallas_call(
        body, name="small_grad_sum_" + tag, grid=(1,),
        in_specs=[pl.BlockSpec((8, R, C), lambda i: (0, 0, 0))], out_specs=pl.BlockSpec((R, C), lambda i: (0, 0)),
        out_shape=jax.ShapeDtypeStruct((R, C), F32), compiler_params=_params(("arbitrary",)),
    )(parts)


def _adamw_math(w, g, m, v):
    m = ADAM_B1 * m + (1.0 - ADAM_B1) * g
    v = ADAM_B2 * v + (1.0 - ADAM_B2) * (g * g)
    m_hat = m / (1.0 - ADAM_B1 ** ADAM_STEP)
    v_hat = v / (1.0 - ADAM_B2 ** ADAM_STEP)
    delta = -ADAM_LR * (m_hat / (jnp.sqrt(v_hat) + ADAM_EPS) + ADAM_WD * w)
    return delta, m, v


def _adamw_big(ws, gs, ms, vs, name, cargo=()):
    n = len(ws)
    nc = len(cargo)
    kinds = [kind for kind, _ in cargo]

    def body(*refs):
        ins, outs = refs[:4 * n], refs[4 * n + nc:7 * n + nc]
        cargo_refs = (kinds, refs[4 * n:4 * n + nc], refs[7 * n + nc:7 * n + 2 * nc], refs[7 * n + 2 * nc:])
        if nc:
            _cargo_start(*cargo_refs, pl.program_id(0) == 0)
        for a in range(n):
            w, g, m, v = (ins[k * n + a][...] for k in range(4))
            outs[a][...], outs[n + a][...], outs[2 * n + a][...] = _adamw_math(w, g, m, v)
        if nc:
            _cargo_finish(*cargo_refs, pl.program_id(0) == 3)

    blks = [pl.BlockSpec((w.shape[0] // 4, w.shape[1]), lambda i: (i, 0)) for w in ws]
    arrays, cargo_specs, shapes, aliases, sems = _cargo_call(cargo, 4 * n, 3 * n)
    out = pl.pallas_call(
        body, name=name, grid=(4,), in_specs=blks * 4 + cargo_specs, out_specs=blks * 3 + cargo_specs,
        out_shape=[jax.ShapeDtypeStruct(w.shape, F32) for w in ws] * 3 + shapes,
        input_output_aliases=aliases, scratch_shapes=sems,
        compiler_params=_params(("arbitrary",)),
    )(*ws, *gs, *ms, *vs, *arrays)
    return (out[:n], out[n:2 * n], out[2 * n:3 * n]), out[3 * n:]


def _adamw_small(ws, gs, ms, vs):
    n = len(ws)

    def body(*refs):
        for a in range(n):
            w, g, m, v = (refs[k * n + a][...] for k in range(4))
            d, nm, nv = _adamw_math(w, g, m, v)
            refs[4 * n + a][...] = d
            refs[5 * n + a][...] = nm
            refs[6 * n + a][...] = nv

    shapes = [jax.ShapeDtypeStruct(w.shape, F32) for w in ws]
    out = pl.pallas_call(body, name="adamw_small", out_shape=shapes * 3)(*ws, *gs, *ms, *vs)
    return out[:n], out[n:2 * n], out[2 * n:]


BIG = ("w_in", "w_out", "w_up", "w_down")
SMALL = ("g_mix_pre", "pool_w", "pool_scale", "g_mix_post", "g_ffn_pre", "conv_b", "g_ffn_post", "conv_w")
ORDER = ("g_mix_pre", "w_in", "pool_w", "pool_scale", "w_out", "g_mix_post", "g_ffn_pre", "w_up", "conv_w", "conv_b",
         "w_down", "g_ffn_post")


def kernel(x, g_mix_pre, w_in, pool_w, pool_scale, w_out, g_mix_post, g_ffn_pre, w_up, conv_w, conv_b, w_down, g_ffn_post, loss_target, m_g_mix_pre, m_w_in, m_pool_w, m_pool_scale, m_w_out, m_g_mix_post, m_g_ffn_pre, m_w_up, m_conv_w, m_conv_b, m_w_down, m_g_ffn_post, v_g_mix_pre, v_w_in, v_pool_w, v_pool_scale, v_w_out, v_g_mix_post, v_g_ffn_pre, v_w_up, v_conv_w, v_conv_b, v_w_down, v_g_ffn_post):
    args = dict(locals())
    W = {n: args[n][0] for n in ORDER}
    M = {n: args["m_" + n][0] for n in ORDER}
    V = {n: args["v_" + n][0] for n in ORDER}
    for d in (W, M, V):
        d["pool_w"] = d["pool_w"].reshape(-1, POOL_GROUP)
        for n in ("g_mix_pre", "pool_scale", "g_mix_post", "g_ffn_pre", "conv_b", "g_ffn_post"):
            d[n] = d[n].reshape(1, -1)
    CW = W["w_up"].shape[1]
    c_arr = lax.axis_index("c").astype(jnp.int32).reshape(1)
    shard = 2 * lax.axis_index("x") + lax.axis_index("y")
    shard_arr = shard.astype(jnp.int32).reshape(1)
    device = 2 * shard + lax.axis_index("c")

    conv_w_slots = lax.dynamic_update_index_in_dim(jnp.zeros((N_SHARD,) + W["conv_w"].shape, F32), W["conv_w"], shard, 0)
    slots = {n: _cast_bf16(W[n], shard_arr, "cast_" + n) for n in BIG}
    w_in_g, conv_w_g = _gather_weights([slots["w_in"], conv_w_slots])
    conv_w_full = conv_w_g.transpose(1, 0, 2).reshape(CONV_WIDTH, 1, N_SHARD * CW)

    loss, grad_x, G = _local_step(
        x[0], loss_target[0], W["g_mix_pre"], w_in_g, W["pool_w"].reshape(-1, POOL_GROUP, POOL_GROUP), W["pool_scale"],
        slots["w_out"], W["g_mix_post"], W["g_ffn_pre"], slots["w_up"], conv_w_full, W["conv_b"],
        slots["w_down"], W["g_ffn_post"], (c_arr, device, shard_arr))

    late, ffn = ("w_in", "w_out"), ("w_up", "w_down")
    in_f32, in_bf16 = _pair_sum(G["w_in"], _swap_halves([G["w_in"]], "mix")[0], c_arr, "pair_sum_w_in")
    loss_rows = jnp.pad(loss, ((0, 7), (0, LANES - 1)))
    small = jnp.concatenate([_pack_small(G, SMALL_LATE), loss_rows], axis=0)
    small_slots = lax.dynamic_update_index_in_dim(jnp.zeros((8,) + small.shape, F32), small, device, 0)
    pick = lambda d, names: [d[n] for n in names]
    delta, new_m, new_v = {}, {}, {}
    updates, landed = _adamw_big(pick(W, ffn), pick(G, ffn), pick(M, ffn), pick(V, ffn), "adamw_ffn",
                                 [("scatter", in_bf16), ("everyone", small_slots)])
    halves = [_shard_sum(in_f32, landed[0], shard_arr, c_arr, "shard_sum_w_in"), G["w_out"]]
    full = dict(zip(late, _join_halves(halves)))
    full.update({n: G[n] for n in ffn})
    full.update(_unpack_small(_small_sum(G["small_early"], "early"), SMALL_EARLY, W, shard))
    late_total = _small_sum(landed[1], "late")
    full.update(_unpack_small(late_total, SMALL_LATE, W, shard))
    loss = late_total[-8, 0]

    for names, (ds, nms, nvs) in ((ffn, updates),
                                  (late, _adamw_big(pick(W, late), pick(full, late), pick(M, late), pick(V, late), "adamw_mix")[0]),
                                  (SMALL, _adamw_small(pick(W, SMALL), pick(full, SMALL), pick(M, SMALL), pick(V, SMALL)))):
        for n, d, nm, nv in zip(names, ds, nms, nvs):
            delta[n], new_m[n], new_v[n] = d, nm, nv

    shaped = lambda d: [d[n].reshape(args[n].shape) for n in ORDER]
    return (loss, grad_x[None], *shaped(full), *shaped(delta), *shaped(new_m), *shaped(new_v))
```

```python
import functools

import jax
import jax.numpy as jnp
from jax import lax
from jax.experimental import pallas as pl
from jax.experimental.pallas import tpu as pltpu

F32 = jnp.float32
BF16 = jnp.bfloat16

RMS_EPS = 1e-6
NEG_INF = -1e30
N_HEADS = 8
HEAD_DIM = 64
ATTN_WIDTH = N_HEADS * HEAD_DIM
ATTN_SCALE = HEAD_DIM ** -0.5
ATTN_BLOCK = 128
DILATIONS = (1, 4, 16)
RESIDUES_PER_STEP = 4
CONSECUTIVE_BLOCKS = 4
POOL_WINDOWS = (2, 4, 8, 16)
POOL_GROUP = 128
POOL_WIDTH = POOL_GROUP * len(POOL_WINDOWS)
POOL_HALO = 16
CONV_WIDTH = 3
CONV_HALO = 8
N_SHARD = 4
LANES = 128
STAT_LANES = 16
STAT_WIDTH = N_HEADS * STAT_LANES

ADAM_LR = 0.001
ADAM_B1 = 0.9
ADAM_B2 = 0.999
ADAM_EPS = 1e-08
ADAM_WD = 0.01
ADAM_STEP = 10

VMEM_LIMIT = 60 * 1024 * 1024
MESH = pl.DeviceIdType.MESH
NT = (((1,), (1,)), ((), ()))
TN = (((0,), (0,)), ((), ()))


def _params(sem, vmem=None):
    return pltpu.CompilerParams(dimension_semantics=sem, vmem_limit_bytes=vmem)


def _const_spec(shape):
    zeros = (0,) * len(shape)
    return pl.BlockSpec(shape, lambda *_: zeros, pipeline_mode=pl.Buffered(1))


def _dot(a, b):
    return jnp.dot(a, b, preferred_element_type=F32)


def _dot_nt(a, b):
    return lax.dot_general(a, b, NT, preferred_element_type=F32)


def _dot_tn(a, b):
    return lax.dot_general(a, b, TN, preferred_element_type=F32)


def _rms_stats(x):
    r = lax.rsqrt(jnp.mean(x * x, axis=-1, keepdims=True) + RMS_EPS)
    return x * r, r


def _rms_bwd(dy, n, r, g):
    dg = jnp.sum(dy * n, axis=0, keepdims=True)
    dn = dy * g
    dx = r * (dn - n * jnp.mean(dn * n, axis=-1, keepdims=True))
    return dx, dg


def _gelu_tanh(g):
    k = 0.7978845608028654
    kc = k * 0.044715
    g2 = g * g
    t = jnp.tanh(g * (k + kc * g2))
    h = 0.5 * t + 0.5
    dh = (0.5 - 0.5 * (t * t)) * (k + (3.0 * kc) * g2)
    return g * h, h + g * dh


def _residue_shape(S, d, dtype, width=ATTN_WIDTH):
    return jax.ShapeDtypeStruct((S // d, d * width), dtype)


def _residue_spec(TM, d, width=ATTN_WIDTH):
    return pl.BlockSpec((TM // d, d * width), lambda i: (i, 0))


def _token_scratch(TM, width=ATTN_WIDTH):
    return [pltpu.VMEM((TM, LANES), F32)] * (width // LANES)


def _head_sum_matrix():
    r = lax.broadcasted_iota(jnp.int32, (ATTN_WIDTH, STAT_WIDTH), 0)
    c = lax.broadcasted_iota(jnp.int32, (ATTN_WIDTH, STAT_WIDTH), 1)
    return (r // HEAD_DIM == c // STAT_LANES).astype(BF16)


def _head_spread_matrix():
    c = lax.broadcasted_iota(jnp.int32, (STAT_WIDTH, ATTN_WIDTH), 0)
    r = lax.broadcasted_iota(jnp.int32, (STAT_WIDTH, ATTN_WIDTH), 1)
    return (c == (r // HEAD_DIM) * STAT_LANES).astype(BF16)


def _bf16_pieces(x, n):
    pieces = []
    for _ in range(n):
        p = x.astype(BF16)
        pieces.append(p)
        x = x - p.astype(F32)
    return pieces


def _put_tokens(dst_s, val):
    for cb, chunk in enumerate(dst_s):
        chunk[...] = val[:, cb * LANES:(cb + 1) * LANES]


def _get_tokens(src_s):
    return jnp.concatenate([chunk[...] for chunk in src_s], axis=1)


def _to_residue(val, src_s, out_ref, d, dtype):
    if d == 1:
        out_ref[...] = val.astype(dtype)
        return
    rows = src_s[0].shape[0]
    for r in range(d):
        for cb, chunk in enumerate(src_s):
            col = (r * len(src_s) + cb) * LANES
            out_ref[:, col:col + LANES] = chunk[pl.ds(r, rows // d, stride=d), :].astype(dtype)


def _from_residue(in_ref, dst_s, d):
    if d == 1:
        return in_ref[...].astype(F32)
    rows = dst_s[0].shape[0]
    for r in range(d):
        for cb, chunk in enumerate(dst_s):
            col = (r * len(dst_s) + cb) * LANES
            chunk[pl.ds(r, rows // d, stride=d), :] = in_ref[:, col:col + LANES].astype(F32)
    return _get_tokens(dst_s)


def _mix_in_fwd(x, g_pre, w_in, cargo=()):
    S, D = x.shape
    TM = 512
    nc = len(cargo)
    kinds = [kind for kind, _ in cargo]
    n_chunks = ATTN_WIDTH // LANES

    def body(x_ref, g_ref, w_ref, *refs):
        cargo_in, refs = refs[:nc], refs[nc:]
        qkv_refs, p_ref, h_ref = refs[:9], refs[9], refs[10]
        t_s = refs[11 + nc:11 + nc + n_chunks]
        cargo_refs = (kinds, cargo_in, refs[11:11 + nc], refs[11 + nc + n_chunks:])
        if nc:
            _cargo_start(*cargo_refs, pl.program_id(0) == 0)
        n, _ = _rms_stats(x_ref[...])
        hb = (n * g_ref[...]).astype(BF16)
        h_ref[...] = hb
        for a in range(3):
            res = _dot(hb, w_ref[a])
            if a == 0:
                res = res * ATTN_SCALE
            _put_tokens(t_s, res)
            for i, d in enumerate(DILATIONS):
                _to_residue(res, t_s, qkv_refs[3 * i + a], d, BF16)
        p_ref[...] = _dot(hb, w_ref[3])
        if nc:
            _cargo_finish(*cargo_refs, pl.program_id(0) == S // TM - 1)

    row = lambda w: pl.BlockSpec((TM, w), lambda i: (i, 0))
    arrays, cargo_specs, shapes, aliases, sems = _cargo_call(cargo, 3, 11)
    out = pl.pallas_call(
        body, name="mix_in_fwd", grid=(S // TM,),
        in_specs=[row(D), _const_spec((1, D)), _const_spec(w_in.shape)] + cargo_specs,
        out_specs=[_residue_spec(TM, d) for d in DILATIONS for _ in range(3)] + [row(POOL_WIDTH), row(D)] + cargo_specs,
        out_shape=[_residue_shape(S, d, BF16) for d in DILATIONS for _ in range(3)]
        + [jax.ShapeDtypeStruct((S, POOL_WIDTH), F32), jax.ShapeDtypeStruct((S, D), BF16)] + shapes,
        input_output_aliases=aliases,
        scratch_shapes=_token_scratch(TM) + sems,
        compiler_params=_params(("arbitrary",), VMEM_LIMIT),
    )(x, g_pre, w_in, *arrays)
    return [out[0:3], out[3:6], out[6:9]], out[9], out[10], out[11:]


def _band_mask(n):
    qi = lax.broadcasted_iota(jnp.int32, (ATTN_BLOCK, 2 * ATTN_BLOCK), 0)
    ki = lax.broadcasted_iota(jnp.int32, (ATTN_BLOCK, 2 * ATTN_BLOCK), 1)
    dist = qi + ATTN_BLOCK - ki
    return (dist >= 0) & (dist <= ATTN_BLOCK) & ((ki >= ATTN_BLOCK) | (n > 0))


def _first_head_lanes():
    return lax.broadcasted_iota(jnp.int32, (1, LANES), 1) < HEAD_DIM


def _stack_heads(pair, first):
    zero = jnp.zeros_like(pair)
    return jnp.concatenate([jnp.where(first, pair, zero), jnp.where(first, zero, pair)], axis=0)


def _unstack_heads(stacked, first):
    return jnp.where(first, stacked[:ATTN_BLOCK], stacked[ATTN_BLOCK:])


CARGO_COPIES = {"ici": 3, "d2d": 3, "scatter": 3, "swap": 1, "everyone": 7, "join": 1}
CARGO_IN_PLACE = ("ici", "d2d", "everyone", "join")


def _cargo_copies(kinds, ins, outs, send_sems, recv_sems, want_recvs=True):
    x, y, c, chips = _position()
    s = 2 * x + y
    me = 2 * s + c
    sibling = (x, y, 1 - c)
    sends, recvs = [], []

    def add(k, src, dst, landing, to):
        sends.append(_remote(src, dst, send_sems.at[k], recv_sems.at[k], to))
        if want_recvs:
            recvs.append(_remote(landing, landing, send_sems.at[k], recv_sems.at[k], to))

    k0 = 0
    for a, kind in enumerate(kinds):
        if kind == "swap":
            rows = ins[a].shape[1] // 2
            add(k0, ins[a].at[:, pl.ds((1 - c) * rows, rows), :], outs[a], outs[a], sibling)
        elif kind == "join":
            rows = outs[a].shape[0] // 2
            mine = outs[a].at[pl.ds(c * rows, rows), :]
            add(k0, mine, mine, outs[a].at[pl.ds((1 - c) * rows, rows), :], sibling)
        elif kind == "everyone":
            for m in range(1, 8):
                peer = (x ^ (m >> 2), y ^ ((m >> 1) & 1), c ^ (m & 1))
                add(k0 + m - 1, outs[a].at[me], outs[a].at[me], outs[a].at[4 * peer[0] + 2 * peer[1] + peer[2]], peer)
        else:
            for j, (px, py) in enumerate(chips):
                sj = 2 * px + py
                if kind == "scatter":
                    add(k0 + j, ins[a].at[sj], outs[a].at[j], outs[a].at[j], (px, py, c))
                    continue
                buf = outs[a]
                rows = buf.shape[1] // 2
                half = lambda shard, h: buf.at[shard, pl.ds(h * rows, rows), :]
                if kind == "ici":
                    add(k0 + j, half(s, c), half(s, c), half(sj, c), (px, py, c))
                else:
                    add(k0 + j, half(sj, c), half(sj, c), half(sj, 1 - c), sibling)
        k0 += CARGO_COPIES[kind]
    return sends, recvs


def _cargo_start(kinds, ins, outs, sems, first_step):
    @pl.when(first_step)
    def _():
        for cp in _cargo_copies(kinds, ins, outs, *sems, want_recvs=False)[0]:
            cp.start()


def _cargo_finish(kinds, ins, outs, sems, last_step):
    @pl.when(last_step)
    def _():
        sends, recvs = _cargo_copies(kinds, ins, outs, *sems)
        for cp in sends:
            cp.wait_send()
        for cp in recvs:
            cp.wait_recv()


def _cargo_call(cargo, n_in, n_out):
    arrays = [a for _, a in cargo]
    shapes = []
    for kind, a in cargo:
        if kind == "scatter":
            shape = (3,) + a.shape[1:]
        elif kind == "swap":
            shape = (a.shape[0], a.shape[1] // 2, a.shape[2])
        else:
            shape = a.shape
        shapes.append(jax.ShapeDtypeStruct(shape, a.dtype))
    aliases = {n_in + i: n_out + i for i, (kind, _) in enumerate(cargo) if kind in CARGO_IN_PLACE}
    n_sems = sum(CARGO_COPIES[kind] for kind, _ in cargo)
    sems = [pltpu.SemaphoreType.DMA((n_sems,))] * 2 if cargo else []
    return arrays, [ANY] * len(cargo), shapes, aliases, sems


def _attn_fwd(q, k, v, d, cargo=()):
    L = q.shape[0]
    group = min(d, RESIDUES_PER_STEP)
    width = group * ATTN_WIDTH
    qb = RESIDUES_PER_STEP // group
    steps = L // (qb * ATTN_BLOCK)
    nc = len(cargo)
    kinds = [kind for kind, _ in cargo]

    def body(*refs):
        q_ref, kp_ref, kc_ref, vp_ref, vc_ref = refs[:5]
        o_ref, lse_ref = refs[5 + nc:7 + nc]
        cargo_refs = (kinds, refs[5:5 + nc], refs[7 + nc:7 + 2 * nc], refs[7 + 2 * nc:])
        r, n = pl.program_id(0), pl.program_id(1)
        if nc:
            _cargo_start(*cargo_refs, (r == 0) & (n == 0))
        first = _first_head_lanes()
        for sub in range(qb):
            rows = slice(sub * ATTN_BLOCK, (sub + 1) * ATTN_BLOCK)
            valid = _band_mask(n if sub == 0 else 1)
            valid2 = jnp.concatenate([valid, valid], axis=0)
            for hp in range(width // LANES):
                sl = slice(hp * LANES, (hp + 1) * LANES)
                if sub == 0:
                    kk = jnp.concatenate([kp_ref[:, sl], kc_ref[rows, sl]], axis=0)
                    vv = jnp.concatenate([vp_ref[:, sl], vc_ref[rows, sl]], axis=0)
                else:
                    keys = slice((sub - 1) * ATTN_BLOCK, (sub + 1) * ATTN_BLOCK)
                    kk, vv = kc_ref[keys, sl], vc_ref[keys, sl]
                s = jnp.where(valid2, _dot_nt(_stack_heads(q_ref[rows, sl], first), kk), NEG_INF)
                m = jnp.max(s, axis=-1, keepdims=True)
                p = jnp.exp(s - m)
                den = jnp.sum(p, axis=-1, keepdims=True)
                o_ref[rows, sl] = _unstack_heads(_dot(p.astype(BF16), vv) / den, first).astype(BF16)
                lse = m + jnp.log(den)
                lane = 2 * hp * STAT_LANES
                for half in range(2):
                    lse_ref[rows, lane + half * STAT_LANES:lane + (half + 1) * STAT_LANES] = jnp.broadcast_to(
                        lse[half * ATTN_BLOCK:(half + 1) * ATTN_BLOCK], (ATTN_BLOCK, STAT_LANES))
        if nc:
            _cargo_finish(*cargo_refs, (r == d // group - 1) & (n == steps - 1))

    cur = pl.BlockSpec((qb * ATTN_BLOCK, width), lambda r, n: (n, r))
    prev = pl.BlockSpec((ATTN_BLOCK, width), lambda r, n: (jnp.maximum(n * qb - 1, 0), r))
    arrays, specs, shapes, aliases, sems = _cargo_call(cargo, 5, 2)
    out = pl.pallas_call(
        body, name=f"attn_fwd_d{d}", grid=(d // group, steps),
        in_specs=[cur, prev, cur, prev, cur] + specs,
        out_specs=[cur, pl.BlockSpec((qb * ATTN_BLOCK, group * STAT_WIDTH), lambda r, n: (n, r))] + specs,
        out_shape=[jax.ShapeDtypeStruct((L, d * ATTN_WIDTH), BF16), jax.ShapeDtypeStruct((L, d * STAT_WIDTH), F32)] + shapes,
        input_output_aliases=aliases, scratch_shapes=sems,
        compiler_params=_params(("arbitrary", "arbitrary")),
    )(q, k, k, v, v, *arrays)
    return out[0], out[1], out[2:]


def _attn_mix(outs, lses):
    S = outs[0].shape[0]
    TM = 512
    n = len(DILATIONS)

    def body(*refs):
        o_refs, l_refs, attn_ref, lse_refs = refs[:n], refs[n:2 * n], refs[2 * n], refs[2 * n + 1:3 * n + 1]
        t_s, c_s = refs[3 * n + 1:-1], refs[-1:]
        os = [_from_residue(o_refs[i], t_s, d) for i, d in enumerate(DILATIONS)]
        ls = [_from_residue(l_refs[i], c_s, d) for i, d in enumerate(DILATIONS)]
        m = jnp.maximum(jnp.maximum(ls[0], ls[1]), ls[2])
        es = [jnp.exp(l - m) for l in ls]
        den = es[0] + es[1] + es[2]
        spread = _head_spread_matrix()
        ws = [sum(_dot(p, spread) for p in _bf16_pieces(e / den, 2)) for e in es]
        attn_ref[...] = (ws[0] * os[0] + ws[1] * os[1] + ws[2] * os[2]).astype(BF16)
        lse = m + jnp.log(den)
        _put_tokens(c_s, lse)
        for i, d in enumerate(DILATIONS):
            _to_residue(lse, c_s, lse_refs[i], d, F32)

    specs = [_residue_spec(TM, d) for d in DILATIONS]
    stats = [_residue_spec(TM, d, STAT_WIDTH) for d in DILATIONS]
    out = pl.pallas_call(
        body, name="attn_mix", grid=(S // TM,),
        in_specs=specs + stats, out_specs=[specs[0]] + stats,
        out_shape=[jax.ShapeDtypeStruct((S, ATTN_WIDTH), BF16)]
        + [_residue_shape(S, d, F32, STAT_WIDTH) for d in DILATIONS],
        scratch_shapes=_token_scratch(TM) + _token_scratch(TM, STAT_WIDTH),
        compiler_params=_params(("parallel",)),
    )(*outs, *lses)
    return out[0], out[1:]


def _pool_counts(first_row, rows, w):
    t = first_row + lax.broadcasted_iota(jnp.int32, (rows, 1), 0)
    return jnp.minimum(t + 1, w).astype(F32)


def _trailing_sums(xe, w):
    s, k = xe, 1
    while k < w:
        s = s + pltpu.roll(s, k, 0)
        k *= 2
    return s


def _leading_sums(xe, w):
    rows = xe.shape[0]
    s, k = xe, 1
    while k < w:
        s = s + pltpu.roll(s, rows - k, 0)
        k *= 2
    return s


def _pooled_groups(halo, cur, first_row):
    TM = cur.shape[0]
    xe = jnp.concatenate([halo, cur], axis=0)
    out = []
    for g, w in enumerate(POOL_WINDOWS):
        a = xe[:, g * POOL_GROUP:(g + 1) * POOL_GROUP]
        s = _trailing_sums(a, w)[POOL_HALO:]
        out.append(s / _pool_counts(first_row, TM, w) - a[POOL_HALO:])
    return out


def _pool_fwd(pool_in, pool_w, pool_scale):
    S = pool_in.shape[0]
    TM = 512
    HB = TM // POOL_HALO

    def body(cur_ref, halo_ref, w_ref, sc_ref, y_ref):
        i = pl.program_id(0)
        halo = jnp.where(i > 0, halo_ref[...], 0.0)
        pooled = _pooled_groups(halo, cur_ref[...], i * TM)
        for g in range(len(POOL_WINDOWS)):
            sl = slice(g * POOL_GROUP, (g + 1) * POOL_GROUP)
            y = _dot(pooled[g].astype(BF16), w_ref[g].astype(BF16)) * sc_ref[:, sl]
            y_ref[:, sl] = y.astype(BF16)

    return pl.pallas_call(
        body, name="pool_fwd", grid=(S // TM,),
        in_specs=[pl.BlockSpec((TM, POOL_WIDTH), lambda i: (i, 0)),
                  pl.BlockSpec((POOL_HALO, POOL_WIDTH), lambda i: (jnp.maximum(i * HB - 1, 0), 0)),
                  _const_spec(pool_w.shape), _const_spec((1, POOL_WIDTH))],
        out_specs=pl.BlockSpec((TM, POOL_WIDTH), lambda i: (i, 0)),
        out_shape=jax.ShapeDtypeStruct((S, POOL_WIDTH), BF16),
        compiler_params=_params(("parallel",)),
    )(pool_in, pool_in, pool_w, pool_scale)


def _mix_out_fwd(attn, pool, w_out, x, g_post, g_ffn_pre):
    S, D = x.shape
    TM = 512

    def body(a_ref, p_ref, w_ref, x_ref, gp_ref, gf_ref, mixed_ref, x1_ref, h2_ref, cat_ref):
        ab = a_ref[...]
        cat_ref[:, :ATTN_WIDTH] = ab
        cat_ref[:, ATTN_WIDTH:] = p_ref[...]
        mixed = _dot(ab, w_ref[:ATTN_WIDTH, :]) + _dot(p_ref[...], w_ref[ATTN_WIDTH:, :])
        mixed_ref[...] = mixed.astype(BF16)
        n, _ = _rms_stats(mixed)
        x1 = x_ref[...] + n * gp_ref[...]
        x1_ref[...] = x1
        n2, _ = _rms_stats(x1)
        h2_ref[...] = (n2 * gf_ref[...]).astype(BF16)

    row = lambda w: pl.BlockSpec((TM, w), lambda i: (i, 0))
    return pl.pallas_call(
        body, name="mix_out_fwd", grid=(S // TM,),
        in_specs=[row(ATTN_WIDTH), row(POOL_WIDTH), _const_spec(w_out.shape), row(D),
                  _const_spec((1, D)), _const_spec((1, D))],
        out_specs=[row(D), row(D), row(D), row(D)],
        out_shape=[jax.ShapeDtypeStruct((S, D), BF16), jax.ShapeDtypeStruct((S, D), F32),
                   jax.ShapeDtypeStruct((S, D), BF16), jax.ShapeDtypeStruct((S, D), BF16)],
        compiler_params=_params(("parallel",), VMEM_LIMIT),
    )(attn, pool, w_out, x, g_post, g_ffn_pre)


def _ffn_fwd(h2, x1, target, w_up, w_down, conv_w, conv_b, g_post):
    S, D = x1.shape
    CW = w_up.shape[2]
    FF = 2 * CW
    TM = 256
    piece = 4 * LANES
    pieces = [(lo, min(lo + piece, CW)) for lo in range(0, CW, piece)]

    def body(h2_ref, x1_ref, t_ref, wu_ref, wd_ref, cw_ref, cb_ref, g_ref,
             yv_ref, dy_ref, df_ref, dc_ref, loss_ref, dg_ref, dcb_ref, dcw_ref,
             ue_s, dgate_s, dval_s):
        i = pl.program_id(0)

        @pl.when(i == 0)
        def _():
            loss_ref[...] = jnp.zeros_like(loss_ref)
            dg_ref[...] = jnp.zeros_like(dg_ref)
            dcb_ref[...] = jnp.zeros_like(dcb_ref)
            dcw_ref[...] = jnp.zeros_like(dcw_ref)
            ue_s[0:CONV_HALO, :] = jnp.zeros((CONV_HALO, 2 * FF), F32)

        @pl.when(i > 0)
        def _():
            ue_s[0:CONV_HALO, :] = ue_s[TM:TM + CONV_HALO, :]

        def shifted(cols, k):
            return pltpu.roll(ue_s[:, cols], k, 0)[CONV_HALO:]

        def conv(cols):
            return (cb_ref[:, cols] + cw_ref[2, :, cols] * ue_s[CONV_HALO:, cols]
                    + cw_ref[1, :, cols] * shifted(cols, 1) + cw_ref[0, :, cols] * shifted(cols, 2))

        hb = h2_ref[...]
        f = jnp.zeros((TM, D), F32)
        for j in range(2):
            jc = slice(j * CW, (j + 1) * CW)
            for half in range(2):
                blk = 2 * half + j
                cols = slice(blk * CW, (blk + 1) * CW)
                ue_s[CONV_HALO:, cols] = _dot(hb, wu_ref[blk])
            for lo, hi in pieces:
                pc = slice(j * CW + lo, j * CW + hi)
                gelu, dgelu = _gelu_tanh(conv(pc).astype(BF16))
                val = conv(slice(FF + j * CW + lo, FF + j * CW + hi)).astype(BF16)
                dgate_s[:, pc] = val * dgelu
                dval_s[:, pc] = gelu
                yv_ref[:, pc] = gelu * val
            f = f + _dot(yv_ref[:, jc], wd_ref[jc, :])

        n, r = _rms_stats(f)
        err = x1_ref[...] + n * g_ref[...] - t_ref[...]
        loss_ref[...] += 0.5 * jnp.sum(jnp.mean(err * err, axis=-1, keepdims=True), axis=0, keepdims=True)
        dy = err / D
        dy_ref[...] = dy
        df, dg = _rms_bwd(dy, n, r, g_ref[...])
        dg_ref[...] += dg
        dfb = df.astype(BF16)
        df_ref[...] = dfb

        for j in range(2):
            jc = slice(j * CW, (j + 1) * CW)
            dyv = _dot_nt(dfb, wd_ref[jc, :])
            for lo, hi in pieces:
                pc = slice(j * CW + lo, j * CW + hi)
                for half, scale_s in ((0, dgate_s), (1, dval_s)):
                    cols = slice(half * FF + j * CW + lo, half * FF + j * CW + hi)
                    dcv = dyv[:, lo:hi] * scale_s[:, pc].astype(F32)
                    dc_ref[:, cols] = dcv.astype(BF16)
                    dcb_ref[:, cols] += jnp.sum(dcv, axis=0, keepdims=True)
                    dcw_ref[2, :, cols] += jnp.sum(dcv * ue_s[CONV_HALO:, cols], axis=0, keepdims=True)
                    dcw_ref[1, :, cols] += jnp.sum(dcv * shifted(cols, 1), axis=0, keepdims=True)
                    dcw_ref[0, :, cols] += jnp.sum(dcv * shifted(cols, 2), axis=0, keepdims=True)

    row = lambda w: pl.BlockSpec((TM, w), lambda i: (i, 0))
    acc = lambda shape: pl.BlockSpec(shape, lambda i: (0,) * len(shape))
    return pl.pallas_call(
        body, name="ffn_fwd", grid=(S // TM,),
        in_specs=[row(D), row(D), row(D), _const_spec(w_up.shape), _const_spec(w_down.shape),
                  _const_spec(conv_w.shape), _const_spec((1, 2 * FF)), _const_spec((1, D))],
        out_specs=[row(FF), row(D), row(D), row(2 * FF),
                   acc((1, 1)), acc((1, D)), acc((1, 2 * FF)), acc((CONV_WIDTH, 1, 2 * FF))],
        out_shape=[jax.ShapeDtypeStruct((S, FF), BF16),
                   jax.ShapeDtypeStruct((S, D), F32), jax.ShapeDtypeStruct((S, D), BF16),
                   jax.ShapeDtypeStruct((S, 2 * FF), BF16),
                   jax.ShapeDtypeStruct((1, 1), F32), jax.ShapeDtypeStruct((1, D), F32),
                   jax.ShapeDtypeStruct((1, 2 * FF), F32), jax.ShapeDtypeStruct((CONV_WIDTH, 1, 2 * FF), F32)],
        scratch_shapes=[pltpu.VMEM((TM + CONV_HALO, 2 * FF), F32), pltpu.VMEM((TM, FF), BF16),
                        pltpu.VMEM((TM, FF), BF16)],
        compiler_params=_params(("arbitrary",), VMEM_LIMIT),
    )(h2, x1, target, w_up, w_down, conv_w, conv_b, g_post)


def _ffn_bwd(dc, conv_w, w_up, x1, g_ffn_pre, dy):
    S, D = x1.shape
    CW = w_up.shape[2]
    F2 = 4 * CW
    TM = 256
    HB = TM // CONV_HALO
    last = S // CONV_HALO - 1
    n_tiles = S // TM

    def body(dc_ref, halo_ref, cw_ref, wu_ref, x1_ref, g_ref, dy_ref, du_ref, dx1_ref, dg_ref):
        i = pl.program_id(0)

        @pl.when(i == 0)
        def _():
            dg_ref[...] = jnp.zeros_like(dg_ref)

        keep = i < n_tiles - 1
        dh2 = jnp.zeros((TM, D), F32)
        for blk in range(N_SHARD):
            cols = slice(blk * CW, (blk + 1) * CW)
            halo = jnp.where(keep, halo_ref[:, cols].astype(F32), 0.0)
            dce = jnp.concatenate([dc_ref[:, cols].astype(F32), halo], axis=0)
            rows = TM + CONV_HALO
            du = (cw_ref[2, :, cols] * dce[:TM]
                  + cw_ref[1, :, cols] * pltpu.roll(dce, rows - 1, 0)[:TM]
                  + cw_ref[0, :, cols] * pltpu.roll(dce, rows - 2, 0)[:TM])
            dub = du.astype(BF16)
            du_ref[:, cols] = dub
            dh2 = dh2 + _dot_nt(dub, wu_ref[blk])
        n2, r2 = _rms_stats(x1_ref[...])
        dx, dg = _rms_bwd(dh2, n2, r2, g_ref[...])
        dg_ref[...] += dg
        dx1_ref[...] = (dy_ref[...] + dx).astype(BF16)

    row = lambda w: pl.BlockSpec((TM, w), lambda i: (i, 0))
    return pl.pallas_call(
        body, name="ffn_bwd", grid=(S // TM,),
        in_specs=[row(F2), pl.BlockSpec((CONV_HALO, F2), lambda i: (jnp.minimum((i + 1) * HB, last), 0)),
                  _const_spec(conv_w.shape), _const_spec(w_up.shape), row(D), _const_spec((1, D)), row(D)],
        out_specs=[row(F2), row(D), pl.BlockSpec((1, D), lambda i: (0, 0))],
        out_shape=[jax.ShapeDtypeStruct((S, F2), BF16), jax.ShapeDtypeStruct((S, D), BF16),
                   jax.ShapeDtypeStruct((1, D), F32)],
        compiler_params=_params(("arbitrary",), VMEM_LIMIT),
    )(dc, dc, conv_w, w_up, x1, g_ffn_pre, dy)


def _matmul_tn(a, b, n_blocks, name):
    S, M = a.shape
    N = b.shape[1]
    tn = N // n_blocks
    tm = M if M <= 1024 else M // 2
    tk = 2048
    nk = S // tk

    def body(a_ref, b_ref, o_ref):
        @pl.when(pl.program_id(2) == 0)
        def _():
            o_ref[...] = jnp.zeros_like(o_ref)
        o_ref[0] += _dot_tn(a_ref[...], b_ref[...])

    return pl.pallas_call(
        body, name=name, grid=(M // tm, n_blocks, nk),
        in_specs=[pl.BlockSpec((tk, tm), lambda i, j, k: (k, i)), pl.BlockSpec((tk, tn), lambda i, j, k: (k, j))],
        out_specs=pl.BlockSpec((1, tm, tn), lambda i, j, k: (j, i, 0)),
        out_shape=jax.ShapeDtypeStruct((n_blocks, M, tn), F32),
        compiler_params=_params(("parallel", "parallel", "arbitrary"), VMEM_LIMIT),
    )(a, b)


def _mix_out_bwd(dx1, mixed, g_post, w_out, attn, cargo=()):
    S, D = dx1.shape
    TM = 512
    nd = len(DILATIONS)
    nc = len(cargo)
    kinds = [kind for kind, _ in cargo]
    n_chunks = ATTN_WIDTH // LANES

    def body(*refs):
        dx_ref, m_ref, g_ref, w_ref, a_ref = refs[:5]
        dm_ref, dp_ref, dg_ref = refs[5 + nc:8 + nc]
        da_refs, dl_refs = refs[8 + nc:8 + nc + nd], refs[8 + nc + nd:8 + nc + 2 * nd]
        n_out = 8 + nc + 2 * nd
        t_s = refs[n_out + nc:n_out + nc + n_chunks]
        c_s = refs[n_out + nc + n_chunks:n_out + nc + n_chunks + 1]
        cargo_refs = (kinds, refs[5:5 + nc], refs[n_out:n_out + nc], refs[n_out + nc + n_chunks + 1:])
        if nc:
            _cargo_start(*cargo_refs, pl.program_id(0) == 0)

        @pl.when(pl.program_id(0) == 0)
        def _():
            dg_ref[...] = jnp.zeros_like(dg_ref)

        n, r = _rms_stats(m_ref[...].astype(F32))
        dm, dg = _rms_bwd(dx_ref[...].astype(F32), n, r, g_ref[...])
        dg_ref[...] += dg
        dmb = dm.astype(BF16)
        dm_ref[...] = dmb
        da = _dot_nt(dmb, w_ref[:ATTN_WIDTH, :])
        _put_tokens(t_s, da)
        for i, d in enumerate(DILATIONS):
            _to_residue(da, t_s, da_refs[i], d, BF16)
        dp_ref[...] = _dot_nt(dmb, w_ref[ATTN_WIDTH:, :]).astype(BF16)
        gather = _head_sum_matrix()
        delta = sum(_dot(p, gather) for p in _bf16_pieces(da * a_ref[...].astype(F32), 2))
        _put_tokens(c_s, delta)
        for i, d in enumerate(DILATIONS):
            _to_residue(delta, c_s, dl_refs[i], d, F32)
        if nc:
            _cargo_finish(*cargo_refs, pl.program_id(0) == S // TM - 1)

    row = lambda w: pl.BlockSpec((TM, w), lambda i: (i, 0))
    specs = [_residue_spec(TM, d) for d in DILATIONS]
    arrays, cargo_specs, shapes, aliases, sems = _cargo_call(cargo, 5, 3 + 2 * nd)
    out = pl.pallas_call(
        body, name="mix_out_bwd", grid=(S // TM,),
        in_specs=[row(D), row(D), _const_spec((1, D)), _const_spec(w_out.shape), row(ATTN_WIDTH)] + cargo_specs,
        out_specs=[row(D), row(POOL_WIDTH), pl.BlockSpec((1, D), lambda i: (0, 0))] + specs
        + [_residue_spec(TM, d, STAT_WIDTH) for d in DILATIONS] + cargo_specs,
        out_shape=[jax.ShapeDtypeStruct((S, D), BF16), jax.ShapeDtypeStruct((S, POOL_WIDTH), BF16),
                   jax.ShapeDtypeStruct((1, D), F32)]
        + [_residue_shape(S, d, BF16) for d in DILATIONS]
        + [_residue_shape(S, d, F32, STAT_WIDTH) for d in DILATIONS] + shapes,
        input_output_aliases=aliases,
        scratch_shapes=_token_scratch(TM) + _token_scratch(TM, STAT_WIDTH) + sems,
        compiler_params=_params(("arbitrary",), VMEM_LIMIT),
    )(dx1, mixed, g_post, w_out, attn, *arrays)
    return out[0], out[1], out[2], out[3:3 + nd], out[3 + nd:3 + 2 * nd], out[3 + 2 * nd:]


def _pool_bwd(pool_in, d_pool, pool_w, pool_scale):
    S = pool_in.shape[0]
    TM = 512
    HB = TM // POOL_HALO
    last = S // POOL_HALO - 1
    G = len(POOL_WINDOWS)

    def body(cur_ref, halo_ref, dcur_ref, dnext_ref, w_ref, sc_ref, dxin_ref, dw_ref, dsc_ref):
        i = pl.program_id(0)

        @pl.when(i == 0)
        def _():
            dw_ref[...] = jnp.zeros_like(dw_ref)
            dsc_ref[...] = jnp.zeros_like(dsc_ref)

        halo = jnp.where(i > 0, halo_ref[...], 0.0)
        pooled = _pooled_groups(halo, cur_ref[...], i * TM)
        dnext = jnp.where(i < S // TM - 1, dnext_ref[...].astype(F32), 0.0)
        dye = jnp.concatenate([dcur_ref[...].astype(F32), dnext], axis=0)
        for g, w in enumerate(POOL_WINDOWS):
            sl = slice(g * POOL_GROUP, (g + 1) * POOL_GROUP)
            wg = w_ref[g].astype(BF16)
            pb = pooled[g].astype(BF16)
            dsc_ref[:, sl] += jnp.sum(dye[:TM, sl] * _dot(pb, wg), axis=0, keepdims=True)
            dpre = (dye[:, sl] * sc_ref[:, sl]).astype(BF16)
            dw_ref[g] += _dot_tn(pb, dpre[:TM])
            dpooled = _dot_nt(dpre, wg)
            z = dpooled / _pool_counts(i * TM, TM + POOL_HALO, w)
            dxin_ref[:, sl] = (_leading_sums(z, w)[:TM] - dpooled[:TM]).astype(BF16)

    row = pl.BlockSpec((TM, POOL_WIDTH), lambda i: (i, 0))
    return pl.pallas_call(
        body, name="pool_bwd", grid=(S // TM,),
        in_specs=[row, pl.BlockSpec((POOL_HALO, POOL_WIDTH), lambda i: (jnp.maximum(i * HB - 1, 0), 0)),
                  row, pl.BlockSpec((POOL_HALO, POOL_WIDTH), lambda i: (jnp.minimum((i + 1) * HB, last), 0)),
                  _const_spec(pool_w.shape), _const_spec((1, POOL_WIDTH))],
        out_specs=[row, pl.BlockSpec((G, POOL_GROUP, POOL_GROUP), lambda i: (0, 0, 0)),
                   pl.BlockSpec((1, POOL_WIDTH), lambda i: (0, 0))],
        out_shape=[jax.ShapeDtypeStruct((S, POOL_WIDTH), BF16), jax.ShapeDtypeStruct((G, POOL_GROUP, POOL_GROUP), F32),
                   jax.ShapeDtypeStruct((1, POOL_WIDTH), F32)],
        compiler_params=_params(("arbitrary",)),
    )(pool_in, pool_in, d_pool, d_pool, pool_w, pool_scale)


def _attn_bwd(q, k, v, d_attn, lse, delta, d, cargo=()):
    L = q.shape[0]
    nb = L // ATTN_BLOCK
    group = min(d, RESIDUES_PER_STEP)
    width = group * ATTN_WIDTH
    nc = len(cargo)
    kinds = [kind for kind, _ in cargo]

    def body(*refs):
        q_ref, kp_ref, kc_ref, vp_ref, vc_ref, do_ref, lse_ref, dl_ref = refs[:8]
        dq_ref, dk_ref, dv_ref = refs[8 + nc:11 + nc]
        ck_s, cv_s = refs[11 + 2 * nc:13 + 2 * nc]
        cargo_refs = (kinds, refs[8:8 + nc], refs[11 + nc:11 + 2 * nc], refs[13 + 2 * nc:])
        r, n = pl.program_id(0), pl.program_id(1)
        if nc:
            _cargo_start(*cargo_refs, (r == 0) & (n == 0))

        @pl.when(n == 0)
        def _():
            ck_s[...] = jnp.zeros_like(ck_s)
            cv_s[...] = jnp.zeros_like(cv_s)

        @pl.when(n < nb)
        def _():
            valid = _band_mask(n)
            valid2 = jnp.concatenate([valid, valid], axis=0)
            first = _first_head_lanes()

            def stacked_column(ref, hp):
                lane = 2 * hp * STAT_LANES
                return jnp.concatenate([ref[:, lane:lane + 1], ref[:, lane + STAT_LANES:lane + STAT_LANES + 1]], axis=0)

            for hp in range(width // LANES):
                sl = slice(hp * LANES, (hp + 1) * LANES)
                qq = _stack_heads(q_ref[:, sl], first)
                dd = _stack_heads(do_ref[:, sl], first)
                kk = jnp.concatenate([kp_ref[:, sl], kc_ref[:, sl]], axis=0)
                vv = jnp.concatenate([vp_ref[:, sl], vc_ref[:, sl]], axis=0)
                s = _dot_nt(qq, kk)
                p = jnp.where(valid2, jnp.exp(s - stacked_column(lse_ref, hp)), 0.0)
                dp = _dot_nt(dd, vv)
                ds = (p * (dp - stacked_column(dl_ref, hp))).astype(BF16)
                dq_ref[:, sl] = (_unstack_heads(_dot(ds, kk), first) * ATTN_SCALE).astype(BF16)
                dk = _dot_tn(ds, qq)
                dv = _dot_tn(p.astype(BF16), dd)
                dk_ref[:, sl] = (ck_s[:, sl] + dk[:ATTN_BLOCK]).astype(BF16)
                dv_ref[:, sl] = (cv_s[:, sl] + dv[:ATTN_BLOCK]).astype(BF16)
                ck_s[:, sl] = dk[ATTN_BLOCK:]
                cv_s[:, sl] = dv[ATTN_BLOCK:]

        @pl.when(n == nb)
        def _():
            dk_ref[...] = ck_s[...].astype(BF16)
            dv_ref[...] = cv_s[...].astype(BF16)

        if nc:
            _cargo_finish(*cargo_refs, (r == d // group - 1) & (n == nb))

    blk = (ATTN_BLOCK, width)
    cur = pl.BlockSpec(blk, lambda r, n: (jnp.minimum(n, nb - 1), r))
    stat = pl.BlockSpec((ATTN_BLOCK, group * STAT_WIDTH), lambda r, n: (jnp.minimum(n, nb - 1), r))
    prev = pl.BlockSpec(blk, lambda r, n: (jnp.maximum(jnp.minimum(n, nb - 1) - 1, 0), r))
    done = pl.BlockSpec(blk, lambda r, n: (jnp.maximum(n - 1, 0), r))
    arrays, specs, shapes, aliases, sems = _cargo_call(cargo, 8, 3)
    out = pl.pallas_call(
        body, name=f"attn_bwd_d{d}", grid=(d // group, nb + 1),
        in_specs=[cur, prev, cur, prev, cur, cur, stat, stat] + specs, out_specs=[cur, done, done] + specs,
        out_shape=[jax.ShapeDtypeStruct((L, d * ATTN_WIDTH), BF16)] * 3 + shapes,
        input_output_aliases=aliases,
        scratch_shapes=[pltpu.VMEM(blk, F32), pltpu.VMEM(blk, F32)] + sems,
        compiler_params=_params(("arbitrary", "arbitrary")),
    )(q, k, k, v, v, d_attn, lse, delta, *arrays)
    return out[:3], out[3:]


def _attn_bwd_consecutive(q, k, v, d_attn, lse, delta, d, cargo=()):
    L = q.shape[0]
    qb = CONSECUTIVE_BLOCKS
    steps = L // (qb * ATTN_BLOCK)
    nc = len(cargo)
    kinds = [kind for kind, _ in cargo]

    def body(*refs):
        q_ref, kp_ref, kc_ref, vp_ref, vc_ref, do_ref, lse_ref, dl_ref = refs[:8]
        dq_ref, dk_ref, dv_ref, ek_ref, ev_ref = refs[8 + nc:13 + nc]
        cargo_refs = (kinds, refs[8:8 + nc], refs[13 + nc:13 + 2 * nc], refs[13 + 2 * nc:])
        r, n = pl.program_id(0), pl.program_id(1)
        if nc:
            _cargo_start(*cargo_refs, (r == 0) & (n == 0))
        first = _first_head_lanes()
        for hp in range(ATTN_WIDTH // LANES):
            sl = slice(hp * LANES, (hp + 1) * LANES)
            for sub in range(qb):
                rows = slice(sub * ATTN_BLOCK, (sub + 1) * ATTN_BLOCK)
                valid = _band_mask(n if sub == 0 else 1)
                valid2 = jnp.concatenate([valid, valid], axis=0)
                if sub == 0:
                    kk = jnp.concatenate([kp_ref[:, sl], kc_ref[rows, sl]], axis=0)
                    vv = jnp.concatenate([vp_ref[:, sl], vc_ref[rows, sl]], axis=0)
                else:
                    keys = slice((sub - 1) * ATTN_BLOCK, (sub + 1) * ATTN_BLOCK)
                    kk, vv = kc_ref[keys, sl], vc_ref[keys, sl]
                qq = _stack_heads(q_ref[rows, sl], first)
                dd = _stack_heads(do_ref[rows, sl], first)
                lane = 2 * hp * STAT_LANES
                column = lambda ref: jnp.concatenate(
                    [ref[rows, lane:lane + 1], ref[rows, lane + STAT_LANES:lane + STAT_LANES + 1]], axis=0)
                p = jnp.where(valid2, jnp.exp(_dot_nt(qq, kk) - column(lse_ref)), 0.0)
                ds = (p * (_dot_nt(dd, vv) - column(dl_ref))).astype(BF16)
                dq_ref[rows, sl] = (_unstack_heads(_dot(ds, kk), first) * ATTN_SCALE).astype(BF16)
                dk = _dot_tn(ds, qq)
                dv = _dot_tn(p.astype(BF16), dd)
                if sub == 0:
                    ek_ref[:, sl] = dk[:ATTN_BLOCK].astype(BF16)
                    ev_ref[:, sl] = dv[:ATTN_BLOCK].astype(BF16)
                else:
                    before = slice((sub - 1) * ATTN_BLOCK, sub * ATTN_BLOCK)
                    dk_ref[before, sl] = (carry_k + dk[:ATTN_BLOCK]).astype(BF16)
                    dv_ref[before, sl] = (carry_v + dv[:ATTN_BLOCK]).astype(BF16)
                carry_k, carry_v = dk[ATTN_BLOCK:], dv[ATTN_BLOCK:]
            dk_ref[rows, sl] = carry_k.astype(BF16)
            dv_ref[rows, sl] = carry_v.astype(BF16)
        if nc:
            _cargo_finish(*cargo_refs, (r == d - 1) & (n == steps - 1))

    cur = pl.BlockSpec((qb * ATTN_BLOCK, ATTN_WIDTH), lambda r, n: (n, r))
    prev = pl.BlockSpec((ATTN_BLOCK, ATTN_WIDTH), lambda r, n: (jnp.maximum(n * qb - 1, 0), r))
    edge = pl.BlockSpec((ATTN_BLOCK, ATTN_WIDTH), lambda r, n: (n, r))
    stat = pl.BlockSpec((qb * ATTN_BLOCK, STAT_WIDTH), lambda r, n: (n, r))
    arrays, specs, shapes, aliases, sems = _cargo_call(cargo, 8, 5)
    out = pl.pallas_call(
        body, name=f"attn_bwd_d{d}", grid=(d, steps),
        in_specs=[cur, prev, cur, prev, cur, cur, stat, stat] + specs, out_specs=[cur, cur, cur, edge, edge] + specs,
        out_shape=[jax.ShapeDtypeStruct((L, d * ATTN_WIDTH), BF16)] * 3
        + [jax.ShapeDtypeStruct((steps * ATTN_BLOCK, d * ATTN_WIDTH), BF16)] * 2 + shapes,
        input_output_aliases=aliases, scratch_shapes=sems,
        compiler_params=_params(("arbitrary", "arbitrary")),
    )(q, k, k, v, v, d_attn, lse, delta, *arrays)
    return out[:3], out[3:5], out[5:]


def _mix_in_bwd(dqkv, edges, d_pool_in, w_in, x, g_pre, dx1):
    S, D = x.shape
    TM = CONSECUTIVE_BLOCKS * ATTN_BLOCK
    nd = len(DILATIONS)
    n_tiles = S // TM

    def body(*refs):
        g_refs = refs[:3 * nd]
        e_refs = (None,) + refs[3 * nd:3 * nd + 2]
        dpi_ref, w_ref, x_ref, g_ref, dx1_ref, dproj_ref, gx_ref, dg_ref = refs[3 * nd + 2:3 * nd + 10]
        t_s = refs[3 * nd + 10:]

        @pl.when(pl.program_id(0) == 0)
        def _():
            dg_ref[...] = jnp.zeros_like(dg_ref)

        dh = jnp.zeros((TM, D), F32)
        for a in range(4):
            if a < 3:
                tot = g_refs[a][...].astype(F32)
                if a > 0:
                    late = jnp.where(pl.program_id(0) < n_tiles - 1, e_refs[a][...].astype(F32), 0.0)
                    tot = jnp.concatenate([tot[:TM - ATTN_BLOCK], tot[TM - ATTN_BLOCK:] + late], axis=0)
                for i, d in enumerate(DILATIONS[1:]):
                    tot = tot + _from_residue(g_refs[3 * (i + 1) + a], t_s, d)
                db = tot.astype(BF16)
            else:
                db = dpi_ref[...]
            dproj_ref[:, a * ATTN_WIDTH:(a + 1) * ATTN_WIDTH] = db
            dh = dh + _dot_nt(db, w_ref[a])
        n, r = _rms_stats(x_ref[...])
        dx, dg = _rms_bwd(dh, n, r, g_ref[...])
        dg_ref[...] += dg
        gx_ref[...] = dx1_ref[...].astype(F32) + dx

    row = lambda w: pl.BlockSpec((TM, w), lambda i: (i, 0))
    edge = pl.BlockSpec((ATTN_BLOCK, ATTN_WIDTH), lambda i: (jnp.minimum(i + 1, n_tiles - 1), 0))
    return pl.pallas_call(
        body, name="mix_in_bwd", grid=(S // TM,),
        in_specs=[_residue_spec(TM, d) for d in DILATIONS for _ in range(3)] + [edge, edge]
        + [row(POOL_WIDTH), _const_spec(w_in.shape), row(D), _const_spec((1, D)), row(D)],
        out_specs=[row(4 * ATTN_WIDTH), row(D), pl.BlockSpec((1, D), lambda i: (0, 0))],
        out_shape=[jax.ShapeDtypeStruct((S, 4 * ATTN_WIDTH), BF16), jax.ShapeDtypeStruct((S, D), F32),
                   jax.ShapeDtypeStruct((1, D), F32)],
        scratch_shapes=_token_scratch(TM),
        compiler_params=_params(("arbitrary",), VMEM_LIMIT),
    )(*[g for gs in dqkv for g in gs], *edges, d_pool_in, w_in, x, g_pre, dx1)


SMALL_EARLY = ("pool_w", "pool_scale", "g_mix_post", "g_ffn_pre", "conv_b", "g_ffn_post", "conv_w")
SMALL_LATE = ("g_mix_pre",)


def _pack_small(grads, names):
    parts = []
    for n in names:
        g = grads[n]
        if n == "conv_w":
            g = g.reshape(CONV_WIDTH, N_SHARD, -1).transpose(1, 0, 2)
        parts.append(g.reshape(-1, LANES))
    return jnp.concatenate(parts, axis=0) if len(parts) > 1 else parts[0]


def _unpack_small(packed, names, like, shard):
    out, row = {}, 0
    for n in names:
        size = like[n].size * (N_SHARD if n == "conv_w" else 1)
        g = packed[row:row + size // LANES]
        row += size // LANES
        if n == "conv_w":
            g = lax.dynamic_slice_in_dim(g.reshape((N_SHARD,) + like[n].shape), shard, 1, axis=0)[0]
        out[n] = g.reshape(like[n].shape)
    return out


def _local_step(x, target, g_mix_pre, w_in, pool_w, pool_scale, w_out, g_mix_post, g_ffn_pre,
                w_up, conv_w, conv_b, w_down, g_ffn_post, mesh_pos=None):
    on_mesh = mesh_pos is not None
    D = x.shape[1]
    CW = w_up.shape[2]
    qkv, pool_in, h1, got = _mix_in_fwd(x, g_mix_pre, w_in, [("ici", w_up)] if on_mesh else ())
    w_up = got[0] if on_mesh else w_up
    o1, l1, got = _attn_fwd(*qkv[0], 1, [("d2d", w_up), ("ici", w_out)] if on_mesh else ())
    w_up, w_out = got if on_mesh else (w_up, w_out)
    o4, l4, got = _attn_fwd(*qkv[1], 4, [("d2d", w_out), ("ici", w_down)] if on_mesh else ())
    w_out, w_down = got if on_mesh else (w_out, w_down)
    o16, l16, got = _attn_fwd(*qkv[2], 16, [("d2d", w_down)] if on_mesh else ())
    w_down = got[0] if on_mesh else w_down
    w_out = w_out.reshape(D, D)
    w_down = w_down.reshape(2 * CW, D)
    attn, lse = _attn_mix((o1, o4, o16), (l1, l4, l16))
    pool = _pool_fwd(pool_in, pool_w, pool_scale)
    mixed, x1, h2, cat = _mix_out_fwd(attn, pool, w_out, x, g_mix_post, g_ffn_pre)

    yv, dy, df, dc, loss, d_g_ffn_post, d_conv_b, d_conv_w = _ffn_fwd(
        h2, x1, target, w_up, w_down, conv_w, conv_b, g_ffn_post)
    du, dx1, d_g_ffn_pre = _ffn_bwd(dc, conv_w, w_up, x1, g_ffn_pre, dy)
    d_w_up = _matmul_tn(h2, du, N_SHARD, "grad_w_up")
    d_w_down = _matmul_tn(yv, df, 1, "grad_w_down")[0].reshape(N_SHARD, CW // 2, D)
    swap = [("swap", d_w_up), ("swap", d_w_down)] if on_mesh else ()
    d_mixed, d_pool, d_g_mix_post, d_attn, delta, from_sibling = _mix_out_bwd(dx1, mixed, g_mix_post, w_out, attn, swap)
    d_w_out = _matmul_tn(cat, d_mixed, 1, "grad_w_out")[0].reshape(N_SHARD, D // N_SHARD, D)
    d_pool_in, d_pool_w, d_pool_scale = _pool_bwd(pool_in, d_pool, pool_w, pool_scale)
    grads = dict(pool_w=d_pool_w, pool_scale=d_pool_scale, w_out=d_w_out, g_mix_post=d_g_mix_post,
                 g_ffn_pre=d_g_ffn_pre, w_up=d_w_up, conv_w=d_conv_w, conv_b=d_conv_b, w_down=d_w_down,
                 g_ffn_post=d_g_ffn_post)
    cargo = [(), (), ()]
    if on_mesh:
        c_arr, device, shard_arr = mesh_pos
        up_f32, up_bf16 = _pair_sum(d_w_up, from_sibling[0], c_arr, "pair_sum_w_up")
        down_f32, down_bf16 = _pair_sum(d_w_down, from_sibling[1], c_arr, "pair_sum_w_down")
        early = _pack_small(grads, SMALL_EARLY)
        early_slots = lax.dynamic_update_index_in_dim(jnp.zeros((8,) + early.shape, F32), early, device, 0)
        cargo = [[("scatter", down_bf16), ("everyone", early_slots), ("swap", d_w_out)], [("scatter", up_bf16)], []]

    dqkv1, edges, landed1 = _attn_bwd_consecutive(*qkv[0], d_attn[0], lse[0], delta[0], 1, cargo[0])
    dqkv4, landed4 = _attn_bwd(*qkv[1], d_attn[1], lse[1], delta[1], 4, cargo[1])
    if on_mesh:
        halves = [_shard_sum(up_f32, landed4[0], shard_arr, c_arr, "shard_sum_w_up"),
                  _shard_sum(down_f32, landed1[0], shard_arr, c_arr, "shard_sum_w_down")]
        out_f32, out_bf16 = _pair_sum(d_w_out, landed1[2], c_arr, "pair_sum_w_out")
        cargo[2] = cargo[2] + [("join", h) for h in halves] + [("scatter", out_bf16)]
    if qkv[2][0].shape[0] == CONSECUTIVE_BLOCKS * ATTN_BLOCK:
        dqkv16, _, landed16 = _attn_bwd_consecutive(*qkv[2], d_attn[2], lse[2], delta[2], 16, cargo[2])
    else:
        dqkv16, landed16 = _attn_bwd(*qkv[2], d_attn[2], lse[2], delta[2], 16, cargo[2])
    if on_mesh:
        grads.update(small_early=landed1[1], w_up=landed16[0], w_down=landed16[1],
                     w_out=_shard_sum(out_f32, landed16[2], shard_arr, c_arr, "shard_sum_w_out"))
    d_proj, grad_x, grads["g_mix_pre"] = _mix_in_bwd((dqkv1, dqkv4, dqkv16), edges, d_pool_in, w_in, x, g_mix_pre, dx1)
    grads["w_in"] = _matmul_tn(h1, d_proj, N_SHARD, "grad_w_in")
    return loss, grad_x, grads


ANY = pl.BlockSpec(memory_space=pl.ANY)


def _position():
    x, y, c = lax.axis_index("x"), lax.axis_index("y"), lax.axis_index("c")
    chips = [(1 - x, y), (x, 1 - y), (1 - x, 1 - y)]
    return x, y, c, chips


def _remote(src, dst, send_sem, recv_sem, to):
    return pltpu.make_async_remote_copy(src_ref=src, dst_ref=dst, send_sem=send_sem, recv_sem=recv_sem,
                                        device_id=to, device_id_type=MESH)


def _cast_bf16(w, shard_arr, name):
    R, C = w.shape
    tr = R // 2

    def body(s_ref, w_ref, o_ref):
        o_ref[0] = w_ref[...].astype(BF16)

    return pl.pallas_call(
        body, name=name,
        grid_spec=pltpu.PrefetchScalarGridSpec(
            num_scalar_prefetch=1, grid=(2,),
            in_specs=[pl.BlockSpec((tr, C), lambda i, s_ref: (i, 0))],
            out_specs=pl.BlockSpec((1, tr, C), lambda i, s_ref: (s_ref[0], i, 0))),
        out_shape=jax.ShapeDtypeStruct((N_SHARD, R, C), BF16),
        compiler_params=_params(("parallel",)))(shard_arr, w)


def _gather_weights(bufs):
    n = len(bufs) - 1

    def body(*refs):
        outs, cw_out = refs[n + 1:2 * n + 1], refs[2 * n + 1]
        ici_send, ici_recv, d2d_send, d2d_recv = refs[2 * n + 2:]
        x, y, c, chips = _position()
        s = 2 * x + y
        sibling = (x, y, 1 - c)

        def half(a, shard, h):
            rows = outs[a].shape[1] // 2
            return outs[a].at[shard, pl.ds(h * rows, rows), :]

        sends = []
        for a in range(n):
            for j, (px, py) in enumerate(chips):
                sends.append(_remote(half(a, s, c), half(a, s, c),
                                     ici_send.at[3 * a + j], ici_recv.at[3 * a + j], (px, py, c)))
        for j, (px, py) in enumerate(chips):
            sends.append(_remote(cw_out.at[s], cw_out.at[s], ici_send.at[3 * n + j], ici_recv.at[3 * n + j], (px, py, c)))
        for cp in sends:
            cp.start()
        passed = []
        for a in range(n):
            for j, (px, py) in enumerate(chips):
                sj = 2 * px + py
                got = half(a, sj, c)
                _remote(got, got, ici_send.at[3 * a + j], ici_recv.at[3 * a + j], (px, py, c)).wait_recv()
                fwd = _remote(got, got, d2d_send.at[3 * a + j], d2d_recv.at[3 * a + j], sibling)
                fwd.start()
                passed.append(fwd)
        for j, (px, py) in enumerate(chips):
            got = cw_out.at[2 * px + py]
            _remote(got, got, ici_send.at[3 * n + j], ici_recv.at[3 * n + j], (px, py, c)).wait_recv()
        for a in range(n):
            for j, (px, py) in enumerate(chips):
                got = half(a, 2 * px + py, 1 - c)
                _remote(got, got, d2d_send.at[3 * a + j], d2d_recv.at[3 * a + j], sibling).wait_recv()
        for cp in sends + passed:
            cp.wait_send()

    return pl.pallas_call(
        body, name="gather_weights",
        in_specs=[ANY] * (n + 1), out_specs=[ANY] * (n + 1),
        out_shape=[jax.ShapeDtypeStruct(b.shape, b.dtype) for b in bufs],
        input_output_aliases={i: i for i in range(n + 1)},
        scratch_shapes=[pltpu.SemaphoreType.DMA((3 * n + 3,)), pltpu.SemaphoreType.DMA((3 * n + 3,)),
                        pltpu.SemaphoreType.DMA((3 * n,)), pltpu.SemaphoreType.DMA((3 * n,))],
        compiler_params=pltpu.CompilerParams(has_side_effects=True),
    )(*bufs)


def _swap_halves(grads, tag):
    n = len(grads)

    def body(*refs):
        ins, outs, send_sem, recv_sem = refs[:n], refs[n:2 * n], refs[2 * n], refs[2 * n + 1]
        x, y, c, _ = _position()
        copies = []
        for a in range(n):
            rows = ins[a].shape[1] // 2
            copies.append(_remote(ins[a].at[:, pl.ds((1 - c) * rows, rows), :], outs[a],
                                  send_sem.at[a], recv_sem.at[a], (x, y, 1 - c)))
        for cp in copies:
            cp.start()
        for cp in copies:
            cp.wait()

    return pl.pallas_call(
        body, name="swap_grad_halves_" + tag,
        in_specs=[ANY] * n, out_specs=[ANY] * n,
        out_shape=[jax.ShapeDtypeStruct((g.shape[0], g.shape[1] // 2, g.shape[2]), F32) for g in grads],
        scratch_shapes=[pltpu.SemaphoreType.DMA((n,)), pltpu.SemaphoreType.DMA((n,))],
        compiler_params=pltpu.CompilerParams(has_side_effects=True),
    )(*grads)


def _pair_sum(g, got, c_arr, name):
    n_sh, R, C = g.shape
    rows = R // 2

    def body(c_ref, g_ref, r_ref, f_ref, b_ref):
        t = g_ref[...] + r_ref[...]
        f_ref[...] = t
        b_ref[...] = t.astype(BF16)

    blk = pl.BlockSpec((1, rows, C), lambda i, c_ref: (i, 0, 0))
    return pl.pallas_call(
        body, name=name,
        grid_spec=pltpu.PrefetchScalarGridSpec(
            num_scalar_prefetch=1, grid=(n_sh,),
            in_specs=[pl.BlockSpec((1, rows, C), lambda i, c_ref: (i, c_ref[0], 0)), blk],
            out_specs=[blk, blk]),
        out_shape=[jax.ShapeDtypeStruct((n_sh, rows, C), F32), jax.ShapeDtypeStruct((n_sh, rows, C), BF16)],
        compiler_params=_params(("parallel",)),
    )(c_arr, g, got)


def _shard_sum(sums_f32, recv, shard_arr, c_arr, name):
    _, rows, C = sums_f32.shape

    def body(s_ref, c_ref, o_ref, r_ref, t_ref):
        t_ref[...] = ((o_ref[0] + r_ref[0].astype(F32)) + r_ref[1].astype(F32)) + r_ref[2].astype(F32)

    return pl.pallas_call(
        body, name=name,
        grid_spec=pltpu.PrefetchScalarGridSpec(
            num_scalar_prefetch=2, grid=(1,),
            in_specs=[pl.BlockSpec((1, rows, C), lambda i, s_ref, c_ref: (s_ref[0], 0, 0)),
                      pl.BlockSpec((3, rows, C), lambda i, s_ref, c_ref: (0, 0, 0))],
            out_specs=pl.BlockSpec((rows, C), lambda i, s_ref, c_ref: (c_ref[0], 0))),
        out_shape=jax.ShapeDtypeStruct((2 * rows, C), F32),
        compiler_params=_params(("arbitrary",)),
    )(shard_arr, c_arr, sums_f32, recv)


def _join_halves(bufs):
    n = len(bufs)

    def body(*refs):
        outs, send_sem, recv_sem = refs[n:2 * n], refs[2 * n], refs[2 * n + 1]
        x, y, c, _ = _position()
        copies = []
        for a in range(n):
            rows = outs[a].shape[0] // 2
            mine = outs[a].at[pl.ds(c * rows, rows), :]
            copies.append(_remote(mine, mine, send_sem.at[a], recv_sem.at[a], (x, y, 1 - c)))
        for cp in copies:
            cp.start()
        for a, cp in enumerate(copies):
            cp.wait_send()
            rows = outs[a].shape[0] // 2
            theirs = outs[a].at[pl.ds((1 - c) * rows, rows), :]
            _remote(theirs, theirs, send_sem.at[a], recv_sem.at[a], (x, y, 1 - c)).wait_recv()

    return pl.pallas_call(
        body, name="join_grad_halves",
        in_specs=[ANY] * n, out_specs=[ANY] * n,
        out_shape=[jax.ShapeDtypeStruct(b.shape, F32) for b in bufs],
        input_output_aliases={i: i for i in range(n)},
        scratch_shapes=[pltpu.SemaphoreType.DMA((n,)), pltpu.SemaphoreType.DMA((n,))],
        compiler_params=pltpu.CompilerParams(has_side_effects=True),
    )(*bufs)


def _small_sum(parts, tag):
    _, R, C = parts.shape

    def body(p_ref, o_ref):
        t = p_ref[0]
        for k in range(1, 8):
            t = t + p_ref[k]
        o_ref[...] = t

    return pl.pallas_call(
        body, name="small_grad_sum_" + tag, grid=(1,),
        in_specs=[pl.BlockSpec((8, R, C), lambda i: (0, 0, 0))], out_specs=pl.BlockSpec((R, C), lambda i: (0, 0)),
        out_shape=jax.ShapeDtypeStruct((R, C), F32), compiler_params=_params(("arbitrary",)),
    )(parts)


def _adamw_math(w, g, m, v):
    m = ADAM_B1 * m + (1.0 - ADAM_B1) * g
    v = ADAM_B2 * v + (1.0 - ADAM_B2) * (g * g)
    m_hat = m / (1.0 - ADAM_B1 ** ADAM_STEP)
    v_hat = v / (1.0 - ADAM_B2 ** ADAM_STEP)
    delta = -ADAM_LR * (m_hat / (jnp.sqrt(v_hat) + ADAM_EPS) + ADAM_WD * w)
    return delta, m, v


def _adamw_big(ws, gs, ms, vs, name, cargo=()):
    n = len(ws)
    nc = len(cargo)
    kinds = [kind for kind, _ in cargo]

    def body(*refs):
        ins, outs = refs[:4 * n], refs[4 * n + nc:7 * n + nc]
        cargo_refs = (kinds, refs[4 * n:4 * n + nc], refs[7 * n + nc:7 * n + 2 * nc], refs[7 * n + 2 * nc:])
        if nc:
            _cargo_start(*cargo_refs, pl.program_id(0) == 0)
        for a in range(n):
            w, g, m, v = (ins[k * n + a][...] for k in range(4))
            outs[a][...], outs[n + a][...], outs[2 * n + a][...] = _adamw_math(w, g, m, v)
        if nc:
            _cargo_finish(*cargo_refs, pl.program_id(0) == 3)

    blks = [pl.BlockSpec((w.shape[0] // 4, w.shape[1]), lambda i: (i, 0)) for w in ws]
    arrays, cargo_specs, shapes, aliases, sems = _cargo_call(cargo, 4 * n, 3 * n)
    out = pl.pallas_call(
        body, name=name, grid=(4,), in_specs=blks * 4 + cargo_specs, out_specs=blks * 3 + cargo_specs,
        out_shape=[jax.ShapeDtypeStruct(w.shape, F32) for w in ws] * 3 + shapes,
        input_output_aliases=aliases, scratch_shapes=sems,
        compiler_params=_params(("arbitrary",)),
    )(*ws, *gs, *ms, *vs, *arrays)
    return (out[:n], out[n:2 * n], out[2 * n:3 * n]), out[3 * n:]


def _adamw_small(ws, gs, ms, vs):
    n = len(ws)

    def body(*refs):
        for a in range(n):
            w, g, m, v = (refs[k * n + a][...] for k in range(4))
            d, nm, nv = _adamw_math(w, g, m, v)
            refs[4 * n + a][...] = d
            refs[5 * n + a][...] = nm
            refs[6 * n + a][...] = nv

    shapes = [jax.ShapeDtypeStruct(w.shape, F32) for w in ws]
    out = pl.pallas_call(body, name="adamw_small", out_shape=shapes * 3)(*ws, *gs, *ms, *vs)
    return out[:n], out[n:2 * n], out[2 * n:]


BIG = ("w_in", "w_out", "w_up", "w_down")
SMALL = ("g_mix_pre", "pool_w", "pool_scale", "g_mix_post", "g_ffn_pre", "conv_b", "g_ffn_post", "conv_w")
ORDER = ("g_mix_pre", "w_in", "pool_w", "pool_scale", "w_out", "g_mix_post", "g_ffn_pre", "w_up", "conv_w", "conv_b",
         "w_down", "g_ffn_post")


def kernel(x, g_mix_pre, w_in, pool_w, pool_scale, w_out, g_mix_post, g_ffn_pre, w_up, conv_w, conv_b, w_down, g_ffn_post, loss_target, m_g_mix_pre, m_w_in, m_pool_w, m_pool_scale, m_w_out, m_g_mix_post, m_g_ffn_pre, m_w_up, m_conv_w, m_conv_b, m_w_down, m_g_ffn_post, v_g_mix_pre, v_w_in, v_pool_w, v_pool_scale, v_w_out, v_g_mix_post, v_g_ffn_pre, v_w_up, v_conv_w, v_conv_b, v_w_down, v_g_ffn_post):
    args = dict(locals())
    W = {n: args[n][0] for n in ORDER}
    M = {n: args["m_" + n][0] for n in ORDER}
    V = {n: args["v_" + n][0] for n in ORDER}
    for d in (W, M, V):
        d["pool_w"] = d["pool_w"].reshape(-1, POOL_GROUP)
        for n in ("g_mix_pre", "pool_scale", "g_mix_post", "g_ffn_pre", "conv_b", "g_ffn_post"):
            d[n] = d[n].reshape(1, -1)
    CW = W["w_up"].shape[1]
    c_arr = lax.axis_index("c").astype(jnp.int32).reshape(1)
    shard = 2 * lax.axis_index("x") + lax.axis_index("y")
    shard_arr = shard.astype(jnp.int32).reshape(1)
    device = 2 * shard + lax.axis_index("c")

    conv_w_slots = lax.dynamic_update_index_in_dim(jnp.zeros((N_SHARD,) + W["conv_w"].shape, F32), W["conv_w"], shard, 0)
    slots = {n: _cast_bf16(W[n], shard_arr, "cast_" + n) for n in BIG}
    w_in_g, conv_w_g = _gather_weights([slots["w_in"], conv_w_slots])
    conv_w_full = conv_w_g.transpose(1, 0, 2).reshape(CONV_WIDTH, 1, N_SHARD * CW)

    loss, grad_x, G = _local_step(
        x[0], loss_target[0], W["g_mix_pre"], w_in_g, W["pool_w"].reshape(-1, POOL_GROUP, POOL_GROUP), W["pool_scale"],
        slots["w_out"], W["g_mix_post"], W["g_ffn_pre"], slots["w_up"], conv_w_full, W["conv_b"],
        slots["w_down"], W["g_ffn_post"], (c_arr, device, shard_arr))

    late, ffn = ("w_in", "w_out"), ("w_up", "w_down")
    in_f32, in_bf16 = _pair_sum(G["w_in"], _swap_halves([G["w_in"]], "mix")[0], c_arr, "pair_sum_w_in")
    loss_rows = jnp.pad(loss, ((0, 7), (0, LANES - 1)))
    small = jnp.concatenate([_pack_small(G, SMALL_LATE), loss_rows], axis=0)
    small_slots = lax.dynamic_update_index_in_dim(jnp.zeros((8,) + small.shape, F32), small, device, 0)
    pick = lambda d, names: [d[n] for n in names]
    delta, new_m, new_v = {}, {}, {}
    updates, landed = _adamw_big(pick(W, ffn), pick(G, ffn), pick(M, ffn), pick(V, ffn), "adamw_ffn",
                                 [("scatter", in_bf16), ("everyone", small_slots)])
    halves = [_shard_sum(in_f32, landed[0], shard_arr, c_arr, "shard_sum_w_in"), G["w_out"]]
    full = dict(zip(late, _join_halves(halves)))
    full.update({n: G[n] for n in ffn})
    full.update(_unpack_small(_small_sum(G["small_early"], "early"), SMALL_EARLY, W, shard))
    late_total = _small_sum(landed[1], "late")
    full.update(_unpack_small(late_total, SMALL_LATE, W, shard))
    loss = late_total[-8, 0]

    lift = lambda d: [d[n][:, None, :] if n == "conv_w" else d[n] for n in SMALL]
    for names, (ds, nms, nvs) in ((ffn, updates),
                                  (late, _adamw_big(pick(W, late), pick(full, late), pick(M, late), pick(V, late), "adamw_mix")[0]),
                                  (SMALL, _adamw_small(lift(W), lift(full), lift(M), lift(V)))):
        for n, d, nm, nv in zip(names, ds, nms, nvs):
            delta[n], new_m[n], new_v[n] = d, nm, nv

    shaped = lambda d: [d[n].reshape(args[n].shape) for n in ORDER]
    return (loss, grad_x[None], *shaped(full), *shaped(delta), *shaped(new_m), *shaped(new_v))
```

```python
import functools

import jax
import jax.numpy as jnp
from jax import lax
from jax.experimental import pallas as pl
from jax.experimental.pallas import tpu as pltpu

F32 = jnp.float32
BF16 = jnp.bfloat16

RMS_EPS = 1e-6
NEG_INF = -1e30
N_HEADS = 8
HEAD_DIM = 64
ATTN_WIDTH = N_HEADS * HEAD_DIM
ATTN_SCALE = HEAD_DIM ** -0.5
ATTN_BLOCK = 128
DILATIONS = (1, 4, 16)
RESIDUES_PER_STEP = 4
CONSECUTIVE_BLOCKS = 4
POOL_WINDOWS = (2, 4, 8, 16)
POOL_GROUP = 128
POOL_WIDTH = POOL_GROUP * len(POOL_WINDOWS)
POOL_HALO = 16
CONV_WIDTH = 3
CONV_HALO = 8
N_SHARD = 4
LANES = 128
STAT_LANES = 16
STAT_WIDTH = N_HEADS * STAT_LANES

ADAM_LR = 0.001
ADAM_B1 = 0.9
ADAM_B2 = 0.999
ADAM_EPS = 1e-08
ADAM_WD = 0.01
ADAM_STEP = 10

VMEM_LIMIT = 60 * 1024 * 1024
MESH = pl.DeviceIdType.MESH
NT = (((1,), (1,)), ((), ()))
TN = (((0,), (0,)), ((), ()))


def _params(sem, vmem=None):
    return pltpu.CompilerParams(dimension_semantics=sem, vmem_limit_bytes=vmem)


def _const_spec(shape):
    zeros = (0,) * len(shape)
    return pl.BlockSpec(shape, lambda *_: zeros, pipeline_mode=pl.Buffered(1))


def _dot(a, b):
    return jnp.dot(a, b, preferred_element_type=F32)


def _dot_nt(a, b):
    return lax.dot_general(a, b, NT, preferred_element_type=F32)


def _dot_tn(a, b):
    return lax.dot_general(a, b, TN, preferred_element_type=F32)


def _rms_stats(x):
    r = lax.rsqrt(jnp.mean(x * x, axis=-1, keepdims=True) + RMS_EPS)
    return x * r, r


def _rms_bwd(dy, n, r, g):
    dg = jnp.sum(dy * n, axis=0, keepdims=True)
    dn = dy * g
    dx = r * (dn - n * jnp.mean(dn * n, axis=-1, keepdims=True))
    return dx, dg


def _gelu_tanh(g):
    k = 0.7978845608028654
    kc = k * 0.044715
    g2 = g * g
    t = jnp.tanh(g * (k + kc * g2))
    h = 0.5 * t + 0.5
    dh = (0.5 - 0.5 * (t * t)) * (k + (3.0 * kc) * g2)
    return g * h, h + g * dh


def _residue_shape(S, d, dtype, width=ATTN_WIDTH):
    return jax.ShapeDtypeStruct((S // d, d * width), dtype)


def _residue_spec(TM, d, width=ATTN_WIDTH):
    return pl.BlockSpec((TM // d, d * width), lambda i: (i, 0))


def _token_scratch(TM, width=ATTN_WIDTH):
    return [pltpu.VMEM((TM, LANES), F32)] * (width // LANES)


def _head_sum_matrix():
    r = lax.broadcasted_iota(jnp.int32, (ATTN_WIDTH, STAT_WIDTH), 0)
    c = lax.broadcasted_iota(jnp.int32, (ATTN_WIDTH, STAT_WIDTH), 1)
    return (r // HEAD_DIM == c // STAT_LANES).astype(BF16)


def _head_spread_matrix():
    c = lax.broadcasted_iota(jnp.int32, (STAT_WIDTH, ATTN_WIDTH), 0)
    r = lax.broadcasted_iota(jnp.int32, (STAT_WIDTH, ATTN_WIDTH), 1)
    return (c == (r // HEAD_DIM) * STAT_LANES).astype(BF16)


def _bf16_pieces(x, n):
    pieces = []
    for _ in range(n):
        p = x.astype(BF16)
        pieces.append(p)
        x = x - p.astype(F32)
    return pieces


def _put_tokens(dst_s, val):
    for cb, chunk in enumerate(dst_s):
        chunk[...] = val[:, cb * LANES:(cb + 1) * LANES]


def _get_tokens(src_s):
    return jnp.concatenate([chunk[...] for chunk in src_s], axis=1)


def _to_residue(val, src_s, out_ref, d, dtype):
    if d == 1:
        out_ref[...] = val.astype(dtype)
        return
    rows = src_s[0].shape[0]
    for r in range(d):
        for cb, chunk in enumerate(src_s):
            col = (r * len(src_s) + cb) * LANES
            out_ref[:, col:col + LANES] = chunk[pl.ds(r, rows // d, stride=d), :].astype(dtype)


def _from_residue(in_ref, dst_s, d):
    if d == 1:
        return in_ref[...].astype(F32)
    rows = dst_s[0].shape[0]
    for r in range(d):
        for cb, chunk in enumerate(dst_s):
            col = (r * len(dst_s) + cb) * LANES
            chunk[pl.ds(r, rows // d, stride=d), :] = in_ref[:, col:col + LANES].astype(F32)
    return _get_tokens(dst_s)


def _mix_in_fwd(x, g_pre, w_in, cargo=()):
    S, D = x.shape
    TM = 512
    nc = len(cargo)
    kinds = [kind for kind, _ in cargo]
    n_chunks = ATTN_WIDTH // LANES

    def body(x_ref, g_ref, w_ref, *refs):
        cargo_in, refs = refs[:nc], refs[nc:]
        qkv_refs, p_ref, h_ref = refs[:9], refs[9], refs[10]
        t_s = refs[11 + nc:11 + nc + n_chunks]
        cargo_refs = (kinds, cargo_in, refs[11:11 + nc], refs[11 + nc + n_chunks:])
        if nc:
            _cargo_start(*cargo_refs, pl.program_id(0) == 0)
        n, _ = _rms_stats(x_ref[...])
        hb = (n * g_ref[...]).astype(BF16)
        h_ref[...] = hb
        for a in range(3):
            res = _dot(hb, w_ref[a])
            if a == 0:
                res = res * ATTN_SCALE
            _put_tokens(t_s, res)
            for i, d in enumerate(DILATIONS):
                _to_residue(res, t_s, qkv_refs[3 * i + a], d, BF16)
        p_ref[...] = _dot(hb, w_ref[3])
        if nc:
            _cargo_finish(*cargo_refs, pl.program_id(0) == S // TM - 1)

    row = lambda w: pl.BlockSpec((TM, w), lambda i: (i, 0))
    arrays, cargo_specs, shapes, aliases, sems = _cargo_call(cargo, 3, 11)
    out = pl.pallas_call(
        body, name="mix_in_fwd", grid=(S // TM,),
        in_specs=[row(D), _const_spec((1, D)), _const_spec(w_in.shape)] + cargo_specs,
        out_specs=[_residue_spec(TM, d) for d in DILATIONS for _ in range(3)] + [row(POOL_WIDTH), row(D)] + cargo_specs,
        out_shape=[_residue_shape(S, d, BF16) for d in DILATIONS for _ in range(3)]
        + [jax.ShapeDtypeStruct((S, POOL_WIDTH), F32), jax.ShapeDtypeStruct((S, D), BF16)] + shapes,
        input_output_aliases=aliases,
        scratch_shapes=_token_scratch(TM) + sems,
        compiler_params=_params(("arbitrary",), VMEM_LIMIT),
    )(x, g_pre, w_in, *arrays)
    return [out[0:3], out[3:6], out[6:9]], out[9], out[10], out[11:]


def _band_mask(n):
    qi = lax.broadcasted_iota(jnp.int32, (ATTN_BLOCK, 2 * ATTN_BLOCK), 0)
    ki = lax.broadcasted_iota(jnp.int32, (ATTN_BLOCK, 2 * ATTN_BLOCK), 1)
    dist = qi + ATTN_BLOCK - ki
    return (dist >= 0) & (dist <= ATTN_BLOCK) & ((ki >= ATTN_BLOCK) | (n > 0))


def _first_head_lanes():
    return lax.broadcasted_iota(jnp.int32, (1, LANES), 1) < HEAD_DIM


def _stack_heads(pair, first):
    zero = jnp.zeros_like(pair)
    return jnp.concatenate([jnp.where(first, pair, zero), jnp.where(first, zero, pair)], axis=0)


def _unstack_heads(stacked, first):
    return jnp.where(first, stacked[:ATTN_BLOCK], stacked[ATTN_BLOCK:])


CARGO_COPIES = {"ici": 3, "d2d": 3, "scatter": 3, "swap": 1, "everyone": 7, "join": 1}
CARGO_IN_PLACE = ("ici", "d2d", "everyone", "join")


def _cargo_copies(kinds, ins, outs, send_sems, recv_sems, want_recvs=True):
    x, y, c, chips = _position()
    s = 2 * x + y
    me = 2 * s + c
    sibling = (x, y, 1 - c)
    sends, recvs = [], []

    def add(k, src, dst, landing, to):
        sends.append(_remote(src, dst, send_sems.at[k], recv_sems.at[k], to))
        if want_recvs:
            recvs.append(_remote(landing, landing, send_sems.at[k], recv_sems.at[k], to))

    k0 = 0
    for a, kind in enumerate(kinds):
        if kind == "swap":
            rows = ins[a].shape[1] // 2
            add(k0, ins[a].at[:, pl.ds((1 - c) * rows, rows), :], outs[a], outs[a], sibling)
        elif kind == "join":
            rows = outs[a].shape[0] // 2
            mine = outs[a].at[pl.ds(c * rows, rows), :]
            add(k0, mine, mine, outs[a].at[pl.ds((1 - c) * rows, rows), :], sibling)
        elif kind == "everyone":
            for m in range(1, 8):
                peer = (x ^ (m >> 2), y ^ ((m >> 1) & 1), c ^ (m & 1))
                add(k0 + m - 1, outs[a].at[me], outs[a].at[me], outs[a].at[4 * peer[0] + 2 * peer[1] + peer[2]], peer)
        else:
            for j, (px, py) in enumerate(chips):
                sj = 2 * px + py
                if kind == "scatter":
                    add(k0 + j, ins[a].at[sj], outs[a].at[j], outs[a].at[j], (px, py, c))
                    continue
                buf = outs[a]
                rows = buf.shape[1] // 2
                half = lambda shard, h: buf.at[shard, pl.ds(h * rows, rows), :]
                if kind == "ici":
                    add(k0 + j, half(s, c), half(s, c), half(sj, c), (px, py, c))
                else:
                    add(k0 + j, half(sj, c), half(sj, c), half(sj, 1 - c), sibling)
        k0 += CARGO_COPIES[kind]
    return sends, recvs


def _cargo_start(kinds, ins, outs, sems, first_step):
    @pl.when(first_step)
    def _():
        for cp in _cargo_copies(kinds, ins, outs, *sems, want_recvs=False)[0]:
            cp.start()


def _cargo_finish(kinds, ins, outs, sems, last_step):
    @pl.when(last_step)
    def _():
        sends, recvs = _cargo_copies(kinds, ins, outs, *sems)
        for cp in sends:
            cp.wait_send()
        for cp in recvs:
            cp.wait_recv()


def _cargo_call(cargo, n_in, n_out):
    arrays = [a for _, a in cargo]
    shapes = []
    for kind, a in cargo:
        if kind == "scatter":
            shape = (3,) + a.shape[1:]
        elif kind == "swap":
            shape = (a.shape[0], a.shape[1] // 2, a.shape[2])
        else:
            shape = a.shape
        shapes.append(jax.ShapeDtypeStruct(shape, a.dtype))
    aliases = {n_in + i: n_out + i for i, (kind, _) in enumerate(cargo) if kind in CARGO_IN_PLACE}
    n_sems = sum(CARGO_COPIES[kind] for kind, _ in cargo)
    sems = [pltpu.SemaphoreType.DMA((n_sems,))] * 2 if cargo else []
    return arrays, [ANY] * len(cargo), shapes, aliases, sems


def _attn_fwd(q, k, v, d, cargo=()):
    L = q.shape[0]
    group = min(d, RESIDUES_PER_STEP)
    width = group * ATTN_WIDTH
    qb = RESIDUES_PER_STEP // group
    steps = L // (qb * ATTN_BLOCK)
    nc = len(cargo)
    kinds = [kind for kind, _ in cargo]

    def body(*refs):
        q_ref, kp_ref, kc_ref, vp_ref, vc_ref = refs[:5]
        o_ref, lse_ref = refs[5 + nc:7 + nc]
        cargo_refs = (kinds, refs[5:5 + nc], refs[7 + nc:7 + 2 * nc], refs[7 + 2 * nc:])
        r, n = pl.program_id(0), pl.program_id(1)
        if nc:
            _cargo_start(*cargo_refs, (r == 0) & (n == 0))
        first = _first_head_lanes()
        for sub in range(qb):
            rows = slice(sub * ATTN_BLOCK, (sub + 1) * ATTN_BLOCK)
            valid = _band_mask(n if sub == 0 else 1)
            valid2 = jnp.concatenate([valid, valid], axis=0)
            for hp in range(width // LANES):
                sl = slice(hp * LANES, (hp + 1) * LANES)
                if sub == 0:
                    kk = jnp.concatenate([kp_ref[:, sl], kc_ref[rows, sl]], axis=0)
                    vv = jnp.concatenate([vp_ref[:, sl], vc_ref[rows, sl]], axis=0)
                else:
                    keys = slice((sub - 1) * ATTN_BLOCK, (sub + 1) * ATTN_BLOCK)
                    kk, vv = kc_ref[keys, sl], vc_ref[keys, sl]
                s = jnp.where(valid2, _dot_nt(_stack_heads(q_ref[rows, sl], first), kk), NEG_INF)
                m = jnp.max(s, axis=-1, keepdims=True)
                p = jnp.exp(s - m)
                den = jnp.sum(p, axis=-1, keepdims=True)
                o_ref[rows, sl] = _unstack_heads(_dot(p.astype(BF16), vv) / den, first).astype(BF16)
                lse = m + jnp.log(den)
                lane = 2 * hp * STAT_LANES
                for half in range(2):
                    lse_ref[rows, lane + half * STAT_LANES:lane + (half + 1) * STAT_LANES] = jnp.broadcast_to(
                        lse[half * ATTN_BLOCK:(half + 1) * ATTN_BLOCK], (ATTN_BLOCK, STAT_LANES))
        if nc:
            _cargo_finish(*cargo_refs, (r == d // group - 1) & (n == steps - 1))

    cur = pl.BlockSpec((qb * ATTN_BLOCK, width), lambda r, n: (n, r))
    prev = pl.BlockSpec((ATTN_BLOCK, width), lambda r, n: (jnp.maximum(n * qb - 1, 0), r))
    arrays, specs, shapes, aliases, sems = _cargo_call(cargo, 5, 2)
    out = pl.pallas_call(
        body, name=f"attn_fwd_d{d}", grid=(d // group, steps),
        in_specs=[cur, prev, cur, prev, cur] + specs,
        out_specs=[cur, pl.BlockSpec((qb * ATTN_BLOCK, group * STAT_WIDTH), lambda r, n: (n, r))] + specs,
        out_shape=[jax.ShapeDtypeStruct((L, d * ATTN_WIDTH), BF16), jax.ShapeDtypeStruct((L, d * STAT_WIDTH), F32)] + shapes,
        input_output_aliases=aliases, scratch_shapes=sems,
        compiler_params=_params(("arbitrary", "arbitrary")),
    )(q, k, k, v, v, *arrays)
    return out[0], out[1], out[2:]


def _attn_mix(outs, lses):
    S = outs[0].shape[0]
    TM = 512
    n = len(DILATIONS)

    def body(*refs):
        o_refs, l_refs, attn_ref, lse_refs = refs[:n], refs[n:2 * n], refs[2 * n], refs[2 * n + 1:3 * n + 1]
        t_s, c_s = refs[3 * n + 1:-1], refs[-1:]
        os = [_from_residue(o_refs[i], t_s, d) for i, d in enumerate(DILATIONS)]
        ls = [_from_residue(l_refs[i], c_s, d) for i, d in enumerate(DILATIONS)]
        m = jnp.maximum(jnp.maximum(ls[0], ls[1]), ls[2])
        es = [jnp.exp(l - m) for l in ls]
        den = es[0] + es[1] + es[2]
        spread = _head_spread_matrix()
        ws = [sum(_dot(p, spread) for p in _bf16_pieces(e / den, 2)) for e in es]
        attn_ref[...] = (ws[0] * os[0] + ws[1] * os[1] + ws[2] * os[2]).astype(BF16)
        lse = m + jnp.log(den)
        _put_tokens(c_s, lse)
        for i, d in enumerate(DILATIONS):
            _to_residue(lse, c_s, lse_refs[i], d, F32)

    specs = [_residue_spec(TM, d) for d in DILATIONS]
    stats = [_residue_spec(TM, d, STAT_WIDTH) for d in DILATIONS]
    out = pl.pallas_call(
        body, name="attn_mix", grid=(S // TM,),
        in_specs=specs + stats, out_specs=[specs[0]] + stats,
        out_shape=[jax.ShapeDtypeStruct((S, ATTN_WIDTH), BF16)]
        + [_residue_shape(S, d, F32, STAT_WIDTH) for d in DILATIONS],
        scratch_shapes=_token_scratch(TM) + _token_scratch(TM, STAT_WIDTH),
        compiler_params=_params(("parallel",)),
    )(*outs, *lses)
    return out[0], out[1:]


def _pool_counts(first_row, rows, w):
    t = first_row + lax.broadcasted_iota(jnp.int32, (rows, 1), 0)
    return jnp.minimum(t + 1, w).astype(F32)


def _trailing_sums(xe, w):
    s, k = xe, 1
    while k < w:
        s = s + pltpu.roll(s, k, 0)
        k *= 2
    return s


def _leading_sums(xe, w):
    rows = xe.shape[0]
    s, k = xe, 1
    while k < w:
        s = s + pltpu.roll(s, rows - k, 0)
        k *= 2
    return s


def _pooled_groups(halo, cur, first_row):
    TM = cur.shape[0]
    xe = jnp.concatenate([halo, cur], axis=0)
    out = []
    for g, w in enumerate(POOL_WINDOWS):
        a = xe[:, g * POOL_GROUP:(g + 1) * POOL_GROUP]
        s = _trailing_sums(a, w)[POOL_HALO:]
        out.append(s / _pool_counts(first_row, TM, w) - a[POOL_HALO:])
    return out


def _pool_fwd(pool_in, pool_w, pool_scale):
    S = pool_in.shape[0]
    TM = 512
    HB = TM // POOL_HALO

    def body(cur_ref, halo_ref, w_ref, sc_ref, y_ref):
        i = pl.program_id(0)
        halo = jnp.where(i > 0, halo_ref[...], 0.0)
        pooled = _pooled_groups(halo, cur_ref[...], i * TM)
        for g in range(len(POOL_WINDOWS)):
            sl = slice(g * POOL_GROUP, (g + 1) * POOL_GROUP)
            y = _dot(pooled[g].astype(BF16), w_ref[g].astype(BF16)) * sc_ref[:, sl]
            y_ref[:, sl] = y.astype(BF16)

    return pl.pallas_call(
        body, name="pool_fwd", grid=(S // TM,),
        in_specs=[pl.BlockSpec((TM, POOL_WIDTH), lambda i: (i, 0)),
                  pl.BlockSpec((POOL_HALO, POOL_WIDTH), lambda i: (jnp.maximum(i * HB - 1, 0), 0)),
                  _const_spec(pool_w.shape), _const_spec((1, POOL_WIDTH))],
        out_specs=pl.BlockSpec((TM, POOL_WIDTH), lambda i: (i, 0)),
        out_shape=jax.ShapeDtypeStruct((S, POOL_WIDTH), BF16),
        compiler_params=_params(("parallel",)),
    )(pool_in, pool_in, pool_w, pool_scale)


def _mix_out_fwd(attn, pool, w_out, x, g_post, g_ffn_pre):
    S, D = x.shape
    TM = 512

    def body(a_ref, p_ref, w_ref, x_ref, gp_ref, gf_ref, mixed_ref, x1_ref, h2_ref, cat_ref):
        ab = a_ref[...]
        cat_ref[:, :ATTN_WIDTH] = ab
        cat_ref[:, ATTN_WIDTH:] = p_ref[...]
        mixed = _dot(ab, w_ref[:ATTN_WIDTH, :]) + _dot(p_ref[...], w_ref[ATTN_WIDTH:, :])
        mixed_ref[...] = mixed.astype(BF16)
        n, _ = _rms_stats(mixed)
        x1 = x_ref[...] + n * gp_ref[...]
        x1_ref[...] = x1
        n2, _ = _rms_stats(x1)
        h2_ref[...] = (n2 * gf_ref[...]).astype(BF16)

    row = lambda w: pl.BlockSpec((TM, w), lambda i: (i, 0))
    return pl.pallas_call(
        body, name="mix_out_fwd", grid=(S // TM,),
        in_specs=[row(ATTN_WIDTH), row(POOL_WIDTH), _const_spec(w_out.shape), row(D),
                  _const_spec((1, D)), _const_spec((1, D))],
        out_specs=[row(D), row(D), row(D), row(D)],
        out_shape=[jax.ShapeDtypeStruct((S, D), BF16), jax.ShapeDtypeStruct((S, D), F32),
                   jax.ShapeDtypeStruct((S, D), BF16), jax.ShapeDtypeStruct((S, D), BF16)],
        compiler_params=_params(("parallel",), VMEM_LIMIT),
    )(attn, pool, w_out, x, g_post, g_ffn_pre)


def _ffn_fwd(h2, x1, target, w_up, w_down, conv_w, conv_b, g_post):
    S, D = x1.shape
    CW = w_up.shape[2]
    FF = 2 * CW
    TM = 256
    piece = 4 * LANES
    pieces = [(lo, min(lo + piece, CW)) for lo in range(0, CW, piece)]

    def body(h2_ref, x1_ref, t_ref, wu_ref, wd_ref, cw_ref, cb_ref, g_ref,
             yv_ref, dy_ref, df_ref, dc_ref, loss_ref, dg_ref, dcb_ref, dcw_ref,
             ue_s, dgate_s, dval_s):
        i = pl.program_id(0)

        @pl.when(i == 0)
        def _():
            loss_ref[...] = jnp.zeros_like(loss_ref)
            dg_ref[...] = jnp.zeros_like(dg_ref)
            dcb_ref[...] = jnp.zeros_like(dcb_ref)
            dcw_ref[...] = jnp.zeros_like(dcw_ref)
            ue_s[0:CONV_HALO, :] = jnp.zeros((CONV_HALO, 2 * FF), F32)

        @pl.when(i > 0)
        def _():
            ue_s[0:CONV_HALO, :] = ue_s[TM:TM + CONV_HALO, :]

        def shifted(cols, k):
            return pltpu.roll(ue_s[:, cols], k, 0)[CONV_HALO:]

        def conv(cols):
            return (cb_ref[:, cols] + cw_ref[2, :, cols] * ue_s[CONV_HALO:, cols]
                    + cw_ref[1, :, cols] * shifted(cols, 1) + cw_ref[0, :, cols] * shifted(cols, 2))

        hb = h2_ref[...]
        f = jnp.zeros((TM, D), F32)
        for j in range(2):
            jc = slice(j * CW, (j + 1) * CW)
            for half in range(2):
                blk = 2 * half + j
                cols = slice(blk * CW, (blk + 1) * CW)
                ue_s[CONV_HALO:, cols] = _dot(hb, wu_ref[blk])
            for lo, hi in pieces:
                pc = slice(j * CW + lo, j * CW + hi)
                gelu, dgelu = _gelu_tanh(conv(pc).astype(BF16))
                val = conv(slice(FF + j * CW + lo, FF + j * CW + hi)).astype(BF16)
                dgate_s[:, pc] = val * dgelu
                dval_s[:, pc] = gelu
                yv_ref[:, pc] = gelu * val
            f = f + _dot(yv_ref[:, jc], wd_ref[jc, :])

        n, r = _rms_stats(f)
        err = x1_ref[...] + n * g_ref[...] - t_ref[...]
        loss_ref[...] += 0.5 * jnp.sum(jnp.mean(err * err, axis=-1, keepdims=True), axis=0, keepdims=True)
        dy = err / D
        dy_ref[...] = dy
        df, dg = _rms_bwd(dy, n, r, g_ref[...])
        dg_ref[...] += dg
        dfb = df.astype(BF16)
        df_ref[...] = dfb

        for j in range(2):
            jc = slice(j * CW, (j + 1) * CW)
            dyv = _dot_nt(dfb, wd_ref[jc, :])
            for lo, hi in pieces:
                pc = slice(j * CW + lo, j * CW + hi)
                for half, scale_s in ((0, dgate_s), (1, dval_s)):
                    cols = slice(half * FF + j * CW + lo, half * FF + j * CW + hi)
                    dcv = dyv[:, lo:hi] * scale_s[:, pc].astype(F32)
                    dc_ref[:, cols] = dcv.astype(BF16)
                    dcb_ref[:, cols] += jnp.sum(dcv, axis=0, keepdims=True)
                    dcw_ref[2, :, cols] += jnp.sum(dcv * ue_s[CONV_HALO:, cols], axis=0, keepdims=True)
                    dcw_ref[1, :, cols] += jnp.sum(dcv * shifted(cols, 1), axis=0, keepdims=True)
                    dcw_ref[0, :, cols] += jnp.sum(dcv * shifted(cols, 2), axis=0, keepdims=True)

    row = lambda w: pl.BlockSpec((TM, w), lambda i: (i, 0))
    acc = lambda shape: pl.BlockSpec(shape, lambda i: (0,) * len(shape))
    return pl.pallas_call(
        body, name="ffn_fwd", grid=(S // TM,),
        in_specs=[row(D), row(D), row(D), _const_spec(w_up.shape), _const_spec(w_down.shape),
                  _const_spec(conv_w.shape), _const_spec((1, 2 * FF)), _const_spec((1, D))],
        out_specs=[row(FF), row(D), row(D), row(2 * FF),
                   acc((1, 1)), acc((1, D)), acc((1, 2 * FF)), acc((CONV_WIDTH, 1, 2 * FF))],
        out_shape=[jax.ShapeDtypeStruct((S, FF), BF16),
                   jax.ShapeDtypeStruct((S, D), F32), jax.ShapeDtypeStruct((S, D), BF16),
                   jax.ShapeDtypeStruct((S, 2 * FF), BF16),
                   jax.ShapeDtypeStruct((1, 1), F32), jax.ShapeDtypeStruct((1, D), F32),
                   jax.ShapeDtypeStruct((1, 2 * FF), F32), jax.ShapeDtypeStruct((CONV_WIDTH, 1, 2 * FF), F32)],
        scratch_shapes=[pltpu.VMEM((TM + CONV_HALO, 2 * FF), F32), pltpu.VMEM((TM, FF), BF16),
                        pltpu.VMEM((TM, FF), BF16)],
        compiler_params=_params(("arbitrary",), VMEM_LIMIT),
    )(h2, x1, target, w_up, w_down, conv_w, conv_b, g_post)


def _ffn_bwd(dc, conv_w, w_up, x1, g_ffn_pre, dy):
    S, D = x1.shape
    CW = w_up.shape[2]
    F2 = 4 * CW
    TM = 256
    HB = TM // CONV_HALO
    last = S // CONV_HALO - 1
    n_tiles = S // TM

    def body(dc_ref, halo_ref, cw_ref, wu_ref, x1_ref, g_ref, dy_ref, du_ref, dx1_ref, dg_ref):
        i = pl.program_id(0)

        @pl.when(i == 0)
        def _():
            dg_ref[...] = jnp.zeros_like(dg_ref)

        keep = i < n_tiles - 1
        dh2 = jnp.zeros((TM, D), F32)
        for blk in range(N_SHARD):
            cols = slice(blk * CW, (blk + 1) * CW)
            halo = jnp.where(keep, halo_ref[:, cols].astype(F32), 0.0)
            dce = jnp.concatenate([dc_ref[:, cols].astype(F32), halo], axis=0)
            rows = TM + CONV_HALO
            du = (cw_ref[2, :, cols] * dce[:TM]
                  + cw_ref[1, :, cols] * pltpu.roll(dce, rows - 1, 0)[:TM]
                  + cw_ref[0, :, cols] * pltpu.roll(dce, rows - 2, 0)[:TM])
            dub = du.astype(BF16)
            du_ref[:, cols] = dub
            dh2 = dh2 + _dot_nt(dub, wu_ref[blk])
        n2, r2 = _rms_stats(x1_ref[...])
        dx, dg = _rms_bwd(dh2, n2, r2, g_ref[...])
        dg_ref[...] += dg
        dx1_ref[...] = (dy_ref[...] + dx).astype(BF16)

    row = lambda w: pl.BlockSpec((TM, w), lambda i: (i, 0))
    return pl.pallas_call(
        body, name="ffn_bwd", grid=(S // TM,),
        in_specs=[row(F2), pl.BlockSpec((CONV_HALO, F2), lambda i: (jnp.minimum((i + 1) * HB, last), 0)),
                  _const_spec(conv_w.shape), _const_spec(w_up.shape), row(D), _const_spec((1, D)), row(D)],
        out_specs=[row(F2), row(D), pl.BlockSpec((1, D), lambda i: (0, 0))],
        out_shape=[jax.ShapeDtypeStruct((S, F2), BF16), jax.ShapeDtypeStruct((S, D), BF16),
                   jax.ShapeDtypeStruct((1, D), F32)],
        compiler_params=_params(("arbitrary",), VMEM_LIMIT),
    )(dc, dc, conv_w, w_up, x1, g_ffn_pre, dy)


def _matmul_tn(a, b, n_blocks, name):
    S, M = a.shape
    N = b.shape[1]
    tn = N // n_blocks
    tm = M if M <= 1024 else M // 2
    tk = 2048
    nk = S // tk

    def body(a_ref, b_ref, o_ref):
        @pl.when(pl.program_id(2) == 0)
        def _():
            o_ref[...] = jnp.zeros_like(o_ref)
        o_ref[0] += _dot_tn(a_ref[...], b_ref[...])

    return pl.pallas_call(
        body, name=name, grid=(M // tm, n_blocks, nk),
        in_specs=[pl.BlockSpec((tk, tm), lambda i, j, k: (k, i)), pl.BlockSpec((tk, tn), lambda i, j, k: (k, j))],
        out_specs=pl.BlockSpec((1, tm, tn), lambda i, j, k: (j, i, 0)),
        out_shape=jax.ShapeDtypeStruct((n_blocks, M, tn), F32),
        compiler_params=_params(("parallel", "parallel", "arbitrary"), VMEM_LIMIT),
    )(a, b)


def _mix_out_bwd(dx1, mixed, g_post, w_out, attn, cargo=()):
    S, D = dx1.shape
    TM = 512
    nd = len(DILATIONS)
    nc = len(cargo)
    kinds = [kind for kind, _ in cargo]
    n_chunks = ATTN_WIDTH // LANES

    def body(*refs):
        dx_ref, m_ref, g_ref, w_ref, a_ref = refs[:5]
        dm_ref, dp_ref, dg_ref = refs[5 + nc:8 + nc]
        da_refs, dl_refs = refs[8 + nc:8 + nc + nd], refs[8 + nc + nd:8 + nc + 2 * nd]
        n_out = 8 + nc + 2 * nd
        t_s = refs[n_out + nc:n_out + nc + n_chunks]
        c_s = refs[n_out + nc + n_chunks:n_out + nc + n_chunks + 1]
        cargo_refs = (kinds, refs[5:5 + nc], refs[n_out:n_out + nc], refs[n_out + nc + n_chunks + 1:])
        if nc:
            _cargo_start(*cargo_refs, pl.program_id(0) == 0)

        @pl.when(pl.program_id(0) == 0)
        def _():
            dg_ref[...] = jnp.zeros_like(dg_ref)

        n, r = _rms_stats(m_ref[...].astype(F32))
        dm, dg = _rms_bwd(dx_ref[...].astype(F32), n, r, g_ref[...])
        dg_ref[...] += dg
        dmb = dm.astype(BF16)
        dm_ref[...] = dmb
        da = _dot_nt(dmb, w_ref[:ATTN_WIDTH, :])
        _put_tokens(t_s, da)
        for i, d in enumerate(DILATIONS):
            _to_residue(da, t_s, da_refs[i], d, BF16)
        dp_ref[...] = _dot_nt(dmb, w_ref[ATTN_WIDTH:, :]).astype(BF16)
        gather = _head_sum_matrix()
        delta = sum(_dot(p, gather) for p in _bf16_pieces(da * a_ref[...].astype(F32), 2))
        _put_tokens(c_s, delta)
        for i, d in enumerate(DILATIONS):
            _to_residue(delta, c_s, dl_refs[i], d, F32)
        if nc:
            _cargo_finish(*cargo_refs, pl.program_id(0) == S // TM - 1)

    row = lambda w: pl.BlockSpec((TM, w), lambda i: (i, 0))
    specs = [_residue_spec(TM, d) for d in DILATIONS]
    arrays, cargo_specs, shapes, aliases, sems = _cargo_call(cargo, 5, 3 + 2 * nd)
    out = pl.pallas_call(
        body, name="mix_out_bwd", grid=(S // TM,),
        in_specs=[row(D), row(D), _const_spec((1, D)), _const_spec(w_out.shape), row(ATTN_WIDTH)] + cargo_specs,
        out_specs=[row(D), row(POOL_WIDTH), pl.BlockSpec((1, D), lambda i: (0, 0))] + specs
        + [_residue_spec(TM, d, STAT_WIDTH) for d in DILATIONS] + cargo_specs,
        out_shape=[jax.ShapeDtypeStruct((S, D), BF16), jax.ShapeDtypeStruct((S, POOL_WIDTH), BF16),
                   jax.ShapeDtypeStruct((1, D), F32)]
        + [_residue_shape(S, d, BF16) for d in DILATIONS]
        + [_residue_shape(S, d, F32, STAT_WIDTH) for d in DILATIONS] + shapes,
        input_output_aliases=aliases,
        scratch_shapes=_token_scratch(TM) + _token_scratch(TM, STAT_WIDTH) + sems,
        compiler_params=_params(("arbitrary",), VMEM_LIMIT),
    )(dx1, mixed, g_post, w_out, attn, *arrays)
    return out[0], out[1], out[2], out[3:3 + nd], out[3 + nd:3 + 2 * nd], out[3 + 2 * nd:]


def _pool_bwd(pool_in, d_pool, pool_w, pool_scale):
    S = pool_in.shape[0]
    TM = 512
    HB = TM // POOL_HALO
    last = S // POOL_HALO - 1
    G = len(POOL_WINDOWS)

    def body(cur_ref, halo_ref, dcur_ref, dnext_ref, w_ref, sc_ref, dxin_ref, dw_ref, dsc_ref):
        i = pl.program_id(0)

        @pl.when(i == 0)
        def _():
            dw_ref[...] = jnp.zeros_like(dw_ref)
            dsc_ref[...] = jnp.zeros_like(dsc_ref)

        halo = jnp.where(i > 0, halo_ref[...], 0.0)
        pooled = _pooled_groups(halo, cur_ref[...], i * TM)
        dnext = jnp.where(i < S // TM - 1, dnext_ref[...].astype(F32), 0.0)
        dye = jnp.concatenate([dcur_ref[...].astype(F32), dnext], axis=0)
        for g, w in enumerate(POOL_WINDOWS):
            sl = slice(g * POOL_GROUP, (g + 1) * POOL_GROUP)
            wg = w_ref[g].astype(BF16)
            pb = pooled[g].astype(BF16)
            dsc_ref[:, sl] += jnp.sum(dye[:TM, sl] * _dot(pb, wg), axis=0, keepdims=True)
            dpre = (dye[:, sl] * sc_ref[:, sl]).astype(BF16)
            dw_ref[g] += _dot_tn(pb, dpre[:TM])
            dpooled = _dot_nt(dpre, wg)
            z = dpooled / _pool_counts(i * TM, TM + POOL_HALO, w)
            dxin_ref[:, sl] = (_leading_sums(z, w)[:TM] - dpooled[:TM]).astype(BF16)

    row = pl.BlockSpec((TM, POOL_WIDTH), lambda i: (i, 0))
    return pl.pallas_call(
        body, name="pool_bwd", grid=(S // TM,),
        in_specs=[row, pl.BlockSpec((POOL_HALO, POOL_WIDTH), lambda i: (jnp.maximum(i * HB - 1, 0), 0)),
                  row, pl.BlockSpec((POOL_HALO, POOL_WIDTH), lambda i: (jnp.minimum((i + 1) * HB, last), 0)),
                  _const_spec(pool_w.shape), _const_spec((1, POOL_WIDTH))],
        out_specs=[row, pl.BlockSpec((G, POOL_GROUP, POOL_GROUP), lambda i: (0, 0, 0)),
                   pl.BlockSpec((1, POOL_WIDTH), lambda i: (0, 0))],
        out_shape=[jax.ShapeDtypeStruct((S, POOL_WIDTH), BF16), jax.ShapeDtypeStruct((G, POOL_GROUP, POOL_GROUP), F32),
                   jax.ShapeDtypeStruct((1, POOL_WIDTH), F32)],
        compiler_params=_params(("arbitrary",)),
    )(pool_in, pool_in, d_pool, d_pool, pool_w, pool_scale)


def _attn_bwd(q, k, v, d_attn, lse, delta, d, cargo=()):
    L = q.shape[0]
    nb = L // ATTN_BLOCK
    group = min(d, RESIDUES_PER_STEP)
    width = group * ATTN_WIDTH
    nc = len(cargo)
    kinds = [kind for kind, _ in cargo]

    def body(*refs):
        q_ref, kp_ref, kc_ref, vp_ref, vc_ref, do_ref, lse_ref, dl_ref = refs[:8]
        dq_ref, dk_ref, dv_ref = refs[8 + nc:11 + nc]
        ck_s, cv_s = refs[11 + 2 * nc:13 + 2 * nc]
        cargo_refs = (kinds, refs[8:8 + nc], refs[11 + nc:11 + 2 * nc], refs[13 + 2 * nc:])
        r, n = pl.program_id(0), pl.program_id(1)
        if nc:
            _cargo_start(*cargo_refs, (r == 0) & (n == 0))

        @pl.when(n == 0)
        def _():
            ck_s[...] = jnp.zeros_like(ck_s)
            cv_s[...] = jnp.zeros_like(cv_s)

        @pl.when(n < nb)
        def _():
            valid = _band_mask(n)
            valid2 = jnp.concatenate([valid, valid], axis=0)
            first = _first_head_lanes()

            def stacked_column(ref, hp):
                lane = 2 * hp * STAT_LANES
                return jnp.concatenate([ref[:, lane:lane + 1], ref[:, lane + STAT_LANES:lane + STAT_LANES + 1]], axis=0)

            for hp in range(width // LANES):
                sl = slice(hp * LANES, (hp + 1) * LANES)
                qq = _stack_heads(q_ref[:, sl], first)
                dd = _stack_heads(do_ref[:, sl], first)
                kk = jnp.concatenate([kp_ref[:, sl], kc_ref[:, sl]], axis=0)
                vv = jnp.concatenate([vp_ref[:, sl], vc_ref[:, sl]], axis=0)
                s = _dot_nt(qq, kk)
                p = jnp.where(valid2, jnp.exp(s - stacked_column(lse_ref, hp)), 0.0)
                dp = _dot_nt(dd, vv)
                ds = (p * (dp - stacked_column(dl_ref, hp))).astype(BF16)
                dq_ref[:, sl] = (_unstack_heads(_dot(ds, kk), first) * ATTN_SCALE).astype(BF16)
                dk = _dot_tn(ds, qq)
                dv = _dot_tn(p.astype(BF16), dd)
                dk_ref[:, sl] = (ck_s[:, sl] + dk[:ATTN_BLOCK]).astype(BF16)
                dv_ref[:, sl] = (cv_s[:, sl] + dv[:ATTN_BLOCK]).astype(BF16)
                ck_s[:, sl] = dk[ATTN_BLOCK:]
                cv_s[:, sl] = dv[ATTN_BLOCK:]

        @pl.when(n == nb)
        def _():
            dk_ref[...] = ck_s[...].astype(BF16)
            dv_ref[...] = cv_s[...].astype(BF16)

        if nc:
            _cargo_finish(*cargo_refs, (r == d // group - 1) & (n == nb))

    blk = (ATTN_BLOCK, width)
    cur = pl.BlockSpec(blk, lambda r, n: (jnp.minimum(n, nb - 1), r))
    stat = pl.BlockSpec((ATTN_BLOCK, group * STAT_WIDTH), lambda r, n: (jnp.minimum(n, nb - 1), r))
    prev = pl.BlockSpec(blk, lambda r, n: (jnp.maximum(jnp.minimum(n, nb - 1) - 1, 0), r))
    done = pl.BlockSpec(blk, lambda r, n: (jnp.maximum(n - 1, 0), r))
    arrays, specs, shapes, aliases, sems = _cargo_call(cargo, 8, 3)
    out = pl.pallas_call(
        body, name=f"attn_bwd_d{d}", grid=(d // group, nb + 1),
        in_specs=[cur, prev, cur, prev, cur, cur, stat, stat] + specs, out_specs=[cur, done, done] + specs,
        out_shape=[jax.ShapeDtypeStruct((L, d * ATTN_WIDTH), BF16)] * 3 + shapes,
        input_output_aliases=aliases,
        scratch_shapes=[pltpu.VMEM(blk, F32), pltpu.VMEM(blk, F32)] + sems,
        compiler_params=_params(("arbitrary", "arbitrary")),
    )(q, k, k, v, v, d_attn, lse, delta, *arrays)
    return out[:3], out[3:]


def _attn_bwd_consecutive(q, k, v, d_attn, lse, delta, d, cargo=()):
    L = q.shape[0]
    qb = CONSECUTIVE_BLOCKS
    steps = L // (qb * ATTN_BLOCK)
    nc = len(cargo)
    kinds = [kind for kind, _ in cargo]

    def body(*refs):
        q_ref, kp_ref, kc_ref, vp_ref, vc_ref, do_ref, lse_ref, dl_ref = refs[:8]
        dq_ref, dk_ref, dv_ref, ek_ref, ev_ref = refs[8 + nc:13 + nc]
        cargo_refs = (kinds, refs[8:8 + nc], refs[13 + nc:13 + 2 * nc], refs[13 + 2 * nc:])
        r, n = pl.program_id(0), pl.program_id(1)
        if nc:
            _cargo_start(*cargo_refs, (r == 0) & (n == 0))
        first = _first_head_lanes()
        for hp in range(ATTN_WIDTH // LANES):
            sl = slice(hp * LANES, (hp + 1) * LANES)
            for sub in range(qb):
                rows = slice(sub * ATTN_BLOCK, (sub + 1) * ATTN_BLOCK)
                valid = _band_mask(n if sub == 0 else 1)
                valid2 = jnp.concatenate([valid, valid], axis=0)
                if sub == 0:
                    kk = jnp.concatenate([kp_ref[:, sl], kc_ref[rows, sl]], axis=0)
                    vv = jnp.concatenate([vp_ref[:, sl], vc_ref[rows, sl]], axis=0)
                else:
                    keys = slice((sub - 1) * ATTN_BLOCK, (sub + 1) * ATTN_BLOCK)
                    kk, vv = kc_ref[keys, sl], vc_ref[keys, sl]
                qq = _stack_heads(q_ref[rows, sl], first)
                dd = _stack_heads(do_ref[rows, sl], first)
                lane = 2 * hp * STAT_LANES
                column = lambda ref: jnp.concatenate(
                    [ref[rows, lane:lane + 1], ref[rows, lane + STAT_LANES:lane + STAT_LANES + 1]], axis=0)
                p = jnp.where(valid2, jnp.exp(_dot_nt(qq, kk) - column(lse_ref)), 0.0)
                ds = (p * (_dot_nt(dd, vv) - column(dl_ref))).astype(BF16)
                dq_ref[rows, sl] = (_unstack_heads(_dot(ds, kk), first) * ATTN_SCALE).astype(BF16)
                dk = _dot_tn(ds, qq)
                dv = _dot_tn(p.astype(BF16), dd)
                if sub == 0:
                    ek_ref[:, sl] = dk[:ATTN_BLOCK].astype(BF16)
                    ev_ref[:, sl] = dv[:ATTN_BLOCK].astype(BF16)
                else:
                    before = slice((sub - 1) * ATTN_BLOCK, sub * ATTN_BLOCK)
                    dk_ref[before, sl] = (carry_k + dk[:ATTN_BLOCK]).astype(BF16)
                    dv_ref[before, sl] = (carry_v + dv[:ATTN_BLOCK]).astype(BF16)
                carry_k, carry_v = dk[ATTN_BLOCK:], dv[ATTN_BLOCK:]
            dk_ref[rows, sl] = carry_k.astype(BF16)
            dv_ref[rows, sl] = carry_v.astype(BF16)
        if nc:
            _cargo_finish(*cargo_refs, (r == d - 1) & (n == steps - 1))

    cur = pl.BlockSpec((qb * ATTN_BLOCK, ATTN_WIDTH), lambda r, n: (n, r))
    prev = pl.BlockSpec((ATTN_BLOCK, ATTN_WIDTH), lambda r, n: (jnp.maximum(n * qb - 1, 0), r))
    edge = pl.BlockSpec((ATTN_BLOCK, ATTN_WIDTH), lambda r, n: (n, r))
    stat = pl.BlockSpec((qb * ATTN_BLOCK, STAT_WIDTH), lambda r, n: (n, r))
    arrays, specs, shapes, aliases, sems = _cargo_call(cargo, 8, 5)
    out = pl.pallas_call(
        body, name=f"attn_bwd_d{d}", grid=(d, steps),
        in_specs=[cur, prev, cur, prev, cur, cur, stat, stat] + specs, out_specs=[cur, cur, cur, edge, edge] + specs,
        out_shape=[jax.ShapeDtypeStruct((L, d * ATTN_WIDTH), BF16)] * 3
        + [jax.ShapeDtypeStruct((steps * ATTN_BLOCK, d * ATTN_WIDTH), BF16)] * 2 + shapes,
        input_output_aliases=aliases, scratch_shapes=sems,
        compiler_params=_params(("arbitrary", "arbitrary")),
    )(q, k, k, v, v, d_attn, lse, delta, *arrays)
    return out[:3], out[3:5], out[5:]


def _mix_in_bwd(dqkv, edges, d_pool_in, w_in, x, g_pre, dx1):
    S, D = x.shape
    TM = CONSECUTIVE_BLOCKS * ATTN_BLOCK
    nd = len(DILATIONS)
    n_tiles = S // TM

    def body(*refs):
        g_refs = refs[:3 * nd]
        e_refs = (None,) + refs[3 * nd:3 * nd + 2]
        dpi_ref, w_ref, x_ref, g_ref, dx1_ref, dproj_ref, gx_ref, dg_ref = refs[3 * nd + 2:3 * nd + 10]
        t_s = refs[3 * nd + 10:]

        @pl.when(pl.program_id(0) == 0)
        def _():
            dg_ref[...] = jnp.zeros_like(dg_ref)

        dh = jnp.zeros((TM, D), F32)
        for a in range(4):
            if a < 3:
                tot = g_refs[a][...].astype(F32)
                if a > 0:
                    late = jnp.where(pl.program_id(0) < n_tiles - 1, e_refs[a][...].astype(F32), 0.0)
                    tot = jnp.concatenate([tot[:TM - ATTN_BLOCK], tot[TM - ATTN_BLOCK:] + late], axis=0)
                for i, d in enumerate(DILATIONS[1:]):
                    tot = tot + _from_residue(g_refs[3 * (i + 1) + a], t_s, d)
                db = tot.astype(BF16)
            else:
                db = dpi_ref[...]
            dproj_ref[:, a * ATTN_WIDTH:(a + 1) * ATTN_WIDTH] = db
            dh = dh + _dot_nt(db, w_ref[a])
        n, r = _rms_stats(x_ref[...])
        dx, dg = _rms_bwd(dh, n, r, g_ref[...])
        dg_ref[...] += dg
        gx_ref[...] = dx1_ref[...].astype(F32) + dx

    row = lambda w: pl.BlockSpec((TM, w), lambda i: (i, 0))
    edge = pl.BlockSpec((ATTN_BLOCK, ATTN_WIDTH), lambda i: (jnp.minimum(i + 1, n_tiles - 1), 0))
    return pl.pallas_call(
        body, name="mix_in_bwd", grid=(S // TM,),
        in_specs=[_residue_spec(TM, d) for d in DILATIONS for _ in range(3)] + [edge, edge]
        + [row(POOL_WIDTH), _const_spec(w_in.shape), row(D), _const_spec((1, D)), row(D)],
        out_specs=[row(4 * ATTN_WIDTH), row(D), pl.BlockSpec((1, D), lambda i: (0, 0))],
        out_shape=[jax.ShapeDtypeStruct((S, 4 * ATTN_WIDTH), BF16), jax.ShapeDtypeStruct((S, D), F32),
                   jax.ShapeDtypeStruct((1, D), F32)],
        scratch_shapes=_token_scratch(TM),
        compiler_params=_params(("arbitrary",), VMEM_LIMIT),
    )(*[g for gs in dqkv for g in gs], *edges, d_pool_in, w_in, x, g_pre, dx1)


SMALL_EARLY = ("pool_w", "pool_scale", "g_mix_post", "g_ffn_pre", "conv_b", "g_ffn_post", "conv_w")
SMALL_LATE = ("g_mix_pre",)


def _pack_small(grads, names):
    parts = []
    for n in names:
        g = grads[n]
        if n == "conv_w":
            g = g.reshape(CONV_WIDTH, N_SHARD, -1).transpose(1, 0, 2)
        parts.append(g.reshape(-1, LANES))
    return jnp.concatenate(parts, axis=0) if len(parts) > 1 else parts[0]


def _unpack_small(packed, names, like, shard):
    out, row = {}, 0
    for n in names:
        size = like[n].size * (N_SHARD if n == "conv_w" else 1)
        g = packed[row:row + size // LANES]
        row += size // LANES
        if n == "conv_w":
            g = lax.dynamic_slice_in_dim(g.reshape((N_SHARD,) + like[n].shape), shard, 1, axis=0)[0]
        out[n] = g.reshape(like[n].shape)
    return out


def _local_step(x, target, g_mix_pre, w_in, pool_w, pool_scale, w_out, g_mix_post, g_ffn_pre,
                w_up, conv_w, conv_b, w_down, g_ffn_post, mesh_pos=None):
    on_mesh = mesh_pos is not None
    D = x.shape[1]
    CW = w_up.shape[2]
    qkv, pool_in, h1, got = _mix_in_fwd(x, g_mix_pre, w_in, [("ici", w_up)] if on_mesh else ())
    w_up = got[0] if on_mesh else w_up
    o1, l1, got = _attn_fwd(*qkv[0], 1, [("d2d", w_up), ("ici", w_out)] if on_mesh else ())
    w_up, w_out = got if on_mesh else (w_up, w_out)
    o4, l4, got = _attn_fwd(*qkv[1], 4, [("d2d", w_out), ("ici", w_down)] if on_mesh else ())
    w_out, w_down = got if on_mesh else (w_out, w_down)
    o16, l16, got = _attn_fwd(*qkv[2], 16, [("d2d", w_down)] if on_mesh else ())
    w_down = got[0] if on_mesh else w_down
    w_out = w_out.reshape(D, D)
    w_down = w_down.reshape(2 * CW, D)
    attn, lse = _attn_mix((o1, o4, o16), (l1, l4, l16))
    pool = _pool_fwd(pool_in, pool_w, pool_scale)
    mixed, x1, h2, cat = _mix_out_fwd(attn, pool, w_out, x, g_mix_post, g_ffn_pre)

    yv, dy, df, dc, loss, d_g_ffn_post, d_conv_b, d_conv_w = _ffn_fwd(
        h2, x1, target, w_up, w_down, conv_w, conv_b, g_ffn_post)
    du, dx1, d_g_ffn_pre = _ffn_bwd(dc, conv_w, w_up, x1, g_ffn_pre, dy)
    d_w_up = _matmul_tn(h2, du, N_SHARD, "grad_w_up")
    d_w_down = _matmul_tn(yv, df, 1, "grad_w_down")[0].reshape(N_SHARD, CW // 2, D)
    swap = [("swap", d_w_up), ("swap", d_w_down)] if on_mesh else ()
    d_mixed, d_pool, d_g_mix_post, d_attn, delta, from_sibling = _mix_out_bwd(dx1, mixed, g_mix_post, w_out, attn, swap)
    d_w_out = _matmul_tn(cat, d_mixed, 1, "grad_w_out")[0].reshape(N_SHARD, D // N_SHARD, D)
    d_pool_in, d_pool_w, d_pool_scale = _pool_bwd(pool_in, d_pool, pool_w, pool_scale)
    grads = dict(pool_w=d_pool_w, pool_scale=d_pool_scale, w_out=d_w_out, g_mix_post=d_g_mix_post,
                 g_ffn_pre=d_g_ffn_pre, w_up=d_w_up, conv_w=d_conv_w, conv_b=d_conv_b, w_down=d_w_down,
                 g_ffn_post=d_g_ffn_post)
    cargo = [(), (), ()]
    if on_mesh:
        c_arr, device, shard_arr = mesh_pos
        up_f32, up_bf16 = _pair_sum(d_w_up, from_sibling[0], c_arr, "pair_sum_w_up")
        down_f32, down_bf16 = _pair_sum(d_w_down, from_sibling[1], c_arr, "pair_sum_w_down")
        early = _pack_small(grads, SMALL_EARLY)
        early_slots = lax.dynamic_update_index_in_dim(jnp.zeros((8,) + early.shape, F32), early, device, 0)
        cargo = [[("scatter", down_bf16), ("everyone", early_slots), ("swap", d_w_out)], [("scatter", up_bf16)], []]

    dqkv1, edges, landed1 = _attn_bwd_consecutive(*qkv[0], d_attn[0], lse[0], delta[0], 1, cargo[0])
    dqkv4, landed4 = _attn_bwd(*qkv[1], d_attn[1], lse[1], delta[1], 4, cargo[1])
    if on_mesh:
        halves = [_shard_sum(up_f32, landed4[0], shard_arr, c_arr, "shard_sum_w_up"),
                  _shard_sum(down_f32, landed1[0], shard_arr, c_arr, "shard_sum_w_down")]
        out_f32, out_bf16 = _pair_sum(d_w_out, landed1[2], c_arr, "pair_sum_w_out")
        cargo[2] = cargo[2] + [("join", h) for h in halves] + [("scatter", out_bf16)]
    if qkv[2][0].shape[0] == CONSECUTIVE_BLOCKS * ATTN_BLOCK:
        dqkv16, _, landed16 = _attn_bwd_consecutive(*qkv[2], d_attn[2], lse[2], delta[2], 16, cargo[2])
    else:
        dqkv16, landed16 = _attn_bwd(*qkv[2], d_attn[2], lse[2], delta[2], 16, cargo[2])
    if on_mesh:
        grads.update(small_early=landed1[1], w_up=landed16[0], w_down=landed16[1],
                     w_out=_shard_sum(out_f32, landed16[2], shard_arr, c_arr, "shard_sum_w_out"))
    d_proj, grad_x, grads["g_mix_pre"] = _mix_in_bwd((dqkv1, dqkv4, dqkv16), edges, d_pool_in, w_in, x, g_mix_pre, dx1)
    grads["w_in"] = _matmul_tn(h1, d_proj, N_SHARD, "grad_w_in")
    return loss, grad_x, grads


ANY = pl.BlockSpec(memory_space=pl.ANY)


def _position():
    x, y, c = lax.axis_index("x"), lax.axis_index("y"), lax.axis_index("c")
    chips = [(1 - x, y), (x, 1 - y), (1 - x, 1 - y)]
    return x, y, c, chips


def _remote(src, dst, send_sem, recv_sem, to):
    return pltpu.make_async_remote_copy(src_ref=src, dst_ref=dst, send_sem=send_sem, recv_sem=recv_sem,
                                        device_id=to, device_id_type=MESH)


def _cast_bf16(w, shard_arr, name):
    R, C = w.shape
    tr = R // 2

    def body(s_ref, w_ref, o_ref):
        o_ref[0] = w_ref[...].astype(BF16)

    return pl.pallas_call(
        body, name=name,
        grid_spec=pltpu.PrefetchScalarGridSpec(
            num_scalar_prefetch=1, grid=(2,),
            in_specs=[pl.BlockSpec((tr, C), lambda i, s_ref: (i, 0))],
            out_specs=pl.BlockSpec((1, tr, C), lambda i, s_ref: (s_ref[0], i, 0))),
        out_shape=jax.ShapeDtypeStruct((N_SHARD, R, C), BF16),
        compiler_params=_params(("parallel",)))(shard_arr, w)


def _gather_weights(bufs, others):
    n = len(bufs) - 1
    m = len(others)

    def body(*refs):
        w_refs = refs[n + 1:n + 1 + m]
        outs, cw_out = refs[n + 1 + m:2 * n + 1 + m], refs[2 * n + 1 + m]
        cast_outs = refs[2 * n + 2 + m:2 * n + 2 + 2 * m]
        ici_send, ici_recv, d2d_send, d2d_recv, local_sem = refs[2 * n + 2 + 2 * m:2 * n + 7 + 2 * m]
        f32_s, bf16_s = refs[2 * n + 7 + 2 * m:2 * n + 7 + 3 * m], refs[2 * n + 7 + 3 * m:]
        x, y, c, chips = _position()
        s = 2 * x + y
        sibling = (x, y, 1 - c)

        def half(a, shard, h):
            rows = outs[a].shape[1] // 2
            return outs[a].at[shard, pl.ds(h * rows, rows), :]

        sends = []
        for a in range(n):
            for j, (px, py) in enumerate(chips):
                sends.append(_remote(half(a, s, c), half(a, s, c),
                                     ici_send.at[3 * a + j], ici_recv.at[3 * a + j], (px, py, c)))
        for j, (px, py) in enumerate(chips):
            sends.append(_remote(cw_out.at[s], cw_out.at[s], ici_send.at[3 * n + j], ici_recv.at[3 * n + j], (px, py, c)))
        for cp in sends:
            cp.start()
        loads = [pltpu.make_async_copy(w_refs[a], f32_s[a], local_sem.at[2 * a]) for a in range(m)]
        stores = [pltpu.make_async_copy(bf16_s[a], cast_outs[a].at[s], local_sem.at[2 * a + 1]) for a in range(m)]
        for cp in loads:
            cp.start()
        for a in range(m):
            loads[a].wait()
            bf16_s[a][...] = f32_s[a][...].astype(BF16)
            stores[a].start()
        passed = []
        for a in range(n):
            for j, (px, py) in enumerate(chips):
                sj = 2 * px + py
                got = half(a, sj, c)
                _remote(got, got, ici_send.at[3 * a + j], ici_recv.at[3 * a + j], (px, py, c)).wait_recv()
                fwd = _remote(got, got, d2d_send.at[3 * a + j], d2d_recv.at[3 * a + j], sibling)
                fwd.start()
                passed.append(fwd)
        for j, (px, py) in enumerate(chips):
            got = cw_out.at[2 * px + py]
            _remote(got, got, ici_send.at[3 * n + j], ici_recv.at[3 * n + j], (px, py, c)).wait_recv()
        for a in range(n):
            for j, (px, py) in enumerate(chips):
                got = half(a, 2 * px + py, 1 - c)
                _remote(got, got, d2d_send.at[3 * a + j], d2d_recv.at[3 * a + j], sibling).wait_recv()
        for cp in sends + passed:
            cp.wait_send()
        for cp in stores:
            cp.wait()

    out = pl.pallas_call(
        body, name="gather_weights",
        in_specs=[ANY] * (n + 1 + m), out_specs=[ANY] * (n + 1 + m),
        out_shape=[jax.ShapeDtypeStruct(b.shape, b.dtype) for b in bufs]
        + [jax.ShapeDtypeStruct((N_SHARD,) + w.shape, BF16) for w in others],
        input_output_aliases={i: i for i in range(n + 1)},
        scratch_shapes=[pltpu.SemaphoreType.DMA((3 * n + 3,)), pltpu.SemaphoreType.DMA((3 * n + 3,)),
                        pltpu.SemaphoreType.DMA((3 * n,)), pltpu.SemaphoreType.DMA((3 * n,)),
                        pltpu.SemaphoreType.DMA((2 * m,))]
        + [pltpu.VMEM(w.shape, F32) for w in others] + [pltpu.VMEM(w.shape, BF16) for w in others],
        compiler_params=pltpu.CompilerParams(has_side_effects=True, vmem_limit_bytes=VMEM_LIMIT),
    )(*bufs, *others)
    return out[:n + 1], out[n + 1:]


def _swap_halves(grads, tag):
    n = len(grads)

    def body(*refs):
        ins, outs, send_sem, recv_sem = refs[:n], refs[n:2 * n], refs[2 * n], refs[2 * n + 1]
        x, y, c, _ = _position()
        copies = []
        for a in range(n):
            rows = ins[a].shape[1] // 2
            copies.append(_remote(ins[a].at[:, pl.ds((1 - c) * rows, rows), :], outs[a],
                                  send_sem.at[a], recv_sem.at[a], (x, y, 1 - c)))
        for cp in copies:
            cp.start()
        for cp in copies:
            cp.wait()

    return pl.pallas_call(
        body, name="swap_grad_halves_" + tag,
        in_specs=[ANY] * n, out_specs=[ANY] * n,
        out_shape=[jax.ShapeDtypeStruct((g.shape[0], g.shape[1] // 2, g.shape[2]), F32) for g in grads],
        scratch_shapes=[pltpu.SemaphoreType.DMA((n,)), pltpu.SemaphoreType.DMA((n,))],
        compiler_params=pltpu.CompilerParams(has_side_effects=True),
    )(*grads)


def _pair_sum(g, got, c_arr, name):
    n_sh, R, C = g.shape
    rows = R // 2

    def body(c_ref, g_ref, r_ref, f_ref, b_ref):
        t = g_ref[...] + r_ref[...]
        f_ref[...] = t
        b_ref[...] = t.astype(BF16)

    blk = pl.BlockSpec((1, rows, C), lambda i, c_ref: (i, 0, 0))
    return pl.pallas_call(
        body, name=name,
        grid_spec=pltpu.PrefetchScalarGridSpec(
            num_scalar_prefetch=1, grid=(n_sh,),
            in_specs=[pl.BlockSpec((1, rows, C), lambda i, c_ref: (i, c_ref[0], 0)), blk],
            out_specs=[blk, blk]),
        out_shape=[jax.ShapeDtypeStruct((n_sh, rows, C), F32), jax.ShapeDtypeStruct((n_sh, rows, C), BF16)],
        compiler_params=_params(("parallel",)),
    )(c_arr, g, got)


def _shard_sum(sums_f32, recv, shard_arr, c_arr, name):
    _, rows, C = sums_f32.shape

    def body(s_ref, c_ref, o_ref, r_ref, t_ref):
        t_ref[...] = ((o_ref[0] + r_ref[0].astype(F32)) + r_ref[1].astype(F32)) + r_ref[2].astype(F32)

    return pl.pallas_call(
        body, name=name,
        grid_spec=pltpu.PrefetchScalarGridSpec(
            num_scalar_prefetch=2, grid=(1,),
            in_specs=[pl.BlockSpec((1, rows, C), lambda i, s_ref, c_ref: (s_ref[0], 0, 0)),
                      pl.BlockSpec((3, rows, C), lambda i, s_ref, c_ref: (0, 0, 0))],
            out_specs=pl.BlockSpec((rows, C), lambda i, s_ref, c_ref: (c_ref[0], 0))),
        out_shape=jax.ShapeDtypeStruct((2 * rows, C), F32),
        compiler_params=_params(("arbitrary",)),
    )(shard_arr, c_arr, sums_f32, recv)


def _join_halves(bufs):
    n = len(bufs)

    def body(*refs):
        outs, send_sem, recv_sem = refs[n:2 * n], refs[2 * n], refs[2 * n + 1]
        x, y, c, _ = _position()
        copies = []
        for a in range(n):
            rows = outs[a].shape[0] // 2
            mine = outs[a].at[pl.ds(c * rows, rows), :]
            copies.append(_remote(mine, mine, send_sem.at[a], recv_sem.at[a], (x, y, 1 - c)))
        for cp in copies:
            cp.start()
        for a, cp in enumerate(copies):
            cp.wait_send()
            rows = outs[a].shape[0] // 2
            theirs = outs[a].at[pl.ds((1 - c) * rows, rows), :]
            _remote(theirs, theirs, send_sem.at[a], recv_sem.at[a], (x, y, 1 - c)).wait_recv()

    return pl.pallas_call(
        body, name="join_grad_halves",
        in_specs=[ANY] * n, out_specs=[ANY] * n,
        out_shape=[jax.ShapeDtypeStruct(b.shape, F32) for b in bufs],
        input_output_aliases={i: i for i in range(n)},
        scratch_shapes=[pltpu.SemaphoreType.DMA((n,)), pltpu.SemaphoreType.DMA((n,))],
        compiler_params=pltpu.CompilerParams(has_side_effects=True),
    )(*bufs)


def _small_sum(parts, tag):
    _, R, C = parts.shape

    def body(p_ref, o_ref):
        t = p_ref[0]
        for k in range(1, 8):
            t = t + p_ref[k]
        o_ref[...] = t

    return pl.pallas_call(
        body, name="small_grad_sum_" + tag, grid=(1,),
        in_specs=[pl.BlockSpec((8, R, C), lambda i: (0, 0, 0))], out_specs=pl.BlockSpec((R, C), lambda i: (0, 0)),
        out_shape=jax.ShapeDtypeStruct((R, C), F32), compiler_params=_params(("arbitrary",)),
    )(parts)


def _adamw_math(w, g, m, v):
    m = ADAM_B1 * m + (1.0 - ADAM_B1) * g
    v = ADAM_B2 * v + (1.0 - ADAM_B2) * (g * g)
    m_hat = m / (1.0 - ADAM_B1 ** ADAM_STEP)
    v_hat = v / (1.0 - ADAM_B2 ** ADAM_STEP)
    delta = -ADAM_LR * (m_hat / (jnp.sqrt(v_hat) + ADAM_EPS) + ADAM_WD * w)
    return delta, m, v


def _adamw_big(ws, gs, ms, vs, name, cargo=()):
    n = len(ws)
    nc = len(cargo)
    kinds = [kind for kind, _ in cargo]

    def body(*refs):
        ins, outs = refs[:4 * n], refs[4 * n + nc:7 * n + nc]
        cargo_refs = (kinds, refs[4 * n:4 * n + nc], refs[7 * n + nc:7 * n + 2 * nc], refs[7 * n + 2 * nc:])
        if nc:
            _cargo_start(*cargo_refs, pl.program_id(0) == 0)
        for a in range(n):
            w, g, m, v = (ins[k * n + a][...] for k in range(4))
            outs[a][...], outs[n + a][...], outs[2 * n + a][...] = _adamw_math(w, g, m, v)
        if nc:
            _cargo_finish(*cargo_refs, pl.program_id(0) == 3)

    blks = [pl.BlockSpec((w.shape[0] // 4, w.shape[1]), lambda i: (i, 0)) for w in ws]
    arrays, cargo_specs, shapes, aliases, sems = _cargo_call(cargo, 4 * n, 3 * n)
    out = pl.pallas_call(
        body, name=name, grid=(4,), in_specs=blks * 4 + cargo_specs, out_specs=blks * 3 + cargo_specs,
        out_shape=[jax.ShapeDtypeStruct(w.shape, F32) for w in ws] * 3 + shapes,
        input_output_aliases=aliases, scratch_shapes=sems,
        compiler_params=_params(("arbitrary",)),
    )(*ws, *gs, *ms, *vs, *arrays)
    return (out[:n], out[n:2 * n], out[2 * n:3 * n]), out[3 * n:]


def _adamw_small(ws, gs, ms, vs):
    n = len(ws)

    def body(*refs):
        for a in range(n):
            w, g, m, v = (refs[k * n + a][...] for k in range(4))
            d, nm, nv = _adamw_math(w, g, m, v)
            refs[4 * n + a][...] = d
            refs[5 * n + a][...] = nm
            refs[6 * n + a][...] = nv

    shapes = [jax.ShapeDtypeStruct(w.shape, F32) for w in ws]
    out = pl.pallas_call(body, name="adamw_small", out_shape=shapes * 3)(*ws, *gs, *ms, *vs)
    return out[:n], out[n:2 * n], out[2 * n:]


BIG = ("w_in", "w_out", "w_up", "w_down")
SMALL = ("g_mix_pre", "pool_w", "pool_scale", "g_mix_post", "g_ffn_pre", "conv_b", "g_ffn_post", "conv_w")
ORDER = ("g_mix_pre", "w_in", "pool_w", "pool_scale", "w_out", "g_mix_post", "g_ffn_pre", "w_up", "conv_w", "conv_b",
         "w_down", "g_ffn_post")


def kernel(x, g_mix_pre, w_in, pool_w, pool_scale, w_out, g_mix_post, g_ffn_pre, w_up, conv_w, conv_b, w_down, g_ffn_post, loss_target, m_g_mix_pre, m_w_in, m_pool_w, m_pool_scale, m_w_out, m_g_mix_post, m_g_ffn_pre, m_w_up, m_conv_w, m_conv_b, m_w_down, m_g_ffn_post, v_g_mix_pre, v_w_in, v_pool_w, v_pool_scale, v_w_out, v_g_mix_post, v_g_ffn_pre, v_w_up, v_conv_w, v_conv_b, v_w_down, v_g_ffn_post):
    args = dict(locals())
    W = {n: args[n][0] for n in ORDER}
    M = {n: args["m_" + n][0] for n in ORDER}
    V = {n: args["v_" + n][0] for n in ORDER}
    for d in (W, M, V):
        d["pool_w"] = d["pool_w"].reshape(-1, POOL_GROUP)
        for n in ("g_mix_pre", "pool_scale", "g_mix_post", "g_ffn_pre", "conv_b", "g_ffn_post"):
            d[n] = d[n].reshape(1, -1)
    CW = W["w_up"].shape[1]
    c_arr = lax.axis_index("c").astype(jnp.int32).reshape(1)
    shard = 2 * lax.axis_index("x") + lax.axis_index("y")
    shard_arr = shard.astype(jnp.int32).reshape(1)
    device = 2 * shard + lax.axis_index("c")

    conv_w_slots = lax.dynamic_update_index_in_dim(jnp.zeros((N_SHARD,) + W["conv_w"].shape, F32), W["conv_w"], shard, 0)
    later = ("w_out", "w_up", "w_down")
    (w_in_g, conv_w_g), cast = _gather_weights([_cast_bf16(W["w_in"], shard_arr, "cast_w_in"), conv_w_slots],
                                               [W[n] for n in later])
    slots = dict(zip(later, cast))
    conv_w_full = conv_w_g.transpose(1, 0, 2).reshape(CONV_WIDTH, 1, N_SHARD * CW)

    loss, grad_x, G = _local_step(
        x[0], loss_target[0], W["g_mix_pre"], w_in_g, W["pool_w"].reshape(-1, POOL_GROUP, POOL_GROUP), W["pool_scale"],
        slots["w_out"], W["g_mix_post"], W["g_ffn_pre"], slots["w_up"], conv_w_full, W["conv_b"],
        slots["w_down"], W["g_ffn_post"], (c_arr, device, shard_arr))

    late, ffn = ("w_in", "w_out"), ("w_up", "w_down")
    in_f32, in_bf16 = _pair_sum(G["w_in"], _swap_halves([G["w_in"]], "mix")[0], c_arr, "pair_sum_w_in")
    loss_rows = jnp.pad(loss, ((0, 7), (0, LANES - 1)))
    small = jnp.concatenate([_pack_small(G, SMALL_LATE), loss_rows], axis=0)
    small_slots = lax.dynamic_update_index_in_dim(jnp.zeros((8,) + small.shape, F32), small, device, 0)
    pick = lambda d, names: [d[n] for n in names]
    delta, new_m, new_v = {}, {}, {}
    updates, landed = _adamw_big(pick(W, ffn), pick(G, ffn), pick(M, ffn), pick(V, ffn), "adamw_ffn",
                                 [("scatter", in_bf16), ("everyone", small_slots)])
    halves = [_shard_sum(in_f32, landed[0], shard_arr, c_arr, "shard_sum_w_in"), G["w_out"]]
    full = dict(zip(late, _join_halves(halves)))
    full.update({n: G[n] for n in ffn})
    full.update(_unpack_small(_small_sum(G["small_early"], "early"), SMALL_EARLY, W, shard))
    late_total = _small_sum(landed[1], "late")
    full.update(_unpack_small(late_total, SMALL_LATE, W, shard))
    loss = late_total[-8, 0]

    lift = lambda d: [d[n][:, None, :] if n == "conv_w" else d[n] for n in SMALL]
    for names, (ds, nms, nvs) in ((ffn, updates),
                                  (late, _adamw_big(pick(W, late), pick(full, late), pick(M, late), pick(V, late), "adamw_mix")[0]),
                                  (SMALL, _adamw_small(lift(W), lift(full), lift(M), lift(V)))):
        for n, d, nm, nv in zip(names, ds, nms, nvs):
            delta[n], new_m[n], new_v[n] = d, nm, nv

    shaped = lambda d: [d[n].reshape(args[n].shape) for n in ORDER]
    return (loss, grad_x[None], *shaped(full), *shaped(delta), *shaped(new_m), *shaped(new_v))
```

```python
import functools

import jax
import jax.numpy as jnp
from jax import lax
from jax.experimental import pallas as pl
from jax.experimental.pallas import tpu as pltpu

F32 = jnp.float32
BF16 = jnp.bfloat16

RMS_EPS = 1e-6
NEG_INF = -1e30
N_HEADS = 8
HEAD_DIM = 64
ATTN_WIDTH = N_HEADS * HEAD_DIM
ATTN_SCALE = HEAD_DIM ** -0.5
ATTN_BLOCK = 128
DILATIONS = (1, 4, 16)
RESIDUES_PER_STEP = 4
CONSECUTIVE_BLOCKS = 4
POOL_WINDOWS = (2, 4, 8, 16)
POOL_GROUP = 128
POOL_WIDTH = POOL_GROUP * len(POOL_WINDOWS)
POOL_HALO = 16
CONV_WIDTH = 3
CONV_HALO = 8
N_SHARD = 4
LANES = 128
STAT_LANES = 16
STAT_WIDTH = N_HEADS * STAT_LANES

ADAM_LR = 0.001
ADAM_B1 = 0.9
ADAM_B2 = 0.999
ADAM_EPS = 1e-08
ADAM_WD = 0.01
ADAM_STEP = 10

VMEM_LIMIT = 60 * 1024 * 1024
MESH = pl.DeviceIdType.MESH
NT = (((1,), (1,)), ((), ()))
TN = (((0,), (0,)), ((), ()))


def _params(sem, vmem=None):
    return pltpu.CompilerParams(dimension_semantics=sem, vmem_limit_bytes=vmem)


def _const_spec(shape):
    zeros = (0,) * len(shape)
    return pl.BlockSpec(shape, lambda *_: zeros, pipeline_mode=pl.Buffered(1))


def _dot(a, b):
    return jnp.dot(a, b, preferred_element_type=F32)


def _dot_nt(a, b):
    return lax.dot_general(a, b, NT, preferred_element_type=F32)


def _dot_tn(a, b):
    return lax.dot_general(a, b, TN, preferred_element_type=F32)


def _rms_stats(x):
    r = lax.rsqrt(jnp.mean(x * x, axis=-1, keepdims=True) + RMS_EPS)
    return x * r, r


def _rms_bwd(dy, n, r, g):
    dg = jnp.sum(dy * n, axis=0, keepdims=True)
    dn = dy * g
    dx = r * (dn - n * jnp.mean(dn * n, axis=-1, keepdims=True))
    return dx, dg


def _gelu_tanh(g):
    k = 0.7978845608028654
    kc = k * 0.044715
    g2 = g * g
    t = jnp.tanh(g * (k + kc * g2))
    h = 0.5 * t + 0.5
    dh = (0.5 - 0.5 * (t * t)) * (k + (3.0 * kc) * g2)
    return g * h, h + g * dh


def _residue_shape(S, d, dtype, width=ATTN_WIDTH):
    return jax.ShapeDtypeStruct((S // d, d * width), dtype)


def _residue_spec(TM, d, width=ATTN_WIDTH):
    return pl.BlockSpec((TM // d, d * width), lambda i: (i, 0))


def _token_scratch(TM, width=ATTN_WIDTH):
    return [pltpu.VMEM((TM, LANES), F32)] * (width // LANES)


def _head_sum_matrix():
    r = lax.broadcasted_iota(jnp.int32, (ATTN_WIDTH, STAT_WIDTH), 0)
    c = lax.broadcasted_iota(jnp.int32, (ATTN_WIDTH, STAT_WIDTH), 1)
    return (r // HEAD_DIM == c // STAT_LANES).astype(BF16)


def _head_spread_matrix():
    c = lax.broadcasted_iota(jnp.int32, (STAT_WIDTH, ATTN_WIDTH), 0)
    r = lax.broadcasted_iota(jnp.int32, (STAT_WIDTH, ATTN_WIDTH), 1)
    return (c == (r // HEAD_DIM) * STAT_LANES).astype(BF16)


def _bf16_pieces(x, n):
    pieces = []
    for _ in range(n):
        p = x.astype(BF16)
        pieces.append(p)
        x = x - p.astype(F32)
    return pieces


def _put_tokens(dst_s, val):
    for cb, chunk in enumerate(dst_s):
        chunk[...] = val[:, cb * LANES:(cb + 1) * LANES]


def _get_tokens(src_s):
    return jnp.concatenate([chunk[...] for chunk in src_s], axis=1)


def _to_residue(val, src_s, out_ref, d, dtype):
    if d == 1:
        out_ref[...] = val.astype(dtype)
        return
    rows = src_s[0].shape[0]
    for r in range(d):
        for cb, chunk in enumerate(src_s):
            col = (r * len(src_s) + cb) * LANES
            out_ref[:, col:col + LANES] = chunk[pl.ds(r, rows // d, stride=d), :].astype(dtype)


def _from_residue(in_ref, dst_s, d):
    if d == 1:
        return in_ref[...].astype(F32)
    rows = dst_s[0].shape[0]
    for r in range(d):
        for cb, chunk in enumerate(dst_s):
            col = (r * len(dst_s) + cb) * LANES
            chunk[pl.ds(r, rows // d, stride=d), :] = in_ref[:, col:col + LANES].astype(F32)
    return _get_tokens(dst_s)


def _mix_in_fwd(x, g_pre, w_in, cargo=()):
    S, D = x.shape
    TM = 512
    nc = len(cargo)
    kinds = [kind for kind, _ in cargo]
    n_chunks = ATTN_WIDTH // LANES

    def body(x_ref, g_ref, w_ref, *refs):
        cargo_in, refs = refs[:nc], refs[nc:]
        qkv_refs, p_ref, h_ref = refs[:9], refs[9], refs[10]
        t_s = refs[11 + nc:11 + nc + n_chunks]
        cargo_refs = (kinds, cargo_in, refs[11:11 + nc], refs[11 + nc + n_chunks:])
        if nc:
            _cargo_start(*cargo_refs, pl.program_id(0) == 0)
        n, _ = _rms_stats(x_ref[...])
        hb = (n * g_ref[...]).astype(BF16)
        h_ref[...] = hb
        for a in range(3):
            res = _dot(hb, w_ref[a])
            if a == 0:
                res = res * ATTN_SCALE
            _put_tokens(t_s, res)
            for i, d in enumerate(DILATIONS):
                _to_residue(res, t_s, qkv_refs[3 * i + a], d, BF16)
        p_ref[...] = _dot(hb, w_ref[3])
        if nc:
            _cargo_finish(*cargo_refs, pl.program_id(0) == S // TM - 1)

    row = lambda w: pl.BlockSpec((TM, w), lambda i: (i, 0))
    arrays, cargo_specs, shapes, aliases, sems = _cargo_call(cargo, 3, 11)
    out = pl.pallas_call(
        body, name="mix_in_fwd", grid=(S // TM,),
        in_specs=[row(D), _const_spec((1, D)), _const_spec(w_in.shape)] + cargo_specs,
        out_specs=[_residue_spec(TM, d) for d in DILATIONS for _ in range(3)] + [row(POOL_WIDTH), row(D)] + cargo_specs,
        out_shape=[_residue_shape(S, d, BF16) for d in DILATIONS for _ in range(3)]
        + [jax.ShapeDtypeStruct((S, POOL_WIDTH), F32), jax.ShapeDtypeStruct((S, D), BF16)] + shapes,
        input_output_aliases=aliases,
        scratch_shapes=_token_scratch(TM) + sems,
        compiler_params=_params(("arbitrary",), VMEM_LIMIT),
    )(x, g_pre, w_in, *arrays)
    return [out[0:3], out[3:6], out[6:9]], out[9], out[10], out[11:]


def _band_mask(n):
    qi = lax.broadcasted_iota(jnp.int32, (ATTN_BLOCK, 2 * ATTN_BLOCK), 0)
    ki = lax.broadcasted_iota(jnp.int32, (ATTN_BLOCK, 2 * ATTN_BLOCK), 1)
    dist = qi + ATTN_BLOCK - ki
    return (dist >= 0) & (dist <= ATTN_BLOCK) & ((ki >= ATTN_BLOCK) | (n > 0))


def _first_head_lanes():
    return lax.broadcasted_iota(jnp.int32, (1, LANES), 1) < HEAD_DIM


def _stack_heads(pair, first):
    zero = jnp.zeros_like(pair)
    return jnp.concatenate([jnp.where(first, pair, zero), jnp.where(first, zero, pair)], axis=0)


def _unstack_heads(stacked, first):
    return jnp.where(first, stacked[:ATTN_BLOCK], stacked[ATTN_BLOCK:])


CARGO_COPIES = {"ici": 3, "d2d": 3, "scatter": 3, "swap": 1, "everyone": 7, "join": 1}
CARGO_IN_PLACE = ("ici", "d2d", "everyone", "join")


def _cargo_copies(kinds, ins, outs, send_sems, recv_sems, want_recvs=True):
    x, y, c, chips = _position()
    s = 2 * x + y
    me = 2 * s + c
    sibling = (x, y, 1 - c)
    sends, recvs = [], []

    def add(k, src, dst, landing, to):
        sends.append(_remote(src, dst, send_sems.at[k], recv_sems.at[k], to))
        if want_recvs:
            recvs.append(_remote(landing, landing, send_sems.at[k], recv_sems.at[k], to))

    k0 = 0
    for a, kind in enumerate(kinds):
        if kind == "swap":
            rows = ins[a].shape[1] // 2
            add(k0, ins[a].at[:, pl.ds((1 - c) * rows, rows), :], outs[a], outs[a], sibling)
        elif kind == "join":
            rows = outs[a].shape[0] // 2
            mine = outs[a].at[pl.ds(c * rows, rows), :]
            add(k0, mine, mine, outs[a].at[pl.ds((1 - c) * rows, rows), :], sibling)
        elif kind == "everyone":
            for m in range(1, 8):
                peer = (x ^ (m >> 2), y ^ ((m >> 1) & 1), c ^ (m & 1))
                add(k0 + m - 1, outs[a].at[me], outs[a].at[me], outs[a].at[4 * peer[0] + 2 * peer[1] + peer[2]], peer)
        else:
            for j, (px, py) in enumerate(chips):
                sj = 2 * px + py
                if kind == "scatter":
                    add(k0 + j, ins[a].at[sj], outs[a].at[j], outs[a].at[j], (px, py, c))
                    continue
                buf = outs[a]
                rows = buf.shape[1] // 2
                half = lambda shard, h: buf.at[shard, pl.ds(h * rows, rows), :]
                if kind == "ici":
                    add(k0 + j, half(s, c), half(s, c), half(sj, c), (px, py, c))
                else:
                    add(k0 + j, half(sj, c), half(sj, c), half(sj, 1 - c), sibling)
        k0 += CARGO_COPIES[kind]
    return sends, recvs


def _cargo_start(kinds, ins, outs, sems, first_step):
    @pl.when(first_step)
    def _():
        for cp in _cargo_copies(kinds, ins, outs, *sems, want_recvs=False)[0]:
            cp.start()


def _cargo_finish(kinds, ins, outs, sems, last_step):
    @pl.when(last_step)
    def _():
        sends, recvs = _cargo_copies(kinds, ins, outs, *sems)
        for cp in sends:
            cp.wait_send()
        for cp in recvs:
            cp.wait_recv()


def _cargo_call(cargo, n_in, n_out):
    arrays = [a for _, a in cargo]
    shapes = []
    for kind, a in cargo:
        if kind == "scatter":
            shape = (3,) + a.shape[1:]
        elif kind == "swap":
            shape = (a.shape[0], a.shape[1] // 2, a.shape[2])
        else:
            shape = a.shape
        shapes.append(jax.ShapeDtypeStruct(shape, a.dtype))
    aliases = {n_in + i: n_out + i for i, (kind, _) in enumerate(cargo) if kind in CARGO_IN_PLACE}
    n_sems = sum(CARGO_COPIES[kind] for kind, _ in cargo)
    sems = [pltpu.SemaphoreType.DMA((n_sems,))] * 2 if cargo else []
    return arrays, [ANY] * len(cargo), shapes, aliases, sems


def _attn_fwd(q, k, v, d, cargo=()):
    L = q.shape[0]
    group = min(d, RESIDUES_PER_STEP)
    width = group * ATTN_WIDTH
    qb = RESIDUES_PER_STEP // group
    steps = L // (qb * ATTN_BLOCK)
    nc = len(cargo)
    kinds = [kind for kind, _ in cargo]

    def body(*refs):
        q_ref, kp_ref, kc_ref, vp_ref, vc_ref = refs[:5]
        o_ref, lse_ref = refs[5 + nc:7 + nc]
        cargo_refs = (kinds, refs[5:5 + nc], refs[7 + nc:7 + 2 * nc], refs[7 + 2 * nc:])
        r, n = pl.program_id(0), pl.program_id(1)
        if nc:
            _cargo_start(*cargo_refs, (r == 0) & (n == 0))
        first = _first_head_lanes()
        for sub in range(qb):
            rows = slice(sub * ATTN_BLOCK, (sub + 1) * ATTN_BLOCK)
            valid = _band_mask(n if sub == 0 else 1)
            valid2 = jnp.concatenate([valid, valid], axis=0)
            for hp in range(width // LANES):
                sl = slice(hp * LANES, (hp + 1) * LANES)
                if sub == 0:
                    kk = jnp.concatenate([kp_ref[:, sl], kc_ref[rows, sl]], axis=0)
                    vv = jnp.concatenate([vp_ref[:, sl], vc_ref[rows, sl]], axis=0)
                else:
                    keys = slice((sub - 1) * ATTN_BLOCK, (sub + 1) * ATTN_BLOCK)
                    kk, vv = kc_ref[keys, sl], vc_ref[keys, sl]
                s = jnp.where(valid2, _dot_nt(_stack_heads(q_ref[rows, sl], first), kk), NEG_INF)
                m = jnp.max(s, axis=-1, keepdims=True)
                p = jnp.exp(s - m)
                den = jnp.sum(p, axis=-1, keepdims=True)
                o_ref[rows, sl] = _unstack_heads(_dot(p.astype(BF16), vv) / den, first).astype(BF16)
                lse = m + jnp.log(den)
                lane = 2 * hp * STAT_LANES
                for half in range(2):
                    lse_ref[rows, lane + half * STAT_LANES:lane + (half + 1) * STAT_LANES] = jnp.broadcast_to(
                        lse[half * ATTN_BLOCK:(half + 1) * ATTN_BLOCK], (ATTN_BLOCK, STAT_LANES))
        if nc:
            _cargo_finish(*cargo_refs, (r == d // group - 1) & (n == steps - 1))

    cur = pl.BlockSpec((qb * ATTN_BLOCK, width), lambda r, n: (n, r))
    prev = pl.BlockSpec((ATTN_BLOCK, width), lambda r, n: (jnp.maximum(n * qb - 1, 0), r))
    arrays, specs, shapes, aliases, sems = _cargo_call(cargo, 5, 2)
    out = pl.pallas_call(
        body, name=f"attn_fwd_d{d}", grid=(d // group, steps),
        in_specs=[cur, prev, cur, prev, cur] + specs,
        out_specs=[cur, pl.BlockSpec((qb * ATTN_BLOCK, group * STAT_WIDTH), lambda r, n: (n, r))] + specs,
        out_shape=[jax.ShapeDtypeStruct((L, d * ATTN_WIDTH), BF16), jax.ShapeDtypeStruct((L, d * STAT_WIDTH), F32)] + shapes,
        input_output_aliases=aliases, scratch_shapes=sems,
        compiler_params=_params(("arbitrary", "arbitrary")),
    )(q, k, k, v, v, *arrays)
    return out[0], out[1], out[2:]


def _attn_mix(outs, lses):
    S = outs[0].shape[0]
    TM = 512
    n = len(DILATIONS)

    def body(*refs):
        o_refs, l_refs, attn_ref, lse_refs = refs[:n], refs[n:2 * n], refs[2 * n], refs[2 * n + 1:3 * n + 1]
        t_s, c_s = refs[3 * n + 1:-1], refs[-1:]
        os = [_from_residue(o_refs[i], t_s, d) for i, d in enumerate(DILATIONS)]
        ls = [_from_residue(l_refs[i], c_s, d) for i, d in enumerate(DILATIONS)]
        m = jnp.maximum(jnp.maximum(ls[0], ls[1]), ls[2])
        es = [jnp.exp(l - m) for l in ls]
        den = es[0] + es[1] + es[2]
        spread = _head_spread_matrix()
        ws = [sum(_dot(p, spread) for p in _bf16_pieces(e / den, 2)) for e in es]
        attn_ref[...] = (ws[0] * os[0] + ws[1] * os[1] + ws[2] * os[2]).astype(BF16)
        lse = m + jnp.log(den)
        _put_tokens(c_s, lse)
        for i, d in enumerate(DILATIONS):
            _to_residue(lse, c_s, lse_refs[i], d, F32)

    specs = [_residue_spec(TM, d) for d in DILATIONS]
    stats = [_residue_spec(TM, d, STAT_WIDTH) for d in DILATIONS]
    out = pl.pallas_call(
        body, name="attn_mix", grid=(S // TM,),
        in_specs=specs + stats, out_specs=[specs[0]] + stats,
        out_shape=[jax.ShapeDtypeStruct((S, ATTN_WIDTH), BF16)]
        + [_residue_shape(S, d, F32, STAT_WIDTH) for d in DILATIONS],
        scratch_shapes=_token_scratch(TM) + _token_scratch(TM, STAT_WIDTH),
        compiler_params=_params(("parallel",)),
    )(*outs, *lses)
    return out[0], out[1:]


def _pool_counts(first_row, rows, w):
    t = first_row + lax.broadcasted_iota(jnp.int32, (rows, 1), 0)
    return jnp.minimum(t + 1, w).astype(F32)


def _trailing_sums(xe, w):
    s, k = xe, 1
    while k < w:
        s = s + pltpu.roll(s, k, 0)
        k *= 2
    return s


def _leading_sums(xe, w):
    rows = xe.shape[0]
    s, k = xe, 1
    while k < w:
        s = s + pltpu.roll(s, rows - k, 0)
        k *= 2
    return s


def _pooled_groups(halo, cur, first_row):
    TM = cur.shape[0]
    xe = jnp.concatenate([halo, cur], axis=0)
    out = []
    for g, w in enumerate(POOL_WINDOWS):
        a = xe[:, g * POOL_GROUP:(g + 1) * POOL_GROUP]
        s = _trailing_sums(a, w)[POOL_HALO:]
        out.append(s / _pool_counts(first_row, TM, w) - a[POOL_HALO:])
    return out


def _pool_fwd(pool_in, pool_w, pool_scale):
    S = pool_in.shape[0]
    TM = 512
    HB = TM // POOL_HALO

    def body(cur_ref, halo_ref, w_ref, sc_ref, y_ref):
        i = pl.program_id(0)
        halo = jnp.where(i > 0, halo_ref[...], 0.0)
        pooled = _pooled_groups(halo, cur_ref[...], i * TM)
        for g in range(len(POOL_WINDOWS)):
            sl = slice(g * POOL_GROUP, (g + 1) * POOL_GROUP)
            y = _dot(pooled[g].astype(BF16), w_ref[g].astype(BF16)) * sc_ref[:, sl]
            y_ref[:, sl] = y.astype(BF16)

    return pl.pallas_call(
        body, name="pool_fwd", grid=(S // TM,),
        in_specs=[pl.BlockSpec((TM, POOL_WIDTH), lambda i: (i, 0)),
                  pl.BlockSpec((POOL_HALO, POOL_WIDTH), lambda i: (jnp.maximum(i * HB - 1, 0), 0)),
                  _const_spec(pool_w.shape), _const_spec((1, POOL_WIDTH))],
        out_specs=pl.BlockSpec((TM, POOL_WIDTH), lambda i: (i, 0)),
        out_shape=jax.ShapeDtypeStruct((S, POOL_WIDTH), BF16),
        compiler_params=_params(("parallel",)),
    )(pool_in, pool_in, pool_w, pool_scale)


def _mix_out_fwd(attn, pool, w_out, x, g_post, g_ffn_pre):
    S, D = x.shape
    TM = 512

    def body(a_ref, p_ref, w_ref, x_ref, gp_ref, gf_ref, mixed_ref, x1_ref, h2_ref, cat_ref):
        ab = a_ref[...]
        cat_ref[:, :ATTN_WIDTH] = ab
        cat_ref[:, ATTN_WIDTH:] = p_ref[...]
        mixed = _dot(ab, w_ref[:ATTN_WIDTH, :]) + _dot(p_ref[...], w_ref[ATTN_WIDTH:, :])
        mixed_ref[...] = mixed.astype(BF16)
        n, _ = _rms_stats(mixed)
        x1 = x_ref[...] + n * gp_ref[...]
        x1_ref[...] = x1
        n2, _ = _rms_stats(x1)
        h2_ref[...] = (n2 * gf_ref[...]).astype(BF16)

    row = lambda w: pl.BlockSpec((TM, w), lambda i: (i, 0))
    return pl.pallas_call(
        body, name="mix_out_fwd", grid=(S // TM,),
        in_specs=[row(ATTN_WIDTH), row(POOL_WIDTH), _const_spec(w_out.shape), row(D),
                  _const_spec((1, D)), _const_spec((1, D))],
        out_specs=[row(D), row(D), row(D), row(D)],
        out_shape=[jax.ShapeDtypeStruct((S, D), BF16), jax.ShapeDtypeStruct((S, D), F32),
                   jax.ShapeDtypeStruct((S, D), BF16), jax.ShapeDtypeStruct((S, D), BF16)],
        compiler_params=_params(("parallel",), VMEM_LIMIT),
    )(attn, pool, w_out, x, g_post, g_ffn_pre)


def _ffn_fwd(h2, x1, target, w_up, w_down, conv_w, conv_b, g_post):
    S, D = x1.shape
    CW = w_up.shape[2]
    FF = 2 * CW
    TM = 256
    piece = 4 * LANES
    pieces = [(lo, min(lo + piece, CW)) for lo in range(0, CW, piece)]

    def body(h2_ref, x1_ref, t_ref, wu_ref, wd_ref, cw_ref, cb_ref, g_ref,
             yv_ref, dy_ref, df_ref, dc_ref, loss_ref, dg_ref, dcb_ref, dcw_ref,
             ue_s, dgate_s, dval_s):
        i = pl.program_id(0)

        @pl.when(i == 0)
        def _():
            loss_ref[...] = jnp.zeros_like(loss_ref)
            dg_ref[...] = jnp.zeros_like(dg_ref)
            dcb_ref[...] = jnp.zeros_like(dcb_ref)
            dcw_ref[...] = jnp.zeros_like(dcw_ref)
            ue_s[0:CONV_HALO, :] = jnp.zeros((CONV_HALO, 2 * FF), F32)

        @pl.when(i > 0)
        def _():
            ue_s[0:CONV_HALO, :] = ue_s[TM:TM + CONV_HALO, :]

        def shifted(cols, k):
            return pltpu.roll(ue_s[:, cols], k, 0)[CONV_HALO:]

        def conv(cols):
            return (cb_ref[:, cols] + cw_ref[2, :, cols] * ue_s[CONV_HALO:, cols]
                    + cw_ref[1, :, cols] * shifted(cols, 1) + cw_ref[0, :, cols] * shifted(cols, 2))

        hb = h2_ref[...]
        f = jnp.zeros((TM, D), F32)
        for j in range(2):
            jc = slice(j * CW, (j + 1) * CW)
            for half in range(2):
                blk = 2 * half + j
                cols = slice(blk * CW, (blk + 1) * CW)
                ue_s[CONV_HALO:, cols] = _dot(hb, wu_ref[blk])
            for lo, hi in pieces:
                pc = slice(j * CW + lo, j * CW + hi)
                gelu, dgelu = _gelu_tanh(conv(pc).astype(BF16))
                val = conv(slice(FF + j * CW + lo, FF + j * CW + hi)).astype(BF16)
                dgate_s[:, pc] = val * dgelu
                dval_s[:, pc] = gelu
                yv_ref[:, pc] = gelu * val
            f = f + _dot(yv_ref[:, jc], wd_ref[jc, :])

        n, r = _rms_stats(f)
        err = x1_ref[...] + n * g_ref[...] - t_ref[...]
        loss_ref[...] += 0.5 * jnp.sum(jnp.mean(err * err, axis=-1, keepdims=True), axis=0, keepdims=True)
        dy = err / D
        dy_ref[...] = dy
        df, dg = _rms_bwd(dy, n, r, g_ref[...])
        dg_ref[...] += dg
        dfb = df.astype(BF16)
        df_ref[...] = dfb

        for j in range(2):
            jc = slice(j * CW, (j + 1) * CW)
            dyv = _dot_nt(dfb, wd_ref[jc, :])
            for lo, hi in pieces:
                pc = slice(j * CW + lo, j * CW + hi)
                for half, scale_s in ((0, dgate_s), (1, dval_s)):
                    cols = slice(half * FF + j * CW + lo, half * FF + j * CW + hi)
                    dcv = dyv[:, lo:hi] * scale_s[:, pc].astype(F32)
                    dc_ref[:, cols] = dcv.astype(BF16)
                    dcb_ref[:, cols] += jnp.sum(dcv, axis=0, keepdims=True)
                    dcw_ref[2, :, cols] += jnp.sum(dcv * ue_s[CONV_HALO:, cols], axis=0, keepdims=True)
                    dcw_ref[1, :, cols] += jnp.sum(dcv * shifted(cols, 1), axis=0, keepdims=True)
                    dcw_ref[0, :, cols] += jnp.sum(dcv * shifted(cols, 2), axis=0, keepdims=True)

    row = lambda w: pl.BlockSpec((TM, w), lambda i: (i, 0))
    acc = lambda shape: pl.BlockSpec(shape, lambda i: (0,) * len(shape))
    return pl.pallas_call(
        body, name="ffn_fwd", grid=(S // TM,),
        in_specs=[row(D), row(D), row(D), _const_spec(w_up.shape), _const_spec(w_down.shape),
                  _const_spec(conv_w.shape), _const_spec((1, 2 * FF)), _const_spec((1, D))],
        out_specs=[row(FF), row(D), row(D), row(2 * FF),
                   acc((1, 1)), acc((1, D)), acc((1, 2 * FF)), acc((CONV_WIDTH, 1, 2 * FF))],
        out_shape=[jax.ShapeDtypeStruct((S, FF), BF16),
                   jax.ShapeDtypeStruct((S, D), F32), jax.ShapeDtypeStruct((S, D), BF16),
                   jax.ShapeDtypeStruct((S, 2 * FF), BF16),
                   jax.ShapeDtypeStruct((1, 1), F32), jax.ShapeDtypeStruct((1, D), F32),
                   jax.ShapeDtypeStruct((1, 2 * FF), F32), jax.ShapeDtypeStruct((CONV_WIDTH, 1, 2 * FF), F32)],
        scratch_shapes=[pltpu.VMEM((TM + CONV_HALO, 2 * FF), F32), pltpu.VMEM((TM, FF), BF16),
                        pltpu.VMEM((TM, FF), BF16)],
        compiler_params=_params(("arbitrary",), VMEM_LIMIT),
    )(h2, x1, target, w_up, w_down, conv_w, conv_b, g_post)


def _ffn_bwd(dc, conv_w, w_up, x1, g_ffn_pre, dy):
    S, D = x1.shape
    CW = w_up.shape[2]
    F2 = 4 * CW
    TM = 256
    HB = TM // CONV_HALO
    last = S // CONV_HALO - 1
    n_tiles = S // TM

    def body(dc_ref, halo_ref, cw_ref, wu_ref, x1_ref, g_ref, dy_ref, du_ref, dx1_ref, dg_ref):
        i = pl.program_id(0)

        @pl.when(i == 0)
        def _():
            dg_ref[...] = jnp.zeros_like(dg_ref)

        keep = i < n_tiles - 1
        dh2 = jnp.zeros((TM, D), F32)
        for blk in range(N_SHARD):
            cols = slice(blk * CW, (blk + 1) * CW)
            halo = jnp.where(keep, halo_ref[:, cols].astype(F32), 0.0)
            dce = jnp.concatenate([dc_ref[:, cols].astype(F32), halo], axis=0)
            rows = TM + CONV_HALO
            du = (cw_ref[2, :, cols] * dce[:TM]
                  + cw_ref[1, :, cols] * pltpu.roll(dce, rows - 1, 0)[:TM]
                  + cw_ref[0, :, cols] * pltpu.roll(dce, rows - 2, 0)[:TM])
            dub = du.astype(BF16)
            du_ref[:, cols] = dub
            dh2 = dh2 + _dot_nt(dub, wu_ref[blk])
        n2, r2 = _rms_stats(x1_ref[...])
        dx, dg = _rms_bwd(dh2, n2, r2, g_ref[...])
        dg_ref[...] += dg
        dx1_ref[...] = (dy_ref[...] + dx).astype(BF16)

    row = lambda w: pl.BlockSpec((TM, w), lambda i: (i, 0))
    return pl.pallas_call(
        body, name="ffn_bwd", grid=(S // TM,),
        in_specs=[row(F2), pl.BlockSpec((CONV_HALO, F2), lambda i: (jnp.minimum((i + 1) * HB, last), 0)),
                  _const_spec(conv_w.shape), _const_spec(w_up.shape), row(D), _const_spec((1, D)), row(D)],
        out_specs=[row(F2), row(D), pl.BlockSpec((1, D), lambda i: (0, 0))],
        out_shape=[jax.ShapeDtypeStruct((S, F2), BF16), jax.ShapeDtypeStruct((S, D), BF16),
                   jax.ShapeDtypeStruct((1, D), F32)],
        compiler_params=_params(("arbitrary",), VMEM_LIMIT),
    )(dc, dc, conv_w, w_up, x1, g_ffn_pre, dy)


def _matmul_tn(a, b, n_blocks, name):
    S, M = a.shape
    N = b.shape[1]
    tn = N // n_blocks
    tm = M if M <= 1024 else M // 2
    tk = 2048
    nk = S // tk

    def body(a_ref, b_ref, o_ref):
        @pl.when(pl.program_id(2) == 0)
        def _():
            o_ref[...] = jnp.zeros_like(o_ref)
        o_ref[0] += _dot_tn(a_ref[...], b_ref[...])

    return pl.pallas_call(
        body, name=name, grid=(M // tm, n_blocks, nk),
        in_specs=[pl.BlockSpec((tk, tm), lambda i, j, k: (k, i)), pl.BlockSpec((tk, tn), lambda i, j, k: (k, j))],
        out_specs=pl.BlockSpec((1, tm, tn), lambda i, j, k: (j, i, 0)),
        out_shape=jax.ShapeDtypeStruct((n_blocks, M, tn), F32),
        compiler_params=_params(("parallel", "parallel", "arbitrary"), VMEM_LIMIT),
    )(a, b)


def _mix_out_bwd(dx1, mixed, g_post, w_out, attn, cargo=()):
    S, D = dx1.shape
    TM = 512
    nd = len(DILATIONS)
    nc = len(cargo)
    kinds = [kind for kind, _ in cargo]
    n_chunks = ATTN_WIDTH // LANES

    def body(*refs):
        dx_ref, m_ref, g_ref, w_ref, a_ref = refs[:5]
        dm_ref, dp_ref, dg_ref = refs[5 + nc:8 + nc]
        da_refs, dl_refs = refs[8 + nc:8 + nc + nd], refs[8 + nc + nd:8 + nc + 2 * nd]
        n_out = 8 + nc + 2 * nd
        t_s = refs[n_out + nc:n_out + nc + n_chunks]
        c_s = refs[n_out + nc + n_chunks:n_out + nc + n_chunks + 1]
        cargo_refs = (kinds, refs[5:5 + nc], refs[n_out:n_out + nc], refs[n_out + nc + n_chunks + 1:])
        if nc:
            _cargo_start(*cargo_refs, pl.program_id(0) == 0)

        @pl.when(pl.program_id(0) == 0)
        def _():
            dg_ref[...] = jnp.zeros_like(dg_ref)

        n, r = _rms_stats(m_ref[...].astype(F32))
        dm, dg = _rms_bwd(dx_ref[...].astype(F32), n, r, g_ref[...])
        dg_ref[...] += dg
        dmb = dm.astype(BF16)
        dm_ref[...] = dmb
        da = _dot_nt(dmb, w_ref[:ATTN_WIDTH, :])
        _put_tokens(t_s, da)
        for i, d in enumerate(DILATIONS):
            _to_residue(da, t_s, da_refs[i], d, BF16)
        dp_ref[...] = _dot_nt(dmb, w_ref[ATTN_WIDTH:, :]).astype(BF16)
        gather = _head_sum_matrix()
        delta = sum(_dot(p, gather) for p in _bf16_pieces(da * a_ref[...].astype(F32), 2))
        _put_tokens(c_s, delta)
        for i, d in enumerate(DILATIONS):
            _to_residue(delta, c_s, dl_refs[i], d, F32)
        if nc:
            _cargo_finish(*cargo_refs, pl.program_id(0) == S // TM - 1)

    row = lambda w: pl.BlockSpec((TM, w), lambda i: (i, 0))
    specs = [_residue_spec(TM, d) for d in DILATIONS]
    arrays, cargo_specs, shapes, aliases, sems = _cargo_call(cargo, 5, 3 + 2 * nd)
    out = pl.pallas_call(
        body, name="mix_out_bwd", grid=(S // TM,),
        in_specs=[row(D), row(D), _const_spec((1, D)), _const_spec(w_out.shape), row(ATTN_WIDTH)] + cargo_specs,
        out_specs=[row(D), row(POOL_WIDTH), pl.BlockSpec((1, D), lambda i: (0, 0))] + specs
        + [_residue_spec(TM, d, STAT_WIDTH) for d in DILATIONS] + cargo_specs,
        out_shape=[jax.ShapeDtypeStruct((S, D), BF16), jax.ShapeDtypeStruct((S, POOL_WIDTH), BF16),
                   jax.ShapeDtypeStruct((1, D), F32)]
        + [_residue_shape(S, d, BF16) for d in DILATIONS]
        + [_residue_shape(S, d, F32, STAT_WIDTH) for d in DILATIONS] + shapes,
        input_output_aliases=aliases,
        scratch_shapes=_token_scratch(TM) + _token_scratch(TM, STAT_WIDTH) + sems,
        compiler_params=_params(("arbitrary",), VMEM_LIMIT),
    )(dx1, mixed, g_post, w_out, attn, *arrays)
    return out[0], out[1], out[2], out[3:3 + nd], out[3 + nd:3 + 2 * nd], out[3 + 2 * nd:]


def _pool_bwd(pool_in, d_pool, pool_w, pool_scale):
    S = pool_in.shape[0]
    TM = 512
    HB = TM // POOL_HALO
    last = S // POOL_HALO - 1
    G = len(POOL_WINDOWS)

    def body(cur_ref, halo_ref, dcur_ref, dnext_ref, w_ref, sc_ref, dxin_ref, dw_ref, dsc_ref):
        i = pl.program_id(0)

        @pl.when(i == 0)
        def _():
            dw_ref[...] = jnp.zeros_like(dw_ref)
            dsc_ref[...] = jnp.zeros_like(dsc_ref)

        halo = jnp.where(i > 0, halo_ref[...], 0.0)
        pooled = _pooled_groups(halo, cur_ref[...], i * TM)
        dnext = jnp.where(i < S // TM - 1, dnext_ref[...].astype(F32), 0.0)
        dye = jnp.concatenate([dcur_ref[...].astype(F32), dnext], axis=0)
        for g, w in enumerate(POOL_WINDOWS):
            sl = slice(g * POOL_GROUP, (g + 1) * POOL_GROUP)
            wg = w_ref[g].astype(BF16)
            pb = pooled[g].astype(BF16)
            dsc_ref[:, sl] += jnp.sum(dye[:TM, sl] * _dot(pb, wg), axis=0, keepdims=True)
            dpre = (dye[:, sl] * sc_ref[:, sl]).astype(BF16)
            dw_ref[g] += _dot_tn(pb, dpre[:TM])
            dpooled = _dot_nt(dpre, wg)
            z = dpooled / _pool_counts(i * TM, TM + POOL_HALO, w)
            dxin_ref[:, sl] = (_leading_sums(z, w)[:TM] - dpooled[:TM]).astype(BF16)

    row = pl.BlockSpec((TM, POOL_WIDTH), lambda i: (i, 0))
    return pl.pallas_call(
        body, name="pool_bwd", grid=(S // TM,),
        in_specs=[row, pl.BlockSpec((POOL_HALO, POOL_WIDTH), lambda i: (jnp.maximum(i * HB - 1, 0), 0)),
                  row, pl.BlockSpec((POOL_HALO, POOL_WIDTH), lambda i: (jnp.minimum((i + 1) * HB, last), 0)),
                  _const_spec(pool_w.shape), _const_spec((1, POOL_WIDTH))],
        out_specs=[row, pl.BlockSpec((G, POOL_GROUP, POOL_GROUP), lambda i: (0, 0, 0)),
                   pl.BlockSpec((1, POOL_WIDTH), lambda i: (0, 0))],
        out_shape=[jax.ShapeDtypeStruct((S, POOL_WIDTH), BF16), jax.ShapeDtypeStruct((G, POOL_GROUP, POOL_GROUP), F32),
                   jax.ShapeDtypeStruct((1, POOL_WIDTH), F32)],
        compiler_params=_params(("arbitrary",)),
    )(pool_in, pool_in, d_pool, d_pool, pool_w, pool_scale)


def _attn_bwd(q, k, v, d_attn, lse, delta, d, cargo=()):
    L = q.shape[0]
    nb = L // ATTN_BLOCK
    group = min(d, RESIDUES_PER_STEP)
    width = group * ATTN_WIDTH
    nc = len(cargo)
    kinds = [kind for kind, _ in cargo]

    def body(*refs):
        q_ref, kp_ref, kc_ref, vp_ref, vc_ref, do_ref, lse_ref, dl_ref = refs[:8]
        dq_ref, dk_ref, dv_ref = refs[8 + nc:11 + nc]
        ck_s, cv_s = refs[11 + 2 * nc:13 + 2 * nc]
        cargo_refs = (kinds, refs[8:8 + nc], refs[11 + nc:11 + 2 * nc], refs[13 + 2 * nc:])
        r, n = pl.program_id(0), pl.program_id(1)
        if nc:
            _cargo_start(*cargo_refs, (r == 0) & (n == 0))

        @pl.when(n == 0)
        def _():
            ck_s[...] = jnp.zeros_like(ck_s)
            cv_s[...] = jnp.zeros_like(cv_s)

        @pl.when(n < nb)
        def _():
            valid = _band_mask(n)
            valid2 = jnp.concatenate([valid, valid], axis=0)
            first = _first_head_lanes()

            def stacked_column(ref, hp):
                lane = 2 * hp * STAT_LANES
                return jnp.concatenate([ref[:, lane:lane + 1], ref[:, lane + STAT_LANES:lane + STAT_LANES + 1]], axis=0)

            for hp in range(width // LANES):
                sl = slice(hp * LANES, (hp + 1) * LANES)
                qq = _stack_heads(q_ref[:, sl], first)
                dd = _stack_heads(do_ref[:, sl], first)
                kk = jnp.concatenate([kp_ref[:, sl], kc_ref[:, sl]], axis=0)
                vv = jnp.concatenate([vp_ref[:, sl], vc_ref[:, sl]], axis=0)
                s = _dot_nt(qq, kk)
                p = jnp.where(valid2, jnp.exp(s - stacked_column(lse_ref, hp)), 0.0)
                dp = _dot_nt(dd, vv)
                ds = (p * (dp - stacked_column(dl_ref, hp))).astype(BF16)
                dq_ref[:, sl] = (_unstack_heads(_dot(ds, kk), first) * ATTN_SCALE).astype(BF16)
                dk = _dot_tn(ds, qq)
                dv = _dot_tn(p.astype(BF16), dd)
                dk_ref[:, sl] = (ck_s[:, sl] + dk[:ATTN_BLOCK]).astype(BF16)
                dv_ref[:, sl] = (cv_s[:, sl] + dv[:ATTN_BLOCK]).astype(BF16)
                ck_s[:, sl] = dk[ATTN_BLOCK:]
                cv_s[:, sl] = dv[ATTN_BLOCK:]

        @pl.when(n == nb)
        def _():
            dk_ref[...] = ck_s[...].astype(BF16)
            dv_ref[...] = cv_s[...].astype(BF16)

        if nc:
            _cargo_finish(*cargo_refs, (r == d // group - 1) & (n == nb))

    blk = (ATTN_BLOCK, width)
    cur = pl.BlockSpec(blk, lambda r, n: (jnp.minimum(n, nb - 1), r))
    stat = pl.BlockSpec((ATTN_BLOCK, group * STAT_WIDTH), lambda r, n: (jnp.minimum(n, nb - 1), r))
    prev = pl.BlockSpec(blk, lambda r, n: (jnp.maximum(jnp.minimum(n, nb - 1) - 1, 0), r))
    done = pl.BlockSpec(blk, lambda r, n: (jnp.maximum(n - 1, 0), r))
    arrays, specs, shapes, aliases, sems = _cargo_call(cargo, 8, 3)
    out = pl.pallas_call(
        body, name=f"attn_bwd_d{d}", grid=(d // group, nb + 1),
        in_specs=[cur, prev, cur, prev, cur, cur, stat, stat] + specs, out_specs=[cur, done, done] + specs,
        out_shape=[jax.ShapeDtypeStruct((L, d * ATTN_WIDTH), BF16)] * 3 + shapes,
        input_output_aliases=aliases,
        scratch_shapes=[pltpu.VMEM(blk, F32), pltpu.VMEM(blk, F32)] + sems,
        compiler_params=_params(("arbitrary", "arbitrary")),
    )(q, k, k, v, v, d_attn, lse, delta, *arrays)
    return out[:3], out[3:]


def _attn_bwd_consecutive(q, k, v, d_attn, lse, delta, d, cargo=()):
    L = q.shape[0]
    qb = CONSECUTIVE_BLOCKS
    steps = L // (qb * ATTN_BLOCK)
    nc = len(cargo)
    kinds = [kind for kind, _ in cargo]

    def body(*refs):
        q_ref, kp_ref, kc_ref, vp_ref, vc_ref, do_ref, lse_ref, dl_ref = refs[:8]
        dq_ref, dk_ref, dv_ref, ek_ref, ev_ref = refs[8 + nc:13 + nc]
        cargo_refs = (kinds, refs[8:8 + nc], refs[13 + nc:13 + 2 * nc], refs[13 + 2 * nc:])
        r, n = pl.program_id(0), pl.program_id(1)
        if nc:
            _cargo_start(*cargo_refs, (r == 0) & (n == 0))
        first = _first_head_lanes()
        for hp in range(ATTN_WIDTH // LANES):
            sl = slice(hp * LANES, (hp + 1) * LANES)
            for sub in range(qb):
                rows = slice(sub * ATTN_BLOCK, (sub + 1) * ATTN_BLOCK)
                valid = _band_mask(n if sub == 0 else 1)
                valid2 = jnp.concatenate([valid, valid], axis=0)
                if sub == 0:
                    kk = jnp.concatenate([kp_ref[:, sl], kc_ref[rows, sl]], axis=0)
                    vv = jnp.concatenate([vp_ref[:, sl], vc_ref[rows, sl]], axis=0)
                else:
                    keys = slice((sub - 1) * ATTN_BLOCK, (sub + 1) * ATTN_BLOCK)
                    kk, vv = kc_ref[keys, sl], vc_ref[keys, sl]
                qq = _stack_heads(q_ref[rows, sl], first)
                dd = _stack_heads(do_ref[rows, sl], first)
                lane = 2 * hp * STAT_LANES
                column = lambda ref: jnp.concatenate(
                    [ref[rows, lane:lane + 1], ref[rows, lane + STAT_LANES:lane + STAT_LANES + 1]], axis=0)
                p = jnp.where(valid2, jnp.exp(_dot_nt(qq, kk) - column(lse_ref)), 0.0)
                ds = (p * (_dot_nt(dd, vv) - column(dl_ref))).astype(BF16)
                dq_ref[rows, sl] = (_unstack_heads(_dot(ds, kk), first) * ATTN_SCALE).astype(BF16)
                dk = _dot_tn(ds, qq)
                dv = _dot_tn(p.astype(BF16), dd)
                if sub == 0:
                    ek_ref[:, sl] = dk[:ATTN_BLOCK].astype(BF16)
                    ev_ref[:, sl] = dv[:ATTN_BLOCK].astype(BF16)
                else:
                    before = slice((sub - 1) * ATTN_BLOCK, sub * ATTN_BLOCK)
                    dk_ref[before, sl] = (carry_k + dk[:ATTN_BLOCK]).astype(BF16)
                    dv_ref[before, sl] = (carry_v + dv[:ATTN_BLOCK]).astype(BF16)
                carry_k, carry_v = dk[ATTN_BLOCK:], dv[ATTN_BLOCK:]
            dk_ref[rows, sl] = carry_k.astype(BF16)
            dv_ref[rows, sl] = carry_v.astype(BF16)
        if nc:
            _cargo_finish(*cargo_refs, (r == d - 1) & (n == steps - 1))

    cur = pl.BlockSpec((qb * ATTN_BLOCK, ATTN_WIDTH), lambda r, n: (n, r))
    prev = pl.BlockSpec((ATTN_BLOCK, ATTN_WIDTH), lambda r, n: (jnp.maximum(n * qb - 1, 0), r))
    edge = pl.BlockSpec((ATTN_BLOCK, ATTN_WIDTH), lambda r, n: (n, r))
    stat = pl.BlockSpec((qb * ATTN_BLOCK, STAT_WIDTH), lambda r, n: (n, r))
    arrays, specs, shapes, aliases, sems = _cargo_call(cargo, 8, 5)
    out = pl.pallas_call(
        body, name=f"attn_bwd_d{d}", grid=(d, steps),
        in_specs=[cur, prev, cur, prev, cur, cur, stat, stat] + specs, out_specs=[cur, cur, cur, edge, edge] + specs,
        out_shape=[jax.ShapeDtypeStruct((L, d * ATTN_WIDTH), BF16)] * 3
        + [jax.ShapeDtypeStruct((steps * ATTN_BLOCK, d * ATTN_WIDTH), BF16)] * 2 + shapes,
        input_output_aliases=aliases, scratch_shapes=sems,
        compiler_params=_params(("arbitrary", "arbitrary")),
    )(q, k, k, v, v, d_attn, lse, delta, *arrays)
    return out[:3], out[3:5], out[5:]


def _mix_in_bwd(dqkv, edges, d_pool_in, w_in, x, g_pre, dx1):
    S, D = x.shape
    TM = CONSECUTIVE_BLOCKS * ATTN_BLOCK
    nd = len(DILATIONS)
    n_tiles = S // TM

    def body(*refs):
        g_refs = refs[:3 * nd]
        e_refs = (None,) + refs[3 * nd:3 * nd + 2]
        dpi_ref, w_ref, x_ref, g_ref, dx1_ref, dproj_ref, gx_ref, dg_ref = refs[3 * nd + 2:3 * nd + 10]
        t_s = refs[3 * nd + 10:]

        @pl.when(pl.program_id(0) == 0)
        def _():
            dg_ref[...] = jnp.zeros_like(dg_ref)

        dh = jnp.zeros((TM, D), F32)
        for a in range(4):
            if a < 3:
                tot = g_refs[a][...].astype(F32)
                if a > 0:
                    late = jnp.where(pl.program_id(0) < n_tiles - 1, e_refs[a][...].astype(F32), 0.0)
                    tot = jnp.concatenate([tot[:TM - ATTN_BLOCK], tot[TM - ATTN_BLOCK:] + late], axis=0)
                for i, d in enumerate(DILATIONS[1:]):
                    tot = tot + _from_residue(g_refs[3 * (i + 1) + a], t_s, d)
                db = tot.astype(BF16)
            else:
                db = dpi_ref[...]
            dproj_ref[:, a * ATTN_WIDTH:(a + 1) * ATTN_WIDTH] = db
            dh = dh + _dot_nt(db, w_ref[a])
        n, r = _rms_stats(x_ref[...])
        dx, dg = _rms_bwd(dh, n, r, g_ref[...])
        dg_ref[...] += dg
        gx_ref[...] = dx1_ref[...].astype(F32) + dx

    row = lambda w: pl.BlockSpec((TM, w), lambda i: (i, 0))
    edge = pl.BlockSpec((ATTN_BLOCK, ATTN_WIDTH), lambda i: (jnp.minimum(i + 1, n_tiles - 1), 0))
    return pl.pallas_call(
        body, name="mix_in_bwd", grid=(S // TM,),
        in_specs=[_residue_spec(TM, d) for d in DILATIONS for _ in range(3)] + [edge, edge]
        + [row(POOL_WIDTH), _const_spec(w_in.shape), row(D), _const_spec((1, D)), row(D)],
        out_specs=[row(4 * ATTN_WIDTH), row(D), pl.BlockSpec((1, D), lambda i: (0, 0))],
        out_shape=[jax.ShapeDtypeStruct((S, 4 * ATTN_WIDTH), BF16), jax.ShapeDtypeStruct((S, D), F32),
                   jax.ShapeDtypeStruct((1, D), F32)],
        scratch_shapes=_token_scratch(TM),
        compiler_params=_params(("arbitrary",), VMEM_LIMIT),
    )(*[g for gs in dqkv for g in gs], *edges, d_pool_in, w_in, x, g_pre, dx1)


SMALL_EARLY = ("pool_w", "pool_scale", "g_mix_post", "g_ffn_pre", "conv_b", "g_ffn_post", "conv_w")
SMALL_LATE = ("g_mix_pre",)


def _pack_small(grads, names):
    parts = []
    for n in names:
        g = grads[n]
        if n == "conv_w":
            g = g.reshape(CONV_WIDTH, N_SHARD, -1).transpose(1, 0, 2)
        parts.append(g.reshape(-1, LANES))
    return jnp.concatenate(parts, axis=0) if len(parts) > 1 else parts[0]


def _unpack_small(packed, names, like, shard):
    out, row = {}, 0
    for n in names:
        size = like[n].size * (N_SHARD if n == "conv_w" else 1)
        g = packed[row:row + size // LANES]
        row += size // LANES
        if n == "conv_w":
            g = lax.dynamic_slice_in_dim(g.reshape((N_SHARD,) + like[n].shape), shard, 1, axis=0)[0]
        out[n] = g.reshape(like[n].shape)
    return out


def _local_step(x, target, g_mix_pre, w_in, pool_w, pool_scale, w_out, g_mix_post, g_ffn_pre,
                w_up, conv_w, conv_b, w_down, g_ffn_post, mesh_pos=None):
    on_mesh = mesh_pos is not None
    D = x.shape[1]
    CW = w_up.shape[2]
    qkv, pool_in, h1, got = _mix_in_fwd(x, g_mix_pre, w_in, [("ici", w_up)] if on_mesh else ())
    w_up = got[0] if on_mesh else w_up
    o1, l1, got = _attn_fwd(*qkv[0], 1, [("d2d", w_up), ("ici", w_out)] if on_mesh else ())
    w_up, w_out = got if on_mesh else (w_up, w_out)
    o4, l4, got = _attn_fwd(*qkv[1], 4, [("d2d", w_out), ("ici", w_down)] if on_mesh else ())
    w_out, w_down = got if on_mesh else (w_out, w_down)
    o16, l16, got = _attn_fwd(*qkv[2], 16, [("d2d", w_down)] if on_mesh else ())
    w_down = got[0] if on_mesh else w_down
    w_out = w_out.reshape(D, D)
    w_down = w_down.reshape(2 * CW, D)
    attn, lse = _attn_mix((o1, o4, o16), (l1, l4, l16))
    pool = _pool_fwd(pool_in, pool_w, pool_scale)
    mixed, x1, h2, cat = _mix_out_fwd(attn, pool, w_out, x, g_mix_post, g_ffn_pre)

    yv, dy, df, dc, loss, d_g_ffn_post, d_conv_b, d_conv_w = _ffn_fwd(
        h2, x1, target, w_up, w_down, conv_w, conv_b, g_ffn_post)
    du, dx1, d_g_ffn_pre = _ffn_bwd(dc, conv_w, w_up, x1, g_ffn_pre, dy)
    d_w_up = _matmul_tn(h2, du, N_SHARD, "grad_w_up")
    d_w_down = _matmul_tn(yv, df, 1, "grad_w_down")[0].reshape(N_SHARD, CW // 2, D)
    swap = [("swap", d_w_up), ("swap", d_w_down)] if on_mesh else ()
    d_mixed, d_pool, d_g_mix_post, d_attn, delta, from_sibling = _mix_out_bwd(dx1, mixed, g_mix_post, w_out, attn, swap)
    d_w_out = _matmul_tn(cat, d_mixed, 1, "grad_w_out")[0].reshape(N_SHARD, D // N_SHARD, D)
    d_pool_in, d_pool_w, d_pool_scale = _pool_bwd(pool_in, d_pool, pool_w, pool_scale)
    grads = dict(pool_w=d_pool_w, pool_scale=d_pool_scale, w_out=d_w_out, g_mix_post=d_g_mix_post,
                 g_ffn_pre=d_g_ffn_pre, w_up=d_w_up, conv_w=d_conv_w, conv_b=d_conv_b, w_down=d_w_down,
                 g_ffn_post=d_g_ffn_post)
    cargo = [(), (), ()]
    if on_mesh:
        c_arr, device, shard_arr = mesh_pos
        up_f32, up_bf16 = _pair_sum(d_w_up, from_sibling[0], c_arr, "pair_sum_w_up")
        down_f32, down_bf16 = _pair_sum(d_w_down, from_sibling[1], c_arr, "pair_sum_w_down")
        early = _pack_small(grads, SMALL_EARLY)
        early_slots = lax.dynamic_update_index_in_dim(jnp.zeros((8,) + early.shape, F32), early, device, 0)
        cargo = [[("scatter", down_bf16), ("everyone", early_slots), ("swap", d_w_out)], [("scatter", up_bf16)], []]

    dqkv1, edges, landed1 = _attn_bwd_consecutive(*qkv[0], d_attn[0], lse[0], delta[0], 1, cargo[0])
    dqkv4, landed4 = _attn_bwd(*qkv[1], d_attn[1], lse[1], delta[1], 4, cargo[1])
    if on_mesh:
        halves = [_shard_sum(up_f32, landed4[0], shard_arr, c_arr, "shard_sum_w_up"),
                  _shard_sum(down_f32, landed1[0], shard_arr, c_arr, "shard_sum_w_down")]
        out_f32, out_bf16 = _pair_sum(d_w_out, landed1[2], c_arr, "pair_sum_w_out")
        cargo[2] = cargo[2] + [("join", h) for h in halves] + [("scatter", out_bf16)]
    if qkv[2][0].shape[0] == CONSECUTIVE_BLOCKS * ATTN_BLOCK:
        dqkv16, _, landed16 = _attn_bwd_consecutive(*qkv[2], d_attn[2], lse[2], delta[2], 16, cargo[2])
    else:
        dqkv16, landed16 = _attn_bwd(*qkv[2], d_attn[2], lse[2], delta[2], 16, cargo[2])
    if on_mesh:
        grads.update(small_early=landed1[1], w_up=landed16[0], w_down=landed16[1],
                     w_out=_shard_sum(out_f32, landed16[2], shard_arr, c_arr, "shard_sum_w_out"))
    d_proj, grad_x, grads["g_mix_pre"] = _mix_in_bwd((dqkv1, dqkv4, dqkv16), edges, d_pool_in, w_in, x, g_mix_pre, dx1)
    grads["w_in"] = _matmul_tn(h1, d_proj, N_SHARD, "grad_w_in")
    return loss, grad_x, grads


ANY = pl.BlockSpec(memory_space=pl.ANY)


def _position():
    x, y, c = lax.axis_index("x"), lax.axis_index("y"), lax.axis_index("c")
    chips = [(1 - x, y), (x, 1 - y), (1 - x, 1 - y)]
    return x, y, c, chips


def _remote(src, dst, send_sem, recv_sem, to):
    return pltpu.make_async_remote_copy(src_ref=src, dst_ref=dst, send_sem=send_sem, recv_sem=recv_sem,
                                        device_id=to, device_id_type=MESH)


def _cast_bf16(w, shard_arr, name):
    R, C = w.shape
    tr = R // 2

    def body(s_ref, w_ref, o_ref):
        o_ref[0] = w_ref[...].astype(BF16)

    return pl.pallas_call(
        body, name=name,
        grid_spec=pltpu.PrefetchScalarGridSpec(
            num_scalar_prefetch=1, grid=(2,),
            in_specs=[pl.BlockSpec((tr, C), lambda i, s_ref: (i, 0))],
            out_specs=pl.BlockSpec((1, tr, C), lambda i, s_ref: (s_ref[0], i, 0))),
        out_shape=jax.ShapeDtypeStruct((N_SHARD, R, C), BF16),
        compiler_params=_params(("parallel",)))(shard_arr, w)


def _gather_weights(bufs, others):
    n = len(bufs) - 1
    m = len(others)

    def body(*refs):
        w_refs = refs[n + 1:n + 1 + m]
        outs, cw_out = refs[n + 1 + m:2 * n + 1 + m], refs[2 * n + 1 + m]
        cast_outs = refs[2 * n + 2 + m:2 * n + 2 + 2 * m]
        ici_send, ici_recv, d2d_send, d2d_recv, local_sem = refs[2 * n + 2 + 2 * m:2 * n + 7 + 2 * m]
        f32_s, bf16_s = refs[2 * n + 7 + 2 * m:2 * n + 7 + 3 * m], refs[2 * n + 7 + 3 * m:]
        x, y, c, chips = _position()
        s = 2 * x + y
        sibling = (x, y, 1 - c)

        def half(a, shard, h):
            rows = outs[a].shape[1] // 2
            return outs[a].at[shard, pl.ds(h * rows, rows), :]

        sends = []
        for a in range(n):
            for j, (px, py) in enumerate(chips):
                sends.append(_remote(half(a, s, c), half(a, s, c),
                                     ici_send.at[3 * a + j], ici_recv.at[3 * a + j], (px, py, c)))
        for j, (px, py) in enumerate(chips):
            sends.append(_remote(cw_out.at[s], cw_out.at[s], ici_send.at[3 * n + j], ici_recv.at[3 * n + j], (px, py, c)))
        for cp in sends:
            cp.start()
        loads = [pltpu.make_async_copy(w_refs[a], f32_s[a], local_sem.at[2 * a]) for a in range(m)]
        stores = [pltpu.make_async_copy(bf16_s[a], cast_outs[a].at[s], local_sem.at[2 * a + 1]) for a in range(m)]
        for cp in loads:
            cp.start()
        for a in range(m):
            loads[a].wait()
            bf16_s[a][...] = f32_s[a][...].astype(BF16)
            stores[a].start()
        passed = []
        for a in range(n):
            for j, (px, py) in enumerate(chips):
                sj = 2 * px + py
                got = half(a, sj, c)
                _remote(got, got, ici_send.at[3 * a + j], ici_recv.at[3 * a + j], (px, py, c)).wait_recv()
                fwd = _remote(got, got, d2d_send.at[3 * a + j], d2d_recv.at[3 * a + j], sibling)
                fwd.start()
                passed.append(fwd)
        for j, (px, py) in enumerate(chips):
            got = cw_out.at[2 * px + py]
            _remote(got, got, ici_send.at[3 * n + j], ici_recv.at[3 * n + j], (px, py, c)).wait_recv()
        for a in range(n):
            for j, (px, py) in enumerate(chips):
                got = half(a, 2 * px + py, 1 - c)
                _remote(got, got, d2d_send.at[3 * a + j], d2d_recv.at[3 * a + j], sibling).wait_recv()
        for cp in sends + passed:
            cp.wait_send()
        for cp in stores:
            cp.wait()

    out = pl.pallas_call(
        body, name="gather_weights",
        in_specs=[ANY] * (n + 1 + m), out_specs=[ANY] * (n + 1 + m),
        out_shape=[jax.ShapeDtypeStruct(b.shape, b.dtype) for b in bufs]
        + [jax.ShapeDtypeStruct((N_SHARD,) + w.shape, BF16) for w in others],
        input_output_aliases={i: i for i in range(n + 1)},
        scratch_shapes=[pltpu.SemaphoreType.DMA((3 * n + 3,)), pltpu.SemaphoreType.DMA((3 * n + 3,)),
                        pltpu.SemaphoreType.DMA((3 * n,)), pltpu.SemaphoreType.DMA((3 * n,)),
                        pltpu.SemaphoreType.DMA((2 * m,))]
        + [pltpu.VMEM(w.shape, F32) for w in others] + [pltpu.VMEM(w.shape, BF16) for w in others],
        compiler_params=pltpu.CompilerParams(has_side_effects=True, vmem_limit_bytes=VMEM_LIMIT),
    )(*bufs, *others)
    return out[:n + 1], out[n + 1:]


def _swap_halves(grads, tag):
    n = len(grads)

    def body(*refs):
        ins, outs, send_sem, recv_sem = refs[:n], refs[n:2 * n], refs[2 * n], refs[2 * n + 1]
        x, y, c, _ = _position()
        copies = []
        for a in range(n):
            rows = ins[a].shape[1] // 2
            copies.append(_remote(ins[a].at[:, pl.ds((1 - c) * rows, rows), :], outs[a],
                                  send_sem.at[a], recv_sem.at[a], (x, y, 1 - c)))
        for cp in copies:
            cp.start()
        for cp in copies:
            cp.wait()

    return pl.pallas_call(
        body, name="swap_grad_halves_" + tag,
        in_specs=[ANY] * n, out_specs=[ANY] * n,
        out_shape=[jax.ShapeDtypeStruct((g.shape[0], g.shape[1] // 2, g.shape[2]), F32) for g in grads],
        scratch_shapes=[pltpu.SemaphoreType.DMA((n,)), pltpu.SemaphoreType.DMA((n,))],
        compiler_params=pltpu.CompilerParams(has_side_effects=True),
    )(*grads)


def _pair_sum(g, got, c_arr, name):
    n_sh, R, C = g.shape
    rows = R // 2

    def body(c_ref, g_ref, r_ref, f_ref, b_ref):
        t = g_ref[...] + r_ref[...]
        f_ref[...] = t
        b_ref[...] = t.astype(BF16)

    blk = pl.BlockSpec((1, rows, C), lambda i, c_ref: (i, 0, 0))
    return pl.pallas_call(
        body, name=name,
        grid_spec=pltpu.PrefetchScalarGridSpec(
            num_scalar_prefetch=1, grid=(n_sh,),
            in_specs=[pl.BlockSpec((1, rows, C), lambda i, c_ref: (i, c_ref[0], 0)), blk],
            out_specs=[blk, blk]),
        out_shape=[jax.ShapeDtypeStruct((n_sh, rows, C), F32), jax.ShapeDtypeStruct((n_sh, rows, C), BF16)],
        compiler_params=_params(("parallel",)),
    )(c_arr, g, got)


def _shard_sum(sums_f32, recv, shard_arr, c_arr, name):
    _, rows, C = sums_f32.shape

    def body(s_ref, c_ref, o_ref, r_ref, t_ref):
        t_ref[...] = ((o_ref[0] + r_ref[0].astype(F32)) + r_ref[1].astype(F32)) + r_ref[2].astype(F32)

    return pl.pallas_call(
        body, name=name,
        grid_spec=pltpu.PrefetchScalarGridSpec(
            num_scalar_prefetch=2, grid=(1,),
            in_specs=[pl.BlockSpec((1, rows, C), lambda i, s_ref, c_ref: (s_ref[0], 0, 0)),
                      pl.BlockSpec((3, rows, C), lambda i, s_ref, c_ref: (0, 0, 0))],
            out_specs=pl.BlockSpec((rows, C), lambda i, s_ref, c_ref: (c_ref[0], 0))),
        out_shape=jax.ShapeDtypeStruct((2 * rows, C), F32),
        compiler_params=_params(("arbitrary",)),
    )(shard_arr, c_arr, sums_f32, recv)


def _join_halves(bufs, parts):
    n = len(bufs)
    m = len(parts)

    def body(*refs):
        p_refs, outs, sums = refs[n:n + m], refs[n + m:2 * n + m], refs[2 * n + m:2 * n + 2 * m]
        send_sem, recv_sem, local_sem = refs[2 * n + 2 * m:2 * n + 2 * m + 3]
        p_s, t_s = refs[2 * n + 2 * m + 3:2 * n + 3 * m + 3], refs[2 * n + 3 * m + 3:]
        x, y, c, _ = _position()
        copies = []
        for a in range(n):
            rows = outs[a].shape[0] // 2
            mine = outs[a].at[pl.ds(c * rows, rows), :]
            copies.append(_remote(mine, mine, send_sem.at[a], recv_sem.at[a], (x, y, 1 - c)))
        for cp in copies:
            cp.start()
        loads = [pltpu.make_async_copy(p_refs[a], p_s[a], local_sem.at[2 * a]) for a in range(m)]
        stores = [pltpu.make_async_copy(t_s[a], sums[a], local_sem.at[2 * a + 1]) for a in range(m)]
        for cp in loads:
            cp.start()
        for a in range(m):
            loads[a].wait()
            t = p_s[a][0]
            for k in range(1, 8):
                t = t + p_s[a][k]
            t_s[a][...] = t
            stores[a].start()
        for a, cp in enumerate(copies):
            cp.wait_send()
            rows = outs[a].shape[0] // 2
            theirs = outs[a].at[pl.ds((1 - c) * rows, rows), :]
            _remote(theirs, theirs, send_sem.at[a], recv_sem.at[a], (x, y, 1 - c)).wait_recv()
        for cp in stores:
            cp.wait()

    out = pl.pallas_call(
        body, name="join_grad_halves",
        in_specs=[ANY] * (n + m), out_specs=[ANY] * (n + m),
        out_shape=[jax.ShapeDtypeStruct(b.shape, F32) for b in bufs]
        + [jax.ShapeDtypeStruct(p.shape[1:], F32) for p in parts],
        input_output_aliases={i: i for i in range(n)},
        scratch_shapes=[pltpu.SemaphoreType.DMA((n,)), pltpu.SemaphoreType.DMA((n,)), pltpu.SemaphoreType.DMA((2 * m,))]
        + [pltpu.VMEM(p.shape, F32) for p in parts] + [pltpu.VMEM(p.shape[1:], F32) for p in parts],
        compiler_params=pltpu.CompilerParams(has_side_effects=True),
    )(*bufs, *parts)
    return out[:n], out[n:]


def _adamw_math(w, g, m, v):
    m = ADAM_B1 * m + (1.0 - ADAM_B1) * g
    v = ADAM_B2 * v + (1.0 - ADAM_B2) * (g * g)
    m_hat = m / (1.0 - ADAM_B1 ** ADAM_STEP)
    v_hat = v / (1.0 - ADAM_B2 ** ADAM_STEP)
    delta = -ADAM_LR * (m_hat / (jnp.sqrt(v_hat) + ADAM_EPS) + ADAM_WD * w)
    return delta, m, v


def _adamw_big(ws, gs, ms, vs, name, cargo=()):
    n = len(ws)
    nc = len(cargo)
    kinds = [kind for kind, _ in cargo]

    def body(*refs):
        ins, outs = refs[:4 * n], refs[4 * n + nc:7 * n + nc]
        cargo_refs = (kinds, refs[4 * n:4 * n + nc], refs[7 * n + nc:7 * n + 2 * nc], refs[7 * n + 2 * nc:])
        if nc:
            _cargo_start(*cargo_refs, pl.program_id(0) == 0)
        for a in range(n):
            w, g, m, v = (ins[k * n + a][...] for k in range(4))
            outs[a][...], outs[n + a][...], outs[2 * n + a][...] = _adamw_math(w, g, m, v)
        if nc:
            _cargo_finish(*cargo_refs, pl.program_id(0) == 3)

    blks = [pl.BlockSpec((w.shape[0] // 4, w.shape[1]), lambda i: (i, 0)) for w in ws]
    arrays, cargo_specs, shapes, aliases, sems = _cargo_call(cargo, 4 * n, 3 * n)
    out = pl.pallas_call(
        body, name=name, grid=(4,), in_specs=blks * 4 + cargo_specs, out_specs=blks * 3 + cargo_specs,
        out_shape=[jax.ShapeDtypeStruct(w.shape, F32) for w in ws] * 3 + shapes,
        input_output_aliases=aliases, scratch_shapes=sems,
        compiler_params=_params(("arbitrary",)),
    )(*ws, *gs, *ms, *vs, *arrays)
    return (out[:n], out[n:2 * n], out[2 * n:3 * n]), out[3 * n:]


def _adamw_small(ws, gs, ms, vs):
    n = len(ws)

    def body(*refs):
        for a in range(n):
            w, g, m, v = (refs[k * n + a][...] for k in range(4))
            d, nm, nv = _adamw_math(w, g, m, v)
            refs[4 * n + a][...] = d
            refs[5 * n + a][...] = nm
            refs[6 * n + a][...] = nv

    shapes = [jax.ShapeDtypeStruct(w.shape, F32) for w in ws]
    out = pl.pallas_call(body, name="adamw_small", out_shape=shapes * 3)(*ws, *gs, *ms, *vs)
    return out[:n], out[n:2 * n], out[2 * n:]


SMALL = ("g_mix_pre", "pool_w", "pool_scale", "g_mix_post", "g_ffn_pre", "conv_b", "g_ffn_post", "conv_w")
ORDER = ("g_mix_pre", "w_in", "pool_w", "pool_scale", "w_out", "g_mix_post", "g_ffn_pre", "w_up", "conv_w", "conv_b",
         "w_down", "g_ffn_post")


def kernel(x, g_mix_pre, w_in, pool_w, pool_scale, w_out, g_mix_post, g_ffn_pre, w_up, conv_w, conv_b, w_down, g_ffn_post, loss_target, m_g_mix_pre, m_w_in, m_pool_w, m_pool_scale, m_w_out, m_g_mix_post, m_g_ffn_pre, m_w_up, m_conv_w, m_conv_b, m_w_down, m_g_ffn_post, v_g_mix_pre, v_w_in, v_pool_w, v_pool_scale, v_w_out, v_g_mix_post, v_g_ffn_pre, v_w_up, v_conv_w, v_conv_b, v_w_down, v_g_ffn_post):
    args = dict(locals())
    W = {n: args[n][0] for n in ORDER}
    M = {n: args["m_" + n][0] for n in ORDER}
    V = {n: args["v_" + n][0] for n in ORDER}
    for d in (W, M, V):
        d["pool_w"] = d["pool_w"].reshape(-1, POOL_GROUP)
        for n in ("g_mix_pre", "pool_scale", "g_mix_post", "g_ffn_pre", "conv_b", "g_ffn_post"):
            d[n] = d[n].reshape(1, -1)
    CW = W["w_up"].shape[1]
    c_arr = lax.axis_index("c").astype(jnp.int32).reshape(1)
    shard = 2 * lax.axis_index("x") + lax.axis_index("y")
    shard_arr = shard.astype(jnp.int32).reshape(1)
    device = 2 * shard + lax.axis_index("c")

    conv_w_slots = lax.dynamic_update_index_in_dim(jnp.zeros((N_SHARD,) + W["conv_w"].shape, F32), W["conv_w"], shard, 0)
    later = ("w_out", "w_up", "w_down")
    (w_in_g, conv_w_g), cast = _gather_weights([_cast_bf16(W["w_in"], shard_arr, "cast_w_in"), conv_w_slots],
                                               [W[n] for n in later])
    slots = dict(zip(later, cast))
    conv_w_full = conv_w_g.transpose(1, 0, 2).reshape(CONV_WIDTH, 1, N_SHARD * CW)

    loss, grad_x, G = _local_step(
        x[0], loss_target[0], W["g_mix_pre"], w_in_g, W["pool_w"].reshape(-1, POOL_GROUP, POOL_GROUP), W["pool_scale"],
        slots["w_out"], W["g_mix_post"], W["g_ffn_pre"], slots["w_up"], conv_w_full, W["conv_b"],
        slots["w_down"], W["g_ffn_post"], (c_arr, device, shard_arr))

    late, ffn = ("w_in", "w_out"), ("w_up", "w_down")
    in_f32, in_bf16 = _pair_sum(G["w_in"], _swap_halves([G["w_in"]], "mix")[0], c_arr, "pair_sum_w_in")
    loss_rows = jnp.pad(loss, ((0, 7), (0, LANES - 1)))
    small = jnp.concatenate([_pack_small(G, SMALL_LATE), loss_rows], axis=0)
    small_slots = lax.dynamic_update_index_in_dim(jnp.zeros((8,) + small.shape, F32), small, device, 0)
    pick = lambda d, names: [d[n] for n in names]
    delta, new_m, new_v = {}, {}, {}
    updates, landed = _adamw_big(pick(W, ffn), pick(G, ffn), pick(M, ffn), pick(V, ffn), "adamw_ffn",
                                 [("scatter", in_bf16), ("everyone", small_slots)])
    halves = [_shard_sum(in_f32, landed[0], shard_arr, c_arr, "shard_sum_w_in"), G["w_out"]]
    joined, (early_total, late_total) = _join_halves(halves, [G["small_early"], landed[1]])
    full = dict(zip(late, joined))
    full.update({n: G[n] for n in ffn})
    full.update(_unpack_small(early_total, SMALL_EARLY, W, shard))
    full.update(_unpack_small(late_total, SMALL_LATE, W, shard))
    loss = late_total[-8, 0]

    lift = lambda d: [d[n][:, None, :] if n == "conv_w" else d[n] for n in SMALL]
    for names, (ds, nms, nvs) in ((ffn, updates),
                                  (late, _adamw_big(pick(W, late), pick(full, late), pick(M, late), pick(V, late), "adamw_mix")[0]),
                                  (SMALL, _adamw_small(lift(W), lift(full), lift(M), lift(V)))):
        for n, d, nm, nv in zip(names, ds, nms, nvs):
            delta[n], new_m[n], new_v[n] = d, nm, nv

    shaped = lambda d: [d[n].reshape(args[n].shape) for n in ORDER]
    return (loss, grad_x[None], *shaped(full), *shaped(delta), *shaped(new_m), *shaped(new_v))
```
